```python
import jax, jax.numpy as jnp
from jax import lax
import numpy as np

D_MODEL = 1024
BATCH = 8
SEQ = 8192
DEPTH = 1

CONV_WIDTH = 512
CONV_GROUPS = 8
CONV_KERNEL = 3
N_Q_HEADS = 8
N_KV_HEADS = 2
HEAD_DIM = 64
ATTN_WIDTH = N_Q_HEADS * HEAD_DIM
KV_WIDTH = N_KV_HEADS * HEAD_DIM
WINDOW = 128
BLOCK = 128
ROPE_THETA = 500000.0
ROT_DIM = HEAD_DIM // 4
MIX_WIDTH = CONV_WIDTH + ATTN_WIDTH
IN_PROJ_WIDTH = 3 * CONV_WIDTH + ATTN_WIDTH + 2 * KV_WIDTH
D_FF = 2816
FFN_RES_SCALE = 0.5
RMS_EPS = 1e-5
MASK_VALUE = -1e30

kernel_name = "hybrid_shortconv_swa_sink_macaron"


def rms_norm(x, gain):
    xf = x.astype(jnp.float32)
    inv = lax.rsqrt(jnp.mean(xf * xf, axis=-1, keepdims=True) + RMS_EPS)
    return (xf * inv).astype(x.dtype) * gain


def swiglu(h, w_gate, w_up, w_down):
    return (jax.nn.silu(h @ w_gate) * (h @ w_up)) @ w_down


def partial_rotary(t, seq_len):
    half = ROT_DIM // 2
    inv_freq = ROPE_THETA ** (-jnp.arange(0, ROT_DIM, 2, dtype=jnp.float32) / ROT_DIM)
    ang = jnp.arange(seq_len, dtype=jnp.float32)[:, None] * inv_freq[None, :]
    cos = jnp.cos(ang)[None, :, None, :].astype(t.dtype)
    sin = jnp.sin(ang)[None, :, None, :].astype(t.dtype)
    t1, t2, t_pass = t[..., :half], t[..., half:ROT_DIM], t[..., ROT_DIM:]
    return jnp.concatenate([t1 * cos - t2 * sin, t2 * cos + t1 * sin, t_pass], axis=-1)


def short_conv_mixer(b_gate, c_gate, u, conv_w):
    v = c_gate * u
    y = lax.conv_general_dilated(
        v, conv_w[:, None, :], window_strides=(1,), padding=[(CONV_KERNEL - 1, 0)],
        dimension_numbers=('NWC', 'WIO', 'NWC'), feature_group_count=CONV_WIDTH)
    return b_gate * y


def sliding_window_sink_attention(q, k, v, sinks):
    b, s = q.shape[0], q.shape[1]
    nb = s // BLOCK
    g = N_Q_HEADS // N_KV_HEADS
    qb = q.reshape(b, nb, BLOCK, N_KV_HEADS, g, HEAD_DIM)

    def band(t):
        tp = jnp.pad(t, ((0, 0), (BLOCK, 0), (0, 0), (0, 0))).reshape(b, nb + 1, BLOCK, N_KV_HEADS, HEAD_DIM)
        return jnp.concatenate([tp[:, :-1], tp[:, 1:]], axis=2)

    kb, vb = band(k), band(v)
    scores = jnp.einsum('bnqhgd,bnkhd->bnhgqk', qb, kb).astype(jnp.float32) * (HEAD_DIM ** -0.5)

    qi = jnp.arange(BLOCK)[:, None]
    kj = jnp.arange(2 * BLOCK)[None, :]
    rel = kj - BLOCK - qi
    in_window = (rel <= 0) & (rel > -WINDOW)
    blk = jnp.arange(nb)[:, None, None]
    k_exists = (blk * BLOCK + kj[None] - BLOCK) >= 0
    mask = (in_window[None] & k_exists)[None, :, None, None]
    scores = jnp.where(mask, scores, MASK_VALUE)

    sink = sinks.astype(jnp.float32).reshape(N_KV_HEADS, g)[None, None, :, :, None, None]
    m = jnp.maximum(jnp.max(scores, axis=-1, keepdims=True), sink)
    p = jnp.exp(scores - m)
    probs = p / (jnp.sum(p, axis=-1, keepdims=True) + jnp.exp(sink - m))
    out = jnp.einsum('bnhgqk,bnkhd->bnqhgd', probs.astype(vb.dtype), vb)
    return out.reshape(b, s, ATTN_WIDTH)


def _fwd_setup_inputs(seed: int = 0) -> dict:
    key = jax.random.key(seed)
    ks = jax.random.split(key, 16)
    f32 = jnp.float32

    def w(k, shape, fan_in):
        return jax.random.normal(k, shape, f32) * (fan_in ** -0.5)

    def gain(k):
        return 1.0 + 0.02 * jax.random.normal(k, (DEPTH, D_MODEL), f32)

    return {
        "x": jax.random.normal(ks[0], (BATCH, SEQ, D_MODEL), f32),
        "ffn1_norm": gain(ks[1]),
        "ffn1_w_gate": w(ks[2], (DEPTH, D_MODEL, D_FF), D_MODEL),
        "ffn1_w_up": w(ks[3], (DEPTH, D_MODEL, D_FF), D_MODEL),
        "ffn1_w_down": w(ks[4], (DEPTH, D_FF, D_MODEL), D_FF),
        "mix_norm": gain(ks[5]),
        "w_in": w(ks[6], (DEPTH, D_MODEL, IN_PROJ_WIDTH), D_MODEL),
        "conv_w": w(ks[7], (DEPTH, CONV_KERNEL, CONV_WIDTH), CONV_KERNEL),
        "attn_sinks": 0.5 * jax.random.normal(ks[8], (DEPTH, N_Q_HEADS), f32),
        "w_out": w(ks[9], (DEPTH, MIX_WIDTH, D_MODEL), MIX_WIDTH),
        "ffn2_norm": gain(ks[10]),
        "ffn2_w_gate": w(ks[11], (DEPTH, D_MODEL, D_FF), D_MODEL),
        "ffn2_w_up": w(ks[12], (DEPTH, D_MODEL, D_FF), D_MODEL),
        "ffn2_w_down": w(ks[13], (DEPTH, D_FF, D_MODEL), D_FF),
        "final_norm": 1.0 + 0.02 * jax.random.normal(ks[14], (D_MODEL,), f32),
    }


def _fwd_reference(x, ffn1_norm, ffn1_w_gate, ffn1_w_up, ffn1_w_down, mix_norm, w_in, conv_w,
              attn_sinks, w_out, ffn2_norm, ffn2_w_gate, ffn2_w_up, ffn2_w_down, final_norm):
    b, s, _ = x.shape
    splits = np.cumsum([CONV_WIDTH, CONV_WIDTH, CONV_WIDTH, ATTN_WIDTH, KV_WIDTH]).tolist()
    for l in range(DEPTH):
        x = x + FFN_RES_SCALE * swiglu(rms_norm(x, ffn1_norm[l]), ffn1_w_gate[l], ffn1_w_up[l], ffn1_w_down[l])

        h = rms_norm(x, mix_norm[l])
        z = h @ w_in[l]
        b_gate, c_gate, u, q, k, v = jnp.split(z, splits, axis=-1)

        y_conv = short_conv_mixer(b_gate, c_gate, u, conv_w[l])

        q = partial_rotary(q.reshape(b, s, N_Q_HEADS, HEAD_DIM), s)
        k = partial_rotary(k.reshape(b, s, N_KV_HEADS, HEAD_DIM), s)
        v = v.reshape(b, s, N_KV_HEADS, HEAD_DIM)
        y_attn = sliding_window_sink_attention(q, k, v, attn_sinks[l])

        x = x + jnp.concatenate([y_conv, y_attn], axis=-1) @ w_out[l]

        x = x + FFN_RES_SCALE * swiglu(rms_norm(x, ffn2_norm[l]), ffn2_w_gate[l], ffn2_w_up[l], ffn2_w_down[l])
    return rms_norm(x, final_norm)


import jax as _jax
import jax.numpy as _jnp

TWIN_FORMAT = 'train_step'
FWD_PARAMS = ['x', 'ffn1_norm', 'ffn1_w_gate', 'ffn1_w_up', 'ffn1_w_down', 'mix_norm', 'w_in', 'conv_w', 'attn_sinks', 'w_out', 'ffn2_norm', 'ffn2_w_gate', 'ffn2_w_up', 'ffn2_w_down', 'final_norm']
TWIN_WEIGHTS = ['ffn1_norm', 'ffn1_w_gate', 'ffn1_w_up', 'ffn1_w_down', 'mix_norm', 'w_in', 'conv_w', 'attn_sinks', 'w_out', 'ffn2_norm', 'ffn2_w_gate', 'ffn2_w_up', 'ffn2_w_down', 'final_norm']
TWIN_DIFF_INPUT = 'x'
TWIN_INPUTS = ['x', 'ffn1_norm', 'ffn1_w_gate', 'ffn1_w_up', 'ffn1_w_down', 'mix_norm', 'w_in', 'conv_w', 'attn_sinks', 'w_out', 'ffn2_norm', 'ffn2_w_gate', 'ffn2_w_up', 'ffn2_w_down', 'final_norm', 'loss_target', 'm_ffn1_norm', 'm_ffn1_w_gate', 'm_ffn1_w_up', 'm_ffn1_w_down', 'm_mix_norm', 'm_w_in', 'm_conv_w', 'm_attn_sinks', 'm_w_out', 'm_ffn2_norm', 'm_ffn2_w_gate', 'm_ffn2_w_up', 'm_ffn2_w_down', 'm_final_norm', 'v_ffn1_norm', 'v_ffn1_w_gate', 'v_ffn1_w_up', 'v_ffn1_w_down', 'v_mix_norm', 'v_w_in', 'v_conv_w', 'v_attn_sinks', 'v_w_out', 'v_ffn2_norm', 'v_ffn2_w_gate', 'v_ffn2_w_up', 'v_ffn2_w_down', 'v_final_norm']
TWIN_OUTPUTS = ['loss', 'grad_x', 'grad_ffn1_norm', 'grad_ffn1_w_gate', 'grad_ffn1_w_up', 'grad_ffn1_w_down', 'grad_mix_norm', 'grad_w_in', 'grad_conv_w', 'grad_attn_sinks', 'grad_w_out', 'grad_ffn2_norm', 'grad_ffn2_w_gate', 'grad_ffn2_w_up', 'grad_ffn2_w_down', 'grad_final_norm', 'delta_ffn1_norm', 'delta_ffn1_w_gate', 'delta_ffn1_w_up', 'delta_ffn1_w_down', 'delta_mix_norm', 'delta_w_in', 'delta_conv_w', 'delta_attn_sinks', 'delta_w_out', 'delta_ffn2_norm', 'delta_ffn2_w_gate', 'delta_ffn2_w_up', 'delta_ffn2_w_down', 'delta_final_norm', 'new_m_ffn1_norm', 'new_m_ffn1_w_gate', 'new_m_ffn1_w_up', 'new_m_ffn1_w_down', 'new_m_mix_norm', 'new_m_w_in', 'new_m_conv_w', 'new_m_attn_sinks', 'new_m_w_out', 'new_m_ffn2_norm', 'new_m_ffn2_w_gate', 'new_m_ffn2_w_up', 'new_m_ffn2_w_down', 'new_m_final_norm', 'new_v_ffn1_norm', 'new_v_ffn1_w_gate', 'new_v_ffn1_w_up', 'new_v_ffn1_w_down', 'new_v_mix_norm', 'new_v_w_in', 'new_v_conv_w', 'new_v_attn_sinks', 'new_v_w_out', 'new_v_ffn2_norm', 'new_v_ffn2_w_gate', 'new_v_ffn2_w_up', 'new_v_ffn2_w_down', 'new_v_final_norm']
TWIN_LEAF_KINDS = {'loss': 'loss', 'grad_x': 'grad_x', 'grad_ffn1_norm': 'grad_w', 'grad_ffn1_w_gate': 'grad_w', 'grad_ffn1_w_up': 'grad_w', 'grad_ffn1_w_down': 'grad_w', 'grad_mix_norm': 'grad_w', 'grad_w_in': 'grad_w', 'grad_conv_w': 'grad_w', 'grad_attn_sinks': 'grad_w', 'grad_w_out': 'grad_w', 'grad_ffn2_norm': 'grad_w', 'grad_ffn2_w_gate': 'grad_w', 'grad_ffn2_w_up': 'grad_w', 'grad_ffn2_w_down': 'grad_w', 'grad_final_norm': 'grad_w', 'delta_ffn1_norm': 'delta_w', 'delta_ffn1_w_gate': 'delta_w', 'delta_ffn1_w_up': 'delta_w', 'delta_ffn1_w_down': 'delta_w', 'delta_mix_norm': 'delta_w', 'delta_w_in': 'delta_w', 'delta_conv_w': 'delta_w', 'delta_attn_sinks': 'delta_w', 'delta_w_out': 'delta_w', 'delta_ffn2_norm': 'delta_w', 'delta_ffn2_w_gate': 'delta_w', 'delta_ffn2_w_up': 'delta_w', 'delta_ffn2_w_down': 'delta_w', 'delta_final_norm': 'delta_w', 'new_m_ffn1_norm': 'new_m', 'new_m_ffn1_w_gate': 'new_m', 'new_m_ffn1_w_up': 'new_m', 'new_m_ffn1_w_down': 'new_m', 'new_m_mix_norm': 'new_m', 'new_m_w_in': 'new_m', 'new_m_conv_w': 'new_m', 'new_m_attn_sinks': 'new_m', 'new_m_w_out': 'new_m', 'new_m_ffn2_norm': 'new_m', 'new_m_ffn2_w_gate': 'new_m', 'new_m_ffn2_w_up': 'new_m', 'new_m_ffn2_w_down': 'new_m', 'new_m_final_norm': 'new_m', 'new_v_ffn1_norm': 'new_v', 'new_v_ffn1_w_gate': 'new_v', 'new_v_ffn1_w_up': 'new_v', 'new_v_ffn1_w_down': 'new_v', 'new_v_mix_norm': 'new_v', 'new_v_w_in': 'new_v', 'new_v_conv_w': 'new_v', 'new_v_attn_sinks': 'new_v', 'new_v_w_out': 'new_v', 'new_v_ffn2_norm': 'new_v', 'new_v_ffn2_w_gate': 'new_v', 'new_v_ffn2_w_up': 'new_v', 'new_v_ffn2_w_down': 'new_v', 'new_v_final_norm': 'new_v'}


def _forward(args):
    return _fwd_reference(*[args[k] for k in FWD_PARAMS])


def _output_shape():
    def fwd():
        inp = _fwd_setup_inputs(0)
        return _fwd_reference(*[inp[k] for k in FWD_PARAMS])
    out = _jax.eval_shape(fwd)
    return out.shape, out.dtype

N_MICROBATCH = 1
ADAM_LR = 0.001
ADAM_B1 = 0.9
ADAM_B2 = 0.999
ADAM_EPS = 1e-08
ADAM_WD = 0.01
ADAM_STEP = 10
PER_EXAMPLE_BATCH_AXIS = {'x': 0, 'loss_target': 0}
SHARED_INPUTS = []
_WEIGHT_DTYPES = {'ffn1_norm': _jnp.float32, 'ffn1_w_gate': _jnp.float32, 'ffn1_w_up': _jnp.float32, 'ffn1_w_down': _jnp.float32, 'mix_norm': _jnp.float32, 'w_in': _jnp.float32, 'conv_w': _jnp.float32, 'attn_sinks': _jnp.float32, 'w_out': _jnp.float32, 'ffn2_norm': _jnp.float32, 'ffn2_w_gate': _jnp.float32, 'ffn2_w_up': _jnp.float32, 'ffn2_w_down': _jnp.float32, 'final_norm': _jnp.float32}
MOMENT_SCALE = {'ffn1_norm': 1.466606e-01, 'ffn1_w_gate': 5.848293e-02, 'ffn1_w_up': 5.672196e-02, 'ffn1_w_down': 9.412078e-02, 'mix_norm': 2.543267e-01, 'w_in': 1.672002e-01, 'conv_w': 2.211481e-01, 'attn_sinks': 3.060626e-02, 'w_out': 1.448912e-01, 'ffn2_norm': 8.625498e-02, 'ffn2_w_gate': 3.589849e-02, 'ffn2_w_up': 3.503389e-02, 'ffn2_w_down': 5.805919e-02, 'final_norm': 6.399645e+01}


def _to_microbatches(a, axis):
    t = _jnp.moveaxis(a, axis, 0)
    t = t.reshape((N_MICROBATCH, t.shape[0] // N_MICROBATCH) + t.shape[1:])
    return _jnp.moveaxis(t, 1, axis + 1)


def setup_inputs(seed: int = 0) -> dict:
    inp = _fwd_setup_inputs(seed)
    key = _jax.random.fold_in(_jax.random.key(seed), 7919)
    shape, _ = _output_shape()
    out = dict(inp)
    out["loss_target"] = _jax.random.normal(_jax.random.fold_in(key, 0), shape, _jnp.float32)
    for i, name in enumerate(TWIN_WEIGHTS):
        w = inp[name].astype(_jnp.float32)
        if MOMENT_SCALE is None:
            s = _jnp.sqrt(_jnp.mean(_jnp.square(w)) + 1e-30)
        else:
            s = MOMENT_SCALE[name]
        km, kv = _jax.random.split(_jax.random.fold_in(key, i + 1))
        out[name] = w
        out["m_" + name] = s * _jax.random.normal(km, w.shape, _jnp.float32)
        out["v_" + name] = (s * s) * _jax.random.uniform(kv, w.shape, _jnp.float32, 0.5, 1.5)
    if N_MICROBATCH > 1:
        for name, axis in PER_EXAMPLE_BATCH_AXIS.items():
            out[name] = _to_microbatches(out[name], axis)
    return {'x': out['x'], 'ffn1_norm': out['ffn1_norm'], 'ffn1_w_gate': out['ffn1_w_gate'], 'ffn1_w_up': out['ffn1_w_up'], 'ffn1_w_down': out['ffn1_w_down'], 'mix_norm': out['mix_norm'], 'w_in': out['w_in'], 'conv_w': out['conv_w'], 'attn_sinks': out['attn_sinks'], 'w_out': out['w_out'], 'ffn2_norm': out['ffn2_norm'], 'ffn2_w_gate': out['ffn2_w_gate'], 'ffn2_w_up': out['ffn2_w_up'], 'ffn2_w_down': out['ffn2_w_down'], 'final_norm': out['final_norm'], 'loss_target': out['loss_target'], 'm_ffn1_norm': out['m_ffn1_norm'], 'm_ffn1_w_gate': out['m_ffn1_w_gate'], 'm_ffn1_w_up': out['m_ffn1_w_up'], 'm_ffn1_w_down': out['m_ffn1_w_down'], 'm_mix_norm': out['m_mix_norm'], 'm_w_in': out['m_w_in'], 'm_conv_w': out['m_conv_w'], 'm_attn_sinks': out['m_attn_sinks'], 'm_w_out': out['m_w_out'], 'm_ffn2_norm': out['m_ffn2_norm'], 'm_ffn2_w_gate': out['m_ffn2_w_gate'], 'm_ffn2_w_up': out['m_ffn2_w_up'], 'm_ffn2_w_down': out['m_ffn2_w_down'], 'm_final_norm': out['m_final_norm'], 'v_ffn1_norm': out['v_ffn1_norm'], 'v_ffn1_w_gate': out['v_ffn1_w_gate'], 'v_ffn1_w_up': out['v_ffn1_w_up'], 'v_ffn1_w_down': out['v_ffn1_w_down'], 'v_mix_norm': out['v_mix_norm'], 'v_w_in': out['v_w_in'], 'v_conv_w': out['v_conv_w'], 'v_attn_sinks': out['v_attn_sinks'], 'v_w_out': out['v_w_out'], 'v_ffn2_norm': out['v_ffn2_norm'], 'v_ffn2_w_gate': out['v_ffn2_w_gate'], 'v_ffn2_w_up': out['v_ffn2_w_up'], 'v_ffn2_w_down': out['v_ffn2_w_down'], 'v_final_norm': out['v_final_norm']}


def _loss(weights, diff, rest, loss_target):
    with _jax.named_scope("forward"):
        args = {**rest, TWIN_DIFF_INPUT: diff, **{k: w.astype(_WEIGHT_DTYPES[k]) for k, w in weights.items()}}
        y = _forward(args)
    with _jax.named_scope("loss_head"):
        err = _jnp.square(y.astype(_jnp.float32) - loss_target)
        return 0.5 * _jnp.sum(_jnp.mean(err, axis=-1)) if err.ndim else 0.5 * err


def _adamw(w, g, m, v):
    m = ADAM_B1 * m + (1.0 - ADAM_B1) * g
    v = ADAM_B2 * v + (1.0 - ADAM_B2) * _jnp.square(g)
    m_hat = m / (1.0 - ADAM_B1 ** ADAM_STEP)
    v_hat = v / (1.0 - ADAM_B2 ** ADAM_STEP)
    delta = -ADAM_LR * (m_hat / (_jnp.sqrt(v_hat) + ADAM_EPS) + ADAM_WD * w)
    return delta, m, v


def reference(x, ffn1_norm, ffn1_w_gate, ffn1_w_up, ffn1_w_down, mix_norm, w_in, conv_w, attn_sinks, w_out, ffn2_norm, ffn2_w_gate, ffn2_w_up, ffn2_w_down, final_norm, loss_target, m_ffn1_norm, m_ffn1_w_gate, m_ffn1_w_up, m_ffn1_w_down, m_mix_norm, m_w_in, m_conv_w, m_attn_sinks, m_w_out, m_ffn2_norm, m_ffn2_w_gate, m_ffn2_w_up, m_ffn2_w_down, m_final_norm, v_ffn1_norm, v_ffn1_w_gate, v_ffn1_w_up, v_ffn1_w_down, v_mix_norm, v_w_in, v_conv_w, v_attn_sinks, v_w_out, v_ffn2_norm, v_ffn2_w_gate, v_ffn2_w_up, v_ffn2_w_down, v_final_norm):
    given = dict(x=x, ffn1_norm=ffn1_norm, ffn1_w_gate=ffn1_w_gate, ffn1_w_up=ffn1_w_up, ffn1_w_down=ffn1_w_down, mix_norm=mix_norm, w_in=w_in, conv_w=conv_w, attn_sinks=attn_sinks, w_out=w_out, ffn2_norm=ffn2_norm, ffn2_w_gate=ffn2_w_gate, ffn2_w_up=ffn2_w_up, ffn2_w_down=ffn2_w_down, final_norm=final_norm, loss_target=loss_target, m_ffn1_norm=m_ffn1_norm, m_ffn1_w_gate=m_ffn1_w_gate, m_ffn1_w_up=m_ffn1_w_up, m_ffn1_w_down=m_ffn1_w_down, m_mix_norm=m_mix_norm, m_w_in=m_w_in, m_conv_w=m_conv_w, m_attn_sinks=m_attn_sinks, m_w_out=m_w_out, m_ffn2_norm=m_ffn2_norm, m_ffn2_w_gate=m_ffn2_w_gate, m_ffn2_w_up=m_ffn2_w_up, m_ffn2_w_down=m_ffn2_w_down, m_final_norm=m_final_norm, v_ffn1_norm=v_ffn1_norm, v_ffn1_w_gate=v_ffn1_w_gate, v_ffn1_w_up=v_ffn1_w_up, v_ffn1_w_down=v_ffn1_w_down, v_mix_norm=v_mix_norm, v_w_in=v_w_in, v_conv_w=v_conv_w, v_attn_sinks=v_attn_sinks, v_w_out=v_w_out, v_ffn2_norm=v_ffn2_norm, v_ffn2_w_gate=v_ffn2_w_gate, v_ffn2_w_up=v_ffn2_w_up, v_ffn2_w_down=v_ffn2_w_down, v_final_norm=v_final_norm)
    weights = {n: given[n] for n in TWIN_WEIGHTS}
    shared = {n: given[n] for n in SHARED_INPUTS}
    per_example = {n: given[n] for n in ['x']}
    grad_fn = _jax.value_and_grad(_loss, argnums=(0, 1))

    def one_microbatch(ex, loss_target):
        ex = dict(ex)
        diff = ex.pop(TWIN_DIFF_INPUT)
        return grad_fn(weights, diff, {**shared, **ex}, loss_target)

    if N_MICROBATCH == 1:
        loss, (grad_w, grad_x) = one_microbatch(per_example, given["loss_target"])
    else:
        def body(carry, xs):
            loss_sum, grad_sum = carry
            l_k, (gw_k, gx_k) = one_microbatch(xs[0], xs[1])
            with _jax.named_scope("update"):
                return (loss_sum + l_k, _jax.tree.map(_jnp.add, grad_sum, gw_k)), gx_k

        init = (_jnp.zeros((), _jnp.float32), _jax.tree.map(_jnp.zeros_like, weights))
        (loss, grad_w), grad_x = _jax.lax.scan(body, init, (per_example, given["loss_target"]))
    with _jax.named_scope("update"):
        delta_w, new_m, new_v = {}, {}, {}
        for n in TWIN_WEIGHTS:
            delta_w[n], new_m[n], new_v[n] = _adamw(weights[n], grad_w[n], given["m_" + n], given["v_" + n])
    return (loss, grad_x, *[grad_w[n] for n in TWIN_WEIGHTS], *[delta_w[n] for n in TWIN_WEIGHTS],
            *[new_m[n] for n in TWIN_WEIGHTS], *[new_v[n] for n in TWIN_WEIGHTS])
```

```python
import functools

import jax
import jax.numpy as jnp
from jax import lax
from jax.experimental import pallas as pl
from jax.experimental.pallas import tpu as pltpu

F32 = jnp.float32
BF16 = jnp.bfloat16
MESH = pl.DeviceIdType.MESH
ANY = pl.BlockSpec(memory_space=pl.ANY)

N_DEV = 8
D_MODEL = 1024
D_FF = 2816
CONV_W = 512
ATTN_W = 512
KV_W = 128
HEAD_DIM = 64
N_Q_HEADS = 8
N_KV_HEADS = 2
Q_PER_KV = N_Q_HEADS // N_KV_HEADS
BLOCK = 128
ROT_DIM = 16
ROPE_THETA = 500000.0
Z_W = 3 * CONV_W + ATTN_W + 2 * KV_W
Q_OFF = 3 * CONV_W
K_OFF = Q_OFF + ATTN_W
V_OFF = K_OFF + KV_W
RMS_EPS = 1e-5
MASK_VALUE = -1e30
SM_SCALE = HEAD_DIM ** -0.5
FFN_RES_SCALE = 0.5

ADAM_LR = 0.001
ADAM_B1 = 0.9
ADAM_B2 = 0.999
ADAM_EPS = 1e-08
ADAM_WD = 0.01
ADAM_STEP = 10

NT_DIMS = (((1,), (1,)), ((), ()))
TN_DIMS = (((0,), (0,)), ((), ()))

VMEM_LIMIT = 56 * 1024 * 1024


def _params(sem, vmem=None):
    return pltpu.CompilerParams(dimension_semantics=sem, vmem_limit_bytes=vmem)


def _rms_stats(xf):
    inv = lax.rsqrt(jnp.mean(xf * xf, axis=-1, keepdims=True) + RMS_EPS)
    return xf * inv, inv


def _rms_bwd(dh, xhat, inv, gain):
    dxhat = dh * gain
    dx = inv * (dxhat - xhat * jnp.mean(dxhat * xhat, axis=-1, keepdims=True))
    dgain = jnp.sum(dh * xhat, axis=0, keepdims=True)
    return dx, dgain


def _ffn_fwd(x, gain, w3, *, name, tm=1024, tf=256):
    t = x.shape[0]
    tm = min(tm, t)
    nf = D_FF // tf

    def body(x_ref, g_ref, wg_ref, wu_ref, wd_ref, xo_ref, h_ref, a_ref, b_ref, acc_ref):
        j = pl.program_id(1)

        @pl.when(j == 0)
        def _():
            xhat, _ = _rms_stats(x_ref[...])
            h_ref[...] = (xhat * g_ref[...]).astype(BF16)
            acc_ref[...] = jnp.zeros_like(acc_ref)

        h = h_ref[...]
        a = lax.dot_general(h, wg_ref[0], NT_DIMS, preferred_element_type=F32)
        b = lax.dot_general(h, wu_ref[0], NT_DIMS, preferred_element_type=F32)
        a_ref[...] = a.astype(BF16)
        b_ref[...] = b.astype(BF16)
        s = (a * jax.nn.sigmoid(a) * b).astype(BF16)
        acc_ref[...] += jnp.dot(s, wd_ref[0], preferred_element_type=F32)

        @pl.when(j == nf - 1)
        def _():
            xo_ref[...] = x_ref[...] + FFN_RES_SCALE * acc_ref[...]

    row = pl.BlockSpec((tm, D_MODEL), lambda i, j: (i, 0))
    hid = pl.BlockSpec((tm, tf), lambda i, j: (i, j))
    wspec = [pl.BlockSpec((1, tf, D_MODEL), functools.partial(lambda i, j, k: (k, j, 0), k=k)) for k in range(3)]
    return pl.pallas_call(
        body, name=name, grid=(t // tm, nf),
        in_specs=[row, pl.BlockSpec((1, D_MODEL), lambda i, j: (0, 0))] + wspec,
        out_specs=[row, row, hid, hid],
        out_shape=[jax.ShapeDtypeStruct((t, D_MODEL), F32), jax.ShapeDtypeStruct((t, D_MODEL), BF16),
                   jax.ShapeDtypeStruct((t, D_FF), BF16), jax.ShapeDtypeStruct((t, D_FF), BF16)],
        scratch_shapes=[pltpu.VMEM((tm, D_MODEL), F32)],
        compiler_params=_params(("parallel", "arbitrary"), VMEM_LIMIT),
    )(x, gain, w3, w3, w3)


def _ffn_dgrad(dxo, x, gain, a, b, w3, *, name, tm=512, tf=256):
    t = x.shape[0]
    tm = min(tm, t)
    nt, nf = t // tm, D_FF // tf

    def body(dxo_ref, x_ref, g_ref, a_ref, b_ref, wg_ref, wu_ref, wd_ref,
             dxi_ref, da_ref, db_ref, s_ref, gb_ref, dg_ref, acc_ref):
        i, j = pl.program_id(0), pl.program_id(1)

        @pl.when(j == 0)
        def _():
            gb_ref[...] = (FFN_RES_SCALE * dxo_ref[...]).astype(BF16)
            acc_ref[...] = jnp.zeros_like(acc_ref)

        @pl.when((i == 0) & (j == 0))
        def _():
            dg_ref[...] = jnp.zeros_like(dg_ref)

        ds = lax.dot_general(gb_ref[...], wd_ref[0], NT_DIMS, preferred_element_type=F32)
        af = a_ref[...].astype(F32)
        bf = b_ref[...].astype(F32)
        sig = jax.nn.sigmoid(af)
        silu = af * sig
        da = (ds * bf * (sig * (1.0 + af * (1.0 - sig)))).astype(BF16)
        db = (ds * silu).astype(BF16)
        da_ref[...] = da
        db_ref[...] = db
        s_ref[...] = (silu * bf).astype(BF16)
        acc_ref[...] += (jnp.dot(da, wg_ref[0], preferred_element_type=F32)
                         + jnp.dot(db, wu_ref[0], preferred_element_type=F32))

        @pl.when(j == nf - 1)
        def _():
            xhat, inv = _rms_stats(x_ref[...])
            dx, dgain = _rms_bwd(acc_ref[...], xhat, inv, g_ref[...])
            dxi_ref[...] = dxo_ref[...] + dx
            dg_ref[...] += dgain

    row = pl.BlockSpec((tm, D_MODEL), lambda i, j: (i, 0))
    hid = pl.BlockSpec((tm, tf), lambda i, j: (i, j))
    vec = pl.BlockSpec((1, D_MODEL), lambda i, j: (0, 0))
    wspec = [pl.BlockSpec((1, tf, D_MODEL), functools.partial(lambda i, j, k: (k, j, 0), k=k)) for k in range(3)]
    return pl.pallas_call(
        body, name=name, grid=(nt, nf),
        in_specs=[row, row, vec, hid, hid] + wspec,
        out_specs=[row, hid, hid, hid, row, vec],
        out_shape=[jax.ShapeDtypeStruct((t, D_MODEL), F32), jax.ShapeDtypeStruct((t, D_FF), BF16),
                   jax.ShapeDtypeStruct((t, D_FF), BF16), jax.ShapeDtypeStruct((t, D_FF), BF16),
                   jax.ShapeDtypeStruct((t, D_MODEL), BF16), jax.ShapeDtypeStruct((1, D_MODEL), F32)],
        scratch_shapes=[pltpu.VMEM((tm, D_MODEL), F32)],
        compiler_params=_params(("arbitrary", "arbitrary"), VMEM_LIMIT),
    )(dxo, x, gain, a, b, w3, w3, w3)


def _tn_matmul(a, b, *, name, bm, tk=1024):
    t, m = a.shape
    n = b.shape[1]
    tk = min(tk, t)
    nk = t // tk

    def body(a_ref, b_ref, o_ref):
        k = pl.program_id(1)
        p = lax.dot_general(a_ref[...], b_ref[...], TN_DIMS, preferred_element_type=F32)

        @pl.when(k == 0)
        def _():
            o_ref[...] = p

        @pl.when(k > 0)
        def _():
            o_ref[...] += p

    return pl.pallas_call(
        body, name=name, grid=(m // bm, nk),
        in_specs=[pl.BlockSpec((tk, bm), lambda i, k: (k, i)), pl.BlockSpec((tk, n), lambda i, k: (k, 0))],
        out_specs=pl.BlockSpec((bm, n), lambda i, k: (i, 0)),
        out_shape=jax.ShapeDtypeStruct((m, n), F32),
        compiler_params=_params(("parallel", "arbitrary"), VMEM_LIMIT),
    )(a, b)


def _loss_head(x, gain, target, *, name, tm=512):
    t = x.shape[0]
    tm = min(tm, t)

    def body(x_ref, g_ref, t_ref, dx_ref, loss_ref, dg_ref):
        @pl.when(pl.program_id(0) == 0)
        def _():
            loss_ref[...] = jnp.zeros_like(loss_ref)
            dg_ref[...] = jnp.zeros_like(dg_ref)

        xhat, inv = _rms_stats(x_ref[...])
        err = xhat * g_ref[...] - t_ref[...]
        loss_ref[...] += 0.5 * jnp.sum(jnp.mean(err * err, axis=-1, keepdims=True), axis=0, keepdims=True)
        dx, dgain = _rms_bwd(err * (1.0 / D_MODEL), xhat, inv, g_ref[...])
        dx_ref[...] = dx
        dg_ref[...] += dgain

    row = pl.BlockSpec((tm, D_MODEL), lambda i: (i, 0))
    vec = pl.BlockSpec((1, D_MODEL), lambda i: (0, 0))
    return pl.pallas_call(
        body, name=name, grid=(t // tm,),
        in_specs=[row, vec, row],
        out_specs=[row, pl.BlockSpec((1, 1), lambda i: (0, 0)), vec],
        out_shape=[jax.ShapeDtypeStruct((t, D_MODEL), F32), jax.ShapeDtypeStruct((1, 1), F32),
                   jax.ShapeDtypeStruct((1, D_MODEL), F32)],
        compiler_params=_params(("arbitrary",)),
    )(x, gain, target)


def _rope_tables(t):
    half = ROT_DIM // 2
    inv_freq = ROPE_THETA ** (-jnp.arange(0, ROT_DIM, 2, dtype=F32) / ROT_DIM)
    ang = jnp.arange(t, dtype=F32)[:, None] * inv_freq[None, :]
    cos, sin = jnp.cos(ang), jnp.sin(ang)
    zeros, ones = jnp.zeros((t, half), F32), jnp.ones((t, HEAD_DIM - ROT_DIM), F32)
    pad = jnp.zeros((t, HEAD_DIM - ROT_DIM), F32)
    mult = jnp.concatenate([cos, cos, ones], axis=1)
    from_lo = jnp.concatenate([zeros, sin, pad], axis=1)
    from_hi = jnp.concatenate([-sin, zeros, pad], axis=1)
    return jnp.stack([jnp.tile(m, (1, 2)) for m in (mult, from_lo, from_hi)])


def _tile_lanes(tab, width):
    return jnp.tile(tab, (1, width // tab.shape[1]))


def _rope(v, tab):
    w = v.shape[1]
    half_rot = ROT_DIM // 2
    return (v * _tile_lanes(tab[0], w)
            + pltpu.roll(v, half_rot, axis=1) * _tile_lanes(tab[1], w)
            + pltpu.roll(v, w - half_rot, axis=1) * _tile_lanes(tab[2], w))


def _rope_bwd(dv, tab):
    w = dv.shape[1]
    half_rot = ROT_DIM // 2
    return (dv * _tile_lanes(tab[0], w)
            + pltpu.roll(dv * _tile_lanes(tab[1], w), w - half_rot, axis=1)
            + pltpu.roll(dv * _tile_lanes(tab[2], w), half_rot, axis=1))


def _shift_rows(v, prev8_ref, n):
    r = lax.broadcasted_iota(jnp.int32, v.shape, 0)
    rolled = pltpu.roll(v, n, axis=0)
    last = prev8_ref[7:8, :]
    if n == 1:
        return jnp.where(r >= 1, rolled, last)
    return jnp.where(r >= 2, rolled, jnp.where(r == 0, prev8_ref[6:7, :], last))


def _shift_rows_up(v, next8_ref, n):
    rows = v.shape[0]
    r = lax.broadcasted_iota(jnp.int32, v.shape, 0)
    rolled = pltpu.roll(v, rows - n, axis=0)
    first = next8_ref[0:1, :]
    if n == 1:
        return jnp.where(r <= rows - 2, rolled, first)
    return jnp.where(r <= rows - 3, rolled, jnp.where(r == rows - 2, first, next8_ref[1:2, :]))


def _lane_half_mask(shape, half):
    lane = lax.broadcasted_iota(jnp.int32, shape, 1)
    return (lane >= HEAD_DIM) if half else (lane < HEAD_DIM)


def _to_kv_lanes(chunk, head, kv):
    if head % 2 != kv:
        chunk = pltpu.roll(chunk, HEAD_DIM, axis=1)
    return jnp.where(_lane_half_mask(chunk.shape, kv), chunk, 0.0)


def _from_kv_lanes(chunk, head, kv):
    chunk = jnp.where(_lane_half_mask(chunk.shape, kv), chunk, 0.0)
    if head % 2 != kv:
        chunk = pltpu.roll(chunk, HEAD_DIM, axis=1)
    return chunk


def _stack_heads(wide, kv):
    parts = []
    for g in range(Q_PER_KV):
        head = kv * Q_PER_KV + g
        chunk = wide[:, (head // 2) * 128:(head // 2 + 1) * 128]
        parts.append(_to_kv_lanes(chunk, head, kv))
    return jnp.concatenate(parts, axis=0)


def _window_mask(has_prev):
    shape = (Q_PER_KV * BLOCK, 2 * BLOCK)
    qi = lax.broadcasted_iota(jnp.int32, shape, 0) & (BLOCK - 1)
    kj = lax.broadcasted_iota(jnp.int32, shape, 1)
    first_key = BLOCK - has_prev * BLOCK
    in_prev = (kj < BLOCK) & (kj > qi) & (kj >= first_key)
    in_own = (kj >= BLOCK) & ((kj - BLOCK) <= qi)
    return in_prev | in_own


def _sink_column(sink_ref, kv):
    row = lax.broadcasted_iota(jnp.int32, (Q_PER_KV * BLOCK, 1), 0)
    col = jnp.full((Q_PER_KV * BLOCK, 1), sink_ref[0, kv * Q_PER_KV], F32)
    for g in range(1, Q_PER_KV):
        col = jnp.where(row >= g * BLOCK, sink_ref[0, kv * Q_PER_KV + g], col)
    return col


def _softmax_with_sink(q4, k2, mask, sink):
    s = lax.dot_general(q4, k2, NT_DIMS, preferred_element_type=F32) * SM_SCALE
    s = jnp.where(mask, s, MASK_VALUE)
    m = jnp.maximum(jnp.max(s, axis=-1, keepdims=True), sink)
    p = jnp.exp(s - m)
    e_sink = jnp.exp(sink - m)
    inv_den = 1.0 / (jnp.sum(p, axis=-1, keepdims=True) + e_sink)
    return p * inv_den, e_sink * inv_den


def _conv_terms(zf, prev8_ref, w_ref):
    b_gate, c_gate, u = zf[:, 0:CONV_W], zf[:, CONV_W:2 * CONV_W], zf[:, 2 * CONV_W:3 * CONV_W]
    vc = c_gate * u
    vm1 = _shift_rows(vc, prev8_ref, 1)
    vm2 = _shift_rows(vc, prev8_ref, 2)
    conv = w_ref[0:1, :] * vm2 + w_ref[1:2, :] * vm1 + w_ref[2:3, :] * vc
    return b_gate, c_gate, u, vc, vm1, vm2, conv


def _mixer_fwd(x, gain, win_t, wout, conv_w, sinks, rope, *, name):
    t = x.shape[0]
    nb = t // BLOCK

    def body(x_ref, g_ref, win_ref, wout_ref, cw_ref, sink_ref, rope_ref,
             xo_ref, h_ref, z_ref, y_ref, kprev_ref, vprev_ref, cprev_ref):
        i = pl.program_id(0)

        @pl.when(i == 0)
        def _():
            kprev_ref[...] = jnp.zeros_like(kprev_ref)
            vprev_ref[...] = jnp.zeros_like(vprev_ref)
            cprev_ref[...] = jnp.zeros_like(cprev_ref)

        xf = x_ref[...]
        xhat, _ = _rms_stats(xf)
        h = (xhat * g_ref[...]).astype(BF16)
        h_ref[...] = h
        zb = lax.dot_general(h, win_ref[...], NT_DIMS, preferred_element_type=F32).astype(BF16)
        z_ref[...] = zb
        zf = zb.astype(F32)

        b_gate, _, _, vc, _, _, conv = _conv_terms(zf, cprev_ref, cw_ref)
        y_conv = b_gate * conv
        cprev_ref[...] = vc[BLOCK - 8:BLOCK, :]

        tab = rope_ref[...]
        qr = _rope(zf[:, Q_OFF:K_OFF], tab)
        kr = _rope(zf[:, K_OFF:V_OFF], tab).astype(BF16)
        vb = zb[:, V_OFF:Z_W]
        k2 = jnp.concatenate([kprev_ref[...], kr], axis=0)
        v2 = jnp.concatenate([vprev_ref[...], vb], axis=0)
        kprev_ref[...] = kr
        vprev_ref[...] = vb

        mask = _window_mask(jnp.minimum(i, 1))
        chunks = [jnp.zeros((BLOCK, 128), F32) for _ in range(ATTN_W // 128)]
        for kv in range(N_KV_HEADS):
            q4 = _stack_heads(qr, kv).astype(BF16)
            probs, _ = _softmax_with_sink(q4, k2, mask, _sink_column(sink_ref, kv))
            o4 = jnp.dot(probs.astype(BF16), v2, preferred_element_type=F32)
            for g in range(Q_PER_KV):
                head = kv * Q_PER_KV + g
                chunks[head // 2] += _from_kv_lanes(o4[g * BLOCK:(g + 1) * BLOCK], head, kv)
        y = jnp.concatenate([y_conv] + chunks, axis=1).astype(BF16)
        y_ref[...] = y
        xo_ref[...] = xf + jnp.dot(y, wout_ref[...], preferred_element_type=F32)

    row = pl.BlockSpec((BLOCK, D_MODEL), lambda i: (i, 0))
    full = lambda shape: pl.BlockSpec(shape, lambda i: (0,) * len(shape))
    return pl.pallas_call(
        body, name=name, grid=(nb,),
        in_specs=[row, full((1, D_MODEL)), full((Z_W, D_MODEL)), full((D_MODEL, D_MODEL)), full((3, CONV_W)),
                  pl.BlockSpec(memory_space=pltpu.SMEM), pl.BlockSpec((3, BLOCK, 128), lambda i: (0, i, 0))],
        out_specs=[row, row, pl.BlockSpec((BLOCK, Z_W), lambda i: (i, 0)), row],
        out_shape=[jax.ShapeDtypeStruct((t, D_MODEL), F32), jax.ShapeDtypeStruct((t, D_MODEL), BF16),
                   jax.ShapeDtypeStruct((t, Z_W), BF16), jax.ShapeDtypeStruct((t, D_MODEL), BF16)],
        scratch_shapes=[pltpu.VMEM((BLOCK, KV_W), BF16), pltpu.VMEM((BLOCK, KV_W), BF16),
                        pltpu.VMEM((8, CONV_W), F32)],
        compiler_params=_params(("arbitrary",), VMEM_LIMIT),
    )(x, gain, win_t, wout, conv_w, sinks, rope)


def _mixer_bwd(dxo, x, gain, y, z, win_t, wout, conv_w, sinks, rope, *, name):
    t = x.shape[0]
    nb = t // BLOCK

    def body(dxo_ref, x_ref, g_ref, y_ref, z_ref, zp_ref, win_ref, wout_ref, cw_ref, sink_ref, rope_ref, ropep_ref,
             dxi_ref, dz_ref, gb_ref, dcw_ref, dsink_ref, dg_ref, dk_ref, dv_ref, dcn_ref, pvc_ref):
        i = pl.program_id(0)
        blk = nb - 1 - i

        @pl.when(i == 0)
        def _():
            dk_ref[...] = jnp.zeros_like(dk_ref)
            dv_ref[...] = jnp.zeros_like(dv_ref)
            dcn_ref[...] = jnp.zeros_like(dcn_ref)
            dcw_ref[...] = jnp.zeros_like(dcw_ref)
            dsink_ref[...] = jnp.zeros_like(dsink_ref)
            dg_ref[...] = jnp.zeros_like(dg_ref)

        has_prev = jnp.minimum(blk, 1)
        go = dxo_ref[...]
        gb = go.astype(BF16)
        gb_ref[...] = gb
        dy = lax.dot_general(gb, wout_ref[...], NT_DIMS, preferred_element_type=F32)
        dy_conv, dy_attn = dy[:, 0:CONV_W], dy[:, CONV_W:D_MODEL]
        zb, zpb = z_ref[...], zp_ref[...]
        zf = zb.astype(F32)
        zpf = zpb.astype(F32) * has_prev.astype(F32)

        pvc_ref[...] = (zpf[:, CONV_W:2 * CONV_W] * zpf[:, 2 * CONV_W:3 * CONV_W])[BLOCK - 8:BLOCK, :]
        b_gate, c_gate, u, vc, vm1, vm2, conv = _conv_terms(zf, pvc_ref, cw_ref)
        d_bgate = dy_conv * conv
        dc = dy_conv * b_gate
        tap = lax.broadcasted_iota(jnp.int32, (8, CONV_W), 0)
        dcw_ref[...] += jnp.where(tap == 0, jnp.sum(dc * vm2, axis=0, keepdims=True),
                                  jnp.where(tap == 1, jnp.sum(dc * vm1, axis=0, keepdims=True),
                                            jnp.where(tap == 2, jnp.sum(dc * vc, axis=0, keepdims=True), 0.0)))
        dvc = (cw_ref[2:3, :] * dc + cw_ref[1:2, :] * _shift_rows_up(dc, dcn_ref, 1)
               + cw_ref[0:1, :] * _shift_rows_up(dc, dcn_ref, 2))
        dcn_ref[...] = dc[0:8, :]
        d_cgate = dvc * u
        d_u = dvc * c_gate

        tab, tabp = rope_ref[...], ropep_ref[...]
        qr = _rope(zf[:, Q_OFF:K_OFF], tab)
        kr = _rope(zf[:, K_OFF:V_OFF], tab).astype(BF16)
        kpr = _rope(zpf[:, K_OFF:V_OFF], tabp).astype(BF16)
        k2 = jnp.concatenate([kpr, kr], axis=0)
        v2 = jnp.concatenate([zpb[:, V_OFF:Z_W], zb[:, V_OFF:Z_W]], axis=0)
        out = y_ref[:, CONV_W:D_MODEL].astype(F32)
        do_out = dy_attn * out
        mask = _window_mask(has_prev)
        dk2 = jnp.zeros((2 * BLOCK, KV_W), F32)
        dv2 = jnp.zeros((2 * BLOCK, KV_W), F32)
        dq_chunks = [jnp.zeros((BLOCK, 128), F32) for _ in range(ATTN_W // 128)]
        lane = lax.broadcasted_iota(jnp.int32, (1, 128), 1)
        dsink = jnp.zeros((1, 128), F32)
        for kv in range(N_KV_HEADS):
            q4 = _stack_heads(qr, kv).astype(BF16)
            do4 = _stack_heads(dy_attn, kv).astype(BF16)
            delta = jnp.sum(_stack_heads(do_out, kv), axis=-1, keepdims=True)
            probs, p_sink = _softmax_with_sink(q4, k2, mask, _sink_column(sink_ref, kv))
            dp = lax.dot_general(do4, v2, NT_DIMS, preferred_element_type=F32)
            ds = (probs * (dp - delta) * SM_SCALE).astype(BF16)
            dq4 = jnp.dot(ds, k2, preferred_element_type=F32)
            dk2 += lax.dot_general(ds, q4, TN_DIMS, preferred_element_type=F32)
            dv2 += lax.dot_general(probs.astype(BF16), do4, TN_DIMS, preferred_element_type=F32)
            sink_terms = p_sink * delta
            for g in range(Q_PER_KV):
                head = kv * Q_PER_KV + g
                rows = slice(g * BLOCK, (g + 1) * BLOCK)
                dq_chunks[head // 2] += _from_kv_lanes(dq4[rows], head, kv)
                dsink = dsink - jnp.where(lane == head, jnp.sum(sink_terms[rows], axis=0, keepdims=True), 0.0)
        dsink_ref[...] += dsink
        dq = _rope_bwd(jnp.concatenate(dq_chunks, axis=1), tab)
        dk = _rope_bwd(dk2[BLOCK:] + dk_ref[...], tab)
        dv = dv2[BLOCK:] + dv_ref[...]
        dk_ref[...] = dk2[:BLOCK]
        dv_ref[...] = dv2[:BLOCK]

        dzb = jnp.concatenate([d_bgate, d_cgate, d_u, dq, dk, dv], axis=1).astype(BF16)
        dz_ref[...] = dzb
        dh = jnp.dot(dzb, win_ref[...], preferred_element_type=F32)
        xhat, inv = _rms_stats(x_ref[...])
        dx, dgain = _rms_bwd(dh, xhat, inv, g_ref[...])
        dxi_ref[...] = go + dx
        dg_ref[...] += dgain

    rev = lambda i: (nb - 1 - i, 0)
    rev_prev = lambda i: (jnp.maximum(nb - 2 - i, 0), 0)
    row = pl.BlockSpec((BLOCK, D_MODEL), rev)
    full = lambda shape: pl.BlockSpec(shape, lambda i: (0,) * len(shape))
    return pl.pallas_call(
        body, name=name, grid=(nb,),
        in_specs=[row, row, full((1, D_MODEL)), row,
                  pl.BlockSpec((BLOCK, Z_W), rev), pl.BlockSpec((BLOCK, Z_W), rev_prev),
                  full((Z_W, D_MODEL)), full((D_MODEL, D_MODEL)), full((3, CONV_W)),
                  pl.BlockSpec(memory_space=pltpu.SMEM),
                  pl.BlockSpec((3, BLOCK, 128), lambda i: (0, nb - 1 - i, 0)),
                  pl.BlockSpec((3, BLOCK, 128), lambda i: (0, jnp.maximum(nb - 2 - i, 0), 0))],
        out_specs=[row, pl.BlockSpec((BLOCK, Z_W), rev), row, full((8, CONV_W)), full((1, 128)), full((1, D_MODEL))],
        out_shape=[jax.ShapeDtypeStruct((t, D_MODEL), F32), jax.ShapeDtypeStruct((t, Z_W), BF16),
                   jax.ShapeDtypeStruct((t, D_MODEL), BF16), jax.ShapeDtypeStruct((8, CONV_W), F32),
                   jax.ShapeDtypeStruct((1, 128), F32), jax.ShapeDtypeStruct((1, D_MODEL), F32)],
        scratch_shapes=[pltpu.VMEM((BLOCK, KV_W), F32), pltpu.VMEM((BLOCK, KV_W), F32), pltpu.VMEM((8, CONV_W), F32),
                        pltpu.VMEM((8, CONV_W), F32)],
        compiler_params=_params(("arbitrary",), VMEM_LIMIT),
    )(dxo, x, gain, y, z, z, win_t, wout, conv_w, sinks, rope, rope)


def _place():
    x, y, c = lax.axis_index("x"), lax.axis_index("y"), lax.axis_index("c")
    other_chips = [(1 - x, y), (x, 1 - y), (1 - x, 1 - y)]
    return x, y, c, other_chips


def _all_gather_rows(shards, *, name):
    n = len(shards)

    def body(*refs):
        srcs, outs = refs[:n], refs[n:2 * n]
        send_sems, recv_sems, local_sems = refs[2 * n:]
        x, y, c, chips = _place()
        me, sibling = (x, y, c), (x, y, 1 - c)

        def rows(t, px, py, pc):
            r = srcs[t].shape[-2]
            start = pl.multiple_of((4 * px + 2 * py + pc) * r, 16 if r % 16 == 0 else 8)
            if len(srcs[t].shape) == 3:
                return outs[t].at[:, pl.ds(start, r), :]
            return outs[t].at[pl.ds(start, r), :]

        def copy(t, k, block, to, own=False):
            return pltpu.make_async_remote_copy(
                src_ref=srcs[t] if own else rows(t, *block), dst_ref=rows(t, *block),
                send_sem=send_sems.at[t, k], recv_sem=recv_sems.at[t, k], device_id=to, device_id_type=MESH)

        mine = [pltpu.make_async_copy(srcs[t], rows(t, *me), local_sems.at[t]) for t in range(n)]
        first = []
        for t in range(n):
            mine[t].start()
            first.append(copy(t, 0, me, sibling, own=True))
            first += [copy(t, 1 + j, me, (*chip, c), own=True) for j, chip in enumerate(chips)]
        for cp in first:
            cp.start()
        passed = []
        for j, chip in enumerate(chips):
            for t in range(n):
                copy(t, 1 + j, (*chip, c), me).wait_recv()
                fwd = copy(t, 4 + j, (*chip, c), sibling)
                fwd.start()
                passed.append(fwd)
        for t in range(n):
            copy(t, 0, sibling, me).wait_recv()
            for j, chip in enumerate(chips):
                copy(t, 4 + j, (*chip, 1 - c), me).wait_recv()
        for cp in first + passed:
            cp.wait_send()
        for cp in mine:
            cp.wait()

    out_shape = [jax.ShapeDtypeStruct(s.shape[:-2] + (N_DEV * s.shape[-2], s.shape[-1]), s.dtype) for s in shards]
    return pl.pallas_call(
        body, name=name, in_specs=[ANY] * n, out_specs=[ANY] * n, out_shape=out_shape,
        scratch_shapes=[pltpu.SemaphoreType.DMA((n, 7)), pltpu.SemaphoreType.DMA((n, 7)),
                        pltpu.SemaphoreType.DMA((n,))],
    )(*shards)


def _exchange_sibling(grads, *, name):
    n = len(grads)

    def body(*refs):
        srcs, outs = refs[:n], refs[n:2 * n]
        send_sems, recv_sems = refs[2 * n:]
        x, y, c, _ = _place()
        copies = [pltpu.make_async_remote_copy(
            src_ref=srcs[t].at[:, 1 - c], dst_ref=outs[t], send_sem=send_sems.at[t], recv_sem=recv_sems.at[t],
            device_id=(x, y, 1 - c), device_id_type=MESH) for t in range(n)]
        for cp in copies:
            cp.start()
        for cp in copies:
            cp.wait()

    out_shape = [jax.ShapeDtypeStruct((4,) + g.shape[2:], g.dtype) for g in grads]
    return pl.pallas_call(
        body, name=name, in_specs=[ANY] * n, out_specs=[ANY] * n, out_shape=out_shape,
        scratch_shapes=[pltpu.SemaphoreType.DMA((n,)), pltpu.SemaphoreType.DMA((n,))],
    )(*grads)


def _exchange_chips(partials, small, *, name):
    n = len(partials)

    def body(*refs):
        srcs, small_ref = refs[:n], refs[n]
        outs, small_out = refs[n + 1:2 * n + 1], refs[2 * n + 1]
        send_sems, recv_sems, small_send, small_recv, local_sem = refs[2 * n + 2:]
        x, y, c, chips = _place()
        my_index = 4 * x + 2 * y + c
        copies = []
        for t in range(n):
            for j, (px, py) in enumerate(chips):
                copies.append(pltpu.make_async_remote_copy(
                    src_ref=srcs[t].at[2 * px + py], dst_ref=outs[t].at[j],
                    send_sem=send_sems.at[t, j], recv_sem=recv_sems.at[t, j],
                    device_id=(px, py, c), device_id_type=MESH))
        flips = [(fx, fy, fc) for fx in range(2) for fy in range(2) for fc in range(2)][1:]
        for k, (fx, fy, fc) in enumerate(flips):
            peer = (x + fx - 2 * x * fx, y + fy - 2 * y * fy, c + fc - 2 * c * fc)
            copies.append(pltpu.make_async_remote_copy(
                src_ref=small_ref, dst_ref=small_out.at[my_index],
                send_sem=small_send.at[k], recv_sem=small_recv.at[k], device_id=peer, device_id_type=MESH))
        own = pltpu.make_async_copy(small_ref, small_out.at[my_index], local_sem)
        own.start()
        for cp in copies:
            cp.start()
        for cp in copies:
            cp.wait()
        own.wait()

    out_shape = ([jax.ShapeDtypeStruct((3,) + p.shape[1:], p.dtype) for p in partials]
                 + [jax.ShapeDtypeStruct((N_DEV,) + small.shape, small.dtype)])
    return pl.pallas_call(
        body, name=name, in_specs=[ANY] * (n + 1), out_specs=[ANY] * (n + 1), out_shape=out_shape,
        scratch_shapes=[pltpu.SemaphoreType.DMA((n, 3)), pltpu.SemaphoreType.DMA((n, 3)),
                        pltpu.SemaphoreType.DMA((7,)), pltpu.SemaphoreType.DMA((7,)), pltpu.SemaphoreType.DMA(())],
    )(*partials, small)


def _add_sibling(grad, recv, core, *, name, tr):
    rows = grad.shape[2]

    def body(core_ref, g_ref, r_ref, o_ref):
        o_ref[...] = g_ref[:, 0] + r_ref[...]

    return pl.pallas_call(
        body, name=name,
        grid_spec=pltpu.PrefetchScalarGridSpec(
            num_scalar_prefetch=1, grid=(rows // tr,),
            in_specs=[pl.BlockSpec((4, 1, tr, D_MODEL), lambda i, core_ref: (0, core_ref[0], i, 0)),
                      pl.BlockSpec((4, tr, D_MODEL), lambda i, core_ref: (0, i, 0))],
            out_specs=pl.BlockSpec((4, tr, D_MODEL), lambda i, core_ref: (0, i, 0))),
        out_shape=jax.ShapeDtypeStruct(recv.shape, F32),
        compiler_params=_params(("arbitrary",)),
    )(core, grad, recv)


def _add_chips(partial, recv, chip, *, name, tr):
    rows = partial.shape[1]

    def body(chip_ref, p_ref, r_ref, o_ref):
        o_ref[...] = p_ref[0] + r_ref[0] + r_ref[1] + r_ref[2]

    return pl.pallas_call(
        body, name=name,
        grid_spec=pltpu.PrefetchScalarGridSpec(
            num_scalar_prefetch=1, grid=(rows // tr,),
            in_specs=[pl.BlockSpec((1, tr, D_MODEL), lambda i, chip_ref: (chip_ref[0], i, 0)),
                      pl.BlockSpec((3, tr, D_MODEL), lambda i, chip_ref: (0, i, 0))],
            out_specs=pl.BlockSpec((tr, D_MODEL), lambda i, chip_ref: (i, 0))),
        out_shape=jax.ShapeDtypeStruct((rows, D_MODEL), F32),
        compiler_params=_params(("arbitrary",)),
    )(chip, partial, recv)


def _adamw_math(w, g, m, v):
    m = ADAM_B1 * m + (1.0 - ADAM_B1) * g
    v = ADAM_B2 * v + (1.0 - ADAM_B2) * (g * g)
    m_hat = m / (1.0 - ADAM_B1 ** ADAM_STEP)
    v_hat = v / (1.0 - ADAM_B2 ** ADAM_STEP)
    delta = -ADAM_LR * (m_hat / (jnp.sqrt(v_hat) + ADAM_EPS) + ADAM_WD * w)
    return delta, m, v


def _adamw(w, g, m, v, *, name, tr):
    rows, cols = w.shape

    def body(w_ref, g_ref, m_ref, v_ref, d_ref, mo_ref, vo_ref):
        d_ref[...], mo_ref[...], vo_ref[...] = _adamw_math(w_ref[...], g_ref[...], m_ref[...], v_ref[...])

    spec = pl.BlockSpec((tr, cols), lambda i: (i, 0))
    return pl.pallas_call(
        body, name=name, grid=(rows // tr,), in_specs=[spec] * 4, out_specs=[spec] * 3,
        out_shape=[jax.ShapeDtypeStruct(w.shape, F32)] * 3,
        compiler_params=_params(("parallel",)),
    )(w, g, m, v)


def _sum_small(gathered, *, name):
    def body(g_ref, o_ref):
        acc = g_ref[0]
        for k in range(1, N_DEV):
            acc = acc + g_ref[k]
        o_ref[...] = acc

    return pl.pallas_call(body, name=name, out_shape=jax.ShapeDtypeStruct(gathered.shape[1:], F32))(gathered)


def kernel(x, ffn1_norm, ffn1_w_gate, ffn1_w_up, ffn1_w_down, mix_norm, w_in, conv_w, attn_sinks, w_out, ffn2_norm, ffn2_w_gate, ffn2_w_up, ffn2_w_down, final_norm, loss_target, m_ffn1_norm, m_ffn1_w_gate, m_ffn1_w_up, m_ffn1_w_down, m_mix_norm, m_w_in, m_conv_w, m_attn_sinks, m_w_out, m_ffn2_norm, m_ffn2_w_gate, m_ffn2_w_up, m_ffn2_w_down, m_final_norm, v_ffn1_norm, v_ffn1_w_gate, v_ffn1_w_up, v_ffn1_w_down, v_mix_norm, v_w_in, v_conv_w, v_attn_sinks, v_w_out, v_ffn2_norm, v_ffn2_w_gate, v_ffn2_w_up, v_ffn2_w_down, v_final_norm):
    ix, iy, ic = lax.axis_index("x"), lax.axis_index("y"), lax.axis_index("c")
    my_index = 4 * ix + 2 * iy + ic
    core = ic.astype(jnp.int32).reshape(1)
    chip = (2 * ix + iy).astype(jnp.int32).reshape(1)

    given = dict(ffn1_norm=ffn1_norm, ffn1_w_gate=ffn1_w_gate, ffn1_w_up=ffn1_w_up, ffn1_w_down=ffn1_w_down,
                 mix_norm=mix_norm, w_in=w_in, conv_w=conv_w, attn_sinks=attn_sinks, w_out=w_out, ffn2_norm=ffn2_norm,
                 ffn2_w_gate=ffn2_w_gate, ffn2_w_up=ffn2_w_up, ffn2_w_down=ffn2_w_down, final_norm=final_norm)
    moments_m = dict(ffn1_norm=m_ffn1_norm, ffn1_w_gate=m_ffn1_w_gate, ffn1_w_up=m_ffn1_w_up, ffn1_w_down=m_ffn1_w_down,
                     mix_norm=m_mix_norm, w_in=m_w_in, conv_w=m_conv_w, attn_sinks=m_attn_sinks, w_out=m_w_out,
                     ffn2_norm=m_ffn2_norm, ffn2_w_gate=m_ffn2_w_gate, ffn2_w_up=m_ffn2_w_up, ffn2_w_down=m_ffn2_w_down,
                     final_norm=m_final_norm)
    moments_v = dict(ffn1_norm=v_ffn1_norm, ffn1_w_gate=v_ffn1_w_gate, ffn1_w_up=v_ffn1_w_up, ffn1_w_down=v_ffn1_w_down,
                     mix_norm=v_mix_norm, w_in=v_w_in, conv_w=v_conv_w, attn_sinks=v_attn_sinks, w_out=v_w_out,
                     ffn2_norm=v_ffn2_norm, ffn2_w_gate=v_ffn2_w_gate, ffn2_w_up=v_ffn2_w_up, ffn2_w_down=v_ffn2_w_down,
                     final_norm=v_final_norm)

    xs = x[0]
    target = loss_target[0]
    final_gain = final_norm.reshape(1, D_MODEL)

    def ffn_shard(wg, wu, wd):
        return jnp.stack([wg[0].T, wu[0].T, wd[0]]).astype(BF16)

    conv_shard = jnp.pad(conv_w[0], ((0, 5), (0, 128 - conv_w.shape[2])))
    w1, w2, win_t, wout, conv_all = _all_gather_rows(
        [ffn_shard(ffn1_w_gate, ffn1_w_up, ffn1_w_down), ffn_shard(ffn2_w_gate, ffn2_w_up, ffn2_w_down),
         w_in[0].T.astype(BF16), w_out[0].astype(BF16), conv_shard], name="gather_weights")
    conv_full = conv_all.reshape(N_DEV, 8, 128)[:, :3, :conv_w.shape[2]].transpose(1, 0, 2).reshape(3, CONV_W)

    loss_local, dx0, grads_t, small = _local_step(xs, target, ffn1_norm, mix_norm, ffn2_norm, final_gain, attn_sinks,
                                                  w1, w2, win_t, wout, conv_full)
    loss = lax.psum(loss_local[0, 0], ("x", "y", "c"))
    names = ["ffn1_gate", "ffn1_up", "ffn1_down", "w_in", "w_out", "ffn2_gate", "ffn2_up", "ffn2_down"]

    views = [g.reshape(4, 2, g.shape[0] // N_DEV, D_MODEL) for g in grads_t]
    from_sibling = _exchange_sibling(views, name="grads_to_sibling")
    partials = [_add_sibling(v, r, core, name=f"add_sibling_{nm}", tr=v.shape[2] // 2)
                for v, r, nm in zip(views, from_sibling, names)]
    *from_chips, small_all = _exchange_chips(partials, small, name="grads_to_chips")
    reduced = [_add_chips(p, r, chip, name=f"add_chips_{nm}", tr=p.shape[1] // 2)
               for p, r, nm in zip(partials, from_chips, names)]
    small_sum = _sum_small(small_all, name="sum_small")

    grad, delta, new_m, new_v = _update(given, moments_m, moments_v, reduced, small_sum, my_index)
    order = list(given)
    return (loss, dx0[None], *[grad[n] for n in order], *[delta[n] for n in order],
            *[new_m[n] for n in order], *[new_v[n] for n in order])


def _local_step(xs, target, ffn1_norm, mix_norm, ffn2_norm, final_gain, attn_sinks, w1, w2, win_t, wout, conv_full):
    t = xs.shape[0]
    rope = _rope_tables(t)

    x1, h1, a1, b1 = _ffn_fwd(xs, ffn1_norm, w1, name="ffn1_fwd")
    x2, hm, z, y = _mixer_fwd(x1, mix_norm, win_t, wout, conv_full, attn_sinks, rope, name="mixer_fwd")
    x3, h2, a2, b2 = _ffn_fwd(x2, ffn2_norm, w2, name="ffn2_fwd")
    dx3, loss_local, d_final = _loss_head(x3, final_gain, target, name="loss_head")

    dx2, da2, db2, s2, g2, d_norm2 = _ffn_dgrad(dx3, x2, ffn2_norm, a2, b2, w2, name="ffn2_dgrad")
    dx1, dz, gm, d_conv, d_sink, d_normm = _mixer_bwd(dx2, x1, mix_norm, y, z, win_t, wout, conv_full, attn_sinks,
                                                       rope, name="mixer_bwd")
    dx0, da1, db1, s1, g1, d_norm1 = _ffn_dgrad(dx1, xs, ffn1_norm, a1, b1, w1, name="ffn1_dgrad")

    half_ff = D_FF // 2
    grads_t = [
        _tn_matmul(da1, h1, name="ffn1_wgrad_gate", bm=half_ff), _tn_matmul(db1, h1, name="ffn1_wgrad_up", bm=half_ff),
        _tn_matmul(s1, g1, name="ffn1_wgrad_down", bm=half_ff),
        _tn_matmul(dz, hm, name="mixer_wgrad_in", bm=Z_W // 3), _tn_matmul(y, gm, name="mixer_wgrad_out", bm=D_MODEL // 2),
        _tn_matmul(da2, h2, name="ffn2_wgrad_gate", bm=half_ff), _tn_matmul(db2, h2, name="ffn2_wgrad_up", bm=half_ff),
        _tn_matmul(s2, g2, name="ffn2_wgrad_down", bm=half_ff),
    ]
    small = jnp.concatenate([
        d_norm1, d_normm, d_norm2, d_final,
        jnp.pad(d_conv[0:3], ((0, 0), (0, D_MODEL - CONV_W))), jnp.pad(d_sink, ((0, 0), (0, D_MODEL - 128)))], axis=0)
    return loss_local, dx0, grads_t, small


def _update(given, moments_m, moments_v, reduced, small_sum, my_index):
    g_gate1, g_up1, g_down1, g_in, g_out, g_gate2, g_up2, g_down2 = reduced
    conv_cols = given["conv_w"].shape[2]
    big = {
        "ffn1_w_gate": g_gate1.T, "ffn1_w_up": g_up1.T, "ffn1_w_down": g_down1, "w_in": g_in.T, "w_out": g_out,
        "ffn2_w_gate": g_gate2.T, "ffn2_w_up": g_up2.T, "ffn2_w_down": g_down2,
    }
    small_g = {
        "ffn1_norm": small_sum[0:1], "mix_norm": small_sum[1:2], "ffn2_norm": small_sum[2:3],
        "final_norm": small_sum[3:4],
        "conv_w": lax.dynamic_slice(small_sum[4:7, :CONV_W], (0, my_index * conv_cols), (3, conv_cols)),
        "attn_sinks": small_sum[7:8, :N_Q_HEADS],
    }
    grad, delta, new_m, new_v = {}, {}, {}, {}
    for nm, g in big.items():
        shape = given[nm].shape
        w2d = given[nm][0]
        d, mo, vo = _adamw(w2d, g, moments_m[nm][0], moments_v[nm][0], name=f"adamw_{nm}", tr=w2d.shape[0] // 2)
        grad[nm], delta[nm], new_m[nm], new_v[nm] = (a.reshape(shape) for a in (g, d, mo, vo))

    small_names = ["ffn1_norm", "mix_norm", "ffn2_norm", "final_norm", "conv_w", "attn_sinks"]

    def pack(parts):
        rows = []
        for nm in small_names:
            p = parts[nm]
            p2 = p.reshape(3, conv_cols) if nm == "conv_w" else p.reshape(1, -1)
            rows.append(jnp.pad(p2, ((0, 0), (0, D_MODEL - p2.shape[1]))))
        rows.append(jnp.zeros((8, D_MODEL), F32))
        return jnp.concatenate(rows, axis=0)

    sd, sm, sv = _adamw(pack(given), pack(small_g), pack(moments_m), pack(moments_v), name="adamw_small", tr=16)
    row = 0
    for nm in small_names:
        shape = given[nm].shape
        nrow = 3 if nm == "conv_w" else 1
        ncol = conv_cols if nm == "conv_w" else given[nm].size
        grad[nm] = small_g[nm].reshape(shape)
        delta[nm], new_m[nm], new_v[nm] = (a[row:row + nrow, :ncol].reshape(shape) for a in (sd, sm, sv))
        row += nrow
    return grad, delta, new_m, new_v
```

```python
import functools

import jax
import jax.numpy as jnp
from jax import lax
from jax.experimental import pallas as pl
from jax.experimental.pallas import tpu as pltpu

F32 = jnp.float32
BF16 = jnp.bfloat16
MESH = pl.DeviceIdType.MESH
ANY = pl.BlockSpec(memory_space=pl.ANY)

N_DEV = 8
D_MODEL = 1024
D_FF = 2816
CONV_W = 512
ATTN_W = 512
KV_W = 128
HEAD_DIM = 64
N_Q_HEADS = 8
N_KV_HEADS = 2
Q_PER_KV = N_Q_HEADS // N_KV_HEADS
BLOCK = 128
ROT_DIM = 16
ROPE_THETA = 500000.0
Z_W = 3 * CONV_W + ATTN_W + 2 * KV_W
Q_OFF = 3 * CONV_W
K_OFF = Q_OFF + ATTN_W
V_OFF = K_OFF + KV_W
RMS_EPS = 1e-5
MASK_VALUE = -1e30
SM_SCALE = HEAD_DIM ** -0.5
FFN_RES_SCALE = 0.5

ADAM_LR = 0.001
ADAM_B1 = 0.9
ADAM_B2 = 0.999
ADAM_EPS = 1e-08
ADAM_WD = 0.01
ADAM_STEP = 10

NT_DIMS = (((1,), (1,)), ((), ()))
TN_DIMS = (((0,), (0,)), ((), ()))

VMEM_LIMIT = 56 * 1024 * 1024
FF_CHUNK = 256


def _params(sem, vmem=None):
    return pltpu.CompilerParams(dimension_semantics=sem, vmem_limit_bytes=vmem)


def _rms_stats(xf):
    inv = lax.rsqrt(jnp.mean(xf * xf, axis=-1, keepdims=True) + RMS_EPS)
    return xf * inv, inv


def _rms_bwd(dh, xhat, inv, gain):
    dxhat = dh * gain
    dx = inv * (dxhat - xhat * jnp.mean(dxhat * xhat, axis=-1, keepdims=True))
    dgain = jnp.sum(dh * xhat, axis=0, keepdims=True)
    return dx, dgain


def _load_resident(w_hbm, w_ref, sem):
    @pl.when(pl.program_id(0) == 0)
    def _():
        cp = pltpu.make_async_copy(w_hbm, w_ref, sem)
        cp.start()
        cp.wait()


def _ffn_fwd(x, gain, w3, *, name, tm=256, tf=FF_CHUNK):
    t = x.shape[0]
    tm = min(tm, t)

    def body(x_ref, g_ref, w_hbm, xo_ref, h_ref, a_ref, b_ref, w_ref, s_ref, sem):
        _load_resident(w_hbm, w_ref, sem)
        xf = x_ref[...]
        xhat, _ = _rms_stats(xf)
        h = (xhat * g_ref[...]).astype(BF16)
        h_ref[...] = h
        for c in range(0, D_FF, tf):
            a = lax.dot_general(h, w_ref[0, c:c + tf, :], NT_DIMS, preferred_element_type=F32)
            b = lax.dot_general(h, w_ref[1, c:c + tf, :], NT_DIMS, preferred_element_type=F32)
            a_ref[:, c:c + tf] = a.astype(BF16)
            b_ref[:, c:c + tf] = b.astype(BF16)
            s_ref[:, c:c + tf] = (a * jax.nn.sigmoid(a) * b).astype(BF16)
        xo_ref[...] = xf + FFN_RES_SCALE * jnp.dot(s_ref[...], w_ref[2], preferred_element_type=F32)

    row = pl.BlockSpec((tm, D_MODEL), lambda i: (i, 0))
    hid = pl.BlockSpec((tm, D_FF), lambda i: (i, 0))
    return pl.pallas_call(
        body, name=name, grid=(t // tm,),
        in_specs=[row, pl.BlockSpec((1, D_MODEL), lambda i: (0, 0)), ANY],
        out_specs=[row, row, hid, hid],
        out_shape=[jax.ShapeDtypeStruct((t, D_MODEL), F32), jax.ShapeDtypeStruct((t, D_MODEL), BF16),
                   jax.ShapeDtypeStruct((t, D_FF), BF16), jax.ShapeDtypeStruct((t, D_FF), BF16)],
        scratch_shapes=[pltpu.VMEM((3, D_FF, D_MODEL), BF16), pltpu.VMEM((tm, D_FF), BF16),
                        pltpu.SemaphoreType.DMA(())],
        compiler_params=_params(("arbitrary",), VMEM_LIMIT),
    )(x, gain, w3)


def _ffn_dgrad(dxo, x, gain, a, b, w3, *, name, tm=256, tf=FF_CHUNK):
    t = x.shape[0]
    tm = min(tm, t)

    def body(dxo_ref, x_ref, g_ref, a_ref, b_ref, w_hbm, dxi_ref, da_ref, db_ref, s_ref, gb_ref, dg_ref, w_ref, sem):
        _load_resident(w_hbm, w_ref, sem)

        @pl.when(pl.program_id(0) == 0)
        def _():
            dg_ref[...] = jnp.zeros_like(dg_ref)

        go = dxo_ref[...]
        gb = (FFN_RES_SCALE * go).astype(BF16)
        gb_ref[...] = gb
        for c in range(0, D_FF, tf):
            ds = lax.dot_general(gb, w_ref[2, c:c + tf, :], NT_DIMS, preferred_element_type=F32)
            af = a_ref[:, c:c + tf].astype(F32)
            bf = b_ref[:, c:c + tf].astype(F32)
            sig = jax.nn.sigmoid(af)
            silu = af * sig
            da_ref[:, c:c + tf] = (ds * bf * (sig * (1.0 + af * (1.0 - sig)))).astype(BF16)
            db_ref[:, c:c + tf] = (ds * silu).astype(BF16)
            s_ref[:, c:c + tf] = (silu * bf).astype(BF16)
        dh = (jnp.dot(da_ref[...], w_ref[0], preferred_element_type=F32)
              + jnp.dot(db_ref[...], w_ref[1], preferred_element_type=F32))
        xhat, inv = _rms_stats(x_ref[...])
        dx, dgain = _rms_bwd(dh, xhat, inv, g_ref[...])
        dxi_ref[...] = go + dx
        dg_ref[...] += dgain

    row = pl.BlockSpec((tm, D_MODEL), lambda i: (i, 0))
    hid = pl.BlockSpec((tm, D_FF), lambda i: (i, 0))
    vec = pl.BlockSpec((1, D_MODEL), lambda i: (0, 0))
    return pl.pallas_call(
        body, name=name, grid=(t // tm,),
        in_specs=[row, row, vec, hid, hid, ANY],
        out_specs=[row, hid, hid, hid, row, vec],
        out_shape=[jax.ShapeDtypeStruct((t, D_MODEL), F32), jax.ShapeDtypeStruct((t, D_FF), BF16),
                   jax.ShapeDtypeStruct((t, D_FF), BF16), jax.ShapeDtypeStruct((t, D_FF), BF16),
                   jax.ShapeDtypeStruct((t, D_MODEL), BF16), jax.ShapeDtypeStruct((1, D_MODEL), F32)],
        scratch_shapes=[pltpu.VMEM((3, D_FF, D_MODEL), BF16), pltpu.SemaphoreType.DMA(())],
        compiler_params=_params(("arbitrary",), VMEM_LIMIT),
    )(dxo, x, gain, a, b, w3)


def _tn_matmul(a, b, *, name, bm, tk=1024):
    t, m = a.shape
    n = b.shape[1]
    tk = min(tk, t)
    nk = t // tk

    def body(a_ref, b_ref, o_ref):
        k = pl.program_id(1)
        p = lax.dot_general(a_ref[...], b_ref[...], TN_DIMS, preferred_element_type=F32)

        @pl.when(k == 0)
        def _():
            o_ref[...] = p

        @pl.when(k > 0)
        def _():
            o_ref[...] += p

    return pl.pallas_call(
        body, name=name, grid=(m // bm, nk),
        in_specs=[pl.BlockSpec((tk, bm), lambda i, k: (k, i)), pl.BlockSpec((tk, n), lambda i, k: (k, 0))],
        out_specs=pl.BlockSpec((bm, n), lambda i, k: (i, 0)),
        out_shape=jax.ShapeDtypeStruct((m, n), F32),
        compiler_params=_params(("parallel", "arbitrary"), VMEM_LIMIT),
    )(a, b)


def _loss_head(x, gain, target, *, name, tm=512):
    t = x.shape[0]
    tm = min(tm, t)

    def body(x_ref, g_ref, t_ref, dx_ref, loss_ref, dg_ref):
        @pl.when(pl.program_id(0) == 0)
        def _():
            loss_ref[...] = jnp.zeros_like(loss_ref)
            dg_ref[...] = jnp.zeros_like(dg_ref)

        xhat, inv = _rms_stats(x_ref[...])
        err = xhat * g_ref[...] - t_ref[...]
        loss_ref[...] += 0.5 * jnp.sum(jnp.mean(err * err, axis=-1, keepdims=True), axis=0, keepdims=True)
        dx, dgain = _rms_bwd(err * (1.0 / D_MODEL), xhat, inv, g_ref[...])
        dx_ref[...] = dx
        dg_ref[...] += dgain

    row = pl.BlockSpec((tm, D_MODEL), lambda i: (i, 0))
    vec = pl.BlockSpec((1, D_MODEL), lambda i: (0, 0))
    return pl.pallas_call(
        body, name=name, grid=(t // tm,),
        in_specs=[row, vec, row],
        out_specs=[row, pl.BlockSpec((1, 1), lambda i: (0, 0)), vec],
        out_shape=[jax.ShapeDtypeStruct((t, D_MODEL), F32), jax.ShapeDtypeStruct((1, 1), F32),
                   jax.ShapeDtypeStruct((1, D_MODEL), F32)],
        compiler_params=_params(("arbitrary",)),
    )(x, gain, target)


def _rope_tables(t):
    half = ROT_DIM // 2
    inv_freq = ROPE_THETA ** (-jnp.arange(0, ROT_DIM, 2, dtype=F32) / ROT_DIM)
    ang = jnp.arange(t, dtype=F32)[:, None] * inv_freq[None, :]
    cos, sin = jnp.cos(ang), jnp.sin(ang)
    zeros, ones = jnp.zeros((t, half), F32), jnp.ones((t, HEAD_DIM - ROT_DIM), F32)
    pad = jnp.zeros((t, HEAD_DIM - ROT_DIM), F32)
    mult = jnp.concatenate([cos, cos, ones], axis=1)
    from_lo = jnp.concatenate([zeros, sin, pad], axis=1)
    from_hi = jnp.concatenate([-sin, zeros, pad], axis=1)
    return jnp.stack([jnp.tile(m, (1, 2)) for m in (mult, from_lo, from_hi)])


def _tile_lanes(tab, width):
    return jnp.tile(tab, (1, width // tab.shape[1]))


def _rope(v, tab):
    w = v.shape[1]
    half_rot = ROT_DIM // 2
    return (v * _tile_lanes(tab[0], w)
            + pltpu.roll(v, half_rot, axis=1) * _tile_lanes(tab[1], w)
            + pltpu.roll(v, w - half_rot, axis=1) * _tile_lanes(tab[2], w))


def _rope_bwd(dv, tab):
    w = dv.shape[1]
    half_rot = ROT_DIM // 2
    return (dv * _tile_lanes(tab[0], w)
            + pltpu.roll(dv * _tile_lanes(tab[1], w), w - half_rot, axis=1)
            + pltpu.roll(dv * _tile_lanes(tab[2], w), half_rot, axis=1))


def _shift_rows(v, prev8_ref, n):
    r = lax.broadcasted_iota(jnp.int32, v.shape, 0)
    rolled = pltpu.roll(v, n, axis=0)
    last = prev8_ref[7:8, :]
    if n == 1:
        return jnp.where(r >= 1, rolled, last)
    return jnp.where(r >= 2, rolled, jnp.where(r == 0, prev8_ref[6:7, :], last))


def _shift_rows_up(v, next8_ref, n):
    rows = v.shape[0]
    r = lax.broadcasted_iota(jnp.int32, v.shape, 0)
    rolled = pltpu.roll(v, rows - n, axis=0)
    first = next8_ref[0:1, :]
    if n == 1:
        return jnp.where(r <= rows - 2, rolled, first)
    return jnp.where(r <= rows - 3, rolled, jnp.where(r == rows - 2, first, next8_ref[1:2, :]))


def _lane_half_mask(shape, half):
    lane = lax.broadcasted_iota(jnp.int32, shape, 1)
    return (lane >= HEAD_DIM) if half else (lane < HEAD_DIM)


def _to_kv_lanes(chunk, head, kv):
    if head % 2 != kv:
        chunk = pltpu.roll(chunk, HEAD_DIM, axis=1)
    return jnp.where(_lane_half_mask(chunk.shape, kv), chunk, 0.0)


def _from_kv_lanes(chunk, head, kv):
    chunk = jnp.where(_lane_half_mask(chunk.shape, kv), chunk, 0.0)
    if head % 2 != kv:
        chunk = pltpu.roll(chunk, HEAD_DIM, axis=1)
    return chunk


def _stack_heads(wide, kv):
    parts = []
    for g in range(Q_PER_KV):
        head = kv * Q_PER_KV + g
        chunk = wide[:, (head // 2) * 128:(head // 2 + 1) * 128]
        parts.append(_to_kv_lanes(chunk, head, kv))
    return jnp.concatenate(parts, axis=0)


def _window_mask(has_prev):
    shape = (Q_PER_KV * BLOCK, 2 * BLOCK)
    qi = lax.broadcasted_iota(jnp.int32, shape, 0) & (BLOCK - 1)
    kj = lax.broadcasted_iota(jnp.int32, shape, 1)
    first_key = BLOCK - has_prev * BLOCK
    in_prev = (kj < BLOCK) & (kj > qi) & (kj >= first_key)
    in_own = (kj >= BLOCK) & ((kj - BLOCK) <= qi)
    return in_prev | in_own


def _sink_column(sink_ref, kv):
    row = lax.broadcasted_iota(jnp.int32, (Q_PER_KV * BLOCK, 1), 0)
    col = jnp.full((Q_PER_KV * BLOCK, 1), sink_ref[0, kv * Q_PER_KV], F32)
    for g in range(1, Q_PER_KV):
        col = jnp.where(row >= g * BLOCK, sink_ref[0, kv * Q_PER_KV + g], col)
    return col


def _softmax_with_sink(q4, k2, mask, sink):
    s = lax.dot_general(q4, k2, NT_DIMS, preferred_element_type=F32) * SM_SCALE
    s = jnp.where(mask, s, MASK_VALUE)
    m = jnp.maximum(jnp.max(s, axis=-1, keepdims=True), sink)
    p = jnp.exp(s - m)
    e_sink = jnp.exp(sink - m)
    inv_den = 1.0 / (jnp.sum(p, axis=-1, keepdims=True) + e_sink)
    return p * inv_den, e_sink * inv_den


def _conv_terms(zf, prev8_ref, w_ref):
    b_gate, c_gate, u = zf[:, 0:CONV_W], zf[:, CONV_W:2 * CONV_W], zf[:, 2 * CONV_W:3 * CONV_W]
    vc = c_gate * u
    vm1 = _shift_rows(vc, prev8_ref, 1)
    vm2 = _shift_rows(vc, prev8_ref, 2)
    conv = w_ref[0:1, :] * vm2 + w_ref[1:2, :] * vm1 + w_ref[2:3, :] * vc
    return b_gate, c_gate, u, vc, vm1, vm2, conv


def _mixer_fwd(x, gain, win_t, wout, conv_w, sinks, rope, *, name):
    t = x.shape[0]
    nb = t // BLOCK

    def body(x_ref, g_ref, win_ref, wout_ref, cw_ref, sink_ref, rope_ref,
             xo_ref, h_ref, z_ref, y_ref, kprev_ref, vprev_ref, cprev_ref):
        i = pl.program_id(0)

        @pl.when(i == 0)
        def _():
            kprev_ref[...] = jnp.zeros_like(kprev_ref)
            vprev_ref[...] = jnp.zeros_like(vprev_ref)
            cprev_ref[...] = jnp.zeros_like(cprev_ref)

        xf = x_ref[...]
        xhat, _ = _rms_stats(xf)
        h = (xhat * g_ref[...]).astype(BF16)
        h_ref[...] = h
        zb = lax.dot_general(h, win_ref[...], NT_DIMS, preferred_element_type=F32).astype(BF16)
        z_ref[...] = zb
        zf = zb.astype(F32)

        b_gate, _, _, vc, _, _, conv = _conv_terms(zf, cprev_ref, cw_ref)
        y_conv = b_gate * conv
        cprev_ref[...] = vc[BLOCK - 8:BLOCK, :]

        tab = rope_ref[...]
        qr = _rope(zf[:, Q_OFF:K_OFF], tab)
        kr = _rope(zf[:, K_OFF:V_OFF], tab).astype(BF16)
        vb = zb[:, V_OFF:Z_W]
        k2 = jnp.concatenate([kprev_ref[...], kr], axis=0)
        v2 = jnp.concatenate([vprev_ref[...], vb], axis=0)
        kprev_ref[...] = kr
        vprev_ref[...] = vb

        mask = _window_mask(jnp.minimum(i, 1))
        chunks = [jnp.zeros((BLOCK, 128), F32) for _ in range(ATTN_W // 128)]
        for kv in range(N_KV_HEADS):
            q4 = _stack_heads(qr, kv).astype(BF16)
            probs, _ = _softmax_with_sink(q4, k2, mask, _sink_column(sink_ref, kv))
            o4 = jnp.dot(probs.astype(BF16), v2, preferred_element_type=F32)
            for g in range(Q_PER_KV):
                head = kv * Q_PER_KV + g
                chunks[head // 2] += _from_kv_lanes(o4[g * BLOCK:(g + 1) * BLOCK], head, kv)
        y = jnp.concatenate([y_conv] + chunks, axis=1).astype(BF16)
        y_ref[...] = y
        xo_ref[...] = xf + jnp.dot(y, wout_ref[...], preferred_element_type=F32)

    row = pl.BlockSpec((BLOCK, D_MODEL), lambda i: (i, 0))
    full = lambda shape: pl.BlockSpec(shape, lambda i: (0,) * len(shape))
    return pl.pallas_call(
        body, name=name, grid=(nb,),
        in_specs=[row, full((1, D_MODEL)), full((Z_W, D_MODEL)), full((D_MODEL, D_MODEL)), full((3, CONV_W)),
                  pl.BlockSpec(memory_space=pltpu.SMEM), pl.BlockSpec((3, BLOCK, 128), lambda i: (0, i, 0))],
        out_specs=[row, row, pl.BlockSpec((BLOCK, Z_W), lambda i: (i, 0)), row],
        out_shape=[jax.ShapeDtypeStruct((t, D_MODEL), F32), jax.ShapeDtypeStruct((t, D_MODEL), BF16),
                   jax.ShapeDtypeStruct((t, Z_W), BF16), jax.ShapeDtypeStruct((t, D_MODEL), BF16)],
        scratch_shapes=[pltpu.VMEM((BLOCK, KV_W), BF16), pltpu.VMEM((BLOCK, KV_W), BF16),
                        pltpu.VMEM((8, CONV_W), F32)],
        compiler_params=_params(("arbitrary",), VMEM_LIMIT),
    )(x, gain, win_t, wout, conv_w, sinks, rope)


def _mixer_bwd(dxo, x, gain, y, z, win_t, wout, conv_w, sinks, rope, *, name):
    t = x.shape[0]
    nb = t // BLOCK

    def body(dxo_ref, x_ref, g_ref, y_ref, z_ref, zp_ref, win_ref, wout_ref, cw_ref, sink_ref, rope_ref, ropep_ref,
             dxi_ref, dz_ref, gb_ref, dcw_ref, dsink_ref, dg_ref, dk_ref, dv_ref, dcn_ref, pvc_ref):
        i = pl.program_id(0)
        blk = nb - 1 - i

        @pl.when(i == 0)
        def _():
            dk_ref[...] = jnp.zeros_like(dk_ref)
            dv_ref[...] = jnp.zeros_like(dv_ref)
            dcn_ref[...] = jnp.zeros_like(dcn_ref)
            dcw_ref[...] = jnp.zeros_like(dcw_ref)
            dsink_ref[...] = jnp.zeros_like(dsink_ref)
            dg_ref[...] = jnp.zeros_like(dg_ref)

        has_prev = jnp.minimum(blk, 1)
        go = dxo_ref[...]
        gb = go.astype(BF16)
        gb_ref[...] = gb
        dy = lax.dot_general(gb, wout_ref[...], NT_DIMS, preferred_element_type=F32)
        dy_conv, dy_attn = dy[:, 0:CONV_W], dy[:, CONV_W:D_MODEL]
        zb, zpb = z_ref[...], zp_ref[...]
        zf = zb.astype(F32)
        zpf = zpb.astype(F32) * has_prev.astype(F32)

        pvc_ref[...] = (zpf[:, CONV_W:2 * CONV_W] * zpf[:, 2 * CONV_W:3 * CONV_W])[BLOCK - 8:BLOCK, :]
        b_gate, c_gate, u, vc, vm1, vm2, conv = _conv_terms(zf, pvc_ref, cw_ref)
        d_bgate = dy_conv * conv
        dc = dy_conv * b_gate
        tap = lax.broadcasted_iota(jnp.int32, (8, CONV_W), 0)
        dcw_ref[...] += jnp.where(tap == 0, jnp.sum(dc * vm2, axis=0, keepdims=True),
                                  jnp.where(tap == 1, jnp.sum(dc * vm1, axis=0, keepdims=True),
                                            jnp.where(tap == 2, jnp.sum(dc * vc, axis=0, keepdims=True), 0.0)))
        dvc = (cw_ref[2:3, :] * dc + cw_ref[1:2, :] * _shift_rows_up(dc, dcn_ref, 1)
               + cw_ref[0:1, :] * _shift_rows_up(dc, dcn_ref, 2))
        dcn_ref[...] = dc[0:8, :]
        d_cgate = dvc * u
        d_u = dvc * c_gate

        tab, tabp = rope_ref[...], ropep_ref[...]
        qr = _rope(zf[:, Q_OFF:K_OFF], tab)
        kr = _rope(zf[:, K_OFF:V_OFF], tab).astype(BF16)
        kpr = _rope(zpf[:, K_OFF:V_OFF], tabp).astype(BF16)
        k2 = jnp.concatenate([kpr, kr], axis=0)
        v2 = jnp.concatenate([zpb[:, V_OFF:Z_W], zb[:, V_OFF:Z_W]], axis=0)
        out = y_ref[:, CONV_W:D_MODEL].astype(F32)
        do_out = dy_attn * out
        mask = _window_mask(has_prev)
        dk2 = jnp.zeros((2 * BLOCK, KV_W), F32)
        dv2 = jnp.zeros((2 * BLOCK, KV_W), F32)
        dq_chunks = [jnp.zeros((BLOCK, 128), F32) for _ in range(ATTN_W // 128)]
        lane = lax.broadcasted_iota(jnp.int32, (1, 128), 1)
        dsink = jnp.zeros((1, 128), F32)
        for kv in range(N_KV_HEADS):
            q4 = _stack_heads(qr, kv).astype(BF16)
            do4 = _stack_heads(dy_attn, kv).astype(BF16)
            delta = jnp.sum(_stack_heads(do_out, kv), axis=-1, keepdims=True)
            probs, p_sink = _softmax_with_sink(q4, k2, mask, _sink_column(sink_ref, kv))
            dp = lax.dot_general(do4, v2, NT_DIMS, preferred_element_type=F32)
            ds = (probs * (dp - delta) * SM_SCALE).astype(BF16)
            dq4 = jnp.dot(ds, k2, preferred_element_type=F32)
            dk2 += lax.dot_general(ds, q4, TN_DIMS, preferred_element_type=F32)
            dv2 += lax.dot_general(probs.astype(BF16), do4, TN_DIMS, preferred_element_type=F32)
            sink_terms = p_sink * delta
            for g in range(Q_PER_KV):
                head = kv * Q_PER_KV + g
                rows = slice(g * BLOCK, (g + 1) * BLOCK)
                dq_chunks[head // 2] += _from_kv_lanes(dq4[rows], head, kv)
                dsink = dsink - jnp.where(lane == head, jnp.sum(sink_terms[rows], axis=0, keepdims=True), 0.0)
        dsink_ref[...] += dsink
        dq = _rope_bwd(jnp.concatenate(dq_chunks, axis=1), tab)
        dk = _rope_bwd(dk2[BLOCK:] + dk_ref[...], tab)
        dv = dv2[BLOCK:] + dv_ref[...]
        dk_ref[...] = dk2[:BLOCK]
        dv_ref[...] = dv2[:BLOCK]

        dzb = jnp.concatenate([d_bgate, d_cgate, d_u, dq, dk, dv], axis=1).astype(BF16)
        dz_ref[...] = dzb
        dh = jnp.dot(dzb, win_ref[...], preferred_element_type=F32)
        xhat, inv = _rms_stats(x_ref[...])
        dx, dgain = _rms_bwd(dh, xhat, inv, g_ref[...])
        dxi_ref[...] = go + dx
        dg_ref[...] += dgain

    rev = lambda i: (nb - 1 - i, 0)
    rev_prev = lambda i: (jnp.maximum(nb - 2 - i, 0), 0)
    row = pl.BlockSpec((BLOCK, D_MODEL), rev)
    full = lambda shape: pl.BlockSpec(shape, lambda i: (0,) * len(shape))
    return pl.pallas_call(
        body, name=name, grid=(nb,),
        in_specs=[row, row, full((1, D_MODEL)), row,
                  pl.BlockSpec((BLOCK, Z_W), rev), pl.BlockSpec((BLOCK, Z_W), rev_prev),
                  full((Z_W, D_MODEL)), full((D_MODEL, D_MODEL)), full((3, CONV_W)),
                  pl.BlockSpec(memory_space=pltpu.SMEM),
                  pl.BlockSpec((3, BLOCK, 128), lambda i: (0, nb - 1 - i, 0)),
                  pl.BlockSpec((3, BLOCK, 128), lambda i: (0, jnp.maximum(nb - 2 - i, 0), 0))],
        out_specs=[row, pl.BlockSpec((BLOCK, Z_W), rev), row, full((8, CONV_W)), full((1, 128)), full((1, D_MODEL))],
        out_shape=[jax.ShapeDtypeStruct((t, D_MODEL), F32), jax.ShapeDtypeStruct((t, Z_W), BF16),
                   jax.ShapeDtypeStruct((t, D_MODEL), BF16), jax.ShapeDtypeStruct((8, CONV_W), F32),
                   jax.ShapeDtypeStruct((1, 128), F32), jax.ShapeDtypeStruct((1, D_MODEL), F32)],
        scratch_shapes=[pltpu.VMEM((BLOCK, KV_W), F32), pltpu.VMEM((BLOCK, KV_W), F32), pltpu.VMEM((8, CONV_W), F32),
                        pltpu.VMEM((8, CONV_W), F32)],
        compiler_params=_params(("arbitrary",), VMEM_LIMIT),
    )(dxo, x, gain, y, z, z, win_t, wout, conv_w, sinks, rope, rope)


def _place():
    x, y, c = lax.axis_index("x"), lax.axis_index("y"), lax.axis_index("c")
    other_chips = [(1 - x, y), (x, 1 - y), (1 - x, 1 - y)]
    return x, y, c, other_chips


def _all_gather_rows(shards, *, name):
    n = len(shards)

    def body(*refs):
        srcs, outs = refs[:n], refs[n:2 * n]
        send_sems, recv_sems, local_sems = refs[2 * n:]
        x, y, c, chips = _place()
        me, sibling = (x, y, c), (x, y, 1 - c)

        def rows(t, px, py, pc):
            r = srcs[t].shape[-2]
            start = pl.multiple_of((4 * px + 2 * py + pc) * r, 16 if r % 16 == 0 else 8)
            if len(srcs[t].shape) == 3:
                return outs[t].at[:, pl.ds(start, r), :]
            return outs[t].at[pl.ds(start, r), :]

        def copy(t, k, block, to, own=False):
            return pltpu.make_async_remote_copy(
                src_ref=srcs[t] if own else rows(t, *block), dst_ref=rows(t, *block),
                send_sem=send_sems.at[t, k], recv_sem=recv_sems.at[t, k], device_id=to, device_id_type=MESH)

        mine = [pltpu.make_async_copy(srcs[t], rows(t, *me), local_sems.at[t]) for t in range(n)]
        first = []
        for t in range(n):
            mine[t].start()
            first.append(copy(t, 0, me, sibling, own=True))
            first += [copy(t, 1 + j, me, (*chip, c), own=True) for j, chip in enumerate(chips)]
        for cp in first:
            cp.start()
        passed = []
        for j, chip in enumerate(chips):
            for t in range(n):
                copy(t, 1 + j, (*chip, c), me).wait_recv()
                fwd = copy(t, 4 + j, (*chip, c), sibling)
                fwd.start()
                passed.append(fwd)
        for t in range(n):
            copy(t, 0, sibling, me).wait_recv()
            for j, chip in enumerate(chips):
                copy(t, 4 + j, (*chip, 1 - c), me).wait_recv()
        for cp in first + passed:
            cp.wait_send()
        for cp in mine:
            cp.wait()

    out_shape = [jax.ShapeDtypeStruct(s.shape[:-2] + (N_DEV * s.shape[-2], s.shape[-1]), s.dtype) for s in shards]
    return pl.pallas_call(
        body, name=name, in_specs=[ANY] * n, out_specs=[ANY] * n, out_shape=out_shape,
        scratch_shapes=[pltpu.SemaphoreType.DMA((n, 7)), pltpu.SemaphoreType.DMA((n, 7)),
                        pltpu.SemaphoreType.DMA((n,))],
    )(*shards)


def _exchange_sibling(grads, *, name):
    n = len(grads)

    def body(*refs):
        srcs, outs = refs[:n], refs[n:2 * n]
        send_sems, recv_sems = refs[2 * n:]
        x, y, c, _ = _place()
        copies = [pltpu.make_async_remote_copy(
            src_ref=srcs[t].at[:, 1 - c], dst_ref=outs[t], send_sem=send_sems.at[t], recv_sem=recv_sems.at[t],
            device_id=(x, y, 1 - c), device_id_type=MESH) for t in range(n)]
        for cp in copies:
            cp.start()
        for cp in copies:
            cp.wait()

    out_shape = [jax.ShapeDtypeStruct((4,) + g.shape[2:], g.dtype) for g in grads]
    return pl.pallas_call(
        body, name=name, in_specs=[ANY] * n, out_specs=[ANY] * n, out_shape=out_shape,
        scratch_shapes=[pltpu.SemaphoreType.DMA((n,)), pltpu.SemaphoreType.DMA((n,))],
    )(*grads)


def _exchange_chips(partials, small, *, name):
    n = len(partials)

    def body(*refs):
        srcs, small_ref = refs[:n], refs[n]
        outs, small_out = refs[n + 1:2 * n + 1], refs[2 * n + 1]
        send_sems, recv_sems, small_send, small_recv, local_sem = refs[2 * n + 2:]
        x, y, c, chips = _place()
        my_index = 4 * x + 2 * y + c
        copies = []
        for t in range(n):
            for j, (px, py) in enumerate(chips):
                copies.append(pltpu.make_async_remote_copy(
                    src_ref=srcs[t].at[2 * px + py], dst_ref=outs[t].at[j],
                    send_sem=send_sems.at[t, j], recv_sem=recv_sems.at[t, j],
                    device_id=(px, py, c), device_id_type=MESH))
        flips = [(fx, fy, fc) for fx in range(2) for fy in range(2) for fc in range(2)][1:]
        for k, (fx, fy, fc) in enumerate(flips):
            peer = (x + fx - 2 * x * fx, y + fy - 2 * y * fy, c + fc - 2 * c * fc)
            copies.append(pltpu.make_async_remote_copy(
                src_ref=small_ref, dst_ref=small_out.at[my_index],
                send_sem=small_send.at[k], recv_sem=small_recv.at[k], device_id=peer, device_id_type=MESH))
        own = pltpu.make_async_copy(small_ref, small_out.at[my_index], local_sem)
        own.start()
        for cp in copies:
            cp.start()
        for cp in copies:
            cp.wait()
        own.wait()

    out_shape = ([jax.ShapeDtypeStruct((3,) + p.shape[1:], p.dtype) for p in partials]
                 + [jax.ShapeDtypeStruct((N_DEV,) + small.shape, small.dtype)])
    return pl.pallas_call(
        body, name=name, in_specs=[ANY] * (n + 1), out_specs=[ANY] * (n + 1), out_shape=out_shape,
        scratch_shapes=[pltpu.SemaphoreType.DMA((n, 3)), pltpu.SemaphoreType.DMA((n, 3)),
                        pltpu.SemaphoreType.DMA((7,)), pltpu.SemaphoreType.DMA((7,)), pltpu.SemaphoreType.DMA(())],
    )(*partials, small)


def _add_sibling(grad, recv, core, *, name, tr):
    rows = grad.shape[2]

    def body(core_ref, g_ref, r_ref, o_ref, ob_ref):
        p = g_ref[:, 0] + r_ref[...]
        o_ref[...] = p
        ob_ref[...] = p.astype(BF16)

    out = pl.BlockSpec((4, tr, D_MODEL), lambda i, core_ref: (0, i, 0))
    return pl.pallas_call(
        body, name=name,
        grid_spec=pltpu.PrefetchScalarGridSpec(
            num_scalar_prefetch=1, grid=(rows // tr,),
            in_specs=[pl.BlockSpec((4, 1, tr, D_MODEL), lambda i, core_ref: (0, core_ref[0], i, 0)), out],
            out_specs=[out, out]),
        out_shape=[jax.ShapeDtypeStruct(recv.shape, F32), jax.ShapeDtypeStruct(recv.shape, BF16)],
        compiler_params=_params(("arbitrary",)),
    )(core, grad, recv)


def _add_chips(partial, recv, chip, *, name, tr):
    rows = partial.shape[1]

    def body(chip_ref, p_ref, r_ref, o_ref):
        o_ref[...] = p_ref[0] + r_ref[0].astype(F32) + r_ref[1].astype(F32) + r_ref[2].astype(F32)

    return pl.pallas_call(
        body, name=name,
        grid_spec=pltpu.PrefetchScalarGridSpec(
            num_scalar_prefetch=1, grid=(rows // tr,),
            in_specs=[pl.BlockSpec((1, tr, D_MODEL), lambda i, chip_ref: (chip_ref[0], i, 0)),
                      pl.BlockSpec((3, tr, D_MODEL), lambda i, chip_ref: (0, i, 0))],
            out_specs=pl.BlockSpec((tr, D_MODEL), lambda i, chip_ref: (i, 0))),
        out_shape=jax.ShapeDtypeStruct((rows, D_MODEL), F32),
        compiler_params=_params(("arbitrary",)),
    )(chip, partial, recv)


def _adamw_math(w, g, m, v):
    m = ADAM_B1 * m + (1.0 - ADAM_B1) * g
    v = ADAM_B2 * v + (1.0 - ADAM_B2) * (g * g)
    m_hat = m / (1.0 - ADAM_B1 ** ADAM_STEP)
    v_hat = v / (1.0 - ADAM_B2 ** ADAM_STEP)
    delta = -ADAM_LR * (m_hat / (jnp.sqrt(v_hat) + ADAM_EPS) + ADAM_WD * w)
    return delta, m, v


def _adamw(w, g, m, v, *, name, tr):
    rows, cols = w.shape

    def body(w_ref, g_ref, m_ref, v_ref, d_ref, mo_ref, vo_ref):
        d_ref[...], mo_ref[...], vo_ref[...] = _adamw_math(w_ref[...], g_ref[...], m_ref[...], v_ref[...])

    spec = pl.BlockSpec((tr, cols), lambda i: (i, 0))
    return pl.pallas_call(
        body, name=name, grid=(rows // tr,), in_specs=[spec] * 4, out_specs=[spec] * 3,
        out_shape=[jax.ShapeDtypeStruct(w.shape, F32)] * 3,
        compiler_params=_params(("parallel",)),
    )(w, g, m, v)


def _sum_small(gathered, *, name):
    def body(g_ref, o_ref):
        acc = g_ref[0]
        for k in range(1, N_DEV):
            acc = acc + g_ref[k]
        o_ref[...] = acc

    return pl.pallas_call(body, name=name, out_shape=jax.ShapeDtypeStruct(gathered.shape[1:], F32))(gathered)


def kernel(x, ffn1_norm, ffn1_w_gate, ffn1_w_up, ffn1_w_down, mix_norm, w_in, conv_w, attn_sinks, w_out, ffn2_norm, ffn2_w_gate, ffn2_w_up, ffn2_w_down, final_norm, loss_target, m_ffn1_norm, m_ffn1_w_gate, m_ffn1_w_up, m_ffn1_w_down, m_mix_norm, m_w_in, m_conv_w, m_attn_sinks, m_w_out, m_ffn2_norm, m_ffn2_w_gate, m_ffn2_w_up, m_ffn2_w_down, m_final_norm, v_ffn1_norm, v_ffn1_w_gate, v_ffn1_w_up, v_ffn1_w_down, v_mix_norm, v_w_in, v_conv_w, v_attn_sinks, v_w_out, v_ffn2_norm, v_ffn2_w_gate, v_ffn2_w_up, v_ffn2_w_down, v_final_norm):
    ix, iy, ic = lax.axis_index("x"), lax.axis_index("y"), lax.axis_index("c")
    my_index = 4 * ix + 2 * iy + ic
    core = ic.astype(jnp.int32).reshape(1)
    chip = (2 * ix + iy).astype(jnp.int32).reshape(1)

    given = dict(ffn1_norm=ffn1_norm, ffn1_w_gate=ffn1_w_gate, ffn1_w_up=ffn1_w_up, ffn1_w_down=ffn1_w_down,
                 mix_norm=mix_norm, w_in=w_in, conv_w=conv_w, attn_sinks=attn_sinks, w_out=w_out, ffn2_norm=ffn2_norm,
                 ffn2_w_gate=ffn2_w_gate, ffn2_w_up=ffn2_w_up, ffn2_w_down=ffn2_w_down, final_norm=final_norm)
    moments_m = dict(ffn1_norm=m_ffn1_norm, ffn1_w_gate=m_ffn1_w_gate, ffn1_w_up=m_ffn1_w_up, ffn1_w_down=m_ffn1_w_down,
                     mix_norm=m_mix_norm, w_in=m_w_in, conv_w=m_conv_w, attn_sinks=m_attn_sinks, w_out=m_w_out,
                     ffn2_norm=m_ffn2_norm, ffn2_w_gate=m_ffn2_w_gate, ffn2_w_up=m_ffn2_w_up, ffn2_w_down=m_ffn2_w_down,
                     final_norm=m_final_norm)
    moments_v = dict(ffn1_norm=v_ffn1_norm, ffn1_w_gate=v_ffn1_w_gate, ffn1_w_up=v_ffn1_w_up, ffn1_w_down=v_ffn1_w_down,
                     mix_norm=v_mix_norm, w_in=v_w_in, conv_w=v_conv_w, attn_sinks=v_attn_sinks, w_out=v_w_out,
                     ffn2_norm=v_ffn2_norm, ffn2_w_gate=v_ffn2_w_gate, ffn2_w_up=v_ffn2_w_up, ffn2_w_down=v_ffn2_w_down,
                     final_norm=v_final_norm)

    xs = x[0]
    target = loss_target[0]
    final_gain = final_norm.reshape(1, D_MODEL)

    def ffn_shard(wg, wu, wd):
        return jnp.stack([wg[0].T, wu[0].T, wd[0]]).astype(BF16)

    conv_shard = jnp.pad(conv_w[0], ((0, 5), (0, 128 - conv_w.shape[2])))
    w1, w2, win_t, wout, conv_all = _all_gather_rows(
        [ffn_shard(ffn1_w_gate, ffn1_w_up, ffn1_w_down), ffn_shard(ffn2_w_gate, ffn2_w_up, ffn2_w_down),
         w_in[0].T.astype(BF16), w_out[0].astype(BF16), conv_shard], name="gather_weights")
    conv_full = conv_all.reshape(N_DEV, 8, 128)[:, :3, :conv_w.shape[2]].transpose(1, 0, 2).reshape(3, CONV_W)

    loss_local, dx0, grads_t, small = _local_step(xs, target, ffn1_norm, mix_norm, ffn2_norm, final_gain, attn_sinks,
                                                  w1, w2, win_t, wout, conv_full)
    loss = lax.psum(loss_local[0, 0], ("x", "y", "c"))
    names = ["ffn1_gate", "ffn1_up", "ffn1_down", "w_in", "w_out", "ffn2_gate", "ffn2_up", "ffn2_down"]

    views = [g.reshape(4, 2, g.shape[0] // N_DEV, D_MODEL) for g in grads_t]
    from_sibling = _exchange_sibling(views, name="grads_to_sibling")
    partials = [_add_sibling(v, r, core, name=f"add_sibling_{nm}", tr=v.shape[2] // 2)
                for v, r, nm in zip(views, from_sibling, names)]
    *from_chips, small_all = _exchange_chips([p16 for _, p16 in partials], small, name="grads_to_chips")
    reduced = [_add_chips(p32, r, chip, name=f"add_chips_{nm}", tr=p32.shape[1] // 2)
               for (p32, _), r, nm in zip(partials, from_chips, names)]
    small_sum = _sum_small(small_all, name="sum_small")

    grad, delta, new_m, new_v = _update(given, moments_m, moments_v, reduced, small_sum, my_index)
    order = list(given)
    return (loss, dx0[None], *[grad[n] for n in order], *[delta[n] for n in order],
            *[new_m[n] for n in order], *[new_v[n] for n in order])


def _local_step(xs, target, ffn1_norm, mix_norm, ffn2_norm, final_gain, attn_sinks, w1, w2, win_t, wout, conv_full):
    t = xs.shape[0]
    rope = _rope_tables(t)

    x1, h1, a1, b1 = _ffn_fwd(xs, ffn1_norm, w1, name="ffn1_fwd")
    x2, hm, z, y = _mixer_fwd(x1, mix_norm, win_t, wout, conv_full, attn_sinks, rope, name="mixer_fwd")
    x3, h2, a2, b2 = _ffn_fwd(x2, ffn2_norm, w2, name="ffn2_fwd")
    dx3, loss_local, d_final = _loss_head(x3, final_gain, target, name="loss_head")

    dx2, da2, db2, s2, g2, d_norm2 = _ffn_dgrad(dx3, x2, ffn2_norm, a2, b2, w2, name="ffn2_dgrad")
    dx1, dz, gm, d_conv, d_sink, d_normm = _mixer_bwd(dx2, x1, mix_norm, y, z, win_t, wout, conv_full, attn_sinks,
                                                       rope, name="mixer_bwd")
    dx0, da1, db1, s1, g1, d_norm1 = _ffn_dgrad(dx1, xs, ffn1_norm, a1, b1, w1, name="ffn1_dgrad")

    half_ff = D_FF // 2
    grads_t = [
        _tn_matmul(da1, h1, name="ffn1_wgrad_gate", bm=half_ff), _tn_matmul(db1, h1, name="ffn1_wgrad_up", bm=half_ff),
        _tn_matmul(s1, g1, name="ffn1_wgrad_down", bm=half_ff),
        _tn_matmul(dz, hm, name="mixer_wgrad_in", bm=Z_W // 3), _tn_matmul(y, gm, name="mixer_wgrad_out", bm=D_MODEL // 2),
        _tn_matmul(da2, h2, name="ffn2_wgrad_gate", bm=half_ff), _tn_matmul(db2, h2, name="ffn2_wgrad_up", bm=half_ff),
        _tn_matmul(s2, g2, name="ffn2_wgrad_down", bm=half_ff),
    ]
    small = jnp.concatenate([
        d_norm1, d_normm, d_norm2, d_final,
        jnp.pad(d_conv[0:3], ((0, 0), (0, D_MODEL - CONV_W))), jnp.pad(d_sink, ((0, 0), (0, D_MODEL - 128)))], axis=0)
    return loss_local, dx0, grads_t, small


def _update(given, moments_m, moments_v, reduced, small_sum, my_index):
    g_gate1, g_up1, g_down1, g_in, g_out, g_gate2, g_up2, g_down2 = reduced
    conv_cols = given["conv_w"].shape[2]
    big = {
        "ffn1_w_gate": g_gate1.T, "ffn1_w_up": g_up1.T, "ffn1_w_down": g_down1, "w_in": g_in.T, "w_out": g_out,
        "ffn2_w_gate": g_gate2.T, "ffn2_w_up": g_up2.T, "ffn2_w_down": g_down2,
    }
    small_g = {
        "ffn1_norm": small_sum[0:1], "mix_norm": small_sum[1:2], "ffn2_norm": small_sum[2:3],
        "final_norm": small_sum[3:4],
        "conv_w": lax.dynamic_slice(small_sum[4:7, :CONV_W], (0, my_index * conv_cols), (3, conv_cols)),
        "attn_sinks": small_sum[7:8, :N_Q_HEADS],
    }
    grad, delta, new_m, new_v = {}, {}, {}, {}
    for nm, g in big.items():
        shape = given[nm].shape
        w2d = given[nm][0]
        d, mo, vo = _adamw(w2d, g, moments_m[nm][0], moments_v[nm][0], name=f"adamw_{nm}", tr=w2d.shape[0] // 2)
        grad[nm], delta[nm], new_m[nm], new_v[nm] = (a.reshape(shape) for a in (g, d, mo, vo))

    small_names = ["ffn1_norm", "mix_norm", "ffn2_norm", "final_norm", "conv_w", "attn_sinks"]

    def pack(parts):
        rows = []
        for nm in small_names:
            p = parts[nm]
            p2 = p.reshape(3, conv_cols) if nm == "conv_w" else p.reshape(1, -1)
            rows.append(jnp.pad(p2, ((0, 0), (0, D_MODEL - p2.shape[1]))))
        rows.append(jnp.zeros((8, D_MODEL), F32))
        return jnp.concatenate(rows, axis=0)

    sd, sm, sv = _adamw(pack(given), pack(small_g), pack(moments_m), pack(moments_v), name="adamw_small", tr=16)
    row = 0
    for nm in small_names:
        shape = given[nm].shape
        nrow = 3 if nm == "conv_w" else 1
        ncol = conv_cols if nm == "conv_w" else given[nm].size
        grad[nm] = small_g[nm].reshape(shape)
        delta[nm], new_m[nm], new_v[nm] = (a[row:row + nrow, :ncol].reshape(shape) for a in (sd, sm, sv))
        row += nrow
    return grad, delta, new_m, new_v
```

```python
import functools

import jax
import jax.numpy as jnp
from jax import lax
from jax.experimental import pallas as pl
from jax.experimental.pallas import tpu as pltpu

F32 = jnp.float32
BF16 = jnp.bfloat16
MESH = pl.DeviceIdType.MESH
ANY = pl.BlockSpec(memory_space=pl.ANY)
HBM_SPEC = pl.BlockSpec(memory_space=pltpu.HBM)
SEM_SPEC = pl.BlockSpec(memory_space=pltpu.SEMAPHORE)
DATAFLOW = pltpu.SideEffectType.DATAFLOW_SIDE_EFFECTING

N_DEV = 8
D_MODEL = 1024
D_FF = 2816
CONV_W = 512
ATTN_W = 512
KV_W = 128
HEAD_DIM = 64
N_Q_HEADS = 8
N_KV_HEADS = 2
Q_PER_KV = N_Q_HEADS // N_KV_HEADS
BLOCK = 128
ROT_DIM = 16
ROPE_THETA = 500000.0
Z_W = 3 * CONV_W + ATTN_W + 2 * KV_W
Q_OFF = 3 * CONV_W
K_OFF = Q_OFF + ATTN_W
V_OFF = K_OFF + KV_W
RMS_EPS = 1e-5
MASK_VALUE = -1e30
SM_SCALE = HEAD_DIM ** -0.5
FFN_RES_SCALE = 0.5

ADAM_LR = 0.001
ADAM_B1 = 0.9
ADAM_B2 = 0.999
ADAM_EPS = 1e-08
ADAM_WD = 0.01
ADAM_STEP = 10

NT_DIMS = (((1,), (1,)), ((), ()))
TN_DIMS = (((0,), (0,)), ((), ()))

VMEM_LIMIT = 56 * 1024 * 1024
FF_CHUNK = 256


def _params(sem, vmem=None):
    return pltpu.CompilerParams(dimension_semantics=sem, vmem_limit_bytes=vmem)


def _rms_stats(xf):
    inv = lax.rsqrt(jnp.mean(xf * xf, axis=-1, keepdims=True) + RMS_EPS)
    return xf * inv, inv


def _rms_bwd(dh, xhat, inv, gain):
    dxhat = dh * gain
    dx = inv * (dxhat - xhat * jnp.mean(dxhat * xhat, axis=-1, keepdims=True))
    dgain = jnp.sum(dh * xhat, axis=0, keepdims=True)
    return dx, dgain


def _load_resident(w_hbm, w_ref, sem):
    @pl.when(pl.program_id(0) == 0)
    def _():
        cp = pltpu.make_async_copy(w_hbm, w_ref, sem)
        cp.start()
        cp.wait()


def _ffn_fwd(x, gain, w3, *, name, tm=256, tf=FF_CHUNK):
    t = x.shape[0]
    tm = min(tm, t)

    def body(x_ref, g_ref, w_hbm, xo_ref, h_ref, a_ref, b_ref, w_ref, s_ref, sem):
        _load_resident(w_hbm, w_ref, sem)
        xf = x_ref[...]
        xhat, _ = _rms_stats(xf)
        h = (xhat * g_ref[...]).astype(BF16)
        h_ref[...] = h
        for c in range(0, D_FF, tf):
            a = lax.dot_general(h, w_ref[0, c:c + tf, :], NT_DIMS, preferred_element_type=F32)
            b = lax.dot_general(h, w_ref[1, c:c + tf, :], NT_DIMS, preferred_element_type=F32)
            a_ref[:, c:c + tf] = a.astype(BF16)
            b_ref[:, c:c + tf] = b.astype(BF16)
            s_ref[:, c:c + tf] = (a * jax.nn.sigmoid(a) * b).astype(BF16)
        xo_ref[...] = xf + FFN_RES_SCALE * jnp.dot(s_ref[...], w_ref[2], preferred_element_type=F32)

    row = pl.BlockSpec((tm, D_MODEL), lambda i: (i, 0))
    hid = pl.BlockSpec((tm, D_FF), lambda i: (i, 0))
    return pl.pallas_call(
        body, name=name, grid=(t // tm,),
        in_specs=[row, pl.BlockSpec((1, D_MODEL), lambda i: (0, 0)), ANY],
        out_specs=[row, row, hid, hid],
        out_shape=[jax.ShapeDtypeStruct((t, D_MODEL), F32), jax.ShapeDtypeStruct((t, D_MODEL), BF16),
                   jax.ShapeDtypeStruct((t, D_FF), BF16), jax.ShapeDtypeStruct((t, D_FF), BF16)],
        scratch_shapes=[pltpu.VMEM((3, D_FF, D_MODEL), BF16), pltpu.VMEM((tm, D_FF), BF16),
                        pltpu.SemaphoreType.DMA(())],
        compiler_params=_params(("arbitrary",), VMEM_LIMIT),
    )(x, gain, w3)


def _ffn_dgrad(dxo, x, gain, a, b, w3, *, name, tm=256, tf=FF_CHUNK):
    t = x.shape[0]
    tm = min(tm, t)

    def body(dxo_ref, x_ref, g_ref, a_ref, b_ref, w_hbm, dxi_ref, da_ref, db_ref, s_ref, gb_ref, dg_ref, w_ref, sem):
        _load_resident(w_hbm, w_ref, sem)

        @pl.when(pl.program_id(0) == 0)
        def _():
            dg_ref[...] = jnp.zeros_like(dg_ref)

        go = dxo_ref[...]
        gb = (FFN_RES_SCALE * go).astype(BF16)
        gb_ref[...] = gb
        for c in range(0, D_FF, tf):
            ds = lax.dot_general(gb, w_ref[2, c:c + tf, :], NT_DIMS, preferred_element_type=F32)
            af = a_ref[:, c:c + tf].astype(F32)
            bf = b_ref[:, c:c + tf].astype(F32)
            sig = jax.nn.sigmoid(af)
            silu = af * sig
            da_ref[:, c:c + tf] = (ds * bf * (sig * (1.0 + af * (1.0 - sig)))).astype(BF16)
            db_ref[:, c:c + tf] = (ds * silu).astype(BF16)
            s_ref[:, c:c + tf] = (silu * bf).astype(BF16)
        dh = (jnp.dot(da_ref[...], w_ref[0], preferred_element_type=F32)
              + jnp.dot(db_ref[...], w_ref[1], preferred_element_type=F32))
        xhat, inv = _rms_stats(x_ref[...])
        dx, dgain = _rms_bwd(dh, xhat, inv, g_ref[...])
        dxi_ref[...] = go + dx
        dg_ref[...] += dgain

    row = pl.BlockSpec((tm, D_MODEL), lambda i: (i, 0))
    hid = pl.BlockSpec((tm, D_FF), lambda i: (i, 0))
    vec = pl.BlockSpec((1, D_MODEL), lambda i: (0, 0))
    return pl.pallas_call(
        body, name=name, grid=(t // tm,),
        in_specs=[row, row, vec, hid, hid, ANY],
        out_specs=[row, hid, hid, hid, row, vec],
        out_shape=[jax.ShapeDtypeStruct((t, D_MODEL), F32), jax.ShapeDtypeStruct((t, D_FF), BF16),
                   jax.ShapeDtypeStruct((t, D_FF), BF16), jax.ShapeDtypeStruct((t, D_FF), BF16),
                   jax.ShapeDtypeStruct((t, D_MODEL), BF16), jax.ShapeDtypeStruct((1, D_MODEL), F32)],
        scratch_shapes=[pltpu.VMEM((3, D_FF, D_MODEL), BF16), pltpu.SemaphoreType.DMA(())],
        compiler_params=_params(("arbitrary",), VMEM_LIMIT),
    )(dxo, x, gain, a, b, w3)


def _tn_matmul(a, b, *, name, bm, tk=1024):
    t, m = a.shape
    n = b.shape[1]
    tk = min(tk, t)
    nk = t // tk

    def body(a_ref, b_ref, o_ref):
        k = pl.program_id(1)
        p = lax.dot_general(a_ref[...], b_ref[...], TN_DIMS, preferred_element_type=F32)

        @pl.when(k == 0)
        def _():
            o_ref[...] = p

        @pl.when(k > 0)
        def _():
            o_ref[...] += p

    return pl.pallas_call(
        body, name=name, grid=(m // bm, nk),
        in_specs=[pl.BlockSpec((tk, bm), lambda i, k: (k, i)), pl.BlockSpec((tk, n), lambda i, k: (k, 0))],
        out_specs=pl.BlockSpec((bm, n), lambda i, k: (i, 0)),
        out_shape=jax.ShapeDtypeStruct((m, n), F32),
        compiler_params=_params(("parallel", "arbitrary"), VMEM_LIMIT),
    )(a, b)


def _loss_head(x, gain, target, *, name, tm=512):
    t = x.shape[0]
    tm = min(tm, t)

    def body(x_ref, g_ref, t_ref, dx_ref, loss_ref, dg_ref):
        @pl.when(pl.program_id(0) == 0)
        def _():
            loss_ref[...] = jnp.zeros_like(loss_ref)
            dg_ref[...] = jnp.zeros_like(dg_ref)

        xhat, inv = _rms_stats(x_ref[...])
        err = xhat * g_ref[...] - t_ref[...]
        loss_ref[...] += 0.5 * jnp.sum(jnp.mean(err * err, axis=-1, keepdims=True), axis=0, keepdims=True)
        dx, dgain = _rms_bwd(err * (1.0 / D_MODEL), xhat, inv, g_ref[...])
        dx_ref[...] = dx
        dg_ref[...] += dgain

    row = pl.BlockSpec((tm, D_MODEL), lambda i: (i, 0))
    vec = pl.BlockSpec((1, D_MODEL), lambda i: (0, 0))
    return pl.pallas_call(
        body, name=name, grid=(t // tm,),
        in_specs=[row, vec, row],
        out_specs=[row, pl.BlockSpec((1, 1), lambda i: (0, 0)), vec],
        out_shape=[jax.ShapeDtypeStruct((t, D_MODEL), F32), jax.ShapeDtypeStruct((1, 1), F32),
                   jax.ShapeDtypeStruct((1, D_MODEL), F32)],
        compiler_params=_params(("arbitrary",)),
    )(x, gain, target)


def _rope_tables(t):
    half = ROT_DIM // 2
    inv_freq = ROPE_THETA ** (-jnp.arange(0, ROT_DIM, 2, dtype=F32) / ROT_DIM)
    ang = jnp.arange(t, dtype=F32)[:, None] * inv_freq[None, :]
    cos, sin = jnp.cos(ang), jnp.sin(ang)
    zeros, ones = jnp.zeros((t, half), F32), jnp.ones((t, HEAD_DIM - ROT_DIM), F32)
    pad = jnp.zeros((t, HEAD_DIM - ROT_DIM), F32)
    mult = jnp.concatenate([cos, cos, ones], axis=1)
    from_lo = jnp.concatenate([zeros, sin, pad], axis=1)
    from_hi = jnp.concatenate([-sin, zeros, pad], axis=1)
    return jnp.stack([jnp.tile(m, (1, 2)) for m in (mult, from_lo, from_hi)])


def _tile_lanes(tab, width):
    return jnp.tile(tab, (1, width // tab.shape[1]))


def _rope(v, tab):
    w = v.shape[1]
    half_rot = ROT_DIM // 2
    return (v * _tile_lanes(tab[0], w)
            + pltpu.roll(v, half_rot, axis=1) * _tile_lanes(tab[1], w)
            + pltpu.roll(v, w - half_rot, axis=1) * _tile_lanes(tab[2], w))


def _rope_bwd(dv, tab):
    w = dv.shape[1]
    half_rot = ROT_DIM // 2
    return (dv * _tile_lanes(tab[0], w)
            + pltpu.roll(dv * _tile_lanes(tab[1], w), w - half_rot, axis=1)
            + pltpu.roll(dv * _tile_lanes(tab[2], w), half_rot, axis=1))


def _shift_rows(v, prev8_ref, n):
    r = lax.broadcasted_iota(jnp.int32, v.shape, 0)
    rolled = pltpu.roll(v, n, axis=0)
    last = prev8_ref[7:8, :]
    if n == 1:
        return jnp.where(r >= 1, rolled, last)
    return jnp.where(r >= 2, rolled, jnp.where(r == 0, prev8_ref[6:7, :], last))


def _shift_rows_up(v, next8_ref, n):
    rows = v.shape[0]
    r = lax.broadcasted_iota(jnp.int32, v.shape, 0)
    rolled = pltpu.roll(v, rows - n, axis=0)
    first = next8_ref[0:1, :]
    if n == 1:
        return jnp.where(r <= rows - 2, rolled, first)
    return jnp.where(r <= rows - 3, rolled, jnp.where(r == rows - 2, first, next8_ref[1:2, :]))


def _lane_half_mask(shape, half):
    lane = lax.broadcasted_iota(jnp.int32, shape, 1)
    return (lane >= HEAD_DIM) if half else (lane < HEAD_DIM)


def _to_kv_lanes(chunk, head, kv):
    if head % 2 != kv:
        chunk = pltpu.roll(chunk, HEAD_DIM, axis=1)
    return jnp.where(_lane_half_mask(chunk.shape, kv), chunk, 0.0)


def _from_kv_lanes(chunk, head, kv):
    chunk = jnp.where(_lane_half_mask(chunk.shape, kv), chunk, 0.0)
    if head % 2 != kv:
        chunk = pltpu.roll(chunk, HEAD_DIM, axis=1)
    return chunk


def _stack_heads(wide, kv):
    parts = []
    for g in range(Q_PER_KV):
        head = kv * Q_PER_KV + g
        chunk = wide[:, (head // 2) * 128:(head // 2 + 1) * 128]
        parts.append(_to_kv_lanes(chunk, head, kv))
    return jnp.concatenate(parts, axis=0)


def _window_mask(has_prev):
    shape = (Q_PER_KV * BLOCK, 2 * BLOCK)
    qi = lax.broadcasted_iota(jnp.int32, shape, 0) & (BLOCK - 1)
    kj = lax.broadcasted_iota(jnp.int32, shape, 1)
    first_key = BLOCK - has_prev * BLOCK
    in_prev = (kj < BLOCK) & (kj > qi) & (kj >= first_key)
    in_own = (kj >= BLOCK) & ((kj - BLOCK) <= qi)
    return in_prev | in_own


def _sink_column(sink_ref, kv):
    row = lax.broadcasted_iota(jnp.int32, (Q_PER_KV * BLOCK, 1), 0)
    col = jnp.full((Q_PER_KV * BLOCK, 1), sink_ref[0, kv * Q_PER_KV], F32)
    for g in range(1, Q_PER_KV):
        col = jnp.where(row >= g * BLOCK, sink_ref[0, kv * Q_PER_KV + g], col)
    return col


def _softmax_with_sink(q4, k2, mask, sink):
    s = lax.dot_general(q4, k2, NT_DIMS, preferred_element_type=F32) * SM_SCALE
    s = jnp.where(mask, s, MASK_VALUE)
    m = jnp.maximum(jnp.max(s, axis=-1, keepdims=True), sink)
    p = jnp.exp(s - m)
    e_sink = jnp.exp(sink - m)
    inv_den = 1.0 / (jnp.sum(p, axis=-1, keepdims=True) + e_sink)
    return p * inv_den, e_sink * inv_den


def _conv_terms(zf, prev8_ref, w_ref):
    b_gate, c_gate, u = zf[:, 0:CONV_W], zf[:, CONV_W:2 * CONV_W], zf[:, 2 * CONV_W:3 * CONV_W]
    vc = c_gate * u
    vm1 = _shift_rows(vc, prev8_ref, 1)
    vm2 = _shift_rows(vc, prev8_ref, 2)
    conv = w_ref[0:1, :] * vm2 + w_ref[1:2, :] * vm1 + w_ref[2:3, :] * vc
    return b_gate, c_gate, u, vc, vm1, vm2, conv


def _mixer_fwd(x, gain, win_t, wout, conv_w, sinks, rope, *, name):
    t = x.shape[0]
    nb = t // BLOCK

    def body(x_ref, g_ref, win_ref, wout_ref, cw_ref, sink_ref, rope_ref,
             xo_ref, h_ref, z_ref, y_ref, kprev_ref, vprev_ref, cprev_ref):
        i = pl.program_id(0)

        @pl.when(i == 0)
        def _():
            kprev_ref[...] = jnp.zeros_like(kprev_ref)
            vprev_ref[...] = jnp.zeros_like(vprev_ref)
            cprev_ref[...] = jnp.zeros_like(cprev_ref)

        xf = x_ref[...]
        xhat, _ = _rms_stats(xf)
        h = (xhat * g_ref[...]).astype(BF16)
        h_ref[...] = h
        zb = lax.dot_general(h, win_ref[...], NT_DIMS, preferred_element_type=F32).astype(BF16)
        z_ref[...] = zb
        zf = zb.astype(F32)

        b_gate, _, _, vc, _, _, conv = _conv_terms(zf, cprev_ref, cw_ref)
        y_conv = b_gate * conv
        cprev_ref[...] = vc[BLOCK - 8:BLOCK, :]

        tab = rope_ref[...]
        qr = _rope(zf[:, Q_OFF:K_OFF], tab)
        kr = _rope(zf[:, K_OFF:V_OFF], tab).astype(BF16)
        vb = zb[:, V_OFF:Z_W]
        k2 = jnp.concatenate([kprev_ref[...], kr], axis=0)
        v2 = jnp.concatenate([vprev_ref[...], vb], axis=0)
        kprev_ref[...] = kr
        vprev_ref[...] = vb

        mask = _window_mask(jnp.minimum(i, 1))
        chunks = [jnp.zeros((BLOCK, 128), F32) for _ in range(ATTN_W // 128)]
        for kv in range(N_KV_HEADS):
            q4 = _stack_heads(qr, kv).astype(BF16)
            probs, _ = _softmax_with_sink(q4, k2, mask, _sink_column(sink_ref, kv))
            o4 = jnp.dot(probs.astype(BF16), v2, preferred_element_type=F32)
            for g in range(Q_PER_KV):
                head = kv * Q_PER_KV + g
                chunks[head // 2] += _from_kv_lanes(o4[g * BLOCK:(g + 1) * BLOCK], head, kv)
        y = jnp.concatenate([y_conv] + chunks, axis=1).astype(BF16)
        y_ref[...] = y
        xo_ref[...] = xf + jnp.dot(y, wout_ref[...], preferred_element_type=F32)

    row = pl.BlockSpec((BLOCK, D_MODEL), lambda i: (i, 0))
    full = lambda shape: pl.BlockSpec(shape, lambda i: (0,) * len(shape))
    return pl.pallas_call(
        body, name=name, grid=(nb,),
        in_specs=[row, full((1, D_MODEL)), full((Z_W, D_MODEL)), full((D_MODEL, D_MODEL)), full((3, CONV_W)),
                  pl.BlockSpec(memory_space=pltpu.SMEM), pl.BlockSpec((3, BLOCK, 128), lambda i: (0, i, 0))],
        out_specs=[row, row, pl.BlockSpec((BLOCK, Z_W), lambda i: (i, 0)), row],
        out_shape=[jax.ShapeDtypeStruct((t, D_MODEL), F32), jax.ShapeDtypeStruct((t, D_MODEL), BF16),
                   jax.ShapeDtypeStruct((t, Z_W), BF16), jax.ShapeDtypeStruct((t, D_MODEL), BF16)],
        scratch_shapes=[pltpu.VMEM((BLOCK, KV_W), BF16), pltpu.VMEM((BLOCK, KV_W), BF16),
                        pltpu.VMEM((8, CONV_W), F32)],
        compiler_params=_params(("arbitrary",), VMEM_LIMIT),
    )(x, gain, win_t, wout, conv_w, sinks, rope)


def _mixer_bwd(dxo, x, gain, y, z, win_t, wout, conv_w, sinks, rope, *, name):
    t = x.shape[0]
    nb = t // BLOCK

    def body(dxo_ref, x_ref, g_ref, y_ref, z_ref, zp_ref, win_ref, wout_ref, cw_ref, sink_ref, rope_ref, ropep_ref,
             dxi_ref, dz_ref, gb_ref, dcw_ref, dsink_ref, dg_ref, dk_ref, dv_ref, dcn_ref, pvc_ref):
        i = pl.program_id(0)
        blk = nb - 1 - i

        @pl.when(i == 0)
        def _():
            dk_ref[...] = jnp.zeros_like(dk_ref)
            dv_ref[...] = jnp.zeros_like(dv_ref)
            dcn_ref[...] = jnp.zeros_like(dcn_ref)
            dcw_ref[...] = jnp.zeros_like(dcw_ref)
            dsink_ref[...] = jnp.zeros_like(dsink_ref)
            dg_ref[...] = jnp.zeros_like(dg_ref)

        has_prev = jnp.minimum(blk, 1)
        go = dxo_ref[...]
        gb = go.astype(BF16)
        gb_ref[...] = gb
        dy = lax.dot_general(gb, wout_ref[...], NT_DIMS, preferred_element_type=F32)
        dy_conv, dy_attn = dy[:, 0:CONV_W], dy[:, CONV_W:D_MODEL]
        zb, zpb = z_ref[...], zp_ref[...]
        zf = zb.astype(F32)
        zpf = zpb.astype(F32) * has_prev.astype(F32)

        pvc_ref[...] = (zpf[:, CONV_W:2 * CONV_W] * zpf[:, 2 * CONV_W:3 * CONV_W])[BLOCK - 8:BLOCK, :]
        b_gate, c_gate, u, vc, vm1, vm2, conv = _conv_terms(zf, pvc_ref, cw_ref)
        d_bgate = dy_conv * conv
        dc = dy_conv * b_gate
        tap = lax.broadcasted_iota(jnp.int32, (8, CONV_W), 0)
        dcw_ref[...] += jnp.where(tap == 0, jnp.sum(dc * vm2, axis=0, keepdims=True),
                                  jnp.where(tap == 1, jnp.sum(dc * vm1, axis=0, keepdims=True),
                                            jnp.where(tap == 2, jnp.sum(dc * vc, axis=0, keepdims=True), 0.0)))
        dvc = (cw_ref[2:3, :] * dc + cw_ref[1:2, :] * _shift_rows_up(dc, dcn_ref, 1)
               + cw_ref[0:1, :] * _shift_rows_up(dc, dcn_ref, 2))
        dcn_ref[...] = dc[0:8, :]
        d_cgate = dvc * u
        d_u = dvc * c_gate

        tab, tabp = rope_ref[...], ropep_ref[...]
        qr = _rope(zf[:, Q_OFF:K_OFF], tab)
        kr = _rope(zf[:, K_OFF:V_OFF], tab).astype(BF16)
        kpr = _rope(zpf[:, K_OFF:V_OFF], tabp).astype(BF16)
        k2 = jnp.concatenate([kpr, kr], axis=0)
        v2 = jnp.concatenate([zpb[:, V_OFF:Z_W], zb[:, V_OFF:Z_W]], axis=0)
        out = y_ref[:, CONV_W:D_MODEL].astype(F32)
        do_out = dy_attn * out
        mask = _window_mask(has_prev)
        dk2 = jnp.zeros((2 * BLOCK, KV_W), F32)
        dv2 = jnp.zeros((2 * BLOCK, KV_W), F32)
        dq_chunks = [jnp.zeros((BLOCK, 128), F32) for _ in range(ATTN_W // 128)]
        lane = lax.broadcasted_iota(jnp.int32, (1, 128), 1)
        dsink = jnp.zeros((1, 128), F32)
        for kv in range(N_KV_HEADS):
            q4 = _stack_heads(qr, kv).astype(BF16)
            do4 = _stack_heads(dy_attn, kv).astype(BF16)
            delta = jnp.sum(_stack_heads(do_out, kv), axis=-1, keepdims=True)
            probs, p_sink = _softmax_with_sink(q4, k2, mask, _sink_column(sink_ref, kv))
            dp = lax.dot_general(do4, v2, NT_DIMS, preferred_element_type=F32)
            ds = (probs * (dp - delta) * SM_SCALE).astype(BF16)
            dq4 = jnp.dot(ds, k2, preferred_element_type=F32)
            dk2 += lax.dot_general(ds, q4, TN_DIMS, preferred_element_type=F32)
            dv2 += lax.dot_general(probs.astype(BF16), do4, TN_DIMS, preferred_element_type=F32)
            sink_terms = p_sink * delta
            for g in range(Q_PER_KV):
                head = kv * Q_PER_KV + g
                rows = slice(g * BLOCK, (g + 1) * BLOCK)
                dq_chunks[head // 2] += _from_kv_lanes(dq4[rows], head, kv)
                dsink = dsink - jnp.where(lane == head, jnp.sum(sink_terms[rows], axis=0, keepdims=True), 0.0)
        dsink_ref[...] += dsink
        dq = _rope_bwd(jnp.concatenate(dq_chunks, axis=1), tab)
        dk = _rope_bwd(dk2[BLOCK:] + dk_ref[...], tab)
        dv = dv2[BLOCK:] + dv_ref[...]
        dk_ref[...] = dk2[:BLOCK]
        dv_ref[...] = dv2[:BLOCK]

        dzb = jnp.concatenate([d_bgate, d_cgate, d_u, dq, dk, dv], axis=1).astype(BF16)
        dz_ref[...] = dzb
        dh = jnp.dot(dzb, win_ref[...], preferred_element_type=F32)
        xhat, inv = _rms_stats(x_ref[...])
        dx, dgain = _rms_bwd(dh, xhat, inv, g_ref[...])
        dxi_ref[...] = go + dx
        dg_ref[...] += dgain

    rev = lambda i: (nb - 1 - i, 0)
    rev_prev = lambda i: (jnp.maximum(nb - 2 - i, 0), 0)
    row = pl.BlockSpec((BLOCK, D_MODEL), rev)
    full = lambda shape: pl.BlockSpec(shape, lambda i: (0,) * len(shape))
    return pl.pallas_call(
        body, name=name, grid=(nb,),
        in_specs=[row, row, full((1, D_MODEL)), row,
                  pl.BlockSpec((BLOCK, Z_W), rev), pl.BlockSpec((BLOCK, Z_W), rev_prev),
                  full((Z_W, D_MODEL)), full((D_MODEL, D_MODEL)), full((3, CONV_W)),
                  pl.BlockSpec(memory_space=pltpu.SMEM),
                  pl.BlockSpec((3, BLOCK, 128), lambda i: (0, nb - 1 - i, 0)),
                  pl.BlockSpec((3, BLOCK, 128), lambda i: (0, jnp.maximum(nb - 2 - i, 0), 0))],
        out_specs=[row, pl.BlockSpec((BLOCK, Z_W), rev), row, full((8, CONV_W)), full((1, 128)), full((1, D_MODEL))],
        out_shape=[jax.ShapeDtypeStruct((t, D_MODEL), F32), jax.ShapeDtypeStruct((t, Z_W), BF16),
                   jax.ShapeDtypeStruct((t, D_MODEL), BF16), jax.ShapeDtypeStruct((8, CONV_W), F32),
                   jax.ShapeDtypeStruct((1, 128), F32), jax.ShapeDtypeStruct((1, D_MODEL), F32)],
        scratch_shapes=[pltpu.VMEM((BLOCK, KV_W), F32), pltpu.VMEM((BLOCK, KV_W), F32), pltpu.VMEM((8, CONV_W), F32),
                        pltpu.VMEM((8, CONV_W), F32)],
        compiler_params=_params(("arbitrary",), VMEM_LIMIT),
    )(dxo, x, gain, y, z, z, win_t, wout, conv_w, sinks, rope, rope)


def _place():
    x, y, c = lax.axis_index("x"), lax.axis_index("y"), lax.axis_index("c")
    other_chips = [(1 - x, y), (x, 1 - y), (1 - x, 1 - y)]
    return x, y, c, other_chips


def _all_gather_rows(shards, *, name):
    n = len(shards)

    def body(*refs):
        srcs, outs = refs[:n], refs[n:2 * n]
        send_sems, recv_sems, local_sems = refs[2 * n:]
        x, y, c, chips = _place()
        me, sibling = (x, y, c), (x, y, 1 - c)

        def rows(t, px, py, pc):
            r = srcs[t].shape[-2]
            start = pl.multiple_of((4 * px + 2 * py + pc) * r, 16 if r % 16 == 0 else 8)
            if len(srcs[t].shape) == 3:
                return outs[t].at[:, pl.ds(start, r), :]
            return outs[t].at[pl.ds(start, r), :]

        def copy(t, k, block, to, own=False):
            return pltpu.make_async_remote_copy(
                src_ref=srcs[t] if own else rows(t, *block), dst_ref=rows(t, *block),
                send_sem=send_sems.at[t, k], recv_sem=recv_sems.at[t, k], device_id=to, device_id_type=MESH)

        mine = [pltpu.make_async_copy(srcs[t], rows(t, *me), local_sems.at[t]) for t in range(n)]
        first = []
        for t in range(n):
            mine[t].start()
            first.append(copy(t, 0, me, sibling, own=True))
            first += [copy(t, 1 + j, me, (*chip, c), own=True) for j, chip in enumerate(chips)]
        for cp in first:
            cp.start()
        passed = []
        for j, chip in enumerate(chips):
            for t in range(n):
                copy(t, 1 + j, (*chip, c), me).wait_recv()
                fwd = copy(t, 4 + j, (*chip, c), sibling)
                fwd.start()
                passed.append(fwd)
        for t in range(n):
            copy(t, 0, sibling, me).wait_recv()
            for j, chip in enumerate(chips):
                copy(t, 4 + j, (*chip, 1 - c), me).wait_recv()
        for cp in first + passed:
            cp.wait_send()
        for cp in mine:
            cp.wait()

    out_shape = [jax.ShapeDtypeStruct(s.shape[:-2] + (N_DEV * s.shape[-2], s.shape[-1]), s.dtype) for s in shards]
    return pl.pallas_call(
        body, name=name, in_specs=[ANY] * n, out_specs=[ANY] * n, out_shape=out_shape,
        scratch_shapes=[pltpu.SemaphoreType.DMA((n, 7)), pltpu.SemaphoreType.DMA((n, 7)),
                        pltpu.SemaphoreType.DMA((n,))],
    )(*shards)


def _split_start(bufs, n_copies, plan, *, name):
    n = len(bufs)

    def body(*refs):
        token = refs[-1]
        for cp in plan(refs[:n], refs[n], refs[n + 1]):
            cp.start()
        token[...] = jnp.zeros_like(token)

    res = pl.pallas_call(
        body, name=name, in_specs=[HBM_SPEC] * n,
        out_specs=(SEM_SPEC, SEM_SPEC, *[HBM_SPEC] * n, pl.BlockSpec(memory_space=pltpu.VMEM)),
        out_shape=(pltpu.SemaphoreType.DMA((n_copies,)), pltpu.SemaphoreType.DMA((n_copies,)),
                   *[pltpu.HBM(b.shape, b.dtype) for b in bufs], jax.ShapeDtypeStruct((8, 128), F32)),
        input_output_aliases={i: 2 + i for i in range(n)},
        compiler_params=pltpu.CompilerParams(has_side_effects=DATAFLOW),
    )(*[pltpu.with_memory_space_constraint(b, pltpu.HBM) for b in bufs])
    return res[0], res[1], list(res[2:2 + n]), res[-1]


def _split_wait(send_sems, recv_sems, bufs, after, plan, *, name):
    n = len(bufs)

    def body(*refs):
        for cp in plan(refs[:n], refs[n], refs[n + 1]):
            cp.wait_send()
            cp.wait_recv()

    return list(pl.pallas_call(
        body, name=name, in_specs=[HBM_SPEC] * n + [SEM_SPEC, SEM_SPEC, ANY], out_specs=[HBM_SPEC] * n,
        out_shape=tuple(pltpu.HBM(b.shape, b.dtype) for b in bufs),
        input_output_aliases={i: i for i in range(n)},
        compiler_params=pltpu.CompilerParams(has_side_effects=DATAFLOW),
    )(*bufs, send_sems, recv_sems, after))


def _sibling_plan(n):
    def plan(bufs, send_sems, recv_sems):
        x, y, c, _ = _place()
        return [pltpu.make_async_remote_copy(
            src_ref=bufs[t].at[:, 1 - c], dst_ref=bufs[n + t], send_sem=send_sems.at[t], recv_sem=recv_sems.at[t],
            device_id=(x, y, 1 - c), device_id_type=MESH) for t in range(n)]
    return plan


def _block_rows(ref, r, blk):
    start = pl.multiple_of(blk * r, 16 if r % 16 == 0 else 8)
    return ref.at[(slice(None),) * (len(ref.shape) - 2) + (pl.ds(start, r), slice(None))]


def _remote(src, dst, send_sems, recv_sems, k, peer):
    return pltpu.make_async_remote_copy(src_ref=src, dst_ref=dst, send_sem=send_sems.at[k], recv_sem=recv_sems.at[k],
                                        device_id=peer, device_id_type=MESH)


def _gather_send_plan(n):
    def plan(bufs, send_sems, recv_sems):
        x, y, c, chips = _place()
        peers = [(x, y, 1 - c)] + [(px, py, c) for px, py in chips]
        copies = []
        for t in range(n):
            dst = _block_rows(bufs[n + t], bufs[t].shape[-2], 4 * x + 2 * y + c)
            copies += [_remote(bufs[t], dst, send_sems, recv_sems, 4 * t + k, peer) for k, peer in enumerate(peers)]
        return copies
    return plan


def _gather_forward_plan(rows):
    def plan(bufs, send_sems, recv_sems):
        x, y, c, chips = _place()
        copies = []
        for t, r in enumerate(rows):
            for j, (px, py) in enumerate(chips):
                blk = _block_rows(bufs[t], r, 4 * px + 2 * py + c)
                copies.append(_remote(blk, blk, send_sems, recv_sems, 3 * t + j, (x, y, 1 - c)))
        return copies
    return plan


def _chips_plan(n, with_small):
    def plan(bufs, send_sems, recv_sems):
        x, y, c, chips = _place()
        copies = []
        for t in range(n):
            for j, (px, py) in enumerate(chips):
                copies.append(_remote(bufs[t].at[2 * px + py], bufs[n + t].at[j], send_sems, recv_sems, 3 * t + j,
                                      (px, py, c)))
        if with_small:
            mine = _block_rows(bufs[2 * n], 8, 4 * x + 2 * y + c)
            flips = [(fx, fy, fc) for fx in range(2) for fy in range(2) for fc in range(2)][1:]
            for k, (fx, fy, fc) in enumerate(flips):
                peer = (x + fx - 2 * x * fx, y + fy - 2 * y * fy, c + fc - 2 * c * fc)
                copies.append(_remote(mine, mine, send_sems, recv_sems, 3 * n + k, peer))
        return copies
    return plan


def _place_own(fulls, shards, *, name):
    n = len(fulls)

    def body(*refs):
        shard_refs, outs, sems = refs[n:2 * n], refs[2 * n:3 * n], refs[3 * n]
        x, y, c, _ = _place()
        copies = [pltpu.make_async_copy(shard_refs[t], _block_rows(outs[t], shard_refs[t].shape[-2], 4 * x + 2 * y + c),
                                        sems.at[t]) for t in range(n)]
        for cp in copies:
            cp.start()
        for cp in copies:
            cp.wait()

    return list(pl.pallas_call(
        body, name=name, in_specs=[ANY] * (2 * n), out_specs=[ANY] * n,
        out_shape=[jax.ShapeDtypeStruct(f.shape, f.dtype) for f in fulls],
        input_output_aliases={t: t for t in range(n)},
        scratch_shapes=[pltpu.SemaphoreType.DMA((n,))],
    )(*fulls, *shards))


def _add_sibling(grad, recv, core, *, name, tr):
    rows = grad.shape[2]

    def body(core_ref, g_ref, r_ref, o_ref, ob_ref):
        p = g_ref[:, 0] + r_ref[...]
        o_ref[...] = p
        ob_ref[...] = p.astype(BF16)

    out = pl.BlockSpec((4, tr, D_MODEL), lambda i, core_ref: (0, i, 0))
    return pl.pallas_call(
        body, name=name,
        grid_spec=pltpu.PrefetchScalarGridSpec(
            num_scalar_prefetch=1, grid=(rows // tr,),
            in_specs=[pl.BlockSpec((4, 1, tr, D_MODEL), lambda i, core_ref: (0, core_ref[0], i, 0)), out],
            out_specs=[out, out]),
        out_shape=[jax.ShapeDtypeStruct(recv.shape, F32), jax.ShapeDtypeStruct(recv.shape, BF16)],
        compiler_params=_params(("arbitrary",)),
    )(core, grad, recv)


def _add_chips(partial, recv, chip, *, name, tr):
    rows = partial.shape[1]

    def body(chip_ref, p_ref, r_ref, o_ref):
        o_ref[...] = p_ref[0] + r_ref[0].astype(F32) + r_ref[1].astype(F32) + r_ref[2].astype(F32)

    return pl.pallas_call(
        body, name=name,
        grid_spec=pltpu.PrefetchScalarGridSpec(
            num_scalar_prefetch=1, grid=(rows // tr,),
            in_specs=[pl.BlockSpec((1, tr, D_MODEL), lambda i, chip_ref: (chip_ref[0], i, 0)),
                      pl.BlockSpec((3, tr, D_MODEL), lambda i, chip_ref: (0, i, 0))],
            out_specs=pl.BlockSpec((tr, D_MODEL), lambda i, chip_ref: (i, 0))),
        out_shape=jax.ShapeDtypeStruct((rows, D_MODEL), F32),
        compiler_params=_params(("arbitrary",)),
    )(chip, partial, recv)


def _adamw_math(w, g, m, v):
    m = ADAM_B1 * m + (1.0 - ADAM_B1) * g
    v = ADAM_B2 * v + (1.0 - ADAM_B2) * (g * g)
    m_hat = m / (1.0 - ADAM_B1 ** ADAM_STEP)
    v_hat = v / (1.0 - ADAM_B2 ** ADAM_STEP)
    delta = -ADAM_LR * (m_hat / (jnp.sqrt(v_hat) + ADAM_EPS) + ADAM_WD * w)
    return delta, m, v


def _adamw(w, g, m, v, *, name, tr):
    rows, cols = w.shape

    def body(w_ref, g_ref, m_ref, v_ref, d_ref, mo_ref, vo_ref):
        d_ref[...], mo_ref[...], vo_ref[...] = _adamw_math(w_ref[...], g_ref[...], m_ref[...], v_ref[...])

    spec = pl.BlockSpec((tr, cols), lambda i: (i, 0))
    return pl.pallas_call(
        body, name=name, grid=(rows // tr,), in_specs=[spec] * 4, out_specs=[spec] * 3,
        out_shape=[jax.ShapeDtypeStruct(w.shape, F32)] * 3,
        compiler_params=_params(("parallel",)),
    )(w, g, m, v)


def _sum_small(gathered, *, name):
    def body(g_ref, o_ref):
        acc = g_ref[0]
        for k in range(1, N_DEV):
            acc = acc + g_ref[k]
        o_ref[...] = acc

    return pl.pallas_call(body, name=name, out_shape=jax.ShapeDtypeStruct(gathered.shape[1:], F32))(gathered)


def kernel(x, ffn1_norm, ffn1_w_gate, ffn1_w_up, ffn1_w_down, mix_norm, w_in, conv_w, attn_sinks, w_out, ffn2_norm, ffn2_w_gate, ffn2_w_up, ffn2_w_down, final_norm, loss_target, m_ffn1_norm, m_ffn1_w_gate, m_ffn1_w_up, m_ffn1_w_down, m_mix_norm, m_w_in, m_conv_w, m_attn_sinks, m_w_out, m_ffn2_norm, m_ffn2_w_gate, m_ffn2_w_up, m_ffn2_w_down, m_final_norm, v_ffn1_norm, v_ffn1_w_gate, v_ffn1_w_up, v_ffn1_w_down, v_mix_norm, v_w_in, v_conv_w, v_attn_sinks, v_w_out, v_ffn2_norm, v_ffn2_w_gate, v_ffn2_w_up, v_ffn2_w_down, v_final_norm):
    ix, iy, ic = lax.axis_index("x"), lax.axis_index("y"), lax.axis_index("c")
    my_index = 4 * ix + 2 * iy + ic
    core = ic.astype(jnp.int32).reshape(1)
    chip = (2 * ix + iy).astype(jnp.int32).reshape(1)

    given = dict(ffn1_norm=ffn1_norm, ffn1_w_gate=ffn1_w_gate, ffn1_w_up=ffn1_w_up, ffn1_w_down=ffn1_w_down,
                 mix_norm=mix_norm, w_in=w_in, conv_w=conv_w, attn_sinks=attn_sinks, w_out=w_out, ffn2_norm=ffn2_norm,
                 ffn2_w_gate=ffn2_w_gate, ffn2_w_up=ffn2_w_up, ffn2_w_down=ffn2_w_down, final_norm=final_norm)
    moments_m = dict(ffn1_norm=m_ffn1_norm, ffn1_w_gate=m_ffn1_w_gate, ffn1_w_up=m_ffn1_w_up, ffn1_w_down=m_ffn1_w_down,
                     mix_norm=m_mix_norm, w_in=m_w_in, conv_w=m_conv_w, attn_sinks=m_attn_sinks, w_out=m_w_out,
                     ffn2_norm=m_ffn2_norm, ffn2_w_gate=m_ffn2_w_gate, ffn2_w_up=m_ffn2_w_up, ffn2_w_down=m_ffn2_w_down,
                     final_norm=m_final_norm)
    moments_v = dict(ffn1_norm=v_ffn1_norm, ffn1_w_gate=v_ffn1_w_gate, ffn1_w_up=v_ffn1_w_up, ffn1_w_down=v_ffn1_w_down,
                     mix_norm=v_mix_norm, w_in=v_w_in, conv_w=v_conv_w, attn_sinks=v_attn_sinks, w_out=v_w_out,
                     ffn2_norm=v_ffn2_norm, ffn2_w_gate=v_ffn2_w_gate, ffn2_w_up=v_ffn2_w_up, ffn2_w_down=v_ffn2_w_down,
                     final_norm=v_final_norm)

    xs = x[0]
    target = loss_target[0]
    final_gain = final_norm.reshape(1, D_MODEL)

    def ffn_shard(wg, wu, wd):
        return jnp.stack([wg[0].T, wu[0].T, wd[0]]).astype(BF16)

    conv_cols = conv_w.shape[2]
    conv_shard = jnp.pad(conv_w[0], ((0, 5), (0, 128 - conv_cols)))
    rest_shards = [ffn_shard(ffn2_w_gate, ffn2_w_up, ffn2_w_down), w_in[0].T.astype(BF16), w_out[0].astype(BF16),
                   conv_shard]
    rest_rows = [s.shape[-2] for s in rest_shards]
    n_rest = len(rest_shards)
    (w1,) = _all_gather_rows([ffn_shard(ffn1_w_gate, ffn1_w_up, ffn1_w_down)], name="gather_ffn1")

    fulls = [lax.empty(s.shape[:-2] + (N_DEV * s.shape[-2], s.shape[-1]), s.dtype) for s in rest_shards]
    fulls = _place_own(fulls, rest_shards, name="place_own_weights")
    send_plan = _gather_send_plan(n_rest)
    ssem, rsem, bufs, _ = _split_start(rest_shards + fulls + [ffn1_norm], 4 * n_rest, send_plan, name="gather_rest_start")
    x1, h1, a1, b1 = _ffn_fwd(xs, bufs[-1], w1, name="ffn1_fwd")
    bufs = _split_wait(ssem, rsem, bufs[:2 * n_rest], x1, send_plan, name="gather_rest_wait")
    w2_part, mixer_parts = bufs[n_rest], bufs[n_rest + 1:]
    fwd_mixer = _gather_forward_plan(rest_rows[1:])
    ssem, rsem, bufs, token = _split_start(mixer_parts, 3 * (n_rest - 1), fwd_mixer, name="forward_mixer_start")
    win_t, wout, conv_all = _split_wait(ssem, rsem, bufs, token, fwd_mixer, name="forward_mixer_wait")
    conv_full = conv_all.reshape(N_DEV, 8, 128)[:, :3, :conv_cols].transpose(1, 0, 2).reshape(3, CONV_W)
    fwd_ffn2 = _gather_forward_plan(rest_rows[:1])
    ssem, rsem, bufs, _ = _split_start([w2_part, mix_norm], 3, fwd_ffn2, name="forward_ffn2_start")
    rope = _rope_tables(xs.shape[0])
    x2, hm, z, y = _mixer_fwd(x1, bufs[1], win_t, wout, conv_full, attn_sinks, rope, name="mixer_fwd")
    (w2,) = _split_wait(ssem, rsem, bufs[:1], x2, fwd_ffn2, name="forward_ffn2_wait")
    x3, h2, a2, b2 = _ffn_fwd(x2, ffn2_norm, w2, name="ffn2_fwd")
    dx3, loss_local, d_final = _loss_head(x3, final_gain, target, name="loss_head")
    loss = lax.psum(loss_local[0, 0], ("x", "y", "c"))

    def to_sibling_start(grads, riders, tag):
        views = [g.reshape(4, 2, g.shape[0] // N_DEV, D_MODEL) for g in grads]
        lands = [lax.empty((4,) + v.shape[2:], F32) for v in views]
        plan = _sibling_plan(len(views))
        ssem, rsem, bufs, _ = _split_start(views + lands + riders, len(views), plan, name=f"{tag}_sibling_start")
        return (ssem, rsem, bufs[:2 * len(views)], plan, tag), bufs[2 * len(views):]

    def to_sibling_finish(handle, after, names):
        ssem, rsem, bufs, plan, tag = handle
        bufs = _split_wait(ssem, rsem, bufs, after, plan, name=f"{tag}_sibling_wait")
        n = len(names)
        return [_add_sibling(v, r, core, name=f"add_sibling_{nm}", tr=v.shape[2] // 2)
                for v, r, nm in zip(bufs[:n], bufs[n:], names)]

    def to_chips_start(partials, riders, tag, small_all=None):
        p16 = [p for _, p in partials]
        lands = [lax.empty((3,) + p.shape[1:], BF16) for p in p16]
        extra = [] if small_all is None else [small_all]
        n = len(p16)
        plan = _chips_plan(n, small_all is not None)
        ssem, rsem, bufs, token = _split_start(p16 + lands + extra + riders, 3 * n + 7 * len(extra), plan,
                                               name=f"{tag}_chips_start")
        n_comm = 2 * n + len(extra)
        return (ssem, rsem, bufs[:n_comm], plan, tag, token), bufs[n_comm:]

    def to_chips_finish(handle, partials, after, names):
        ssem, rsem, bufs, plan, tag, _ = handle
        bufs = _split_wait(ssem, rsem, bufs, after, plan, name=f"{tag}_chips_wait")
        n = len(names)
        reduced = [_add_chips(p32, r, chip, name=f"add_chips_{nm}", tr=p32.shape[1] // 2)
                   for (p32, _), r, nm in zip(partials, bufs[n:2 * n], names)]
        return reduced, bufs[2 * n:]

    half_ff = D_FF // 2
    names2, namesm, names1 = ["ffn2_w_gate", "ffn2_w_up", "ffn2_w_down"], ["w_in", "w_out"], \
        ["ffn1_w_gate", "ffn1_w_up", "ffn1_w_down"]
    transposed = {"ffn1_w_gate", "ffn1_w_up", "w_in", "ffn2_w_gate", "ffn2_w_up"}

    dx2, da2, db2, s2, g2b, d_norm2 = _ffn_dgrad(dx3, x2, ffn2_norm, a2, b2, w2, name="ffn2_dgrad")
    gw2 = [_tn_matmul(da2, h2, name="ffn2_wgrad_gate", bm=half_ff), _tn_matmul(db2, h2, name="ffn2_wgrad_up", bm=half_ff),
           _tn_matmul(s2, g2b, name="ffn2_wgrad_down", bm=half_ff)]
    sib2, (gain_m,) = to_sibling_start(gw2, [mix_norm], "ffn2")
    dx1, dz, gmb, d_conv, d_sink, d_normm = _mixer_bwd(dx2, x1, gain_m, y, z, win_t, wout, conv_full, attn_sinks,
                                                       rope, name="mixer_bwd")
    p2 = to_sibling_finish(sib2, dx1, names2)
    chips2, (dz,) = to_chips_start(p2, [dz], "ffn2")
    gwm = [_tn_matmul(dz, hm, name="mixer_wgrad_in", bm=Z_W // 3), _tn_matmul(y, gmb, name="mixer_wgrad_out", bm=D_MODEL // 2)]
    sibm, (gain_1,) = to_sibling_start(gwm, [ffn1_norm], "mixer")
    dx0, da1, db1, s1, g1b, d_norm1 = _ffn_dgrad(dx1, xs, gain_1, a1, b1, w1, name="ffn1_dgrad")
    r2, _ = to_chips_finish(chips2, p2, dx0, names2)
    pm = to_sibling_finish(sibm, dx0, namesm)
    chipsm, (da1,) = to_chips_start(pm, [da1], "mixer")
    gw1 = [_tn_matmul(da1, h1, name="ffn1_wgrad_gate", bm=half_ff), _tn_matmul(db1, h1, name="ffn1_wgrad_up", bm=half_ff),
           _tn_matmul(s1, g1b, name="ffn1_wgrad_down", bm=half_ff)]
    rm, _ = to_chips_finish(chipsm, pm, gw1[2], namesm)
    sib1, (r2_first,) = to_sibling_start(gw1, [r2[0]], "ffn1")
    r2 = [r2_first] + r2[1:]

    grad, delta, new_m, new_v = {}, {}, {}, {}

    def adam_big(nm, g_rows):
        g = g_rows.T if nm in transposed else g_rows
        shape = given[nm].shape
        w2d = given[nm][0]
        d, mo, vo = _adamw(w2d, g, moments_m[nm][0], moments_v[nm][0], name=f"adamw_{nm}", tr=w2d.shape[0] // 2)
        grad[nm], delta[nm], new_m[nm], new_v[nm] = (a.reshape(shape) for a in (g, d, mo, vo))

    for nm, g in zip(names2 + namesm, r2 + rm):
        adam_big(nm, g)
    p1 = to_sibling_finish(sib1, new_v["w_out"], names1)
    small = jnp.concatenate([
        d_norm1, d_normm, d_norm2, d_final,
        jnp.pad(d_conv[0:3], ((0, 0), (0, D_MODEL - CONV_W))), jnp.pad(d_sink, ((0, 0), (0, D_MODEL - 128)))], axis=0)
    (small_all,) = _place_own([lax.empty((N_DEV * 8, D_MODEL), F32)], [small], name="place_own_small")
    chips1, _ = to_chips_start(p1, [], "ffn1", small_all)
    r1, (small_all,) = to_chips_finish(chips1, p1, chips1[-1], names1)
    for nm, g in zip(names1, r1):
        adam_big(nm, g)
    small_sum = _sum_small(small_all.reshape(N_DEV, 8, D_MODEL), name="sum_small")
    _update_small(given, moments_m, moments_v, small_sum, my_index, grad, delta, new_m, new_v)

    order = list(given)
    return (loss, dx0[None], *[grad[n] for n in order], *[delta[n] for n in order],
            *[new_m[n] for n in order], *[new_v[n] for n in order])


def _update_small(given, moments_m, moments_v, small_sum, my_index, grad, delta, new_m, new_v):
    conv_cols = given["conv_w"].shape[2]
    small_g = {
        "ffn1_norm": small_sum[0:1], "mix_norm": small_sum[1:2], "ffn2_norm": small_sum[2:3],
        "final_norm": small_sum[3:4],
        "conv_w": lax.dynamic_slice(small_sum[4:7, :CONV_W], (0, my_index * conv_cols), (3, conv_cols)),
        "attn_sinks": small_sum[7:8, :N_Q_HEADS],
    }

    small_names = ["ffn1_norm", "mix_norm", "ffn2_norm", "final_norm", "conv_w", "attn_sinks"]

    def pack(parts):
        rows = []
        for nm in small_names:
            p = parts[nm]
            p2 = p.reshape(3, conv_cols) if nm == "conv_w" else p.reshape(1, -1)
            rows.append(jnp.pad(p2, ((0, 0), (0, D_MODEL - p2.shape[1]))))
        rows.append(jnp.zeros((8, D_MODEL), F32))
        return jnp.concatenate(rows, axis=0)

    sd, sm, sv = _adamw(pack(given), pack(small_g), pack(moments_m), pack(moments_v), name="adamw_small", tr=16)
    row = 0
    for nm in small_names:
        shape = given[nm].shape
        nrow = 3 if nm == "conv_w" else 1
        ncol = conv_cols if nm == "conv_w" else given[nm].size
        grad[nm] = small_g[nm].reshape(shape)
        delta[nm], new_m[nm], new_v[nm] = (a[row:row + nrow, :ncol].reshape(shape) for a in (sd, sm, sv))
        row += nrow
```

```python
import functools

import jax
import jax.numpy as jnp
from jax import lax
from jax.experimental import pallas as pl
from jax.experimental.pallas import tpu as pltpu

F32 = jnp.float32
BF16 = jnp.bfloat16
MESH = pl.DeviceIdType.MESH
ANY = pl.BlockSpec(memory_space=pl.ANY)
HBM_SPEC = pl.BlockSpec(memory_space=pltpu.HBM)
SEM_SPEC = pl.BlockSpec(memory_space=pltpu.SEMAPHORE)
DATAFLOW = pltpu.SideEffectType.DATAFLOW_SIDE_EFFECTING

N_DEV = 8
D_MODEL = 1024
D_FF = 2816
CONV_W = 512
ATTN_W = 512
KV_W = 128
HEAD_DIM = 64
N_Q_HEADS = 8
N_KV_HEADS = 2
Q_PER_KV = N_Q_HEADS // N_KV_HEADS
BLOCK = 128
ROT_DIM = 16
ROPE_THETA = 500000.0
Z_W = 3 * CONV_W + ATTN_W + 2 * KV_W
Q_OFF = 3 * CONV_W
K_OFF = Q_OFF + ATTN_W
V_OFF = K_OFF + KV_W
RMS_EPS = 1e-5
MASK_VALUE = -1e30
SM_SCALE = HEAD_DIM ** -0.5
FFN_RES_SCALE = 0.5

ADAM_LR = 0.001
ADAM_B1 = 0.9
ADAM_B2 = 0.999
ADAM_EPS = 1e-08
ADAM_WD = 0.01
ADAM_STEP = 10

NT_DIMS = (((1,), (1,)), ((), ()))
TN_DIMS = (((0,), (0,)), ((), ()))

VMEM_LIMIT = 56 * 1024 * 1024
FF_CHUNK = 256


def _params(sem, vmem=None):
    return pltpu.CompilerParams(dimension_semantics=sem, vmem_limit_bytes=vmem)


def _behind(body, n_in, after):
    k = len(after)
    if k == 0:
        return body
    return lambda *refs: body(*refs[:n_in], *refs[n_in + k:])


def _rms_stats(xf):
    inv = lax.rsqrt(jnp.mean(xf * xf, axis=-1, keepdims=True) + RMS_EPS)
    return xf * inv, inv


def _rms_bwd(dh, xhat, inv, gain):
    dxhat = dh * gain
    dx = inv * (dxhat - xhat * jnp.mean(dxhat * xhat, axis=-1, keepdims=True))
    dgain = jnp.sum(dh * xhat, axis=0, keepdims=True)
    return dx, dgain


def _load_resident(w_hbm, w_ref, sem):
    @pl.when(pl.program_id(0) == 0)
    def _():
        cp = pltpu.make_async_copy(w_hbm, w_ref, sem)
        cp.start()
        cp.wait()


def _ffn_fwd(x, gain, w3, *, name, after=(), tm=256, tf=FF_CHUNK):
    t = x.shape[0]
    tm = min(tm, t)

    def body(x_ref, g_ref, w_hbm, xo_ref, h_ref, a_ref, b_ref, w_ref, s_ref, sem):
        _load_resident(w_hbm, w_ref, sem)
        xf = x_ref[...]
        xhat, _ = _rms_stats(xf)
        h = (xhat * g_ref[...]).astype(BF16)
        h_ref[...] = h
        for c in range(0, D_FF, tf):
            a = lax.dot_general(h, w_ref[0, c:c + tf, :], NT_DIMS, preferred_element_type=F32)
            b = lax.dot_general(h, w_ref[1, c:c + tf, :], NT_DIMS, preferred_element_type=F32)
            a_ref[:, c:c + tf] = a.astype(BF16)
            b_ref[:, c:c + tf] = b.astype(BF16)
            s_ref[:, c:c + tf] = (a * jax.nn.sigmoid(a) * b).astype(BF16)
        xo_ref[...] = xf + FFN_RES_SCALE * jnp.dot(s_ref[...], w_ref[2], preferred_element_type=F32)

    row = pl.BlockSpec((tm, D_MODEL), lambda i: (i, 0))
    hid = pl.BlockSpec((tm, D_FF), lambda i: (i, 0))
    return pl.pallas_call(
        _behind(body, 3, after), name=name, grid=(t // tm,),
        in_specs=[row, pl.BlockSpec((1, D_MODEL), lambda i: (0, 0)), ANY] + [ANY] * len(after),
        out_specs=[row, row, hid, hid],
        out_shape=[jax.ShapeDtypeStruct((t, D_MODEL), F32), jax.ShapeDtypeStruct((t, D_MODEL), BF16),
                   jax.ShapeDtypeStruct((t, D_FF), BF16), jax.ShapeDtypeStruct((t, D_FF), BF16)],
        scratch_shapes=[pltpu.VMEM((3, D_FF, D_MODEL), BF16), pltpu.VMEM((tm, D_FF), BF16),
                        pltpu.SemaphoreType.DMA(())],
        compiler_params=_params(("arbitrary",), VMEM_LIMIT),
    )(x, gain, w3, *after)


def _ffn_dgrad(dxo, x, gain, a, b, w3, *, name, after=(), tm=256, tf=FF_CHUNK):
    t = x.shape[0]
    tm = min(tm, t)

    def body(dxo_ref, x_ref, g_ref, a_ref, b_ref, w_hbm, dxi_ref, da_ref, db_ref, s_ref, gb_ref, dg_ref, w_ref, sem):
        _load_resident(w_hbm, w_ref, sem)

        @pl.when(pl.program_id(0) == 0)
        def _():
            dg_ref[...] = jnp.zeros_like(dg_ref)

        go = dxo_ref[...]
        gb = (FFN_RES_SCALE * go).astype(BF16)
        gb_ref[...] = gb
        for c in range(0, D_FF, tf):
            ds = lax.dot_general(gb, w_ref[2, c:c + tf, :], NT_DIMS, preferred_element_type=F32)
            af = a_ref[:, c:c + tf].astype(F32)
            bf = b_ref[:, c:c + tf].astype(F32)
            sig = jax.nn.sigmoid(af)
            silu = af * sig
            da_ref[:, c:c + tf] = (ds * bf * (sig * (1.0 + af * (1.0 - sig)))).astype(BF16)
            db_ref[:, c:c + tf] = (ds * silu).astype(BF16)
            s_ref[:, c:c + tf] = (silu * bf).astype(BF16)
        dh = (jnp.dot(da_ref[...], w_ref[0], preferred_element_type=F32)
              + jnp.dot(db_ref[...], w_ref[1], preferred_element_type=F32))
        xhat, inv = _rms_stats(x_ref[...])
        dx, dgain = _rms_bwd(dh, xhat, inv, g_ref[...])
        dxi_ref[...] = go + dx
        dg_ref[...] += dgain

    row = pl.BlockSpec((tm, D_MODEL), lambda i: (i, 0))
    hid = pl.BlockSpec((tm, D_FF), lambda i: (i, 0))
    vec = pl.BlockSpec((1, D_MODEL), lambda i: (0, 0))
    return pl.pallas_call(
        _behind(body, 6, after), name=name, grid=(t // tm,),
        in_specs=[row, row, vec, hid, hid, ANY] + [ANY] * len(after),
        out_specs=[row, hid, hid, hid, row, vec],
        out_shape=[jax.ShapeDtypeStruct((t, D_MODEL), F32), jax.ShapeDtypeStruct((t, D_FF), BF16),
                   jax.ShapeDtypeStruct((t, D_FF), BF16), jax.ShapeDtypeStruct((t, D_FF), BF16),
                   jax.ShapeDtypeStruct((t, D_MODEL), BF16), jax.ShapeDtypeStruct((1, D_MODEL), F32)],
        scratch_shapes=[pltpu.VMEM((3, D_FF, D_MODEL), BF16), pltpu.SemaphoreType.DMA(())],
        compiler_params=_params(("arbitrary",), VMEM_LIMIT),
    )(dxo, x, gain, a, b, w3, *after)


def _tn_matmul(a, b, *, name, bm, after=(), tk=1024):
    t, m = a.shape
    n = b.shape[1]
    tk = min(tk, t)
    nk = t // tk

    def body(a_ref, b_ref, o_ref):
        k = pl.program_id(1)
        p = lax.dot_general(a_ref[...], b_ref[...], TN_DIMS, preferred_element_type=F32)

        @pl.when(k == 0)
        def _():
            o_ref[...] = p

        @pl.when(k > 0)
        def _():
            o_ref[...] += p

    return pl.pallas_call(
        _behind(body, 2, after), name=name, grid=(m // bm, nk),
        in_specs=[pl.BlockSpec((tk, bm), lambda i, k: (k, i)), pl.BlockSpec((tk, n), lambda i, k: (k, 0))]
        + [ANY] * len(after),
        out_specs=pl.BlockSpec((bm, n), lambda i, k: (i, 0)),
        out_shape=jax.ShapeDtypeStruct((m, n), F32),
        compiler_params=_params(("parallel", "arbitrary"), VMEM_LIMIT),
    )(a, b, *after)


def _loss_head(x, gain, target, *, name, tm=512):
    t = x.shape[0]
    tm = min(tm, t)

    def body(x_ref, g_ref, t_ref, dx_ref, loss_ref, dg_ref):
        @pl.when(pl.program_id(0) == 0)
        def _():
            loss_ref[...] = jnp.zeros_like(loss_ref)
            dg_ref[...] = jnp.zeros_like(dg_ref)

        xhat, inv = _rms_stats(x_ref[...])
        err = xhat * g_ref[...] - t_ref[...]
        loss_ref[...] += 0.5 * jnp.sum(jnp.mean(err * err, axis=-1, keepdims=True), axis=0, keepdims=True)
        dx, dgain = _rms_bwd(err * (1.0 / D_MODEL), xhat, inv, g_ref[...])
        dx_ref[...] = dx
        dg_ref[...] += dgain

    row = pl.BlockSpec((tm, D_MODEL), lambda i: (i, 0))
    vec = pl.BlockSpec((1, D_MODEL), lambda i: (0, 0))
    return pl.pallas_call(
        body, name=name, grid=(t // tm,),
        in_specs=[row, vec, row],
        out_specs=[row, pl.BlockSpec((1, 1), lambda i: (0, 0)), vec],
        out_shape=[jax.ShapeDtypeStruct((t, D_MODEL), F32), jax.ShapeDtypeStruct((1, 1), F32),
                   jax.ShapeDtypeStruct((1, D_MODEL), F32)],
        compiler_params=_params(("arbitrary",)),
    )(x, gain, target)


def _rope_tables(t):
    half = ROT_DIM // 2
    inv_freq = ROPE_THETA ** (-jnp.arange(0, ROT_DIM, 2, dtype=F32) / ROT_DIM)
    ang = jnp.arange(t, dtype=F32)[:, None] * inv_freq[None, :]
    cos, sin = jnp.cos(ang), jnp.sin(ang)
    zeros, ones = jnp.zeros((t, half), F32), jnp.ones((t, HEAD_DIM - ROT_DIM), F32)
    pad = jnp.zeros((t, HEAD_DIM - ROT_DIM), F32)
    mult = jnp.concatenate([cos, cos, ones], axis=1)
    from_lo = jnp.concatenate([zeros, sin, pad], axis=1)
    from_hi = jnp.concatenate([-sin, zeros, pad], axis=1)
    return jnp.stack([jnp.tile(m, (1, 2)) for m in (mult, from_lo, from_hi)])


def _tile_lanes(tab, width):
    return jnp.tile(tab, (1, width // tab.shape[1]))


def _rope(v, tab):
    w = v.shape[1]
    half_rot = ROT_DIM // 2
    return (v * _tile_lanes(tab[0], w)
            + pltpu.roll(v, half_rot, axis=1) * _tile_lanes(tab[1], w)
            + pltpu.roll(v, w - half_rot, axis=1) * _tile_lanes(tab[2], w))


def _rope_bwd(dv, tab):
    w = dv.shape[1]
    half_rot = ROT_DIM // 2
    return (dv * _tile_lanes(tab[0], w)
            + pltpu.roll(dv * _tile_lanes(tab[1], w), w - half_rot, axis=1)
            + pltpu.roll(dv * _tile_lanes(tab[2], w), half_rot, axis=1))


def _shift_rows(v, prev8_ref, n):
    r = lax.broadcasted_iota(jnp.int32, v.shape, 0)
    rolled = pltpu.roll(v, n, axis=0)
    last = prev8_ref[7:8, :]
    if n == 1:
        return jnp.where(r >= 1, rolled, last)
    return jnp.where(r >= 2, rolled, jnp.where(r == 0, prev8_ref[6:7, :], last))


def _shift_rows_up(v, next8_ref, n):
    rows = v.shape[0]
    r = lax.broadcasted_iota(jnp.int32, v.shape, 0)
    rolled = pltpu.roll(v, rows - n, axis=0)
    first = next8_ref[0:1, :]
    if n == 1:
        return jnp.where(r <= rows - 2, rolled, first)
    return jnp.where(r <= rows - 3, rolled, jnp.where(r == rows - 2, first, next8_ref[1:2, :]))


def _lane_half_mask(shape, half):
    lane = lax.broadcasted_iota(jnp.int32, shape, 1)
    return (lane >= HEAD_DIM) if half else (lane < HEAD_DIM)


def _to_kv_lanes(chunk, head, kv):
    if head % 2 != kv:
        chunk = pltpu.roll(chunk, HEAD_DIM, axis=1)
    return jnp.where(_lane_half_mask(chunk.shape, kv), chunk, 0.0)


def _from_kv_lanes(chunk, head, kv):
    chunk = jnp.where(_lane_half_mask(chunk.shape, kv), chunk, 0.0)
    if head % 2 != kv:
        chunk = pltpu.roll(chunk, HEAD_DIM, axis=1)
    return chunk


def _stack_heads(wide, kv):
    parts = []
    for g in range(Q_PER_KV):
        head = kv * Q_PER_KV + g
        chunk = wide[:, (head // 2) * 128:(head // 2 + 1) * 128]
        parts.append(_to_kv_lanes(chunk, head, kv))
    return jnp.concatenate(parts, axis=0)


def _window_mask(has_prev):
    shape = (Q_PER_KV * BLOCK, 2 * BLOCK)
    qi = lax.broadcasted_iota(jnp.int32, shape, 0) & (BLOCK - 1)
    kj = lax.broadcasted_iota(jnp.int32, shape, 1)
    first_key = BLOCK - has_prev * BLOCK
    in_prev = (kj < BLOCK) & (kj > qi) & (kj >= first_key)
    in_own = (kj >= BLOCK) & ((kj - BLOCK) <= qi)
    return in_prev | in_own


def _sink_column(sink_ref, kv):
    row = lax.broadcasted_iota(jnp.int32, (Q_PER_KV * BLOCK, 1), 0)
    col = jnp.full((Q_PER_KV * BLOCK, 1), sink_ref[0, kv * Q_PER_KV], F32)
    for g in range(1, Q_PER_KV):
        col = jnp.where(row >= g * BLOCK, sink_ref[0, kv * Q_PER_KV + g], col)
    return col


def _softmax_with_sink(q4, k2, mask, sink):
    s = lax.dot_general(q4, k2, NT_DIMS, preferred_element_type=F32) * SM_SCALE
    s = jnp.where(mask, s, MASK_VALUE)
    m = jnp.maximum(jnp.max(s, axis=-1, keepdims=True), sink)
    p = jnp.exp(s - m)
    e_sink = jnp.exp(sink - m)
    inv_den = 1.0 / (jnp.sum(p, axis=-1, keepdims=True) + e_sink)
    return p * inv_den, e_sink * inv_den


def _conv_terms(zf, prev8_ref, w_ref):
    b_gate, c_gate, u = zf[:, 0:CONV_W], zf[:, CONV_W:2 * CONV_W], zf[:, 2 * CONV_W:3 * CONV_W]
    vc = c_gate * u
    vm1 = _shift_rows(vc, prev8_ref, 1)
    vm2 = _shift_rows(vc, prev8_ref, 2)
    conv = w_ref[0:1, :] * vm2 + w_ref[1:2, :] * vm1 + w_ref[2:3, :] * vc
    return b_gate, c_gate, u, vc, vm1, vm2, conv


def _mixer_fwd(x, gain, win_t, wout, conv_w, sinks, rope, *, name, after=()):
    t = x.shape[0]
    nb = t // BLOCK

    def body(x_ref, g_ref, win_ref, wout_ref, cw_ref, sink_ref, rope_ref,
             xo_ref, h_ref, z_ref, y_ref, kprev_ref, vprev_ref, cprev_ref):
        i = pl.program_id(0)

        @pl.when(i == 0)
        def _():
            kprev_ref[...] = jnp.zeros_like(kprev_ref)
            vprev_ref[...] = jnp.zeros_like(vprev_ref)
            cprev_ref[...] = jnp.zeros_like(cprev_ref)

        xf = x_ref[...]
        xhat, _ = _rms_stats(xf)
        h = (xhat * g_ref[...]).astype(BF16)
        h_ref[...] = h
        zb = lax.dot_general(h, win_ref[...], NT_DIMS, preferred_element_type=F32).astype(BF16)
        z_ref[...] = zb
        zf = zb.astype(F32)

        b_gate, _, _, vc, _, _, conv = _conv_terms(zf, cprev_ref, cw_ref)
        y_conv = b_gate * conv
        cprev_ref[...] = vc[BLOCK - 8:BLOCK, :]

        tab = rope_ref[...]
        qr = _rope(zf[:, Q_OFF:K_OFF], tab)
        kr = _rope(zf[:, K_OFF:V_OFF], tab).astype(BF16)
        vb = zb[:, V_OFF:Z_W]
        k2 = jnp.concatenate([kprev_ref[...], kr], axis=0)
        v2 = jnp.concatenate([vprev_ref[...], vb], axis=0)
        kprev_ref[...] = kr
        vprev_ref[...] = vb

        mask = _window_mask(jnp.minimum(i, 1))
        chunks = [jnp.zeros((BLOCK, 128), F32) for _ in range(ATTN_W // 128)]
        for kv in range(N_KV_HEADS):
            q4 = _stack_heads(qr, kv).astype(BF16)
            probs, _ = _softmax_with_sink(q4, k2, mask, _sink_column(sink_ref, kv))
            o4 = jnp.dot(probs.astype(BF16), v2, preferred_element_type=F32)
            for g in range(Q_PER_KV):
                head = kv * Q_PER_KV + g
                chunks[head // 2] += _from_kv_lanes(o4[g * BLOCK:(g + 1) * BLOCK], head, kv)
        y = jnp.concatenate([y_conv] + chunks, axis=1).astype(BF16)
        y_ref[...] = y
        xo_ref[...] = xf + jnp.dot(y, wout_ref[...], preferred_element_type=F32)

    row = pl.BlockSpec((BLOCK, D_MODEL), lambda i: (i, 0))
    full = lambda shape: pl.BlockSpec(shape, lambda i: (0,) * len(shape))
    return pl.pallas_call(
        _behind(body, 7, after), name=name, grid=(nb,),
        in_specs=[row, full((1, D_MODEL)), full((Z_W, D_MODEL)), full((D_MODEL, D_MODEL)), full((3, CONV_W)),
                  pl.BlockSpec(memory_space=pltpu.SMEM), pl.BlockSpec((3, BLOCK, 128), lambda i: (0, i, 0))]
        + [ANY] * len(after),
        out_specs=[row, row, pl.BlockSpec((BLOCK, Z_W), lambda i: (i, 0)), row],
        out_shape=[jax.ShapeDtypeStruct((t, D_MODEL), F32), jax.ShapeDtypeStruct((t, D_MODEL), BF16),
                   jax.ShapeDtypeStruct((t, Z_W), BF16), jax.ShapeDtypeStruct((t, D_MODEL), BF16)],
        scratch_shapes=[pltpu.VMEM((BLOCK, KV_W), BF16), pltpu.VMEM((BLOCK, KV_W), BF16),
                        pltpu.VMEM((8, CONV_W), F32)],
        compiler_params=_params(("arbitrary",), VMEM_LIMIT),
    )(x, gain, win_t, wout, conv_w, sinks, rope, *after)


def _mixer_bwd(dxo, x, gain, y, z, win_t, wout, conv_w, sinks, rope, *, name, after=()):
    t = x.shape[0]
    nb = t // BLOCK

    def body(dxo_ref, x_ref, g_ref, y_ref, z_ref, zp_ref, win_ref, wout_ref, cw_ref, sink_ref, rope_ref, ropep_ref,
             dxi_ref, dz_ref, gb_ref, dcw_ref, dsink_ref, dg_ref, dk_ref, dv_ref, dcn_ref, pvc_ref):
        i = pl.program_id(0)
        blk = nb - 1 - i

        @pl.when(i == 0)
        def _():
            dk_ref[...] = jnp.zeros_like(dk_ref)
            dv_ref[...] = jnp.zeros_like(dv_ref)
            dcn_ref[...] = jnp.zeros_like(dcn_ref)
            dcw_ref[...] = jnp.zeros_like(dcw_ref)
            dsink_ref[...] = jnp.zeros_like(dsink_ref)
            dg_ref[...] = jnp.zeros_like(dg_ref)

        has_prev = jnp.minimum(blk, 1)
        go = dxo_ref[...]
        gb = go.astype(BF16)
        gb_ref[...] = gb
        dy = lax.dot_general(gb, wout_ref[...], NT_DIMS, preferred_element_type=F32)
        dy_conv, dy_attn = dy[:, 0:CONV_W], dy[:, CONV_W:D_MODEL]
        zb, zpb = z_ref[...], zp_ref[...]
        zf = zb.astype(F32)
        zpf = zpb.astype(F32) * has_prev.astype(F32)

        pvc_ref[...] = (zpf[:, CONV_W:2 * CONV_W] * zpf[:, 2 * CONV_W:3 * CONV_W])[BLOCK - 8:BLOCK, :]
        b_gate, c_gate, u, vc, vm1, vm2, conv = _conv_terms(zf, pvc_ref, cw_ref)
        d_bgate = dy_conv * conv
        dc = dy_conv * b_gate
        tap = lax.broadcasted_iota(jnp.int32, (8, CONV_W), 0)
        dcw_ref[...] += jnp.where(tap == 0, jnp.sum(dc * vm2, axis=0, keepdims=True),
                                  jnp.where(tap == 1, jnp.sum(dc * vm1, axis=0, keepdims=True),
                                            jnp.where(tap == 2, jnp.sum(dc * vc, axis=0, keepdims=True), 0.0)))
        dvc = (cw_ref[2:3, :] * dc + cw_ref[1:2, :] * _shift_rows_up(dc, dcn_ref, 1)
               + cw_ref[0:1, :] * _shift_rows_up(dc, dcn_ref, 2))
        dcn_ref[...] = dc[0:8, :]
        d_cgate = dvc * u
        d_u = dvc * c_gate

        tab, tabp = rope_ref[...], ropep_ref[...]
        qr = _rope(zf[:, Q_OFF:K_OFF], tab)
        kr = _rope(zf[:, K_OFF:V_OFF], tab).astype(BF16)
        kpr = _rope(zpf[:, K_OFF:V_OFF], tabp).astype(BF16)
        k2 = jnp.concatenate([kpr, kr], axis=0)
        v2 = jnp.concatenate([zpb[:, V_OFF:Z_W], zb[:, V_OFF:Z_W]], axis=0)
        out = y_ref[:, CONV_W:D_MODEL].astype(F32)
        do_out = dy_attn * out
        mask = _window_mask(has_prev)
        dk2 = jnp.zeros((2 * BLOCK, KV_W), F32)
        dv2 = jnp.zeros((2 * BLOCK, KV_W), F32)
        dq_chunks = [jnp.zeros((BLOCK, 128), F32) for _ in range(ATTN_W // 128)]
        lane = lax.broadcasted_iota(jnp.int32, (1, 128), 1)
        dsink = jnp.zeros((1, 128), F32)
        for kv in range(N_KV_HEADS):
            q4 = _stack_heads(qr, kv).astype(BF16)
            do4 = _stack_heads(dy_attn, kv).astype(BF16)
            delta = jnp.sum(_stack_heads(do_out, kv), axis=-1, keepdims=True)
            probs, p_sink = _softmax_with_sink(q4, k2, mask, _sink_column(sink_ref, kv))
            dp = lax.dot_general(do4, v2, NT_DIMS, preferred_element_type=F32)
            ds = (probs * (dp - delta) * SM_SCALE).astype(BF16)
            dq4 = jnp.dot(ds, k2, preferred_element_type=F32)
            dk2 += lax.dot_general(ds, q4, TN_DIMS, preferred_element_type=F32)
            dv2 += lax.dot_general(probs.astype(BF16), do4, TN_DIMS, preferred_element_type=F32)
            sink_terms = p_sink * delta
            for g in range(Q_PER_KV):
                head = kv * Q_PER_KV + g
                rows = slice(g * BLOCK, (g + 1) * BLOCK)
                dq_chunks[head // 2] += _from_kv_lanes(dq4[rows], head, kv)
                dsink = dsink - jnp.where(lane == head, jnp.sum(sink_terms[rows], axis=0, keepdims=True), 0.0)
        dsink_ref[...] += dsink
        dq = _rope_bwd(jnp.concatenate(dq_chunks, axis=1), tab)
        dk = _rope_bwd(dk2[BLOCK:] + dk_ref[...], tab)
        dv = dv2[BLOCK:] + dv_ref[...]
        dk_ref[...] = dk2[:BLOCK]
        dv_ref[...] = dv2[:BLOCK]

        dzb = jnp.concatenate([d_bgate, d_cgate, d_u, dq, dk, dv], axis=1).astype(BF16)
        dz_ref[...] = dzb
        dh = jnp.dot(dzb, win_ref[...], preferred_element_type=F32)
        xhat, inv = _rms_stats(x_ref[...])
        dx, dgain = _rms_bwd(dh, xhat, inv, g_ref[...])
        dxi_ref[...] = go + dx
        dg_ref[...] += dgain

    rev = lambda i: (nb - 1 - i, 0)
    rev_prev = lambda i: (jnp.maximum(nb - 2 - i, 0), 0)
    row = pl.BlockSpec((BLOCK, D_MODEL), rev)
    full = lambda shape: pl.BlockSpec(shape, lambda i: (0,) * len(shape))
    return pl.pallas_call(
        _behind(body, 12, after), name=name, grid=(nb,),
        in_specs=[row, row, full((1, D_MODEL)), row,
                  pl.BlockSpec((BLOCK, Z_W), rev), pl.BlockSpec((BLOCK, Z_W), rev_prev),
                  full((Z_W, D_MODEL)), full((D_MODEL, D_MODEL)), full((3, CONV_W)),
                  pl.BlockSpec(memory_space=pltpu.SMEM),
                  pl.BlockSpec((3, BLOCK, 128), lambda i: (0, nb - 1 - i, 0)),
                  pl.BlockSpec((3, BLOCK, 128), lambda i: (0, jnp.maximum(nb - 2 - i, 0), 0))] + [ANY] * len(after),
        out_specs=[row, pl.BlockSpec((BLOCK, Z_W), rev), row, full((8, CONV_W)), full((1, 128)), full((1, D_MODEL))],
        out_shape=[jax.ShapeDtypeStruct((t, D_MODEL), F32), jax.ShapeDtypeStruct((t, Z_W), BF16),
                   jax.ShapeDtypeStruct((t, D_MODEL), BF16), jax.ShapeDtypeStruct((8, CONV_W), F32),
                   jax.ShapeDtypeStruct((1, 128), F32), jax.ShapeDtypeStruct((1, D_MODEL), F32)],
        scratch_shapes=[pltpu.VMEM((BLOCK, KV_W), F32), pltpu.VMEM((BLOCK, KV_W), F32), pltpu.VMEM((8, CONV_W), F32),
                        pltpu.VMEM((8, CONV_W), F32)],
        compiler_params=_params(("arbitrary",), VMEM_LIMIT),
    )(dxo, x, gain, y, z, z, win_t, wout, conv_w, sinks, rope, rope, *after)


def _place():
    x, y, c = lax.axis_index("x"), lax.axis_index("y"), lax.axis_index("c")
    other_chips = [(1 - x, y), (x, 1 - y), (1 - x, 1 - y)]
    return x, y, c, other_chips


def _all_gather_rows(shards, place=(), *, name):
    n, p = len(shards), len(place)

    def body(*refs):
        srcs, place_srcs = refs[:n], refs[n:n + p]
        outs, place_outs = refs[n + p:2 * n + p], refs[2 * n + p:2 * (n + p)]
        send_sems, recv_sems, local_sems = refs[2 * (n + p):]
        x, y, c, chips = _place()
        me, sibling = (x, y, c), (x, y, 1 - c)

        def rows(t, px, py, pc):
            r = srcs[t].shape[-2]
            start = pl.multiple_of((4 * px + 2 * py + pc) * r, 16 if r % 16 == 0 else 8)
            if len(srcs[t].shape) == 3:
                return outs[t].at[:, pl.ds(start, r), :]
            return outs[t].at[pl.ds(start, r), :]

        def copy(t, k, block, to, own=False):
            return pltpu.make_async_remote_copy(
                src_ref=srcs[t] if own else rows(t, *block), dst_ref=rows(t, *block),
                send_sem=send_sems.at[t, k], recv_sem=recv_sems.at[t, k], device_id=to, device_id_type=MESH)

        mine = [pltpu.make_async_copy(srcs[t], rows(t, *me), local_sems.at[t]) for t in range(n)]
        mine += [pltpu.make_async_copy(place_srcs[q],
                                       _block_rows(place_outs[q], place_srcs[q].shape[-2], 4 * x + 2 * y + c),
                                       local_sems.at[n + q]) for q in range(p)]
        for q in range(p):
            mine[n + q].start()
        first = []
        for t in range(n):
            mine[t].start()
            first.append(copy(t, 0, me, sibling, own=True))
            first += [copy(t, 1 + j, me, (*chip, c), own=True) for j, chip in enumerate(chips)]
        for cp in first:
            cp.start()
        passed = []
        for j, chip in enumerate(chips):
            for t in range(n):
                copy(t, 1 + j, (*chip, c), me).wait_recv()
                fwd = copy(t, 4 + j, (*chip, c), sibling)
                fwd.start()
                passed.append(fwd)
        for t in range(n):
            copy(t, 0, sibling, me).wait_recv()
            for j, chip in enumerate(chips):
                copy(t, 4 + j, (*chip, 1 - c), me).wait_recv()
        for cp in first + passed:
            cp.wait_send()
        for cp in mine:
            cp.wait()

    out_shape = [jax.ShapeDtypeStruct(s.shape[:-2] + (N_DEV * s.shape[-2], s.shape[-1]), s.dtype)
                 for s in list(shards) + list(place)]
    res = pl.pallas_call(
        body, name=name, in_specs=[ANY] * (n + p), out_specs=[ANY] * (n + p), out_shape=out_shape,
        scratch_shapes=[pltpu.SemaphoreType.DMA((n, 7)), pltpu.SemaphoreType.DMA((n, 7)),
                        pltpu.SemaphoreType.DMA((n + p,))],
    )(*shards, *place)
    return res[:n], res[n:]


def _split_start(bufs, n_copies, plan, *, name, after=()):
    n = len(bufs)

    def body(*refs):
        token = refs[-1]
        for cp in plan(refs[:n], refs[n], refs[n + 1]):
            cp.start()
        token[...] = jnp.zeros_like(token)

    res = pl.pallas_call(
        _behind(body, n, after), name=name, in_specs=[HBM_SPEC] * n + [ANY] * len(after),
        out_specs=(SEM_SPEC, SEM_SPEC, *[HBM_SPEC] * n, pl.BlockSpec(memory_space=pltpu.VMEM)),
        out_shape=(pltpu.SemaphoreType.DMA((n_copies,)), pltpu.SemaphoreType.DMA((n_copies,)),
                   *[pltpu.HBM(b.shape, b.dtype) for b in bufs], jax.ShapeDtypeStruct((8, 128), F32)),
        input_output_aliases={i: 2 + i for i in range(n)},
        compiler_params=pltpu.CompilerParams(has_side_effects=DATAFLOW),
    )(*[pltpu.with_memory_space_constraint(b, pltpu.HBM) for b in bufs], *after)
    return res[0], res[1], list(res[2:2 + n]), res[-1]


def _split_wait(send_sems, recv_sems, bufs, after, plan, *, name):
    n = len(bufs)

    def body(*refs):
        for cp in plan(refs[:n], refs[n], refs[n + 1]):
            cp.wait_send()
            cp.wait_recv()

    return list(pl.pallas_call(
        body, name=name, in_specs=[HBM_SPEC] * n + [SEM_SPEC, SEM_SPEC, ANY], out_specs=[HBM_SPEC] * n,
        out_shape=tuple(pltpu.HBM(b.shape, b.dtype) for b in bufs),
        input_output_aliases={i: i for i in range(n)},
        compiler_params=pltpu.CompilerParams(has_side_effects=DATAFLOW),
    )(*bufs, send_sems, recv_sems, after))


def _sibling_plan(n):
    def plan(bufs, send_sems, recv_sems):
        x, y, c, _ = _place()
        return [pltpu.make_async_remote_copy(
            src_ref=bufs[t].at[:, 1 - c], dst_ref=bufs[n + t], send_sem=send_sems.at[t], recv_sem=recv_sems.at[t],
            device_id=(x, y, 1 - c), device_id_type=MESH) for t in range(n)]
    return plan


def _block_rows(ref, r, blk):
    start = pl.multiple_of(blk * r, 16 if r % 16 == 0 else 8)
    return ref.at[(slice(None),) * (len(ref.shape) - 2) + (pl.ds(start, r), slice(None))]


def _remote(src, dst, send_sems, recv_sems, k, peer):
    return pltpu.make_async_remote_copy(src_ref=src, dst_ref=dst, send_sem=send_sems.at[k], recv_sem=recv_sems.at[k],
                                        device_id=peer, device_id_type=MESH)


def _gather_send_plan(n):
    def plan(bufs, send_sems, recv_sems):
        x, y, c, chips = _place()
        peers = [(x, y, 1 - c)] + [(px, py, c) for px, py in chips]
        copies = []
        for t in range(n):
            dst = _block_rows(bufs[n + t], bufs[t].shape[-2], 4 * x + 2 * y + c)
            copies += [_remote(bufs[t], dst, send_sems, recv_sems, 4 * t + k, peer) for k, peer in enumerate(peers)]
        return copies
    return plan


def _gather_forward_plan(rows):
    def plan(bufs, send_sems, recv_sems):
        x, y, c, chips = _place()
        copies = []
        for t, r in enumerate(rows):
            for j, (px, py) in enumerate(chips):
                blk = _block_rows(bufs[t], r, 4 * px + 2 * py + c)
                copies.append(_remote(blk, blk, send_sems, recv_sems, 3 * t + j, (x, y, 1 - c)))
        return copies
    return plan


def _chips_plan(n, with_small):
    def plan(bufs, send_sems, recv_sems):
        x, y, c, chips = _place()
        copies = []
        for t in range(n):
            for j, (px, py) in enumerate(chips):
                copies.append(_remote(bufs[t].at[2 * px + py], bufs[n + t].at[j], send_sems, recv_sems, 3 * t + j,
                                      (px, py, c)))
        if with_small:
            mine = _block_rows(bufs[2 * n], 8, 4 * x + 2 * y + c)
            flips = [(fx, fy, fc) for fx in range(2) for fy in range(2) for fc in range(2)][1:]
            for k, (fx, fy, fc) in enumerate(flips):
                peer = (x + fx - 2 * x * fx, y + fy - 2 * y * fy, c + fc - 2 * c * fc)
                copies.append(_remote(mine, mine, send_sems, recv_sems, 3 * n + k, peer))
        return copies
    return plan


def _place_own(fulls, shards, *, name):
    n = len(fulls)

    def body(*refs):
        shard_refs, outs, sems = refs[n:2 * n], refs[2 * n:3 * n], refs[3 * n]
        x, y, c, _ = _place()
        copies = [pltpu.make_async_copy(shard_refs[t], _block_rows(outs[t], shard_refs[t].shape[-2], 4 * x + 2 * y + c),
                                        sems.at[t]) for t in range(n)]
        for cp in copies:
            cp.start()
        for cp in copies:
            cp.wait()

    return list(pl.pallas_call(
        body, name=name, in_specs=[ANY] * (2 * n), out_specs=[ANY] * n,
        out_shape=[jax.ShapeDtypeStruct(f.shape, f.dtype) for f in fulls],
        input_output_aliases={t: t for t in range(n)},
        scratch_shapes=[pltpu.SemaphoreType.DMA((n,))],
    )(*fulls, *shards))


def _add_sibling(grad, recv, core, *, name, tr):
    rows = grad.shape[2]

    def body(core_ref, g_ref, r_ref, o_ref, ob_ref):
        p = g_ref[:, 0] + r_ref[...]
        o_ref[...] = p
        ob_ref[...] = p.astype(BF16)

    out = pl.BlockSpec((4, tr, D_MODEL), lambda i, core_ref: (0, i, 0))
    return pl.pallas_call(
        body, name=name,
        grid_spec=pltpu.PrefetchScalarGridSpec(
            num_scalar_prefetch=1, grid=(rows // tr,),
            in_specs=[pl.BlockSpec((4, 1, tr, D_MODEL), lambda i, core_ref: (0, core_ref[0], i, 0)), out],
            out_specs=[out, out]),
        out_shape=[jax.ShapeDtypeStruct(recv.shape, F32), jax.ShapeDtypeStruct(recv.shape, BF16)],
        compiler_params=_params(("arbitrary",)),
    )(core, grad, recv)


def _add_chips(partial, recv, chip, *, name, tr):
    rows = partial.shape[1]

    def body(chip_ref, p_ref, r_ref, o_ref):
        o_ref[...] = p_ref[0] + r_ref[0].astype(F32) + r_ref[1].astype(F32) + r_ref[2].astype(F32)

    return pl.pallas_call(
        body, name=name,
        grid_spec=pltpu.PrefetchScalarGridSpec(
            num_scalar_prefetch=1, grid=(rows // tr,),
            in_specs=[pl.BlockSpec((1, tr, D_MODEL), lambda i, chip_ref: (chip_ref[0], i, 0)),
                      pl.BlockSpec((3, tr, D_MODEL), lambda i, chip_ref: (0, i, 0))],
            out_specs=pl.BlockSpec((tr, D_MODEL), lambda i, chip_ref: (i, 0))),
        out_shape=jax.ShapeDtypeStruct((rows, D_MODEL), F32),
        compiler_params=_params(("arbitrary",)),
    )(chip, partial, recv)


def _adamw_math(w, g, m, v):
    m = ADAM_B1 * m + (1.0 - ADAM_B1) * g
    v = ADAM_B2 * v + (1.0 - ADAM_B2) * (g * g)
    m_hat = m / (1.0 - ADAM_B1 ** ADAM_STEP)
    v_hat = v / (1.0 - ADAM_B2 ** ADAM_STEP)
    delta = -ADAM_LR * (m_hat / (jnp.sqrt(v_hat) + ADAM_EPS) + ADAM_WD * w)
    return delta, m, v


def _adamw(w, g, m, v, *, name, tr, after=()):
    rows, cols = w.shape

    def body(w_ref, g_ref, m_ref, v_ref, d_ref, mo_ref, vo_ref):
        d_ref[...], mo_ref[...], vo_ref[...] = _adamw_math(w_ref[...], g_ref[...], m_ref[...], v_ref[...])

    spec = pl.BlockSpec((tr, cols), lambda i: (i, 0))
    return pl.pallas_call(
        _behind(body, 4, after), name=name, grid=(rows // tr,), in_specs=[spec] * 4 + [ANY] * len(after),
        out_specs=[spec] * 3, out_shape=[jax.ShapeDtypeStruct(w.shape, F32)] * 3,
        compiler_params=_params(("parallel",)),
    )(w, g, m, v, *after)


def _sum_small(gathered, *, name):
    def body(g_ref, o_ref):
        acc = g_ref[0]
        for k in range(1, N_DEV):
            acc = acc + g_ref[k]
        o_ref[...] = acc

    return pl.pallas_call(body, name=name, out_shape=jax.ShapeDtypeStruct(gathered.shape[1:], F32))(gathered)


def kernel(x, ffn1_norm, ffn1_w_gate, ffn1_w_up, ffn1_w_down, mix_norm, w_in, conv_w, attn_sinks, w_out, ffn2_norm, ffn2_w_gate, ffn2_w_up, ffn2_w_down, final_norm, loss_target, m_ffn1_norm, m_ffn1_w_gate, m_ffn1_w_up, m_ffn1_w_down, m_mix_norm, m_w_in, m_conv_w, m_attn_sinks, m_w_out, m_ffn2_norm, m_ffn2_w_gate, m_ffn2_w_up, m_ffn2_w_down, m_final_norm, v_ffn1_norm, v_ffn1_w_gate, v_ffn1_w_up, v_ffn1_w_down, v_mix_norm, v_w_in, v_conv_w, v_attn_sinks, v_w_out, v_ffn2_norm, v_ffn2_w_gate, v_ffn2_w_up, v_ffn2_w_down, v_final_norm):
    ix, iy, ic = lax.axis_index("x"), lax.axis_index("y"), lax.axis_index("c")
    my_index = 4 * ix + 2 * iy + ic
    core = ic.astype(jnp.int32).reshape(1)
    chip = (2 * ix + iy).astype(jnp.int32).reshape(1)

    given = dict(ffn1_norm=ffn1_norm, ffn1_w_gate=ffn1_w_gate, ffn1_w_up=ffn1_w_up, ffn1_w_down=ffn1_w_down,
                 mix_norm=mix_norm, w_in=w_in, conv_w=conv_w, attn_sinks=attn_sinks, w_out=w_out, ffn2_norm=ffn2_norm,
                 ffn2_w_gate=ffn2_w_gate, ffn2_w_up=ffn2_w_up, ffn2_w_down=ffn2_w_down, final_norm=final_norm)
    moments_m = dict(ffn1_norm=m_ffn1_norm, ffn1_w_gate=m_ffn1_w_gate, ffn1_w_up=m_ffn1_w_up, ffn1_w_down=m_ffn1_w_down,
                     mix_norm=m_mix_norm, w_in=m_w_in, conv_w=m_conv_w, attn_sinks=m_attn_sinks, w_out=m_w_out,
                     ffn2_norm=m_ffn2_norm, ffn2_w_gate=m_ffn2_w_gate, ffn2_w_up=m_ffn2_w_up, ffn2_w_down=m_ffn2_w_down,
                     final_norm=m_final_norm)
    moments_v = dict(ffn1_norm=v_ffn1_norm, ffn1_w_gate=v_ffn1_w_gate, ffn1_w_up=v_ffn1_w_up, ffn1_w_down=v_ffn1_w_down,
                     mix_norm=v_mix_norm, w_in=v_w_in, conv_w=v_conv_w, attn_sinks=v_attn_sinks, w_out=v_w_out,
                     ffn2_norm=v_ffn2_norm, ffn2_w_gate=v_ffn2_w_gate, ffn2_w_up=v_ffn2_w_up, ffn2_w_down=v_ffn2_w_down,
                     final_norm=v_final_norm)

    xs = x[0]
    target = loss_target[0]
    final_gain = final_norm.reshape(1, D_MODEL)

    def ffn_shard(wg, wu, wd):
        return jnp.stack([wg[0].T, wu[0].T, wd[0]]).astype(BF16)

    conv_cols = conv_w.shape[2]
    conv_shard = jnp.pad(conv_w[0], ((0, 5), (0, 128 - conv_cols)))
    rest_shards = [ffn_shard(ffn2_w_gate, ffn2_w_up, ffn2_w_down), w_in[0].T.astype(BF16), w_out[0].astype(BF16),
                   conv_shard]
    rest_rows = [s.shape[-2] for s in rest_shards]
    n_rest = len(rest_shards)
    (w1,), fulls = _all_gather_rows([ffn_shard(ffn1_w_gate, ffn1_w_up, ffn1_w_down)], rest_shards, name="gather_ffn1")

    send_plan = _gather_send_plan(n_rest)
    ssem, rsem, bufs, token = _split_start(rest_shards + list(fulls), 4 * n_rest, send_plan, name="gather_rest_start",
                                           after=[w1])
    x1, h1, a1, b1 = _ffn_fwd(xs, ffn1_norm, w1, name="ffn1_fwd", after=[token])
    bufs = _split_wait(ssem, rsem, bufs, x1, send_plan, name="gather_rest_wait")
    w2_part, mixer_parts = bufs[n_rest], bufs[n_rest + 1:]
    fwd_mixer = _gather_forward_plan(rest_rows[1:])
    ssem, rsem, bufs, token = _split_start(mixer_parts, 3 * (n_rest - 1), fwd_mixer, name="forward_mixer_start")
    win_t, wout, conv_all = _split_wait(ssem, rsem, bufs, token, fwd_mixer, name="forward_mixer_wait")
    conv_full = conv_all.reshape(N_DEV, 8, 128)[:, :3, :conv_cols].transpose(1, 0, 2).reshape(3, CONV_W)
    fwd_ffn2 = _gather_forward_plan(rest_rows[:1])
    ssem, rsem, bufs, token = _split_start([w2_part], 3, fwd_ffn2, name="forward_ffn2_start", after=[win_t])
    rope = _rope_tables(xs.shape[0])
    x2, hm, z, y = _mixer_fwd(x1, mix_norm, win_t, wout, conv_full, attn_sinks, rope, name="mixer_fwd", after=[token])
    (w2,) = _split_wait(ssem, rsem, bufs, x2, fwd_ffn2, name="forward_ffn2_wait")
    x3, h2, a2, b2 = _ffn_fwd(x2, ffn2_norm, w2, name="ffn2_fwd")
    dx3, loss_local, d_final = _loss_head(x3, final_gain, target, name="loss_head")
    loss = lax.psum(loss_local[0, 0], ("x", "y", "c"))

    def to_sibling_start(grads, tag, after=()):
        views = [g.reshape(4, 2, g.shape[0] // N_DEV, D_MODEL) for g in grads]
        lands = [lax.empty((4,) + v.shape[2:], F32) for v in views]
        plan = _sibling_plan(len(views))
        ssem, rsem, bufs, token = _split_start(views + lands, len(views), plan, name=f"{tag}_sibling_start", after=after)
        return (ssem, rsem, bufs, plan, tag), token

    def to_sibling_finish(handle, after, names):
        ssem, rsem, bufs, plan, tag = handle
        bufs = _split_wait(ssem, rsem, bufs, after, plan, name=f"{tag}_sibling_wait")
        n = len(names)
        return [_add_sibling(v, r, core, name=f"add_sibling_{nm}", tr=v.shape[2] // 2)
                for v, r, nm in zip(bufs[:n], bufs[n:], names)]

    def to_chips_start(partials, tag, small_all=None, after=()):
        p16 = [p for _, p in partials]
        lands = [lax.empty((3,) + p.shape[1:], BF16) for p in p16]
        extra = [] if small_all is None else [small_all]
        plan = _chips_plan(len(p16), small_all is not None)
        ssem, rsem, bufs, token = _split_start(p16 + lands + extra, 3 * len(p16) + 7 * len(extra), plan,
                                               name=f"{tag}_chips_start", after=after)
        return (ssem, rsem, bufs, plan, tag), token

    def to_chips_finish(handle, partials, after, names):
        ssem, rsem, bufs, plan, tag = handle
        bufs = _split_wait(ssem, rsem, bufs, after, plan, name=f"{tag}_chips_wait")
        n = len(names)
        reduced = [_add_chips(p32, r, chip, name=f"add_chips_{nm}", tr=p32.shape[1] // 2)
                   for (p32, _), r, nm in zip(partials, bufs[n:2 * n], names)]
        return reduced, bufs[2 * n:]

    half_ff = D_FF // 2
    names2, namesm = ["ffn2_w_gate", "ffn2_w_up", "ffn2_w_down"], ["w_in", "w_out"]
    transposed = {"ffn1_w_gate", "ffn1_w_up", "w_in", "ffn2_w_gate", "ffn2_w_up"}
    grad, delta, new_m, new_v = {}, {}, {}, {}

    def adam_big(nm, g_rows, after=()):
        g = g_rows.T if nm in transposed else g_rows
        shape = given[nm].shape
        w2d = given[nm][0]
        d, mo, vo = _adamw(w2d, g, moments_m[nm][0], moments_v[nm][0], name=f"adamw_{nm}", tr=w2d.shape[0] // 2,
                           after=after)
        grad[nm], delta[nm], new_m[nm], new_v[nm] = (a.reshape(shape) for a in (g, d, mo, vo))

    dx2, da2, db2, s2, g2b, d_norm2 = _ffn_dgrad(dx3, x2, ffn2_norm, a2, b2, w2, name="ffn2_dgrad")
    gw2 = [_tn_matmul(da2, h2, name="ffn2_wgrad_gate", bm=half_ff), _tn_matmul(db2, h2, name="ffn2_wgrad_up", bm=half_ff),
           _tn_matmul(s2, g2b, name="ffn2_wgrad_down", bm=half_ff)]
    sib2, tok = to_sibling_start(gw2, "ffn2")
    dx1, dz, gmb, d_conv, d_sink, d_normm = _mixer_bwd(dx2, x1, mix_norm, y, z, win_t, wout, conv_full, attn_sinks,
                                                       rope, name="mixer_bwd", after=[tok])
    p2 = to_sibling_finish(sib2, dx1, names2)
    chips2, tok = to_chips_start(p2, "ffn2")
    gwm = [_tn_matmul(dz, hm, name="mixer_wgrad_in", bm=Z_W // 3, after=[tok]),
           _tn_matmul(y, gmb, name="mixer_wgrad_out", bm=D_MODEL // 2, after=[tok])]
    sibm, tok = to_sibling_start(gwm, "mixer")
    dx0, da1, db1, s1, g1b, d_norm1 = _ffn_dgrad(dx1, xs, ffn1_norm, a1, b1, w1, name="ffn1_dgrad", after=[tok])
    r2, _ = to_chips_finish(chips2, p2, dx0, names2)
    pm = to_sibling_finish(sibm, dx0, namesm)
    chipsm, tok = to_chips_start(pm, "mixer")
    gw_gate = _tn_matmul(da1, h1, name="ffn1_wgrad_gate", bm=half_ff, after=[tok])
    sib_gate, tok = to_sibling_start([gw_gate], "ffn1_gate")
    gw_up = _tn_matmul(db1, h1, name="ffn1_wgrad_up", bm=half_ff, after=[tok])
    rm, _ = to_chips_finish(chipsm, pm, gw_up, namesm)
    p_gate = to_sibling_finish(sib_gate, gw_up, ["ffn1_w_gate"])
    chips_gate, tok_a = to_chips_start(p_gate, "ffn1_gate")
    sib_up, tok_b = to_sibling_start([gw_up], "ffn1_up", after=[tok_a])
    gw_down = _tn_matmul(s1, g1b, name="ffn1_wgrad_down", bm=half_ff, after=[tok_a, tok_b])
    p_up = to_sibling_finish(sib_up, gw_down, ["ffn1_w_up"])
    chips_up, tok_a = to_chips_start(p_up, "ffn1_up")
    sib_down, tok_b = to_sibling_start([gw_down], "ffn1_down", after=[tok_a])
    for nm, g in zip(names2 + namesm, r2 + rm):
        adam_big(nm, g, after=[tok_a, tok_b])
    r_gate, _ = to_chips_finish(chips_gate, p_gate, new_v["w_out"], ["ffn1_w_gate"])
    p_down = to_sibling_finish(sib_down, new_v["w_out"], ["ffn1_w_down"])
    small = jnp.concatenate([
        d_norm1, d_normm, d_norm2, d_final,
        jnp.pad(d_conv[0:3], ((0, 0), (0, D_MODEL - CONV_W))), jnp.pad(d_sink, ((0, 0), (0, D_MODEL - 128)))], axis=0)
    (small_all,) = _place_own([lax.empty((N_DEV * 8, D_MODEL), F32)], [small], name="place_own_small")
    chips_down, tok = to_chips_start(p_down, "ffn1_down", small_all)
    adam_big("ffn1_w_gate", r_gate[0], after=[tok])
    r_up, _ = to_chips_finish(chips_up, p_up, new_v["ffn1_w_gate"], ["ffn1_w_up"])
    adam_big("ffn1_w_up", r_up[0])
    r_down, (small_all,) = to_chips_finish(chips_down, p_down, new_v["ffn1_w_up"], ["ffn1_w_down"])
    adam_big("ffn1_w_down", r_down[0])
    small_sum = _sum_small(small_all.reshape(N_DEV, 8, D_MODEL), name="sum_small")
    _update_small(given, moments_m, moments_v, small_sum, my_index, grad, delta, new_m, new_v)

    order = list(given)
    return (loss, dx0[None], *[grad[n] for n in order], *[delta[n] for n in order],
            *[new_m[n] for n in order], *[new_v[n] for n in order])


def _update_small(given, moments_m, moments_v, small_sum, my_index, grad, delta, new_m, new_v):
    conv_cols = given["conv_w"].shape[2]
    small_g = {
        "ffn1_norm": small_sum[0:1], "mix_norm": small_sum[1:2], "ffn2_norm": small_sum[2:3],
        "final_norm": small_sum[3:4],
        "conv_w": lax.dynamic_slice(small_sum[4:7, :CONV_W], (0, my_index * conv_cols), (3, conv_cols)),
        "attn_sinks": small_sum[7:8, :N_Q_HEADS],
    }

    small_names = ["ffn1_norm", "mix_norm", "ffn2_norm", "final_norm", "conv_w", "attn_sinks"]

    def pack(parts):
        rows = []
        for nm in small_names:
            p = parts[nm]
            p2 = p.reshape(3, conv_cols) if nm == "conv_w" else p.reshape(1, -1)
            rows.append(jnp.pad(p2, ((0, 0), (0, D_MODEL - p2.shape[1]))))
        rows.append(jnp.zeros((8, D_MODEL), F32))
        return jnp.concatenate(rows, axis=0)

    sd, sm, sv = _adamw(pack(given), pack(small_g), pack(moments_m), pack(moments_v), name="adamw_small", tr=16)
    row = 0
    for nm in small_names:
        shape = given[nm].shape
        nrow = 3 if nm == "conv_w" else 1
        ncol = conv_cols if nm == "conv_w" else given[nm].size
        grad[nm] = small_g[nm].reshape(shape)
        delta[nm], new_m[nm], new_v[nm] = (a[row:row + nrow, :ncol].reshape(shape) for a in (sd, sm, sv))
        row += nrow
```

```python
import functools

import jax
import jax.numpy as jnp
from jax import lax
from jax.experimental import pallas as pl
from jax.experimental.pallas import tpu as pltpu

F32 = jnp.float32
BF16 = jnp.bfloat16
MESH = pl.DeviceIdType.MESH
ANY = pl.BlockSpec(memory_space=pl.ANY)
HBM_SPEC = pl.BlockSpec(memory_space=pltpu.HBM)
SEM_SPEC = pl.BlockSpec(memory_space=pltpu.SEMAPHORE)
DATAFLOW = pltpu.SideEffectType.DATAFLOW_SIDE_EFFECTING

N_DEV = 8
D_MODEL = 1024
D_FF = 2816
CONV_W = 512
ATTN_W = 512
KV_W = 128
HEAD_DIM = 64
N_Q_HEADS = 8
N_KV_HEADS = 2
Q_PER_KV = N_Q_HEADS // N_KV_HEADS
BLOCK = 128
ROT_DIM = 16
ROPE_THETA = 500000.0
Z_W = 3 * CONV_W + ATTN_W + 2 * KV_W
Q_OFF = 3 * CONV_W
K_OFF = Q_OFF + ATTN_W
V_OFF = K_OFF + KV_W
RMS_EPS = 1e-5
MASK_VALUE = -1e30
SM_SCALE = HEAD_DIM ** -0.5
FFN_RES_SCALE = 0.5

ADAM_LR = 0.001
ADAM_B1 = 0.9
ADAM_B2 = 0.999
ADAM_EPS = 1e-08
ADAM_WD = 0.01
ADAM_STEP = 10

NT_DIMS = (((1,), (1,)), ((), ()))
TN_DIMS = (((0,), (0,)), ((), ()))

VMEM_LIMIT = 56 * 1024 * 1024
FF_CHUNK = 256


def _params(sem, vmem=None):
    return pltpu.CompilerParams(dimension_semantics=sem, vmem_limit_bytes=vmem)


def _behind(body, n_in, after):
    k = len(after)
    if k == 0:
        return body
    return lambda *refs: body(*refs[:n_in], *refs[n_in + k:])


def _rms_stats(xf):
    inv = lax.rsqrt(jnp.mean(xf * xf, axis=-1, keepdims=True) + RMS_EPS)
    return xf * inv, inv


def _rms_bwd(dh, xhat, inv, gain):
    dxhat = dh * gain
    dx = inv * (dxhat - xhat * jnp.mean(dxhat * xhat, axis=-1, keepdims=True))
    dgain = jnp.sum(dh * xhat, axis=0, keepdims=True)
    return dx, dgain


def _load_resident(w_hbm, w_ref, sem):
    @pl.when(pl.program_id(0) == 0)
    def _():
        cp = pltpu.make_async_copy(w_hbm, w_ref, sem)
        cp.start()
        cp.wait()


def _ffn_fwd(x, gain, w3, *, name, after=(), tm=256, tf=FF_CHUNK):
    t = x.shape[0]
    tm = min(tm, t)

    def body(x_ref, g_ref, w_hbm, xo_ref, h_ref, a_ref, b_ref, w_ref, s_ref, sem):
        _load_resident(w_hbm, w_ref, sem)
        xf = x_ref[...]
        xhat, _ = _rms_stats(xf)
        h = (xhat * g_ref[...]).astype(BF16)
        h_ref[...] = h
        for c in range(0, D_FF, tf):
            a = lax.dot_general(h, w_ref[0, c:c + tf, :], NT_DIMS, preferred_element_type=F32)
            b = lax.dot_general(h, w_ref[1, c:c + tf, :], NT_DIMS, preferred_element_type=F32)
            a_ref[:, c:c + tf] = a.astype(BF16)
            b_ref[:, c:c + tf] = b.astype(BF16)
            s_ref[:, c:c + tf] = (a * jax.nn.sigmoid(a) * b).astype(BF16)
        xo_ref[...] = xf + FFN_RES_SCALE * jnp.dot(s_ref[...], w_ref[2], preferred_element_type=F32)

    row = pl.BlockSpec((tm, D_MODEL), lambda i: (i, 0))
    hid = pl.BlockSpec((tm, D_FF), lambda i: (i, 0))
    return pl.pallas_call(
        _behind(body, 3, after), name=name, grid=(t // tm,),
        in_specs=[row, pl.BlockSpec((1, D_MODEL), lambda i: (0, 0)), ANY] + [ANY] * len(after),
        out_specs=[row, row, hid, hid],
        out_shape=[jax.ShapeDtypeStruct((t, D_MODEL), F32), jax.ShapeDtypeStruct((t, D_MODEL), BF16),
                   jax.ShapeDtypeStruct((t, D_FF), BF16), jax.ShapeDtypeStruct((t, D_FF), BF16)],
        scratch_shapes=[pltpu.VMEM((3, D_FF, D_MODEL), BF16), pltpu.VMEM((tm, D_FF), BF16),
                        pltpu.SemaphoreType.DMA(())],
        compiler_params=_params(("arbitrary",), VMEM_LIMIT),
    )(x, gain, w3, *after)


def _ffn_dgrad(dxo, x, gain, a, b, w3, *, name, after=(), tm=256, tf=FF_CHUNK):
    t = x.shape[0]
    tm = min(tm, t)

    def body(dxo_ref, x_ref, g_ref, a_ref, b_ref, w_hbm, dxi_ref, da_ref, db_ref, s_ref, gb_ref, dg_ref, w_ref, sem):
        _load_resident(w_hbm, w_ref, sem)

        @pl.when(pl.program_id(0) == 0)
        def _():
            dg_ref[...] = jnp.zeros_like(dg_ref)

        go = dxo_ref[...]
        gb = (FFN_RES_SCALE * go).astype(BF16)
        gb_ref[...] = gb
        for c in range(0, D_FF, tf):
            ds = lax.dot_general(gb, w_ref[2, c:c + tf, :], NT_DIMS, preferred_element_type=F32)
            af = a_ref[:, c:c + tf].astype(F32)
            bf = b_ref[:, c:c + tf].astype(F32)
            sig = jax.nn.sigmoid(af)
            silu = af * sig
            da_ref[:, c:c + tf] = (ds * bf * (sig * (1.0 + af * (1.0 - sig)))).astype(BF16)
            db_ref[:, c:c + tf] = (ds * silu).astype(BF16)
            s_ref[:, c:c + tf] = (silu * bf).astype(BF16)
        dh = (jnp.dot(da_ref[...], w_ref[0], preferred_element_type=F32)
              + jnp.dot(db_ref[...], w_ref[1], preferred_element_type=F32))
        xhat, inv = _rms_stats(x_ref[...])
        dx, dgain = _rms_bwd(dh, xhat, inv, g_ref[...])
        dxi_ref[...] = go + dx
        dg_ref[...] += dgain

    row = pl.BlockSpec((tm, D_MODEL), lambda i: (i, 0))
    hid = pl.BlockSpec((tm, D_FF), lambda i: (i, 0))
    vec = pl.BlockSpec((1, D_MODEL), lambda i: (0, 0))
    return pl.pallas_call(
        _behind(body, 6, after), name=name, grid=(t // tm,),
        in_specs=[row, row, vec, hid, hid, ANY] + [ANY] * len(after),
        out_specs=[row, hid, hid, hid, row, vec],
        out_shape=[jax.ShapeDtypeStruct((t, D_MODEL), F32), jax.ShapeDtypeStruct((t, D_FF), BF16),
                   jax.ShapeDtypeStruct((t, D_FF), BF16), jax.ShapeDtypeStruct((t, D_FF), BF16),
                   jax.ShapeDtypeStruct((t, D_MODEL), BF16), jax.ShapeDtypeStruct((1, D_MODEL), F32)],
        scratch_shapes=[pltpu.VMEM((3, D_FF, D_MODEL), BF16), pltpu.SemaphoreType.DMA(())],
        compiler_params=_params(("arbitrary",), VMEM_LIMIT),
    )(dxo, x, gain, a, b, w3, *after)


def _tn_matmul(a, b, *, name, bm, after=(), tk=1024):
    t, m = a.shape
    n = b.shape[1]
    tk = min(tk, t)
    nk = t // tk

    def body(a_ref, b_ref, o_ref):
        k = pl.program_id(1)
        p = lax.dot_general(a_ref[...], b_ref[...], TN_DIMS, preferred_element_type=F32)

        @pl.when(k == 0)
        def _():
            o_ref[...] = p

        @pl.when(k > 0)
        def _():
            o_ref[...] += p

    return pl.pallas_call(
        _behind(body, 2, after), name=name, grid=(m // bm, nk),
        in_specs=[pl.BlockSpec((tk, bm), lambda i, k: (k, i)), pl.BlockSpec((tk, n), lambda i, k: (k, 0))]
        + [ANY] * len(after),
        out_specs=pl.BlockSpec((bm, n), lambda i, k: (i, 0)),
        out_shape=jax.ShapeDtypeStruct((m, n), F32),
        compiler_params=_params(("parallel", "arbitrary"), VMEM_LIMIT),
    )(a, b, *after)


def _loss_head(x, gain, target, *, name, tm=512):
    t = x.shape[0]
    tm = min(tm, t)

    def body(x_ref, g_ref, t_ref, dx_ref, loss_ref, dg_ref):
        @pl.when(pl.program_id(0) == 0)
        def _():
            loss_ref[...] = jnp.zeros_like(loss_ref)
            dg_ref[...] = jnp.zeros_like(dg_ref)

        xhat, inv = _rms_stats(x_ref[...])
        err = xhat * g_ref[...] - t_ref[...]
        loss_ref[...] += 0.5 * jnp.sum(jnp.mean(err * err, axis=-1, keepdims=True), axis=0, keepdims=True)
        dx, dgain = _rms_bwd(err * (1.0 / D_MODEL), xhat, inv, g_ref[...])
        dx_ref[...] = dx
        dg_ref[...] += dgain

    row = pl.BlockSpec((tm, D_MODEL), lambda i: (i, 0))
    vec = pl.BlockSpec((1, D_MODEL), lambda i: (0, 0))
    return pl.pallas_call(
        body, name=name, grid=(t // tm,),
        in_specs=[row, vec, row],
        out_specs=[row, pl.BlockSpec((1, 1), lambda i: (0, 0)), vec],
        out_shape=[jax.ShapeDtypeStruct((t, D_MODEL), F32), jax.ShapeDtypeStruct((1, 1), F32),
                   jax.ShapeDtypeStruct((1, D_MODEL), F32)],
        compiler_params=_params(("arbitrary",)),
    )(x, gain, target)


def _rope_tables(t):
    half = ROT_DIM // 2
    inv_freq = ROPE_THETA ** (-jnp.arange(0, ROT_DIM, 2, dtype=F32) / ROT_DIM)
    ang = jnp.arange(t, dtype=F32)[:, None] * inv_freq[None, :]
    cos, sin = jnp.cos(ang), jnp.sin(ang)
    zeros, ones = jnp.zeros((t, half), F32), jnp.ones((t, HEAD_DIM - ROT_DIM), F32)
    pad = jnp.zeros((t, HEAD_DIM - ROT_DIM), F32)
    mult = jnp.concatenate([cos, cos, ones], axis=1)
    from_lo = jnp.concatenate([zeros, sin, pad], axis=1)
    from_hi = jnp.concatenate([-sin, zeros, pad], axis=1)
    return jnp.stack([jnp.tile(m, (1, 2)) for m in (mult, from_lo, from_hi)])


def _tile_lanes(tab, width):
    return jnp.tile(tab, (1, width // tab.shape[1]))


def _rope(v, tab):
    w = v.shape[1]
    half_rot = ROT_DIM // 2
    return (v * _tile_lanes(tab[0], w)
            + pltpu.roll(v, half_rot, axis=1) * _tile_lanes(tab[1], w)
            + pltpu.roll(v, w - half_rot, axis=1) * _tile_lanes(tab[2], w))


def _rope_bwd(dv, tab):
    w = dv.shape[1]
    half_rot = ROT_DIM // 2
    return (dv * _tile_lanes(tab[0], w)
            + pltpu.roll(dv * _tile_lanes(tab[1], w), w - half_rot, axis=1)
            + pltpu.roll(dv * _tile_lanes(tab[2], w), half_rot, axis=1))


def _shift_rows(v, prev8_ref, n):
    r = lax.broadcasted_iota(jnp.int32, v.shape, 0)
    rolled = pltpu.roll(v, n, axis=0)
    last = prev8_ref[7:8, :]
    if n == 1:
        return jnp.where(r >= 1, rolled, last)
    return jnp.where(r >= 2, rolled, jnp.where(r == 0, prev8_ref[6:7, :], last))


def _shift_rows_up(v, next8_ref, n):
    rows = v.shape[0]
    r = lax.broadcasted_iota(jnp.int32, v.shape, 0)
    rolled = pltpu.roll(v, rows - n, axis=0)
    first = next8_ref[0:1, :]
    if n == 1:
        return jnp.where(r <= rows - 2, rolled, first)
    return jnp.where(r <= rows - 3, rolled, jnp.where(r == rows - 2, first, next8_ref[1:2, :]))


def _lane_half_mask(shape, half):
    lane = lax.broadcasted_iota(jnp.int32, shape, 1)
    return (lane >= HEAD_DIM) if half else (lane < HEAD_DIM)


def _to_kv_lanes(chunk, head, kv):
    if head % 2 != kv:
        chunk = pltpu.roll(chunk, HEAD_DIM, axis=1)
    return jnp.where(_lane_half_mask(chunk.shape, kv), chunk, 0.0)


def _from_kv_lanes(chunk, head, kv):
    chunk = jnp.where(_lane_half_mask(chunk.shape, kv), chunk, 0.0)
    if head % 2 != kv:
        chunk = pltpu.roll(chunk, HEAD_DIM, axis=1)
    return chunk


def _stack_heads(wide, kv):
    parts = []
    for g in range(Q_PER_KV):
        head = kv * Q_PER_KV + g
        chunk = wide[:, (head // 2) * 128:(head // 2 + 1) * 128]
        parts.append(_to_kv_lanes(chunk, head, kv))
    return jnp.concatenate(parts, axis=0)


def _window_mask(has_prev):
    shape = (Q_PER_KV * BLOCK, 2 * BLOCK)
    qi = lax.broadcasted_iota(jnp.int32, shape, 0) & (BLOCK - 1)
    kj = lax.broadcasted_iota(jnp.int32, shape, 1)
    first_key = BLOCK - has_prev * BLOCK
    in_prev = (kj < BLOCK) & (kj > qi) & (kj >= first_key)
    in_own = (kj >= BLOCK) & ((kj - BLOCK) <= qi)
    return in_prev | in_own


def _sink_column(sink_ref, kv):
    row = lax.broadcasted_iota(jnp.int32, (Q_PER_KV * BLOCK, 1), 0)
    col = jnp.full((Q_PER_KV * BLOCK, 1), sink_ref[0, kv * Q_PER_KV], F32)
    for g in range(1, Q_PER_KV):
        col = jnp.where(row >= g * BLOCK, sink_ref[0, kv * Q_PER_KV + g], col)
    return col


def _softmax_with_sink(q4, k2, mask, sink):
    s = lax.dot_general(q4, k2, NT_DIMS, preferred_element_type=F32) * SM_SCALE
    s = jnp.where(mask, s, MASK_VALUE)
    m = jnp.maximum(jnp.max(s, axis=-1, keepdims=True), sink)
    p = jnp.exp(s - m)
    e_sink = jnp.exp(sink - m)
    inv_den = 1.0 / (jnp.sum(p, axis=-1, keepdims=True) + e_sink)
    return p * inv_den, e_sink * inv_den


def _conv_terms(zf, prev8_ref, w_ref):
    b_gate, c_gate, u = zf[:, 0:CONV_W], zf[:, CONV_W:2 * CONV_W], zf[:, 2 * CONV_W:3 * CONV_W]
    vc = c_gate * u
    vm1 = _shift_rows(vc, prev8_ref, 1)
    vm2 = _shift_rows(vc, prev8_ref, 2)
    conv = w_ref[0:1, :] * vm2 + w_ref[1:2, :] * vm1 + w_ref[2:3, :] * vc
    return b_gate, c_gate, u, vc, vm1, vm2, conv


def _mixer_fwd(x, gain, win_t, wout, conv_w, sinks, rope, *, name, after=(), tq=512):
    t = x.shape[0]
    tq = min(tq, t)
    nblk = tq // BLOCK

    def body(x_ref, g_ref, win_hbm, wout_hbm, cw_ref, sink_ref, rope_ref,
             xo_ref, h_ref, z_ref, y_ref, kprev_ref, vprev_ref, cprev_ref, win_ref, wout_ref, sems):
        i = pl.program_id(0)
        _load_resident(win_hbm, win_ref, sems.at[0])
        _load_resident(wout_hbm, wout_ref, sems.at[1])

        @pl.when(i == 0)
        def _():
            kprev_ref[...] = jnp.zeros_like(kprev_ref)
            vprev_ref[...] = jnp.zeros_like(vprev_ref)
            cprev_ref[...] = jnp.zeros_like(cprev_ref)

        xf = x_ref[...]
        xhat, _ = _rms_stats(xf)
        h = (xhat * g_ref[...]).astype(BF16)
        h_ref[...] = h
        zb = lax.dot_general(h, win_ref[...], NT_DIMS, preferred_element_type=F32).astype(BF16)
        z_ref[...] = zb
        zf = zb.astype(F32)

        b_gate, _, _, vc, _, _, conv = _conv_terms(zf, cprev_ref, cw_ref)
        y_conv = b_gate * conv
        cprev_ref[...] = vc[tq - 8:tq, :]

        tab = rope_ref[...]
        qr = _rope(zf[:, Q_OFF:K_OFF], tab)
        kr = _rope(zf[:, K_OFF:V_OFF], tab).astype(BF16)
        vb = zb[:, V_OFF:Z_W]

        y_attn = []
        for j in range(nblk):
            rows = slice(j * BLOCK, (j + 1) * BLOCK)
            prev = slice((j - 1) * BLOCK, j * BLOCK)
            k2 = jnp.concatenate([kprev_ref[...] if j == 0 else kr[prev], kr[rows]], axis=0)
            v2 = jnp.concatenate([vprev_ref[...] if j == 0 else vb[prev], vb[rows]], axis=0)
            mask = _window_mask(jnp.minimum(i, 1) if j == 0 else 1)
            chunks = [jnp.zeros((BLOCK, 128), F32) for _ in range(ATTN_W // 128)]
            for kv in range(N_KV_HEADS):
                q4 = _stack_heads(qr[rows], kv).astype(BF16)
                probs, _ = _softmax_with_sink(q4, k2, mask, _sink_column(sink_ref, kv))
                o4 = jnp.dot(probs.astype(BF16), v2, preferred_element_type=F32)
                for g in range(Q_PER_KV):
                    head = kv * Q_PER_KV + g
                    chunks[head // 2] += _from_kv_lanes(o4[g * BLOCK:(g + 1) * BLOCK], head, kv)
            y_attn.append(jnp.concatenate(chunks, axis=1))
        kprev_ref[...] = kr[tq - BLOCK:tq]
        vprev_ref[...] = vb[tq - BLOCK:tq]
        y = jnp.concatenate([y_conv, jnp.concatenate(y_attn, axis=0)], axis=1).astype(BF16)
        y_ref[...] = y
        xo_ref[...] = xf + jnp.dot(y, wout_ref[...], preferred_element_type=F32)

    row = pl.BlockSpec((tq, D_MODEL), lambda i: (i, 0))
    full = lambda shape: pl.BlockSpec(shape, lambda i: (0,) * len(shape))
    return pl.pallas_call(
        _behind(body, 7, after), name=name, grid=(t // tq,),
        in_specs=[row, full((1, D_MODEL)), ANY, ANY, full((3, CONV_W)),
                  pl.BlockSpec(memory_space=pltpu.SMEM), pl.BlockSpec((3, tq, 128), lambda i: (0, i, 0))]
        + [ANY] * len(after),
        out_specs=[row, row, pl.BlockSpec((tq, Z_W), lambda i: (i, 0)), row],
        out_shape=[jax.ShapeDtypeStruct((t, D_MODEL), F32), jax.ShapeDtypeStruct((t, D_MODEL), BF16),
                   jax.ShapeDtypeStruct((t, Z_W), BF16), jax.ShapeDtypeStruct((t, D_MODEL), BF16)],
        scratch_shapes=[pltpu.VMEM((BLOCK, KV_W), BF16), pltpu.VMEM((BLOCK, KV_W), BF16),
                        pltpu.VMEM((8, CONV_W), F32), pltpu.VMEM((Z_W, D_MODEL), BF16),
                        pltpu.VMEM((D_MODEL, D_MODEL), BF16), pltpu.SemaphoreType.DMA((2,))],
        compiler_params=_params(("arbitrary",), VMEM_LIMIT),
    )(x, gain, win_t, wout, conv_w, sinks, rope, *after)


def _mixer_bwd(dxo, x, gain, y, z, win_t, wout, conv_w, sinks, rope, *, name, after=(), tq=256):
    t = x.shape[0]
    tq = min(tq, t)
    nt, nblk = t // tq, tq // BLOCK

    def body(dxo_ref, x_ref, g_ref, y_ref, z_ref, zp_ref, win_hbm, wout_hbm, cw_ref, sink_ref, rope_ref, ropep_ref,
             dxi_ref, dz_ref, gb_ref, dcw_ref, dsink_ref, dg_ref, dk_ref, dv_ref, dcn_ref, pvc_ref,
             win_ref, wout_ref, sems):
        i = pl.program_id(0)
        tile = nt - 1 - i
        _load_resident(win_hbm, win_ref, sems.at[0])
        _load_resident(wout_hbm, wout_ref, sems.at[1])

        @pl.when(i == 0)
        def _():
            dk_ref[...] = jnp.zeros_like(dk_ref)
            dv_ref[...] = jnp.zeros_like(dv_ref)
            dcn_ref[...] = jnp.zeros_like(dcn_ref)
            dcw_ref[...] = jnp.zeros_like(dcw_ref)
            dsink_ref[...] = jnp.zeros_like(dsink_ref)
            dg_ref[...] = jnp.zeros_like(dg_ref)

        has_prev = jnp.minimum(tile, 1)
        go = dxo_ref[...]
        gb = go.astype(BF16)
        gb_ref[...] = gb
        dy = lax.dot_general(gb, wout_ref[...], NT_DIMS, preferred_element_type=F32)
        dy_conv, dy_attn = dy[:, 0:CONV_W], dy[:, CONV_W:D_MODEL]
        zb, zpb = z_ref[...], zp_ref[...]
        zf = zb.astype(F32)
        zpf = zpb.astype(F32) * has_prev.astype(F32)

        pvc_ref[...] = (zpf[:, CONV_W:2 * CONV_W] * zpf[:, 2 * CONV_W:3 * CONV_W])[BLOCK - 8:BLOCK, :]
        b_gate, c_gate, u, vc, vm1, vm2, conv = _conv_terms(zf, pvc_ref, cw_ref)
        d_bgate = dy_conv * conv
        dc = dy_conv * b_gate
        tap = lax.broadcasted_iota(jnp.int32, (8, CONV_W), 0)
        dcw_ref[...] += jnp.where(tap == 0, jnp.sum(dc * vm2, axis=0, keepdims=True),
                                  jnp.where(tap == 1, jnp.sum(dc * vm1, axis=0, keepdims=True),
                                            jnp.where(tap == 2, jnp.sum(dc * vc, axis=0, keepdims=True), 0.0)))
        dvc = (cw_ref[2:3, :] * dc + cw_ref[1:2, :] * _shift_rows_up(dc, dcn_ref, 1)
               + cw_ref[0:1, :] * _shift_rows_up(dc, dcn_ref, 2))
        dcn_ref[...] = dc[0:8, :]
        d_cgate = dvc * u
        d_u = dvc * c_gate

        tab, tabp = rope_ref[...], ropep_ref[...]
        qr = _rope(zf[:, Q_OFF:K_OFF], tab)
        kr = _rope(zf[:, K_OFF:V_OFF], tab).astype(BF16)
        kpr = _rope(zpf[:, K_OFF:V_OFF], tabp).astype(BF16)
        vb, vpb = zb[:, V_OFF:Z_W], zpb[:, V_OFF:Z_W]
        out = y_ref[:, CONV_W:D_MODEL].astype(F32)
        do_out = dy_attn * out
        lane = lax.broadcasted_iota(jnp.int32, (1, 128), 1)
        dsink = jnp.zeros((1, 128), F32)
        dk_next, dv_next = dk_ref[...], dv_ref[...]
        dq_rows, dk_rows, dv_rows = [None] * nblk, [None] * nblk, [None] * nblk
        for j in reversed(range(nblk)):
            rows = slice(j * BLOCK, (j + 1) * BLOCK)
            prev = slice((j - 1) * BLOCK, j * BLOCK)
            k2 = jnp.concatenate([kpr if j == 0 else kr[prev], kr[rows]], axis=0)
            v2 = jnp.concatenate([vpb if j == 0 else vb[prev], vb[rows]], axis=0)
            mask = _window_mask(has_prev if j == 0 else 1)
            dk2 = jnp.zeros((2 * BLOCK, KV_W), F32)
            dv2 = jnp.zeros((2 * BLOCK, KV_W), F32)
            dq_chunks = [jnp.zeros((BLOCK, 128), F32) for _ in range(ATTN_W // 128)]
            for kv in range(N_KV_HEADS):
                q4 = _stack_heads(qr[rows], kv).astype(BF16)
                do4 = _stack_heads(dy_attn[rows], kv).astype(BF16)
                delta = jnp.sum(_stack_heads(do_out[rows], kv), axis=-1, keepdims=True)
                probs, p_sink = _softmax_with_sink(q4, k2, mask, _sink_column(sink_ref, kv))
                dp = lax.dot_general(do4, v2, NT_DIMS, preferred_element_type=F32)
                ds = (probs * (dp - delta) * SM_SCALE).astype(BF16)
                dq4 = jnp.dot(ds, k2, preferred_element_type=F32)
                dk2 += lax.dot_general(ds, q4, TN_DIMS, preferred_element_type=F32)
                dv2 += lax.dot_general(probs.astype(BF16), do4, TN_DIMS, preferred_element_type=F32)
                sink_terms = p_sink * delta
                for g in range(Q_PER_KV):
                    head = kv * Q_PER_KV + g
                    grp = slice(g * BLOCK, (g + 1) * BLOCK)
                    dq_chunks[head // 2] += _from_kv_lanes(dq4[grp], head, kv)
                    dsink = dsink - jnp.where(lane == head, jnp.sum(sink_terms[grp], axis=0, keepdims=True), 0.0)
            dq_rows[j] = jnp.concatenate(dq_chunks, axis=1)
            dk_rows[j] = dk2[BLOCK:] + dk_next
            dv_rows[j] = dv2[BLOCK:] + dv_next
            dk_next, dv_next = dk2[:BLOCK], dv2[:BLOCK]
        dk_ref[...] = dk_next
        dv_ref[...] = dv_next
        dsink_ref[...] += dsink
        dq = _rope_bwd(jnp.concatenate(dq_rows, axis=0), tab)
        dk = _rope_bwd(jnp.concatenate(dk_rows, axis=0), tab)
        dv = jnp.concatenate(dv_rows, axis=0)

        dzb = jnp.concatenate([d_bgate, d_cgate, d_u, dq, dk, dv], axis=1).astype(BF16)
        dz_ref[...] = dzb
        dh = jnp.dot(dzb, win_ref[...], preferred_element_type=F32)
        xhat, inv = _rms_stats(x_ref[...])
        dx, dgain = _rms_bwd(dh, xhat, inv, g_ref[...])
        dxi_ref[...] = go + dx
        dg_ref[...] += dgain

    rev = lambda i: (nt - 1 - i, 0)
    block_before = lambda i: jnp.maximum((nt - 1 - i) * nblk - 1, 0)
    row = pl.BlockSpec((tq, D_MODEL), rev)
    full = lambda shape: pl.BlockSpec(shape, lambda i: (0,) * len(shape))
    return pl.pallas_call(
        _behind(body, 12, after), name=name, grid=(nt,),
        in_specs=[row, row, full((1, D_MODEL)), row,
                  pl.BlockSpec((tq, Z_W), rev), pl.BlockSpec((BLOCK, Z_W), lambda i: (block_before(i), 0)),
                  ANY, ANY, full((3, CONV_W)),
                  pl.BlockSpec(memory_space=pltpu.SMEM),
                  pl.BlockSpec((3, tq, 128), lambda i: (0, nt - 1 - i, 0)),
                  pl.BlockSpec((3, BLOCK, 128), lambda i: (0, block_before(i), 0))] + [ANY] * len(after),
        out_specs=[row, pl.BlockSpec((tq, Z_W), rev), row, full((8, CONV_W)), full((1, 128)), full((1, D_MODEL))],
        out_shape=[jax.ShapeDtypeStruct((t, D_MODEL), F32), jax.ShapeDtypeStruct((t, Z_W), BF16),
                   jax.ShapeDtypeStruct((t, D_MODEL), BF16), jax.ShapeDtypeStruct((8, CONV_W), F32),
                   jax.ShapeDtypeStruct((1, 128), F32), jax.ShapeDtypeStruct((1, D_MODEL), F32)],
        scratch_shapes=[pltpu.VMEM((BLOCK, KV_W), F32), pltpu.VMEM((BLOCK, KV_W), F32), pltpu.VMEM((8, CONV_W), F32),
                        pltpu.VMEM((8, CONV_W), F32), pltpu.VMEM((Z_W, D_MODEL), BF16),
                        pltpu.VMEM((D_MODEL, D_MODEL), BF16), pltpu.SemaphoreType.DMA((2,))],
        compiler_params=_params(("arbitrary",), VMEM_LIMIT),
    )(dxo, x, gain, y, z, z, win_t, wout, conv_w, sinks, rope, rope, *after)


def _place():
    x, y, c = lax.axis_index("x"), lax.axis_index("y"), lax.axis_index("c")
    other_chips = [(1 - x, y), (x, 1 - y), (1 - x, 1 - y)]
    return x, y, c, other_chips


def _all_gather_rows(shards, place=(), *, name):
    n, p = len(shards), len(place)

    def body(*refs):
        srcs, place_srcs = refs[:n], refs[n:n + p]
        outs, place_outs = refs[n + p:2 * n + p], refs[2 * n + p:2 * (n + p)]
        send_sems, recv_sems, local_sems = refs[2 * (n + p):]
        x, y, c, chips = _place()
        me, sibling = (x, y, c), (x, y, 1 - c)

        def rows(t, px, py, pc):
            r = srcs[t].shape[-2]
            start = pl.multiple_of((4 * px + 2 * py + pc) * r, 16 if r % 16 == 0 else 8)
            if len(srcs[t].shape) == 3:
                return outs[t].at[:, pl.ds(start, r), :]
            return outs[t].at[pl.ds(start, r), :]

        def copy(t, k, block, to, own=False):
            return pltpu.make_async_remote_copy(
                src_ref=srcs[t] if own else rows(t, *block), dst_ref=rows(t, *block),
                send_sem=send_sems.at[t, k], recv_sem=recv_sems.at[t, k], device_id=to, device_id_type=MESH)

        mine = [pltpu.make_async_copy(srcs[t], rows(t, *me), local_sems.at[t]) for t in range(n)]
        mine += [pltpu.make_async_copy(place_srcs[q],
                                       _block_rows(place_outs[q], place_srcs[q].shape[-2], 4 * x + 2 * y + c),
                                       local_sems.at[n + q]) for q in range(p)]
        for q in range(p):
            mine[n + q].start()
        first = []
        for t in range(n):
            mine[t].start()
            first.append(copy(t, 0, me, sibling, own=True))
            first += [copy(t, 1 + j, me, (*chip, c), own=True) for j, chip in enumerate(chips)]
        for cp in first:
            cp.start()
        passed = []
        for j, chip in enumerate(chips):
            for t in range(n):
                copy(t, 1 + j, (*chip, c), me).wait_recv()
                fwd = copy(t, 4 + j, (*chip, c), sibling)
                fwd.start()
                passed.append(fwd)
        for t in range(n):
            copy(t, 0, sibling, me).wait_recv()
            for j, chip in enumerate(chips):
                copy(t, 4 + j, (*chip, 1 - c), me).wait_recv()
        for cp in first + passed:
            cp.wait_send()
        for cp in mine:
            cp.wait()

    out_shape = [jax.ShapeDtypeStruct(s.shape[:-2] + (N_DEV * s.shape[-2], s.shape[-1]), s.dtype)
                 for s in list(shards) + list(place)]
    res = pl.pallas_call(
        body, name=name, in_specs=[ANY] * (n + p), out_specs=[ANY] * (n + p), out_shape=out_shape,
        scratch_shapes=[pltpu.SemaphoreType.DMA((n, 7)), pltpu.SemaphoreType.DMA((n, 7)),
                        pltpu.SemaphoreType.DMA((n + p,))],
    )(*shards, *place)
    return res[:n], res[n:]


def _split_start(bufs, n_copies, plan, *, name, after=()):
    n = len(bufs)

    def body(*refs):
        token = refs[-1]
        for cp in plan(refs[:n], refs[n], refs[n + 1]):
            cp.start()
        token[...] = jnp.zeros_like(token)

    res = pl.pallas_call(
        _behind(body, n, after), name=name, in_specs=[HBM_SPEC] * n + [ANY] * len(after),
        out_specs=(SEM_SPEC, SEM_SPEC, *[HBM_SPEC] * n, pl.BlockSpec(memory_space=pltpu.VMEM)),
        out_shape=(pltpu.SemaphoreType.DMA((n_copies,)), pltpu.SemaphoreType.DMA((n_copies,)),
                   *[pltpu.HBM(b.shape, b.dtype) for b in bufs], jax.ShapeDtypeStruct((8, 128), F32)),
        input_output_aliases={i: 2 + i for i in range(n)},
        compiler_params=pltpu.CompilerParams(has_side_effects=DATAFLOW),
    )(*[pltpu.with_memory_space_constraint(b, pltpu.HBM) for b in bufs], *after)
    return res[0], res[1], list(res[2:2 + n]), res[-1]


def _split_wait(send_sems, recv_sems, bufs, after, plan, *, name):
    n = len(bufs)

    def body(*refs):
        for cp in plan(refs[:n], refs[n], refs[n + 1]):
            cp.wait_send()
            cp.wait_recv()

    return list(pl.pallas_call(
        body, name=name, in_specs=[HBM_SPEC] * n + [SEM_SPEC, SEM_SPEC, ANY], out_specs=[HBM_SPEC] * n,
        out_shape=tuple(pltpu.HBM(b.shape, b.dtype) for b in bufs),
        input_output_aliases={i: i for i in range(n)},
        compiler_params=pltpu.CompilerParams(has_side_effects=DATAFLOW),
    )(*bufs, send_sems, recv_sems, after))


def _sibling_plan(n):
    def plan(bufs, send_sems, recv_sems):
        x, y, c, _ = _place()
        return [pltpu.make_async_remote_copy(
            src_ref=bufs[t].at[:, 1 - c], dst_ref=bufs[n + t], send_sem=send_sems.at[t], recv_sem=recv_sems.at[t],
            device_id=(x, y, 1 - c), device_id_type=MESH) for t in range(n)]
    return plan


def _block_rows(ref, r, blk):
    start = pl.multiple_of(blk * r, 16 if r % 16 == 0 else 8)
    return ref.at[(slice(None),) * (len(ref.shape) - 2) + (pl.ds(start, r), slice(None))]


def _remote(src, dst, send_sems, recv_sems, k, peer):
    return pltpu.make_async_remote_copy(src_ref=src, dst_ref=dst, send_sem=send_sems.at[k], recv_sem=recv_sems.at[k],
                                        device_id=peer, device_id_type=MESH)


def _gather_send_plan(n):
    def plan(bufs, send_sems, recv_sems):
        x, y, c, chips = _place()
        peers = [(x, y, 1 - c)] + [(px, py, c) for px, py in chips]
        copies = []
        for t in range(n):
            dst = _block_rows(bufs[n + t], bufs[t].shape[-2], 4 * x + 2 * y + c)
            copies += [_remote(bufs[t], dst, send_sems, recv_sems, 4 * t + k, peer) for k, peer in enumerate(peers)]
        return copies
    return plan


def _gather_forward_plan(rows):
    def plan(bufs, send_sems, recv_sems):
        x, y, c, chips = _place()
        copies = []
        for t, r in enumerate(rows):
            for j, (px, py) in enumerate(chips):
                blk = _block_rows(bufs[t], r, 4 * px + 2 * py + c)
                copies.append(_remote(blk, blk, send_sems, recv_sems, 3 * t + j, (x, y, 1 - c)))
        return copies
    return plan


def _chips_plan(n, with_small):
    def plan(bufs, send_sems, recv_sems):
        x, y, c, chips = _place()
        copies = []
        for t in range(n):
            for j, (px, py) in enumerate(chips):
                copies.append(_remote(bufs[t].at[2 * px + py], bufs[n + t].at[j], send_sems, recv_sems, 3 * t + j,
                                      (px, py, c)))
        if with_small:
            mine = _block_rows(bufs[2 * n], 8, 4 * x + 2 * y + c)
            flips = [(fx, fy, fc) for fx in range(2) for fy in range(2) for fc in range(2)][1:]
            for k, (fx, fy, fc) in enumerate(flips):
                peer = (x + fx - 2 * x * fx, y + fy - 2 * y * fy, c + fc - 2 * c * fc)
                copies.append(_remote(mine, mine, send_sems, recv_sems, 3 * n + k, peer))
        return copies
    return plan


def _place_own(fulls, shards, *, name):
    n = len(fulls)

    def body(*refs):
        shard_refs, outs, sems = refs[n:2 * n], refs[2 * n:3 * n], refs[3 * n]
        x, y, c, _ = _place()
        copies = [pltpu.make_async_copy(shard_refs[t], _block_rows(outs[t], shard_refs[t].shape[-2], 4 * x + 2 * y + c),
                                        sems.at[t]) for t in range(n)]
        for cp in copies:
            cp.start()
        for cp in copies:
            cp.wait()

    return list(pl.pallas_call(
        body, name=name, in_specs=[ANY] * (2 * n), out_specs=[ANY] * n,
        out_shape=[jax.ShapeDtypeStruct(f.shape, f.dtype) for f in fulls],
        input_output_aliases={t: t for t in range(n)},
        scratch_shapes=[pltpu.SemaphoreType.DMA((n,))],
    )(*fulls, *shards))


def _add_sibling(grad, recv, core, *, name, tr):
    rows = grad.shape[2]

    def body(core_ref, g_ref, r_ref, o_ref, ob_ref):
        p = g_ref[:, 0] + r_ref[...]
        o_ref[...] = p
        ob_ref[...] = p.astype(BF16)

    out = pl.BlockSpec((4, tr, D_MODEL), lambda i, core_ref: (0, i, 0))
    return pl.pallas_call(
        body, name=name,
        grid_spec=pltpu.PrefetchScalarGridSpec(
            num_scalar_prefetch=1, grid=(rows // tr,),
            in_specs=[pl.BlockSpec((4, 1, tr, D_MODEL), lambda i, core_ref: (0, core_ref[0], i, 0)), out],
            out_specs=[out, out]),
        out_shape=[jax.ShapeDtypeStruct(recv.shape, F32), jax.ShapeDtypeStruct(recv.shape, BF16)],
        compiler_params=_params(("arbitrary",)),
    )(core, grad, recv)


def _add_chips(partial, recv, chip, *, name, tr):
    rows = partial.shape[1]

    def body(chip_ref, p_ref, r_ref, o_ref):
        o_ref[...] = p_ref[0] + r_ref[0].astype(F32) + r_ref[1].astype(F32) + r_ref[2].astype(F32)

    return pl.pallas_call(
        body, name=name,
        grid_spec=pltpu.PrefetchScalarGridSpec(
            num_scalar_prefetch=1, grid=(rows // tr,),
            in_specs=[pl.BlockSpec((1, tr, D_MODEL), lambda i, chip_ref: (chip_ref[0], i, 0)),
                      pl.BlockSpec((3, tr, D_MODEL), lambda i, chip_ref: (0, i, 0))],
            out_specs=pl.BlockSpec((tr, D_MODEL), lambda i, chip_ref: (i, 0))),
        out_shape=jax.ShapeDtypeStruct((rows, D_MODEL), F32),
        compiler_params=_params(("arbitrary",)),
    )(chip, partial, recv)


def _adamw_math(w, g, m, v):
    m = ADAM_B1 * m + (1.0 - ADAM_B1) * g
    v = ADAM_B2 * v + (1.0 - ADAM_B2) * (g * g)
    m_hat = m / (1.0 - ADAM_B1 ** ADAM_STEP)
    v_hat = v / (1.0 - ADAM_B2 ** ADAM_STEP)
    delta = -ADAM_LR * (m_hat / (jnp.sqrt(v_hat) + ADAM_EPS) + ADAM_WD * w)
    return delta, m, v


def _adamw(w, g, m, v, *, name, tr, after=()):
    rows, cols = w.shape

    def body(w_ref, g_ref, m_ref, v_ref, d_ref, mo_ref, vo_ref):
        d_ref[...], mo_ref[...], vo_ref[...] = _adamw_math(w_ref[...], g_ref[...], m_ref[...], v_ref[...])

    spec = pl.BlockSpec((tr, cols), lambda i: (i, 0))
    return pl.pallas_call(
        _behind(body, 4, after), name=name, grid=(rows // tr,), in_specs=[spec] * 4 + [ANY] * len(after),
        out_specs=[spec] * 3, out_shape=[jax.ShapeDtypeStruct(w.shape, F32)] * 3,
        compiler_params=_params(("parallel",)),
    )(w, g, m, v, *after)


def _sum_small(gathered, *, name):
    def body(g_ref, o_ref):
        acc = g_ref[0]
        for k in range(1, N_DEV):
            acc = acc + g_ref[k]
        o_ref[...] = acc

    return pl.pallas_call(body, name=name, out_shape=jax.ShapeDtypeStruct(gathered.shape[1:], F32))(gathered)


def kernel(x, ffn1_norm, ffn1_w_gate, ffn1_w_up, ffn1_w_down, mix_norm, w_in, conv_w, attn_sinks, w_out, ffn2_norm, ffn2_w_gate, ffn2_w_up, ffn2_w_down, final_norm, loss_target, m_ffn1_norm, m_ffn1_w_gate, m_ffn1_w_up, m_ffn1_w_down, m_mix_norm, m_w_in, m_conv_w, m_attn_sinks, m_w_out, m_ffn2_norm, m_ffn2_w_gate, m_ffn2_w_up, m_ffn2_w_down, m_final_norm, v_ffn1_norm, v_ffn1_w_gate, v_ffn1_w_up, v_ffn1_w_down, v_mix_norm, v_w_in, v_conv_w, v_attn_sinks, v_w_out, v_ffn2_norm, v_ffn2_w_gate, v_ffn2_w_up, v_ffn2_w_down, v_final_norm):
    ix, iy, ic = lax.axis_index("x"), lax.axis_index("y"), lax.axis_index("c")
    my_index = 4 * ix + 2 * iy + ic
    core = ic.astype(jnp.int32).reshape(1)
    chip = (2 * ix + iy).astype(jnp.int32).reshape(1)

    given = dict(ffn1_norm=ffn1_norm, ffn1_w_gate=ffn1_w_gate, ffn1_w_up=ffn1_w_up, ffn1_w_down=ffn1_w_down,
                 mix_norm=mix_norm, w_in=w_in, conv_w=conv_w, attn_sinks=attn_sinks, w_out=w_out, ffn2_norm=ffn2_norm,
                 ffn2_w_gate=ffn2_w_gate, ffn2_w_up=ffn2_w_up, ffn2_w_down=ffn2_w_down, final_norm=final_norm)
    moments_m = dict(ffn1_norm=m_ffn1_norm, ffn1_w_gate=m_ffn1_w_gate, ffn1_w_up=m_ffn1_w_up, ffn1_w_down=m_ffn1_w_down,
                     mix_norm=m_mix_norm, w_in=m_w_in, conv_w=m_conv_w, attn_sinks=m_attn_sinks, w_out=m_w_out,
                     ffn2_norm=m_ffn2_norm, ffn2_w_gate=m_ffn2_w_gate, ffn2_w_up=m_ffn2_w_up, ffn2_w_down=m_ffn2_w_down,
                     final_norm=m_final_norm)
    moments_v = dict(ffn1_norm=v_ffn1_norm, ffn1_w_gate=v_ffn1_w_gate, ffn1_w_up=v_ffn1_w_up, ffn1_w_down=v_ffn1_w_down,
                     mix_norm=v_mix_norm, w_in=v_w_in, conv_w=v_conv_w, attn_sinks=v_attn_sinks, w_out=v_w_out,
                     ffn2_norm=v_ffn2_norm, ffn2_w_gate=v_ffn2_w_gate, ffn2_w_up=v_ffn2_w_up, ffn2_w_down=v_ffn2_w_down,
                     final_norm=v_final_norm)

    xs = x[0]
    target = loss_target[0]
    final_gain = final_norm.reshape(1, D_MODEL)

    def ffn_shard(wg, wu, wd):
        return jnp.stack([wg[0].T, wu[0].T, wd[0]]).astype(BF16)

    conv_cols = conv_w.shape[2]
    conv_shard = jnp.pad(conv_w[0], ((0, 5), (0, 128 - conv_cols)))
    rest_shards = [ffn_shard(ffn2_w_gate, ffn2_w_up, ffn2_w_down), w_in[0].T.astype(BF16), w_out[0].astype(BF16),
                   conv_shard]
    rest_rows = [s.shape[-2] for s in rest_shards]
    n_rest = len(rest_shards)
    (w1,), fulls = _all_gather_rows([ffn_shard(ffn1_w_gate, ffn1_w_up, ffn1_w_down)], rest_shards, name="gather_ffn1")

    send_plan = _gather_send_plan(n_rest)
    ssem, rsem, bufs, token = _split_start(rest_shards + list(fulls), 4 * n_rest, send_plan, name="gather_rest_start",
                                           after=[w1])
    x1, h1, a1, b1 = _ffn_fwd(xs, ffn1_norm, w1, name="ffn1_fwd", after=[token])
    bufs = _split_wait(ssem, rsem, bufs, x1, send_plan, name="gather_rest_wait")
    w2_part, mixer_parts = bufs[n_rest], bufs[n_rest + 1:]
    fwd_mixer = _gather_forward_plan(rest_rows[1:])
    ssem, rsem, bufs, token = _split_start(mixer_parts, 3 * (n_rest - 1), fwd_mixer, name="forward_mixer_start")
    win_t, wout, conv_all = _split_wait(ssem, rsem, bufs, token, fwd_mixer, name="forward_mixer_wait")
    conv_full = conv_all.reshape(N_DEV, 8, 128)[:, :3, :conv_cols].transpose(1, 0, 2).reshape(3, CONV_W)
    fwd_ffn2 = _gather_forward_plan(rest_rows[:1])
    ssem, rsem, bufs, token = _split_start([w2_part], 3, fwd_ffn2, name="forward_ffn2_start", after=[win_t])
    rope = _rope_tables(xs.shape[0])
    x2, hm, z, y = _mixer_fwd(x1, mix_norm, win_t, wout, conv_full, attn_sinks, rope, name="mixer_fwd", after=[token])
    (w2,) = _split_wait(ssem, rsem, bufs, x2, fwd_ffn2, name="forward_ffn2_wait")
    x3, h2, a2, b2 = _ffn_fwd(x2, ffn2_norm, w2, name="ffn2_fwd")
    dx3, loss_local, d_final = _loss_head(x3, final_gain, target, name="loss_head")
    loss = lax.psum(loss_local[0, 0], ("x", "y", "c"))

    def to_sibling_start(grads, tag, after=()):
        views = [g.reshape(4, 2, g.shape[0] // N_DEV, D_MODEL) for g in grads]
        lands = [lax.empty((4,) + v.shape[2:], F32) for v in views]
        plan = _sibling_plan(len(views))
        ssem, rsem, bufs, token = _split_start(views + lands, len(views), plan, name=f"{tag}_sibling_start", after=after)
        return (ssem, rsem, bufs, plan, tag), token

    def to_sibling_finish(handle, after, names):
        ssem, rsem, bufs, plan, tag = handle
        bufs = _split_wait(ssem, rsem, bufs, after, plan, name=f"{tag}_sibling_wait")
        n = len(names)
        return [_add_sibling(v, r, core, name=f"add_sibling_{nm}", tr=v.shape[2] // 2)
                for v, r, nm in zip(bufs[:n], bufs[n:], names)]

    def to_chips_start(partials, tag, small_all=None, after=()):
        p16 = [p for _, p in partials]
        lands = [lax.empty((3,) + p.shape[1:], BF16) for p in p16]
        extra = [] if small_all is None else [small_all]
        plan = _chips_plan(len(p16), small_all is not None)
        ssem, rsem, bufs, token = _split_start(p16 + lands + extra, 3 * len(p16) + 7 * len(extra), plan,
                                               name=f"{tag}_chips_start", after=after)
        return (ssem, rsem, bufs, plan, tag), token

    def to_chips_finish(handle, partials, after, names):
        ssem, rsem, bufs, plan, tag = handle
        bufs = _split_wait(ssem, rsem, bufs, after, plan, name=f"{tag}_chips_wait")
        n = len(names)
        reduced = [_add_chips(p32, r, chip, name=f"add_chips_{nm}", tr=p32.shape[1] // 2)
                   for (p32, _), r, nm in zip(partials, bufs[n:2 * n], names)]
        return reduced, bufs[2 * n:]

    half_ff = D_FF // 2
    names2, namesm = ["ffn2_w_gate", "ffn2_w_up", "ffn2_w_down"], ["w_in", "w_out"]
    transposed = {"ffn1_w_gate", "ffn1_w_up", "w_in", "ffn2_w_gate", "ffn2_w_up"}
    grad, delta, new_m, new_v = {}, {}, {}, {}

    def adam_big(nm, g_rows, after=()):
        g = g_rows.T if nm in transposed else g_rows
        shape = given[nm].shape
        w2d = given[nm][0]
        d, mo, vo = _adamw(w2d, g, moments_m[nm][0], moments_v[nm][0], name=f"adamw_{nm}", tr=w2d.shape[0] // 2,
                           after=after)
        grad[nm], delta[nm], new_m[nm], new_v[nm] = (a.reshape(shape) for a in (g, d, mo, vo))

    dx2, da2, db2, s2, g2b, d_norm2 = _ffn_dgrad(dx3, x2, ffn2_norm, a2, b2, w2, name="ffn2_dgrad")
    gw2 = [_tn_matmul(da2, h2, name="ffn2_wgrad_gate", bm=half_ff), _tn_matmul(db2, h2, name="ffn2_wgrad_up", bm=half_ff),
           _tn_matmul(s2, g2b, name="ffn2_wgrad_down", bm=half_ff)]
    sib2, tok = to_sibling_start(gw2, "ffn2")
    dx1, dz, gmb, d_conv, d_sink, d_normm = _mixer_bwd(dx2, x1, mix_norm, y, z, win_t, wout, conv_full, attn_sinks,
                                                       rope, name="mixer_bwd", after=[tok])
    p2 = to_sibling_finish(sib2, dx1, names2)
    chips2, tok = to_chips_start(p2, "ffn2")
    gwm = [_tn_matmul(dz, hm, name="mixer_wgrad_in", bm=Z_W // 3, after=[tok]),
           _tn_matmul(y, gmb, name="mixer_wgrad_out", bm=D_MODEL // 2, after=[tok])]
    sibm, tok = to_sibling_start(gwm, "mixer")
    dx0, da1, db1, s1, g1b, d_norm1 = _ffn_dgrad(dx1, xs, ffn1_norm, a1, b1, w1, name="ffn1_dgrad", after=[tok])
    r2, _ = to_chips_finish(chips2, p2, dx0, names2)
    pm = to_sibling_finish(sibm, dx0, namesm)
    chipsm, tok = to_chips_start(pm, "mixer")
    gw_gate = _tn_matmul(da1, h1, name="ffn1_wgrad_gate", bm=half_ff, after=[tok])
    sib_gate, tok = to_sibling_start([gw_gate], "ffn1_gate")
    gw_up = _tn_matmul(db1, h1, name="ffn1_wgrad_up", bm=half_ff, after=[tok])
    rm, _ = to_chips_finish(chipsm, pm, gw_up, namesm)
    p_gate = to_sibling_finish(sib_gate, gw_up, ["ffn1_w_gate"])
    chips_gate, tok_a = to_chips_start(p_gate, "ffn1_gate")
    sib_up, tok_b = to_sibling_start([gw_up], "ffn1_up", after=[tok_a])
    gw_down = _tn_matmul(s1, g1b, name="ffn1_wgrad_down", bm=half_ff, after=[tok_a, tok_b])
    p_up = to_sibling_finish(sib_up, gw_down, ["ffn1_w_up"])
    chips_up, tok_a = to_chips_start(p_up, "ffn1_up")
    sib_down, tok_b = to_sibling_start([gw_down], "ffn1_down", after=[tok_a])
    for nm, g in zip(names2 + namesm, r2 + rm):
        adam_big(nm, g, after=[tok_a, tok_b])
    r_gate, _ = to_chips_finish(chips_gate, p_gate, new_v["w_out"], ["ffn1_w_gate"])
    p_down = to_sibling_finish(sib_down, new_v["w_out"], ["ffn1_w_down"])
    small = jnp.concatenate([
        d_norm1, d_normm, d_norm2, d_final,
        jnp.pad(d_conv[0:3], ((0, 0), (0, D_MODEL - CONV_W))), jnp.pad(d_sink, ((0, 0), (0, D_MODEL - 128)))], axis=0)
    (small_all,) = _place_own([lax.empty((N_DEV * 8, D_MODEL), F32)], [small], name="place_own_small")
    chips_down, tok = to_chips_start(p_down, "ffn1_down", small_all)
    adam_big("ffn1_w_gate", r_gate[0], after=[tok])
    r_up, _ = to_chips_finish(chips_up, p_up, new_v["ffn1_w_gate"], ["ffn1_w_up"])
    adam_big("ffn1_w_up", r_up[0])
    r_down, (small_all,) = to_chips_finish(chips_down, p_down, new_v["ffn1_w_up"], ["ffn1_w_down"])
    adam_big("ffn1_w_down", r_down[0])
    small_sum = _sum_small(small_all.reshape(N_DEV, 8, D_MODEL), name="sum_small")
    _update_small(given, moments_m, moments_v, small_sum, my_index, grad, delta, new_m, new_v)

    order = list(given)
    return (loss, dx0[None], *[grad[n] for n in order], *[delta[n] for n in order],
            *[new_m[n] for n in order], *[new_v[n] for n in order])


def _update_small(given, moments_m, moments_v, small_sum, my_index, grad, delta, new_m, new_v):
    conv_cols = given["conv_w"].shape[2]
    small_g = {
        "ffn1_norm": small_sum[0:1], "mix_norm": small_sum[1:2], "ffn2_norm": small_sum[2:3],
        "final_norm": small_sum[3:4],
        "conv_w": lax.dynamic_slice(small_sum[4:7, :CONV_W], (0, my_index * conv_cols), (3, conv_cols)),
        "attn_sinks": small_sum[7:8, :N_Q_HEADS],
    }

    small_names = ["ffn1_norm", "mix_norm", "ffn2_norm", "final_norm", "conv_w", "attn_sinks"]

    def pack(parts):
        rows = []
        for nm in small_names:
            p = parts[nm]
            p2 = p.reshape(3, conv_cols) if nm == "conv_w" else p.reshape(1, -1)
            rows.append(jnp.pad(p2, ((0, 0), (0, D_MODEL - p2.shape[1]))))
        rows.append(jnp.zeros((8, D_MODEL), F32))
        return jnp.concatenate(rows, axis=0)

    sd, sm, sv = _adamw(pack(given), pack(small_g), pack(moments_m), pack(moments_v), name="adamw_small", tr=16)
    row = 0
    for nm in small_names:
        shape = given[nm].shape
        nrow = 3 if nm == "conv_w" else 1
        ncol = conv_cols if nm == "conv_w" else given[nm].size
        grad[nm] = small_g[nm].reshape(shape)
        delta[nm], new_m[nm], new_v[nm] = (a[row:row + nrow, :ncol].reshape(shape) for a in (sd, sm, sv))
        row += nrow
```

```python
import functools

import jax
import jax.numpy as jnp
from jax import lax
from jax.experimental import pallas as pl
from jax.experimental.pallas import tpu as pltpu

F32 = jnp.float32
BF16 = jnp.bfloat16
MESH = pl.DeviceIdType.MESH
ANY = pl.BlockSpec(memory_space=pl.ANY)
HBM_SPEC = pl.BlockSpec(memory_space=pltpu.HBM)
SEM_SPEC = pl.BlockSpec(memory_space=pltpu.SEMAPHORE)
DATAFLOW = pltpu.SideEffectType.DATAFLOW_SIDE_EFFECTING

N_DEV = 8
D_MODEL = 1024
D_FF = 2816
CONV_W = 512
ATTN_W = 512
KV_W = 128
HEAD_DIM = 64
N_Q_HEADS = 8
N_KV_HEADS = 2
Q_PER_KV = N_Q_HEADS // N_KV_HEADS
BLOCK = 128
ROT_DIM = 16
ROPE_THETA = 500000.0
Z_W = 3 * CONV_W + ATTN_W + 2 * KV_W
Q_OFF = 3 * CONV_W
K_OFF = Q_OFF + ATTN_W
V_OFF = K_OFF + KV_W
RMS_EPS = 1e-5
MASK_VALUE = -1e30
SM_SCALE = HEAD_DIM ** -0.5
FFN_RES_SCALE = 0.5

ADAM_LR = 0.001
ADAM_B1 = 0.9
ADAM_B2 = 0.999
ADAM_EPS = 1e-08
ADAM_WD = 0.01
ADAM_STEP = 10

NT_DIMS = (((1,), (1,)), ((), ()))
TN_DIMS = (((0,), (0,)), ((), ()))

VMEM_LIMIT = 56 * 1024 * 1024
FF_CHUNK = 256


def _params(sem, vmem=None):
    return pltpu.CompilerParams(dimension_semantics=sem, vmem_limit_bytes=vmem)


def _behind(body, n_in, after):
    k = len(after)
    if k == 0:
        return body
    return lambda *refs: body(*refs[:n_in], *refs[n_in + k:])


def _rms_stats(xf):
    inv = lax.rsqrt(jnp.mean(xf * xf, axis=-1, keepdims=True) + RMS_EPS)
    return xf * inv, inv


def _rms_bwd(dh, xhat, inv, gain):
    dxhat = dh * gain
    dx = inv * (dxhat - xhat * jnp.mean(dxhat * xhat, axis=-1, keepdims=True))
    dgain = jnp.sum(dh * xhat, axis=0, keepdims=True)
    return dx, dgain


def _load_resident(w_hbm, w_ref, sem):
    @pl.when(pl.program_id(0) == 0)
    def _():
        cp = pltpu.make_async_copy(w_hbm, w_ref, sem)
        cp.start()
        cp.wait()


def _ffn_fwd(x, gain, w3, *, name, after=(), tm=256, tf=FF_CHUNK):
    t = x.shape[0]
    tm = min(tm, t)

    def body(x_ref, g_ref, w_hbm, xo_ref, h_ref, a_ref, b_ref, w_ref, s_ref, sem):
        _load_resident(w_hbm, w_ref, sem)
        xf = x_ref[...]
        xhat, _ = _rms_stats(xf)
        h = (xhat * g_ref[...]).astype(BF16)
        h_ref[...] = h
        for c in range(0, D_FF, tf):
            a = lax.dot_general(h, w_ref[0, c:c + tf, :], NT_DIMS, preferred_element_type=F32)
            b = lax.dot_general(h, w_ref[1, c:c + tf, :], NT_DIMS, preferred_element_type=F32)
            a_ref[:, c:c + tf] = a.astype(BF16)
            b_ref[:, c:c + tf] = b.astype(BF16)
            s_ref[:, c:c + tf] = (a * jax.nn.sigmoid(a) * b).astype(BF16)
        xo_ref[...] = xf + FFN_RES_SCALE * jnp.dot(s_ref[...], w_ref[2], preferred_element_type=F32)

    row = pl.BlockSpec((tm, D_MODEL), lambda i: (i, 0))
    hid = pl.BlockSpec((tm, D_FF), lambda i: (i, 0))
    return pl.pallas_call(
        _behind(body, 3, after), name=name, grid=(t // tm,),
        in_specs=[row, pl.BlockSpec((1, D_MODEL), lambda i: (0, 0)), ANY] + [ANY] * len(after),
        out_specs=[row, row, hid, hid],
        out_shape=[jax.ShapeDtypeStruct((t, D_MODEL), F32), jax.ShapeDtypeStruct((t, D_MODEL), BF16),
                   jax.ShapeDtypeStruct((t, D_FF), BF16), jax.ShapeDtypeStruct((t, D_FF), BF16)],
        scratch_shapes=[pltpu.VMEM((3, D_FF, D_MODEL), BF16), pltpu.VMEM((tm, D_FF), BF16),
                        pltpu.SemaphoreType.DMA(())],
        compiler_params=_params(("arbitrary",), VMEM_LIMIT),
    )(x, gain, w3, *after)


def _ffn_dgrad(dxo, x, gain, a, b, w3, *, name, after=(), tm=256, tf=FF_CHUNK):
    t = x.shape[0]
    tm = min(tm, t)

    def body(dxo_ref, x_ref, g_ref, a_ref, b_ref, w_hbm, dxi_ref, da_ref, db_ref, s_ref, gb_ref, dg_ref, w_ref, sem):
        _load_resident(w_hbm, w_ref, sem)

        @pl.when(pl.program_id(0) == 0)
        def _():
            dg_ref[...] = jnp.zeros_like(dg_ref)

        go = dxo_ref[...]
        gb = (FFN_RES_SCALE * go).astype(BF16)
        gb_ref[...] = gb
        for c in range(0, D_FF, tf):
            ds = lax.dot_general(gb, w_ref[2, c:c + tf, :], NT_DIMS, preferred_element_type=F32)
            af = a_ref[:, c:c + tf].astype(F32)
            bf = b_ref[:, c:c + tf].astype(F32)
            sig = jax.nn.sigmoid(af)
            silu = af * sig
            da_ref[:, c:c + tf] = (ds * bf * (sig * (1.0 + af * (1.0 - sig)))).astype(BF16)
            db_ref[:, c:c + tf] = (ds * silu).astype(BF16)
            s_ref[:, c:c + tf] = (silu * bf).astype(BF16)
        dh = (jnp.dot(da_ref[...], w_ref[0], preferred_element_type=F32)
              + jnp.dot(db_ref[...], w_ref[1], preferred_element_type=F32))
        xhat, inv = _rms_stats(x_ref[...])
        dx, dgain = _rms_bwd(dh, xhat, inv, g_ref[...])
        dxi_ref[...] = go + dx
        dg_ref[...] += dgain

    row = pl.BlockSpec((tm, D_MODEL), lambda i: (i, 0))
    hid = pl.BlockSpec((tm, D_FF), lambda i: (i, 0))
    vec = pl.BlockSpec((1, D_MODEL), lambda i: (0, 0))
    return pl.pallas_call(
        _behind(body, 6, after), name=name, grid=(t // tm,),
        in_specs=[row, row, vec, hid, hid, ANY] + [ANY] * len(after),
        out_specs=[row, hid, hid, hid, row, vec],
        out_shape=[jax.ShapeDtypeStruct((t, D_MODEL), F32), jax.ShapeDtypeStruct((t, D_FF), BF16),
                   jax.ShapeDtypeStruct((t, D_FF), BF16), jax.ShapeDtypeStruct((t, D_FF), BF16),
                   jax.ShapeDtypeStruct((t, D_MODEL), BF16), jax.ShapeDtypeStruct((1, D_MODEL), F32)],
        scratch_shapes=[pltpu.VMEM((3, D_FF, D_MODEL), BF16), pltpu.SemaphoreType.DMA(())],
        compiler_params=_params(("arbitrary",), VMEM_LIMIT),
    )(dxo, x, gain, a, b, w3, *after)


def _tn_matmul(a, b, *, name, bm, after=(), tk=2048):
    t, m = a.shape
    n = b.shape[1]
    tk = min(tk, t)
    nk = t // tk

    def body(a_ref, b_ref, o_ref):
        @pl.when(pl.program_id(1) == 0)
        def _():
            o_ref[...] = jnp.zeros_like(o_ref)

        o_ref[...] += lax.dot_general(a_ref[...], b_ref[...], TN_DIMS, preferred_element_type=F32)

    return pl.pallas_call(
        _behind(body, 2, after), name=name, grid=(m // bm, nk),
        in_specs=[pl.BlockSpec((tk, bm), lambda i, k: (k, i)), pl.BlockSpec((tk, n), lambda i, k: (k, 0))]
        + [ANY] * len(after),
        out_specs=pl.BlockSpec((bm, n), lambda i, k: (i, 0)),
        out_shape=jax.ShapeDtypeStruct((m, n), F32),
        compiler_params=_params(("parallel", "arbitrary"), VMEM_LIMIT),
    )(a, b, *after)


def _loss_head(x, gain, target, *, name, tm=512):
    t = x.shape[0]
    tm = min(tm, t)

    def body(x_ref, g_ref, t_ref, dx_ref, loss_ref, dg_ref):
        @pl.when(pl.program_id(0) == 0)
        def _():
            loss_ref[...] = jnp.zeros_like(loss_ref)
            dg_ref[...] = jnp.zeros_like(dg_ref)

        xhat, inv = _rms_stats(x_ref[...])
        err = xhat * g_ref[...] - t_ref[...]
        loss_ref[...] += 0.5 * jnp.sum(jnp.mean(err * err, axis=-1, keepdims=True), axis=0, keepdims=True)
        dx, dgain = _rms_bwd(err * (1.0 / D_MODEL), xhat, inv, g_ref[...])
        dx_ref[...] = dx
        dg_ref[...] += dgain

    row = pl.BlockSpec((tm, D_MODEL), lambda i: (i, 0))
    vec = pl.BlockSpec((1, D_MODEL), lambda i: (0, 0))
    return pl.pallas_call(
        body, name=name, grid=(t // tm,),
        in_specs=[row, vec, row],
        out_specs=[row, pl.BlockSpec((1, 1), lambda i: (0, 0)), vec],
        out_shape=[jax.ShapeDtypeStruct((t, D_MODEL), F32), jax.ShapeDtypeStruct((1, 1), F32),
                   jax.ShapeDtypeStruct((1, D_MODEL), F32)],
        compiler_params=_params(("arbitrary",)),
    )(x, gain, target)


def _rope_tables(t):
    half = ROT_DIM // 2
    inv_freq = ROPE_THETA ** (-jnp.arange(0, ROT_DIM, 2, dtype=F32) / ROT_DIM)
    ang = jnp.arange(t, dtype=F32)[:, None] * inv_freq[None, :]
    cos, sin = jnp.cos(ang), jnp.sin(ang)
    zeros, ones = jnp.zeros((t, half), F32), jnp.ones((t, HEAD_DIM - ROT_DIM), F32)
    pad = jnp.zeros((t, HEAD_DIM - ROT_DIM), F32)
    mult = jnp.concatenate([cos, cos, ones], axis=1)
    from_lo = jnp.concatenate([zeros, sin, pad], axis=1)
    from_hi = jnp.concatenate([-sin, zeros, pad], axis=1)
    return jnp.stack([jnp.tile(m, (1, 2)) for m in (mult, from_lo, from_hi)])


def _tile_lanes(tab, width):
    return jnp.tile(tab, (1, width // tab.shape[1]))


def _rope(v, tab):
    w = v.shape[1]
    half_rot = ROT_DIM // 2
    return (v * _tile_lanes(tab[0], w)
            + pltpu.roll(v, half_rot, axis=1) * _tile_lanes(tab[1], w)
            + pltpu.roll(v, w - half_rot, axis=1) * _tile_lanes(tab[2], w))


def _rope_bwd(dv, tab):
    w = dv.shape[1]
    half_rot = ROT_DIM // 2
    return (dv * _tile_lanes(tab[0], w)
            + pltpu.roll(dv * _tile_lanes(tab[1], w), w - half_rot, axis=1)
            + pltpu.roll(dv * _tile_lanes(tab[2], w), half_rot, axis=1))


def _shift_rows(v, prev8_ref, n):
    r = lax.broadcasted_iota(jnp.int32, v.shape, 0)
    rolled = pltpu.roll(v, n, axis=0)
    last = prev8_ref[7:8, :]
    if n == 1:
        return jnp.where(r >= 1, rolled, last)
    return jnp.where(r >= 2, rolled, jnp.where(r == 0, prev8_ref[6:7, :], last))


def _shift_rows_up(v, next8_ref, n):
    rows = v.shape[0]
    r = lax.broadcasted_iota(jnp.int32, v.shape, 0)
    rolled = pltpu.roll(v, rows - n, axis=0)
    first = next8_ref[0:1, :]
    if n == 1:
        return jnp.where(r <= rows - 2, rolled, first)
    return jnp.where(r <= rows - 3, rolled, jnp.where(r == rows - 2, first, next8_ref[1:2, :]))


def _lane_half_mask(shape, half):
    lane = lax.broadcasted_iota(jnp.int32, shape, 1)
    return (lane >= HEAD_DIM) if half else (lane < HEAD_DIM)


def _to_kv_lanes(chunk, head, kv):
    if head % 2 != kv:
        chunk = pltpu.roll(chunk, HEAD_DIM, axis=1)
    return jnp.where(_lane_half_mask(chunk.shape, kv), chunk, 0.0)


def _from_kv_lanes(chunk, head, kv):
    chunk = jnp.where(_lane_half_mask(chunk.shape, kv), chunk, 0.0)
    if head % 2 != kv:
        chunk = pltpu.roll(chunk, HEAD_DIM, axis=1)
    return chunk


def _stack_heads(wide, kv):
    parts = []
    for g in range(Q_PER_KV):
        head = kv * Q_PER_KV + g
        chunk = wide[:, (head // 2) * 128:(head // 2 + 1) * 128]
        parts.append(_to_kv_lanes(chunk, head, kv))
    return jnp.concatenate(parts, axis=0)


def _window_mask(has_prev):
    shape = (Q_PER_KV * BLOCK, 2 * BLOCK)
    qi = lax.broadcasted_iota(jnp.int32, shape, 0) & (BLOCK - 1)
    kj = lax.broadcasted_iota(jnp.int32, shape, 1)
    first_key = BLOCK - has_prev * BLOCK
    in_prev = (kj < BLOCK) & (kj > qi) & (kj >= first_key)
    in_own = (kj >= BLOCK) & ((kj - BLOCK) <= qi)
    return in_prev | in_own


def _sink_column(sink_ref, kv):
    row = lax.broadcasted_iota(jnp.int32, (Q_PER_KV * BLOCK, 1), 0)
    col = jnp.full((Q_PER_KV * BLOCK, 1), sink_ref[0, kv * Q_PER_KV], F32)
    for g in range(1, Q_PER_KV):
        col = jnp.where(row >= g * BLOCK, sink_ref[0, kv * Q_PER_KV + g], col)
    return col


def _softmax_with_sink(q4, k2, mask, sink):
    s = lax.dot_general(q4, k2, NT_DIMS, preferred_element_type=F32) * SM_SCALE
    s = jnp.where(mask, s, MASK_VALUE)
    m = jnp.maximum(jnp.max(s, axis=-1, keepdims=True), sink)
    p = jnp.exp(s - m)
    e_sink = jnp.exp(sink - m)
    inv_den = 1.0 / (jnp.sum(p, axis=-1, keepdims=True) + e_sink)
    return p * inv_den, e_sink * inv_den


def _conv_terms(zf, prev8_ref, w_ref):
    b_gate, c_gate, u = zf[:, 0:CONV_W], zf[:, CONV_W:2 * CONV_W], zf[:, 2 * CONV_W:3 * CONV_W]
    vc = c_gate * u
    vm1 = _shift_rows(vc, prev8_ref, 1)
    vm2 = _shift_rows(vc, prev8_ref, 2)
    conv = w_ref[0:1, :] * vm2 + w_ref[1:2, :] * vm1 + w_ref[2:3, :] * vc
    return b_gate, c_gate, u, vc, vm1, vm2, conv


def _mixer_fwd(x, gain, win_t, wout, conv_w, sinks, rope, *, name, after=(), tq=512):
    t = x.shape[0]
    tq = min(tq, t)
    nblk = tq // BLOCK

    def body(x_ref, g_ref, win_hbm, wout_hbm, cw_ref, sink_ref, rope_ref,
             xo_ref, h_ref, z_ref, y_ref, kprev_ref, vprev_ref, cprev_ref, win_ref, wout_ref, sems):
        i = pl.program_id(0)
        _load_resident(win_hbm, win_ref, sems.at[0])
        _load_resident(wout_hbm, wout_ref, sems.at[1])

        @pl.when(i == 0)
        def _():
            kprev_ref[...] = jnp.zeros_like(kprev_ref)
            vprev_ref[...] = jnp.zeros_like(vprev_ref)
            cprev_ref[...] = jnp.zeros_like(cprev_ref)

        xf = x_ref[...]
        xhat, _ = _rms_stats(xf)
        h = (xhat * g_ref[...]).astype(BF16)
        h_ref[...] = h
        zb = lax.dot_general(h, win_ref[...], NT_DIMS, preferred_element_type=F32).astype(BF16)
        z_ref[...] = zb
        zf = zb.astype(F32)

        b_gate, _, _, vc, _, _, conv = _conv_terms(zf, cprev_ref, cw_ref)
        y_conv = b_gate * conv
        cprev_ref[...] = vc[tq - 8:tq, :]

        tab = rope_ref[...]
        qr = _rope(zf[:, Q_OFF:K_OFF], tab)
        kr = _rope(zf[:, K_OFF:V_OFF], tab).astype(BF16)
        vb = zb[:, V_OFF:Z_W]

        y_attn = []
        for j in range(nblk):
            rows = slice(j * BLOCK, (j + 1) * BLOCK)
            prev = slice((j - 1) * BLOCK, j * BLOCK)
            k2 = jnp.concatenate([kprev_ref[...] if j == 0 else kr[prev], kr[rows]], axis=0)
            v2 = jnp.concatenate([vprev_ref[...] if j == 0 else vb[prev], vb[rows]], axis=0)
            mask = _window_mask(jnp.minimum(i, 1) if j == 0 else 1)
            chunks = [jnp.zeros((BLOCK, 128), F32) for _ in range(ATTN_W // 128)]
            for kv in range(N_KV_HEADS):
                q4 = _stack_heads(qr[rows], kv).astype(BF16)
                probs, _ = _softmax_with_sink(q4, k2, mask, _sink_column(sink_ref, kv))
                o4 = jnp.dot(probs.astype(BF16), v2, preferred_element_type=F32)
                for g in range(Q_PER_KV):
                    head = kv * Q_PER_KV + g
                    chunks[head // 2] += _from_kv_lanes(o4[g * BLOCK:(g + 1) * BLOCK], head, kv)
            y_attn.append(jnp.concatenate(chunks, axis=1))
        kprev_ref[...] = kr[tq - BLOCK:tq]
        vprev_ref[...] = vb[tq - BLOCK:tq]
        y = jnp.concatenate([y_conv, jnp.concatenate(y_attn, axis=0)], axis=1).astype(BF16)
        y_ref[...] = y
        xo_ref[...] = xf + jnp.dot(y, wout_ref[...], preferred_element_type=F32)

    row = pl.BlockSpec((tq, D_MODEL), lambda i: (i, 0))
    full = lambda shape: pl.BlockSpec(shape, lambda i: (0,) * len(shape))
    return pl.pallas_call(
        _behind(body, 7, after), name=name, grid=(t // tq,),
        in_specs=[row, full((1, D_MODEL)), ANY, ANY, full((3, CONV_W)),
                  pl.BlockSpec(memory_space=pltpu.SMEM), pl.BlockSpec((3, tq, 128), lambda i: (0, i, 0))]
        + [ANY] * len(after),
        out_specs=[row, row, pl.BlockSpec((tq, Z_W), lambda i: (i, 0)), row],
        out_shape=[jax.ShapeDtypeStruct((t, D_MODEL), F32), jax.ShapeDtypeStruct((t, D_MODEL), BF16),
                   jax.ShapeDtypeStruct((t, Z_W), BF16), jax.ShapeDtypeStruct((t, D_MODEL), BF16)],
        scratch_shapes=[pltpu.VMEM((BLOCK, KV_W), BF16), pltpu.VMEM((BLOCK, KV_W), BF16),
                        pltpu.VMEM((8, CONV_W), F32), pltpu.VMEM((Z_W, D_MODEL), BF16),
                        pltpu.VMEM((D_MODEL, D_MODEL), BF16), pltpu.SemaphoreType.DMA((2,))],
        compiler_params=_params(("arbitrary",), VMEM_LIMIT),
    )(x, gain, win_t, wout, conv_w, sinks, rope, *after)


def _mixer_bwd(dxo, x, gain, y, z, win_t, wout, conv_w, sinks, rope, *, name, after=(), tq=256):
    t = x.shape[0]
    tq = min(tq, t)
    nt, nblk = t // tq, tq // BLOCK

    def body(dxo_ref, x_ref, g_ref, y_ref, z_ref, zp_ref, win_hbm, wout_hbm, cw_ref, sink_ref, rope_ref, ropep_ref,
             dxi_ref, dz_ref, gb_ref, dcw_ref, dsink_ref, dg_ref, dk_ref, dv_ref, dcn_ref, pvc_ref,
             win_ref, wout_ref, sems):
        i = pl.program_id(0)
        tile = nt - 1 - i
        _load_resident(win_hbm, win_ref, sems.at[0])
        _load_resident(wout_hbm, wout_ref, sems.at[1])

        @pl.when(i == 0)
        def _():
            dk_ref[...] = jnp.zeros_like(dk_ref)
            dv_ref[...] = jnp.zeros_like(dv_ref)
            dcn_ref[...] = jnp.zeros_like(dcn_ref)
            dcw_ref[...] = jnp.zeros_like(dcw_ref)
            dsink_ref[...] = jnp.zeros_like(dsink_ref)
            dg_ref[...] = jnp.zeros_like(dg_ref)

        has_prev = jnp.minimum(tile, 1)
        go = dxo_ref[...]
        gb = go.astype(BF16)
        gb_ref[...] = gb
        dy = lax.dot_general(gb, wout_ref[...], NT_DIMS, preferred_element_type=F32)
        dy_conv, dy_attn = dy[:, 0:CONV_W], dy[:, CONV_W:D_MODEL]
        zb, zpb = z_ref[...], zp_ref[...]
        zf = zb.astype(F32)
        zpf = zpb.astype(F32) * has_prev.astype(F32)

        pvc_ref[...] = (zpf[:, CONV_W:2 * CONV_W] * zpf[:, 2 * CONV_W:3 * CONV_W])[BLOCK - 8:BLOCK, :]
        b_gate, c_gate, u, vc, vm1, vm2, conv = _conv_terms(zf, pvc_ref, cw_ref)
        d_bgate = dy_conv * conv
        dc = dy_conv * b_gate
        tap = lax.broadcasted_iota(jnp.int32, (8, CONV_W), 0)
        dcw_ref[...] += jnp.where(tap == 0, jnp.sum(dc * vm2, axis=0, keepdims=True),
                                  jnp.where(tap == 1, jnp.sum(dc * vm1, axis=0, keepdims=True),
                                            jnp.where(tap == 2, jnp.sum(dc * vc, axis=0, keepdims=True), 0.0)))
        dvc = (cw_ref[2:3, :] * dc + cw_ref[1:2, :] * _shift_rows_up(dc, dcn_ref, 1)
               + cw_ref[0:1, :] * _shift_rows_up(dc, dcn_ref, 2))
        dcn_ref[...] = dc[0:8, :]
        d_cgate = dvc * u
        d_u = dvc * c_gate

        tab, tabp = rope_ref[...], ropep_ref[...]
        qr = _rope(zf[:, Q_OFF:K_OFF], tab)
        kr = _rope(zf[:, K_OFF:V_OFF], tab).astype(BF16)
        kpr = _rope(zpf[:, K_OFF:V_OFF], tabp).astype(BF16)
        vb, vpb = zb[:, V_OFF:Z_W], zpb[:, V_OFF:Z_W]
        out = y_ref[:, CONV_W:D_MODEL].astype(F32)
        do_out = dy_attn * out
        lane = lax.broadcasted_iota(jnp.int32, (1, 128), 1)
        dsink = jnp.zeros((1, 128), F32)
        dk_next, dv_next = dk_ref[...], dv_ref[...]
        dq_rows, dk_rows, dv_rows = [None] * nblk, [None] * nblk, [None] * nblk
        for j in reversed(range(nblk)):
            rows = slice(j * BLOCK, (j + 1) * BLOCK)
            prev = slice((j - 1) * BLOCK, j * BLOCK)
            k2 = jnp.concatenate([kpr if j == 0 else kr[prev], kr[rows]], axis=0)
            v2 = jnp.concatenate([vpb if j == 0 else vb[prev], vb[rows]], axis=0)
            mask = _window_mask(has_prev if j == 0 else 1)
            dk2 = jnp.zeros((2 * BLOCK, KV_W), F32)
            dv2 = jnp.zeros((2 * BLOCK, KV_W), F32)
            dq_chunks = [jnp.zeros((BLOCK, 128), F32) for _ in range(ATTN_W // 128)]
            for kv in range(N_KV_HEADS):
                q4 = _stack_heads(qr[rows], kv).astype(BF16)
                do4 = _stack_heads(dy_attn[rows], kv).astype(BF16)
                delta = jnp.sum(_stack_heads(do_out[rows], kv), axis=-1, keepdims=True)
                probs, p_sink = _softmax_with_sink(q4, k2, mask, _sink_column(sink_ref, kv))
                dp = lax.dot_general(do4, v2, NT_DIMS, preferred_element_type=F32)
                ds = (probs * (dp - delta) * SM_SCALE).astype(BF16)
                dq4 = jnp.dot(ds, k2, preferred_element_type=F32)
                dk2 += lax.dot_general(ds, q4, TN_DIMS, preferred_element_type=F32)
                dv2 += lax.dot_general(probs.astype(BF16), do4, TN_DIMS, preferred_element_type=F32)
                sink_terms = p_sink * delta
                for g in range(Q_PER_KV):
                    head = kv * Q_PER_KV + g
                    grp = slice(g * BLOCK, (g + 1) * BLOCK)
                    dq_chunks[head // 2] += _from_kv_lanes(dq4[grp], head, kv)
                    dsink = dsink - jnp.where(lane == head, jnp.sum(sink_terms[grp], axis=0, keepdims=True), 0.0)
            dq_rows[j] = jnp.concatenate(dq_chunks, axis=1)
            dk_rows[j] = dk2[BLOCK:] + dk_next
            dv_rows[j] = dv2[BLOCK:] + dv_next
            dk_next, dv_next = dk2[:BLOCK], dv2[:BLOCK]
        dk_ref[...] = dk_next
        dv_ref[...] = dv_next
        dsink_ref[...] += dsink
        dq = _rope_bwd(jnp.concatenate(dq_rows, axis=0), tab)
        dk = _rope_bwd(jnp.concatenate(dk_rows, axis=0), tab)
        dv = jnp.concatenate(dv_rows, axis=0)

        dzb = jnp.concatenate([d_bgate, d_cgate, d_u, dq, dk, dv], axis=1).astype(BF16)
        dz_ref[...] = dzb
        dh = jnp.dot(dzb, win_ref[...], preferred_element_type=F32)
        xhat, inv = _rms_stats(x_ref[...])
        dx, dgain = _rms_bwd(dh, xhat, inv, g_ref[...])
        dxi_ref[...] = go + dx
        dg_ref[...] += dgain

    rev = lambda i: (nt - 1 - i, 0)
    block_before = lambda i: jnp.maximum((nt - 1 - i) * nblk - 1, 0)
    row = pl.BlockSpec((tq, D_MODEL), rev)
    full = lambda shape: pl.BlockSpec(shape, lambda i: (0,) * len(shape))
    return pl.pallas_call(
        _behind(body, 12, after), name=name, grid=(nt,),
        in_specs=[row, row, full((1, D_MODEL)), row,
                  pl.BlockSpec((tq, Z_W), rev), pl.BlockSpec((BLOCK, Z_W), lambda i: (block_before(i), 0)),
                  ANY, ANY, full((3, CONV_W)),
                  pl.BlockSpec(memory_space=pltpu.SMEM),
                  pl.BlockSpec((3, tq, 128), lambda i: (0, nt - 1 - i, 0)),
                  pl.BlockSpec((3, BLOCK, 128), lambda i: (0, block_before(i), 0))] + [ANY] * len(after),
        out_specs=[row, pl.BlockSpec((tq, Z_W), rev), row, full((8, CONV_W)), full((1, 128)), full((1, D_MODEL))],
        out_shape=[jax.ShapeDtypeStruct((t, D_MODEL), F32), jax.ShapeDtypeStruct((t, Z_W), BF16),
                   jax.ShapeDtypeStruct((t, D_MODEL), BF16), jax.ShapeDtypeStruct((8, CONV_W), F32),
                   jax.ShapeDtypeStruct((1, 128), F32), jax.ShapeDtypeStruct((1, D_MODEL), F32)],
        scratch_shapes=[pltpu.VMEM((BLOCK, KV_W), F32), pltpu.VMEM((BLOCK, KV_W), F32), pltpu.VMEM((8, CONV_W), F32),
                        pltpu.VMEM((8, CONV_W), F32), pltpu.VMEM((Z_W, D_MODEL), BF16),
                        pltpu.VMEM((D_MODEL, D_MODEL), BF16), pltpu.SemaphoreType.DMA((2,))],
        compiler_params=_params(("arbitrary",), VMEM_LIMIT),
    )(dxo, x, gain, y, z, z, win_t, wout, conv_w, sinks, rope, rope, *after)


def _place():
    x, y, c = lax.axis_index("x"), lax.axis_index("y"), lax.axis_index("c")
    other_chips = [(1 - x, y), (x, 1 - y), (1 - x, 1 - y)]
    return x, y, c, other_chips


def _all_gather_rows(shards, place=(), *, name):
    n, p = len(shards), len(place)

    def body(*refs):
        srcs, place_srcs = refs[:n], refs[n:n + p]
        outs, place_outs = refs[n + p:2 * n + p], refs[2 * n + p:2 * (n + p)]
        send_sems, recv_sems, local_sems = refs[2 * (n + p):]
        x, y, c, chips = _place()
        me, sibling = (x, y, c), (x, y, 1 - c)

        def rows(t, px, py, pc):
            r = srcs[t].shape[-2]
            start = pl.multiple_of((4 * px + 2 * py + pc) * r, 16 if r % 16 == 0 else 8)
            if len(srcs[t].shape) == 3:
                return outs[t].at[:, pl.ds(start, r), :]
            return outs[t].at[pl.ds(start, r), :]

        def copy(t, k, block, to, own=False):
            return pltpu.make_async_remote_copy(
                src_ref=srcs[t] if own else rows(t, *block), dst_ref=rows(t, *block),
                send_sem=send_sems.at[t, k], recv_sem=recv_sems.at[t, k], device_id=to, device_id_type=MESH)

        mine = [pltpu.make_async_copy(srcs[t], rows(t, *me), local_sems.at[t]) for t in range(n)]
        mine += [pltpu.make_async_copy(place_srcs[q],
                                       _block_rows(place_outs[q], place_srcs[q].shape[-2], 4 * x + 2 * y + c),
                                       local_sems.at[n + q]) for q in range(p)]
        for q in range(p):
            mine[n + q].start()
        first = []
        for t in range(n):
            mine[t].start()
            first.append(copy(t, 0, me, sibling, own=True))
            first += [copy(t, 1 + j, me, (*chip, c), own=True) for j, chip in enumerate(chips)]
        for cp in first:
            cp.start()
        passed = []
        for j, chip in enumerate(chips):
            for t in range(n):
                copy(t, 1 + j, (*chip, c), me).wait_recv()
                fwd = copy(t, 4 + j, (*chip, c), sibling)
                fwd.start()
                passed.append(fwd)
        for t in range(n):
            copy(t, 0, sibling, me).wait_recv()
            for j, chip in enumerate(chips):
                copy(t, 4 + j, (*chip, 1 - c), me).wait_recv()
        for cp in first + passed:
            cp.wait_send()
        for cp in mine:
            cp.wait()

    out_shape = [jax.ShapeDtypeStruct(s.shape[:-2] + (N_DEV * s.shape[-2], s.shape[-1]), s.dtype)
                 for s in list(shards) + list(place)]
    res = pl.pallas_call(
        body, name=name, in_specs=[ANY] * (n + p), out_specs=[ANY] * (n + p), out_shape=out_shape,
        scratch_shapes=[pltpu.SemaphoreType.DMA((n, 7)), pltpu.SemaphoreType.DMA((n, 7)),
                        pltpu.SemaphoreType.DMA((n + p,))],
    )(*shards, *place)
    return res[:n], res[n:]


def _split_start(bufs, n_copies, plan, *, name, after=()):
    n = len(bufs)

    def body(*refs):
        token = refs[-1]
        for cp in plan(refs[:n], refs[n], refs[n + 1]):
            cp.start()
        token[...] = jnp.zeros_like(token)

    res = pl.pallas_call(
        _behind(body, n, after), name=name, in_specs=[HBM_SPEC] * n + [ANY] * len(after),
        out_specs=(SEM_SPEC, SEM_SPEC, *[HBM_SPEC] * n, pl.BlockSpec(memory_space=pltpu.VMEM)),
        out_shape=(pltpu.SemaphoreType.DMA((n_copies,)), pltpu.SemaphoreType.DMA((n_copies,)),
                   *[pltpu.HBM(b.shape, b.dtype) for b in bufs], jax.ShapeDtypeStruct((8, 128), F32)),
        input_output_aliases={i: 2 + i for i in range(n)},
        compiler_params=pltpu.CompilerParams(has_side_effects=DATAFLOW),
    )(*[pltpu.with_memory_space_constraint(b, pltpu.HBM) for b in bufs], *after)
    return res[0], res[1], list(res[2:2 + n]), res[-1]


def _split_wait(send_sems, recv_sems, bufs, after, plan, *, name):
    n = len(bufs)

    def body(*refs):
        for cp in plan(refs[:n], refs[n], refs[n + 1]):
            cp.wait_send()
            cp.wait_recv()

    return list(pl.pallas_call(
        body, name=name, in_specs=[HBM_SPEC] * n + [SEM_SPEC, SEM_SPEC, ANY], out_specs=[HBM_SPEC] * n,
        out_shape=tuple(pltpu.HBM(b.shape, b.dtype) for b in bufs),
        input_output_aliases={i: i for i in range(n)},
        compiler_params=pltpu.CompilerParams(has_side_effects=DATAFLOW),
    )(*bufs, send_sems, recv_sems, after))


def _sibling_plan(n):
    def plan(bufs, send_sems, recv_sems):
        x, y, c, _ = _place()
        return [pltpu.make_async_remote_copy(
            src_ref=bufs[t].at[:, 1 - c], dst_ref=bufs[n + t], send_sem=send_sems.at[t], recv_sem=recv_sems.at[t],
            device_id=(x, y, 1 - c), device_id_type=MESH) for t in range(n)]
    return plan


def _block_rows(ref, r, blk):
    start = pl.multiple_of(blk * r, 16 if r % 16 == 0 else 8)
    return ref.at[(slice(None),) * (len(ref.shape) - 2) + (pl.ds(start, r), slice(None))]


def _remote(src, dst, send_sems, recv_sems, k, peer):
    return pltpu.make_async_remote_copy(src_ref=src, dst_ref=dst, send_sem=send_sems.at[k], recv_sem=recv_sems.at[k],
                                        device_id=peer, device_id_type=MESH)


def _gather_send_plan(n):
    def plan(bufs, send_sems, recv_sems):
        x, y, c, chips = _place()
        peers = [(x, y, 1 - c)] + [(px, py, c) for px, py in chips]
        copies = []
        for t in range(n):
            dst = _block_rows(bufs[n + t], bufs[t].shape[-2], 4 * x + 2 * y + c)
            copies += [_remote(bufs[t], dst, send_sems, recv_sems, 4 * t + k, peer) for k, peer in enumerate(peers)]
        return copies
    return plan


def _gather_forward_plan(rows):
    def plan(bufs, send_sems, recv_sems):
        x, y, c, chips = _place()
        copies = []
        for t, r in enumerate(rows):
            for j, (px, py) in enumerate(chips):
                blk = _block_rows(bufs[t], r, 4 * px + 2 * py + c)
                copies.append(_remote(blk, blk, send_sems, recv_sems, 3 * t + j, (x, y, 1 - c)))
        return copies
    return plan


def _chips_plan(n, with_small):
    def plan(bufs, send_sems, recv_sems):
        x, y, c, chips = _place()
        copies = []
        for t in range(n):
            for j, (px, py) in enumerate(chips):
                copies.append(_remote(bufs[t].at[2 * px + py], bufs[n + t].at[j], send_sems, recv_sems, 3 * t + j,
                                      (px, py, c)))
        if with_small:
            mine = _block_rows(bufs[2 * n], 8, 4 * x + 2 * y + c)
            flips = [(fx, fy, fc) for fx in range(2) for fy in range(2) for fc in range(2)][1:]
            for k, (fx, fy, fc) in enumerate(flips):
                peer = (x + fx - 2 * x * fx, y + fy - 2 * y * fy, c + fc - 2 * c * fc)
                copies.append(_remote(mine, mine, send_sems, recv_sems, 3 * n + k, peer))
        return copies
    return plan


def _place_own(fulls, shards, index, *, name):
    n = len(fulls)

    def body(index_ref, *refs):
        for t in range(n):
            refs[2 * n + t][...] = refs[n + t][...]

    def block_of(shard):
        lead = len(shard.shape) - 2
        return pl.BlockSpec(shard.shape, lambda i, index_ref: (0,) * lead + (index_ref[0], 0))

    def whole(shard):
        return pl.BlockSpec(shard.shape, lambda i, index_ref: (0,) * len(shard.shape))

    return list(pl.pallas_call(
        body, name=name,
        grid_spec=pltpu.PrefetchScalarGridSpec(
            num_scalar_prefetch=1, grid=(1,),
            in_specs=[ANY] * n + [whole(s) for s in shards], out_specs=[block_of(s) for s in shards]),
        out_shape=[jax.ShapeDtypeStruct(f.shape, f.dtype) for f in fulls],
        input_output_aliases={1 + t: t for t in range(n)},
        compiler_params=_params(("arbitrary",)),
    )(index, *fulls, *shards))


def _add_sibling(grad, recv, core, *, name, tr):
    rows = grad.shape[2]

    def body(core_ref, g_ref, r_ref, o_ref, ob_ref):
        p = g_ref[:, 0] + r_ref[...]
        o_ref[...] = p
        ob_ref[...] = p.astype(BF16)

    out = pl.BlockSpec((4, tr, D_MODEL), lambda i, core_ref: (0, i, 0))
    return pl.pallas_call(
        body, name=name,
        grid_spec=pltpu.PrefetchScalarGridSpec(
            num_scalar_prefetch=1, grid=(rows // tr,),
            in_specs=[pl.BlockSpec((4, 1, tr, D_MODEL), lambda i, core_ref: (0, core_ref[0], i, 0)), out],
            out_specs=[out, out]),
        out_shape=[jax.ShapeDtypeStruct(recv.shape, F32), jax.ShapeDtypeStruct(recv.shape, BF16)],
        compiler_params=_params(("arbitrary",)),
    )(core, grad, recv)


def _reduce_adamw(partial, recv, chip, w, m, v, *, name, tr, after=()):
    rows = partial.shape[1]

    def body(chip_ref, p_ref, r_ref, w_ref, m_ref, v_ref, g_ref, d_ref, mo_ref, vo_ref):
        g = p_ref[0] + r_ref[0].astype(F32) + r_ref[1].astype(F32) + r_ref[2].astype(F32)
        g_ref[...] = g
        d_ref[...], mo_ref[...], vo_ref[...] = _adamw_math(w_ref[...], g, m_ref[...], v_ref[...])

    spec = pl.BlockSpec((tr, D_MODEL), lambda i, chip_ref: (i, 0))
    return pl.pallas_call(
        _behind(body, 6, after), name=name,
        grid_spec=pltpu.PrefetchScalarGridSpec(
            num_scalar_prefetch=1, grid=(rows // tr,),
            in_specs=[pl.BlockSpec((1, tr, D_MODEL), lambda i, chip_ref: (chip_ref[0], i, 0)),
                      pl.BlockSpec((3, tr, D_MODEL), lambda i, chip_ref: (0, i, 0)), spec, spec, spec]
            + [ANY] * len(after),
            out_specs=[spec] * 4),
        out_shape=[jax.ShapeDtypeStruct((rows, D_MODEL), F32)] * 4,
        compiler_params=_params(("arbitrary",)),
    )(chip, partial, recv, w, m, v, *after)


def _adamw_math(w, g, m, v):
    m = ADAM_B1 * m + (1.0 - ADAM_B1) * g
    v = ADAM_B2 * v + (1.0 - ADAM_B2) * (g * g)
    m_hat = m / (1.0 - ADAM_B1 ** ADAM_STEP)
    v_hat = v / (1.0 - ADAM_B2 ** ADAM_STEP)
    delta = -ADAM_LR * (m_hat / (jnp.sqrt(v_hat) + ADAM_EPS) + ADAM_WD * w)
    return delta, m, v


def _adamw(w, g, m, v, *, name, tr, after=()):
    rows, cols = w.shape

    def body(w_ref, g_ref, m_ref, v_ref, d_ref, mo_ref, vo_ref):
        d_ref[...], mo_ref[...], vo_ref[...] = _adamw_math(w_ref[...], g_ref[...], m_ref[...], v_ref[...])

    spec = pl.BlockSpec((tr, cols), lambda i: (i, 0))
    return pl.pallas_call(
        _behind(body, 4, after), name=name, grid=(rows // tr,), in_specs=[spec] * 4 + [ANY] * len(after),
        out_specs=[spec] * 3, out_shape=[jax.ShapeDtypeStruct(w.shape, F32)] * 3,
        compiler_params=_params(("parallel",)),
    )(w, g, m, v, *after)


def _sum_small(gathered, *, name):
    def body(g_ref, o_ref):
        acc = g_ref[0]
        for k in range(1, N_DEV):
            acc = acc + g_ref[k]
        o_ref[...] = acc

    return pl.pallas_call(body, name=name, out_shape=jax.ShapeDtypeStruct(gathered.shape[1:], F32))(gathered)


def kernel(x, ffn1_norm, ffn1_w_gate, ffn1_w_up, ffn1_w_down, mix_norm, w_in, conv_w, attn_sinks, w_out, ffn2_norm, ffn2_w_gate, ffn2_w_up, ffn2_w_down, final_norm, loss_target, m_ffn1_norm, m_ffn1_w_gate, m_ffn1_w_up, m_ffn1_w_down, m_mix_norm, m_w_in, m_conv_w, m_attn_sinks, m_w_out, m_ffn2_norm, m_ffn2_w_gate, m_ffn2_w_up, m_ffn2_w_down, m_final_norm, v_ffn1_norm, v_ffn1_w_gate, v_ffn1_w_up, v_ffn1_w_down, v_mix_norm, v_w_in, v_conv_w, v_attn_sinks, v_w_out, v_ffn2_norm, v_ffn2_w_gate, v_ffn2_w_up, v_ffn2_w_down, v_final_norm):
    ix, iy, ic = lax.axis_index("x"), lax.axis_index("y"), lax.axis_index("c")
    my_index = 4 * ix + 2 * iy + ic
    core = ic.astype(jnp.int32).reshape(1)
    chip = (2 * ix + iy).astype(jnp.int32).reshape(1)

    given = dict(ffn1_norm=ffn1_norm, ffn1_w_gate=ffn1_w_gate, ffn1_w_up=ffn1_w_up, ffn1_w_down=ffn1_w_down,
                 mix_norm=mix_norm, w_in=w_in, conv_w=conv_w, attn_sinks=attn_sinks, w_out=w_out, ffn2_norm=ffn2_norm,
                 ffn2_w_gate=ffn2_w_gate, ffn2_w_up=ffn2_w_up, ffn2_w_down=ffn2_w_down, final_norm=final_norm)
    moments_m = dict(ffn1_norm=m_ffn1_norm, ffn1_w_gate=m_ffn1_w_gate, ffn1_w_up=m_ffn1_w_up, ffn1_w_down=m_ffn1_w_down,
                     mix_norm=m_mix_norm, w_in=m_w_in, conv_w=m_conv_w, attn_sinks=m_attn_sinks, w_out=m_w_out,
                     ffn2_norm=m_ffn2_norm, ffn2_w_gate=m_ffn2_w_gate, ffn2_w_up=m_ffn2_w_up, ffn2_w_down=m_ffn2_w_down,
                     final_norm=m_final_norm)
    moments_v = dict(ffn1_norm=v_ffn1_norm, ffn1_w_gate=v_ffn1_w_gate, ffn1_w_up=v_ffn1_w_up, ffn1_w_down=v_ffn1_w_down,
                     mix_norm=v_mix_norm, w_in=v_w_in, conv_w=v_conv_w, attn_sinks=v_attn_sinks, w_out=v_w_out,
                     ffn2_norm=v_ffn2_norm, ffn2_w_gate=v_ffn2_w_gate, ffn2_w_up=v_ffn2_w_up, ffn2_w_down=v_ffn2_w_down,
                     final_norm=v_final_norm)

    xs = x[0]
    target = loss_target[0]
    final_gain = final_norm.reshape(1, D_MODEL)

    def ffn_shard(wg, wu, wd):
        return jnp.stack([wg[0].T, wu[0].T, wd[0]]).astype(BF16)

    conv_cols = conv_w.shape[2]
    conv_shard = jnp.pad(conv_w[0], ((0, 5), (0, 128 - conv_cols)))
    rest_shards = [ffn_shard(ffn2_w_gate, ffn2_w_up, ffn2_w_down), w_in[0].T.astype(BF16), w_out[0].astype(BF16),
                   conv_shard]
    rest_rows = [s.shape[-2] for s in rest_shards]
    n_rest = len(rest_shards)
    (w1,), _ = _all_gather_rows([ffn_shard(ffn1_w_gate, ffn1_w_up, ffn1_w_down)], name="gather_ffn1")

    fulls = [lax.empty(s.shape[:-2] + (N_DEV * s.shape[-2], s.shape[-1]), s.dtype) for s in rest_shards]
    fulls = _place_own(fulls, rest_shards, my_index.astype(jnp.int32).reshape(1), name="place_own_weights")
    send_plan = _gather_send_plan(n_rest)
    ssem, rsem, bufs, token = _split_start(rest_shards + list(fulls), 4 * n_rest, send_plan, name="gather_rest_start",
                                           after=[w1])
    x1, h1, a1, b1 = _ffn_fwd(xs, ffn1_norm, w1, name="ffn1_fwd", after=[token])
    bufs = _split_wait(ssem, rsem, bufs, x1, send_plan, name="gather_rest_wait")
    w2_part, mixer_parts = bufs[n_rest], bufs[n_rest + 1:]
    fwd_mixer = _gather_forward_plan(rest_rows[1:])
    ssem, rsem, bufs, token = _split_start(mixer_parts, 3 * (n_rest - 1), fwd_mixer, name="forward_mixer_start")
    win_t, wout, conv_all = _split_wait(ssem, rsem, bufs, token, fwd_mixer, name="forward_mixer_wait")
    conv_full = conv_all.reshape(N_DEV, 8, 128)[:, :3, :conv_cols].transpose(1, 0, 2).reshape(3, CONV_W)
    fwd_ffn2 = _gather_forward_plan(rest_rows[:1])
    ssem, rsem, bufs, token = _split_start([w2_part], 3, fwd_ffn2, name="forward_ffn2_start", after=[win_t])
    rope = _rope_tables(xs.shape[0])
    x2, hm, z, y = _mixer_fwd(x1, mix_norm, win_t, wout, conv_full, attn_sinks, rope, name="mixer_fwd", after=[token])
    (w2,) = _split_wait(ssem, rsem, bufs, x2, fwd_ffn2, name="forward_ffn2_wait")
    x3, h2, a2, b2 = _ffn_fwd(x2, ffn2_norm, w2, name="ffn2_fwd")
    dx3, loss_local, d_final = _loss_head(x3, final_gain, target, name="loss_head")
    loss = lax.psum(loss_local[0, 0], ("x", "y", "c"))

    def to_sibling_start(grads, tag, after=()):
        views = [g.reshape(4, 2, g.shape[0] // N_DEV, D_MODEL) for g in grads]
        lands = [lax.empty((4,) + v.shape[2:], F32) for v in views]
        plan = _sibling_plan(len(views))
        ssem, rsem, bufs, token = _split_start(views + lands, len(views), plan, name=f"{tag}_sibling_start", after=after)
        return (ssem, rsem, bufs, plan, tag), token

    def to_sibling_finish(handle, after, names):
        ssem, rsem, bufs, plan, tag = handle
        bufs = _split_wait(ssem, rsem, bufs, after, plan, name=f"{tag}_sibling_wait")
        n = len(names)
        return [_add_sibling(v, r, core, name=f"add_sibling_{nm}", tr=v.shape[2] // 2)
                for v, r, nm in zip(bufs[:n], bufs[n:], names)]

    def to_chips_start(partials, tag, small_all=None, after=()):
        p16 = [p for _, p in partials]
        lands = [lax.empty((3,) + p.shape[1:], BF16) for p in p16]
        extra = [] if small_all is None else [small_all]
        plan = _chips_plan(len(p16), small_all is not None)
        ssem, rsem, bufs, token = _split_start(p16 + lands + extra, 3 * len(p16) + 7 * len(extra), plan,
                                               name=f"{tag}_chips_start", after=after)
        return (ssem, rsem, bufs, plan, tag), token

    def to_chips_finish(handle, partials, after, names):
        ssem, rsem, bufs, plan, tag = handle
        bufs = _split_wait(ssem, rsem, bufs, after, plan, name=f"{tag}_chips_wait")
        n = len(names)
        return [(p32, r) for (p32, _), r in zip(partials, bufs[n:2 * n])], bufs[2 * n:]

    half_ff = D_FF // 2
    names2, namesm = ["ffn2_w_gate", "ffn2_w_up", "ffn2_w_down"], ["w_in", "w_out"]
    transposed = {"ffn1_w_gate", "ffn1_w_up", "w_in", "ffn2_w_gate", "ffn2_w_up"}
    grad, delta, new_m, new_v = {}, {}, {}, {}

    def adam_big(nm, parts, after=()):
        to_rows = (lambda a: a[0].T) if nm in transposed else (lambda a: a[0])
        from_rows = (lambda a: a.T[None]) if nm in transposed else (lambda a: a[None])
        p32, recv = parts
        outs = _reduce_adamw(p32, recv, chip, to_rows(given[nm]), to_rows(moments_m[nm]), to_rows(moments_v[nm]),
                             name=f"adamw_{nm}", tr=p32.shape[1] // 2, after=after)
        grad[nm], delta[nm], new_m[nm], new_v[nm] = (from_rows(a) for a in outs)

    dx2, da2, db2, s2, g2b, d_norm2 = _ffn_dgrad(dx3, x2, ffn2_norm, a2, b2, w2, name="ffn2_dgrad")
    gw2 = [_tn_matmul(da2, h2, name="ffn2_wgrad_gate", bm=half_ff), _tn_matmul(db2, h2, name="ffn2_wgrad_up", bm=half_ff),
           _tn_matmul(s2, g2b, name="ffn2_wgrad_down", bm=half_ff)]
    sib2, tok = to_sibling_start(gw2, "ffn2")
    dx1, dz, gmb, d_conv, d_sink, d_normm = _mixer_bwd(dx2, x1, mix_norm, y, z, win_t, wout, conv_full, attn_sinks,
                                                       rope, name="mixer_bwd", after=[tok])
    p2 = to_sibling_finish(sib2, dx1, names2)
    chips2, tok = to_chips_start(p2, "ffn2")
    gwm = [_tn_matmul(dz, hm, name="mixer_wgrad_in", bm=Z_W // 3, after=[tok]),
           _tn_matmul(y, gmb, name="mixer_wgrad_out", bm=D_MODEL // 2, after=[tok])]
    sibm, tok = to_sibling_start(gwm, "mixer")
    dx0, da1, db1, s1, g1b, d_norm1 = _ffn_dgrad(dx1, xs, ffn1_norm, a1, b1, w1, name="ffn1_dgrad", after=[tok])
    r2, _ = to_chips_finish(chips2, p2, dx0, names2)
    pm = to_sibling_finish(sibm, dx0, namesm)
    chipsm, tok = to_chips_start(pm, "mixer")
    gw_gate = _tn_matmul(da1, h1, name="ffn1_wgrad_gate", bm=half_ff, after=[tok])
    sib_gate, tok = to_sibling_start([gw_gate], "ffn1_gate")
    gw_up = _tn_matmul(db1, h1, name="ffn1_wgrad_up", bm=half_ff, after=[tok])
    rm, _ = to_chips_finish(chipsm, pm, gw_up, namesm)
    p_gate = to_sibling_finish(sib_gate, gw_up, ["ffn1_w_gate"])
    chips_gate, tok_a = to_chips_start(p_gate, "ffn1_gate")
    sib_up, tok_b = to_sibling_start([gw_up], "ffn1_up", after=[tok_a])
    gw_down = _tn_matmul(s1, g1b, name="ffn1_wgrad_down", bm=half_ff, after=[tok_a, tok_b])
    p_up = to_sibling_finish(sib_up, gw_down, ["ffn1_w_up"])
    chips_up, tok_a = to_chips_start(p_up, "ffn1_up")
    sib_down, tok_b = to_sibling_start([gw_down], "ffn1_down", after=[tok_a])
    for nm, g in zip(names2 + namesm, r2 + rm):
        adam_big(nm, g, after=[tok_a, tok_b])
    r_gate, _ = to_chips_finish(chips_gate, p_gate, new_v["w_out"], ["ffn1_w_gate"])
    p_down = to_sibling_finish(sib_down, new_v["w_out"], ["ffn1_w_down"])
    small = jnp.concatenate([
        d_norm1, d_normm, d_norm2, d_final,
        jnp.pad(d_conv[0:3], ((0, 0), (0, D_MODEL - CONV_W))), jnp.pad(d_sink, ((0, 0), (0, D_MODEL - 128)))], axis=0)
    (small_all,) = _place_own([lax.empty((N_DEV * 8, D_MODEL), F32)], [small], my_index.astype(jnp.int32).reshape(1),
                              name="place_own_small")
    chips_down, tok = to_chips_start(p_down, "ffn1_down", small_all)
    adam_big("ffn1_w_gate", r_gate[0], after=[tok])
    r_up, _ = to_chips_finish(chips_up, p_up, new_v["ffn1_w_gate"], ["ffn1_w_up"])
    adam_big("ffn1_w_up", r_up[0])
    r_down, (small_all,) = to_chips_finish(chips_down, p_down, new_v["ffn1_w_up"], ["ffn1_w_down"])
    adam_big("ffn1_w_down", r_down[0])
    small_sum = _sum_small(small_all.reshape(N_DEV, 8, D_MODEL), name="sum_small")
    _update_small(given, moments_m, moments_v, small_sum, my_index, grad, delta, new_m, new_v)

    order = list(given)
    return (loss, dx0[None], *[grad[n] for n in order], *[delta[n] for n in order],
            *[new_m[n] for n in order], *[new_v[n] for n in order])


def _update_small(given, moments_m, moments_v, small_sum, my_index, grad, delta, new_m, new_v):
    conv_cols = given["conv_w"].shape[2]
    small_g = {
        "ffn1_norm": small_sum[0:1], "mix_norm": small_sum[1:2], "ffn2_norm": small_sum[2:3],
        "final_norm": small_sum[3:4],
        "conv_w": lax.dynamic_slice(small_sum[4:7, :CONV_W], (0, my_index * conv_cols), (3, conv_cols)),
        "attn_sinks": small_sum[7:8, :N_Q_HEADS],
    }

    small_names = ["ffn1_norm", "mix_norm", "ffn2_norm", "final_norm", "conv_w", "attn_sinks"]

    def pack(parts):
        rows = []
        for nm in small_names:
            p = parts[nm]
            p2 = p.reshape(3, conv_cols) if nm == "conv_w" else p.reshape(1, -1)
            rows.append(jnp.pad(p2, ((0, 0), (0, D_MODEL - p2.shape[1]))))
        rows.append(jnp.zeros((8, D_MODEL), F32))
        return jnp.concatenate(rows, axis=0)

    sd, sm, sv = _adamw(pack(given), pack(small_g), pack(moments_m), pack(moments_v), name="adamw_small", tr=16)
    row = 0
    for nm in small_names:
        shape = given[nm].shape
        nrow = 3 if nm == "conv_w" else 1
        ncol = conv_cols if nm == "conv_w" else given[nm].size
        grad[nm] = small_g[nm].reshape(shape)
        delta[nm], new_m[nm], new_v[nm] = (a[row:row + nrow, :ncol].reshape(shape) for a in (sd, sm, sv))
        row += nrow
```

```python
import functools

import jax
import jax.numpy as jnp
from jax import lax
from jax.experimental import pallas as pl
from jax.experimental.pallas import tpu as pltpu

F32 = jnp.float32
BF16 = jnp.bfloat16
MESH = pl.DeviceIdType.MESH
ANY = pl.BlockSpec(memory_space=pl.ANY)
HBM_SPEC = pl.BlockSpec(memory_space=pltpu.HBM)
SEM_SPEC = pl.BlockSpec(memory_space=pltpu.SEMAPHORE)
DATAFLOW = pltpu.SideEffectType.DATAFLOW_SIDE_EFFECTING

N_DEV = 8
LOSS_LANE = 128
D_MODEL = 1024
D_FF = 2816
CONV_W = 512
ATTN_W = 512
KV_W = 128
HEAD_DIM = 64
N_Q_HEADS = 8
N_KV_HEADS = 2
Q_PER_KV = N_Q_HEADS // N_KV_HEADS
BLOCK = 128
ROT_DIM = 16
ROPE_THETA = 500000.0
Z_W = 3 * CONV_W + ATTN_W + 2 * KV_W
Q_OFF = 3 * CONV_W
K_OFF = Q_OFF + ATTN_W
V_OFF = K_OFF + KV_W
RMS_EPS = 1e-5
MASK_VALUE = -1e30
SM_SCALE = HEAD_DIM ** -0.5
FFN_RES_SCALE = 0.5

ADAM_LR = 0.001
ADAM_B1 = 0.9
ADAM_B2 = 0.999
ADAM_EPS = 1e-08
ADAM_WD = 0.01
ADAM_STEP = 10

NT_DIMS = (((1,), (1,)), ((), ()))
TN_DIMS = (((0,), (0,)), ((), ()))

VMEM_LIMIT = 56 * 1024 * 1024
FF_CHUNK = 256


def _params(sem, vmem=None):
    return pltpu.CompilerParams(dimension_semantics=sem, vmem_limit_bytes=vmem)


def _behind(body, n_in, after):
    k = len(after)
    if k == 0:
        return body
    return lambda *refs: body(*refs[:n_in], *refs[n_in + k:])


def _rms_stats(xf):
    inv = lax.rsqrt(jnp.mean(xf * xf, axis=-1, keepdims=True) + RMS_EPS)
    return xf * inv, inv


def _rms_bwd(dh, xhat, inv, gain):
    dxhat = dh * gain
    dx = inv * (dxhat - xhat * jnp.mean(dxhat * xhat, axis=-1, keepdims=True))
    dgain = jnp.sum(dh * xhat, axis=0, keepdims=True)
    return dx, dgain


def _load_resident(w_hbm, w_ref, sem):
    @pl.when(pl.program_id(0) == 0)
    def _():
        cp = pltpu.make_async_copy(w_hbm, w_ref, sem)
        cp.start()
        cp.wait()


def _ffn_fwd(x, gain, w3, *, name, after=(), tm=256, tf=FF_CHUNK):
    t = x.shape[0]
    tm = min(tm, t)

    def body(x_ref, g_ref, w_hbm, xo_ref, h_ref, a_ref, b_ref, w_ref, s_ref, sem):
        _load_resident(w_hbm, w_ref, sem)
        xf = x_ref[...]
        xhat, _ = _rms_stats(xf)
        h = (xhat * g_ref[...]).astype(BF16)
        h_ref[...] = h
        for c in range(0, D_FF, tf):
            a = lax.dot_general(h, w_ref[0, c:c + tf, :], NT_DIMS, preferred_element_type=F32)
            b = lax.dot_general(h, w_ref[1, c:c + tf, :], NT_DIMS, preferred_element_type=F32)
            a_ref[:, c:c + tf] = a.astype(BF16)
            b_ref[:, c:c + tf] = b.astype(BF16)
            s_ref[:, c:c + tf] = (a * jax.nn.sigmoid(a) * b).astype(BF16)
        xo_ref[...] = xf + FFN_RES_SCALE * jnp.dot(s_ref[...], w_ref[2], preferred_element_type=F32)

    row = pl.BlockSpec((tm, D_MODEL), lambda i: (i, 0))
    hid = pl.BlockSpec((tm, D_FF), lambda i: (i, 0))
    return pl.pallas_call(
        _behind(body, 3, after), name=name, grid=(t // tm,),
        in_specs=[row, pl.BlockSpec((1, D_MODEL), lambda i: (0, 0)), ANY] + [ANY] * len(after),
        out_specs=[row, row, hid, hid],
        out_shape=[jax.ShapeDtypeStruct((t, D_MODEL), F32), jax.ShapeDtypeStruct((t, D_MODEL), BF16),
                   jax.ShapeDtypeStruct((t, D_FF), BF16), jax.ShapeDtypeStruct((t, D_FF), BF16)],
        scratch_shapes=[pltpu.VMEM((3, D_FF, D_MODEL), BF16), pltpu.VMEM((tm, D_FF), BF16),
                        pltpu.SemaphoreType.DMA(())],
        compiler_params=_params(("arbitrary",), VMEM_LIMIT),
    )(x, gain, w3, *after)


def _ffn_dgrad(dxo, x, gain, a, b, w3, *, name, head=None, after=(), tm=256, tf=FF_CHUNK):
    t = x.shape[0]
    tm = min(tm, t)
    n_head = 0 if head is None else 2

    def body(*refs):
        dxo_ref, x_ref, g_ref, a_ref, b_ref, w_hbm = refs[:6]
        head_refs = refs[6:6 + n_head]
        dxi_ref, da_ref, db_ref, s_ref, gb_ref, dg_ref = refs[6 + n_head:12 + n_head]
        head_outs = refs[12 + n_head:12 + 2 * n_head]
        w_ref, sem = refs[12 + 2 * n_head:]
        _load_resident(w_hbm, w_ref, sem)

        @pl.when(pl.program_id(0) == 0)
        def _():
            dg_ref[...] = jnp.zeros_like(dg_ref)
            for ref in head_outs:
                ref[...] = jnp.zeros_like(ref)

        if head is None:
            go = dxo_ref[...]
        else:
            fg_ref, t_ref = head_refs
            loss_ref, dfg_ref = head_outs
            xhat_o, inv_o = _rms_stats(dxo_ref[...])
            err = xhat_o * fg_ref[...] - t_ref[...]
            loss_ref[...] += 0.5 * jnp.sum(jnp.mean(err * err, axis=-1, keepdims=True), axis=0, keepdims=True)
            go, dfg = _rms_bwd(err * (1.0 / D_MODEL), xhat_o, inv_o, fg_ref[...])
            dfg_ref[...] += dfg
        gb = (FFN_RES_SCALE * go).astype(BF16)
        gb_ref[...] = gb
        for c in range(0, D_FF, tf):
            ds = lax.dot_general(gb, w_ref[2, c:c + tf, :], NT_DIMS, preferred_element_type=F32)
            af = a_ref[:, c:c + tf].astype(F32)
            bf = b_ref[:, c:c + tf].astype(F32)
            sig = jax.nn.sigmoid(af)
            silu = af * sig
            da_ref[:, c:c + tf] = (ds * bf * (sig * (1.0 + af * (1.0 - sig)))).astype(BF16)
            db_ref[:, c:c + tf] = (ds * silu).astype(BF16)
            s_ref[:, c:c + tf] = (silu * bf).astype(BF16)
        dh = (jnp.dot(da_ref[...], w_ref[0], preferred_element_type=F32)
              + jnp.dot(db_ref[...], w_ref[1], preferred_element_type=F32))
        xhat, inv = _rms_stats(x_ref[...])
        dx, dgain = _rms_bwd(dh, xhat, inv, g_ref[...])
        dxi_ref[...] = go + dx
        dg_ref[...] += dgain

    row = pl.BlockSpec((tm, D_MODEL), lambda i: (i, 0))
    hid = pl.BlockSpec((tm, D_FF), lambda i: (i, 0))
    vec = pl.BlockSpec((1, D_MODEL), lambda i: (0, 0))
    head_in = [] if head is None else [vec, row]
    head_out = [] if head is None else [pl.BlockSpec((1, 1), lambda i: (0, 0)), vec]
    head_shape = [] if head is None else [jax.ShapeDtypeStruct((1, 1), F32), jax.ShapeDtypeStruct((1, D_MODEL), F32)]
    return pl.pallas_call(
        _behind(body, 6 + n_head, after), name=name, grid=(t // tm,),
        in_specs=[row, row, vec, hid, hid, ANY] + head_in + [ANY] * len(after),
        out_specs=[row, hid, hid, hid, row, vec] + head_out,
        out_shape=[jax.ShapeDtypeStruct((t, D_MODEL), F32), jax.ShapeDtypeStruct((t, D_FF), BF16),
                   jax.ShapeDtypeStruct((t, D_FF), BF16), jax.ShapeDtypeStruct((t, D_FF), BF16),
                   jax.ShapeDtypeStruct((t, D_MODEL), BF16), jax.ShapeDtypeStruct((1, D_MODEL), F32)] + head_shape,
        scratch_shapes=[pltpu.VMEM((3, D_FF, D_MODEL), BF16), pltpu.SemaphoreType.DMA(())],
        compiler_params=_params(("arbitrary",), VMEM_LIMIT),
    )(dxo, x, gain, a, b, w3, *(head or ()), *after)


def _tn_matmul(a, b, *, name, bm, after=(), tk=2048):
    t, m = a.shape
    n = b.shape[1]
    tk = min(tk, t)
    nk = t // tk

    def body(a_ref, b_ref, o_ref):
        @pl.when(pl.program_id(1) == 0)
        def _():
            o_ref[...] = jnp.zeros_like(o_ref)

        o_ref[...] += lax.dot_general(a_ref[...], b_ref[...], TN_DIMS, preferred_element_type=F32)

    return pl.pallas_call(
        _behind(body, 2, after), name=name, grid=(m // bm, nk),
        in_specs=[pl.BlockSpec((tk, bm), lambda i, k: (k, i)), pl.BlockSpec((tk, n), lambda i, k: (k, 0))]
        + [ANY] * len(after),
        out_specs=pl.BlockSpec((bm, n), lambda i, k: (i, 0)),
        out_shape=jax.ShapeDtypeStruct((m, n), F32),
        compiler_params=_params(("parallel", "arbitrary"), VMEM_LIMIT),
    )(a, b, *after)


def _rope_tables(t):
    half = ROT_DIM // 2
    inv_freq = ROPE_THETA ** (-jnp.arange(0, ROT_DIM, 2, dtype=F32) / ROT_DIM)
    d = jnp.arange(128) % HEAD_DIM
    ang = jnp.arange(t, dtype=F32)[:, None] * inv_freq[d % half][None, :]
    cos, sin = jnp.cos(ang), jnp.sin(ang)
    mult = jnp.where(d < ROT_DIM, cos, 1.0)
    from_lo = jnp.where((d >= half) & (d < ROT_DIM), sin, 0.0)
    from_hi = jnp.where(d < half, -sin, 0.0)
    return jnp.stack([mult, from_lo, from_hi])


def _tile_lanes(tab, width):
    return jnp.tile(tab, (1, width // tab.shape[1]))


def _rope(v, tab):
    w = v.shape[1]
    half_rot = ROT_DIM // 2
    return (v * _tile_lanes(tab[0], w)
            + pltpu.roll(v, half_rot, axis=1) * _tile_lanes(tab[1], w)
            + pltpu.roll(v, w - half_rot, axis=1) * _tile_lanes(tab[2], w))


def _rope_bwd(dv, tab):
    w = dv.shape[1]
    half_rot = ROT_DIM // 2
    return (dv * _tile_lanes(tab[0], w)
            + pltpu.roll(dv * _tile_lanes(tab[1], w), w - half_rot, axis=1)
            + pltpu.roll(dv * _tile_lanes(tab[2], w), half_rot, axis=1))


def _shift_rows(v, prev8_ref, n):
    r = lax.broadcasted_iota(jnp.int32, v.shape, 0)
    rolled = pltpu.roll(v, n, axis=0)
    last = prev8_ref[7:8, :]
    if n == 1:
        return jnp.where(r >= 1, rolled, last)
    return jnp.where(r >= 2, rolled, jnp.where(r == 0, prev8_ref[6:7, :], last))


def _shift_rows_up(v, next8_ref, n):
    rows = v.shape[0]
    r = lax.broadcasted_iota(jnp.int32, v.shape, 0)
    rolled = pltpu.roll(v, rows - n, axis=0)
    first = next8_ref[0:1, :]
    if n == 1:
        return jnp.where(r <= rows - 2, rolled, first)
    return jnp.where(r <= rows - 3, rolled, jnp.where(r == rows - 2, first, next8_ref[1:2, :]))


def _lane_half_mask(shape, half):
    lane = lax.broadcasted_iota(jnp.int32, shape, 1)
    return (lane >= HEAD_DIM) if half else (lane < HEAD_DIM)


def _to_kv_lanes(chunk, head, kv):
    if head % 2 != kv:
        chunk = pltpu.roll(chunk, HEAD_DIM, axis=1)
    return jnp.where(_lane_half_mask(chunk.shape, kv), chunk, 0.0)


def _from_kv_lanes(chunk, head, kv):
    chunk = jnp.where(_lane_half_mask(chunk.shape, kv), chunk, 0.0)
    if head % 2 != kv:
        chunk = pltpu.roll(chunk, HEAD_DIM, axis=1)
    return chunk


def _stack_heads(wide, kv):
    parts = []
    for g in range(Q_PER_KV):
        head = kv * Q_PER_KV + g
        chunk = wide[:, (head // 2) * 128:(head // 2 + 1) * 128]
        parts.append(_to_kv_lanes(chunk, head, kv))
    return jnp.concatenate(parts, axis=0)


def _window_mask(has_prev):
    shape = (Q_PER_KV * BLOCK, 2 * BLOCK)
    qi = lax.broadcasted_iota(jnp.int32, shape, 0) & (BLOCK - 1)
    kj = lax.broadcasted_iota(jnp.int32, shape, 1)
    first_key = BLOCK - has_prev * BLOCK
    in_prev = (kj < BLOCK) & (kj > qi) & (kj >= first_key)
    in_own = (kj >= BLOCK) & ((kj - BLOCK) <= qi)
    return in_prev | in_own


def _sink_column(sink_ref, kv):
    row = lax.broadcasted_iota(jnp.int32, (Q_PER_KV * BLOCK, 1), 0)
    col = jnp.full((Q_PER_KV * BLOCK, 1), sink_ref[0, kv * Q_PER_KV], F32)
    for g in range(1, Q_PER_KV):
        col = jnp.where(row >= g * BLOCK, sink_ref[0, kv * Q_PER_KV + g], col)
    return col


def _softmax_with_sink(q4, k2, mask, sink):
    s = lax.dot_general(q4, k2, NT_DIMS, preferred_element_type=F32) * SM_SCALE
    s = jnp.where(mask, s, MASK_VALUE)
    m = jnp.maximum(jnp.max(s, axis=-1, keepdims=True), sink)
    p = jnp.exp(s - m)
    e_sink = jnp.exp(sink - m)
    inv_den = 1.0 / (jnp.sum(p, axis=-1, keepdims=True) + e_sink)
    return p * inv_den, e_sink * inv_den


def _conv_terms(zf, prev8_ref, w_ref):
    b_gate, c_gate, u = zf[:, 0:CONV_W], zf[:, CONV_W:2 * CONV_W], zf[:, 2 * CONV_W:3 * CONV_W]
    vc = c_gate * u
    vm1 = _shift_rows(vc, prev8_ref, 1)
    vm2 = _shift_rows(vc, prev8_ref, 2)
    conv = w_ref[0:1, :] * vm2 + w_ref[1:2, :] * vm1 + w_ref[2:3, :] * vc
    return b_gate, c_gate, u, vc, vm1, vm2, conv


def _mixer_fwd(x, gain, win_t, wout, conv_w, sinks, rope, *, name, after=(), tq=512):
    t = x.shape[0]
    tq = min(tq, t)
    nblk = tq // BLOCK

    def body(x_ref, g_ref, win_hbm, wout_hbm, cw_ref, sink_ref, rope_ref,
             xo_ref, h_ref, z_ref, y_ref, kprev_ref, vprev_ref, cprev_ref, win_ref, wout_ref, sems):
        i = pl.program_id(0)
        _load_resident(win_hbm, win_ref, sems.at[0])
        _load_resident(wout_hbm, wout_ref, sems.at[1])

        @pl.when(i == 0)
        def _():
            kprev_ref[...] = jnp.zeros_like(kprev_ref)
            vprev_ref[...] = jnp.zeros_like(vprev_ref)
            cprev_ref[...] = jnp.zeros_like(cprev_ref)

        xf = x_ref[...]
        xhat, _ = _rms_stats(xf)
        h = (xhat * g_ref[...]).astype(BF16)
        h_ref[...] = h
        zb = lax.dot_general(h, win_ref[...], NT_DIMS, preferred_element_type=F32).astype(BF16)
        z_ref[...] = zb
        zf = zb.astype(F32)

        b_gate, _, _, vc, _, _, conv = _conv_terms(zf, cprev_ref, cw_ref)
        y_conv = b_gate * conv
        cprev_ref[...] = vc[tq - 8:tq, :]

        tab = rope_ref[...]
        qr = _rope(zf[:, Q_OFF:K_OFF], tab)
        kr = _rope(zf[:, K_OFF:V_OFF], tab).astype(BF16)
        vb = zb[:, V_OFF:Z_W]

        y_attn = []
        for j in range(nblk):
            rows = slice(j * BLOCK, (j + 1) * BLOCK)
            prev = slice((j - 1) * BLOCK, j * BLOCK)
            k2 = jnp.concatenate([kprev_ref[...] if j == 0 else kr[prev], kr[rows]], axis=0)
            v2 = jnp.concatenate([vprev_ref[...] if j == 0 else vb[prev], vb[rows]], axis=0)
            mask = _window_mask(jnp.minimum(i, 1) if j == 0 else 1)
            chunks = [jnp.zeros((BLOCK, 128), F32) for _ in range(ATTN_W // 128)]
            for kv in range(N_KV_HEADS):
                q4 = _stack_heads(qr[rows], kv).astype(BF16)
                probs, _ = _softmax_with_sink(q4, k2, mask, _sink_column(sink_ref, kv))
                o4 = jnp.dot(probs.astype(BF16), v2, preferred_element_type=F32)
                for g in range(Q_PER_KV):
                    head = kv * Q_PER_KV + g
                    chunks[head // 2] += _from_kv_lanes(o4[g * BLOCK:(g + 1) * BLOCK], head, kv)
            y_attn.append(jnp.concatenate(chunks, axis=1))
        kprev_ref[...] = kr[tq - BLOCK:tq]
        vprev_ref[...] = vb[tq - BLOCK:tq]
        y = jnp.concatenate([y_conv, jnp.concatenate(y_attn, axis=0)], axis=1).astype(BF16)
        y_ref[...] = y
        xo_ref[...] = xf + jnp.dot(y, wout_ref[...], preferred_element_type=F32)

    row = pl.BlockSpec((tq, D_MODEL), lambda i: (i, 0))
    full = lambda shape: pl.BlockSpec(shape, lambda i: (0,) * len(shape))
    return pl.pallas_call(
        _behind(body, 7, after), name=name, grid=(t // tq,),
        in_specs=[row, full((1, D_MODEL)), ANY, ANY, full((3, CONV_W)),
                  pl.BlockSpec(memory_space=pltpu.SMEM), pl.BlockSpec((3, tq, 128), lambda i: (0, i, 0))]
        + [ANY] * len(after),
        out_specs=[row, row, pl.BlockSpec((tq, Z_W), lambda i: (i, 0)), row],
        out_shape=[jax.ShapeDtypeStruct((t, D_MODEL), F32), jax.ShapeDtypeStruct((t, D_MODEL), BF16),
                   jax.ShapeDtypeStruct((t, Z_W), BF16), jax.ShapeDtypeStruct((t, D_MODEL), BF16)],
        scratch_shapes=[pltpu.VMEM((BLOCK, KV_W), BF16), pltpu.VMEM((BLOCK, KV_W), BF16),
                        pltpu.VMEM((8, CONV_W), F32), pltpu.VMEM((Z_W, D_MODEL), BF16),
                        pltpu.VMEM((D_MODEL, D_MODEL), BF16), pltpu.SemaphoreType.DMA((2,))],
        compiler_params=_params(("arbitrary",), VMEM_LIMIT),
    )(x, gain, win_t, wout, conv_w, sinks, rope, *after)


def _mixer_bwd(dxo, x, gain, y, z, win_t, wout, conv_w, sinks, rope, *, name, after=(), tq=256):
    t = x.shape[0]
    tq = min(tq, t)
    nt, nblk = t // tq, tq // BLOCK

    def body(dxo_ref, x_ref, g_ref, y_ref, z_ref, zp_ref, win_hbm, wout_hbm, cw_ref, sink_ref, rope_ref, ropep_ref,
             dxi_ref, dz_ref, gb_ref, dcw_ref, dsink_ref, dg_ref, dk_ref, dv_ref, dcn_ref, pvc_ref,
             win_ref, wout_ref, sems):
        i = pl.program_id(0)
        tile = nt - 1 - i
        _load_resident(win_hbm, win_ref, sems.at[0])
        _load_resident(wout_hbm, wout_ref, sems.at[1])

        @pl.when(i == 0)
        def _():
            dk_ref[...] = jnp.zeros_like(dk_ref)
            dv_ref[...] = jnp.zeros_like(dv_ref)
            dcn_ref[...] = jnp.zeros_like(dcn_ref)
            dcw_ref[...] = jnp.zeros_like(dcw_ref)
            dsink_ref[...] = jnp.zeros_like(dsink_ref)
            dg_ref[...] = jnp.zeros_like(dg_ref)

        has_prev = jnp.minimum(tile, 1)
        go = dxo_ref[...]
        gb = go.astype(BF16)
        gb_ref[...] = gb
        dy = lax.dot_general(gb, wout_ref[...], NT_DIMS, preferred_element_type=F32)
        dy_conv, dy_attn = dy[:, 0:CONV_W], dy[:, CONV_W:D_MODEL]
        zb, zpb = z_ref[...], zp_ref[...]
        zf = zb.astype(F32)
        zpf = zpb.astype(F32) * has_prev.astype(F32)

        pvc_ref[...] = (zpf[:, CONV_W:2 * CONV_W] * zpf[:, 2 * CONV_W:3 * CONV_W])[BLOCK - 8:BLOCK, :]
        b_gate, c_gate, u, vc, vm1, vm2, conv = _conv_terms(zf, pvc_ref, cw_ref)
        d_bgate = dy_conv * conv
        dc = dy_conv * b_gate
        tap = lax.broadcasted_iota(jnp.int32, (8, CONV_W), 0)
        dcw_ref[...] += jnp.where(tap == 0, jnp.sum(dc * vm2, axis=0, keepdims=True),
                                  jnp.where(tap == 1, jnp.sum(dc * vm1, axis=0, keepdims=True),
                                            jnp.where(tap == 2, jnp.sum(dc * vc, axis=0, keepdims=True), 0.0)))
        dvc = (cw_ref[2:3, :] * dc + cw_ref[1:2, :] * _shift_rows_up(dc, dcn_ref, 1)
               + cw_ref[0:1, :] * _shift_rows_up(dc, dcn_ref, 2))
        dcn_ref[...] = dc[0:8, :]
        d_cgate = dvc * u
        d_u = dvc * c_gate

        tab, tabp = rope_ref[...], ropep_ref[...]
        qr = _rope(zf[:, Q_OFF:K_OFF], tab)
        kr = _rope(zf[:, K_OFF:V_OFF], tab).astype(BF16)
        kpr = _rope(zpf[:, K_OFF:V_OFF], tabp).astype(BF16)
        vb, vpb = zb[:, V_OFF:Z_W], zpb[:, V_OFF:Z_W]
        out = y_ref[:, CONV_W:D_MODEL].astype(F32)
        do_out = dy_attn * out
        lane = lax.broadcasted_iota(jnp.int32, (1, 128), 1)
        dsink = jnp.zeros((1, 128), F32)
        dk_next, dv_next = dk_ref[...], dv_ref[...]
        dq_rows, dk_rows, dv_rows = [None] * nblk, [None] * nblk, [None] * nblk
        for j in reversed(range(nblk)):
            rows = slice(j * BLOCK, (j + 1) * BLOCK)
            prev = slice((j - 1) * BLOCK, j * BLOCK)
            k2 = jnp.concatenate([kpr if j == 0 else kr[prev], kr[rows]], axis=0)
            v2 = jnp.concatenate([vpb if j == 0 else vb[prev], vb[rows]], axis=0)
            mask = _window_mask(has_prev if j == 0 else 1)
            dk2 = jnp.zeros((2 * BLOCK, KV_W), F32)
            dv2 = jnp.zeros((2 * BLOCK, KV_W), F32)
            dq_chunks = [jnp.zeros((BLOCK, 128), F32) for _ in range(ATTN_W // 128)]
            for kv in range(N_KV_HEADS):
                q4 = _stack_heads(qr[rows], kv).astype(BF16)
                do4 = _stack_heads(dy_attn[rows], kv).astype(BF16)
                delta = jnp.sum(_stack_heads(do_out[rows], kv), axis=-1, keepdims=True)
                probs, p_sink = _softmax_with_sink(q4, k2, mask, _sink_column(sink_ref, kv))
                dp = lax.dot_general(do4, v2, NT_DIMS, preferred_element_type=F32)
                ds = (probs * (dp - delta) * SM_SCALE).astype(BF16)
                dq4 = jnp.dot(ds, k2, preferred_element_type=F32)
                dk2 += lax.dot_general(ds, q4, TN_DIMS, preferred_element_type=F32)
                dv2 += lax.dot_general(probs.astype(BF16), do4, TN_DIMS, preferred_element_type=F32)
                sink_terms = p_sink * delta
                for g in range(Q_PER_KV):
                    head = kv * Q_PER_KV + g
                    grp = slice(g * BLOCK, (g + 1) * BLOCK)
                    dq_chunks[head // 2] += _from_kv_lanes(dq4[grp], head, kv)
                    dsink = dsink - jnp.where(lane == head, jnp.sum(sink_terms[grp], axis=0, keepdims=True), 0.0)
            dq_rows[j] = jnp.concatenate(dq_chunks, axis=1)
            dk_rows[j] = dk2[BLOCK:] + dk_next
            dv_rows[j] = dv2[BLOCK:] + dv_next
            dk_next, dv_next = dk2[:BLOCK], dv2[:BLOCK]
        dk_ref[...] = dk_next
        dv_ref[...] = dv_next
        dsink_ref[...] += dsink
        dq = _rope_bwd(jnp.concatenate(dq_rows, axis=0), tab)
        dk = _rope_bwd(jnp.concatenate(dk_rows, axis=0), tab)
        dv = jnp.concatenate(dv_rows, axis=0)

        dzb = jnp.concatenate([d_bgate, d_cgate, d_u, dq, dk, dv], axis=1).astype(BF16)
        dz_ref[...] = dzb
        dh = jnp.dot(dzb, win_ref[...], preferred_element_type=F32)
        xhat, inv = _rms_stats(x_ref[...])
        dx, dgain = _rms_bwd(dh, xhat, inv, g_ref[...])
        dxi_ref[...] = go + dx
        dg_ref[...] += dgain

    rev = lambda i: (nt - 1 - i, 0)
    block_before = lambda i: jnp.maximum((nt - 1 - i) * nblk - 1, 0)
    row = pl.BlockSpec((tq, D_MODEL), rev)
    full = lambda shape: pl.BlockSpec(shape, lambda i: (0,) * len(shape))
    return pl.pallas_call(
        _behind(body, 12, after), name=name, grid=(nt,),
        in_specs=[row, row, full((1, D_MODEL)), row,
                  pl.BlockSpec((tq, Z_W), rev), pl.BlockSpec((BLOCK, Z_W), lambda i: (block_before(i), 0)),
                  ANY, ANY, full((3, CONV_W)),
                  pl.BlockSpec(memory_space=pltpu.SMEM),
                  pl.BlockSpec((3, tq, 128), lambda i: (0, nt - 1 - i, 0)),
                  pl.BlockSpec((3, BLOCK, 128), lambda i: (0, block_before(i), 0))] + [ANY] * len(after),
        out_specs=[row, pl.BlockSpec((tq, Z_W), rev), row, full((8, CONV_W)), full((1, 128)), full((1, D_MODEL))],
        out_shape=[jax.ShapeDtypeStruct((t, D_MODEL), F32), jax.ShapeDtypeStruct((t, Z_W), BF16),
                   jax.ShapeDtypeStruct((t, D_MODEL), BF16), jax.ShapeDtypeStruct((8, CONV_W), F32),
                   jax.ShapeDtypeStruct((1, 128), F32), jax.ShapeDtypeStruct((1, D_MODEL), F32)],
        scratch_shapes=[pltpu.VMEM((BLOCK, KV_W), F32), pltpu.VMEM((BLOCK, KV_W), F32), pltpu.VMEM((8, CONV_W), F32),
                        pltpu.VMEM((8, CONV_W), F32), pltpu.VMEM((Z_W, D_MODEL), BF16),
                        pltpu.VMEM((D_MODEL, D_MODEL), BF16), pltpu.SemaphoreType.DMA((2,))],
        compiler_params=_params(("arbitrary",), VMEM_LIMIT),
    )(dxo, x, gain, y, z, z, win_t, wout, conv_w, sinks, rope, rope, *after)


def _place():
    x, y, c = lax.axis_index("x"), lax.axis_index("y"), lax.axis_index("c")
    other_chips = [(1 - x, y), (x, 1 - y), (1 - x, 1 - y)]
    return x, y, c, other_chips


def _all_gather_rows(shards, place=(), *, name):
    n, p = len(shards), len(place)

    def body(*refs):
        srcs, place_srcs = refs[:n], refs[n:n + p]
        outs, place_outs = refs[n + p:2 * n + p], refs[2 * n + p:2 * (n + p)]
        send_sems, recv_sems, local_sems = refs[2 * (n + p):]
        x, y, c, chips = _place()
        me, sibling = (x, y, c), (x, y, 1 - c)

        def rows(t, px, py, pc):
            r = srcs[t].shape[-2]
            start = pl.multiple_of((4 * px + 2 * py + pc) * r, 16 if r % 16 == 0 else 8)
            if len(srcs[t].shape) == 3:
                return outs[t].at[:, pl.ds(start, r), :]
            return outs[t].at[pl.ds(start, r), :]

        def copy(t, k, block, to, own=False):
            return pltpu.make_async_remote_copy(
                src_ref=srcs[t] if own else rows(t, *block), dst_ref=rows(t, *block),
                send_sem=send_sems.at[t, k], recv_sem=recv_sems.at[t, k], device_id=to, device_id_type=MESH)

        mine = [pltpu.make_async_copy(srcs[t], rows(t, *me), local_sems.at[t]) for t in range(n)]
        mine += [pltpu.make_async_copy(place_srcs[q],
                                       _block_rows(place_outs[q], place_srcs[q].shape[-2], 4 * x + 2 * y + c),
                                       local_sems.at[n + q]) for q in range(p)]
        for q in range(p):
            mine[n + q].start()
        first = []
        for t in range(n):
            mine[t].start()
            first.append(copy(t, 0, me, sibling, own=True))
            first += [copy(t, 1 + j, me, (*chip, c), own=True) for j, chip in enumerate(chips)]
        for cp in first:
            cp.start()
        passed = []
        for j, chip in enumerate(chips):
            for t in range(n):
                copy(t, 1 + j, (*chip, c), me).wait_recv()
                fwd = copy(t, 4 + j, (*chip, c), sibling)
                fwd.start()
                passed.append(fwd)
        for t in range(n):
            copy(t, 0, sibling, me).wait_recv()
            for j, chip in enumerate(chips):
                copy(t, 4 + j, (*chip, 1 - c), me).wait_recv()
        for cp in first + passed:
            cp.wait_send()
        for cp in mine:
            cp.wait()

    out_shape = [jax.ShapeDtypeStruct(s.shape[:-2] + (N_DEV * s.shape[-2], s.shape[-1]), s.dtype)
                 for s in list(shards) + list(place)]
    res = pl.pallas_call(
        body, name=name, in_specs=[ANY] * (n + p), out_specs=[ANY] * (n + p), out_shape=out_shape,
        scratch_shapes=[pltpu.SemaphoreType.DMA((n, 7)), pltpu.SemaphoreType.DMA((n, 7)),
                        pltpu.SemaphoreType.DMA((n + p,))],
    )(*shards, *place)
    return res[:n], res[n:]


def _split_start(bufs, n_copies, plan, *, name, after=()):
    n = len(bufs)

    def body(*refs):
        token = refs[-1]
        for cp in plan(refs[:n], refs[n], refs[n + 1]):
            cp.start()
        token[...] = jnp.zeros_like(token)

    res = pl.pallas_call(
        _behind(body, n, after), name=name, in_specs=[HBM_SPEC] * n + [ANY] * len(after),
        out_specs=(SEM_SPEC, SEM_SPEC, *[HBM_SPEC] * n, pl.BlockSpec(memory_space=pltpu.VMEM)),
        out_shape=(pltpu.SemaphoreType.DMA((n_copies,)), pltpu.SemaphoreType.DMA((n_copies,)),
                   *[pltpu.HBM(b.shape, b.dtype) for b in bufs], jax.ShapeDtypeStruct((8, 128), F32)),
        input_output_aliases={i: 2 + i for i in range(n)},
        compiler_params=pltpu.CompilerParams(has_side_effects=DATAFLOW),
    )(*[pltpu.with_memory_space_constraint(b, pltpu.HBM) for b in bufs], *after)
    return res[0], res[1], list(res[2:2 + n]), res[-1]


def _split_wait(send_sems, recv_sems, bufs, after, plan, *, name):
    n = len(bufs)

    def body(*refs):
        for cp in plan(refs[:n], refs[n], refs[n + 1]):
            cp.wait_send()
            cp.wait_recv()

    return list(pl.pallas_call(
        body, name=name, in_specs=[HBM_SPEC] * n + [SEM_SPEC, SEM_SPEC, ANY], out_specs=[HBM_SPEC] * n,
        out_shape=tuple(pltpu.HBM(b.shape, b.dtype) for b in bufs),
        input_output_aliases={i: i for i in range(n)},
        compiler_params=pltpu.CompilerParams(has_side_effects=DATAFLOW),
    )(*bufs, send_sems, recv_sems, after))


def _sibling_plan(n):
    def plan(bufs, send_sems, recv_sems):
        x, y, c, _ = _place()
        return [pltpu.make_async_remote_copy(
            src_ref=bufs[t].at[:, 1 - c], dst_ref=bufs[n + t], send_sem=send_sems.at[t], recv_sem=recv_sems.at[t],
            device_id=(x, y, 1 - c), device_id_type=MESH) for t in range(n)]
    return plan


def _block_rows(ref, r, blk):
    start = pl.multiple_of(blk * r, 16 if r % 16 == 0 else 8)
    return ref.at[(slice(None),) * (len(ref.shape) - 2) + (pl.ds(start, r), slice(None))]


def _remote(src, dst, send_sems, recv_sems, k, peer):
    return pltpu.make_async_remote_copy(src_ref=src, dst_ref=dst, send_sem=send_sems.at[k], recv_sem=recv_sems.at[k],
                                        device_id=peer, device_id_type=MESH)


def _gather_send_plan(n):
    def plan(bufs, send_sems, recv_sems):
        x, y, c, chips = _place()
        peers = [(x, y, 1 - c)] + [(px, py, c) for px, py in chips]
        copies = []
        for t in range(n):
            dst = _block_rows(bufs[n + t], bufs[t].shape[-2], 4 * x + 2 * y + c)
            copies += [_remote(bufs[t], dst, send_sems, recv_sems, 4 * t + k, peer) for k, peer in enumerate(peers)]
        return copies
    return plan


def _gather_forward_plan(rows):
    def plan(bufs, send_sems, recv_sems):
        x, y, c, chips = _place()
        copies = []
        for t, r in enumerate(rows):
            for j, (px, py) in enumerate(chips):
                blk = _block_rows(bufs[t], r, 4 * px + 2 * py + c)
                copies.append(_remote(blk, blk, send_sems, recv_sems, 3 * t + j, (x, y, 1 - c)))
        return copies
    return plan


def _chips_plan(n, with_small):
    def plan(bufs, send_sems, recv_sems):
        x, y, c, chips = _place()
        copies = []
        for t in range(n):
            for j, (px, py) in enumerate(chips):
                copies.append(_remote(bufs[t].at[2 * px + py], bufs[n + t].at[j], send_sems, recv_sems, 3 * t + j,
                                      (px, py, c)))
        if with_small:
            mine = _block_rows(bufs[2 * n], 8, 4 * x + 2 * y + c)
            flips = [(fx, fy, fc) for fx in range(2) for fy in range(2) for fc in range(2)][1:]
            for k, (fx, fy, fc) in enumerate(flips):
                peer = (x + fx - 2 * x * fx, y + fy - 2 * y * fy, c + fc - 2 * c * fc)
                copies.append(_remote(mine, mine, send_sems, recv_sems, 3 * n + k, peer))
        return copies
    return plan


def _place_own(fulls, shards, index, *, name):
    n = len(fulls)

    def body(index_ref, *refs):
        for t in range(n):
            refs[2 * n + t][...] = refs[n + t][...]

    def block_of(shard):
        lead = len(shard.shape) - 2
        return pl.BlockSpec(shard.shape, lambda i, index_ref: (0,) * lead + (index_ref[0], 0))

    def whole(shard):
        return pl.BlockSpec(shard.shape, lambda i, index_ref: (0,) * len(shard.shape))

    return list(pl.pallas_call(
        body, name=name,
        grid_spec=pltpu.PrefetchScalarGridSpec(
            num_scalar_prefetch=1, grid=(1,),
            in_specs=[ANY] * n + [whole(s) for s in shards], out_specs=[block_of(s) for s in shards]),
        out_shape=[jax.ShapeDtypeStruct(f.shape, f.dtype) for f in fulls],
        input_output_aliases={1 + t: t for t in range(n)},
        compiler_params=_params(("arbitrary",)),
    )(index, *fulls, *shards))


def _add_sibling(grad, recv, core, *, name, tr):
    rows = grad.shape[2]

    def body(core_ref, g_ref, r_ref, o_ref, ob_ref):
        p = g_ref[:, 0] + r_ref[...]
        o_ref[...] = p
        ob_ref[...] = p.astype(BF16)

    out = pl.BlockSpec((4, tr, D_MODEL), lambda i, core_ref: (0, i, 0))
    return pl.pallas_call(
        body, name=name,
        grid_spec=pltpu.PrefetchScalarGridSpec(
            num_scalar_prefetch=1, grid=(rows // tr,),
            in_specs=[pl.BlockSpec((4, 1, tr, D_MODEL), lambda i, core_ref: (0, core_ref[0], i, 0)), out],
            out_specs=[out, out]),
        out_shape=[jax.ShapeDtypeStruct(recv.shape, F32), jax.ShapeDtypeStruct(recv.shape, BF16)],
        compiler_params=_params(("arbitrary",)),
    )(core, grad, recv)


def _reduce_adamw(partial, recv, chip, w, m, v, *, name, tr, after=()):
    rows = partial.shape[1]

    def body(chip_ref, p_ref, r_ref, w_ref, m_ref, v_ref, g_ref, d_ref, mo_ref, vo_ref):
        g = p_ref[0] + r_ref[0].astype(F32) + r_ref[1].astype(F32) + r_ref[2].astype(F32)
        g_ref[...] = g
        d_ref[...], mo_ref[...], vo_ref[...] = _adamw_math(w_ref[...], g, m_ref[...], v_ref[...])

    spec = pl.BlockSpec((tr, D_MODEL), lambda i, chip_ref: (i, 0))
    return pl.pallas_call(
        _behind(body, 6, after), name=name,
        grid_spec=pltpu.PrefetchScalarGridSpec(
            num_scalar_prefetch=1, grid=(rows // tr,),
            in_specs=[pl.BlockSpec((1, tr, D_MODEL), lambda i, chip_ref: (chip_ref[0], i, 0)),
                      pl.BlockSpec((3, tr, D_MODEL), lambda i, chip_ref: (0, i, 0)), spec, spec, spec]
            + [ANY] * len(after),
            out_specs=[spec] * 4),
        out_shape=[jax.ShapeDtypeStruct((rows, D_MODEL), F32)] * 4,
        compiler_params=_params(("arbitrary",)),
    )(chip, partial, recv, w, m, v, *after)


def _adamw_math(w, g, m, v):
    m = ADAM_B1 * m + (1.0 - ADAM_B1) * g
    v = ADAM_B2 * v + (1.0 - ADAM_B2) * (g * g)
    m_hat = m / (1.0 - ADAM_B1 ** ADAM_STEP)
    v_hat = v / (1.0 - ADAM_B2 ** ADAM_STEP)
    delta = -ADAM_LR * (m_hat / (jnp.sqrt(v_hat) + ADAM_EPS) + ADAM_WD * w)
    return delta, m, v


def _adamw(w, g, m, v, *, name, tr, after=()):
    rows, cols = w.shape

    def body(w_ref, g_ref, m_ref, v_ref, d_ref, mo_ref, vo_ref):
        d_ref[...], mo_ref[...], vo_ref[...] = _adamw_math(w_ref[...], g_ref[...], m_ref[...], v_ref[...])

    spec = pl.BlockSpec((tr, cols), lambda i: (i, 0))
    return pl.pallas_call(
        _behind(body, 4, after), name=name, grid=(rows // tr,), in_specs=[spec] * 4 + [ANY] * len(after),
        out_specs=[spec] * 3, out_shape=[jax.ShapeDtypeStruct(w.shape, F32)] * 3,
        compiler_params=_params(("parallel",)),
    )(w, g, m, v, *after)


def _sum_small(gathered, *, name):
    def body(g_ref, o_ref):
        acc = g_ref[0]
        for k in range(1, N_DEV):
            acc = acc + g_ref[k]
        o_ref[...] = acc

    return pl.pallas_call(body, name=name, out_shape=jax.ShapeDtypeStruct(gathered.shape[1:], F32))(gathered)


def kernel(x, ffn1_norm, ffn1_w_gate, ffn1_w_up, ffn1_w_down, mix_norm, w_in, conv_w, attn_sinks, w_out, ffn2_norm, ffn2_w_gate, ffn2_w_up, ffn2_w_down, final_norm, loss_target, m_ffn1_norm, m_ffn1_w_gate, m_ffn1_w_up, m_ffn1_w_down, m_mix_norm, m_w_in, m_conv_w, m_attn_sinks, m_w_out, m_ffn2_norm, m_ffn2_w_gate, m_ffn2_w_up, m_ffn2_w_down, m_final_norm, v_ffn1_norm, v_ffn1_w_gate, v_ffn1_w_up, v_ffn1_w_down, v_mix_norm, v_w_in, v_conv_w, v_attn_sinks, v_w_out, v_ffn2_norm, v_ffn2_w_gate, v_ffn2_w_up, v_ffn2_w_down, v_final_norm):
    ix, iy, ic = lax.axis_index("x"), lax.axis_index("y"), lax.axis_index("c")
    my_index = 4 * ix + 2 * iy + ic
    core = ic.astype(jnp.int32).reshape(1)
    chip = (2 * ix + iy).astype(jnp.int32).reshape(1)

    given = dict(ffn1_norm=ffn1_norm, ffn1_w_gate=ffn1_w_gate, ffn1_w_up=ffn1_w_up, ffn1_w_down=ffn1_w_down,
                 mix_norm=mix_norm, w_in=w_in, conv_w=conv_w, attn_sinks=attn_sinks, w_out=w_out, ffn2_norm=ffn2_norm,
                 ffn2_w_gate=ffn2_w_gate, ffn2_w_up=ffn2_w_up, ffn2_w_down=ffn2_w_down, final_norm=final_norm)
    moments_m = dict(ffn1_norm=m_ffn1_norm, ffn1_w_gate=m_ffn1_w_gate, ffn1_w_up=m_ffn1_w_up, ffn1_w_down=m_ffn1_w_down,
                     mix_norm=m_mix_norm, w_in=m_w_in, conv_w=m_conv_w, attn_sinks=m_attn_sinks, w_out=m_w_out,
                     ffn2_norm=m_ffn2_norm, ffn2_w_gate=m_ffn2_w_gate, ffn2_w_up=m_ffn2_w_up, ffn2_w_down=m_ffn2_w_down,
                     final_norm=m_final_norm)
    moments_v = dict(ffn1_norm=v_ffn1_norm, ffn1_w_gate=v_ffn1_w_gate, ffn1_w_up=v_ffn1_w_up, ffn1_w_down=v_ffn1_w_down,
                     mix_norm=v_mix_norm, w_in=v_w_in, conv_w=v_conv_w, attn_sinks=v_attn_sinks, w_out=v_w_out,
                     ffn2_norm=v_ffn2_norm, ffn2_w_gate=v_ffn2_w_gate, ffn2_w_up=v_ffn2_w_up, ffn2_w_down=v_ffn2_w_down,
                     final_norm=v_final_norm)

    xs = x[0]
    target = loss_target[0]
    final_gain = final_norm.reshape(1, D_MODEL)

    def ffn_shard(wg, wu, wd):
        return jnp.stack([wg[0].T, wu[0].T, wd[0]]).astype(BF16)

    conv_cols = conv_w.shape[2]
    conv_shard = jnp.pad(conv_w[0], ((0, 5), (0, 128 - conv_cols)))
    rest_shards = [ffn_shard(ffn2_w_gate, ffn2_w_up, ffn2_w_down), w_in[0].T.astype(BF16), w_out[0].astype(BF16),
                   conv_shard]
    rest_rows = [s.shape[-2] for s in rest_shards]
    n_rest = len(rest_shards)
    (w1,), _ = _all_gather_rows([ffn_shard(ffn1_w_gate, ffn1_w_up, ffn1_w_down)], name="gather_ffn1")

    fulls = [lax.empty(s.shape[:-2] + (N_DEV * s.shape[-2], s.shape[-1]), s.dtype) for s in rest_shards]
    fulls = _place_own(fulls, rest_shards, my_index.astype(jnp.int32).reshape(1), name="place_own_weights")
    send_plan = _gather_send_plan(n_rest)
    ssem, rsem, bufs, token = _split_start(rest_shards + list(fulls), 4 * n_rest, send_plan, name="gather_rest_start",
                                           after=[w1])
    x1, h1, a1, b1 = _ffn_fwd(xs, ffn1_norm, w1, name="ffn1_fwd", after=[token])
    bufs = _split_wait(ssem, rsem, bufs, x1, send_plan, name="gather_rest_wait")
    w2_part, mixer_parts = bufs[n_rest], bufs[n_rest + 1:]
    fwd_mixer = _gather_forward_plan(rest_rows[1:])
    ssem, rsem, bufs, token = _split_start(mixer_parts, 3 * (n_rest - 1), fwd_mixer, name="forward_mixer_start")
    win_t, wout, conv_all = _split_wait(ssem, rsem, bufs, token, fwd_mixer, name="forward_mixer_wait")
    conv_full = conv_all.reshape(N_DEV, 8, 128)[:, :3, :conv_cols].transpose(1, 0, 2).reshape(3, CONV_W)
    fwd_ffn2 = _gather_forward_plan(rest_rows[:1])
    ssem, rsem, bufs, token = _split_start([w2_part], 3, fwd_ffn2, name="forward_ffn2_start", after=[win_t])
    rope = _rope_tables(xs.shape[0])
    x2, hm, z, y = _mixer_fwd(x1, mix_norm, win_t, wout, conv_full, attn_sinks, rope, name="mixer_fwd", after=[token])
    (w2,) = _split_wait(ssem, rsem, bufs, x2, fwd_ffn2, name="forward_ffn2_wait")
    x3, h2, a2, b2 = _ffn_fwd(x2, ffn2_norm, w2, name="ffn2_fwd")

    def to_sibling_start(grads, tag, after=()):
        views = [g.reshape(4, 2, g.shape[0] // N_DEV, D_MODEL) for g in grads]
        lands = [lax.empty((4,) + v.shape[2:], F32) for v in views]
        plan = _sibling_plan(len(views))
        ssem, rsem, bufs, token = _split_start(views + lands, len(views), plan, name=f"{tag}_sibling_start", after=after)
        return (ssem, rsem, bufs, plan, tag), token

    def to_sibling_finish(handle, after, names):
        ssem, rsem, bufs, plan, tag = handle
        bufs = _split_wait(ssem, rsem, bufs, after, plan, name=f"{tag}_sibling_wait")
        n = len(names)
        return [_add_sibling(v, r, core, name=f"add_sibling_{nm}", tr=v.shape[2] // 2)
                for v, r, nm in zip(bufs[:n], bufs[n:], names)]

    def to_chips_start(partials, tag, small_all=None, after=()):
        p16 = [p for _, p in partials]
        lands = [lax.empty((3,) + p.shape[1:], BF16) for p in p16]
        extra = [] if small_all is None else [small_all]
        plan = _chips_plan(len(p16), small_all is not None)
        ssem, rsem, bufs, token = _split_start(p16 + lands + extra, 3 * len(p16) + 7 * len(extra), plan,
                                               name=f"{tag}_chips_start", after=after)
        return (ssem, rsem, bufs, plan, tag), token

    def to_chips_finish(handle, partials, after, names):
        ssem, rsem, bufs, plan, tag = handle
        bufs = _split_wait(ssem, rsem, bufs, after, plan, name=f"{tag}_chips_wait")
        n = len(names)
        return [(p32, r) for (p32, _), r in zip(partials, bufs[n:2 * n])], bufs[2 * n:]

    half_ff = D_FF // 2
    names2, namesm = ["ffn2_w_gate", "ffn2_w_up", "ffn2_w_down"], ["w_in", "w_out"]
    transposed = {"ffn1_w_gate", "ffn1_w_up", "w_in", "ffn2_w_gate", "ffn2_w_up"}
    grad, delta, new_m, new_v = {}, {}, {}, {}

    def adam_big(nm, parts, after=()):
        to_rows = (lambda a: a[0].T) if nm in transposed else (lambda a: a[0])
        from_rows = (lambda a: a.T[None]) if nm in transposed else (lambda a: a[None])
        p32, recv = parts
        outs = _reduce_adamw(p32, recv, chip, to_rows(given[nm]), to_rows(moments_m[nm]), to_rows(moments_v[nm]),
                             name=f"adamw_{nm}", tr=p32.shape[1] // 2, after=after)
        grad[nm], delta[nm], new_m[nm], new_v[nm] = (from_rows(a) for a in outs)

    dx2, da2, db2, s2, g2b, d_norm2, loss_local, d_final = _ffn_dgrad(
        x3, x2, ffn2_norm, a2, b2, w2, head=(final_gain, target), name="ffn2_dgrad")
    gw2 = [_tn_matmul(da2, h2, name="ffn2_wgrad_gate", bm=half_ff), _tn_matmul(db2, h2, name="ffn2_wgrad_up", bm=half_ff),
           _tn_matmul(s2, g2b, name="ffn2_wgrad_down", bm=half_ff)]
    sib2, tok = to_sibling_start(gw2, "ffn2")
    dx1, dz, gmb, d_conv, d_sink, d_normm = _mixer_bwd(dx2, x1, mix_norm, y, z, win_t, wout, conv_full, attn_sinks,
                                                       rope, name="mixer_bwd", after=[tok])
    p2 = to_sibling_finish(sib2, dx1, names2)
    chips2, tok = to_chips_start(p2, "ffn2")
    gwm = [_tn_matmul(dz, hm, name="mixer_wgrad_in", bm=Z_W // 3, after=[tok]),
           _tn_matmul(y, gmb, name="mixer_wgrad_out", bm=D_MODEL // 2, after=[tok])]
    sibm, tok = to_sibling_start(gwm, "mixer")
    dx0, da1, db1, s1, g1b, d_norm1 = _ffn_dgrad(dx1, xs, ffn1_norm, a1, b1, w1, name="ffn1_dgrad", after=[tok])
    r2, _ = to_chips_finish(chips2, p2, dx0, names2)
    pm = to_sibling_finish(sibm, dx0, namesm)
    chipsm, tok = to_chips_start(pm, "mixer")
    gw_gate = _tn_matmul(da1, h1, name="ffn1_wgrad_gate", bm=half_ff, after=[tok])
    sib_gate, tok = to_sibling_start([gw_gate], "ffn1_gate")
    gw_up = _tn_matmul(db1, h1, name="ffn1_wgrad_up", bm=half_ff, after=[tok])
    rm, _ = to_chips_finish(chipsm, pm, gw_up, namesm)
    p_gate = to_sibling_finish(sib_gate, gw_up, ["ffn1_w_gate"])
    chips_gate, tok_a = to_chips_start(p_gate, "ffn1_gate")
    sib_up, tok_b = to_sibling_start([gw_up], "ffn1_up", after=[tok_a])
    gw_down = _tn_matmul(s1, g1b, name="ffn1_wgrad_down", bm=half_ff, after=[tok_a, tok_b])
    p_up = to_sibling_finish(sib_up, gw_down, ["ffn1_w_up"])
    chips_up, tok_a = to_chips_start(p_up, "ffn1_up")
    sib_down, tok_b = to_sibling_start([gw_down], "ffn1_down", after=[tok_a])
    behind = [tok_a, tok_b]
    for nm, g in zip(names2 + namesm, r2 + rm):
        adam_big(nm, g, after=behind)
        behind = [new_v[nm]]
    p_down = to_sibling_finish(sib_down, new_v["w_out"], ["ffn1_w_down"])
    last_row = (jnp.pad(d_sink, ((0, 0), (0, D_MODEL - 128)))
                + jnp.pad(loss_local, ((0, 0), (LOSS_LANE, D_MODEL - LOSS_LANE - 1))))
    small = jnp.concatenate([
        d_norm1, d_normm, d_norm2, d_final, jnp.pad(d_conv[0:3], ((0, 0), (0, D_MODEL - CONV_W))), last_row], axis=0)
    (small_all,) = _place_own([lax.empty((N_DEV * 8, D_MODEL), F32)], [small], my_index.astype(jnp.int32).reshape(1),
                              name="place_own_small")
    chips_down, tok = to_chips_start(p_down, "ffn1_down", small_all)
    r_gate, _ = to_chips_finish(chips_gate, p_gate, tok, ["ffn1_w_gate"])
    adam_big("ffn1_w_gate", r_gate[0])
    r_up, _ = to_chips_finish(chips_up, p_up, new_v["ffn1_w_gate"], ["ffn1_w_up"])
    adam_big("ffn1_w_up", r_up[0])
    r_down, (small_all,) = to_chips_finish(chips_down, p_down, new_v["ffn1_w_up"], ["ffn1_w_down"])
    adam_big("ffn1_w_down", r_down[0])
    small_sum = _sum_small(small_all.reshape(N_DEV, 8, D_MODEL), name="sum_small")
    loss = small_sum[7, LOSS_LANE]
    _update_small(given, moments_m, moments_v, small_sum, my_index, grad, delta, new_m, new_v)

    order = list(given)
    return (loss, dx0[None], *[grad[n] for n in order], *[delta[n] for n in order],
            *[new_m[n] for n in order], *[new_v[n] for n in order])


def _update_small(given, moments_m, moments_v, small_sum, my_index, grad, delta, new_m, new_v):
    conv_cols = given["conv_w"].shape[2]
    small_g = {
        "ffn1_norm": small_sum[0:1], "mix_norm": small_sum[1:2], "ffn2_norm": small_sum[2:3],
        "final_norm": small_sum[3:4],
        "conv_w": lax.dynamic_slice(small_sum[4:7, :CONV_W], (0, my_index * conv_cols), (3, conv_cols)),
        "attn_sinks": small_sum[7:8, :N_Q_HEADS],
    }

    small_names = ["ffn1_norm", "mix_norm", "ffn2_norm", "final_norm", "conv_w", "attn_sinks"]

    def pack(parts):
        rows = []
        for nm in small_names:
            p = parts[nm]
            p2 = p.reshape(3, conv_cols) if nm == "conv_w" else p.reshape(1, -1)
            rows.append(jnp.pad(p2, ((0, 0), (0, D_MODEL - p2.shape[1]))))
        rows.append(jnp.zeros((8, D_MODEL), F32))
        return jnp.concatenate(rows, axis=0)

    sd, sm, sv = _adamw(pack(given), pack(small_g), pack(moments_m), pack(moments_v), name="adamw_small", tr=16)
    row = 0
    for nm in small_names:
        shape = given[nm].shape
        nrow = 3 if nm == "conv_w" else 1
        ncol = conv_cols if nm == "conv_w" else given[nm].size
        grad[nm] = small_g[nm].reshape(shape)
        delta[nm], new_m[nm], new_v[nm] = (a[row:row + nrow, :ncol].reshape(shape) for a in (sd, sm, sv))
        row += nrow
```

```python
import functools

import jax
import jax.numpy as jnp
from jax import lax
from jax.experimental import pallas as pl
from jax.experimental.pallas import tpu as pltpu

F32 = jnp.float32
BF16 = jnp.bfloat16
MESH = pl.DeviceIdType.MESH
ANY = pl.BlockSpec(memory_space=pl.ANY)
HBM_SPEC = pl.BlockSpec(memory_space=pltpu.HBM)
SEM_SPEC = pl.BlockSpec(memory_space=pltpu.SEMAPHORE)
DATAFLOW = pltpu.SideEffectType.DATAFLOW_SIDE_EFFECTING

N_DEV = 8
LOSS_LANE = 128
D_MODEL = 1024
D_FF = 2816
CONV_W = 512
ATTN_W = 512
KV_W = 128
HEAD_DIM = 64
N_Q_HEADS = 8
N_KV_HEADS = 2
Q_PER_KV = N_Q_HEADS // N_KV_HEADS
BLOCK = 128
ROT_DIM = 16
ROPE_THETA = 500000.0
Z_W = 3 * CONV_W + ATTN_W + 2 * KV_W
Q_OFF = 3 * CONV_W
K_OFF = Q_OFF + ATTN_W
V_OFF = K_OFF + KV_W
RMS_EPS = 1e-5
MASK_VALUE = -1e30
SM_SCALE = HEAD_DIM ** -0.5
FFN_RES_SCALE = 0.5

ADAM_LR = 0.001
ADAM_B1 = 0.9
ADAM_B2 = 0.999
ADAM_EPS = 1e-08
ADAM_WD = 0.01
ADAM_STEP = 10

NT_DIMS = (((1,), (1,)), ((), ()))
TN_DIMS = (((0,), (0,)), ((), ()))

VMEM_LIMIT = 56 * 1024 * 1024
FF_CHUNK = 256


def _params(sem, vmem=None):
    return pltpu.CompilerParams(dimension_semantics=sem, vmem_limit_bytes=vmem)


def _behind(body, n_in, after):
    k = len(after)
    if k == 0:
        return body
    return lambda *refs: body(*refs[:n_in], *refs[n_in + k:])


def _rms_stats(xf):
    inv = lax.rsqrt(jnp.mean(xf * xf, axis=-1, keepdims=True) + RMS_EPS)
    return xf * inv, inv


def _rms_bwd(dh, xhat, inv, gain):
    dxhat = dh * gain
    dx = inv * (dxhat - xhat * jnp.mean(dxhat * xhat, axis=-1, keepdims=True))
    dgain = jnp.sum(dh * xhat, axis=0, keepdims=True)
    return dx, dgain


def _load_resident(w_hbm, w_ref, sem):
    @pl.when(pl.program_id(0) == 0)
    def _():
        cp = pltpu.make_async_copy(w_hbm, w_ref, sem)
        cp.start()
        cp.wait()


def _ffn_fwd(x, gain, w3, *, name, after=(), tm=256, tf=FF_CHUNK):
    t = x.shape[0]
    tm = min(tm, t)

    def body(x_ref, g_ref, w_hbm, xo_ref, h_ref, a_ref, b_ref, w_ref, s_ref, sem):
        _load_resident(w_hbm, w_ref, sem)
        xf = x_ref[...]
        xhat, _ = _rms_stats(xf)
        h = (xhat * g_ref[...]).astype(BF16)
        h_ref[...] = h
        for c in range(0, D_FF, tf):
            a = lax.dot_general(h, w_ref[0, c:c + tf, :], NT_DIMS, preferred_element_type=F32)
            b = lax.dot_general(h, w_ref[1, c:c + tf, :], NT_DIMS, preferred_element_type=F32)
            a_ref[:, c:c + tf] = a.astype(BF16)
            b_ref[:, c:c + tf] = b.astype(BF16)
            s_ref[:, c:c + tf] = (a * jax.nn.sigmoid(a) * b).astype(BF16)
        xo_ref[...] = xf + FFN_RES_SCALE * jnp.dot(s_ref[...], w_ref[2], preferred_element_type=F32)

    row = pl.BlockSpec((tm, D_MODEL), lambda i: (i, 0))
    hid = pl.BlockSpec((tm, D_FF), lambda i: (i, 0))
    return pl.pallas_call(
        _behind(body, 3, after), name=name, grid=(t // tm,),
        in_specs=[row, pl.BlockSpec((1, D_MODEL), lambda i: (0, 0)), ANY] + [ANY] * len(after),
        out_specs=[row, row, hid, hid],
        out_shape=[jax.ShapeDtypeStruct((t, D_MODEL), F32), jax.ShapeDtypeStruct((t, D_MODEL), BF16),
                   jax.ShapeDtypeStruct((t, D_FF), BF16), jax.ShapeDtypeStruct((t, D_FF), BF16)],
        scratch_shapes=[pltpu.VMEM((3, D_FF, D_MODEL), BF16), pltpu.VMEM((tm, D_FF), BF16),
                        pltpu.SemaphoreType.DMA(())],
        compiler_params=_params(("arbitrary",), VMEM_LIMIT),
    )(x, gain, w3, *after)


def _ffn_dgrad(dxo, x, gain, a, b, w3, *, name, head=None, after=(), tm=256, tf=FF_CHUNK):
    t = x.shape[0]
    tm = min(tm, t)
    n_head = 0 if head is None else 2

    def body(*refs):
        dxo_ref, x_ref, g_ref, a_ref, b_ref, w_hbm = refs[:6]
        head_refs = refs[6:6 + n_head]
        dxi_ref, da_ref, db_ref, s_ref, gb_ref, dg_ref = refs[6 + n_head:12 + n_head]
        head_outs = refs[12 + n_head:12 + 2 * n_head]
        w_ref, sem = refs[12 + 2 * n_head:]
        _load_resident(w_hbm, w_ref, sem)

        @pl.when(pl.program_id(0) == 0)
        def _():
            dg_ref[...] = jnp.zeros_like(dg_ref)
            for ref in head_outs:
                ref[...] = jnp.zeros_like(ref)

        if head is None:
            go = dxo_ref[...]
        else:
            fg_ref, t_ref = head_refs
            loss_ref, dfg_ref = head_outs
            xhat_o, inv_o = _rms_stats(dxo_ref[...])
            err = xhat_o * fg_ref[...] - t_ref[...]
            loss_ref[...] += 0.5 * jnp.sum(jnp.mean(err * err, axis=-1, keepdims=True), axis=0, keepdims=True)
            go, dfg = _rms_bwd(err * (1.0 / D_MODEL), xhat_o, inv_o, fg_ref[...])
            dfg_ref[...] += dfg
        gb = (FFN_RES_SCALE * go).astype(BF16)
        gb_ref[...] = gb
        for c in range(0, D_FF, tf):
            ds = lax.dot_general(gb, w_ref[2, c:c + tf, :], NT_DIMS, preferred_element_type=F32)
            af = a_ref[:, c:c + tf].astype(F32)
            bf = b_ref[:, c:c + tf].astype(F32)
            sig = jax.nn.sigmoid(af)
            silu = af * sig
            da_ref[:, c:c + tf] = (ds * bf * (sig * (1.0 + af * (1.0 - sig)))).astype(BF16)
            db_ref[:, c:c + tf] = (ds * silu).astype(BF16)
            s_ref[:, c:c + tf] = (silu * bf).astype(BF16)
        dh = (jnp.dot(da_ref[...], w_ref[0], preferred_element_type=F32)
              + jnp.dot(db_ref[...], w_ref[1], preferred_element_type=F32))
        xhat, inv = _rms_stats(x_ref[...])
        dx, dgain = _rms_bwd(dh, xhat, inv, g_ref[...])
        dxi_ref[...] = go + dx
        dg_ref[...] += dgain

    row = pl.BlockSpec((tm, D_MODEL), lambda i: (i, 0))
    hid = pl.BlockSpec((tm, D_FF), lambda i: (i, 0))
    vec = pl.BlockSpec((1, D_MODEL), lambda i: (0, 0))
    head_in = [] if head is None else [vec, row]
    head_out = [] if head is None else [pl.BlockSpec((1, 1), lambda i: (0, 0)), vec]
    head_shape = [] if head is None else [jax.ShapeDtypeStruct((1, 1), F32), jax.ShapeDtypeStruct((1, D_MODEL), F32)]
    return pl.pallas_call(
        _behind(body, 6 + n_head, after), name=name, grid=(t // tm,),
        in_specs=[row, row, vec, hid, hid, ANY] + head_in + [ANY] * len(after),
        out_specs=[row, hid, hid, hid, row, vec] + head_out,
        out_shape=[jax.ShapeDtypeStruct((t, D_MODEL), F32), jax.ShapeDtypeStruct((t, D_FF), BF16),
                   jax.ShapeDtypeStruct((t, D_FF), BF16), jax.ShapeDtypeStruct((t, D_FF), BF16),
                   jax.ShapeDtypeStruct((t, D_MODEL), BF16), jax.ShapeDtypeStruct((1, D_MODEL), F32)] + head_shape,
        scratch_shapes=[pltpu.VMEM((3, D_FF, D_MODEL), BF16), pltpu.SemaphoreType.DMA(())],
        compiler_params=_params(("arbitrary",), VMEM_LIMIT),
    )(dxo, x, gain, a, b, w3, *(head or ()), *after)


def _tn_matmul(a, b, *, name, bm, after=(), tk=2048):
    t, m = a.shape
    n = b.shape[1]
    tk = min(tk, t)
    nk = t // tk

    def body(a_ref, b_ref, o_ref):
        @pl.when(pl.program_id(1) == 0)
        def _():
            o_ref[...] = jnp.zeros_like(o_ref)

        o_ref[...] += lax.dot_general(a_ref[...], b_ref[...], TN_DIMS, preferred_element_type=F32)

    return pl.pallas_call(
        _behind(body, 2, after), name=name, grid=(m // bm, nk),
        in_specs=[pl.BlockSpec((tk, bm), lambda i, k: (k, i)), pl.BlockSpec((tk, n), lambda i, k: (k, 0))]
        + [ANY] * len(after),
        out_specs=pl.BlockSpec((bm, n), lambda i, k: (i, 0)),
        out_shape=jax.ShapeDtypeStruct((m, n), F32),
        compiler_params=_params(("parallel", "arbitrary"), VMEM_LIMIT),
    )(a, b, *after)


def _rope_tables(t):
    half = ROT_DIM // 2
    inv_freq = ROPE_THETA ** (-jnp.arange(0, ROT_DIM, 2, dtype=F32) / ROT_DIM)
    ang = jnp.arange(t, dtype=F32)[:, None] * inv_freq[None, :]
    cos8, sin8 = jnp.cos(ang), jnp.sin(ang)
    d = jnp.arange(128) % HEAD_DIM
    cos, sin = jnp.zeros((t, 128), F32), jnp.zeros((t, 128), F32)
    for k in range(half):
        cos = jnp.where(d % half == k, cos8[:, k:k + 1], cos)
        sin = jnp.where(d % half == k, sin8[:, k:k + 1], sin)
    mult = jnp.where(d < ROT_DIM, cos, 1.0)
    from_lo = jnp.where((d >= half) & (d < ROT_DIM), sin, 0.0)
    from_hi = jnp.where(d < half, -sin, 0.0)
    return jnp.stack([mult, from_lo, from_hi])


def _tile_lanes(tab, width):
    return jnp.tile(tab, (1, width // tab.shape[1]))


def _rope(v, tab):
    w = v.shape[1]
    half_rot = ROT_DIM // 2
    return (v * _tile_lanes(tab[0], w)
            + pltpu.roll(v, half_rot, axis=1) * _tile_lanes(tab[1], w)
            + pltpu.roll(v, w - half_rot, axis=1) * _tile_lanes(tab[2], w))


def _rope_bwd(dv, tab):
    w = dv.shape[1]
    half_rot = ROT_DIM // 2
    return (dv * _tile_lanes(tab[0], w)
            + pltpu.roll(dv * _tile_lanes(tab[1], w), w - half_rot, axis=1)
            + pltpu.roll(dv * _tile_lanes(tab[2], w), half_rot, axis=1))


def _shift_rows(v, prev8_ref, n):
    r = lax.broadcasted_iota(jnp.int32, v.shape, 0)
    rolled = pltpu.roll(v, n, axis=0)
    last = prev8_ref[7:8, :]
    if n == 1:
        return jnp.where(r >= 1, rolled, last)
    return jnp.where(r >= 2, rolled, jnp.where(r == 0, prev8_ref[6:7, :], last))


def _shift_rows_up(v, next8_ref, n):
    rows = v.shape[0]
    r = lax.broadcasted_iota(jnp.int32, v.shape, 0)
    rolled = pltpu.roll(v, rows - n, axis=0)
    first = next8_ref[0:1, :]
    if n == 1:
        return jnp.where(r <= rows - 2, rolled, first)
    return jnp.where(r <= rows - 3, rolled, jnp.where(r == rows - 2, first, next8_ref[1:2, :]))


def _lane_half_mask(shape, half):
    lane = lax.broadcasted_iota(jnp.int32, shape, 1)
    return (lane >= HEAD_DIM) if half else (lane < HEAD_DIM)


def _to_kv_lanes(chunk, head, kv):
    if head % 2 != kv:
        chunk = pltpu.roll(chunk, HEAD_DIM, axis=1)
    return jnp.where(_lane_half_mask(chunk.shape, kv), chunk, 0.0)


def _from_kv_lanes(chunk, head, kv):
    chunk = jnp.where(_lane_half_mask(chunk.shape, kv), chunk, 0.0)
    if head % 2 != kv:
        chunk = pltpu.roll(chunk, HEAD_DIM, axis=1)
    return chunk


def _stack_heads(wide):
    parts = []
    for head in range(N_Q_HEADS):
        chunk = wide[:, (head // 2) * 128:(head // 2 + 1) * 128]
        parts.append(_to_kv_lanes(chunk, head, head // Q_PER_KV))
    return jnp.concatenate(parts, axis=0)


def _window_mask(has_prev):
    shape = (N_Q_HEADS * BLOCK, 2 * BLOCK)
    qi = lax.broadcasted_iota(jnp.int32, shape, 0) & (BLOCK - 1)
    kj = lax.broadcasted_iota(jnp.int32, shape, 1)
    first_key = BLOCK - has_prev * BLOCK
    in_prev = (kj < BLOCK) & (kj > qi) & (kj >= first_key)
    in_own = (kj >= BLOCK) & ((kj - BLOCK) <= qi)
    return in_prev | in_own


def _sink_column(sink_ref):
    row = lax.broadcasted_iota(jnp.int32, (N_Q_HEADS * BLOCK, 1), 0)
    col = jnp.full((N_Q_HEADS * BLOCK, 1), sink_ref[0, 0], F32)
    for head in range(1, N_Q_HEADS):
        col = jnp.where(row >= head * BLOCK, sink_ref[0, head], col)
    return col


def _softmax_with_sink(q4, k2, mask, sink):
    s = lax.dot_general(q4, k2, NT_DIMS, preferred_element_type=F32) * SM_SCALE
    s = jnp.where(mask, s, MASK_VALUE)
    m = jnp.maximum(jnp.max(s, axis=-1, keepdims=True), sink)
    p = jnp.exp(s - m)
    e_sink = jnp.exp(sink - m)
    inv_den = 1.0 / (jnp.sum(p, axis=-1, keepdims=True) + e_sink)
    return p * inv_den, e_sink * inv_den


def _conv_terms(zf, prev8_ref, w_ref):
    b_gate, c_gate, u = zf[:, 0:CONV_W], zf[:, CONV_W:2 * CONV_W], zf[:, 2 * CONV_W:3 * CONV_W]
    vc = c_gate * u
    vm1 = _shift_rows(vc, prev8_ref, 1)
    vm2 = _shift_rows(vc, prev8_ref, 2)
    conv = w_ref[0:1, :] * vm2 + w_ref[1:2, :] * vm1 + w_ref[2:3, :] * vc
    return b_gate, c_gate, u, vc, vm1, vm2, conv


def _mixer_fwd(x, gain, win_t, wout, conv_w, sinks, rope, *, name, after=(), tq=512):
    t = x.shape[0]
    tq = min(tq, t)
    nblk = tq // BLOCK

    def body(x_ref, g_ref, win_hbm, wout_hbm, cw_ref, sink_ref, rope_ref,
             xo_ref, h_ref, z_ref, y_ref, kprev_ref, vprev_ref, cprev_ref, win_ref, wout_ref, sems):
        i = pl.program_id(0)
        _load_resident(win_hbm, win_ref, sems.at[0])
        _load_resident(wout_hbm, wout_ref, sems.at[1])

        @pl.when(i == 0)
        def _():
            kprev_ref[...] = jnp.zeros_like(kprev_ref)
            vprev_ref[...] = jnp.zeros_like(vprev_ref)
            cprev_ref[...] = jnp.zeros_like(cprev_ref)

        xf = x_ref[...]
        xhat, _ = _rms_stats(xf)
        h = (xhat * g_ref[...]).astype(BF16)
        h_ref[...] = h
        zb = lax.dot_general(h, win_ref[...], NT_DIMS, preferred_element_type=F32).astype(BF16)
        z_ref[...] = zb
        zf = zb.astype(F32)

        b_gate, _, _, vc, _, _, conv = _conv_terms(zf, cprev_ref, cw_ref)
        y_conv = b_gate * conv
        cprev_ref[...] = vc[tq - 8:tq, :]

        tab = rope_ref[...]
        qr = _rope(zf[:, Q_OFF:K_OFF], tab)
        kr = _rope(zf[:, K_OFF:V_OFF], tab).astype(BF16)
        vb = zb[:, V_OFF:Z_W]

        y_attn = []
        for j in range(nblk):
            rows = slice(j * BLOCK, (j + 1) * BLOCK)
            prev = slice((j - 1) * BLOCK, j * BLOCK)
            k2 = jnp.concatenate([kprev_ref[...] if j == 0 else kr[prev], kr[rows]], axis=0)
            v2 = jnp.concatenate([vprev_ref[...] if j == 0 else vb[prev], vb[rows]], axis=0)
            mask = _window_mask(jnp.minimum(i, 1) if j == 0 else 1)
            q8 = _stack_heads(qr[rows]).astype(BF16)
            probs, _ = _softmax_with_sink(q8, k2, mask, _sink_column(sink_ref))
            o8 = jnp.dot(probs.astype(BF16), v2, preferred_element_type=F32)
            chunks = [jnp.zeros((BLOCK, 128), F32) for _ in range(ATTN_W // 128)]
            for head in range(N_Q_HEADS):
                chunks[head // 2] += _from_kv_lanes(o8[head * BLOCK:(head + 1) * BLOCK], head, head // Q_PER_KV)
            y_attn.append(jnp.concatenate(chunks, axis=1))
        kprev_ref[...] = kr[tq - BLOCK:tq]
        vprev_ref[...] = vb[tq - BLOCK:tq]
        y = jnp.concatenate([y_conv, jnp.concatenate(y_attn, axis=0)], axis=1).astype(BF16)
        y_ref[...] = y
        xo_ref[...] = xf + jnp.dot(y, wout_ref[...], preferred_element_type=F32)

    row = pl.BlockSpec((tq, D_MODEL), lambda i: (i, 0))
    full = lambda shape: pl.BlockSpec(shape, lambda i: (0,) * len(shape))
    return pl.pallas_call(
        _behind(body, 7, after), name=name, grid=(t // tq,),
        in_specs=[row, full((1, D_MODEL)), ANY, ANY, full((3, CONV_W)),
                  pl.BlockSpec(memory_space=pltpu.SMEM), pl.BlockSpec((3, tq, 128), lambda i: (0, i, 0))]
        + [ANY] * len(after),
        out_specs=[row, row, pl.BlockSpec((tq, Z_W), lambda i: (i, 0)), row],
        out_shape=[jax.ShapeDtypeStruct((t, D_MODEL), F32), jax.ShapeDtypeStruct((t, D_MODEL), BF16),
                   jax.ShapeDtypeStruct((t, Z_W), BF16), jax.ShapeDtypeStruct((t, D_MODEL), BF16)],
        scratch_shapes=[pltpu.VMEM((BLOCK, KV_W), BF16), pltpu.VMEM((BLOCK, KV_W), BF16),
                        pltpu.VMEM((8, CONV_W), F32), pltpu.VMEM((Z_W, D_MODEL), BF16),
                        pltpu.VMEM((D_MODEL, D_MODEL), BF16), pltpu.SemaphoreType.DMA((2,))],
        compiler_params=_params(("arbitrary",), VMEM_LIMIT),
    )(x, gain, win_t, wout, conv_w, sinks, rope, *after)


def _mixer_bwd(dxo, x, gain, y, z, win_t, wout, conv_w, sinks, rope, *, name, after=(), tq=256):
    t = x.shape[0]
    tq = min(tq, t)
    nt, nblk = t // tq, tq // BLOCK

    def body(dxo_ref, x_ref, g_ref, y_ref, z_ref, zp_ref, win_hbm, wout_hbm, cw_ref, sink_ref, rope_ref, ropep_ref,
             dxi_ref, dz_ref, gb_ref, dcw_ref, dsink_ref, dg_ref, dk_ref, dv_ref, dcn_ref, pvc_ref,
             win_ref, wout_ref, sems):
        i = pl.program_id(0)
        tile = nt - 1 - i
        _load_resident(win_hbm, win_ref, sems.at[0])
        _load_resident(wout_hbm, wout_ref, sems.at[1])

        @pl.when(i == 0)
        def _():
            dk_ref[...] = jnp.zeros_like(dk_ref)
            dv_ref[...] = jnp.zeros_like(dv_ref)
            dcn_ref[...] = jnp.zeros_like(dcn_ref)
            dcw_ref[...] = jnp.zeros_like(dcw_ref)
            dsink_ref[...] = jnp.zeros_like(dsink_ref)
            dg_ref[...] = jnp.zeros_like(dg_ref)

        has_prev = jnp.minimum(tile, 1)
        go = dxo_ref[...]
        gb = go.astype(BF16)
        gb_ref[...] = gb
        dy = lax.dot_general(gb, wout_ref[...], NT_DIMS, preferred_element_type=F32)
        dy_conv, dy_attn = dy[:, 0:CONV_W], dy[:, CONV_W:D_MODEL]
        zb, zpb = z_ref[...], zp_ref[...]
        zf = zb.astype(F32)
        zpf = zpb.astype(F32) * has_prev.astype(F32)

        pvc_ref[...] = (zpf[:, CONV_W:2 * CONV_W] * zpf[:, 2 * CONV_W:3 * CONV_W])[BLOCK - 8:BLOCK, :]
        b_gate, c_gate, u, vc, vm1, vm2, conv = _conv_terms(zf, pvc_ref, cw_ref)
        d_bgate = dy_conv * conv
        dc = dy_conv * b_gate
        tap = lax.broadcasted_iota(jnp.int32, (8, CONV_W), 0)
        dcw_ref[...] += jnp.where(tap == 0, jnp.sum(dc * vm2, axis=0, keepdims=True),
                                  jnp.where(tap == 1, jnp.sum(dc * vm1, axis=0, keepdims=True),
                                            jnp.where(tap == 2, jnp.sum(dc * vc, axis=0, keepdims=True), 0.0)))
        dvc = (cw_ref[2:3, :] * dc + cw_ref[1:2, :] * _shift_rows_up(dc, dcn_ref, 1)
               + cw_ref[0:1, :] * _shift_rows_up(dc, dcn_ref, 2))
        dcn_ref[...] = dc[0:8, :]
        d_cgate = dvc * u
        d_u = dvc * c_gate

        tab, tabp = rope_ref[...], ropep_ref[...]
        qr = _rope(zf[:, Q_OFF:K_OFF], tab)
        kr = _rope(zf[:, K_OFF:V_OFF], tab).astype(BF16)
        kpr = _rope(zpf[:, K_OFF:V_OFF], tabp).astype(BF16)
        vb, vpb = zb[:, V_OFF:Z_W], zpb[:, V_OFF:Z_W]
        out = y_ref[:, CONV_W:D_MODEL].astype(F32)
        do_out = dy_attn * out
        lane = lax.broadcasted_iota(jnp.int32, (1, 128), 1)
        dsink = jnp.zeros((1, 128), F32)
        dk_next, dv_next = dk_ref[...], dv_ref[...]
        dq_rows, dk_rows, dv_rows = [None] * nblk, [None] * nblk, [None] * nblk
        for j in reversed(range(nblk)):
            rows = slice(j * BLOCK, (j + 1) * BLOCK)
            prev = slice((j - 1) * BLOCK, j * BLOCK)
            k2 = jnp.concatenate([kpr if j == 0 else kr[prev], kr[rows]], axis=0)
            v2 = jnp.concatenate([vpb if j == 0 else vb[prev], vb[rows]], axis=0)
            mask = _window_mask(has_prev if j == 0 else 1)
            q8 = _stack_heads(qr[rows]).astype(BF16)
            do8 = _stack_heads(dy_attn[rows]).astype(BF16)
            delta = jnp.sum(_stack_heads(do_out[rows]), axis=-1, keepdims=True)
            probs, p_sink = _softmax_with_sink(q8, k2, mask, _sink_column(sink_ref))
            dp = lax.dot_general(do8, v2, NT_DIMS, preferred_element_type=F32)
            ds = (probs * (dp - delta) * SM_SCALE).astype(BF16)
            dq8 = jnp.dot(ds, k2, preferred_element_type=F32)
            dk2 = lax.dot_general(ds, q8, TN_DIMS, preferred_element_type=F32)
            dv2 = lax.dot_general(probs.astype(BF16), do8, TN_DIMS, preferred_element_type=F32)
            sink_terms = p_sink * delta
            dq_chunks = [jnp.zeros((BLOCK, 128), F32) for _ in range(ATTN_W // 128)]
            for head in range(N_Q_HEADS):
                grp = slice(head * BLOCK, (head + 1) * BLOCK)
                dq_chunks[head // 2] += _from_kv_lanes(dq8[grp], head, head // Q_PER_KV)
                dsink = dsink - jnp.where(lane == head, jnp.sum(sink_terms[grp], axis=0, keepdims=True), 0.0)
            dq_rows[j] = jnp.concatenate(dq_chunks, axis=1)
            dk_rows[j] = dk2[BLOCK:] + dk_next
            dv_rows[j] = dv2[BLOCK:] + dv_next
            dk_next, dv_next = dk2[:BLOCK], dv2[:BLOCK]
        dk_ref[...] = dk_next
        dv_ref[...] = dv_next
        dsink_ref[...] += dsink
        dq = _rope_bwd(jnp.concatenate(dq_rows, axis=0), tab)
        dk = _rope_bwd(jnp.concatenate(dk_rows, axis=0), tab)
        dv = jnp.concatenate(dv_rows, axis=0)

        dzb = jnp.concatenate([d_bgate, d_cgate, d_u, dq, dk, dv], axis=1).astype(BF16)
        dz_ref[...] = dzb
        dh = jnp.dot(dzb, win_ref[...], preferred_element_type=F32)
        xhat, inv = _rms_stats(x_ref[...])
        dx, dgain = _rms_bwd(dh, xhat, inv, g_ref[...])
        dxi_ref[...] = go + dx
        dg_ref[...] += dgain

    rev = lambda i: (nt - 1 - i, 0)
    block_before = lambda i: jnp.maximum((nt - 1 - i) * nblk - 1, 0)
    row = pl.BlockSpec((tq, D_MODEL), rev)
    full = lambda shape: pl.BlockSpec(shape, lambda i: (0,) * len(shape))
    return pl.pallas_call(
        _behind(body, 12, after), name=name, grid=(nt,),
        in_specs=[row, row, full((1, D_MODEL)), row,
                  pl.BlockSpec((tq, Z_W), rev), pl.BlockSpec((BLOCK, Z_W), lambda i: (block_before(i), 0)),
                  ANY, ANY, full((3, CONV_W)),
                  pl.BlockSpec(memory_space=pltpu.SMEM),
                  pl.BlockSpec((3, tq, 128), lambda i: (0, nt - 1 - i, 0)),
                  pl.BlockSpec((3, BLOCK, 128), lambda i: (0, block_before(i), 0))] + [ANY] * len(after),
        out_specs=[row, pl.BlockSpec((tq, Z_W), rev), row, full((8, CONV_W)), full((1, 128)), full((1, D_MODEL))],
        out_shape=[jax.ShapeDtypeStruct((t, D_MODEL), F32), jax.ShapeDtypeStruct((t, Z_W), BF16),
                   jax.ShapeDtypeStruct((t, D_MODEL), BF16), jax.ShapeDtypeStruct((8, CONV_W), F32),
                   jax.ShapeDtypeStruct((1, 128), F32), jax.ShapeDtypeStruct((1, D_MODEL), F32)],
        scratch_shapes=[pltpu.VMEM((BLOCK, KV_W), F32), pltpu.VMEM((BLOCK, KV_W), F32), pltpu.VMEM((8, CONV_W), F32),
                        pltpu.VMEM((8, CONV_W), F32), pltpu.VMEM((Z_W, D_MODEL), BF16),
                        pltpu.VMEM((D_MODEL, D_MODEL), BF16), pltpu.SemaphoreType.DMA((2,))],
        compiler_params=_params(("arbitrary",), VMEM_LIMIT),
    )(dxo, x, gain, y, z, z, win_t, wout, conv_w, sinks, rope, rope, *after)


def _place():
    x, y, c = lax.axis_index("x"), lax.axis_index("y"), lax.axis_index("c")
    other_chips = [(1 - x, y), (x, 1 - y), (1 - x, 1 - y)]
    return x, y, c, other_chips


def _all_gather_rows(shards, place=(), *, name):
    n, p = len(shards), len(place)

    def body(*refs):
        srcs, place_srcs = refs[:n], refs[n:n + p]
        outs, place_outs = refs[n + p:2 * n + p], refs[2 * n + p:2 * (n + p)]
        send_sems, recv_sems, local_sems = refs[2 * (n + p):]
        x, y, c, chips = _place()
        me, sibling = (x, y, c), (x, y, 1 - c)

        def rows(t, px, py, pc):
            r = srcs[t].shape[-2]
            start = pl.multiple_of((4 * px + 2 * py + pc) * r, 16 if r % 16 == 0 else 8)
            if len(srcs[t].shape) == 3:
                return outs[t].at[:, pl.ds(start, r), :]
            return outs[t].at[pl.ds(start, r), :]

        def copy(t, k, block, to, own=False):
            return pltpu.make_async_remote_copy(
                src_ref=srcs[t] if own else rows(t, *block), dst_ref=rows(t, *block),
                send_sem=send_sems.at[t, k], recv_sem=recv_sems.at[t, k], device_id=to, device_id_type=MESH)

        mine = [pltpu.make_async_copy(srcs[t], rows(t, *me), local_sems.at[t]) for t in range(n)]
        mine += [pltpu.make_async_copy(place_srcs[q],
                                       _block_rows(place_outs[q], place_srcs[q].shape[-2], 4 * x + 2 * y + c),
                                       local_sems.at[n + q]) for q in range(p)]
        for q in range(p):
            mine[n + q].start()
        first = []
        for t in range(n):
            mine[t].start()
            first.append(copy(t, 0, me, sibling, own=True))
            first += [copy(t, 1 + j, me, (*chip, c), own=True) for j, chip in enumerate(chips)]
        for cp in first:
            cp.start()
        passed = []
        for j, chip in enumerate(chips):
            for t in range(n):
                copy(t, 1 + j, (*chip, c), me).wait_recv()
                fwd = copy(t, 4 + j, (*chip, c), sibling)
                fwd.start()
                passed.append(fwd)
        for t in range(n):
            copy(t, 0, sibling, me).wait_recv()
            for j, chip in enumerate(chips):
                copy(t, 4 + j, (*chip, 1 - c), me).wait_recv()
        for cp in first + passed:
            cp.wait_send()
        for cp in mine:
            cp.wait()

    out_shape = [jax.ShapeDtypeStruct(s.shape[:-2] + (N_DEV * s.shape[-2], s.shape[-1]), s.dtype)
                 for s in list(shards) + list(place)]
    res = pl.pallas_call(
        body, name=name, in_specs=[ANY] * (n + p), out_specs=[ANY] * (n + p), out_shape=out_shape,
        scratch_shapes=[pltpu.SemaphoreType.DMA((n, 7)), pltpu.SemaphoreType.DMA((n, 7)),
                        pltpu.SemaphoreType.DMA((n + p,))],
    )(*shards, *place)
    return res[:n], res[n:]


def _split_start(bufs, n_copies, plan, *, name, after=()):
    n = len(bufs)

    def body(*refs):
        token = refs[-1]
        for cp in plan(refs[:n], refs[n], refs[n + 1]):
            cp.start()
        token[...] = jnp.zeros_like(token)

    res = pl.pallas_call(
        _behind(body, n, after), name=name, in_specs=[HBM_SPEC] * n + [ANY] * len(after),
        out_specs=(SEM_SPEC, SEM_SPEC, *[HBM_SPEC] * n, pl.BlockSpec(memory_space=pltpu.VMEM)),
        out_shape=(pltpu.SemaphoreType.DMA((n_copies,)), pltpu.SemaphoreType.DMA((n_copies,)),
                   *[pltpu.HBM(b.shape, b.dtype) for b in bufs], jax.ShapeDtypeStruct((8, 128), F32)),
        input_output_aliases={i: 2 + i for i in range(n)},
        compiler_params=pltpu.CompilerParams(has_side_effects=DATAFLOW),
    )(*[pltpu.with_memory_space_constraint(b, pltpu.HBM) for b in bufs], *after)
    return res[0], res[1], list(res[2:2 + n]), res[-1]


def _split_wait(send_sems, recv_sems, bufs, after, plan, *, name):
    n = len(bufs)

    def body(*refs):
        for cp in plan(refs[:n], refs[n], refs[n + 1]):
            cp.wait_send()
            cp.wait_recv()

    return list(pl.pallas_call(
        body, name=name, in_specs=[HBM_SPEC] * n + [SEM_SPEC, SEM_SPEC, ANY], out_specs=[HBM_SPEC] * n,
        out_shape=tuple(pltpu.HBM(b.shape, b.dtype) for b in bufs),
        input_output_aliases={i: i for i in range(n)},
        compiler_params=pltpu.CompilerParams(has_side_effects=DATAFLOW),
    )(*bufs, send_sems, recv_sems, after))


def _sibling_plan(n):
    def plan(bufs, send_sems, recv_sems):
        x, y, c, _ = _place()
        return [pltpu.make_async_remote_copy(
            src_ref=bufs[t].at[:, 1 - c], dst_ref=bufs[n + t], send_sem=send_sems.at[t], recv_sem=recv_sems.at[t],
            device_id=(x, y, 1 - c), device_id_type=MESH) for t in range(n)]
    return plan


def _block_rows(ref, r, blk):
    start = pl.multiple_of(blk * r, 16 if r % 16 == 0 else 8)
    return ref.at[(slice(None),) * (len(ref.shape) - 2) + (pl.ds(start, r), slice(None))]


def _remote(src, dst, send_sems, recv_sems, k, peer):
    return pltpu.make_async_remote_copy(src_ref=src, dst_ref=dst, send_sem=send_sems.at[k], recv_sem=recv_sems.at[k],
                                        device_id=peer, device_id_type=MESH)


def _gather_send_plan(n):
    def plan(bufs, send_sems, recv_sems):
        x, y, c, chips = _place()
        peers = [(x, y, 1 - c)] + [(px, py, c) for px, py in chips]
        copies = []
        for t in range(n):
            dst = _block_rows(bufs[n + t], bufs[t].shape[-2], 4 * x + 2 * y + c)
            copies += [_remote(bufs[t], dst, send_sems, recv_sems, 4 * t + k, peer) for k, peer in enumerate(peers)]
        return copies
    return plan


def _gather_forward_plan(rows):
    def plan(bufs, send_sems, recv_sems):
        x, y, c, chips = _place()
        copies = []
        for t, r in enumerate(rows):
            for j, (px, py) in enumerate(chips):
                blk = _block_rows(bufs[t], r, 4 * px + 2 * py + c)
                copies.append(_remote(blk, blk, send_sems, recv_sems, 3 * t + j, (x, y, 1 - c)))
        return copies
    return plan


def _chips_plan(n, with_small):
    def plan(bufs, send_sems, recv_sems):
        x, y, c, chips = _place()
        copies = []
        for t in range(n):
            for j, (px, py) in enumerate(chips):
                copies.append(_remote(bufs[t].at[2 * px + py], bufs[n + t].at[j], send_sems, recv_sems, 3 * t + j,
                                      (px, py, c)))
        if with_small:
            mine = _block_rows(bufs[2 * n], 8, 4 * x + 2 * y + c)
            flips = [(fx, fy, fc) for fx in range(2) for fy in range(2) for fc in range(2)][1:]
            for k, (fx, fy, fc) in enumerate(flips):
                peer = (x + fx - 2 * x * fx, y + fy - 2 * y * fy, c + fc - 2 * c * fc)
                copies.append(_remote(mine, mine, send_sems, recv_sems, 3 * n + k, peer))
        return copies
    return plan


def _place_own(fulls, shards, index, *, name):
    n = len(fulls)

    def body(index_ref, *refs):
        for t in range(n):
            refs[2 * n + t][...] = refs[n + t][...]

    def block_of(shard):
        lead = len(shard.shape) - 2
        return pl.BlockSpec(shard.shape, lambda i, index_ref: (0,) * lead + (index_ref[0], 0))

    def whole(shard):
        return pl.BlockSpec(shard.shape, lambda i, index_ref: (0,) * len(shard.shape))

    return list(pl.pallas_call(
        body, name=name,
        grid_spec=pltpu.PrefetchScalarGridSpec(
            num_scalar_prefetch=1, grid=(1,),
            in_specs=[ANY] * n + [whole(s) for s in shards], out_specs=[block_of(s) for s in shards]),
        out_shape=[jax.ShapeDtypeStruct(f.shape, f.dtype) for f in fulls],
        input_output_aliases={1 + t: t for t in range(n)},
        compiler_params=_params(("arbitrary",)),
    )(index, *fulls, *shards))


def _add_sibling(grad, recv, core, *, name, tr):
    rows = grad.shape[2]

    def body(core_ref, g_ref, r_ref, o_ref, ob_ref):
        p = g_ref[:, 0] + r_ref[...]
        o_ref[...] = p
        ob_ref[...] = p.astype(BF16)

    out = pl.BlockSpec((4, tr, D_MODEL), lambda i, core_ref: (0, i, 0))
    return pl.pallas_call(
        body, name=name,
        grid_spec=pltpu.PrefetchScalarGridSpec(
            num_scalar_prefetch=1, grid=(rows // tr,),
            in_specs=[pl.BlockSpec((4, 1, tr, D_MODEL), lambda i, core_ref: (0, core_ref[0], i, 0)), out],
            out_specs=[out, out]),
        out_shape=[jax.ShapeDtypeStruct(recv.shape, F32), jax.ShapeDtypeStruct(recv.shape, BF16)],
        compiler_params=_params(("arbitrary",)),
    )(core, grad, recv)


def _reduce_adamw(partial, recv, chip, w, m, v, *, name, tr, after=()):
    rows = partial.shape[1]

    def body(chip_ref, p_ref, r_ref, w_ref, m_ref, v_ref, g_ref, d_ref, mo_ref, vo_ref):
        g = p_ref[0] + r_ref[0].astype(F32) + r_ref[1].astype(F32) + r_ref[2].astype(F32)
        g_ref[...] = g
        d_ref[...], mo_ref[...], vo_ref[...] = _adamw_math(w_ref[...], g, m_ref[...], v_ref[...])

    spec = pl.BlockSpec((tr, D_MODEL), lambda i, chip_ref: (i, 0))
    return pl.pallas_call(
        _behind(body, 6, after), name=name,
        grid_spec=pltpu.PrefetchScalarGridSpec(
            num_scalar_prefetch=1, grid=(rows // tr,),
            in_specs=[pl.BlockSpec((1, tr, D_MODEL), lambda i, chip_ref: (chip_ref[0], i, 0)),
                      pl.BlockSpec((3, tr, D_MODEL), lambda i, chip_ref: (0, i, 0)), spec, spec, spec]
            + [ANY] * len(after),
            out_specs=[spec] * 4),
        out_shape=[jax.ShapeDtypeStruct((rows, D_MODEL), F32)] * 4,
        compiler_params=_params(("arbitrary",)),
    )(chip, partial, recv, w, m, v, *after)


def _adamw_math(w, g, m, v):
    m = ADAM_B1 * m + (1.0 - ADAM_B1) * g
    v = ADAM_B2 * v + (1.0 - ADAM_B2) * (g * g)
    m_hat = m / (1.0 - ADAM_B1 ** ADAM_STEP)
    v_hat = v / (1.0 - ADAM_B2 ** ADAM_STEP)
    delta = -ADAM_LR * (m_hat / (jnp.sqrt(v_hat) + ADAM_EPS) + ADAM_WD * w)
    return delta, m, v


def _adamw(w, g, m, v, *, name, tr, after=()):
    rows, cols = w.shape

    def body(w_ref, g_ref, m_ref, v_ref, d_ref, mo_ref, vo_ref):
        d_ref[...], mo_ref[...], vo_ref[...] = _adamw_math(w_ref[...], g_ref[...], m_ref[...], v_ref[...])

    spec = pl.BlockSpec((tr, cols), lambda i: (i, 0))
    return pl.pallas_call(
        _behind(body, 4, after), name=name, grid=(rows // tr,), in_specs=[spec] * 4 + [ANY] * len(after),
        out_specs=[spec] * 3, out_shape=[jax.ShapeDtypeStruct(w.shape, F32)] * 3,
        compiler_params=_params(("parallel",)),
    )(w, g, m, v, *after)


def _sum_small(gathered, *, name):
    def body(g_ref, o_ref):
        acc = g_ref[0]
        for k in range(1, N_DEV):
            acc = acc + g_ref[k]
        o_ref[...] = acc

    return pl.pallas_call(body, name=name, out_shape=jax.ShapeDtypeStruct(gathered.shape[1:], F32))(gathered)


def kernel(x, ffn1_norm, ffn1_w_gate, ffn1_w_up, ffn1_w_down, mix_norm, w_in, conv_w, attn_sinks, w_out, ffn2_norm, ffn2_w_gate, ffn2_w_up, ffn2_w_down, final_norm, loss_target, m_ffn1_norm, m_ffn1_w_gate, m_ffn1_w_up, m_ffn1_w_down, m_mix_norm, m_w_in, m_conv_w, m_attn_sinks, m_w_out, m_ffn2_norm, m_ffn2_w_gate, m_ffn2_w_up, m_ffn2_w_down, m_final_norm, v_ffn1_norm, v_ffn1_w_gate, v_ffn1_w_up, v_ffn1_w_down, v_mix_norm, v_w_in, v_conv_w, v_attn_sinks, v_w_out, v_ffn2_norm, v_ffn2_w_gate, v_ffn2_w_up, v_ffn2_w_down, v_final_norm):
    ix, iy, ic = lax.axis_index("x"), lax.axis_index("y"), lax.axis_index("c")
    my_index = 4 * ix + 2 * iy + ic
    core = ic.astype(jnp.int32).reshape(1)
    chip = (2 * ix + iy).astype(jnp.int32).reshape(1)

    given = dict(ffn1_norm=ffn1_norm, ffn1_w_gate=ffn1_w_gate, ffn1_w_up=ffn1_w_up, ffn1_w_down=ffn1_w_down,
                 mix_norm=mix_norm, w_in=w_in, conv_w=conv_w, attn_sinks=attn_sinks, w_out=w_out, ffn2_norm=ffn2_norm,
                 ffn2_w_gate=ffn2_w_gate, ffn2_w_up=ffn2_w_up, ffn2_w_down=ffn2_w_down, final_norm=final_norm)
    moments_m = dict(ffn1_norm=m_ffn1_norm, ffn1_w_gate=m_ffn1_w_gate, ffn1_w_up=m_ffn1_w_up, ffn1_w_down=m_ffn1_w_down,
                     mix_norm=m_mix_norm, w_in=m_w_in, conv_w=m_conv_w, attn_sinks=m_attn_sinks, w_out=m_w_out,
                     ffn2_norm=m_ffn2_norm, ffn2_w_gate=m_ffn2_w_gate, ffn2_w_up=m_ffn2_w_up, ffn2_w_down=m_ffn2_w_down,
                     final_norm=m_final_norm)
    moments_v = dict(ffn1_norm=v_ffn1_norm, ffn1_w_gate=v_ffn1_w_gate, ffn1_w_up=v_ffn1_w_up, ffn1_w_down=v_ffn1_w_down,
                     mix_norm=v_mix_norm, w_in=v_w_in, conv_w=v_conv_w, attn_sinks=v_attn_sinks, w_out=v_w_out,
                     ffn2_norm=v_ffn2_norm, ffn2_w_gate=v_ffn2_w_gate, ffn2_w_up=v_ffn2_w_up, ffn2_w_down=v_ffn2_w_down,
                     final_norm=v_final_norm)

    xs = x[0]
    target = loss_target[0]
    final_gain = final_norm.reshape(1, D_MODEL)

    def ffn_shard(wg, wu, wd):
        return jnp.stack([wg[0].T, wu[0].T, wd[0]]).astype(BF16)

    conv_cols = conv_w.shape[2]
    conv_shard = jnp.pad(conv_w[0], ((0, 5), (0, 128 - conv_cols)))
    rest_shards = [ffn_shard(ffn2_w_gate, ffn2_w_up, ffn2_w_down), w_in[0].T.astype(BF16), w_out[0].astype(BF16),
                   conv_shard]
    rest_rows = [s.shape[-2] for s in rest_shards]
    n_rest = len(rest_shards)
    (w1,), _ = _all_gather_rows([ffn_shard(ffn1_w_gate, ffn1_w_up, ffn1_w_down)], name="gather_ffn1")

    fulls = [lax.empty(s.shape[:-2] + (N_DEV * s.shape[-2], s.shape[-1]), s.dtype) for s in rest_shards]
    fulls = _place_own(fulls, rest_shards, my_index.astype(jnp.int32).reshape(1), name="place_own_weights")
    send_plan = _gather_send_plan(n_rest)
    ssem, rsem, bufs, token = _split_start(rest_shards + list(fulls), 4 * n_rest, send_plan, name="gather_rest_start",
                                           after=[w1])
    x1, h1, a1, b1 = _ffn_fwd(xs, ffn1_norm, w1, name="ffn1_fwd", after=[token])
    bufs = _split_wait(ssem, rsem, bufs, x1, send_plan, name="gather_rest_wait")
    w2_part, mixer_parts = bufs[n_rest], bufs[n_rest + 1:]
    fwd_mixer = _gather_forward_plan(rest_rows[1:])
    ssem, rsem, bufs, token = _split_start(mixer_parts, 3 * (n_rest - 1), fwd_mixer, name="forward_mixer_start")
    win_t, wout, conv_all = _split_wait(ssem, rsem, bufs, token, fwd_mixer, name="forward_mixer_wait")
    conv_full = conv_all.reshape(N_DEV, 8, 128)[:, :3, :conv_cols].transpose(1, 0, 2).reshape(3, CONV_W)
    fwd_ffn2 = _gather_forward_plan(rest_rows[:1])
    ssem, rsem, bufs, token = _split_start([w2_part], 3, fwd_ffn2, name="forward_ffn2_start", after=[win_t])
    rope = _rope_tables(xs.shape[0])
    x2, hm, z, y = _mixer_fwd(x1, mix_norm, win_t, wout, conv_full, attn_sinks, rope, name="mixer_fwd", after=[token])
    (w2,) = _split_wait(ssem, rsem, bufs, x2, fwd_ffn2, name="forward_ffn2_wait")
    x3, h2, a2, b2 = _ffn_fwd(x2, ffn2_norm, w2, name="ffn2_fwd")

    def to_sibling_start(grads, tag, after=()):
        views = [g.reshape(4, 2, g.shape[0] // N_DEV, D_MODEL) for g in grads]
        lands = [lax.empty((4,) + v.shape[2:], F32) for v in views]
        plan = _sibling_plan(len(views))
        ssem, rsem, bufs, token = _split_start(views + lands, len(views), plan, name=f"{tag}_sibling_start", after=after)
        return (ssem, rsem, bufs, plan, tag), token

    def to_sibling_finish(handle, after, names):
        ssem, rsem, bufs, plan, tag = handle
        bufs = _split_wait(ssem, rsem, bufs, after, plan, name=f"{tag}_sibling_wait")
        n = len(names)
        return [_add_sibling(v, r, core, name=f"add_sibling_{nm}", tr=v.shape[2] // 2)
                for v, r, nm in zip(bufs[:n], bufs[n:], names)]

    def to_chips_start(partials, tag, small_all=None, after=()):
        p16 = [p for _, p in partials]
        lands = [lax.empty((3,) + p.shape[1:], BF16) for p in p16]
        extra = [] if small_all is None else [small_all]
        plan = _chips_plan(len(p16), small_all is not None)
        ssem, rsem, bufs, token = _split_start(p16 + lands + extra, 3 * len(p16) + 7 * len(extra), plan,
                                               name=f"{tag}_chips_start", after=after)
        return (ssem, rsem, bufs, plan, tag), token

    def to_chips_finish(handle, partials, after, names):
        ssem, rsem, bufs, plan, tag = handle
        bufs = _split_wait(ssem, rsem, bufs, after, plan, name=f"{tag}_chips_wait")
        n = len(names)
        return [(p32, r) for (p32, _), r in zip(partials, bufs[n:2 * n])], bufs[2 * n:]

    half_ff = D_FF // 2
    names2, namesm = ["ffn2_w_gate", "ffn2_w_up", "ffn2_w_down"], ["w_in", "w_out"]
    transposed = {"ffn1_w_gate", "ffn1_w_up", "w_in", "ffn2_w_gate", "ffn2_w_up"}
    grad, delta, new_m, new_v = {}, {}, {}, {}

    def adam_big(nm, parts, after=()):
        to_rows = (lambda a: a[0].T) if nm in transposed else (lambda a: a[0])
        from_rows = (lambda a: a.T[None]) if nm in transposed else (lambda a: a[None])
        p32, recv = parts
        outs = _reduce_adamw(p32, recv, chip, to_rows(given[nm]), to_rows(moments_m[nm]), to_rows(moments_v[nm]),
                             name=f"adamw_{nm}", tr=p32.shape[1] // 2, after=after)
        grad[nm], delta[nm], new_m[nm], new_v[nm] = (from_rows(a) for a in outs)

    dx2, da2, db2, s2, g2b, d_norm2, loss_local, d_final = _ffn_dgrad(
        x3, x2, ffn2_norm, a2, b2, w2, head=(final_gain, target), name="ffn2_dgrad")
    gw2 = [_tn_matmul(da2, h2, name="ffn2_wgrad_gate", bm=half_ff), _tn_matmul(db2, h2, name="ffn2_wgrad_up", bm=half_ff),
           _tn_matmul(s2, g2b, name="ffn2_wgrad_down", bm=half_ff)]
    sib2, tok = to_sibling_start(gw2, "ffn2")
    dx1, dz, gmb, d_conv, d_sink, d_normm = _mixer_bwd(dx2, x1, mix_norm, y, z, win_t, wout, conv_full, attn_sinks,
                                                       rope, name="mixer_bwd", after=[tok])
    p2 = to_sibling_finish(sib2, dx1, names2)
    chips2, tok = to_chips_start(p2, "ffn2")
    gwm = [_tn_matmul(dz, hm, name="mixer_wgrad_in", bm=Z_W // 3, after=[tok]),
           _tn_matmul(y, gmb, name="mixer_wgrad_out", bm=D_MODEL // 2, after=[tok])]
    sibm, tok = to_sibling_start(gwm, "mixer")
    dx0, da1, db1, s1, g1b, d_norm1 = _ffn_dgrad(dx1, xs, ffn1_norm, a1, b1, w1, name="ffn1_dgrad", after=[tok])
    r2, _ = to_chips_finish(chips2, p2, dx0, names2)
    pm = to_sibling_finish(sibm, dx0, namesm)
    chipsm, tok = to_chips_start(pm, "mixer")
    gw_gate = _tn_matmul(da1, h1, name="ffn1_wgrad_gate", bm=half_ff, after=[tok])
    sib_gate, tok = to_sibling_start([gw_gate], "ffn1_gate")
    gw_up = _tn_matmul(db1, h1, name="ffn1_wgrad_up", bm=half_ff, after=[tok])
    rm, _ = to_chips_finish(chipsm, pm, gw_up, namesm)
    p_gate = to_sibling_finish(sib_gate, gw_up, ["ffn1_w_gate"])
    chips_gate, tok_a = to_chips_start(p_gate, "ffn1_gate")
    sib_up, tok_b = to_sibling_start([gw_up], "ffn1_up", after=[tok_a])
    gw_down = _tn_matmul(s1, g1b, name="ffn1_wgrad_down", bm=half_ff, after=[tok_a, tok_b])
    p_up = to_sibling_finish(sib_up, gw_down, ["ffn1_w_up"])
    chips_up, tok_a = to_chips_start(p_up, "ffn1_up")
    sib_down, tok_b = to_sibling_start([gw_down], "ffn1_down", after=[tok_a])
    behind = [tok_a, tok_b]
    for nm, g in zip(names2 + namesm, r2 + rm):
        adam_big(nm, g, after=behind)
        behind = [new_v[nm]]
    p_down = to_sibling_finish(sib_down, new_v["w_out"], ["ffn1_w_down"])
    last_row = (jnp.pad(d_sink, ((0, 0), (0, D_MODEL - 128)))
                + jnp.pad(loss_local, ((0, 0), (LOSS_LANE, D_MODEL - LOSS_LANE - 1))))
    small = jnp.concatenate([
        d_norm1, d_normm, d_norm2, d_final, jnp.pad(d_conv[0:3], ((0, 0), (0, D_MODEL - CONV_W))), last_row], axis=0)
    (small_all,) = _place_own([lax.empty((N_DEV * 8, D_MODEL), F32)], [small], my_index.astype(jnp.int32).reshape(1),
                              name="place_own_small")
    chips_down, tok = to_chips_start(p_down, "ffn1_down", small_all)
    r_gate, _ = to_chips_finish(chips_gate, p_gate, tok, ["ffn1_w_gate"])
    adam_big("ffn1_w_gate", r_gate[0])
    r_up, _ = to_chips_finish(chips_up, p_up, new_v["ffn1_w_gate"], ["ffn1_w_up"])
    adam_big("ffn1_w_up", r_up[0])
    r_down, (small_all,) = to_chips_finish(chips_down, p_down, new_v["ffn1_w_up"], ["ffn1_w_down"])
    adam_big("ffn1_w_down", r_down[0])
    small_sum = _sum_small(small_all.reshape(N_DEV, 8, D_MODEL), name="sum_small")
    loss = small_sum[7, LOSS_LANE]
    _update_small(given, moments_m, moments_v, small_sum, my_index, grad, delta, new_m, new_v)

    order = list(given)
    return (loss, dx0[None], *[grad[n] for n in order], *[delta[n] for n in order],
            *[new_m[n] for n in order], *[new_v[n] for n in order])


def _update_small(given, moments_m, moments_v, small_sum, my_index, grad, delta, new_m, new_v):
    conv_cols = given["conv_w"].shape[2]
    small_g = {
        "ffn1_norm": small_sum[0:1], "mix_norm": small_sum[1:2], "ffn2_norm": small_sum[2:3],
        "final_norm": small_sum[3:4],
        "conv_w": lax.dynamic_slice(small_sum[4:7, :CONV_W], (0, my_index * conv_cols), (3, conv_cols)),
        "attn_sinks": small_sum[7:8, :N_Q_HEADS],
    }

    small_names = ["ffn1_norm", "mix_norm", "ffn2_norm", "final_norm", "conv_w", "attn_sinks"]

    def pack(parts):
        rows = []
        for nm in small_names:
            p = parts[nm]
            p2 = p.reshape(3, conv_cols) if nm == "conv_w" else p.reshape(1, -1)
            rows.append(jnp.pad(p2, ((0, 0), (0, D_MODEL - p2.shape[1]))))
        rows.append(jnp.zeros((8, D_MODEL), F32))
        return jnp.concatenate(rows, axis=0)

    sd, sm, sv = _adamw(pack(given), pack(small_g), pack(moments_m), pack(moments_v), name="adamw_small", tr=16)
    row = 0
    for nm in small_names:
        shape = given[nm].shape
        nrow = 3 if nm == "conv_w" else 1
        ncol = conv_cols if nm == "conv_w" else given[nm].size
        grad[nm] = small_g[nm].reshape(shape)
        delta[nm], new_m[nm], new_v[nm] = (a[row:row + nrow, :ncol].reshape(shape) for a in (sd, sm, sv))
        row += nrow
```

```python
import functools

import jax
import jax.numpy as jnp
from jax import lax
from jax.experimental import pallas as pl
from jax.experimental.pallas import tpu as pltpu

F32 = jnp.float32
BF16 = jnp.bfloat16
MESH = pl.DeviceIdType.MESH
ANY = pl.BlockSpec(memory_space=pl.ANY)
HBM_SPEC = pl.BlockSpec(memory_space=pltpu.HBM)
SEM_SPEC = pl.BlockSpec(memory_space=pltpu.SEMAPHORE)
DATAFLOW = pltpu.SideEffectType.DATAFLOW_SIDE_EFFECTING

N_DEV = 8
LOSS_LANE = 128
D_MODEL = 1024
D_FF = 2816
CONV_W = 512
ATTN_W = 512
KV_W = 128
HEAD_DIM = 64
N_Q_HEADS = 8
N_KV_HEADS = 2
Q_PER_KV = N_Q_HEADS // N_KV_HEADS
BLOCK = 128
ROT_DIM = 16
ROPE_THETA = 500000.0
Z_W = 3 * CONV_W + ATTN_W + 2 * KV_W
Q_OFF = 3 * CONV_W
K_OFF = Q_OFF + ATTN_W
V_OFF = K_OFF + KV_W
RMS_EPS = 1e-5
MASK_VALUE = -1e30
SM_SCALE = HEAD_DIM ** -0.5
FFN_RES_SCALE = 0.5

ADAM_LR = 0.001
ADAM_B1 = 0.9
ADAM_B2 = 0.999
ADAM_EPS = 1e-08
ADAM_WD = 0.01
ADAM_STEP = 10

NT_DIMS = (((1,), (1,)), ((), ()))
TN_DIMS = (((0,), (0,)), ((), ()))

VMEM_LIMIT = 56 * 1024 * 1024
FF_CHUNK = 256


def _params(sem, vmem=None):
    return pltpu.CompilerParams(dimension_semantics=sem, vmem_limit_bytes=vmem)


def _behind(body, n_in, after):
    k = len(after)
    if k == 0:
        return body
    return lambda *refs: body(*refs[:n_in], *refs[n_in + k:])


def _rms_stats(xf):
    inv = lax.rsqrt(jnp.mean(xf * xf, axis=-1, keepdims=True) + RMS_EPS)
    return xf * inv, inv


def _rms_bwd(dh, xhat, inv, gain):
    dxhat = dh * gain
    dx = inv * (dxhat - xhat * jnp.mean(dxhat * xhat, axis=-1, keepdims=True))
    dgain = jnp.sum(dh * xhat, axis=0, keepdims=True)
    return dx, dgain


def _load_resident(w_hbm, w_ref, sem):
    @pl.when(pl.program_id(0) == 0)
    def _():
        cp = pltpu.make_async_copy(w_hbm, w_ref, sem)
        cp.start()
        cp.wait()


def _ffn_fwd(x, gain, w3, *, name, after=(), tm=512, tf=FF_CHUNK):
    t = x.shape[0]
    tm = min(tm, t)

    def body(x_ref, g_ref, w_hbm, xo_ref, h_ref, a_ref, b_ref, w_ref, s_ref, sem):
        _load_resident(w_hbm, w_ref, sem)
        xf = x_ref[...]
        xhat, _ = _rms_stats(xf)
        h = (xhat * g_ref[...]).astype(BF16)
        h_ref[...] = h
        for c in range(0, D_FF, tf):
            cols = slice(c, min(c + tf, D_FF))
            a = lax.dot_general(h, w_ref[0, cols, :], NT_DIMS, preferred_element_type=F32)
            b = lax.dot_general(h, w_ref[1, cols, :], NT_DIMS, preferred_element_type=F32)
            a_ref[:, cols] = a.astype(BF16)
            b_ref[:, cols] = b.astype(BF16)
            s_ref[:, cols] = (a * jax.nn.sigmoid(a) * b).astype(BF16)
        xo_ref[...] = xf + FFN_RES_SCALE * jnp.dot(s_ref[...], w_ref[2], preferred_element_type=F32)

    row = pl.BlockSpec((tm, D_MODEL), lambda i: (i, 0))
    hid = pl.BlockSpec((tm, D_FF), lambda i: (i, 0))
    return pl.pallas_call(
        _behind(body, 3, after), name=name, grid=(t // tm,),
        in_specs=[row, pl.BlockSpec((1, D_MODEL), lambda i: (0, 0)), ANY] + [ANY] * len(after),
        out_specs=[row, row, hid, hid],
        out_shape=[jax.ShapeDtypeStruct((t, D_MODEL), F32), jax.ShapeDtypeStruct((t, D_MODEL), BF16),
                   jax.ShapeDtypeStruct((t, D_FF), BF16), jax.ShapeDtypeStruct((t, D_FF), BF16)],
        scratch_shapes=[pltpu.VMEM((3, D_FF, D_MODEL), BF16), pltpu.VMEM((tm, D_FF), BF16),
                        pltpu.SemaphoreType.DMA(())],
        compiler_params=_params(("arbitrary",), VMEM_LIMIT),
    )(x, gain, w3, *after)


def _ffn_dgrad(dxo, x, gain, a, b, w3, *, name, head=None, after=(), tm=256, tf=FF_CHUNK):
    t = x.shape[0]
    tm = min(tm, t)
    n_head = 0 if head is None else 2

    def body(*refs):
        dxo_ref, x_ref, g_ref, a_ref, b_ref, w_hbm = refs[:6]
        head_refs = refs[6:6 + n_head]
        dxi_ref, da_ref, db_ref, s_ref, gb_ref, dg_ref = refs[6 + n_head:12 + n_head]
        head_outs = refs[12 + n_head:12 + 2 * n_head]
        w_ref, sem = refs[12 + 2 * n_head:]
        _load_resident(w_hbm, w_ref, sem)

        @pl.when(pl.program_id(0) == 0)
        def _():
            dg_ref[...] = jnp.zeros_like(dg_ref)
            for ref in head_outs:
                ref[...] = jnp.zeros_like(ref)

        if head is None:
            go = dxo_ref[...]
        else:
            fg_ref, t_ref = head_refs
            loss_ref, dfg_ref = head_outs
            xhat_o, inv_o = _rms_stats(dxo_ref[...])
            err = xhat_o * fg_ref[...] - t_ref[...]
            loss_ref[...] += 0.5 * jnp.sum(jnp.mean(err * err, axis=-1, keepdims=True), axis=0, keepdims=True)
            go, dfg = _rms_bwd(err * (1.0 / D_MODEL), xhat_o, inv_o, fg_ref[...])
            dfg_ref[...] += dfg
        gb = (FFN_RES_SCALE * go).astype(BF16)
        gb_ref[...] = gb
        for c in range(0, D_FF, tf):
            cols = slice(c, min(c + tf, D_FF))
            ds = lax.dot_general(gb, w_ref[2, cols, :], NT_DIMS, preferred_element_type=F32)
            af = a_ref[:, cols].astype(F32)
            bf = b_ref[:, cols].astype(F32)
            sig = jax.nn.sigmoid(af)
            silu = af * sig
            da_ref[:, cols] = (ds * bf * (sig * (1.0 + af * (1.0 - sig)))).astype(BF16)
            db_ref[:, cols] = (ds * silu).astype(BF16)
            s_ref[:, cols] = (silu * bf).astype(BF16)
        dh = (jnp.dot(da_ref[...], w_ref[0], preferred_element_type=F32)
              + jnp.dot(db_ref[...], w_ref[1], preferred_element_type=F32))
        xhat, inv = _rms_stats(x_ref[...])
        dx, dgain = _rms_bwd(dh, xhat, inv, g_ref[...])
        dxi_ref[...] = go + dx
        dg_ref[...] += dgain

    row = pl.BlockSpec((tm, D_MODEL), lambda i: (i, 0))
    hid = pl.BlockSpec((tm, D_FF), lambda i: (i, 0))
    vec = pl.BlockSpec((1, D_MODEL), lambda i: (0, 0))
    head_in = [] if head is None else [vec, row]
    head_out = [] if head is None else [pl.BlockSpec((1, 1), lambda i: (0, 0)), vec]
    head_shape = [] if head is None else [jax.ShapeDtypeStruct((1, 1), F32), jax.ShapeDtypeStruct((1, D_MODEL), F32)]
    return pl.pallas_call(
        _behind(body, 6 + n_head, after), name=name, grid=(t // tm,),
        in_specs=[row, row, vec, hid, hid, ANY] + head_in + [ANY] * len(after),
        out_specs=[row, hid, hid, hid, row, vec] + head_out,
        out_shape=[jax.ShapeDtypeStruct((t, D_MODEL), F32), jax.ShapeDtypeStruct((t, D_FF), BF16),
                   jax.ShapeDtypeStruct((t, D_FF), BF16), jax.ShapeDtypeStruct((t, D_FF), BF16),
                   jax.ShapeDtypeStruct((t, D_MODEL), BF16), jax.ShapeDtypeStruct((1, D_MODEL), F32)] + head_shape,
        scratch_shapes=[pltpu.VMEM((3, D_FF, D_MODEL), BF16), pltpu.SemaphoreType.DMA(())],
        compiler_params=_params(("arbitrary",), VMEM_LIMIT),
    )(dxo, x, gain, a, b, w3, *(head or ()), *after)


def _tn_matmul(a, b, *, name, bm, after=(), tk=2048):
    t, m = a.shape
    n = b.shape[1]
    tk = min(tk, t)
    nk = t // tk

    def body(a_ref, b_ref, o_ref):
        @pl.when(pl.program_id(1) == 0)
        def _():
            o_ref[...] = jnp.zeros_like(o_ref)

        o_ref[...] += lax.dot_general(a_ref[...], b_ref[...], TN_DIMS, preferred_element_type=F32)

    return pl.pallas_call(
        _behind(body, 2, after), name=name, grid=(m // bm, nk),
        in_specs=[pl.BlockSpec((tk, bm), lambda i, k: (k, i)), pl.BlockSpec((tk, n), lambda i, k: (k, 0))]
        + [ANY] * len(after),
        out_specs=pl.BlockSpec((bm, n), lambda i, k: (i, 0)),
        out_shape=jax.ShapeDtypeStruct((m, n), F32),
        compiler_params=_params(("parallel", "arbitrary"), VMEM_LIMIT),
    )(a, b, *after)


def _rope_tables(t):
    half = ROT_DIM // 2
    inv_freq = ROPE_THETA ** (-jnp.arange(0, ROT_DIM, 2, dtype=F32) / ROT_DIM)
    ang = jnp.arange(t, dtype=F32)[:, None] * inv_freq[None, :]
    cos8, sin8 = jnp.cos(ang), jnp.sin(ang)
    d = jnp.arange(128) % HEAD_DIM
    cos, sin = jnp.zeros((t, 128), F32), jnp.zeros((t, 128), F32)
    for k in range(half):
        cos = jnp.where(d % half == k, cos8[:, k:k + 1], cos)
        sin = jnp.where(d % half == k, sin8[:, k:k + 1], sin)
    mult = jnp.where(d < ROT_DIM, cos, 1.0)
    from_lo = jnp.where((d >= half) & (d < ROT_DIM), sin, 0.0)
    from_hi = jnp.where(d < half, -sin, 0.0)
    return jnp.stack([mult, from_lo, from_hi])


def _tile_lanes(tab, width):
    return jnp.tile(tab, (1, width // tab.shape[1]))


def _rope(v, tab):
    w = v.shape[1]
    half_rot = ROT_DIM // 2
    return (v * _tile_lanes(tab[0], w)
            + pltpu.roll(v, half_rot, axis=1) * _tile_lanes(tab[1], w)
            + pltpu.roll(v, w - half_rot, axis=1) * _tile_lanes(tab[2], w))


def _rope_bwd(dv, tab):
    w = dv.shape[1]
    half_rot = ROT_DIM // 2
    return (dv * _tile_lanes(tab[0], w)
            + pltpu.roll(dv * _tile_lanes(tab[1], w), w - half_rot, axis=1)
            + pltpu.roll(dv * _tile_lanes(tab[2], w), half_rot, axis=1))


def _shift_rows(v, prev8_ref, n):
    r = lax.broadcasted_iota(jnp.int32, v.shape, 0)
    rolled = pltpu.roll(v, n, axis=0)
    last = prev8_ref[7:8, :]
    if n == 1:
        return jnp.where(r >= 1, rolled, last)
    return jnp.where(r >= 2, rolled, jnp.where(r == 0, prev8_ref[6:7, :], last))


def _shift_rows_up(v, next8_ref, n):
    rows = v.shape[0]
    r = lax.broadcasted_iota(jnp.int32, v.shape, 0)
    rolled = pltpu.roll(v, rows - n, axis=0)
    first = next8_ref[0:1, :]
    if n == 1:
        return jnp.where(r <= rows - 2, rolled, first)
    return jnp.where(r <= rows - 3, rolled, jnp.where(r == rows - 2, first, next8_ref[1:2, :]))


def _lane_half_mask(shape, half):
    lane = lax.broadcasted_iota(jnp.int32, shape, 1)
    return (lane >= HEAD_DIM) if half else (lane < HEAD_DIM)


def _to_kv_lanes(chunk, head, kv):
    if head % 2 != kv:
        chunk = pltpu.roll(chunk, HEAD_DIM, axis=1)
    return jnp.where(_lane_half_mask(chunk.shape, kv), chunk, 0.0)


def _from_kv_lanes(chunk, head, kv):
    chunk = jnp.where(_lane_half_mask(chunk.shape, kv), chunk, 0.0)
    if head % 2 != kv:
        chunk = pltpu.roll(chunk, HEAD_DIM, axis=1)
    return chunk


def _stack_heads(wide):
    parts = []
    for head in range(N_Q_HEADS):
        chunk = wide[:, (head // 2) * 128:(head // 2 + 1) * 128]
        parts.append(_to_kv_lanes(chunk, head, head // Q_PER_KV))
    return jnp.concatenate(parts, axis=0)


def _window_mask(has_prev):
    shape = (N_Q_HEADS * BLOCK, 2 * BLOCK)
    qi = lax.broadcasted_iota(jnp.int32, shape, 0) & (BLOCK - 1)
    kj = lax.broadcasted_iota(jnp.int32, shape, 1)
    first_key = BLOCK - has_prev * BLOCK
    in_prev = (kj < BLOCK) & (kj > qi) & (kj >= first_key)
    in_own = (kj >= BLOCK) & ((kj - BLOCK) <= qi)
    return in_prev | in_own


def _sink_column(sink_ref):
    row = lax.broadcasted_iota(jnp.int32, (N_Q_HEADS * BLOCK, 1), 0)
    col = jnp.full((N_Q_HEADS * BLOCK, 1), sink_ref[0, 0], F32)
    for head in range(1, N_Q_HEADS):
        col = jnp.where(row >= head * BLOCK, sink_ref[0, head], col)
    return col


def _softmax_with_sink(q4, k2, mask, sink):
    s = lax.dot_general(q4, k2, NT_DIMS, preferred_element_type=F32) * SM_SCALE
    s = jnp.where(mask, s, MASK_VALUE)
    m = jnp.maximum(jnp.max(s, axis=-1, keepdims=True), sink)
    p = jnp.exp(s - m)
    e_sink = jnp.exp(sink - m)
    inv_den = 1.0 / (jnp.sum(p, axis=-1, keepdims=True) + e_sink)
    return p * inv_den, e_sink * inv_den


def _conv_terms(zf, prev8_ref, w_ref):
    b_gate, c_gate, u = zf[:, 0:CONV_W], zf[:, CONV_W:2 * CONV_W], zf[:, 2 * CONV_W:3 * CONV_W]
    vc = c_gate * u
    vm1 = _shift_rows(vc, prev8_ref, 1)
    vm2 = _shift_rows(vc, prev8_ref, 2)
    conv = w_ref[0:1, :] * vm2 + w_ref[1:2, :] * vm1 + w_ref[2:3, :] * vc
    return b_gate, c_gate, u, vc, vm1, vm2, conv


def _mixer_fwd(x, gain, win_t, wout, conv_w, sinks, rope, *, name, after=(), tq=512):
    t = x.shape[0]
    tq = min(tq, t)
    nblk = tq // BLOCK

    def body(x_ref, g_ref, win_hbm, wout_hbm, cw_ref, sink_ref, rope_ref,
             xo_ref, h_ref, z_ref, y_ref, kprev_ref, vprev_ref, cprev_ref, win_ref, wout_ref, sems):
        i = pl.program_id(0)
        _load_resident(win_hbm, win_ref, sems.at[0])
        _load_resident(wout_hbm, wout_ref, sems.at[1])

        @pl.when(i == 0)
        def _():
            kprev_ref[...] = jnp.zeros_like(kprev_ref)
            vprev_ref[...] = jnp.zeros_like(vprev_ref)
            cprev_ref[...] = jnp.zeros_like(cprev_ref)

        xf = x_ref[...]
        xhat, _ = _rms_stats(xf)
        h = (xhat * g_ref[...]).astype(BF16)
        h_ref[...] = h
        zb = lax.dot_general(h, win_ref[...], NT_DIMS, preferred_element_type=F32).astype(BF16)
        z_ref[...] = zb
        zf = zb.astype(F32)

        b_gate, _, _, vc, _, _, conv = _conv_terms(zf, cprev_ref, cw_ref)
        y_conv = b_gate * conv
        cprev_ref[...] = vc[tq - 8:tq, :]

        tab = rope_ref[...]
        qr = _rope(zf[:, Q_OFF:K_OFF], tab)
        kr = _rope(zf[:, K_OFF:V_OFF], tab).astype(BF16)
        vb = zb[:, V_OFF:Z_W]

        y_attn = []
        for j in range(nblk):
            rows = slice(j * BLOCK, (j + 1) * BLOCK)
            prev = slice((j - 1) * BLOCK, j * BLOCK)
            k2 = jnp.concatenate([kprev_ref[...] if j == 0 else kr[prev], kr[rows]], axis=0)
            v2 = jnp.concatenate([vprev_ref[...] if j == 0 else vb[prev], vb[rows]], axis=0)
            mask = _window_mask(jnp.minimum(i, 1) if j == 0 else 1)
            q8 = _stack_heads(qr[rows]).astype(BF16)
            probs, _ = _softmax_with_sink(q8, k2, mask, _sink_column(sink_ref))
            o8 = jnp.dot(probs.astype(BF16), v2, preferred_element_type=F32)
            chunks = [jnp.zeros((BLOCK, 128), F32) for _ in range(ATTN_W // 128)]
            for head in range(N_Q_HEADS):
                chunks[head // 2] += _from_kv_lanes(o8[head * BLOCK:(head + 1) * BLOCK], head, head // Q_PER_KV)
            y_attn.append(jnp.concatenate(chunks, axis=1))
        kprev_ref[...] = kr[tq - BLOCK:tq]
        vprev_ref[...] = vb[tq - BLOCK:tq]
        y = jnp.concatenate([y_conv, jnp.concatenate(y_attn, axis=0)], axis=1).astype(BF16)
        y_ref[...] = y
        xo_ref[...] = xf + jnp.dot(y, wout_ref[...], preferred_element_type=F32)

    row = pl.BlockSpec((tq, D_MODEL), lambda i: (i, 0))
    full = lambda shape: pl.BlockSpec(shape, lambda i: (0,) * len(shape))
    return pl.pallas_call(
        _behind(body, 7, after), name=name, grid=(t // tq,),
        in_specs=[row, full((1, D_MODEL)), ANY, ANY, full((3, CONV_W)),
                  pl.BlockSpec(memory_space=pltpu.SMEM), pl.BlockSpec((3, tq, 128), lambda i: (0, i, 0))]
        + [ANY] * len(after),
        out_specs=[row, row, pl.BlockSpec((tq, Z_W), lambda i: (i, 0)), row],
        out_shape=[jax.ShapeDtypeStruct((t, D_MODEL), F32), jax.ShapeDtypeStruct((t, D_MODEL), BF16),
                   jax.ShapeDtypeStruct((t, Z_W), BF16), jax.ShapeDtypeStruct((t, D_MODEL), BF16)],
        scratch_shapes=[pltpu.VMEM((BLOCK, KV_W), BF16), pltpu.VMEM((BLOCK, KV_W), BF16),
                        pltpu.VMEM((8, CONV_W), F32), pltpu.VMEM((Z_W, D_MODEL), BF16),
                        pltpu.VMEM((D_MODEL, D_MODEL), BF16), pltpu.SemaphoreType.DMA((2,))],
        compiler_params=_params(("arbitrary",), VMEM_LIMIT),
    )(x, gain, win_t, wout, conv_w, sinks, rope, *after)


def _mixer_bwd(dxo, x, gain, y, z, win_t, wout, conv_w, sinks, rope, *, name, after=(), tq=256):
    t = x.shape[0]
    tq = min(tq, t)
    nt, nblk = t // tq, tq // BLOCK

    def body(dxo_ref, x_ref, g_ref, y_ref, z_ref, zp_ref, win_hbm, wout_hbm, cw_ref, sink_ref, rope_ref, ropep_ref,
             dxi_ref, dz_ref, gb_ref, dcw_ref, dsink_ref, dg_ref, dk_ref, dv_ref, dcn_ref, pvc_ref,
             win_ref, wout_ref, sems):
        i = pl.program_id(0)
        tile = nt - 1 - i
        _load_resident(win_hbm, win_ref, sems.at[0])
        _load_resident(wout_hbm, wout_ref, sems.at[1])

        @pl.when(i == 0)
        def _():
            dk_ref[...] = jnp.zeros_like(dk_ref)
            dv_ref[...] = jnp.zeros_like(dv_ref)
            dcn_ref[...] = jnp.zeros_like(dcn_ref)
            dcw_ref[...] = jnp.zeros_like(dcw_ref)
            dsink_ref[...] = jnp.zeros_like(dsink_ref)
            dg_ref[...] = jnp.zeros_like(dg_ref)

        has_prev = jnp.minimum(tile, 1)
        go = dxo_ref[...]
        gb = go.astype(BF16)
        gb_ref[...] = gb
        dy = lax.dot_general(gb, wout_ref[...], NT_DIMS, preferred_element_type=F32)
        dy_conv, dy_attn = dy[:, 0:CONV_W], dy[:, CONV_W:D_MODEL]
        zb, zpb = z_ref[...], zp_ref[...]
        zf = zb.astype(F32)
        zpf = zpb.astype(F32) * has_prev.astype(F32)

        pvc_ref[...] = (zpf[:, CONV_W:2 * CONV_W] * zpf[:, 2 * CONV_W:3 * CONV_W])[BLOCK - 8:BLOCK, :]
        b_gate, c_gate, u, vc, vm1, vm2, conv = _conv_terms(zf, pvc_ref, cw_ref)
        d_bgate = dy_conv * conv
        dc = dy_conv * b_gate
        tap = lax.broadcasted_iota(jnp.int32, (8, CONV_W), 0)
        dcw_ref[...] += jnp.where(tap == 0, jnp.sum(dc * vm2, axis=0, keepdims=True),
                                  jnp.where(tap == 1, jnp.sum(dc * vm1, axis=0, keepdims=True),
                                            jnp.where(tap == 2, jnp.sum(dc * vc, axis=0, keepdims=True), 0.0)))
        dvc = (cw_ref[2:3, :] * dc + cw_ref[1:2, :] * _shift_rows_up(dc, dcn_ref, 1)
               + cw_ref[0:1, :] * _shift_rows_up(dc, dcn_ref, 2))
        dcn_ref[...] = dc[0:8, :]
        d_cgate = dvc * u
        d_u = dvc * c_gate

        tab, tabp = rope_ref[...], ropep_ref[...]
        qr = _rope(zf[:, Q_OFF:K_OFF], tab)
        kr = _rope(zf[:, K_OFF:V_OFF], tab).astype(BF16)
        kpr = _rope(zpf[:, K_OFF:V_OFF], tabp).astype(BF16)
        vb, vpb = zb[:, V_OFF:Z_W], zpb[:, V_OFF:Z_W]
        out = y_ref[:, CONV_W:D_MODEL].astype(F32)
        do_out = dy_attn * out
        lane = lax.broadcasted_iota(jnp.int32, (1, 128), 1)
        dsink = jnp.zeros((1, 128), F32)
        dk_next, dv_next = dk_ref[...], dv_ref[...]
        dq_rows, dk_rows, dv_rows = [None] * nblk, [None] * nblk, [None] * nblk
        for j in reversed(range(nblk)):
            rows = slice(j * BLOCK, (j + 1) * BLOCK)
            prev = slice((j - 1) * BLOCK, j * BLOCK)
            k2 = jnp.concatenate([kpr if j == 0 else kr[prev], kr[rows]], axis=0)
            v2 = jnp.concatenate([vpb if j == 0 else vb[prev], vb[rows]], axis=0)
            mask = _window_mask(has_prev if j == 0 else 1)
            q8 = _stack_heads(qr[rows]).astype(BF16)
            do8 = _stack_heads(dy_attn[rows]).astype(BF16)
            delta = jnp.sum(_stack_heads(do_out[rows]), axis=-1, keepdims=True)
            probs, p_sink = _softmax_with_sink(q8, k2, mask, _sink_column(sink_ref))
            dp = lax.dot_general(do8, v2, NT_DIMS, preferred_element_type=F32)
            ds = (probs * (dp - delta) * SM_SCALE).astype(BF16)
            dq8 = jnp.dot(ds, k2, preferred_element_type=F32)
            dk2 = lax.dot_general(ds, q8, TN_DIMS, preferred_element_type=F32)
            dv2 = lax.dot_general(probs.astype(BF16), do8, TN_DIMS, preferred_element_type=F32)
            sink_terms = p_sink * delta
            dq_chunks = [jnp.zeros((BLOCK, 128), F32) for _ in range(ATTN_W // 128)]
            for head in range(N_Q_HEADS):
                grp = slice(head * BLOCK, (head + 1) * BLOCK)
                dq_chunks[head // 2] += _from_kv_lanes(dq8[grp], head, head // Q_PER_KV)
                dsink = dsink - jnp.where(lane == head, jnp.sum(sink_terms[grp], axis=0, keepdims=True), 0.0)
            dq_rows[j] = jnp.concatenate(dq_chunks, axis=1)
            dk_rows[j] = dk2[BLOCK:] + dk_next
            dv_rows[j] = dv2[BLOCK:] + dv_next
            dk_next, dv_next = dk2[:BLOCK], dv2[:BLOCK]
        dk_ref[...] = dk_next
        dv_ref[...] = dv_next
        dsink_ref[...] += dsink
        dq = _rope_bwd(jnp.concatenate(dq_rows, axis=0), tab)
        dk = _rope_bwd(jnp.concatenate(dk_rows, axis=0), tab)
        dv = jnp.concatenate(dv_rows, axis=0)

        dzb = jnp.concatenate([d_bgate, d_cgate, d_u, dq, dk, dv], axis=1).astype(BF16)
        dz_ref[...] = dzb
        dh = jnp.dot(dzb, win_ref[...], preferred_element_type=F32)
        xhat, inv = _rms_stats(x_ref[...])
        dx, dgain = _rms_bwd(dh, xhat, inv, g_ref[...])
        dxi_ref[...] = go + dx
        dg_ref[...] += dgain

    rev = lambda i: (nt - 1 - i, 0)
    block_before = lambda i: jnp.maximum((nt - 1 - i) * nblk - 1, 0)
    row = pl.BlockSpec((tq, D_MODEL), rev)
    full = lambda shape: pl.BlockSpec(shape, lambda i: (0,) * len(shape))
    return pl.pallas_call(
        _behind(body, 12, after), name=name, grid=(nt,),
        in_specs=[row, row, full((1, D_MODEL)), row,
                  pl.BlockSpec((tq, Z_W), rev), pl.BlockSpec((BLOCK, Z_W), lambda i: (block_before(i), 0)),
                  ANY, ANY, full((3, CONV_W)),
                  pl.BlockSpec(memory_space=pltpu.SMEM),
                  pl.BlockSpec((3, tq, 128), lambda i: (0, nt - 1 - i, 0)),
                  pl.BlockSpec((3, BLOCK, 128), lambda i: (0, block_before(i), 0))] + [ANY] * len(after),
        out_specs=[row, pl.BlockSpec((tq, Z_W), rev), row, full((8, CONV_W)), full((1, 128)), full((1, D_MODEL))],
        out_shape=[jax.ShapeDtypeStruct((t, D_MODEL), F32), jax.ShapeDtypeStruct((t, Z_W), BF16),
                   jax.ShapeDtypeStruct((t, D_MODEL), BF16), jax.ShapeDtypeStruct((8, CONV_W), F32),
                   jax.ShapeDtypeStruct((1, 128), F32), jax.ShapeDtypeStruct((1, D_MODEL), F32)],
        scratch_shapes=[pltpu.VMEM((BLOCK, KV_W), F32), pltpu.VMEM((BLOCK, KV_W), F32), pltpu.VMEM((8, CONV_W), F32),
                        pltpu.VMEM((8, CONV_W), F32), pltpu.VMEM((Z_W, D_MODEL), BF16),
                        pltpu.VMEM((D_MODEL, D_MODEL), BF16), pltpu.SemaphoreType.DMA((2,))],
        compiler_params=_params(("arbitrary",), VMEM_LIMIT),
    )(dxo, x, gain, y, z, z, win_t, wout, conv_w, sinks, rope, rope, *after)


def _place():
    x, y, c = lax.axis_index("x"), lax.axis_index("y"), lax.axis_index("c")
    other_chips = [(1 - x, y), (x, 1 - y), (1 - x, 1 - y)]
    return x, y, c, other_chips


def _all_gather_rows(shards, place=(), *, name):
    n, p = len(shards), len(place)

    def body(*refs):
        srcs, place_srcs = refs[:n], refs[n:n + p]
        outs, place_outs = refs[n + p:2 * n + p], refs[2 * n + p:2 * (n + p)]
        send_sems, recv_sems, local_sems = refs[2 * (n + p):]
        x, y, c, chips = _place()
        me, sibling = (x, y, c), (x, y, 1 - c)

        def rows(t, px, py, pc):
            r = srcs[t].shape[-2]
            start = pl.multiple_of((4 * px + 2 * py + pc) * r, 16 if r % 16 == 0 else 8)
            if len(srcs[t].shape) == 3:
                return outs[t].at[:, pl.ds(start, r), :]
            return outs[t].at[pl.ds(start, r), :]

        def copy(t, k, block, to, own=False):
            return pltpu.make_async_remote_copy(
                src_ref=srcs[t] if own else rows(t, *block), dst_ref=rows(t, *block),
                send_sem=send_sems.at[t, k], recv_sem=recv_sems.at[t, k], device_id=to, device_id_type=MESH)

        mine = [pltpu.make_async_copy(srcs[t], rows(t, *me), local_sems.at[t]) for t in range(n)]
        mine += [pltpu.make_async_copy(place_srcs[q],
                                       _block_rows(place_outs[q], place_srcs[q].shape[-2], 4 * x + 2 * y + c),
                                       local_sems.at[n + q]) for q in range(p)]
        for q in range(p):
            mine[n + q].start()
        first = []
        for t in range(n):
            mine[t].start()
            first.append(copy(t, 0, me, sibling, own=True))
            first += [copy(t, 1 + j, me, (*chip, c), own=True) for j, chip in enumerate(chips)]
        for cp in first:
            cp.start()
        passed = []
        for j, chip in enumerate(chips):
            for t in range(n):
                copy(t, 1 + j, (*chip, c), me).wait_recv()
                fwd = copy(t, 4 + j, (*chip, c), sibling)
                fwd.start()
                passed.append(fwd)
        for t in range(n):
            copy(t, 0, sibling, me).wait_recv()
            for j, chip in enumerate(chips):
                copy(t, 4 + j, (*chip, 1 - c), me).wait_recv()
        for cp in first + passed:
            cp.wait_send()
        for cp in mine:
            cp.wait()

    out_shape = [jax.ShapeDtypeStruct(s.shape[:-2] + (N_DEV * s.shape[-2], s.shape[-1]), s.dtype)
                 for s in list(shards) + list(place)]
    res = pl.pallas_call(
        body, name=name, in_specs=[ANY] * (n + p), out_specs=[ANY] * (n + p), out_shape=out_shape,
        scratch_shapes=[pltpu.SemaphoreType.DMA((n, 7)), pltpu.SemaphoreType.DMA((n, 7)),
                        pltpu.SemaphoreType.DMA((n + p,))],
    )(*shards, *place)
    return res[:n], res[n:]


def _split_start(bufs, n_copies, plan, *, name, after=()):
    n = len(bufs)

    def body(*refs):
        token = refs[-1]
        for cp in plan(refs[:n], refs[n], refs[n + 1]):
            cp.start()
        token[...] = jnp.zeros_like(token)

    res = pl.pallas_call(
        _behind(body, n, after), name=name, in_specs=[HBM_SPEC] * n + [ANY] * len(after),
        out_specs=(SEM_SPEC, SEM_SPEC, *[HBM_SPEC] * n, pl.BlockSpec(memory_space=pltpu.VMEM)),
        out_shape=(pltpu.SemaphoreType.DMA((n_copies,)), pltpu.SemaphoreType.DMA((n_copies,)),
                   *[pltpu.HBM(b.shape, b.dtype) for b in bufs], jax.ShapeDtypeStruct((8, 128), F32)),
        input_output_aliases={i: 2 + i for i in range(n)},
        compiler_params=pltpu.CompilerParams(has_side_effects=DATAFLOW),
    )(*[pltpu.with_memory_space_constraint(b, pltpu.HBM) for b in bufs], *after)
    return res[0], res[1], list(res[2:2 + n]), res[-1]


def _split_wait(send_sems, recv_sems, bufs, after, plan, *, name):
    n = len(bufs)

    def body(*refs):
        for cp in plan(refs[:n], refs[n], refs[n + 1]):
            cp.wait_send()
            cp.wait_recv()

    return list(pl.pallas_call(
        body, name=name, in_specs=[HBM_SPEC] * n + [SEM_SPEC, SEM_SPEC, ANY], out_specs=[HBM_SPEC] * n,
        out_shape=tuple(pltpu.HBM(b.shape, b.dtype) for b in bufs),
        input_output_aliases={i: i for i in range(n)},
        compiler_params=pltpu.CompilerParams(has_side_effects=DATAFLOW),
    )(*bufs, send_sems, recv_sems, after))


def _sibling_plan(n):
    def plan(bufs, send_sems, recv_sems):
        x, y, c, _ = _place()
        return [pltpu.make_async_remote_copy(
            src_ref=bufs[t].at[:, 1 - c], dst_ref=bufs[n + t], send_sem=send_sems.at[t], recv_sem=recv_sems.at[t],
            device_id=(x, y, 1 - c), device_id_type=MESH) for t in range(n)]
    return plan


def _block_rows(ref, r, blk):
    start = pl.multiple_of(blk * r, 16 if r % 16 == 0 else 8)
    return ref.at[(slice(None),) * (len(ref.shape) - 2) + (pl.ds(start, r), slice(None))]


def _remote(src, dst, send_sems, recv_sems, k, peer):
    return pltpu.make_async_remote_copy(src_ref=src, dst_ref=dst, send_sem=send_sems.at[k], recv_sem=recv_sems.at[k],
                                        device_id=peer, device_id_type=MESH)


def _gather_send_plan(n):
    def plan(bufs, send_sems, recv_sems):
        x, y, c, chips = _place()
        peers = [(x, y, 1 - c)] + [(px, py, c) for px, py in chips]
        copies = []
        for t in range(n):
            dst = _block_rows(bufs[n + t], bufs[t].shape[-2], 4 * x + 2 * y + c)
            copies += [_remote(bufs[t], dst, send_sems, recv_sems, 4 * t + k, peer) for k, peer in enumerate(peers)]
        return copies
    return plan


def _gather_forward_plan(rows):
    def plan(bufs, send_sems, recv_sems):
        x, y, c, chips = _place()
        copies = []
        for t, r in enumerate(rows):
            for j, (px, py) in enumerate(chips):
                blk = _block_rows(bufs[t], r, 4 * px + 2 * py + c)
                copies.append(_remote(blk, blk, send_sems, recv_sems, 3 * t + j, (x, y, 1 - c)))
        return copies
    return plan


def _chips_plan(n, with_small):
    def plan(bufs, send_sems, recv_sems):
        x, y, c, chips = _place()
        copies = []
        for t in range(n):
            for j, (px, py) in enumerate(chips):
                copies.append(_remote(bufs[t].at[2 * px + py], bufs[n + t].at[j], send_sems, recv_sems, 3 * t + j,
                                      (px, py, c)))
        if with_small:
            mine = _block_rows(bufs[2 * n], 8, 4 * x + 2 * y + c)
            flips = [(fx, fy, fc) for fx in range(2) for fy in range(2) for fc in range(2)][1:]
            for k, (fx, fy, fc) in enumerate(flips):
                peer = (x + fx - 2 * x * fx, y + fy - 2 * y * fy, c + fc - 2 * c * fc)
                copies.append(_remote(mine, mine, send_sems, recv_sems, 3 * n + k, peer))
        return copies
    return plan


def _place_own(fulls, shards, index, *, name):
    n = len(fulls)

    def body(index_ref, *refs):
        for t in range(n):
            refs[2 * n + t][...] = refs[n + t][...]

    def block_of(shard):
        lead = len(shard.shape) - 2
        return pl.BlockSpec(shard.shape, lambda i, index_ref: (0,) * lead + (index_ref[0], 0))

    def whole(shard):
        return pl.BlockSpec(shard.shape, lambda i, index_ref: (0,) * len(shard.shape))

    return list(pl.pallas_call(
        body, name=name,
        grid_spec=pltpu.PrefetchScalarGridSpec(
            num_scalar_prefetch=1, grid=(1,),
            in_specs=[ANY] * n + [whole(s) for s in shards], out_specs=[block_of(s) for s in shards]),
        out_shape=[jax.ShapeDtypeStruct(f.shape, f.dtype) for f in fulls],
        input_output_aliases={1 + t: t for t in range(n)},
        compiler_params=_params(("arbitrary",)),
    )(index, *fulls, *shards))


def _add_sibling(grad, recv, core, *, name, tr):
    rows = grad.shape[2]

    def body(core_ref, g_ref, r_ref, o_ref, ob_ref):
        p = g_ref[:, 0] + r_ref[...]
        o_ref[...] = p
        ob_ref[...] = p.astype(BF16)

    out = pl.BlockSpec((4, tr, D_MODEL), lambda i, core_ref: (0, i, 0))
    return pl.pallas_call(
        body, name=name,
        grid_spec=pltpu.PrefetchScalarGridSpec(
            num_scalar_prefetch=1, grid=(rows // tr,),
            in_specs=[pl.BlockSpec((4, 1, tr, D_MODEL), lambda i, core_ref: (0, core_ref[0], i, 0)), out],
            out_specs=[out, out]),
        out_shape=[jax.ShapeDtypeStruct(recv.shape, F32), jax.ShapeDtypeStruct(recv.shape, BF16)],
        compiler_params=_params(("arbitrary",)),
    )(core, grad, recv)


def _reduce_adamw(partial, recv, chip, w, m, v, *, name, tr, after=()):
    rows = partial.shape[1]

    def body(chip_ref, p_ref, r_ref, w_ref, m_ref, v_ref, g_ref, d_ref, mo_ref, vo_ref):
        g = p_ref[0] + r_ref[0].astype(F32) + r_ref[1].astype(F32) + r_ref[2].astype(F32)
        g_ref[...] = g
        d_ref[...], mo_ref[...], vo_ref[...] = _adamw_math(w_ref[...], g, m_ref[...], v_ref[...])

    spec = pl.BlockSpec((tr, D_MODEL), lambda i, chip_ref: (i, 0))
    return pl.pallas_call(
        _behind(body, 6, after), name=name,
        grid_spec=pltpu.PrefetchScalarGridSpec(
            num_scalar_prefetch=1, grid=(rows // tr,),
            in_specs=[pl.BlockSpec((1, tr, D_MODEL), lambda i, chip_ref: (chip_ref[0], i, 0)),
                      pl.BlockSpec((3, tr, D_MODEL), lambda i, chip_ref: (0, i, 0)), spec, spec, spec]
            + [ANY] * len(after),
            out_specs=[spec] * 4),
        out_shape=[jax.ShapeDtypeStruct((rows, D_MODEL), F32)] * 4,
        compiler_params=_params(("arbitrary",)),
    )(chip, partial, recv, w, m, v, *after)


def _adamw_math(w, g, m, v):
    m = ADAM_B1 * m + (1.0 - ADAM_B1) * g
    v = ADAM_B2 * v + (1.0 - ADAM_B2) * (g * g)
    m_hat = m / (1.0 - ADAM_B1 ** ADAM_STEP)
    v_hat = v / (1.0 - ADAM_B2 ** ADAM_STEP)
    delta = -ADAM_LR * (m_hat / (jnp.sqrt(v_hat) + ADAM_EPS) + ADAM_WD * w)
    return delta, m, v


def _adamw(w, g, m, v, *, name, tr, after=()):
    rows, cols = w.shape

    def body(w_ref, g_ref, m_ref, v_ref, d_ref, mo_ref, vo_ref):
        d_ref[...], mo_ref[...], vo_ref[...] = _adamw_math(w_ref[...], g_ref[...], m_ref[...], v_ref[...])

    spec = pl.BlockSpec((tr, cols), lambda i: (i, 0))
    return pl.pallas_call(
        _behind(body, 4, after), name=name, grid=(rows // tr,), in_specs=[spec] * 4 + [ANY] * len(after),
        out_specs=[spec] * 3, out_shape=[jax.ShapeDtypeStruct(w.shape, F32)] * 3,
        compiler_params=_params(("parallel",)),
    )(w, g, m, v, *after)


def _sum_small(gathered, *, name):
    def body(g_ref, o_ref):
        acc = g_ref[0]
        for k in range(1, N_DEV):
            acc = acc + g_ref[k]
        o_ref[...] = acc

    return pl.pallas_call(body, name=name, out_shape=jax.ShapeDtypeStruct(gathered.shape[1:], F32))(gathered)


def kernel(x, ffn1_norm, ffn1_w_gate, ffn1_w_up, ffn1_w_down, mix_norm, w_in, conv_w, attn_sinks, w_out, ffn2_norm, ffn2_w_gate, ffn2_w_up, ffn2_w_down, final_norm, loss_target, m_ffn1_norm, m_ffn1_w_gate, m_ffn1_w_up, m_ffn1_w_down, m_mix_norm, m_w_in, m_conv_w, m_attn_sinks, m_w_out, m_ffn2_norm, m_ffn2_w_gate, m_ffn2_w_up, m_ffn2_w_down, m_final_norm, v_ffn1_norm, v_ffn1_w_gate, v_ffn1_w_up, v_ffn1_w_down, v_mix_norm, v_w_in, v_conv_w, v_attn_sinks, v_w_out, v_ffn2_norm, v_ffn2_w_gate, v_ffn2_w_up, v_ffn2_w_down, v_final_norm):
    ix, iy, ic = lax.axis_index("x"), lax.axis_index("y"), lax.axis_index("c")
    my_index = 4 * ix + 2 * iy + ic
    core = ic.astype(jnp.int32).reshape(1)
    chip = (2 * ix + iy).astype(jnp.int32).reshape(1)

    given = dict(ffn1_norm=ffn1_norm, ffn1_w_gate=ffn1_w_gate, ffn1_w_up=ffn1_w_up, ffn1_w_down=ffn1_w_down,
                 mix_norm=mix_norm, w_in=w_in, conv_w=conv_w, attn_sinks=attn_sinks, w_out=w_out, ffn2_norm=ffn2_norm,
                 ffn2_w_gate=ffn2_w_gate, ffn2_w_up=ffn2_w_up, ffn2_w_down=ffn2_w_down, final_norm=final_norm)
    moments_m = dict(ffn1_norm=m_ffn1_norm, ffn1_w_gate=m_ffn1_w_gate, ffn1_w_up=m_ffn1_w_up, ffn1_w_down=m_ffn1_w_down,
                     mix_norm=m_mix_norm, w_in=m_w_in, conv_w=m_conv_w, attn_sinks=m_attn_sinks, w_out=m_w_out,
                     ffn2_norm=m_ffn2_norm, ffn2_w_gate=m_ffn2_w_gate, ffn2_w_up=m_ffn2_w_up, ffn2_w_down=m_ffn2_w_down,
                     final_norm=m_final_norm)
    moments_v = dict(ffn1_norm=v_ffn1_norm, ffn1_w_gate=v_ffn1_w_gate, ffn1_w_up=v_ffn1_w_up, ffn1_w_down=v_ffn1_w_down,
                     mix_norm=v_mix_norm, w_in=v_w_in, conv_w=v_conv_w, attn_sinks=v_attn_sinks, w_out=v_w_out,
                     ffn2_norm=v_ffn2_norm, ffn2_w_gate=v_ffn2_w_gate, ffn2_w_up=v_ffn2_w_up, ffn2_w_down=v_ffn2_w_down,
                     final_norm=v_final_norm)

    xs = x[0]
    target = loss_target[0]
    final_gain = final_norm.reshape(1, D_MODEL)

    def ffn_shard(wg, wu, wd):
        return jnp.stack([wg[0].T, wu[0].T, wd[0]]).astype(BF16)

    conv_cols = conv_w.shape[2]
    conv_shard = jnp.pad(conv_w[0], ((0, 5), (0, 128 - conv_cols)))
    rest_shards = [ffn_shard(ffn2_w_gate, ffn2_w_up, ffn2_w_down), w_in[0].T.astype(BF16), w_out[0].astype(BF16),
                   conv_shard]
    rest_rows = [s.shape[-2] for s in rest_shards]
    n_rest = len(rest_shards)
    (w1,), _ = _all_gather_rows([ffn_shard(ffn1_w_gate, ffn1_w_up, ffn1_w_down)], name="gather_ffn1")

    fulls = [lax.empty(s.shape[:-2] + (N_DEV * s.shape[-2], s.shape[-1]), s.dtype) for s in rest_shards]
    fulls = _place_own(fulls, rest_shards, my_index.astype(jnp.int32).reshape(1), name="place_own_weights")
    send_plan = _gather_send_plan(n_rest)
    ssem, rsem, bufs, token = _split_start(rest_shards + list(fulls), 4 * n_rest, send_plan, name="gather_rest_start",
                                           after=[w1])
    x1, h1, a1, b1 = _ffn_fwd(xs, ffn1_norm, w1, name="ffn1_fwd", after=[token])
    bufs = _split_wait(ssem, rsem, bufs, x1, send_plan, name="gather_rest_wait")
    w2_part, mixer_parts = bufs[n_rest], bufs[n_rest + 1:]
    fwd_mixer = _gather_forward_plan(rest_rows[1:])
    ssem, rsem, bufs, token = _split_start(mixer_parts, 3 * (n_rest - 1), fwd_mixer, name="forward_mixer_start")
    win_t, wout, conv_all = _split_wait(ssem, rsem, bufs, token, fwd_mixer, name="forward_mixer_wait")
    conv_full = conv_all.reshape(N_DEV, 8, 128)[:, :3, :conv_cols].transpose(1, 0, 2).reshape(3, CONV_W)
    fwd_ffn2 = _gather_forward_plan(rest_rows[:1])
    ssem, rsem, bufs, token = _split_start([w2_part], 3, fwd_ffn2, name="forward_ffn2_start", after=[win_t])
    rope = _rope_tables(xs.shape[0])
    x2, hm, z, y = _mixer_fwd(x1, mix_norm, win_t, wout, conv_full, attn_sinks, rope, name="mixer_fwd", after=[token])
    (w2,) = _split_wait(ssem, rsem, bufs, x2, fwd_ffn2, name="forward_ffn2_wait")
    x3, h2, a2, b2 = _ffn_fwd(x2, ffn2_norm, w2, name="ffn2_fwd")

    def to_sibling_start(grads, tag, after=()):
        views = [g.reshape(4, 2, g.shape[0] // N_DEV, D_MODEL) for g in grads]
        lands = [lax.empty((4,) + v.shape[2:], F32) for v in views]
        plan = _sibling_plan(len(views))
        ssem, rsem, bufs, token = _split_start(views + lands, len(views), plan, name=f"{tag}_sibling_start", after=after)
        return (ssem, rsem, bufs, plan, tag), token

    def to_sibling_finish(handle, after, names):
        ssem, rsem, bufs, plan, tag = handle
        bufs = _split_wait(ssem, rsem, bufs, after, plan, name=f"{tag}_sibling_wait")
        n = len(names)
        return [_add_sibling(v, r, core, name=f"add_sibling_{nm}", tr=v.shape[2] // 2)
                for v, r, nm in zip(bufs[:n], bufs[n:], names)]

    def to_chips_start(partials, tag, small_all=None, after=()):
        p16 = [p for _, p in partials]
        lands = [lax.empty((3,) + p.shape[1:], BF16) for p in p16]
        extra = [] if small_all is None else [small_all]
        plan = _chips_plan(len(p16), small_all is not None)
        ssem, rsem, bufs, token = _split_start(p16 + lands + extra, 3 * len(p16) + 7 * len(extra), plan,
                                               name=f"{tag}_chips_start", after=after)
        return (ssem, rsem, bufs, plan, tag), token

    def to_chips_finish(handle, partials, after, names):
        ssem, rsem, bufs, plan, tag = handle
        bufs = _split_wait(ssem, rsem, bufs, after, plan, name=f"{tag}_chips_wait")
        n = len(names)
        return [(p32, r) for (p32, _), r in zip(partials, bufs[n:2 * n])], bufs[2 * n:]

    half_ff = D_FF // 2
    names2, namesm = ["ffn2_w_gate", "ffn2_w_up", "ffn2_w_down"], ["w_in", "w_out"]
    transposed = {"ffn1_w_gate", "ffn1_w_up", "w_in", "ffn2_w_gate", "ffn2_w_up"}
    grad, delta, new_m, new_v = {}, {}, {}, {}

    def adam_big(nm, parts, after=()):
        to_rows = (lambda a: a[0].T) if nm in transposed else (lambda a: a[0])
        from_rows = (lambda a: a.T[None]) if nm in transposed else (lambda a: a[None])
        p32, recv = parts
        outs = _reduce_adamw(p32, recv, chip, to_rows(given[nm]), to_rows(moments_m[nm]), to_rows(moments_v[nm]),
                             name=f"adamw_{nm}", tr=p32.shape[1] // 2, after=after)
        grad[nm], delta[nm], new_m[nm], new_v[nm] = (from_rows(a) for a in outs)

    dx2, da2, db2, s2, g2b, d_norm2, loss_local, d_final = _ffn_dgrad(
        x3, x2, ffn2_norm, a2, b2, w2, head=(final_gain, target), name="ffn2_dgrad")
    gw2 = [_tn_matmul(da2, h2, name="ffn2_wgrad_gate", bm=half_ff), _tn_matmul(db2, h2, name="ffn2_wgrad_up", bm=half_ff),
           _tn_matmul(s2, g2b, name="ffn2_wgrad_down", bm=half_ff)]
    sib2, tok = to_sibling_start(gw2, "ffn2")
    dx1, dz, gmb, d_conv, d_sink, d_normm = _mixer_bwd(dx2, x1, mix_norm, y, z, win_t, wout, conv_full, attn_sinks,
                                                       rope, name="mixer_bwd", after=[tok])
    p2 = to_sibling_finish(sib2, dx1, names2)
    chips2, tok = to_chips_start(p2, "ffn2")
    gwm = [_tn_matmul(dz, hm, name="mixer_wgrad_in", bm=Z_W // 3, after=[tok]),
           _tn_matmul(y, gmb, name="mixer_wgrad_out", bm=D_MODEL // 2, after=[tok])]
    sibm, tok = to_sibling_start(gwm, "mixer")
    dx0, da1, db1, s1, g1b, d_norm1 = _ffn_dgrad(dx1, xs, ffn1_norm, a1, b1, w1, name="ffn1_dgrad", after=[tok])
    r2, _ = to_chips_finish(chips2, p2, dx0, names2)
    pm = to_sibling_finish(sibm, dx0, namesm)
    chipsm, tok = to_chips_start(pm, "mixer")
    gw_gate = _tn_matmul(da1, h1, name="ffn1_wgrad_gate", bm=half_ff, after=[tok])
    sib_gate, tok = to_sibling_start([gw_gate], "ffn1_gate")
    gw_up = _tn_matmul(db1, h1, name="ffn1_wgrad_up", bm=half_ff, after=[tok])
    rm, _ = to_chips_finish(chipsm, pm, gw_up, namesm)
    p_gate = to_sibling_finish(sib_gate, gw_up, ["ffn1_w_gate"])
    chips_gate, tok_a = to_chips_start(p_gate, "ffn1_gate")
    sib_up, tok_b = to_sibling_start([gw_up], "ffn1_up", after=[tok_a])
    gw_down = _tn_matmul(s1, g1b, name="ffn1_wgrad_down", bm=half_ff, after=[tok_a, tok_b])
    p_up = to_sibling_finish(sib_up, gw_down, ["ffn1_w_up"])
    chips_up, tok_a = to_chips_start(p_up, "ffn1_up")
    sib_down, tok_b = to_sibling_start([gw_down], "ffn1_down", after=[tok_a])
    p_down = to_sibling_finish(sib_down, tok_b, ["ffn1_w_down"])
    last_row = (jnp.pad(d_sink, ((0, 0), (0, D_MODEL - 128)))
                + jnp.pad(loss_local, ((0, 0), (LOSS_LANE, D_MODEL - LOSS_LANE - 1))))
    small = jnp.concatenate([
        d_norm1, d_normm, d_norm2, d_final, jnp.pad(d_conv[0:3], ((0, 0), (0, D_MODEL - CONV_W))), last_row], axis=0)
    (small_all,) = _place_own([lax.empty((N_DEV * 8, D_MODEL), F32)], [small], my_index.astype(jnp.int32).reshape(1),
                              name="place_own_small")
    chips_down, tok = to_chips_start(p_down, "ffn1_down", small_all)
    behind = [tok]
    for nm, g in zip(names2 + namesm, r2 + rm):
        adam_big(nm, g, after=behind)
        behind = [new_v[nm]]
    r_gate, _ = to_chips_finish(chips_gate, p_gate, new_v["w_out"], ["ffn1_w_gate"])
    adam_big("ffn1_w_gate", r_gate[0])
    r_up, _ = to_chips_finish(chips_up, p_up, new_v["ffn1_w_gate"], ["ffn1_w_up"])
    adam_big("ffn1_w_up", r_up[0])
    r_down, (small_all,) = to_chips_finish(chips_down, p_down, new_v["ffn1_w_up"], ["ffn1_w_down"])
    adam_big("ffn1_w_down", r_down[0])
    small_sum = _sum_small(small_all.reshape(N_DEV, 8, D_MODEL), name="sum_small")
    loss = small_sum[7, LOSS_LANE]
    _update_small(given, moments_m, moments_v, small_sum, my_index, grad, delta, new_m, new_v)

    order = list(given)
    return (loss, dx0[None], *[grad[n] for n in order], *[delta[n] for n in order],
            *[new_m[n] for n in order], *[new_v[n] for n in order])


def _update_small(given, moments_m, moments_v, small_sum, my_index, grad, delta, new_m, new_v):
    conv_cols = given["conv_w"].shape[2]
    small_g = {
        "ffn1_norm": small_sum[0:1], "mix_norm": small_sum[1:2], "ffn2_norm": small_sum[2:3],
        "final_norm": small_sum[3:4],
        "conv_w": lax.dynamic_slice(small_sum[4:7, :CONV_W], (0, my_index * conv_cols), (3, conv_cols)),
        "attn_sinks": small_sum[7:8, :N_Q_HEADS],
    }

    small_names = ["ffn1_norm", "mix_norm", "ffn2_norm", "final_norm", "conv_w", "attn_sinks"]

    def pack(parts):
        rows = []
        for nm in small_names:
            p = parts[nm]
            p2 = p.reshape(3, conv_cols) if nm == "conv_w" else p.reshape(1, -1)
            rows.append(jnp.pad(p2, ((0, 0), (0, D_MODEL - p2.shape[1]))))
        rows.append(jnp.zeros((8, D_MODEL), F32))
        return jnp.concatenate(rows, axis=0)

    sd, sm, sv = _adamw(pack(given), pack(small_g), pack(moments_m), pack(moments_v), name="adamw_small", tr=16)
    row = 0
    for nm in small_names:
        shape = given[nm].shape
        nrow = 3 if nm == "conv_w" else 1
        ncol = conv_cols if nm == "conv_w" else given[nm].size
        grad[nm] = small_g[nm].reshape(shape)
        delta[nm], new_m[nm], new_v[nm] = (a[row:row + nrow, :ncol].reshape(shape) for a in (sd, sm, sv))
        row += nrow
```

```python
import functools

import jax
import jax.numpy as jnp
from jax import lax
from jax.experimental import pallas as pl
from jax.experimental.pallas import tpu as pltpu

F32 = jnp.float32
BF16 = jnp.bfloat16
MESH = pl.DeviceIdType.MESH
ANY = pl.BlockSpec(memory_space=pl.ANY)
HBM_SPEC = pl.BlockSpec(memory_space=pltpu.HBM)
SEM_SPEC = pl.BlockSpec(memory_space=pltpu.SEMAPHORE)
DATAFLOW = pltpu.SideEffectType.DATAFLOW_SIDE_EFFECTING

N_DEV = 8
LOSS_LANE = 128
D_MODEL = 1024
D_FF = 2816
CONV_W = 512
ATTN_W = 512
KV_W = 128
HEAD_DIM = 64
N_Q_HEADS = 8
N_KV_HEADS = 2
Q_PER_KV = N_Q_HEADS // N_KV_HEADS
BLOCK = 128
ROT_DIM = 16
ROPE_THETA = 500000.0
Z_W = 3 * CONV_W + ATTN_W + 2 * KV_W
Q_OFF = 3 * CONV_W
K_OFF = Q_OFF + ATTN_W
V_OFF = K_OFF + KV_W
RMS_EPS = 1e-5
MASK_VALUE = -1e30
SM_SCALE = HEAD_DIM ** -0.5
FFN_RES_SCALE = 0.5

ADAM_LR = 0.001
ADAM_B1 = 0.9
ADAM_B2 = 0.999
ADAM_EPS = 1e-08
ADAM_WD = 0.01
ADAM_STEP = 10

NT_DIMS = (((1,), (1,)), ((), ()))
TN_DIMS = (((0,), (0,)), ((), ()))

VMEM_LIMIT = 62 * 1024 * 1024
FF_CHUNK = 256


def _params(sem, vmem=None):
    return pltpu.CompilerParams(dimension_semantics=sem, vmem_limit_bytes=vmem)


def _behind(body, n_in, after):
    k = len(after)
    if k == 0:
        return body
    return lambda *refs: body(*refs[:n_in], *refs[n_in + k:])


def _rms_stats(xf):
    inv = lax.rsqrt(jnp.mean(xf * xf, axis=-1, keepdims=True) + RMS_EPS)
    return xf * inv, inv


def _rms_bwd(dh, xhat, inv, gain):
    dxhat = dh * gain
    dx = inv * (dxhat - xhat * jnp.mean(dxhat * xhat, axis=-1, keepdims=True))
    dgain = jnp.sum(dh * xhat, axis=0, keepdims=True)
    return dx, dgain


def _load_resident(w_hbm, w_ref, sem):
    @pl.when(pl.program_id(0) == 0)
    def _():
        cp = pltpu.make_async_copy(w_hbm, w_ref, sem)
        cp.start()
        cp.wait()


def _ffn_fwd(x, gain, w3, *, name, after=(), tm=512, tf=FF_CHUNK):
    t = x.shape[0]
    tm = min(tm, t)

    def body(x_ref, g_ref, w_hbm, xo_ref, h_ref, s_ref, sa_ref, sb_ref, w_ref, sem):
        _load_resident(w_hbm, w_ref, sem)
        xf = x_ref[...]
        xhat, _ = _rms_stats(xf)
        h = (xhat * g_ref[...]).astype(BF16)
        h_ref[...] = h
        for c in range(0, D_FF, tf):
            cols = slice(c, min(c + tf, D_FF))
            a = lax.dot_general(h, w_ref[0, cols, :], NT_DIMS, preferred_element_type=F32)
            b = lax.dot_general(h, w_ref[1, cols, :], NT_DIMS, preferred_element_type=F32)
            sig = jax.nn.sigmoid(a)
            silu = a * sig
            s_ref[:, cols] = (silu * b).astype(BF16)
            sa_ref[:, cols] = (b * (sig * (1.0 + a * (1.0 - sig)))).astype(BF16)
            sb_ref[:, cols] = silu.astype(BF16)
        xo_ref[...] = xf + FFN_RES_SCALE * jnp.dot(s_ref[...], w_ref[2], preferred_element_type=F32)

    row = pl.BlockSpec((tm, D_MODEL), lambda i: (i, 0))
    hid = pl.BlockSpec((tm, D_FF), lambda i: (i, 0))
    return pl.pallas_call(
        _behind(body, 3, after), name=name, grid=(t // tm,),
        in_specs=[row, pl.BlockSpec((1, D_MODEL), lambda i: (0, 0)), ANY] + [ANY] * len(after),
        out_specs=[row, row, hid, hid, hid],
        out_shape=[jax.ShapeDtypeStruct((t, D_MODEL), F32), jax.ShapeDtypeStruct((t, D_MODEL), BF16)]
        + [jax.ShapeDtypeStruct((t, D_FF), BF16)] * 3,
        scratch_shapes=[pltpu.VMEM((3, D_FF, D_MODEL), BF16), pltpu.SemaphoreType.DMA(())],
        compiler_params=_params(("arbitrary",), VMEM_LIMIT),
    )(x, gain, w3, *after)


def _ffn_dgrad(dxo, x, gain, sa, sb, w3, *, name, head=None, after=(), tm=256, tf=FF_CHUNK):
    t = x.shape[0]
    tm = min(tm, t)
    n_head = 0 if head is None else 2

    def body(*refs):
        dxo_ref, x_ref, g_ref, sa_ref, sb_ref, w_hbm = refs[:6]
        head_refs = refs[6:6 + n_head]
        dxi_ref, da_ref, db_ref, gb_ref, dg_ref = refs[6 + n_head:11 + n_head]
        head_outs = refs[11 + n_head:11 + 2 * n_head]
        w_ref, sem = refs[11 + 2 * n_head:]
        _load_resident(w_hbm, w_ref, sem)

        @pl.when(pl.program_id(0) == 0)
        def _():
            dg_ref[...] = jnp.zeros_like(dg_ref)
            for ref in head_outs:
                ref[...] = jnp.zeros_like(ref)

        if head is None:
            go = dxo_ref[...]
        else:
            fg_ref, t_ref = head_refs
            loss_ref, dfg_ref = head_outs
            xhat_o, inv_o = _rms_stats(dxo_ref[...])
            err = xhat_o * fg_ref[...] - t_ref[...]
            loss_ref[...] += 0.5 * jnp.sum(jnp.mean(err * err, axis=-1, keepdims=True), axis=0, keepdims=True)
            go, dfg = _rms_bwd(err * (1.0 / D_MODEL), xhat_o, inv_o, fg_ref[...])
            dfg_ref[...] += dfg
        gb = (FFN_RES_SCALE * go).astype(BF16)
        gb_ref[...] = gb
        for c in range(0, D_FF, tf):
            cols = slice(c, min(c + tf, D_FF))
            ds = lax.dot_general(gb, w_ref[2, cols, :], NT_DIMS, preferred_element_type=F32)
            da_ref[:, cols] = (ds * sa_ref[:, cols].astype(F32)).astype(BF16)
            db_ref[:, cols] = (ds * sb_ref[:, cols].astype(F32)).astype(BF16)
        dh = (jnp.dot(da_ref[...], w_ref[0], preferred_element_type=F32)
              + jnp.dot(db_ref[...], w_ref[1], preferred_element_type=F32))
        xhat, inv = _rms_stats(x_ref[...])
        dx, dgain = _rms_bwd(dh, xhat, inv, g_ref[...])
        dxi_ref[...] = go + dx
        dg_ref[...] += dgain

    row = pl.BlockSpec((tm, D_MODEL), lambda i: (i, 0))
    hid = pl.BlockSpec((tm, D_FF), lambda i: (i, 0))
    vec = pl.BlockSpec((1, D_MODEL), lambda i: (0, 0))
    head_in = [] if head is None else [vec, row]
    head_out = [] if head is None else [pl.BlockSpec((1, 1), lambda i: (0, 0)), vec]
    head_shape = [] if head is None else [jax.ShapeDtypeStruct((1, 1), F32), jax.ShapeDtypeStruct((1, D_MODEL), F32)]
    return pl.pallas_call(
        _behind(body, 6 + n_head, after), name=name, grid=(t // tm,),
        in_specs=[row, row, vec, hid, hid, ANY] + head_in + [ANY] * len(after),
        out_specs=[row, hid, hid, row, vec] + head_out,
        out_shape=[jax.ShapeDtypeStruct((t, D_MODEL), F32), jax.ShapeDtypeStruct((t, D_FF), BF16),
                   jax.ShapeDtypeStruct((t, D_FF), BF16),
                   jax.ShapeDtypeStruct((t, D_MODEL), BF16), jax.ShapeDtypeStruct((1, D_MODEL), F32)] + head_shape,
        scratch_shapes=[pltpu.VMEM((3, D_FF, D_MODEL), BF16), pltpu.SemaphoreType.DMA(())],
        compiler_params=_params(("arbitrary",), VMEM_LIMIT),
    )(dxo, x, gain, sa, sb, w3, *(head or ()), *after)


def _tn_matmul(a, b, *, name, bm, after=(), tk=2048):
    t, m = a.shape
    n = b.shape[1]
    tk = min(tk, t)
    nk = t // tk

    def body(a_ref, b_ref, o_ref):
        @pl.when(pl.program_id(1) == 0)
        def _():
            o_ref[...] = jnp.zeros_like(o_ref)

        o_ref[...] += lax.dot_general(a_ref[...], b_ref[...], TN_DIMS, preferred_element_type=F32)

    return pl.pallas_call(
        _behind(body, 2, after), name=name, grid=(m // bm, nk),
        in_specs=[pl.BlockSpec((tk, bm), lambda i, k: (k, i)), pl.BlockSpec((tk, n), lambda i, k: (k, 0))]
        + [ANY] * len(after),
        out_specs=pl.BlockSpec((bm, n), lambda i, k: (i, 0)),
        out_shape=jax.ShapeDtypeStruct((m, n), F32),
        compiler_params=_params(("parallel", "arbitrary"), VMEM_LIMIT),
    )(a, b, *after)


def _rope_tables(t):
    half = ROT_DIM // 2
    inv_freq = ROPE_THETA ** (-jnp.arange(0, ROT_DIM, 2, dtype=F32) / ROT_DIM)
    ang = jnp.arange(t, dtype=F32)[:, None] * inv_freq[None, :]
    cos8, sin8 = jnp.cos(ang), jnp.sin(ang)
    d = jnp.arange(128) % HEAD_DIM
    cos, sin = jnp.zeros((t, 128), F32), jnp.zeros((t, 128), F32)
    for k in range(half):
        cos = jnp.where(d % half == k, cos8[:, k:k + 1], cos)
        sin = jnp.where(d % half == k, sin8[:, k:k + 1], sin)
    mult = jnp.where(d < ROT_DIM, cos, 1.0)
    from_lo = jnp.where((d >= half) & (d < ROT_DIM), sin, 0.0)
    from_hi = jnp.where(d < half, -sin, 0.0)
    return jnp.stack([mult, from_lo, from_hi])


def _tile_lanes(tab, width):
    return jnp.tile(tab, (1, width // tab.shape[1]))


def _rope(v, tab):
    w = v.shape[1]
    half_rot = ROT_DIM // 2
    return (v * _tile_lanes(tab[0], w)
            + pltpu.roll(v, half_rot, axis=1) * _tile_lanes(tab[1], w)
            + pltpu.roll(v, w - half_rot, axis=1) * _tile_lanes(tab[2], w))


def _rope_bwd(dv, tab):
    w = dv.shape[1]
    half_rot = ROT_DIM // 2
    return (dv * _tile_lanes(tab[0], w)
            + pltpu.roll(dv * _tile_lanes(tab[1], w), w - half_rot, axis=1)
            + pltpu.roll(dv * _tile_lanes(tab[2], w), half_rot, axis=1))


def _shift_rows(v, prev8_ref, n):
    r = lax.broadcasted_iota(jnp.int32, v.shape, 0)
    rolled = pltpu.roll(v, n, axis=0)
    last = prev8_ref[7:8, :]
    if n == 1:
        return jnp.where(r >= 1, rolled, last)
    return jnp.where(r >= 2, rolled, jnp.where(r == 0, prev8_ref[6:7, :], last))


def _shift_rows_up(v, next8_ref, n):
    rows = v.shape[0]
    r = lax.broadcasted_iota(jnp.int32, v.shape, 0)
    rolled = pltpu.roll(v, rows - n, axis=0)
    first = next8_ref[0:1, :]
    if n == 1:
        return jnp.where(r <= rows - 2, rolled, first)
    return jnp.where(r <= rows - 3, rolled, jnp.where(r == rows - 2, first, next8_ref[1:2, :]))


def _lane_half_mask(shape, half):
    lane = lax.broadcasted_iota(jnp.int32, shape, 1)
    return (lane >= HEAD_DIM) if half else (lane < HEAD_DIM)


def _to_kv_lanes(chunk, head, kv):
    if head % 2 != kv:
        chunk = pltpu.roll(chunk, HEAD_DIM, axis=1)
    return jnp.where(_lane_half_mask(chunk.shape, kv), chunk, 0.0)


def _from_kv_lanes(chunk, head, kv):
    chunk = jnp.where(_lane_half_mask(chunk.shape, kv), chunk, 0.0)
    if head % 2 != kv:
        chunk = pltpu.roll(chunk, HEAD_DIM, axis=1)
    return chunk


def _stack_heads(wide):
    parts = []
    for head in range(N_Q_HEADS):
        chunk = wide[:, (head // 2) * 128:(head // 2 + 1) * 128]
        parts.append(_to_kv_lanes(chunk, head, head // Q_PER_KV))
    return jnp.concatenate(parts, axis=0)


def _window_mask(has_prev):
    shape = (N_Q_HEADS * BLOCK, 2 * BLOCK)
    qi = lax.broadcasted_iota(jnp.int32, shape, 0) & (BLOCK - 1)
    kj = lax.broadcasted_iota(jnp.int32, shape, 1)
    first_key = BLOCK - has_prev * BLOCK
    in_prev = (kj < BLOCK) & (kj > qi) & (kj >= first_key)
    in_own = (kj >= BLOCK) & ((kj - BLOCK) <= qi)
    return in_prev | in_own


def _sink_column(sink_ref):
    row = lax.broadcasted_iota(jnp.int32, (N_Q_HEADS * BLOCK, 1), 0)
    col = jnp.full((N_Q_HEADS * BLOCK, 1), sink_ref[0, 0], F32)
    for head in range(1, N_Q_HEADS):
        col = jnp.where(row >= head * BLOCK, sink_ref[0, head], col)
    return col


def _softmax_with_sink(q4, k2, mask, sink):
    s = lax.dot_general(q4, k2, NT_DIMS, preferred_element_type=F32) * SM_SCALE
    s = jnp.where(mask, s, MASK_VALUE)
    m = jnp.maximum(jnp.max(s, axis=-1, keepdims=True), sink)
    p = jnp.exp(s - m)
    e_sink = jnp.exp(sink - m)
    inv_den = 1.0 / (jnp.sum(p, axis=-1, keepdims=True) + e_sink)
    return p * inv_den, e_sink * inv_den


def _conv_terms(zf, prev8_ref, w_ref):
    b_gate, c_gate, u = zf[:, 0:CONV_W], zf[:, CONV_W:2 * CONV_W], zf[:, 2 * CONV_W:3 * CONV_W]
    vc = c_gate * u
    vm1 = _shift_rows(vc, prev8_ref, 1)
    vm2 = _shift_rows(vc, prev8_ref, 2)
    conv = w_ref[0:1, :] * vm2 + w_ref[1:2, :] * vm1 + w_ref[2:3, :] * vc
    return b_gate, c_gate, u, vc, vm1, vm2, conv


def _mixer_fwd(x, gain, win_t, wout, conv_w, sinks, rope, *, name, after=(), tq=512):
    t = x.shape[0]
    tq = min(tq, t)
    nblk = tq // BLOCK

    def body(x_ref, g_ref, win_hbm, wout_hbm, cw_ref, sink_ref, rope_ref,
             xo_ref, h_ref, z_ref, y_ref, kprev_ref, vprev_ref, cprev_ref, win_ref, wout_ref, sems):
        i = pl.program_id(0)
        _load_resident(win_hbm, win_ref, sems.at[0])
        _load_resident(wout_hbm, wout_ref, sems.at[1])

        @pl.when(i == 0)
        def _():
            kprev_ref[...] = jnp.zeros_like(kprev_ref)
            vprev_ref[...] = jnp.zeros_like(vprev_ref)
            cprev_ref[...] = jnp.zeros_like(cprev_ref)

        xf = x_ref[...]
        xhat, _ = _rms_stats(xf)
        h = (xhat * g_ref[...]).astype(BF16)
        h_ref[...] = h
        zb = lax.dot_general(h, win_ref[...], NT_DIMS, preferred_element_type=F32).astype(BF16)
        z_ref[...] = zb
        zf = zb.astype(F32)

        b_gate, _, _, vc, _, _, conv = _conv_terms(zf, cprev_ref, cw_ref)
        y_conv = b_gate * conv
        cprev_ref[...] = vc[tq - 8:tq, :]

        tab = rope_ref[...]
        qr = _rope(zf[:, Q_OFF:K_OFF], tab)
        kr = _rope(zf[:, K_OFF:V_OFF], tab).astype(BF16)
        vb = zb[:, V_OFF:Z_W]

        y_attn = []
        for j in range(nblk):
            rows = slice(j * BLOCK, (j + 1) * BLOCK)
            prev = slice((j - 1) * BLOCK, j * BLOCK)
            k2 = jnp.concatenate([kprev_ref[...] if j == 0 else kr[prev], kr[rows]], axis=0)
            v2 = jnp.concatenate([vprev_ref[...] if j == 0 else vb[prev], vb[rows]], axis=0)
            mask = _window_mask(jnp.minimum(i, 1) if j == 0 else 1)
            q8 = _stack_heads(qr[rows]).astype(BF16)
            probs, _ = _softmax_with_sink(q8, k2, mask, _sink_column(sink_ref))
            o8 = jnp.dot(probs.astype(BF16), v2, preferred_element_type=F32)
            chunks = [jnp.zeros((BLOCK, 128), F32) for _ in range(ATTN_W // 128)]
            for head in range(N_Q_HEADS):
                chunks[head // 2] += _from_kv_lanes(o8[head * BLOCK:(head + 1) * BLOCK], head, head // Q_PER_KV)
            y_attn.append(jnp.concatenate(chunks, axis=1))
        kprev_ref[...] = kr[tq - BLOCK:tq]
        vprev_ref[...] = vb[tq - BLOCK:tq]
        y = jnp.concatenate([y_conv, jnp.concatenate(y_attn, axis=0)], axis=1).astype(BF16)
        y_ref[...] = y
        xo_ref[...] = xf + jnp.dot(y, wout_ref[...], preferred_element_type=F32)

    row = pl.BlockSpec((tq, D_MODEL), lambda i: (i, 0))
    full = lambda shape: pl.BlockSpec(shape, lambda i: (0,) * len(shape))
    return pl.pallas_call(
        _behind(body, 7, after), name=name, grid=(t // tq,),
        in_specs=[row, full((1, D_MODEL)), ANY, ANY, full((3, CONV_W)),
                  pl.BlockSpec(memory_space=pltpu.SMEM), pl.BlockSpec((3, tq, 128), lambda i: (0, i, 0))]
        + [ANY] * len(after),
        out_specs=[row, row, pl.BlockSpec((tq, Z_W), lambda i: (i, 0)), row],
        out_shape=[jax.ShapeDtypeStruct((t, D_MODEL), F32), jax.ShapeDtypeStruct((t, D_MODEL), BF16),
                   jax.ShapeDtypeStruct((t, Z_W), BF16), jax.ShapeDtypeStruct((t, D_MODEL), BF16)],
        scratch_shapes=[pltpu.VMEM((BLOCK, KV_W), BF16), pltpu.VMEM((BLOCK, KV_W), BF16),
                        pltpu.VMEM((8, CONV_W), F32), pltpu.VMEM((Z_W, D_MODEL), BF16),
                        pltpu.VMEM((D_MODEL, D_MODEL), BF16), pltpu.SemaphoreType.DMA((2,))],
        compiler_params=_params(("arbitrary",), VMEM_LIMIT),
    )(x, gain, win_t, wout, conv_w, sinks, rope, *after)


def _mixer_bwd(dxo, x, gain, y, z, win_t, wout, conv_w, sinks, rope, *, name, after=(), tq=256):
    t = x.shape[0]
    tq = min(tq, t)
    nt, nblk = t // tq, tq // BLOCK

    def body(dxo_ref, x_ref, g_ref, y_ref, z_ref, zp_ref, win_hbm, wout_hbm, cw_ref, sink_ref, rope_ref, ropep_ref,
             dxi_ref, dz_ref, gb_ref, dcw_ref, dsink_ref, dg_ref, dk_ref, dv_ref, dcn_ref, pvc_ref,
             win_ref, wout_ref, sems):
        i = pl.program_id(0)
        tile = nt - 1 - i
        _load_resident(win_hbm, win_ref, sems.at[0])
        _load_resident(wout_hbm, wout_ref, sems.at[1])

        @pl.when(i == 0)
        def _():
            dk_ref[...] = jnp.zeros_like(dk_ref)
            dv_ref[...] = jnp.zeros_like(dv_ref)
            dcn_ref[...] = jnp.zeros_like(dcn_ref)
            dcw_ref[...] = jnp.zeros_like(dcw_ref)
            dsink_ref[...] = jnp.zeros_like(dsink_ref)
            dg_ref[...] = jnp.zeros_like(dg_ref)

        has_prev = jnp.minimum(tile, 1)
        go = dxo_ref[...]
        gb = go.astype(BF16)
        gb_ref[...] = gb
        dy = lax.dot_general(gb, wout_ref[...], NT_DIMS, preferred_element_type=F32)
        dy_conv, dy_attn = dy[:, 0:CONV_W], dy[:, CONV_W:D_MODEL]
        zb, zpb = z_ref[...], zp_ref[...]
        zf = zb.astype(F32)
        zpf = zpb.astype(F32) * has_prev.astype(F32)

        pvc_ref[...] = (zpf[:, CONV_W:2 * CONV_W] * zpf[:, 2 * CONV_W:3 * CONV_W])[BLOCK - 8:BLOCK, :]
        b_gate, c_gate, u, vc, vm1, vm2, conv = _conv_terms(zf, pvc_ref, cw_ref)
        d_bgate = dy_conv * conv
        dc = dy_conv * b_gate
        tap = lax.broadcasted_iota(jnp.int32, (8, CONV_W), 0)
        dcw_ref[...] += jnp.where(tap == 0, jnp.sum(dc * vm2, axis=0, keepdims=True),
                                  jnp.where(tap == 1, jnp.sum(dc * vm1, axis=0, keepdims=True),
                                            jnp.where(tap == 2, jnp.sum(dc * vc, axis=0, keepdims=True), 0.0)))
        dvc = (cw_ref[2:3, :] * dc + cw_ref[1:2, :] * _shift_rows_up(dc, dcn_ref, 1)
               + cw_ref[0:1, :] * _shift_rows_up(dc, dcn_ref, 2))
        dcn_ref[...] = dc[0:8, :]
        d_cgate = dvc * u
        d_u = dvc * c_gate

        tab, tabp = rope_ref[...], ropep_ref[...]
        qr = _rope(zf[:, Q_OFF:K_OFF], tab)
        kr = _rope(zf[:, K_OFF:V_OFF], tab).astype(BF16)
        kpr = _rope(zpf[:, K_OFF:V_OFF], tabp).astype(BF16)
        vb, vpb = zb[:, V_OFF:Z_W], zpb[:, V_OFF:Z_W]
        out = y_ref[:, CONV_W:D_MODEL].astype(F32)
        do_out = dy_attn * out
        lane = lax.broadcasted_iota(jnp.int32, (1, 128), 1)
        dsink = jnp.zeros((1, 128), F32)
        dk_next, dv_next = dk_ref[...], dv_ref[...]
        dq_rows, dk_rows, dv_rows = [None] * nblk, [None] * nblk, [None] * nblk
        for j in reversed(range(nblk)):
            rows = slice(j * BLOCK, (j + 1) * BLOCK)
            prev = slice((j - 1) * BLOCK, j * BLOCK)
            k2 = jnp.concatenate([kpr if j == 0 else kr[prev], kr[rows]], axis=0)
            v2 = jnp.concatenate([vpb if j == 0 else vb[prev], vb[rows]], axis=0)
            mask = _window_mask(has_prev if j == 0 else 1)
            q8 = _stack_heads(qr[rows]).astype(BF16)
            do8 = _stack_heads(dy_attn[rows]).astype(BF16)
            delta = jnp.sum(_stack_heads(do_out[rows]), axis=-1, keepdims=True)
            probs, p_sink = _softmax_with_sink(q8, k2, mask, _sink_column(sink_ref))
            dp = lax.dot_general(do8, v2, NT_DIMS, preferred_element_type=F32)
            ds = (probs * (dp - delta) * SM_SCALE).astype(BF16)
            dq8 = jnp.dot(ds, k2, preferred_element_type=F32)
            dk2 = lax.dot_general(ds, q8, TN_DIMS, preferred_element_type=F32)
            dv2 = lax.dot_general(probs.astype(BF16), do8, TN_DIMS, preferred_element_type=F32)
            sink_terms = p_sink * delta
            dq_chunks = [jnp.zeros((BLOCK, 128), F32) for _ in range(ATTN_W // 128)]
            for head in range(N_Q_HEADS):
                grp = slice(head * BLOCK, (head + 1) * BLOCK)
                dq_chunks[head // 2] += _from_kv_lanes(dq8[grp], head, head // Q_PER_KV)
                dsink = dsink - jnp.where(lane == head, jnp.sum(sink_terms[grp], axis=0, keepdims=True), 0.0)
            dq_rows[j] = jnp.concatenate(dq_chunks, axis=1)
            dk_rows[j] = dk2[BLOCK:] + dk_next
            dv_rows[j] = dv2[BLOCK:] + dv_next
            dk_next, dv_next = dk2[:BLOCK], dv2[:BLOCK]
        dk_ref[...] = dk_next
        dv_ref[...] = dv_next
        dsink_ref[...] += dsink
        dq = _rope_bwd(jnp.concatenate(dq_rows, axis=0), tab)
        dk = _rope_bwd(jnp.concatenate(dk_rows, axis=0), tab)
        dv = jnp.concatenate(dv_rows, axis=0)

        dzb = jnp.concatenate([d_bgate, d_cgate, d_u, dq, dk, dv], axis=1).astype(BF16)
        dz_ref[...] = dzb
        dh = jnp.dot(dzb, win_ref[...], preferred_element_type=F32)
        xhat, inv = _rms_stats(x_ref[...])
        dx, dgain = _rms_bwd(dh, xhat, inv, g_ref[...])
        dxi_ref[...] = go + dx
        dg_ref[...] += dgain

    rev = lambda i: (nt - 1 - i, 0)
    block_before = lambda i: jnp.maximum((nt - 1 - i) * nblk - 1, 0)
    row = pl.BlockSpec((tq, D_MODEL), rev)
    full = lambda shape: pl.BlockSpec(shape, lambda i: (0,) * len(shape))
    return pl.pallas_call(
        _behind(body, 12, after), name=name, grid=(nt,),
        in_specs=[row, row, full((1, D_MODEL)), row,
                  pl.BlockSpec((tq, Z_W), rev), pl.BlockSpec((BLOCK, Z_W), lambda i: (block_before(i), 0)),
                  ANY, ANY, full((3, CONV_W)),
                  pl.BlockSpec(memory_space=pltpu.SMEM),
                  pl.BlockSpec((3, tq, 128), lambda i: (0, nt - 1 - i, 0)),
                  pl.BlockSpec((3, BLOCK, 128), lambda i: (0, block_before(i), 0))] + [ANY] * len(after),
        out_specs=[row, pl.BlockSpec((tq, Z_W), rev), row, full((8, CONV_W)), full((1, 128)), full((1, D_MODEL))],
        out_shape=[jax.ShapeDtypeStruct((t, D_MODEL), F32), jax.ShapeDtypeStruct((t, Z_W), BF16),
                   jax.ShapeDtypeStruct((t, D_MODEL), BF16), jax.ShapeDtypeStruct((8, CONV_W), F32),
                   jax.ShapeDtypeStruct((1, 128), F32), jax.ShapeDtypeStruct((1, D_MODEL), F32)],
        scratch_shapes=[pltpu.VMEM((BLOCK, KV_W), F32), pltpu.VMEM((BLOCK, KV_W), F32), pltpu.VMEM((8, CONV_W), F32),
                        pltpu.VMEM((8, CONV_W), F32), pltpu.VMEM((Z_W, D_MODEL), BF16),
                        pltpu.VMEM((D_MODEL, D_MODEL), BF16), pltpu.SemaphoreType.DMA((2,))],
        compiler_params=_params(("arbitrary",), VMEM_LIMIT),
    )(dxo, x, gain, y, z, z, win_t, wout, conv_w, sinks, rope, rope, *after)


def _place():
    x, y, c = lax.axis_index("x"), lax.axis_index("y"), lax.axis_index("c")
    other_chips = [(1 - x, y), (x, 1 - y), (1 - x, 1 - y)]
    return x, y, c, other_chips


def _all_gather_rows(shards, place=(), *, name):
    n, p = len(shards), len(place)

    def body(*refs):
        srcs, place_srcs = refs[:n], refs[n:n + p]
        outs, place_outs = refs[n + p:2 * n + p], refs[2 * n + p:2 * (n + p)]
        send_sems, recv_sems, local_sems = refs[2 * (n + p):]
        x, y, c, chips = _place()
        me, sibling = (x, y, c), (x, y, 1 - c)

        def rows(t, px, py, pc):
            r = srcs[t].shape[-2]
            start = pl.multiple_of((4 * px + 2 * py + pc) * r, 16 if r % 16 == 0 else 8)
            if len(srcs[t].shape) == 3:
                return outs[t].at[:, pl.ds(start, r), :]
            return outs[t].at[pl.ds(start, r), :]

        def copy(t, k, block, to, own=False):
            return pltpu.make_async_remote_copy(
                src_ref=srcs[t] if own else rows(t, *block), dst_ref=rows(t, *block),
                send_sem=send_sems.at[t, k], recv_sem=recv_sems.at[t, k], device_id=to, device_id_type=MESH)

        mine = [pltpu.make_async_copy(srcs[t], rows(t, *me), local_sems.at[t]) for t in range(n)]
        mine += [pltpu.make_async_copy(place_srcs[q],
                                       _block_rows(place_outs[q], place_srcs[q].shape[-2], 4 * x + 2 * y + c),
                                       local_sems.at[n + q]) for q in range(p)]
        for q in range(p):
            mine[n + q].start()
        first = []
        for t in range(n):
            mine[t].start()
            first.append(copy(t, 0, me, sibling, own=True))
            first += [copy(t, 1 + j, me, (*chip, c), own=True) for j, chip in enumerate(chips)]
        for cp in first:
            cp.start()
        passed = []
        for j, chip in enumerate(chips):
            for t in range(n):
                copy(t, 1 + j, (*chip, c), me).wait_recv()
                fwd = copy(t, 4 + j, (*chip, c), sibling)
                fwd.start()
                passed.append(fwd)
        for t in range(n):
            copy(t, 0, sibling, me).wait_recv()
            for j, chip in enumerate(chips):
                copy(t, 4 + j, (*chip, 1 - c), me).wait_recv()
        for cp in first + passed:
            cp.wait_send()
        for cp in mine:
            cp.wait()

    out_shape = [jax.ShapeDtypeStruct(s.shape[:-2] + (N_DEV * s.shape[-2], s.shape[-1]), s.dtype)
                 for s in list(shards) + list(place)]
    res = pl.pallas_call(
        body, name=name, in_specs=[ANY] * (n + p), out_specs=[ANY] * (n + p), out_shape=out_shape,
        scratch_shapes=[pltpu.SemaphoreType.DMA((n, 7)), pltpu.SemaphoreType.DMA((n, 7)),
                        pltpu.SemaphoreType.DMA((n + p,))],
    )(*shards, *place)
    return res[:n], res[n:]


def _split_start(bufs, n_copies, plan, *, name, after=()):
    n = len(bufs)

    def body(*refs):
        token = refs[-1]
        for cp in plan(refs[:n], refs[n], refs[n + 1]):
            cp.start()
        token[...] = jnp.zeros_like(token)

    res = pl.pallas_call(
        _behind(body, n, after), name=name, in_specs=[HBM_SPEC] * n + [ANY] * len(after),
        out_specs=(SEM_SPEC, SEM_SPEC, *[HBM_SPEC] * n, pl.BlockSpec(memory_space=pltpu.VMEM)),
        out_shape=(pltpu.SemaphoreType.DMA((n_copies,)), pltpu.SemaphoreType.DMA((n_copies,)),
                   *[pltpu.HBM(b.shape, b.dtype) for b in bufs], jax.ShapeDtypeStruct((8, 128), F32)),
        input_output_aliases={i: 2 + i for i in range(n)},
        compiler_params=pltpu.CompilerParams(has_side_effects=DATAFLOW),
    )(*[pltpu.with_memory_space_constraint(b, pltpu.HBM) for b in bufs], *after)
    return res[0], res[1], list(res[2:2 + n]), res[-1]


def _split_wait(send_sems, recv_sems, bufs, after, plan, *, name):
    n = len(bufs)

    def body(*refs):
        for cp in plan(refs[:n], refs[n], refs[n + 1]):
            cp.wait_send()
            cp.wait_recv()

    return list(pl.pallas_call(
        body, name=name, in_specs=[HBM_SPEC] * n + [SEM_SPEC, SEM_SPEC, ANY], out_specs=[HBM_SPEC] * n,
        out_shape=tuple(pltpu.HBM(b.shape, b.dtype) for b in bufs),
        input_output_aliases={i: i for i in range(n)},
        compiler_params=pltpu.CompilerParams(has_side_effects=DATAFLOW),
    )(*bufs, send_sems, recv_sems, after))


def _sibling_plan(n):
    def plan(bufs, send_sems, recv_sems):
        x, y, c, _ = _place()
        return [pltpu.make_async_remote_copy(
            src_ref=bufs[t].at[:, 1 - c], dst_ref=bufs[n + t], send_sem=send_sems.at[t], recv_sem=recv_sems.at[t],
            device_id=(x, y, 1 - c), device_id_type=MESH) for t in range(n)]
    return plan


def _block_rows(ref, r, blk):
    start = pl.multiple_of(blk * r, 16 if r % 16 == 0 else 8)
    return ref.at[(slice(None),) * (len(ref.shape) - 2) + (pl.ds(start, r), slice(None))]


def _remote(src, dst, send_sems, recv_sems, k, peer):
    return pltpu.make_async_remote_copy(src_ref=src, dst_ref=dst, send_sem=send_sems.at[k], recv_sem=recv_sems.at[k],
                                        device_id=peer, device_id_type=MESH)


def _gather_send_plan(n):
    def plan(bufs, send_sems, recv_sems):
        x, y, c, chips = _place()
        peers = [(x, y, 1 - c)] + [(px, py, c) for px, py in chips]
        copies = []
        for t in range(n):
            dst = _block_rows(bufs[n + t], bufs[t].shape[-2], 4 * x + 2 * y + c)
            copies += [_remote(bufs[t], dst, send_sems, recv_sems, 4 * t + k, peer) for k, peer in enumerate(peers)]
        return copies
    return plan


def _gather_forward_plan(rows):
    def plan(bufs, send_sems, recv_sems):
        x, y, c, chips = _place()
        copies = []
        for t, r in enumerate(rows):
            for j, (px, py) in enumerate(chips):
                blk = _block_rows(bufs[t], r, 4 * px + 2 * py + c)
                copies.append(_remote(blk, blk, send_sems, recv_sems, 3 * t + j, (x, y, 1 - c)))
        return copies
    return plan


def _chips_plan(n, with_small):
    def plan(bufs, send_sems, recv_sems):
        x, y, c, chips = _place()
        copies = []
        for t in range(n):
            for j, (px, py) in enumerate(chips):
                copies.append(_remote(bufs[t].at[2 * px + py], bufs[n + t].at[j], send_sems, recv_sems, 3 * t + j,
                                      (px, py, c)))
        if with_small:
            mine = _block_rows(bufs[2 * n], 8, 4 * x + 2 * y + c)
            flips = [(fx, fy, fc) for fx in range(2) for fy in range(2) for fc in range(2)][1:]
            for k, (fx, fy, fc) in enumerate(flips):
                peer = (x + fx - 2 * x * fx, y + fy - 2 * y * fy, c + fc - 2 * c * fc)
                copies.append(_remote(mine, mine, send_sems, recv_sems, 3 * n + k, peer))
        return copies
    return plan


def _place_own(fulls, shards, index, *, name):
    n = len(fulls)

    def body(index_ref, *refs):
        for t in range(n):
            refs[2 * n + t][...] = refs[n + t][...]

    def block_of(shard):
        lead = len(shard.shape) - 2
        return pl.BlockSpec(shard.shape, lambda i, index_ref: (0,) * lead + (index_ref[0], 0))

    def whole(shard):
        return pl.BlockSpec(shard.shape, lambda i, index_ref: (0,) * len(shard.shape))

    return list(pl.pallas_call(
        body, name=name,
        grid_spec=pltpu.PrefetchScalarGridSpec(
            num_scalar_prefetch=1, grid=(1,),
            in_specs=[ANY] * n + [whole(s) for s in shards], out_specs=[block_of(s) for s in shards]),
        out_shape=[jax.ShapeDtypeStruct(f.shape, f.dtype) for f in fulls],
        input_output_aliases={1 + t: t for t in range(n)},
        compiler_params=_params(("arbitrary",)),
    )(index, *fulls, *shards))


def _add_sibling(grad, recv, core, *, name, tr):
    rows = grad.shape[2]

    def body(core_ref, g_ref, r_ref, o_ref, ob_ref):
        p = g_ref[:, 0] + r_ref[...]
        o_ref[...] = p
        ob_ref[...] = p.astype(BF16)

    out = pl.BlockSpec((4, tr, D_MODEL), lambda i, core_ref: (0, i, 0))
    return pl.pallas_call(
        body, name=name,
        grid_spec=pltpu.PrefetchScalarGridSpec(
            num_scalar_prefetch=1, grid=(rows // tr,),
            in_specs=[pl.BlockSpec((4, 1, tr, D_MODEL), lambda i, core_ref: (0, core_ref[0], i, 0)), out],
            out_specs=[out, out]),
        out_shape=[jax.ShapeDtypeStruct(recv.shape, F32), jax.ShapeDtypeStruct(recv.shape, BF16)],
        compiler_params=_params(("arbitrary",)),
    )(core, grad, recv)


def _reduce_adamw(partial, recv, chip, w, m, v, *, name, tr, after=()):
    rows = partial.shape[1]

    def body(chip_ref, p_ref, r_ref, w_ref, m_ref, v_ref, g_ref, d_ref, mo_ref, vo_ref):
        g = p_ref[0] + r_ref[0].astype(F32) + r_ref[1].astype(F32) + r_ref[2].astype(F32)
        g_ref[...] = g
        d_ref[...], mo_ref[...], vo_ref[...] = _adamw_math(w_ref[...], g, m_ref[...], v_ref[...])

    spec = pl.BlockSpec((tr, D_MODEL), lambda i, chip_ref: (i, 0))
    return pl.pallas_call(
        _behind(body, 6, after), name=name,
        grid_spec=pltpu.PrefetchScalarGridSpec(
            num_scalar_prefetch=1, grid=(rows // tr,),
            in_specs=[pl.BlockSpec((1, tr, D_MODEL), lambda i, chip_ref: (chip_ref[0], i, 0)),
                      pl.BlockSpec((3, tr, D_MODEL), lambda i, chip_ref: (0, i, 0)), spec, spec, spec]
            + [ANY] * len(after),
            out_specs=[spec] * 4),
        out_shape=[jax.ShapeDtypeStruct((rows, D_MODEL), F32)] * 4,
        compiler_params=_params(("arbitrary",)),
    )(chip, partial, recv, w, m, v, *after)


def _adamw_math(w, g, m, v):
    m = ADAM_B1 * m + (1.0 - ADAM_B1) * g
    v = ADAM_B2 * v + (1.0 - ADAM_B2) * (g * g)
    m_hat = m / (1.0 - ADAM_B1 ** ADAM_STEP)
    v_hat = v / (1.0 - ADAM_B2 ** ADAM_STEP)
    delta = -ADAM_LR * (m_hat / (jnp.sqrt(v_hat) + ADAM_EPS) + ADAM_WD * w)
    return delta, m, v


def _adamw(w, g, m, v, *, name, tr, after=()):
    rows, cols = w.shape

    def body(w_ref, g_ref, m_ref, v_ref, d_ref, mo_ref, vo_ref):
        d_ref[...], mo_ref[...], vo_ref[...] = _adamw_math(w_ref[...], g_ref[...], m_ref[...], v_ref[...])

    spec = pl.BlockSpec((tr, cols), lambda i: (i, 0))
    return pl.pallas_call(
        _behind(body, 4, after), name=name, grid=(rows // tr,), in_specs=[spec] * 4 + [ANY] * len(after),
        out_specs=[spec] * 3, out_shape=[jax.ShapeDtypeStruct(w.shape, F32)] * 3,
        compiler_params=_params(("parallel",)),
    )(w, g, m, v, *after)


def _sum_small(gathered, *, name):
    def body(g_ref, o_ref):
        acc = g_ref[0]
        for k in range(1, N_DEV):
            acc = acc + g_ref[k]
        o_ref[...] = acc

    return pl.pallas_call(body, name=name, out_shape=jax.ShapeDtypeStruct(gathered.shape[1:], F32))(gathered)


def kernel(x, ffn1_norm, ffn1_w_gate, ffn1_w_up, ffn1_w_down, mix_norm, w_in, conv_w, attn_sinks, w_out, ffn2_norm, ffn2_w_gate, ffn2_w_up, ffn2_w_down, final_norm, loss_target, m_ffn1_norm, m_ffn1_w_gate, m_ffn1_w_up, m_ffn1_w_down, m_mix_norm, m_w_in, m_conv_w, m_attn_sinks, m_w_out, m_ffn2_norm, m_ffn2_w_gate, m_ffn2_w_up, m_ffn2_w_down, m_final_norm, v_ffn1_norm, v_ffn1_w_gate, v_ffn1_w_up, v_ffn1_w_down, v_mix_norm, v_w_in, v_conv_w, v_attn_sinks, v_w_out, v_ffn2_norm, v_ffn2_w_gate, v_ffn2_w_up, v_ffn2_w_down, v_final_norm):
    ix, iy, ic = lax.axis_index("x"), lax.axis_index("y"), lax.axis_index("c")
    my_index = 4 * ix + 2 * iy + ic
    core = ic.astype(jnp.int32).reshape(1)
    chip = (2 * ix + iy).astype(jnp.int32).reshape(1)

    given = dict(ffn1_norm=ffn1_norm, ffn1_w_gate=ffn1_w_gate, ffn1_w_up=ffn1_w_up, ffn1_w_down=ffn1_w_down,
                 mix_norm=mix_norm, w_in=w_in, conv_w=conv_w, attn_sinks=attn_sinks, w_out=w_out, ffn2_norm=ffn2_norm,
                 ffn2_w_gate=ffn2_w_gate, ffn2_w_up=ffn2_w_up, ffn2_w_down=ffn2_w_down, final_norm=final_norm)
    moments_m = dict(ffn1_norm=m_ffn1_norm, ffn1_w_gate=m_ffn1_w_gate, ffn1_w_up=m_ffn1_w_up, ffn1_w_down=m_ffn1_w_down,
                     mix_norm=m_mix_norm, w_in=m_w_in, conv_w=m_conv_w, attn_sinks=m_attn_sinks, w_out=m_w_out,
                     ffn2_norm=m_ffn2_norm, ffn2_w_gate=m_ffn2_w_gate, ffn2_w_up=m_ffn2_w_up, ffn2_w_down=m_ffn2_w_down,
                     final_norm=m_final_norm)
    moments_v = dict(ffn1_norm=v_ffn1_norm, ffn1_w_gate=v_ffn1_w_gate, ffn1_w_up=v_ffn1_w_up, ffn1_w_down=v_ffn1_w_down,
                     mix_norm=v_mix_norm, w_in=v_w_in, conv_w=v_conv_w, attn_sinks=v_attn_sinks, w_out=v_w_out,
                     ffn2_norm=v_ffn2_norm, ffn2_w_gate=v_ffn2_w_gate, ffn2_w_up=v_ffn2_w_up, ffn2_w_down=v_ffn2_w_down,
                     final_norm=v_final_norm)

    xs = x[0]
    target = loss_target[0]
    final_gain = final_norm.reshape(1, D_MODEL)

    def ffn_shard(wg, wu, wd):
        return jnp.stack([wg[0].T, wu[0].T, wd[0]]).astype(BF16)

    conv_cols = conv_w.shape[2]
    conv_shard = jnp.pad(conv_w[0], ((0, 5), (0, 128 - conv_cols)))
    rest_shards = [ffn_shard(ffn2_w_gate, ffn2_w_up, ffn2_w_down), w_in[0].T.astype(BF16), w_out[0].astype(BF16),
                   conv_shard]
    rest_rows = [s.shape[-2] for s in rest_shards]
    n_rest = len(rest_shards)
    (w1,), _ = _all_gather_rows([ffn_shard(ffn1_w_gate, ffn1_w_up, ffn1_w_down)], name="gather_ffn1")

    fulls = [lax.empty(s.shape[:-2] + (N_DEV * s.shape[-2], s.shape[-1]), s.dtype) for s in rest_shards]
    fulls = _place_own(fulls, rest_shards, my_index.astype(jnp.int32).reshape(1), name="place_own_weights")
    send_plan = _gather_send_plan(n_rest)
    ssem, rsem, bufs, token = _split_start(rest_shards + list(fulls), 4 * n_rest, send_plan, name="gather_rest_start",
                                           after=[w1])
    x1, h1, s1, sa1, sb1 = _ffn_fwd(xs, ffn1_norm, w1, name="ffn1_fwd", after=[token])
    bufs = _split_wait(ssem, rsem, bufs, x1, send_plan, name="gather_rest_wait")
    w2_part, mixer_parts = bufs[n_rest], bufs[n_rest + 1:]
    fwd_mixer = _gather_forward_plan(rest_rows[1:])
    ssem, rsem, bufs, token = _split_start(mixer_parts, 3 * (n_rest - 1), fwd_mixer, name="forward_mixer_start")
    win_t, wout, conv_all = _split_wait(ssem, rsem, bufs, token, fwd_mixer, name="forward_mixer_wait")
    conv_full = conv_all.reshape(N_DEV, 8, 128)[:, :3, :conv_cols].transpose(1, 0, 2).reshape(3, CONV_W)
    fwd_ffn2 = _gather_forward_plan(rest_rows[:1])
    ssem, rsem, bufs, token = _split_start([w2_part], 3, fwd_ffn2, name="forward_ffn2_start", after=[win_t])
    rope = _rope_tables(xs.shape[0])
    x2, hm, z, y = _mixer_fwd(x1, mix_norm, win_t, wout, conv_full, attn_sinks, rope, name="mixer_fwd", after=[token])
    (w2,) = _split_wait(ssem, rsem, bufs, x2, fwd_ffn2, name="forward_ffn2_wait")
    x3, h2, s2, sa2, sb2 = _ffn_fwd(x2, ffn2_norm, w2, name="ffn2_fwd")

    def to_sibling_start(grads, tag, after=()):
        views = [g.reshape(4, 2, g.shape[0] // N_DEV, D_MODEL) for g in grads]
        lands = [lax.empty((4,) + v.shape[2:], F32) for v in views]
        plan = _sibling_plan(len(views))
        ssem, rsem, bufs, token = _split_start(views + lands, len(views), plan, name=f"{tag}_sibling_start", after=after)
        return (ssem, rsem, bufs, plan, tag), token

    def to_sibling_finish(handle, after, names):
        ssem, rsem, bufs, plan, tag = handle
        bufs = _split_wait(ssem, rsem, bufs, after, plan, name=f"{tag}_sibling_wait")
        n = len(names)
        return [_add_sibling(v, r, core, name=f"add_sibling_{nm}", tr=v.shape[2] // 2)
                for v, r, nm in zip(bufs[:n], bufs[n:], names)]

    def to_chips_start(partials, tag, small_all=None, after=()):
        p16 = [p for _, p in partials]
        lands = [lax.empty((3,) + p.shape[1:], BF16) for p in p16]
        extra = [] if small_all is None else [small_all]
        plan = _chips_plan(len(p16), small_all is not None)
        ssem, rsem, bufs, token = _split_start(p16 + lands + extra, 3 * len(p16) + 7 * len(extra), plan,
                                               name=f"{tag}_chips_start", after=after)
        return (ssem, rsem, bufs, plan, tag), token

    def to_chips_finish(handle, partials, after, names):
        ssem, rsem, bufs, plan, tag = handle
        bufs = _split_wait(ssem, rsem, bufs, after, plan, name=f"{tag}_chips_wait")
        n = len(names)
        return [(p32, r) for (p32, _), r in zip(partials, bufs[n:2 * n])], bufs[2 * n:]

    half_ff = D_FF // 2
    names2, namesm = ["ffn2_w_gate", "ffn2_w_up", "ffn2_w_down"], ["w_in", "w_out"]
    transposed = {"ffn1_w_gate", "ffn1_w_up", "w_in", "ffn2_w_gate", "ffn2_w_up"}
    grad, delta, new_m, new_v = {}, {}, {}, {}

    def adam_big(nm, parts, after=()):
        to_rows = (lambda a: a[0].T) if nm in transposed else (lambda a: a[0])
        from_rows = (lambda a: a.T[None]) if nm in transposed else (lambda a: a[None])
        p32, recv = parts
        outs = _reduce_adamw(p32, recv, chip, to_rows(given[nm]), to_rows(moments_m[nm]), to_rows(moments_v[nm]),
                             name=f"adamw_{nm}", tr=p32.shape[1] // 2, after=after)
        grad[nm], delta[nm], new_m[nm], new_v[nm] = (from_rows(a) for a in outs)

    dx2, da2, db2, g2b, d_norm2, loss_local, d_final = _ffn_dgrad(
        x3, x2, ffn2_norm, sa2, sb2, w2, head=(final_gain, target), name="ffn2_dgrad")
    gw2 = [_tn_matmul(da2, h2, name="ffn2_wgrad_gate", bm=half_ff), _tn_matmul(db2, h2, name="ffn2_wgrad_up", bm=half_ff),
           _tn_matmul(s2, g2b, name="ffn2_wgrad_down", bm=half_ff)]
    sib2, tok = to_sibling_start(gw2, "ffn2")
    dx1, dz, gmb, d_conv, d_sink, d_normm = _mixer_bwd(dx2, x1, mix_norm, y, z, win_t, wout, conv_full, attn_sinks,
                                                       rope, name="mixer_bwd", after=[tok])
    p2 = to_sibling_finish(sib2, dx1, names2)
    chips2, tok = to_chips_start(p2, "ffn2")
    gwm = [_tn_matmul(dz, hm, name="mixer_wgrad_in", bm=Z_W // 3, after=[tok]),
           _tn_matmul(y, gmb, name="mixer_wgrad_out", bm=D_MODEL // 2, after=[tok])]
    sibm, tok = to_sibling_start(gwm, "mixer")
    dx0, da1, db1, g1b, d_norm1 = _ffn_dgrad(dx1, xs, ffn1_norm, sa1, sb1, w1, name="ffn1_dgrad", after=[tok], tm=512)
    r2, _ = to_chips_finish(chips2, p2, dx0, names2)
    pm = to_sibling_finish(sibm, dx0, namesm)
    chipsm, tok = to_chips_start(pm, "mixer")
    gw_gate = _tn_matmul(da1, h1, name="ffn1_wgrad_gate", bm=half_ff, after=[tok])
    sib_gate, tok = to_sibling_start([gw_gate], "ffn1_gate")
    gw_up = _tn_matmul(db1, h1, name="ffn1_wgrad_up", bm=half_ff, after=[tok])
    rm, _ = to_chips_finish(chipsm, pm, gw_up, namesm)
    p_gate = to_sibling_finish(sib_gate, gw_up, ["ffn1_w_gate"])
    chips_gate, tok_a = to_chips_start(p_gate, "ffn1_gate")
    sib_up, tok_b = to_sibling_start([gw_up], "ffn1_up", after=[tok_a])
    gw_down = _tn_matmul(s1, g1b, name="ffn1_wgrad_down", bm=half_ff, after=[tok_a, tok_b])
    p_up = to_sibling_finish(sib_up, gw_down, ["ffn1_w_up"])
    chips_up, tok_a = to_chips_start(p_up, "ffn1_up")
    sib_down, tok_b = to_sibling_start([gw_down], "ffn1_down", after=[tok_a])
    p_down = to_sibling_finish(sib_down, tok_b, ["ffn1_w_down"])
    last_row = (jnp.pad(d_sink, ((0, 0), (0, D_MODEL - 128)))
                + jnp.pad(loss_local, ((0, 0), (LOSS_LANE, D_MODEL - LOSS_LANE - 1))))
    small = jnp.concatenate([
        d_norm1, d_normm, d_norm2, d_final, jnp.pad(d_conv[0:3], ((0, 0), (0, D_MODEL - CONV_W))), last_row], axis=0)
    (small_all,) = _place_own([lax.empty((N_DEV * 8, D_MODEL), F32)], [small], my_index.astype(jnp.int32).reshape(1),
                              name="place_own_small")
    chips_down, tok = to_chips_start(p_down, "ffn1_down", small_all)
    behind = [tok]
    for nm, g in zip(names2 + namesm, r2 + rm):
        adam_big(nm, g, after=behind)
        behind = [new_v[nm]]
    r_gate, _ = to_chips_finish(chips_gate, p_gate, new_v["w_out"], ["ffn1_w_gate"])
    adam_big("ffn1_w_gate", r_gate[0])
    r_up, _ = to_chips_finish(chips_up, p_up, new_v["ffn1_w_gate"], ["ffn1_w_up"])
    adam_big("ffn1_w_up", r_up[0])
    r_down, (small_all,) = to_chips_finish(chips_down, p_down, new_v["ffn1_w_up"], ["ffn1_w_down"])
    adam_big("ffn1_w_down", r_down[0])
    small_sum = _sum_small(small_all.reshape(N_DEV, 8, D_MODEL), name="sum_small")
    loss = small_sum[7, LOSS_LANE]
    _update_small(given, moments_m, moments_v, small_sum, my_index, grad, delta, new_m, new_v)

    order = list(given)
    return (loss, dx0[None], *[grad[n] for n in order], *[delta[n] for n in order],
            *[new_m[n] for n in order], *[new_v[n] for n in order])


def _update_small(given, moments_m, moments_v, small_sum, my_index, grad, delta, new_m, new_v):
    conv_cols = given["conv_w"].shape[2]
    small_g = {
        "ffn1_norm": small_sum[0:1], "mix_norm": small_sum[1:2], "ffn2_norm": small_sum[2:3],
        "final_norm": small_sum[3:4],
        "conv_w": lax.dynamic_slice(small_sum[4:7, :CONV_W], (0, my_index * conv_cols), (3, conv_cols)),
        "attn_sinks": small_sum[7:8, :N_Q_HEADS],
    }

    small_names = ["ffn1_norm", "mix_norm", "ffn2_norm", "final_norm", "conv_w", "attn_sinks"]

    def pack(parts):
        rows = []
        for nm in small_names:
            p = parts[nm]
            p2 = p.reshape(3, conv_cols) if nm == "conv_w" else p.reshape(1, -1)
            rows.append(jnp.pad(p2, ((0, 0), (0, D_MODEL - p2.shape[1]))))
        rows.append(jnp.zeros((8, D_MODEL), F32))
        return jnp.concatenate(rows, axis=0)

    sd, sm, sv = _adamw(pack(given), pack(small_g), pack(moments_m), pack(moments_v), name="adamw_small", tr=16)
    row = 0
    for nm in small_names:
        shape = given[nm].shape
        nrow = 3 if nm == "conv_w" else 1
        ncol = conv_cols if nm == "conv_w" else given[nm].size
        grad[nm] = small_g[nm].reshape(shape)
        delta[nm], new_m[nm], new_v[nm] = (a[row:row + nrow, :ncol].reshape(shape) for a in (sd, sm, sv))
        row += nrow
```

```python
import functools

import jax
import jax.numpy as jnp
from jax import lax
from jax.experimental import pallas as pl
from jax.experimental.pallas import tpu as pltpu

F32 = jnp.float32
BF16 = jnp.bfloat16
MESH = pl.DeviceIdType.MESH
ANY = pl.BlockSpec(memory_space=pl.ANY)
HBM_SPEC = pl.BlockSpec(memory_space=pltpu.HBM)
SEM_SPEC = pl.BlockSpec(memory_space=pltpu.SEMAPHORE)
DATAFLOW = pltpu.SideEffectType.DATAFLOW_SIDE_EFFECTING

N_DEV = 8
LOSS_LANE = 128
D_MODEL = 1024
D_FF = 2816
CONV_W = 512
ATTN_W = 512
KV_W = 128
HEAD_DIM = 64
N_Q_HEADS = 8
N_KV_HEADS = 2
Q_PER_KV = N_Q_HEADS // N_KV_HEADS
BLOCK = 128
ROT_DIM = 16
ROPE_THETA = 500000.0
Z_W = 3 * CONV_W + ATTN_W + 2 * KV_W
Q_OFF = 3 * CONV_W
K_OFF = Q_OFF + ATTN_W
V_OFF = K_OFF + KV_W
RMS_EPS = 1e-5
MASK_VALUE = -1e30
SM_SCALE = HEAD_DIM ** -0.5
FFN_RES_SCALE = 0.5

ADAM_LR = 0.001
ADAM_B1 = 0.9
ADAM_B2 = 0.999
ADAM_EPS = 1e-08
ADAM_WD = 0.01
ADAM_STEP = 10

NT_DIMS = (((1,), (1,)), ((), ()))
TN_DIMS = (((0,), (0,)), ((), ()))

VMEM_LIMIT = 62 * 1024 * 1024
FF_CHUNK = 256


def _params(sem, vmem=None):
    return pltpu.CompilerParams(dimension_semantics=sem, vmem_limit_bytes=vmem)


def _behind(body, n_in, after):
    k = len(after)
    if k == 0:
        return body
    return lambda *refs: body(*refs[:n_in], *refs[n_in + k:])


def _rms_stats(xf):
    inv = lax.rsqrt(jnp.mean(xf * xf, axis=-1, keepdims=True) + RMS_EPS)
    return xf * inv, inv


def _rms_bwd(dh, xhat, inv, gain):
    dxhat = dh * gain
    dx = inv * (dxhat - xhat * jnp.mean(dxhat * xhat, axis=-1, keepdims=True))
    dgain = jnp.sum(dh * xhat, axis=0, keepdims=True)
    return dx, dgain


def _load_resident(w_hbm, w_ref, sem):
    @pl.when(pl.program_id(0) == 0)
    def _():
        cp = pltpu.make_async_copy(w_hbm, w_ref, sem)
        cp.start()
        cp.wait()


def _ffn_fwd(x, gain, w3, *, name, after=(), tm=256, tf=FF_CHUNK):
    t = x.shape[0]
    tm = min(tm, t)

    def body(x_ref, g_ref, w_hbm, xo_ref, h_ref, s_ref, sa_ref, sb_ref, w_ref, sem):
        _load_resident(w_hbm, w_ref, sem)
        xf = x_ref[...]
        xhat, _ = _rms_stats(xf)
        h = (xhat * g_ref[...]).astype(BF16)
        h_ref[...] = h
        for c in range(0, D_FF, tf):
            cols = slice(c, min(c + tf, D_FF))
            a = lax.dot_general(h, w_ref[0, cols, :], NT_DIMS, preferred_element_type=F32)
            b = lax.dot_general(h, w_ref[1, cols, :], NT_DIMS, preferred_element_type=F32)
            sig = jax.nn.sigmoid(a)
            silu = a * sig
            s_ref[:, cols] = (silu * b).astype(BF16)
            sa_ref[:, cols] = (b * (sig * (1.0 + a * (1.0 - sig)))).astype(BF16)
            sb_ref[:, cols] = silu.astype(BF16)
        xo_ref[...] = xf + FFN_RES_SCALE * jnp.dot(s_ref[...], w_ref[2], preferred_element_type=F32)

    row = pl.BlockSpec((tm, D_MODEL), lambda i: (i, 0))
    hid = pl.BlockSpec((tm, D_FF), lambda i: (i, 0))
    return pl.pallas_call(
        _behind(body, 3, after), name=name, grid=(t // tm,),
        in_specs=[row, pl.BlockSpec((1, D_MODEL), lambda i: (0, 0)), ANY] + [ANY] * len(after),
        out_specs=[row, row, hid, hid, hid],
        out_shape=[jax.ShapeDtypeStruct((t, D_MODEL), F32), jax.ShapeDtypeStruct((t, D_MODEL), BF16)]
        + [jax.ShapeDtypeStruct((t, D_FF), BF16)] * 3,
        scratch_shapes=[pltpu.VMEM((3, D_FF, D_MODEL), BF16), pltpu.SemaphoreType.DMA(())],
        compiler_params=_params(("arbitrary",), VMEM_LIMIT),
    )(x, gain, w3, *after)


def _ffn_dgrad(dxo, x, gain, sa, sb, w3, *, name, head=None, after=(), tm=256, tf=FF_CHUNK):
    t = x.shape[0]
    tm = min(tm, t)
    n_head = 0 if head is None else 2

    def body(*refs):
        dxo_ref, x_ref, g_ref, sa_ref, sb_ref, w_hbm = refs[:6]
        head_refs = refs[6:6 + n_head]
        dxi_ref, da_ref, db_ref, gb_ref, dg_ref = refs[6 + n_head:11 + n_head]
        head_outs = refs[11 + n_head:11 + 2 * n_head]
        w_ref, sem = refs[11 + 2 * n_head:]
        _load_resident(w_hbm, w_ref, sem)

        @pl.when(pl.program_id(0) == 0)
        def _():
            dg_ref[...] = jnp.zeros_like(dg_ref)
            for ref in head_outs:
                ref[...] = jnp.zeros_like(ref)

        if head is None:
            go = dxo_ref[...]
        else:
            fg_ref, t_ref = head_refs
            loss_ref, dfg_ref = head_outs
            xhat_o, inv_o = _rms_stats(dxo_ref[...])
            err = xhat_o * fg_ref[...] - t_ref[...]
            loss_ref[...] += 0.5 * jnp.sum(jnp.mean(err * err, axis=-1, keepdims=True), axis=0, keepdims=True)
            go, dfg = _rms_bwd(err * (1.0 / D_MODEL), xhat_o, inv_o, fg_ref[...])
            dfg_ref[...] += dfg
        gb = (FFN_RES_SCALE * go).astype(BF16)
        gb_ref[...] = gb
        for c in range(0, D_FF, tf):
            cols = slice(c, min(c + tf, D_FF))
            ds = lax.dot_general(gb, w_ref[2, cols, :], NT_DIMS, preferred_element_type=F32)
            da_ref[:, cols] = (ds * sa_ref[:, cols].astype(F32)).astype(BF16)
            db_ref[:, cols] = (ds * sb_ref[:, cols].astype(F32)).astype(BF16)
        dh = (jnp.dot(da_ref[...], w_ref[0], preferred_element_type=F32)
              + jnp.dot(db_ref[...], w_ref[1], preferred_element_type=F32))
        xhat, inv = _rms_stats(x_ref[...])
        dx, dgain = _rms_bwd(dh, xhat, inv, g_ref[...])
        dxi_ref[...] = go + dx
        dg_ref[...] += dgain

    row = pl.BlockSpec((tm, D_MODEL), lambda i: (i, 0))
    hid = pl.BlockSpec((tm, D_FF), lambda i: (i, 0))
    vec = pl.BlockSpec((1, D_MODEL), lambda i: (0, 0))
    head_in = [] if head is None else [vec, row]
    head_out = [] if head is None else [pl.BlockSpec((1, 1), lambda i: (0, 0)), vec]
    head_shape = [] if head is None else [jax.ShapeDtypeStruct((1, 1), F32), jax.ShapeDtypeStruct((1, D_MODEL), F32)]
    return pl.pallas_call(
        _behind(body, 6 + n_head, after), name=name, grid=(t // tm,),
        in_specs=[row, row, vec, hid, hid, ANY] + head_in + [ANY] * len(after),
        out_specs=[row, hid, hid, row, vec] + head_out,
        out_shape=[jax.ShapeDtypeStruct((t, D_MODEL), F32), jax.ShapeDtypeStruct((t, D_FF), BF16),
                   jax.ShapeDtypeStruct((t, D_FF), BF16),
                   jax.ShapeDtypeStruct((t, D_MODEL), BF16), jax.ShapeDtypeStruct((1, D_MODEL), F32)] + head_shape,
        scratch_shapes=[pltpu.VMEM((3, D_FF, D_MODEL), BF16), pltpu.SemaphoreType.DMA(())],
        compiler_params=_params(("arbitrary",), VMEM_LIMIT),
    )(dxo, x, gain, sa, sb, w3, *(head or ()), *after)


def _tn_matmul(a, b, *, name, bm, after=(), tk=2048):
    t, m = a.shape
    n = b.shape[1]
    tk = min(tk, t)
    nk = t // tk

    def body(a_ref, b_ref, o_ref):
        @pl.when(pl.program_id(1) == 0)
        def _():
            o_ref[...] = jnp.zeros_like(o_ref)

        o_ref[...] += lax.dot_general(a_ref[...], b_ref[...], TN_DIMS, preferred_element_type=F32)

    return pl.pallas_call(
        _behind(body, 2, after), name=name, grid=(m // bm, nk),
        in_specs=[pl.BlockSpec((tk, bm), lambda i, k: (k, i)), pl.BlockSpec((tk, n), lambda i, k: (k, 0))]
        + [ANY] * len(after),
        out_specs=pl.BlockSpec((bm, n), lambda i, k: (i, 0)),
        out_shape=jax.ShapeDtypeStruct((m, n), F32),
        compiler_params=_params(("parallel", "arbitrary"), VMEM_LIMIT),
    )(a, b, *after)


def _rope_tables(t):
    half = ROT_DIM // 2
    inv_freq = ROPE_THETA ** (-jnp.arange(0, ROT_DIM, 2, dtype=F32) / ROT_DIM)
    ang = jnp.arange(t, dtype=F32)[:, None] * inv_freq[None, :]
    cos8, sin8 = jnp.cos(ang), jnp.sin(ang)
    d = jnp.arange(128) % HEAD_DIM
    cos, sin = jnp.zeros((t, 128), F32), jnp.zeros((t, 128), F32)
    for k in range(half):
        cos = jnp.where(d % half == k, cos8[:, k:k + 1], cos)
        sin = jnp.where(d % half == k, sin8[:, k:k + 1], sin)
    mult = jnp.where(d < ROT_DIM, cos, 1.0)
    from_lo = jnp.where((d >= half) & (d < ROT_DIM), sin, 0.0)
    from_hi = jnp.where(d < half, -sin, 0.0)
    return jnp.stack([mult, from_lo, from_hi])


def _tile_lanes(tab, width):
    return jnp.tile(tab, (1, width // tab.shape[1]))


def _rope(v, tab):
    w = v.shape[1]
    half_rot = ROT_DIM // 2
    return (v * _tile_lanes(tab[0], w)
            + pltpu.roll(v, half_rot, axis=1) * _tile_lanes(tab[1], w)
            + pltpu.roll(v, w - half_rot, axis=1) * _tile_lanes(tab[2], w))


def _rope_bwd(dv, tab):
    w = dv.shape[1]
    half_rot = ROT_DIM // 2
    return (dv * _tile_lanes(tab[0], w)
            + pltpu.roll(dv * _tile_lanes(tab[1], w), w - half_rot, axis=1)
            + pltpu.roll(dv * _tile_lanes(tab[2], w), half_rot, axis=1))


def _shift_rows(v, prev8_ref, n):
    r = lax.broadcasted_iota(jnp.int32, v.shape, 0)
    rolled = pltpu.roll(v, n, axis=0)
    last = prev8_ref[7:8, :]
    if n == 1:
        return jnp.where(r >= 1, rolled, last)
    return jnp.where(r >= 2, rolled, jnp.where(r == 0, prev8_ref[6:7, :], last))


def _shift_rows_up(v, next8_ref, n):
    rows = v.shape[0]
    r = lax.broadcasted_iota(jnp.int32, v.shape, 0)
    rolled = pltpu.roll(v, rows - n, axis=0)
    first = next8_ref[0:1, :]
    if n == 1:
        return jnp.where(r <= rows - 2, rolled, first)
    return jnp.where(r <= rows - 3, rolled, jnp.where(r == rows - 2, first, next8_ref[1:2, :]))


def _lane_half_mask(shape, half):
    lane = lax.broadcasted_iota(jnp.int32, shape, 1)
    return (lane >= HEAD_DIM) if half else (lane < HEAD_DIM)


def _to_kv_lanes(chunk, head, kv):
    if head % 2 != kv:
        chunk = pltpu.roll(chunk, HEAD_DIM, axis=1)
    return jnp.where(_lane_half_mask(chunk.shape, kv), chunk, 0.0)


def _from_kv_lanes(chunk, head, kv):
    chunk = jnp.where(_lane_half_mask(chunk.shape, kv), chunk, 0.0)
    if head % 2 != kv:
        chunk = pltpu.roll(chunk, HEAD_DIM, axis=1)
    return chunk


def _stack_heads(wide):
    parts = []
    for head in range(N_Q_HEADS):
        chunk = wide[:, (head // 2) * 128:(head // 2 + 1) * 128]
        parts.append(_to_kv_lanes(chunk, head, head // Q_PER_KV))
    return jnp.concatenate(parts, axis=0)


def _window_mask(has_prev):
    shape = (N_Q_HEADS * BLOCK, 2 * BLOCK)
    qi = lax.broadcasted_iota(jnp.int32, shape, 0) & (BLOCK - 1)
    kj = lax.broadcasted_iota(jnp.int32, shape, 1)
    first_key = BLOCK - has_prev * BLOCK
    in_prev = (kj < BLOCK) & (kj > qi) & (kj >= first_key)
    in_own = (kj >= BLOCK) & ((kj - BLOCK) <= qi)
    return in_prev | in_own


def _sink_column(sink_ref):
    row = lax.broadcasted_iota(jnp.int32, (N_Q_HEADS * BLOCK, 1), 0)
    col = jnp.full((N_Q_HEADS * BLOCK, 1), sink_ref[0, 0], F32)
    for head in range(1, N_Q_HEADS):
        col = jnp.where(row >= head * BLOCK, sink_ref[0, head], col)
    return col


def _softmax_with_sink(q4, k2, mask, sink):
    s = lax.dot_general(q4, k2, NT_DIMS, preferred_element_type=F32) * SM_SCALE
    s = jnp.where(mask, s, MASK_VALUE)
    m = jnp.maximum(jnp.max(s, axis=-1, keepdims=True), sink)
    p = jnp.exp(s - m)
    e_sink = jnp.exp(sink - m)
    inv_den = 1.0 / (jnp.sum(p, axis=-1, keepdims=True) + e_sink)
    return p * inv_den, e_sink * inv_den


def _conv_terms(zf, prev8_ref, w_ref):
    b_gate, c_gate, u = zf[:, 0:CONV_W], zf[:, CONV_W:2 * CONV_W], zf[:, 2 * CONV_W:3 * CONV_W]
    vc = c_gate * u
    vm1 = _shift_rows(vc, prev8_ref, 1)
    vm2 = _shift_rows(vc, prev8_ref, 2)
    conv = w_ref[0:1, :] * vm2 + w_ref[1:2, :] * vm1 + w_ref[2:3, :] * vc
    return b_gate, c_gate, u, vc, vm1, vm2, conv


def _mixer_fwd(x, gain, win_t, wout, conv_w, sinks, rope, *, name, after=(), tq=512):
    t = x.shape[0]
    tq = min(tq, t)
    nblk = tq // BLOCK

    def body(x_ref, g_ref, win_hbm, wout_hbm, cw_ref, sink_ref, rope_ref,
             xo_ref, h_ref, z_ref, y_ref, kprev_ref, vprev_ref, cprev_ref, win_ref, wout_ref, sems):
        i = pl.program_id(0)
        _load_resident(win_hbm, win_ref, sems.at[0])
        _load_resident(wout_hbm, wout_ref, sems.at[1])

        @pl.when(i == 0)
        def _():
            kprev_ref[...] = jnp.zeros_like(kprev_ref)
            vprev_ref[...] = jnp.zeros_like(vprev_ref)
            cprev_ref[...] = jnp.zeros_like(cprev_ref)

        xf = x_ref[...]
        xhat, _ = _rms_stats(xf)
        h = (xhat * g_ref[...]).astype(BF16)
        h_ref[...] = h
        zb = lax.dot_general(h, win_ref[...], NT_DIMS, preferred_element_type=F32).astype(BF16)
        z_ref[...] = zb
        zf = zb.astype(F32)

        b_gate, _, _, vc, _, _, conv = _conv_terms(zf, cprev_ref, cw_ref)
        y_conv = b_gate * conv
        cprev_ref[...] = vc[tq - 8:tq, :]

        tab = rope_ref[...]
        qr = _rope(zf[:, Q_OFF:K_OFF], tab)
        kr = _rope(zf[:, K_OFF:V_OFF], tab).astype(BF16)
        vb = zb[:, V_OFF:Z_W]

        y_attn = []
        for j in range(nblk):
            rows = slice(j * BLOCK, (j + 1) * BLOCK)
            prev = slice((j - 1) * BLOCK, j * BLOCK)
            k2 = jnp.concatenate([kprev_ref[...] if j == 0 else kr[prev], kr[rows]], axis=0)
            v2 = jnp.concatenate([vprev_ref[...] if j == 0 else vb[prev], vb[rows]], axis=0)
            mask = _window_mask(jnp.minimum(i, 1) if j == 0 else 1)
            q8 = _stack_heads(qr[rows]).astype(BF16)
            probs, _ = _softmax_with_sink(q8, k2, mask, _sink_column(sink_ref))
            o8 = jnp.dot(probs.astype(BF16), v2, preferred_element_type=F32)
            chunks = [jnp.zeros((BLOCK, 128), F32) for _ in range(ATTN_W // 128)]
            for head in range(N_Q_HEADS):
                chunks[head // 2] += _from_kv_lanes(o8[head * BLOCK:(head + 1) * BLOCK], head, head // Q_PER_KV)
            y_attn.append(jnp.concatenate(chunks, axis=1))
        kprev_ref[...] = kr[tq - BLOCK:tq]
        vprev_ref[...] = vb[tq - BLOCK:tq]
        y = jnp.concatenate([y_conv, jnp.concatenate(y_attn, axis=0)], axis=1).astype(BF16)
        y_ref[...] = y
        xo_ref[...] = xf + jnp.dot(y, wout_ref[...], preferred_element_type=F32)

    row = pl.BlockSpec((tq, D_MODEL), lambda i: (i, 0))
    full = lambda shape: pl.BlockSpec(shape, lambda i: (0,) * len(shape))
    return pl.pallas_call(
        _behind(body, 7, after), name=name, grid=(t // tq,),
        in_specs=[row, full((1, D_MODEL)), ANY, ANY, full((3, CONV_W)),
                  pl.BlockSpec(memory_space=pltpu.SMEM), pl.BlockSpec((3, tq, 128), lambda i: (0, i, 0))]
        + [ANY] * len(after),
        out_specs=[row, row, pl.BlockSpec((tq, Z_W), lambda i: (i, 0)), row],
        out_shape=[jax.ShapeDtypeStruct((t, D_MODEL), F32), jax.ShapeDtypeStruct((t, D_MODEL), BF16),
                   jax.ShapeDtypeStruct((t, Z_W), BF16), jax.ShapeDtypeStruct((t, D_MODEL), BF16)],
        scratch_shapes=[pltpu.VMEM((BLOCK, KV_W), BF16), pltpu.VMEM((BLOCK, KV_W), BF16),
                        pltpu.VMEM((8, CONV_W), F32), pltpu.VMEM((Z_W, D_MODEL), BF16),
                        pltpu.VMEM((D_MODEL, D_MODEL), BF16), pltpu.SemaphoreType.DMA((2,))],
        compiler_params=_params(("arbitrary",), VMEM_LIMIT),
    )(x, gain, win_t, wout, conv_w, sinks, rope, *after)


def _mixer_bwd(dxo, x, gain, y, z, win_t, wout, conv_w, sinks, rope, *, name, after=(), tq=256):
    t = x.shape[0]
    tq = min(tq, t)
    nt, nblk = t // tq, tq // BLOCK

    def body(dxo_ref, x_ref, g_ref, y_ref, z_ref, zp_ref, win_hbm, wout_hbm, cw_ref, sink_ref, rope_ref, ropep_ref,
             dxi_ref, dz_ref, gb_ref, dcw_ref, dsink_ref, dg_ref, dk_ref, dv_ref, dcn_ref, pvc_ref,
             win_ref, wout_ref, sems):
        i = pl.program_id(0)
        tile = nt - 1 - i
        _load_resident(win_hbm, win_ref, sems.at[0])
        _load_resident(wout_hbm, wout_ref, sems.at[1])

        @pl.when(i == 0)
        def _():
            dk_ref[...] = jnp.zeros_like(dk_ref)
            dv_ref[...] = jnp.zeros_like(dv_ref)
            dcn_ref[...] = jnp.zeros_like(dcn_ref)
            dcw_ref[...] = jnp.zeros_like(dcw_ref)
            dsink_ref[...] = jnp.zeros_like(dsink_ref)
            dg_ref[...] = jnp.zeros_like(dg_ref)

        has_prev = jnp.minimum(tile, 1)
        go = dxo_ref[...]
        gb = go.astype(BF16)
        gb_ref[...] = gb
        dy = lax.dot_general(gb, wout_ref[...], NT_DIMS, preferred_element_type=F32)
        dy_conv, dy_attn = dy[:, 0:CONV_W], dy[:, CONV_W:D_MODEL]
        zb, zpb = z_ref[...], zp_ref[...]
        zf = zb.astype(F32)
        zpf = zpb.astype(F32) * has_prev.astype(F32)

        pvc_ref[...] = (zpf[:, CONV_W:2 * CONV_W] * zpf[:, 2 * CONV_W:3 * CONV_W])[BLOCK - 8:BLOCK, :]
        b_gate, c_gate, u, vc, vm1, vm2, conv = _conv_terms(zf, pvc_ref, cw_ref)
        d_bgate = dy_conv * conv
        dc = dy_conv * b_gate
        tap = lax.broadcasted_iota(jnp.int32, (8, CONV_W), 0)
        dcw_ref[...] += jnp.where(tap == 0, jnp.sum(dc * vm2, axis=0, keepdims=True),
                                  jnp.where(tap == 1, jnp.sum(dc * vm1, axis=0, keepdims=True),
                                            jnp.where(tap == 2, jnp.sum(dc * vc, axis=0, keepdims=True), 0.0)))
        dvc = (cw_ref[2:3, :] * dc + cw_ref[1:2, :] * _shift_rows_up(dc, dcn_ref, 1)
               + cw_ref[0:1, :] * _shift_rows_up(dc, dcn_ref, 2))
        dcn_ref[...] = dc[0:8, :]
        d_cgate = dvc * u
        d_u = dvc * c_gate

        tab, tabp = rope_ref[...], ropep_ref[...]
        qr = _rope(zf[:, Q_OFF:K_OFF], tab)
        kr = _rope(zf[:, K_OFF:V_OFF], tab).astype(BF16)
        kpr = _rope(zpf[:, K_OFF:V_OFF], tabp).astype(BF16)
        vb, vpb = zb[:, V_OFF:Z_W], zpb[:, V_OFF:Z_W]
        out = y_ref[:, CONV_W:D_MODEL].astype(F32)
        do_out = dy_attn * out
        lane = lax.broadcasted_iota(jnp.int32, (1, 128), 1)
        dsink = jnp.zeros((1, 128), F32)
        dk_next, dv_next = dk_ref[...], dv_ref[...]
        dq_rows, dk_rows, dv_rows = [None] * nblk, [None] * nblk, [None] * nblk
        for j in reversed(range(nblk)):
            rows = slice(j * BLOCK, (j + 1) * BLOCK)
            prev = slice((j - 1) * BLOCK, j * BLOCK)
            k2 = jnp.concatenate([kpr if j == 0 else kr[prev], kr[rows]], axis=0)
            v2 = jnp.concatenate([vpb if j == 0 else vb[prev], vb[rows]], axis=0)
            mask = _window_mask(has_prev if j == 0 else 1)
            q8 = _stack_heads(qr[rows]).astype(BF16)
            do8 = _stack_heads(dy_attn[rows]).astype(BF16)
            delta = jnp.sum(_stack_heads(do_out[rows]), axis=-1, keepdims=True)
            probs, p_sink = _softmax_with_sink(q8, k2, mask, _sink_column(sink_ref))
            dp = lax.dot_general(do8, v2, NT_DIMS, preferred_element_type=F32)
            ds = (probs * (dp - delta) * SM_SCALE).astype(BF16)
            dq8 = jnp.dot(ds, k2, preferred_element_type=F32)
            dk2 = lax.dot_general(ds, q8, TN_DIMS, preferred_element_type=F32)
            dv2 = lax.dot_general(probs.astype(BF16), do8, TN_DIMS, preferred_element_type=F32)
            sink_terms = p_sink * delta
            dq_chunks = [jnp.zeros((BLOCK, 128), F32) for _ in range(ATTN_W // 128)]
            for head in range(N_Q_HEADS):
                grp = slice(head * BLOCK, (head + 1) * BLOCK)
                dq_chunks[head // 2] += _from_kv_lanes(dq8[grp], head, head // Q_PER_KV)
                dsink = dsink - jnp.where(lane == head, jnp.sum(sink_terms[grp], axis=0, keepdims=True), 0.0)
            dq_rows[j] = jnp.concatenate(dq_chunks, axis=1)
            dk_rows[j] = dk2[BLOCK:] + dk_next
            dv_rows[j] = dv2[BLOCK:] + dv_next
            dk_next, dv_next = dk2[:BLOCK], dv2[:BLOCK]
        dk_ref[...] = dk_next
        dv_ref[...] = dv_next
        dsink_ref[...] += dsink
        dq = _rope_bwd(jnp.concatenate(dq_rows, axis=0), tab)
        dk = _rope_bwd(jnp.concatenate(dk_rows, axis=0), tab)
        dv = jnp.concatenate(dv_rows, axis=0)

        dzb = jnp.concatenate([d_bgate, d_cgate, d_u, dq, dk, dv], axis=1).astype(BF16)
        dz_ref[...] = dzb
        dh = jnp.dot(dzb, win_ref[...], preferred_element_type=F32)
        xhat, inv = _rms_stats(x_ref[...])
        dx, dgain = _rms_bwd(dh, xhat, inv, g_ref[...])
        dxi_ref[...] = go + dx
        dg_ref[...] += dgain

    rev = lambda i: (nt - 1 - i, 0)
    block_before = lambda i: jnp.maximum((nt - 1 - i) * nblk - 1, 0)
    row = pl.BlockSpec((tq, D_MODEL), rev)
    full = lambda shape: pl.BlockSpec(shape, lambda i: (0,) * len(shape))
    return pl.pallas_call(
        _behind(body, 12, after), name=name, grid=(nt,),
        in_specs=[row, row, full((1, D_MODEL)), row,
                  pl.BlockSpec((tq, Z_W), rev), pl.BlockSpec((BLOCK, Z_W), lambda i: (block_before(i), 0)),
                  ANY, ANY, full((3, CONV_W)),
                  pl.BlockSpec(memory_space=pltpu.SMEM),
                  pl.BlockSpec((3, tq, 128), lambda i: (0, nt - 1 - i, 0)),
                  pl.BlockSpec((3, BLOCK, 128), lambda i: (0, block_before(i), 0))] + [ANY] * len(after),
        out_specs=[row, pl.BlockSpec((tq, Z_W), rev), row, full((8, CONV_W)), full((1, 128)), full((1, D_MODEL))],
        out_shape=[jax.ShapeDtypeStruct((t, D_MODEL), F32), jax.ShapeDtypeStruct((t, Z_W), BF16),
                   jax.ShapeDtypeStruct((t, D_MODEL), BF16), jax.ShapeDtypeStruct((8, CONV_W), F32),
                   jax.ShapeDtypeStruct((1, 128), F32), jax.ShapeDtypeStruct((1, D_MODEL), F32)],
        scratch_shapes=[pltpu.VMEM((BLOCK, KV_W), F32), pltpu.VMEM((BLOCK, KV_W), F32), pltpu.VMEM((8, CONV_W), F32),
                        pltpu.VMEM((8, CONV_W), F32), pltpu.VMEM((Z_W, D_MODEL), BF16),
                        pltpu.VMEM((D_MODEL, D_MODEL), BF16), pltpu.SemaphoreType.DMA((2,))],
        compiler_params=_params(("arbitrary",), VMEM_LIMIT),
    )(dxo, x, gain, y, z, z, win_t, wout, conv_w, sinks, rope, rope, *after)


def _place():
    x, y, c = lax.axis_index("x"), lax.axis_index("y"), lax.axis_index("c")
    other_chips = [(1 - x, y), (x, 1 - y), (1 - x, 1 - y)]
    return x, y, c, other_chips


def _all_gather_rows(shards, place=(), *, name):
    n, p = len(shards), len(place)

    def body(*refs):
        srcs, place_srcs = refs[:n], refs[n:n + p]
        outs, place_outs = refs[n + p:2 * n + p], refs[2 * n + p:2 * (n + p)]
        send_sems, recv_sems, local_sems = refs[2 * (n + p):]
        x, y, c, chips = _place()
        me, sibling = (x, y, c), (x, y, 1 - c)

        def rows(t, px, py, pc):
            r = srcs[t].shape[-2]
            start = pl.multiple_of((4 * px + 2 * py + pc) * r, 16 if r % 16 == 0 else 8)
            if len(srcs[t].shape) == 3:
                return outs[t].at[:, pl.ds(start, r), :]
            return outs[t].at[pl.ds(start, r), :]

        def copy(t, k, block, to, own=False):
            return pltpu.make_async_remote_copy(
                src_ref=srcs[t] if own else rows(t, *block), dst_ref=rows(t, *block),
                send_sem=send_sems.at[t, k], recv_sem=recv_sems.at[t, k], device_id=to, device_id_type=MESH)

        mine = [pltpu.make_async_copy(srcs[t], rows(t, *me), local_sems.at[t]) for t in range(n)]
        mine += [pltpu.make_async_copy(place_srcs[q],
                                       _block_rows(place_outs[q], place_srcs[q].shape[-2], 4 * x + 2 * y + c),
                                       local_sems.at[n + q]) for q in range(p)]
        for q in range(p):
            mine[n + q].start()
        first = []
        for t in range(n):
            mine[t].start()
            first.append(copy(t, 0, me, sibling, own=True))
            first += [copy(t, 1 + j, me, (*chip, c), own=True) for j, chip in enumerate(chips)]
        for cp in first:
            cp.start()
        passed = []
        for j, chip in enumerate(chips):
            for t in range(n):
                copy(t, 1 + j, (*chip, c), me).wait_recv()
                fwd = copy(t, 4 + j, (*chip, c), sibling)
                fwd.start()
                passed.append(fwd)
        for t in range(n):
            copy(t, 0, sibling, me).wait_recv()
            for j, chip in enumerate(chips):
                copy(t, 4 + j, (*chip, 1 - c), me).wait_recv()
        for cp in first + passed:
            cp.wait_send()
        for cp in mine:
            cp.wait()

    out_shape = [jax.ShapeDtypeStruct(s.shape[:-2] + (N_DEV * s.shape[-2], s.shape[-1]), s.dtype)
                 for s in list(shards) + list(place)]
    res = pl.pallas_call(
        body, name=name, in_specs=[ANY] * (n + p), out_specs=[ANY] * (n + p), out_shape=out_shape,
        scratch_shapes=[pltpu.SemaphoreType.DMA((n, 7)), pltpu.SemaphoreType.DMA((n, 7)),
                        pltpu.SemaphoreType.DMA((n + p,))],
    )(*shards, *place)
    return res[:n], res[n:]


def _split_start(bufs, n_copies, plan, *, name, after=()):
    n = len(bufs)

    def body(*refs):
        token = refs[-1]
        for cp in plan(refs[:n], refs[n], refs[n + 1]):
            cp.start()
        token[...] = jnp.zeros_like(token)

    res = pl.pallas_call(
        _behind(body, n, after), name=name, in_specs=[HBM_SPEC] * n + [ANY] * len(after),
        out_specs=(SEM_SPEC, SEM_SPEC, *[HBM_SPEC] * n, pl.BlockSpec(memory_space=pltpu.VMEM)),
        out_shape=(pltpu.SemaphoreType.DMA((n_copies,)), pltpu.SemaphoreType.DMA((n_copies,)),
                   *[pltpu.HBM(b.shape, b.dtype) for b in bufs], jax.ShapeDtypeStruct((8, 128), F32)),
        input_output_aliases={i: 2 + i for i in range(n)},
        compiler_params=pltpu.CompilerParams(has_side_effects=DATAFLOW),
    )(*[pltpu.with_memory_space_constraint(b, pltpu.HBM) for b in bufs], *after)
    return res[0], res[1], list(res[2:2 + n]), res[-1]


def _split_wait(send_sems, recv_sems, bufs, after, plan, *, name):
    n = len(bufs)

    def body(*refs):
        for cp in plan(refs[:n], refs[n], refs[n + 1]):
            cp.wait_send()
            cp.wait_recv()

    return list(pl.pallas_call(
        body, name=name, in_specs=[HBM_SPEC] * n + [SEM_SPEC, SEM_SPEC, ANY], out_specs=[HBM_SPEC] * n,
        out_shape=tuple(pltpu.HBM(b.shape, b.dtype) for b in bufs),
        input_output_aliases={i: i for i in range(n)},
        compiler_params=pltpu.CompilerParams(has_side_effects=DATAFLOW),
    )(*bufs, send_sems, recv_sems, after))


def _sibling_plan(n):
    def plan(bufs, send_sems, recv_sems):
        x, y, c, _ = _place()
        return [pltpu.make_async_remote_copy(
            src_ref=bufs[t].at[:, 1 - c], dst_ref=bufs[n + t], send_sem=send_sems.at[t], recv_sem=recv_sems.at[t],
            device_id=(x, y, 1 - c), device_id_type=MESH) for t in range(n)]
    return plan


def _block_rows(ref, r, blk):
    start = pl.multiple_of(blk * r, 16 if r % 16 == 0 else 8)
    return ref.at[(slice(None),) * (len(ref.shape) - 2) + (pl.ds(start, r), slice(None))]


def _remote(src, dst, send_sems, recv_sems, k, peer):
    return pltpu.make_async_remote_copy(src_ref=src, dst_ref=dst, send_sem=send_sems.at[k], recv_sem=recv_sems.at[k],
                                        device_id=peer, device_id_type=MESH)


def _gather_send_plan(n):
    def plan(bufs, send_sems, recv_sems):
        x, y, c, chips = _place()
        peers = [(x, y, 1 - c)] + [(px, py, c) for px, py in chips]
        copies = []
        for t in range(n):
            dst = _block_rows(bufs[n + t], bufs[t].shape[-2], 4 * x + 2 * y + c)
            copies += [_remote(bufs[t], dst, send_sems, recv_sems, 4 * t + k, peer) for k, peer in enumerate(peers)]
        return copies
    return plan


def _gather_forward_plan(rows):
    def plan(bufs, send_sems, recv_sems):
        x, y, c, chips = _place()
        copies = []
        for t, r in enumerate(rows):
            for j, (px, py) in enumerate(chips):
                blk = _block_rows(bufs[t], r, 4 * px + 2 * py + c)
                copies.append(_remote(blk, blk, send_sems, recv_sems, 3 * t + j, (x, y, 1 - c)))
        return copies
    return plan


def _chips_plan(n, with_small):
    def plan(bufs, send_sems, recv_sems):
        x, y, c, chips = _place()
        copies = []
        for t in range(n):
            for j, (px, py) in enumerate(chips):
                copies.append(_remote(bufs[t].at[2 * px + py], bufs[n + t].at[j], send_sems, recv_sems, 3 * t + j,
                                      (px, py, c)))
        if with_small:
            mine = _block_rows(bufs[2 * n], 8, 4 * x + 2 * y + c)
            flips = [(fx, fy, fc) for fx in range(2) for fy in range(2) for fc in range(2)][1:]
            for k, (fx, fy, fc) in enumerate(flips):
                peer = (x + fx - 2 * x * fx, y + fy - 2 * y * fy, c + fc - 2 * c * fc)
                copies.append(_remote(mine, mine, send_sems, recv_sems, 3 * n + k, peer))
        return copies
    return plan


def _place_own(fulls, shards, index, *, name):
    n = len(fulls)

    def body(index_ref, *refs):
        for t in range(n):
            refs[2 * n + t][...] = refs[n + t][...]

    def block_of(shard):
        lead = len(shard.shape) - 2
        return pl.BlockSpec(shard.shape, lambda i, index_ref: (0,) * lead + (index_ref[0], 0))

    def whole(shard):
        return pl.BlockSpec(shard.shape, lambda i, index_ref: (0,) * len(shard.shape))

    return list(pl.pallas_call(
        body, name=name,
        grid_spec=pltpu.PrefetchScalarGridSpec(
            num_scalar_prefetch=1, grid=(1,),
            in_specs=[ANY] * n + [whole(s) for s in shards], out_specs=[block_of(s) for s in shards]),
        out_shape=[jax.ShapeDtypeStruct(f.shape, f.dtype) for f in fulls],
        input_output_aliases={1 + t: t for t in range(n)},
        compiler_params=_params(("arbitrary",)),
    )(index, *fulls, *shards))


def _add_sibling(grad, recv, core, *, name, tr):
    rows = grad.shape[2]

    def body(core_ref, g_ref, r_ref, o_ref, ob_ref):
        p = g_ref[:, 0] + r_ref[...]
        o_ref[...] = p
        ob_ref[...] = p.astype(BF16)

    out = pl.BlockSpec((4, tr, D_MODEL), lambda i, core_ref: (0, i, 0))
    return pl.pallas_call(
        body, name=name,
        grid_spec=pltpu.PrefetchScalarGridSpec(
            num_scalar_prefetch=1, grid=(rows // tr,),
            in_specs=[pl.BlockSpec((4, 1, tr, D_MODEL), lambda i, core_ref: (0, core_ref[0], i, 0)), out],
            out_specs=[out, out]),
        out_shape=[jax.ShapeDtypeStruct(recv.shape, F32), jax.ShapeDtypeStruct(recv.shape, BF16)],
        compiler_params=_params(("arbitrary",)),
    )(core, grad, recv)


def _reduce_adamw(partial, recv, chip, w, m, v, *, name, tr, after=()):
    rows = partial.shape[1]

    def body(chip_ref, p_ref, r_ref, w_ref, m_ref, v_ref, g_ref, d_ref, mo_ref, vo_ref):
        g = p_ref[0] + r_ref[0].astype(F32) + r_ref[1].astype(F32) + r_ref[2].astype(F32)
        g_ref[...] = g
        d_ref[...], mo_ref[...], vo_ref[...] = _adamw_math(w_ref[...], g, m_ref[...], v_ref[...])

    spec = pl.BlockSpec((tr, D_MODEL), lambda i, chip_ref: (i, 0))
    return pl.pallas_call(
        _behind(body, 6, after), name=name,
        grid_spec=pltpu.PrefetchScalarGridSpec(
            num_scalar_prefetch=1, grid=(rows // tr,),
            in_specs=[pl.BlockSpec((1, tr, D_MODEL), lambda i, chip_ref: (chip_ref[0], i, 0)),
                      pl.BlockSpec((3, tr, D_MODEL), lambda i, chip_ref: (0, i, 0)), spec, spec, spec]
            + [ANY] * len(after),
            out_specs=[spec] * 4),
        out_shape=[jax.ShapeDtypeStruct((rows, D_MODEL), F32)] * 4,
        compiler_params=_params(("arbitrary",)),
    )(chip, partial, recv, w, m, v, *after)


def _adamw_math(w, g, m, v):
    m = ADAM_B1 * m + (1.0 - ADAM_B1) * g
    v = ADAM_B2 * v + (1.0 - ADAM_B2) * (g * g)
    m_hat = m / (1.0 - ADAM_B1 ** ADAM_STEP)
    v_hat = v / (1.0 - ADAM_B2 ** ADAM_STEP)
    delta = -ADAM_LR * (m_hat / (jnp.sqrt(v_hat) + ADAM_EPS) + ADAM_WD * w)
    return delta, m, v


def _adamw(w, g, m, v, *, name, tr, after=()):
    rows, cols = w.shape

    def body(w_ref, g_ref, m_ref, v_ref, d_ref, mo_ref, vo_ref):
        d_ref[...], mo_ref[...], vo_ref[...] = _adamw_math(w_ref[...], g_ref[...], m_ref[...], v_ref[...])

    spec = pl.BlockSpec((tr, cols), lambda i: (i, 0))
    return pl.pallas_call(
        _behind(body, 4, after), name=name, grid=(rows // tr,), in_specs=[spec] * 4 + [ANY] * len(after),
        out_specs=[spec] * 3, out_shape=[jax.ShapeDtypeStruct(w.shape, F32)] * 3,
        compiler_params=_params(("parallel",)),
    )(w, g, m, v, *after)


def _sum_small(gathered, *, name):
    def body(g_ref, o_ref):
        acc = g_ref[0]
        for k in range(1, N_DEV):
            acc = acc + g_ref[k]
        o_ref[...] = acc

    return pl.pallas_call(body, name=name, out_shape=jax.ShapeDtypeStruct(gathered.shape[1:], F32))(gathered)


def kernel(x, ffn1_norm, ffn1_w_gate, ffn1_w_up, ffn1_w_down, mix_norm, w_in, conv_w, attn_sinks, w_out, ffn2_norm, ffn2_w_gate, ffn2_w_up, ffn2_w_down, final_norm, loss_target, m_ffn1_norm, m_ffn1_w_gate, m_ffn1_w_up, m_ffn1_w_down, m_mix_norm, m_w_in, m_conv_w, m_attn_sinks, m_w_out, m_ffn2_norm, m_ffn2_w_gate, m_ffn2_w_up, m_ffn2_w_down, m_final_norm, v_ffn1_norm, v_ffn1_w_gate, v_ffn1_w_up, v_ffn1_w_down, v_mix_norm, v_w_in, v_conv_w, v_attn_sinks, v_w_out, v_ffn2_norm, v_ffn2_w_gate, v_ffn2_w_up, v_ffn2_w_down, v_final_norm):
    ix, iy, ic = lax.axis_index("x"), lax.axis_index("y"), lax.axis_index("c")
    my_index = 4 * ix + 2 * iy + ic
    core = ic.astype(jnp.int32).reshape(1)
    chip = (2 * ix + iy).astype(jnp.int32).reshape(1)

    given = dict(ffn1_norm=ffn1_norm, ffn1_w_gate=ffn1_w_gate, ffn1_w_up=ffn1_w_up, ffn1_w_down=ffn1_w_down,
                 mix_norm=mix_norm, w_in=w_in, conv_w=conv_w, attn_sinks=attn_sinks, w_out=w_out, ffn2_norm=ffn2_norm,
                 ffn2_w_gate=ffn2_w_gate, ffn2_w_up=ffn2_w_up, ffn2_w_down=ffn2_w_down, final_norm=final_norm)
    moments_m = dict(ffn1_norm=m_ffn1_norm, ffn1_w_gate=m_ffn1_w_gate, ffn1_w_up=m_ffn1_w_up, ffn1_w_down=m_ffn1_w_down,
                     mix_norm=m_mix_norm, w_in=m_w_in, conv_w=m_conv_w, attn_sinks=m_attn_sinks, w_out=m_w_out,
                     ffn2_norm=m_ffn2_norm, ffn2_w_gate=m_ffn2_w_gate, ffn2_w_up=m_ffn2_w_up, ffn2_w_down=m_ffn2_w_down,
                     final_norm=m_final_norm)
    moments_v = dict(ffn1_norm=v_ffn1_norm, ffn1_w_gate=v_ffn1_w_gate, ffn1_w_up=v_ffn1_w_up, ffn1_w_down=v_ffn1_w_down,
                     mix_norm=v_mix_norm, w_in=v_w_in, conv_w=v_conv_w, attn_sinks=v_attn_sinks, w_out=v_w_out,
                     ffn2_norm=v_ffn2_norm, ffn2_w_gate=v_ffn2_w_gate, ffn2_w_up=v_ffn2_w_up, ffn2_w_down=v_ffn2_w_down,
                     final_norm=v_final_norm)

    xs = x[0]
    target = loss_target[0]
    final_gain = final_norm.reshape(1, D_MODEL)

    def ffn_shard(wg, wu, wd):
        return jnp.stack([wg[0].T, wu[0].T, wd[0]]).astype(BF16)

    conv_cols = conv_w.shape[2]
    conv_shard = jnp.pad(conv_w[0], ((0, 5), (0, 128 - conv_cols)))
    rest_shards = [ffn_shard(ffn2_w_gate, ffn2_w_up, ffn2_w_down), w_in[0].T.astype(BF16), w_out[0].astype(BF16),
                   conv_shard]
    rest_rows = [s.shape[-2] for s in rest_shards]
    n_rest = len(rest_shards)
    (w1,), _ = _all_gather_rows([ffn_shard(ffn1_w_gate, ffn1_w_up, ffn1_w_down)], name="gather_ffn1")

    fulls = [lax.empty(s.shape[:-2] + (N_DEV * s.shape[-2], s.shape[-1]), s.dtype) for s in rest_shards]
    fulls = _place_own(fulls, rest_shards, my_index.astype(jnp.int32).reshape(1), name="place_own_weights")
    send_plan = _gather_send_plan(n_rest)
    ssem, rsem, bufs, token = _split_start(rest_shards + list(fulls), 4 * n_rest, send_plan, name="gather_rest_start",
                                           after=[w1])
    x1, h1, s1, sa1, sb1 = _ffn_fwd(xs, ffn1_norm, w1, name="ffn1_fwd", after=[token])
    bufs = _split_wait(ssem, rsem, bufs, x1, send_plan, name="gather_rest_wait")
    w2_part, mixer_parts = bufs[n_rest], bufs[n_rest + 1:]
    fwd_mixer = _gather_forward_plan(rest_rows[1:])
    ssem, rsem, bufs, token = _split_start(mixer_parts, 3 * (n_rest - 1), fwd_mixer, name="forward_mixer_start")
    win_t, wout, conv_all = _split_wait(ssem, rsem, bufs, token, fwd_mixer, name="forward_mixer_wait")
    conv_full = conv_all.reshape(N_DEV, 8, 128)[:, :3, :conv_cols].transpose(1, 0, 2).reshape(3, CONV_W)
    fwd_ffn2 = _gather_forward_plan(rest_rows[:1])
    ssem, rsem, bufs, token = _split_start([w2_part], 3, fwd_ffn2, name="forward_ffn2_start", after=[win_t])
    rope = _rope_tables(xs.shape[0])
    x2, hm, z, y = _mixer_fwd(x1, mix_norm, win_t, wout, conv_full, attn_sinks, rope, name="mixer_fwd", after=[token])
    (w2,) = _split_wait(ssem, rsem, bufs, x2, fwd_ffn2, name="forward_ffn2_wait")
    x3, h2, s2, sa2, sb2 = _ffn_fwd(x2, ffn2_norm, w2, name="ffn2_fwd")

    def to_sibling_start(grads, tag, after=()):
        views = [g.reshape(4, 2, g.shape[0] // N_DEV, D_MODEL) for g in grads]
        lands = [lax.empty((4,) + v.shape[2:], F32) for v in views]
        plan = _sibling_plan(len(views))
        ssem, rsem, bufs, token = _split_start(views + lands, len(views), plan, name=f"{tag}_sibling_start", after=after)
        return (ssem, rsem, bufs, plan, tag), token

    def to_sibling_finish(handle, after, names):
        ssem, rsem, bufs, plan, tag = handle
        bufs = _split_wait(ssem, rsem, bufs, after, plan, name=f"{tag}_sibling_wait")
        n = len(names)
        return [_add_sibling(v, r, core, name=f"add_sibling_{nm}", tr=v.shape[2] // 2)
                for v, r, nm in zip(bufs[:n], bufs[n:], names)]

    def to_chips_start(partials, tag, small_all=None, after=()):
        p16 = [p for _, p in partials]
        lands = [lax.empty((3,) + p.shape[1:], BF16) for p in p16]
        extra = [] if small_all is None else [small_all]
        plan = _chips_plan(len(p16), small_all is not None)
        ssem, rsem, bufs, token = _split_start(p16 + lands + extra, 3 * len(p16) + 7 * len(extra), plan,
                                               name=f"{tag}_chips_start", after=after)
        return (ssem, rsem, bufs, plan, tag), token

    def to_chips_finish(handle, partials, after, names):
        ssem, rsem, bufs, plan, tag = handle
        bufs = _split_wait(ssem, rsem, bufs, after, plan, name=f"{tag}_chips_wait")
        n = len(names)
        return [(p32, r) for (p32, _), r in zip(partials, bufs[n:2 * n])], bufs[2 * n:]

    half_ff = D_FF // 2
    names2, namesm = ["ffn2_w_gate", "ffn2_w_up", "ffn2_w_down"], ["w_in", "w_out"]
    transposed = {"ffn1_w_gate", "ffn1_w_up", "w_in", "ffn2_w_gate", "ffn2_w_up"}
    grad, delta, new_m, new_v = {}, {}, {}, {}

    def adam_big(nm, parts, after=()):
        to_rows = (lambda a: a[0].T) if nm in transposed else (lambda a: a[0])
        from_rows = (lambda a: a.T[None]) if nm in transposed else (lambda a: a[None])
        p32, recv = parts
        outs = _reduce_adamw(p32, recv, chip, to_rows(given[nm]), to_rows(moments_m[nm]), to_rows(moments_v[nm]),
                             name=f"adamw_{nm}", tr=p32.shape[1] // 2, after=after)
        grad[nm], delta[nm], new_m[nm], new_v[nm] = (from_rows(a) for a in outs)

    dx2, da2, db2, g2b, d_norm2, loss_local, d_final = _ffn_dgrad(
        x3, x2, ffn2_norm, sa2, sb2, w2, head=(final_gain, target), name="ffn2_dgrad")
    gw2 = [_tn_matmul(da2, h2, name="ffn2_wgrad_gate", bm=half_ff), _tn_matmul(db2, h2, name="ffn2_wgrad_up", bm=half_ff),
           _tn_matmul(s2, g2b, name="ffn2_wgrad_down", bm=half_ff)]
    sib2, tok = to_sibling_start(gw2, "ffn2")
    dx1, dz, gmb, d_conv, d_sink, d_normm = _mixer_bwd(dx2, x1, mix_norm, y, z, win_t, wout, conv_full, attn_sinks,
                                                       rope, name="mixer_bwd", after=[tok])
    p2 = to_sibling_finish(sib2, dx1, names2)
    chips2, tok = to_chips_start(p2, "ffn2")
    gwm = [_tn_matmul(dz, hm, name="mixer_wgrad_in", bm=Z_W // 3, after=[tok]),
           _tn_matmul(y, gmb, name="mixer_wgrad_out", bm=D_MODEL // 2, after=[tok])]
    sibm, tok = to_sibling_start(gwm, "mixer")
    dx0, da1, db1, g1b, d_norm1 = _ffn_dgrad(dx1, xs, ffn1_norm, sa1, sb1, w1, name="ffn1_dgrad", after=[tok], tm=512)
    r2, _ = to_chips_finish(chips2, p2, dx0, names2)
    pm = to_sibling_finish(sibm, dx0, namesm)
    chipsm, tok = to_chips_start(pm, "mixer")
    gw_gate = _tn_matmul(da1, h1, name="ffn1_wgrad_gate", bm=half_ff, after=[tok])
    sib_gate, tok = to_sibling_start([gw_gate], "ffn1_gate")
    gw_up = _tn_matmul(db1, h1, name="ffn1_wgrad_up", bm=half_ff, after=[tok])
    rm, _ = to_chips_finish(chipsm, pm, gw_up, namesm)
    p_gate = to_sibling_finish(sib_gate, gw_up, ["ffn1_w_gate"])
    chips_gate, tok_a = to_chips_start(p_gate, "ffn1_gate")
    sib_up, tok_b = to_sibling_start([gw_up], "ffn1_up", after=[tok_a])
    gw_down = _tn_matmul(s1, g1b, name="ffn1_wgrad_down", bm=half_ff, after=[tok_a, tok_b])
    p_up = to_sibling_finish(sib_up, gw_down, ["ffn1_w_up"])
    chips_up, tok_a = to_chips_start(p_up, "ffn1_up")
    sib_down, tok_b = to_sibling_start([gw_down], "ffn1_down", after=[tok_a])
    p_down = to_sibling_finish(sib_down, tok_b, ["ffn1_w_down"])
    last_row = (jnp.pad(d_sink, ((0, 0), (0, D_MODEL - 128)))
                + jnp.pad(loss_local, ((0, 0), (LOSS_LANE, D_MODEL - LOSS_LANE - 1))))
    small = jnp.concatenate([
        d_norm1, d_normm, d_norm2, d_final, jnp.pad(d_conv[0:3], ((0, 0), (0, D_MODEL - CONV_W))), last_row], axis=0)
    (small_all,) = _place_own([lax.empty((N_DEV * 8, D_MODEL), F32)], [small], my_index.astype(jnp.int32).reshape(1),
                              name="place_own_small")
    chips_down, tok = to_chips_start(p_down, "ffn1_down", small_all)
    behind = [tok]
    for nm, g in zip(names2 + namesm, r2 + rm):
        adam_big(nm, g, after=behind)
        behind = [new_v[nm]]
    r_gate, _ = to_chips_finish(chips_gate, p_gate, new_v["w_out"], ["ffn1_w_gate"])
    adam_big("ffn1_w_gate", r_gate[0])
    r_up, _ = to_chips_finish(chips_up, p_up, new_v["ffn1_w_gate"], ["ffn1_w_up"])
    adam_big("ffn1_w_up", r_up[0])
    r_down, (small_all,) = to_chips_finish(chips_down, p_down, new_v["ffn1_w_up"], ["ffn1_w_down"])
    adam_big("ffn1_w_down", r_down[0])
    small_sum = _sum_small(small_all.reshape(N_DEV, 8, D_MODEL), name="sum_small")
    loss = small_sum[7, LOSS_LANE]
    _update_small(given, moments_m, moments_v, small_sum, my_index, grad, delta, new_m, new_v)

    order = list(given)
    return (loss, dx0[None], *[grad[n] for n in order], *[delta[n] for n in order],
            *[new_m[n] for n in order], *[new_v[n] for n in order])


def _update_small(given, moments_m, moments_v, small_sum, my_index, grad, delta, new_m, new_v):
    conv_cols = given["conv_w"].shape[2]
    small_g = {
        "ffn1_norm": small_sum[0:1], "mix_norm": small_sum[1:2], "ffn2_norm": small_sum[2:3],
        "final_norm": small_sum[3:4],
        "conv_w": lax.dynamic_slice(small_sum[4:7, :CONV_W], (0, my_index * conv_cols), (3, conv_cols)),
        "attn_sinks": small_sum[7:8, :N_Q_HEADS],
    }

    small_names = ["ffn1_norm", "mix_norm", "ffn2_norm", "final_norm", "conv_w", "attn_sinks"]

    def pack(parts):
        rows = []
        for nm in small_names:
            p = parts[nm]
            p2 = p.reshape(3, conv_cols) if nm == "conv_w" else p.reshape(1, -1)
            rows.append(jnp.pad(p2, ((0, 0), (0, D_MODEL - p2.shape[1]))))
        rows.append(jnp.zeros((8, D_MODEL), F32))
        return jnp.concatenate(rows, axis=0)

    sd, sm, sv = _adamw(pack(given), pack(small_g), pack(moments_m), pack(moments_v), name="adamw_small", tr=16)
    row = 0
    for nm in small_names:
        shape = given[nm].shape
        nrow = 3 if nm == "conv_w" else 1
        ncol = conv_cols if nm == "conv_w" else given[nm].size
        grad[nm] = small_g[nm].reshape(shape)
        delta[nm], new_m[nm], new_v[nm] = (a[row:row + nrow, :ncol].reshape(shape) for a in (sd, sm, sv))
        row += nrow
```

```python
import functools

import jax
import jax.numpy as jnp
from jax import lax
from jax.experimental import pallas as pl
from jax.experimental.pallas import tpu as pltpu

F32 = jnp.float32
BF16 = jnp.bfloat16
MESH = pl.DeviceIdType.MESH
ANY = pl.BlockSpec(memory_space=pl.ANY)
HBM_SPEC = pl.BlockSpec(memory_space=pltpu.HBM)
SEM_SPEC = pl.BlockSpec(memory_space=pltpu.SEMAPHORE)
DATAFLOW = pltpu.SideEffectType.DATAFLOW_SIDE_EFFECTING

N_DEV = 8
LOSS_LANE = 128
D_MODEL = 1024
D_FF = 2816
CONV_W = 512
ATTN_W = 512
KV_W = 128
HEAD_DIM = 64
N_Q_HEADS = 8
N_KV_HEADS = 2
Q_PER_KV = N_Q_HEADS // N_KV_HEADS
BLOCK = 128
ROT_DIM = 16
ROPE_THETA = 500000.0
Z_W = 3 * CONV_W + ATTN_W + 2 * KV_W
Q_OFF = 3 * CONV_W
K_OFF = Q_OFF + ATTN_W
V_OFF = K_OFF + KV_W
RMS_EPS = 1e-5
MASK_VALUE = -1e30
SM_SCALE = HEAD_DIM ** -0.5
FFN_RES_SCALE = 0.5

ADAM_LR = 0.001
ADAM_B1 = 0.9
ADAM_B2 = 0.999
ADAM_EPS = 1e-08
ADAM_WD = 0.01
ADAM_STEP = 10

NT_DIMS = (((1,), (1,)), ((), ()))
TN_DIMS = (((0,), (0,)), ((), ()))

VMEM_LIMIT = 62 * 1024 * 1024
FF_CHUNK = 256


def _params(sem, vmem=None):
    return pltpu.CompilerParams(dimension_semantics=sem, vmem_limit_bytes=vmem)


def _behind(body, n_in, after):
    k = len(after)
    if k == 0:
        return body
    return lambda *refs: body(*refs[:n_in], *refs[n_in + k:])


def _rms_stats(xf):
    inv = lax.rsqrt(jnp.mean(xf * xf, axis=-1, keepdims=True) + RMS_EPS)
    return xf * inv, inv


def _rms_bwd(dh, xhat, inv, gain):
    dxhat = dh * gain
    dx = inv * (dxhat - xhat * jnp.mean(dxhat * xhat, axis=-1, keepdims=True))
    dgain = jnp.sum(dh * xhat, axis=0, keepdims=True)
    return dx, dgain


def _load_resident(w_hbm, w_ref, sem):
    @pl.when(pl.program_id(0) == 0)
    def _():
        cp = pltpu.make_async_copy(w_hbm, w_ref, sem)
        cp.start()
        cp.wait()


def _ffn_fwd(x, gain, w3, *, name, after=(), tm=256, tf=FF_CHUNK):
    t = x.shape[0]
    tm = min(tm, t)

    def body(x_ref, g_ref, w_hbm, xo_ref, h_ref, s_ref, sa_ref, sb_ref, w_ref, sem):
        _load_resident(w_hbm, w_ref, sem)
        xf = x_ref[...]
        xhat, _ = _rms_stats(xf)
        h = (xhat * g_ref[...]).astype(BF16)
        h_ref[...] = h
        for c in range(0, D_FF, tf):
            cols = slice(c, min(c + tf, D_FF))
            a = lax.dot_general(h, w_ref[0, cols, :], NT_DIMS, preferred_element_type=F32)
            b = lax.dot_general(h, w_ref[1, cols, :], NT_DIMS, preferred_element_type=F32)
            sig = jax.nn.sigmoid(a)
            silu = a * sig
            s_ref[:, cols] = (silu * b).astype(BF16)
            sa_ref[:, cols] = (b * (sig * (1.0 + a * (1.0 - sig)))).astype(BF16)
            sb_ref[:, cols] = silu.astype(BF16)
        xo_ref[...] = xf + FFN_RES_SCALE * jnp.dot(s_ref[...], w_ref[2], preferred_element_type=F32)

    row = pl.BlockSpec((tm, D_MODEL), lambda i: (i, 0))
    hid = pl.BlockSpec((tm, D_FF), lambda i: (i, 0))
    return pl.pallas_call(
        _behind(body, 3, after), name=name, grid=(t // tm,),
        in_specs=[row, pl.BlockSpec((1, D_MODEL), lambda i: (0, 0)), ANY] + [ANY] * len(after),
        out_specs=[row, row, hid, hid, hid],
        out_shape=[jax.ShapeDtypeStruct((t, D_MODEL), F32), jax.ShapeDtypeStruct((t, D_MODEL), BF16)]
        + [jax.ShapeDtypeStruct((t, D_FF), BF16)] * 3,
        scratch_shapes=[pltpu.VMEM((3, D_FF, D_MODEL), BF16), pltpu.SemaphoreType.DMA(())],
        compiler_params=_params(("arbitrary",), VMEM_LIMIT),
    )(x, gain, w3, *after)


def _ffn_dgrad(dxo, x, gain, sa, sb, w3, *, name, head=None, after=(), tm=256, tf=FF_CHUNK):
    t = x.shape[0]
    tm = min(tm, t)
    n_head = 0 if head is None else 2

    def body(*refs):
        dxo_ref, x_ref, g_ref, sa_ref, sb_ref, w_hbm = refs[:6]
        head_refs = refs[6:6 + n_head]
        dxi_ref, da_ref, db_ref, gb_ref, dg_ref = refs[6 + n_head:11 + n_head]
        head_outs = refs[11 + n_head:11 + 2 * n_head]
        w_ref, sem = refs[11 + 2 * n_head:]
        _load_resident(w_hbm, w_ref, sem)

        @pl.when(pl.program_id(0) == 0)
        def _():
            dg_ref[...] = jnp.zeros_like(dg_ref)
            for ref in head_outs:
                ref[...] = jnp.zeros_like(ref)

        if head is None:
            go = dxo_ref[...]
        else:
            fg_ref, t_ref = head_refs
            loss_ref, dfg_ref = head_outs
            xhat_o, inv_o = _rms_stats(dxo_ref[...])
            err = xhat_o * fg_ref[...] - t_ref[...]
            loss_ref[...] += 0.5 * jnp.sum(jnp.mean(err * err, axis=-1, keepdims=True), axis=0, keepdims=True)
            go, dfg = _rms_bwd(err * (1.0 / D_MODEL), xhat_o, inv_o, fg_ref[...])
            dfg_ref[...] += dfg
        gb = (FFN_RES_SCALE * go).astype(BF16)
        gb_ref[...] = gb
        for c in range(0, D_FF, tf):
            cols = slice(c, min(c + tf, D_FF))
            ds = lax.dot_general(gb, w_ref[2, cols, :], NT_DIMS, preferred_element_type=F32)
            da_ref[:, cols] = (ds * sa_ref[:, cols].astype(F32)).astype(BF16)
            db_ref[:, cols] = (ds * sb_ref[:, cols].astype(F32)).astype(BF16)
        dh = (jnp.dot(da_ref[...], w_ref[0], preferred_element_type=F32)
              + jnp.dot(db_ref[...], w_ref[1], preferred_element_type=F32))
        xhat, inv = _rms_stats(x_ref[...])
        dx, dgain = _rms_bwd(dh, xhat, inv, g_ref[...])
        dxi_ref[...] = go + dx
        dg_ref[...] += dgain

    row = pl.BlockSpec((tm, D_MODEL), lambda i: (i, 0))
    hid = pl.BlockSpec((tm, D_FF), lambda i: (i, 0))
    vec = pl.BlockSpec((1, D_MODEL), lambda i: (0, 0))
    head_in = [] if head is None else [vec, row]
    head_out = [] if head is None else [pl.BlockSpec((1, 1), lambda i: (0, 0)), vec]
    head_shape = [] if head is None else [jax.ShapeDtypeStruct((1, 1), F32), jax.ShapeDtypeStruct((1, D_MODEL), F32)]
    return pl.pallas_call(
        _behind(body, 6 + n_head, after), name=name, grid=(t // tm,),
        in_specs=[row, row, vec, hid, hid, ANY] + head_in + [ANY] * len(after),
        out_specs=[row, hid, hid, row, vec] + head_out,
        out_shape=[jax.ShapeDtypeStruct((t, D_MODEL), F32), jax.ShapeDtypeStruct((t, D_FF), BF16),
                   jax.ShapeDtypeStruct((t, D_FF), BF16),
                   jax.ShapeDtypeStruct((t, D_MODEL), BF16), jax.ShapeDtypeStruct((1, D_MODEL), F32)] + head_shape,
        scratch_shapes=[pltpu.VMEM((3, D_FF, D_MODEL), BF16), pltpu.SemaphoreType.DMA(())],
        compiler_params=_params(("arbitrary",), VMEM_LIMIT),
    )(dxo, x, gain, sa, sb, w3, *(head or ()), *after)


def _tn_matmul(a, b, *, name, bm, after=(), tk=2048):
    t, m = a.shape
    n = b.shape[1]
    tk = min(tk, t)
    nk = t // tk

    def body(a_ref, b_ref, o_ref):
        @pl.when(pl.program_id(1) == 0)
        def _():
            o_ref[...] = jnp.zeros_like(o_ref)

        o_ref[...] += lax.dot_general(a_ref[...], b_ref[...], TN_DIMS, preferred_element_type=F32)

    return pl.pallas_call(
        _behind(body, 2, after), name=name, grid=(m // bm, nk),
        in_specs=[pl.BlockSpec((tk, bm), lambda i, k: (k, i)), pl.BlockSpec((tk, n), lambda i, k: (k, 0))]
        + [ANY] * len(after),
        out_specs=pl.BlockSpec((bm, n), lambda i, k: (i, 0)),
        out_shape=jax.ShapeDtypeStruct((m, n), F32),
        compiler_params=_params(("parallel", "arbitrary"), VMEM_LIMIT),
    )(a, b, *after)


def _rope_tables(t):
    inv_freq = ROPE_THETA ** (-jnp.arange(0, ROT_DIM, 2, dtype=F32) / ROT_DIM)
    ang = inv_freq[:, None] * jnp.arange(t, dtype=F32)[None, :]
    return jnp.stack([jnp.tile(jnp.cos(ang).T, (1, 16)), jnp.tile(jnp.sin(ang).T, (1, 16))])


def _rope_multipliers(cos_sin):
    half = ROT_DIM // 2
    cos, sin = cos_sin[0], cos_sin[1]
    d = lax.broadcasted_iota(jnp.int32, cos.shape, 1) & (HEAD_DIM - 1)
    mult = jnp.where(d < ROT_DIM, cos, 1.0)
    from_lo = jnp.where((d >= half) & (d < ROT_DIM), sin, 0.0)
    from_hi = jnp.where(d < half, -sin, 0.0)
    return mult, from_lo, from_hi


def _tile_lanes(tab, width):
    return jnp.tile(tab, (1, width // tab.shape[1]))


def _rope(v, tab):
    w = v.shape[1]
    half_rot = ROT_DIM // 2
    return (v * _tile_lanes(tab[0], w)
            + pltpu.roll(v, half_rot, axis=1) * _tile_lanes(tab[1], w)
            + pltpu.roll(v, w - half_rot, axis=1) * _tile_lanes(tab[2], w))


def _rope_bwd(dv, tab):
    w = dv.shape[1]
    half_rot = ROT_DIM // 2
    return (dv * _tile_lanes(tab[0], w)
            + pltpu.roll(dv * _tile_lanes(tab[1], w), w - half_rot, axis=1)
            + pltpu.roll(dv * _tile_lanes(tab[2], w), half_rot, axis=1))


def _shift_rows(v, prev8_ref, n):
    r = lax.broadcasted_iota(jnp.int32, v.shape, 0)
    rolled = pltpu.roll(v, n, axis=0)
    last = prev8_ref[7:8, :]
    if n == 1:
        return jnp.where(r >= 1, rolled, last)
    return jnp.where(r >= 2, rolled, jnp.where(r == 0, prev8_ref[6:7, :], last))


def _shift_rows_up(v, next8_ref, n):
    rows = v.shape[0]
    r = lax.broadcasted_iota(jnp.int32, v.shape, 0)
    rolled = pltpu.roll(v, rows - n, axis=0)
    first = next8_ref[0:1, :]
    if n == 1:
        return jnp.where(r <= rows - 2, rolled, first)
    return jnp.where(r <= rows - 3, rolled, jnp.where(r == rows - 2, first, next8_ref[1:2, :]))


def _lane_half_mask(shape, half):
    lane = lax.broadcasted_iota(jnp.int32, shape, 1)
    return (lane >= HEAD_DIM) if half else (lane < HEAD_DIM)


def _to_kv_lanes(chunk, head, kv):
    if head % 2 != kv:
        chunk = pltpu.roll(chunk, HEAD_DIM, axis=1)
    return jnp.where(_lane_half_mask(chunk.shape, kv), chunk, 0.0)


def _from_kv_lanes(chunk, head, kv):
    chunk = jnp.where(_lane_half_mask(chunk.shape, kv), chunk, 0.0)
    if head % 2 != kv:
        chunk = pltpu.roll(chunk, HEAD_DIM, axis=1)
    return chunk


def _stack_heads(wide):
    parts = []
    for head in range(N_Q_HEADS):
        chunk = wide[:, (head // 2) * 128:(head // 2 + 1) * 128]
        parts.append(_to_kv_lanes(chunk, head, head // Q_PER_KV))
    return jnp.concatenate(parts, axis=0)


def _window_mask(has_prev):
    shape = (N_Q_HEADS * BLOCK, 2 * BLOCK)
    qi = lax.broadcasted_iota(jnp.int32, shape, 0) & (BLOCK - 1)
    kj = lax.broadcasted_iota(jnp.int32, shape, 1)
    first_key = BLOCK - has_prev * BLOCK
    in_prev = (kj < BLOCK) & (kj > qi) & (kj >= first_key)
    in_own = (kj >= BLOCK) & ((kj - BLOCK) <= qi)
    return in_prev | in_own


def _sink_column(sink_ref):
    row = lax.broadcasted_iota(jnp.int32, (N_Q_HEADS * BLOCK, 1), 0)
    col = jnp.full((N_Q_HEADS * BLOCK, 1), sink_ref[0, 0], F32)
    for head in range(1, N_Q_HEADS):
        col = jnp.where(row >= head * BLOCK, sink_ref[0, head], col)
    return col


def _softmax_with_sink(q4, k2, mask, sink):
    s = lax.dot_general(q4, k2, NT_DIMS, preferred_element_type=F32) * SM_SCALE
    s = jnp.where(mask, s, MASK_VALUE)
    m = jnp.maximum(jnp.max(s, axis=-1, keepdims=True), sink)
    p = jnp.exp(s - m)
    e_sink = jnp.exp(sink - m)
    inv_den = 1.0 / (jnp.sum(p, axis=-1, keepdims=True) + e_sink)
    return p * inv_den, e_sink * inv_den


def _conv_terms(zf, prev8_ref, w_ref):
    b_gate, c_gate, u = zf[:, 0:CONV_W], zf[:, CONV_W:2 * CONV_W], zf[:, 2 * CONV_W:3 * CONV_W]
    vc = c_gate * u
    vm1 = _shift_rows(vc, prev8_ref, 1)
    vm2 = _shift_rows(vc, prev8_ref, 2)
    conv = w_ref[0:1, :] * vm2 + w_ref[1:2, :] * vm1 + w_ref[2:3, :] * vc
    return b_gate, c_gate, u, vc, vm1, vm2, conv


def _mixer_fwd(x, gain, win_t, wout, conv_w, sinks, rope, *, name, after=(), tq=512):
    t = x.shape[0]
    tq = min(tq, t)
    nblk = tq // BLOCK

    def body(x_ref, g_ref, win_hbm, wout_hbm, cw_ref, sink_ref, rope_ref,
             xo_ref, h_ref, z_ref, y_ref, kprev_ref, vprev_ref, cprev_ref, win_ref, wout_ref, sems):
        i = pl.program_id(0)
        _load_resident(win_hbm, win_ref, sems.at[0])
        _load_resident(wout_hbm, wout_ref, sems.at[1])

        @pl.when(i == 0)
        def _():
            kprev_ref[...] = jnp.zeros_like(kprev_ref)
            vprev_ref[...] = jnp.zeros_like(vprev_ref)
            cprev_ref[...] = jnp.zeros_like(cprev_ref)

        xf = x_ref[...]
        xhat, _ = _rms_stats(xf)
        h = (xhat * g_ref[...]).astype(BF16)
        h_ref[...] = h
        zb = lax.dot_general(h, win_ref[...], NT_DIMS, preferred_element_type=F32).astype(BF16)
        z_ref[...] = zb
        zf = zb.astype(F32)

        b_gate, _, _, vc, _, _, conv = _conv_terms(zf, cprev_ref, cw_ref)
        y_conv = b_gate * conv
        cprev_ref[...] = vc[tq - 8:tq, :]

        tab = _rope_multipliers(rope_ref[...])
        qr = _rope(zf[:, Q_OFF:K_OFF], tab)
        kr = _rope(zf[:, K_OFF:V_OFF], tab).astype(BF16)
        vb = zb[:, V_OFF:Z_W]

        y_attn = []
        for j in range(nblk):
            rows = slice(j * BLOCK, (j + 1) * BLOCK)
            prev = slice((j - 1) * BLOCK, j * BLOCK)
            k2 = jnp.concatenate([kprev_ref[...] if j == 0 else kr[prev], kr[rows]], axis=0)
            v2 = jnp.concatenate([vprev_ref[...] if j == 0 else vb[prev], vb[rows]], axis=0)
            mask = _window_mask(jnp.minimum(i, 1) if j == 0 else 1)
            q8 = _stack_heads(qr[rows]).astype(BF16)
            probs, _ = _softmax_with_sink(q8, k2, mask, _sink_column(sink_ref))
            o8 = jnp.dot(probs.astype(BF16), v2, preferred_element_type=F32)
            chunks = [jnp.zeros((BLOCK, 128), F32) for _ in range(ATTN_W // 128)]
            for head in range(N_Q_HEADS):
                chunks[head // 2] += _from_kv_lanes(o8[head * BLOCK:(head + 1) * BLOCK], head, head // Q_PER_KV)
            y_attn.append(jnp.concatenate(chunks, axis=1))
        kprev_ref[...] = kr[tq - BLOCK:tq]
        vprev_ref[...] = vb[tq - BLOCK:tq]
        y = jnp.concatenate([y_conv, jnp.concatenate(y_attn, axis=0)], axis=1).astype(BF16)
        y_ref[...] = y
        xo_ref[...] = xf + jnp.dot(y, wout_ref[...], preferred_element_type=F32)

    row = pl.BlockSpec((tq, D_MODEL), lambda i: (i, 0))
    full = lambda shape: pl.BlockSpec(shape, lambda i: (0,) * len(shape))
    return pl.pallas_call(
        _behind(body, 7, after), name=name, grid=(t // tq,),
        in_specs=[row, full((1, D_MODEL)), ANY, ANY, full((3, CONV_W)),
                  pl.BlockSpec(memory_space=pltpu.SMEM), pl.BlockSpec((2, tq, 128), lambda i: (0, i, 0))]
        + [ANY] * len(after),
        out_specs=[row, row, pl.BlockSpec((tq, Z_W), lambda i: (i, 0)), row],
        out_shape=[jax.ShapeDtypeStruct((t, D_MODEL), F32), jax.ShapeDtypeStruct((t, D_MODEL), BF16),
                   jax.ShapeDtypeStruct((t, Z_W), BF16), jax.ShapeDtypeStruct((t, D_MODEL), BF16)],
        scratch_shapes=[pltpu.VMEM((BLOCK, KV_W), BF16), pltpu.VMEM((BLOCK, KV_W), BF16),
                        pltpu.VMEM((8, CONV_W), F32), pltpu.VMEM((Z_W, D_MODEL), BF16),
                        pltpu.VMEM((D_MODEL, D_MODEL), BF16), pltpu.SemaphoreType.DMA((2,))],
        compiler_params=_params(("arbitrary",), VMEM_LIMIT),
    )(x, gain, win_t, wout, conv_w, sinks, rope, *after)


def _mixer_bwd(dxo, x, gain, y, z, win_t, wout, conv_w, sinks, rope, *, name, after=(), tq=256):
    t = x.shape[0]
    tq = min(tq, t)
    nt, nblk = t // tq, tq // BLOCK

    def body(dxo_ref, x_ref, g_ref, y_ref, z_ref, zp_ref, win_hbm, wout_hbm, cw_ref, sink_ref, rope_ref, ropep_ref,
             dxi_ref, dz_ref, gb_ref, dcw_ref, dsink_ref, dg_ref, dk_ref, dv_ref, dcn_ref, pvc_ref,
             win_ref, wout_ref, sems):
        i = pl.program_id(0)
        tile = nt - 1 - i
        _load_resident(win_hbm, win_ref, sems.at[0])
        _load_resident(wout_hbm, wout_ref, sems.at[1])

        @pl.when(i == 0)
        def _():
            dk_ref[...] = jnp.zeros_like(dk_ref)
            dv_ref[...] = jnp.zeros_like(dv_ref)
            dcn_ref[...] = jnp.zeros_like(dcn_ref)
            dcw_ref[...] = jnp.zeros_like(dcw_ref)
            dsink_ref[...] = jnp.zeros_like(dsink_ref)
            dg_ref[...] = jnp.zeros_like(dg_ref)

        has_prev = jnp.minimum(tile, 1)
        go = dxo_ref[...]
        gb = go.astype(BF16)
        gb_ref[...] = gb
        dy = lax.dot_general(gb, wout_ref[...], NT_DIMS, preferred_element_type=F32)
        dy_conv, dy_attn = dy[:, 0:CONV_W], dy[:, CONV_W:D_MODEL]
        zb, zpb = z_ref[...], zp_ref[...]
        zf = zb.astype(F32)
        zpf = zpb.astype(F32) * has_prev.astype(F32)

        pvc_ref[...] = (zpf[:, CONV_W:2 * CONV_W] * zpf[:, 2 * CONV_W:3 * CONV_W])[BLOCK - 8:BLOCK, :]
        b_gate, c_gate, u, vc, vm1, vm2, conv = _conv_terms(zf, pvc_ref, cw_ref)
        d_bgate = dy_conv * conv
        dc = dy_conv * b_gate
        tap = lax.broadcasted_iota(jnp.int32, (8, CONV_W), 0)
        dcw_ref[...] += jnp.where(tap == 0, jnp.sum(dc * vm2, axis=0, keepdims=True),
                                  jnp.where(tap == 1, jnp.sum(dc * vm1, axis=0, keepdims=True),
                                            jnp.where(tap == 2, jnp.sum(dc * vc, axis=0, keepdims=True), 0.0)))
        dvc = (cw_ref[2:3, :] * dc + cw_ref[1:2, :] * _shift_rows_up(dc, dcn_ref, 1)
               + cw_ref[0:1, :] * _shift_rows_up(dc, dcn_ref, 2))
        dcn_ref[...] = dc[0:8, :]
        d_cgate = dvc * u
        d_u = dvc * c_gate

        tab, tabp = _rope_multipliers(rope_ref[...]), _rope_multipliers(ropep_ref[...])
        qr = _rope(zf[:, Q_OFF:K_OFF], tab)
        kr = _rope(zf[:, K_OFF:V_OFF], tab).astype(BF16)
        kpr = _rope(zpf[:, K_OFF:V_OFF], tabp).astype(BF16)
        vb, vpb = zb[:, V_OFF:Z_W], zpb[:, V_OFF:Z_W]
        out = y_ref[:, CONV_W:D_MODEL].astype(F32)
        do_out = dy_attn * out
        lane = lax.broadcasted_iota(jnp.int32, (1, 128), 1)
        dsink = jnp.zeros((1, 128), F32)
        dk_next, dv_next = dk_ref[...], dv_ref[...]
        dq_rows, dk_rows, dv_rows = [None] * nblk, [None] * nblk, [None] * nblk
        for j in reversed(range(nblk)):
            rows = slice(j * BLOCK, (j + 1) * BLOCK)
            prev = slice((j - 1) * BLOCK, j * BLOCK)
            k2 = jnp.concatenate([kpr if j == 0 else kr[prev], kr[rows]], axis=0)
            v2 = jnp.concatenate([vpb if j == 0 else vb[prev], vb[rows]], axis=0)
            mask = _window_mask(has_prev if j == 0 else 1)
            q8 = _stack_heads(qr[rows]).astype(BF16)
            do8 = _stack_heads(dy_attn[rows]).astype(BF16)
            delta = jnp.sum(_stack_heads(do_out[rows]), axis=-1, keepdims=True)
            probs, p_sink = _softmax_with_sink(q8, k2, mask, _sink_column(sink_ref))
            dp = lax.dot_general(do8, v2, NT_DIMS, preferred_element_type=F32)
            ds = (probs * (dp - delta) * SM_SCALE).astype(BF16)
            dq8 = jnp.dot(ds, k2, preferred_element_type=F32)
            dk2 = lax.dot_general(ds, q8, TN_DIMS, preferred_element_type=F32)
            dv2 = lax.dot_general(probs.astype(BF16), do8, TN_DIMS, preferred_element_type=F32)
            sink_terms = p_sink * delta
            dq_chunks = [jnp.zeros((BLOCK, 128), F32) for _ in range(ATTN_W // 128)]
            for head in range(N_Q_HEADS):
                grp = slice(head * BLOCK, (head + 1) * BLOCK)
                dq_chunks[head // 2] += _from_kv_lanes(dq8[grp], head, head // Q_PER_KV)
                dsink = dsink - jnp.where(lane == head, jnp.sum(sink_terms[grp], axis=0, keepdims=True), 0.0)
            dq_rows[j] = jnp.concatenate(dq_chunks, axis=1)
            dk_rows[j] = dk2[BLOCK:] + dk_next
            dv_rows[j] = dv2[BLOCK:] + dv_next
            dk_next, dv_next = dk2[:BLOCK], dv2[:BLOCK]
        dk_ref[...] = dk_next
        dv_ref[...] = dv_next
        dsink_ref[...] += dsink
        dq = _rope_bwd(jnp.concatenate(dq_rows, axis=0), tab)
        dk = _rope_bwd(jnp.concatenate(dk_rows, axis=0), tab)
        dv = jnp.concatenate(dv_rows, axis=0)

        dzb = jnp.concatenate([d_bgate, d_cgate, d_u, dq, dk, dv], axis=1).astype(BF16)
        dz_ref[...] = dzb
        dh = jnp.dot(dzb, win_ref[...], preferred_element_type=F32)
        xhat, inv = _rms_stats(x_ref[...])
        dx, dgain = _rms_bwd(dh, xhat, inv, g_ref[...])
        dxi_ref[...] = go + dx
        dg_ref[...] += dgain

    rev = lambda i: (nt - 1 - i, 0)
    block_before = lambda i: jnp.maximum((nt - 1 - i) * nblk - 1, 0)
    row = pl.BlockSpec((tq, D_MODEL), rev)
    full = lambda shape: pl.BlockSpec(shape, lambda i: (0,) * len(shape))
    return pl.pallas_call(
        _behind(body, 12, after), name=name, grid=(nt,),
        in_specs=[row, row, full((1, D_MODEL)), row,
                  pl.BlockSpec((tq, Z_W), rev), pl.BlockSpec((BLOCK, Z_W), lambda i: (block_before(i), 0)),
                  ANY, ANY, full((3, CONV_W)),
                  pl.BlockSpec(memory_space=pltpu.SMEM),
                  pl.BlockSpec((2, tq, 128), lambda i: (0, nt - 1 - i, 0)),
                  pl.BlockSpec((2, BLOCK, 128), lambda i: (0, block_before(i), 0))] + [ANY] * len(after),
        out_specs=[row, pl.BlockSpec((tq, Z_W), rev), row, full((8, CONV_W)), full((1, 128)), full((1, D_MODEL))],
        out_shape=[jax.ShapeDtypeStruct((t, D_MODEL), F32), jax.ShapeDtypeStruct((t, Z_W), BF16),
                   jax.ShapeDtypeStruct((t, D_MODEL), BF16), jax.ShapeDtypeStruct((8, CONV_W), F32),
                   jax.ShapeDtypeStruct((1, 128), F32), jax.ShapeDtypeStruct((1, D_MODEL), F32)],
        scratch_shapes=[pltpu.VMEM((BLOCK, KV_W), F32), pltpu.VMEM((BLOCK, KV_W), F32), pltpu.VMEM((8, CONV_W), F32),
                        pltpu.VMEM((8, CONV_W), F32), pltpu.VMEM((Z_W, D_MODEL), BF16),
                        pltpu.VMEM((D_MODEL, D_MODEL), BF16), pltpu.SemaphoreType.DMA((2,))],
        compiler_params=_params(("arbitrary",), VMEM_LIMIT),
    )(dxo, x, gain, y, z, z, win_t, wout, conv_w, sinks, rope, rope, *after)


def _place():
    x, y, c = lax.axis_index("x"), lax.axis_index("y"), lax.axis_index("c")
    other_chips = [(1 - x, y), (x, 1 - y), (1 - x, 1 - y)]
    return x, y, c, other_chips


def _all_gather_rows(shards, place=(), *, name):
    n, p = len(shards), len(place)

    def body(*refs):
        srcs, place_srcs = refs[:n], refs[n:n + p]
        outs, place_outs = refs[n + p:2 * n + p], refs[2 * n + p:2 * (n + p)]
        send_sems, recv_sems, local_sems = refs[2 * (n + p):]
        x, y, c, chips = _place()
        me, sibling = (x, y, c), (x, y, 1 - c)

        def rows(t, px, py, pc):
            r = srcs[t].shape[-2]
            start = pl.multiple_of((4 * px + 2 * py + pc) * r, 16 if r % 16 == 0 else 8)
            if len(srcs[t].shape) == 3:
                return outs[t].at[:, pl.ds(start, r), :]
            return outs[t].at[pl.ds(start, r), :]

        def copy(t, k, block, to, own=False):
            return pltpu.make_async_remote_copy(
                src_ref=srcs[t] if own else rows(t, *block), dst_ref=rows(t, *block),
                send_sem=send_sems.at[t, k], recv_sem=recv_sems.at[t, k], device_id=to, device_id_type=MESH)

        mine = [pltpu.make_async_copy(srcs[t], rows(t, *me), local_sems.at[t]) for t in range(n)]
        mine += [pltpu.make_async_copy(place_srcs[q],
                                       _block_rows(place_outs[q], place_srcs[q].shape[-2], 4 * x + 2 * y + c),
                                       local_sems.at[n + q]) for q in range(p)]
        for q in range(p):
            mine[n + q].start()
        first = []
        for t in range(n):
            mine[t].start()
            first.append(copy(t, 0, me, sibling, own=True))
            first += [copy(t, 1 + j, me, (*chip, c), own=True) for j, chip in enumerate(chips)]
        for cp in first:
            cp.start()
        passed = []
        for j, chip in enumerate(chips):
            for t in range(n):
                copy(t, 1 + j, (*chip, c), me).wait_recv()
                fwd = copy(t, 4 + j, (*chip, c), sibling)
                fwd.start()
                passed.append(fwd)
        for t in range(n):
            copy(t, 0, sibling, me).wait_recv()
            for j, chip in enumerate(chips):
                copy(t, 4 + j, (*chip, 1 - c), me).wait_recv()
        for cp in first + passed:
            cp.wait_send()
        for cp in mine:
            cp.wait()

    out_shape = [jax.ShapeDtypeStruct(s.shape[:-2] + (N_DEV * s.shape[-2], s.shape[-1]), s.dtype)
                 for s in list(shards) + list(place)]
    res = pl.pallas_call(
        body, name=name, in_specs=[ANY] * (n + p), out_specs=[ANY] * (n + p), out_shape=out_shape,
        scratch_shapes=[pltpu.SemaphoreType.DMA((n, 7)), pltpu.SemaphoreType.DMA((n, 7)),
                        pltpu.SemaphoreType.DMA((n + p,))],
    )(*shards, *place)
    return res[:n], res[n:]


def _split_start(bufs, n_copies, plan, *, name, after=()):
    n = len(bufs)

    def body(*refs):
        token = refs[-1]
        for cp in plan(refs[:n], refs[n], refs[n + 1]):
            cp.start()
        token[...] = jnp.zeros_like(token)

    res = pl.pallas_call(
        _behind(body, n, after), name=name, in_specs=[HBM_SPEC] * n + [ANY] * len(after),
        out_specs=(SEM_SPEC, SEM_SPEC, *[HBM_SPEC] * n, pl.BlockSpec(memory_space=pltpu.VMEM)),
        out_shape=(pltpu.SemaphoreType.DMA((n_copies,)), pltpu.SemaphoreType.DMA((n_copies,)),
                   *[pltpu.HBM(b.shape, b.dtype) for b in bufs], jax.ShapeDtypeStruct((8, 128), F32)),
        input_output_aliases={i: 2 + i for i in range(n)},
        compiler_params=pltpu.CompilerParams(has_side_effects=DATAFLOW),
    )(*[pltpu.with_memory_space_constraint(b, pltpu.HBM) for b in bufs], *after)
    return res[0], res[1], list(res[2:2 + n]), res[-1]


def _split_wait(send_sems, recv_sems, bufs, after, plan, *, name):
    n = len(bufs)

    def body(*refs):
        for cp in plan(refs[:n], refs[n], refs[n + 1]):
            cp.wait_send()
            cp.wait_recv()

    return list(pl.pallas_call(
        body, name=name, in_specs=[HBM_SPEC] * n + [SEM_SPEC, SEM_SPEC, ANY], out_specs=[HBM_SPEC] * n,
        out_shape=tuple(pltpu.HBM(b.shape, b.dtype) for b in bufs),
        input_output_aliases={i: i for i in range(n)},
        compiler_params=pltpu.CompilerParams(has_side_effects=DATAFLOW),
    )(*bufs, send_sems, recv_sems, after))


def _sibling_plan(n):
    def plan(bufs, send_sems, recv_sems):
        x, y, c, _ = _place()
        return [pltpu.make_async_remote_copy(
            src_ref=bufs[t].at[:, 1 - c], dst_ref=bufs[n + t], send_sem=send_sems.at[t], recv_sem=recv_sems.at[t],
            device_id=(x, y, 1 - c), device_id_type=MESH) for t in range(n)]
    return plan


def _block_rows(ref, r, blk):
    start = pl.multiple_of(blk * r, 16 if r % 16 == 0 else 8)
    return ref.at[(slice(None),) * (len(ref.shape) - 2) + (pl.ds(start, r), slice(None))]


def _remote(src, dst, send_sems, recv_sems, k, peer):
    return pltpu.make_async_remote_copy(src_ref=src, dst_ref=dst, send_sem=send_sems.at[k], recv_sem=recv_sems.at[k],
                                        device_id=peer, device_id_type=MESH)


def _gather_send_plan(n):
    def plan(bufs, send_sems, recv_sems):
        x, y, c, chips = _place()
        peers = [(x, y, 1 - c)] + [(px, py, c) for px, py in chips]
        copies = []
        for t in range(n):
            dst = _block_rows(bufs[n + t], bufs[t].shape[-2], 4 * x + 2 * y + c)
            copies += [_remote(bufs[t], dst, send_sems, recv_sems, 4 * t + k, peer) for k, peer in enumerate(peers)]
        return copies
    return plan


def _gather_forward_plan(rows):
    def plan(bufs, send_sems, recv_sems):
        x, y, c, chips = _place()
        copies = []
        for t, r in enumerate(rows):
            for j, (px, py) in enumerate(chips):
                blk = _block_rows(bufs[t], r, 4 * px + 2 * py + c)
                copies.append(_remote(blk, blk, send_sems, recv_sems, 3 * t + j, (x, y, 1 - c)))
        return copies
    return plan


def _chips_plan(n, with_small):
    def plan(bufs, send_sems, recv_sems):
        x, y, c, chips = _place()
        copies = []
        for t in range(n):
            for j, (px, py) in enumerate(chips):
                copies.append(_remote(bufs[t].at[2 * px + py], bufs[n + t].at[j], send_sems, recv_sems, 3 * t + j,
                                      (px, py, c)))
        if with_small:
            mine = _block_rows(bufs[2 * n], 8, 4 * x + 2 * y + c)
            flips = [(fx, fy, fc) for fx in range(2) for fy in range(2) for fc in range(2)][1:]
            for k, (fx, fy, fc) in enumerate(flips):
                peer = (x + fx - 2 * x * fx, y + fy - 2 * y * fy, c + fc - 2 * c * fc)
                copies.append(_remote(mine, mine, send_sems, recv_sems, 3 * n + k, peer))
        return copies
    return plan


def _place_own(fulls, shards, index, *, name):
    n = len(fulls)

    def body(index_ref, *refs):
        for t in range(n):
            refs[2 * n + t][...] = refs[n + t][...]

    def block_of(shard):
        lead = len(shard.shape) - 2
        return pl.BlockSpec(shard.shape, lambda i, index_ref: (0,) * lead + (index_ref[0], 0))

    def whole(shard):
        return pl.BlockSpec(shard.shape, lambda i, index_ref: (0,) * len(shard.shape))

    return list(pl.pallas_call(
        body, name=name,
        grid_spec=pltpu.PrefetchScalarGridSpec(
            num_scalar_prefetch=1, grid=(1,),
            in_specs=[ANY] * n + [whole(s) for s in shards], out_specs=[block_of(s) for s in shards]),
        out_shape=[jax.ShapeDtypeStruct(f.shape, f.dtype) for f in fulls],
        input_output_aliases={1 + t: t for t in range(n)},
        compiler_params=_params(("arbitrary",)),
    )(index, *fulls, *shards))


def _add_sibling(grad, recv, core, *, name, tr):
    rows = grad.shape[2]

    def body(core_ref, g_ref, r_ref, o_ref, ob_ref):
        p = g_ref[:, 0] + r_ref[...]
        o_ref[...] = p
        ob_ref[...] = p.astype(BF16)

    out = pl.BlockSpec((4, tr, D_MODEL), lambda i, core_ref: (0, i, 0))
    return pl.pallas_call(
        body, name=name,
        grid_spec=pltpu.PrefetchScalarGridSpec(
            num_scalar_prefetch=1, grid=(rows // tr,),
            in_specs=[pl.BlockSpec((4, 1, tr, D_MODEL), lambda i, core_ref: (0, core_ref[0], i, 0)), out],
            out_specs=[out, out]),
        out_shape=[jax.ShapeDtypeStruct(recv.shape, F32), jax.ShapeDtypeStruct(recv.shape, BF16)],
        compiler_params=_params(("arbitrary",)),
    )(core, grad, recv)


def _reduce_adamw(partial, recv, chip, w, m, v, *, name, tr, after=()):
    rows = partial.shape[1]

    def body(chip_ref, p_ref, r_ref, w_ref, m_ref, v_ref, g_ref, d_ref, mo_ref, vo_ref, token_ref):
        g = p_ref[0] + r_ref[0].astype(F32) + r_ref[1].astype(F32) + r_ref[2].astype(F32)
        g_ref[...] = g
        d_ref[...], mo_ref[...], vo_ref[...] = _adamw_math(w_ref[...], g, m_ref[...], v_ref[...])
        token_ref[...] = jnp.zeros_like(token_ref)

    spec = pl.BlockSpec((tr, D_MODEL), lambda i, chip_ref: (i, 0))
    return pl.pallas_call(
        _behind(body, 6, after), name=name,
        grid_spec=pltpu.PrefetchScalarGridSpec(
            num_scalar_prefetch=1, grid=(rows // tr,),
            in_specs=[pl.BlockSpec((1, tr, D_MODEL), lambda i, chip_ref: (chip_ref[0], i, 0)),
                      pl.BlockSpec((3, tr, D_MODEL), lambda i, chip_ref: (0, i, 0)), spec, spec, spec]
            + [ANY] * len(after),
            out_specs=[spec] * 4 + [pl.BlockSpec((8, 128), lambda i, chip_ref: (0, 0))]),
        out_shape=[jax.ShapeDtypeStruct((rows, D_MODEL), F32)] * 4 + [jax.ShapeDtypeStruct((8, 128), F32)],
        compiler_params=_params(("arbitrary",)),
    )(chip, partial, recv, w, m, v, *after)


def _adamw_math(w, g, m, v):
    m = ADAM_B1 * m + (1.0 - ADAM_B1) * g
    v = ADAM_B2 * v + (1.0 - ADAM_B2) * (g * g)
    m_hat = m / (1.0 - ADAM_B1 ** ADAM_STEP)
    v_hat = v / (1.0 - ADAM_B2 ** ADAM_STEP)
    delta = -ADAM_LR * (m_hat / (jnp.sqrt(v_hat) + ADAM_EPS) + ADAM_WD * w)
    return delta, m, v


def _adamw(w, g, m, v, *, name, tr, after=()):
    rows, cols = w.shape

    def body(w_ref, g_ref, m_ref, v_ref, d_ref, mo_ref, vo_ref):
        d_ref[...], mo_ref[...], vo_ref[...] = _adamw_math(w_ref[...], g_ref[...], m_ref[...], v_ref[...])

    spec = pl.BlockSpec((tr, cols), lambda i: (i, 0))
    return pl.pallas_call(
        _behind(body, 4, after), name=name, grid=(rows // tr,), in_specs=[spec] * 4 + [ANY] * len(after),
        out_specs=[spec] * 3, out_shape=[jax.ShapeDtypeStruct(w.shape, F32)] * 3,
        compiler_params=_params(("parallel",)),
    )(w, g, m, v, *after)


def _sum_small(gathered, *, name):
    def body(g_ref, o_ref):
        acc = g_ref[0]
        for k in range(1, N_DEV):
            acc = acc + g_ref[k]
        o_ref[...] = acc

    return pl.pallas_call(body, name=name, out_shape=jax.ShapeDtypeStruct(gathered.shape[1:], F32))(gathered)


def kernel(x, ffn1_norm, ffn1_w_gate, ffn1_w_up, ffn1_w_down, mix_norm, w_in, conv_w, attn_sinks, w_out, ffn2_norm, ffn2_w_gate, ffn2_w_up, ffn2_w_down, final_norm, loss_target, m_ffn1_norm, m_ffn1_w_gate, m_ffn1_w_up, m_ffn1_w_down, m_mix_norm, m_w_in, m_conv_w, m_attn_sinks, m_w_out, m_ffn2_norm, m_ffn2_w_gate, m_ffn2_w_up, m_ffn2_w_down, m_final_norm, v_ffn1_norm, v_ffn1_w_gate, v_ffn1_w_up, v_ffn1_w_down, v_mix_norm, v_w_in, v_conv_w, v_attn_sinks, v_w_out, v_ffn2_norm, v_ffn2_w_gate, v_ffn2_w_up, v_ffn2_w_down, v_final_norm):
    ix, iy, ic = lax.axis_index("x"), lax.axis_index("y"), lax.axis_index("c")
    my_index = 4 * ix + 2 * iy + ic
    core = ic.astype(jnp.int32).reshape(1)
    chip = (2 * ix + iy).astype(jnp.int32).reshape(1)

    given = dict(ffn1_norm=ffn1_norm, ffn1_w_gate=ffn1_w_gate, ffn1_w_up=ffn1_w_up, ffn1_w_down=ffn1_w_down,
                 mix_norm=mix_norm, w_in=w_in, conv_w=conv_w, attn_sinks=attn_sinks, w_out=w_out, ffn2_norm=ffn2_norm,
                 ffn2_w_gate=ffn2_w_gate, ffn2_w_up=ffn2_w_up, ffn2_w_down=ffn2_w_down, final_norm=final_norm)
    moments_m = dict(ffn1_norm=m_ffn1_norm, ffn1_w_gate=m_ffn1_w_gate, ffn1_w_up=m_ffn1_w_up, ffn1_w_down=m_ffn1_w_down,
                     mix_norm=m_mix_norm, w_in=m_w_in, conv_w=m_conv_w, attn_sinks=m_attn_sinks, w_out=m_w_out,
                     ffn2_norm=m_ffn2_norm, ffn2_w_gate=m_ffn2_w_gate, ffn2_w_up=m_ffn2_w_up, ffn2_w_down=m_ffn2_w_down,
                     final_norm=m_final_norm)
    moments_v = dict(ffn1_norm=v_ffn1_norm, ffn1_w_gate=v_ffn1_w_gate, ffn1_w_up=v_ffn1_w_up, ffn1_w_down=v_ffn1_w_down,
                     mix_norm=v_mix_norm, w_in=v_w_in, conv_w=v_conv_w, attn_sinks=v_attn_sinks, w_out=v_w_out,
                     ffn2_norm=v_ffn2_norm, ffn2_w_gate=v_ffn2_w_gate, ffn2_w_up=v_ffn2_w_up, ffn2_w_down=v_ffn2_w_down,
                     final_norm=v_final_norm)

    xs = x[0]
    target = loss_target[0]
    final_gain = final_norm.reshape(1, D_MODEL)

    def ffn_shard(wg, wu, wd):
        return jnp.stack([wg[0].T, wu[0].T, wd[0]]).astype(BF16)

    conv_cols = conv_w.shape[2]
    conv_shard = jnp.pad(conv_w[0], ((0, 5), (0, 128 - conv_cols)))
    rest_shards = [ffn_shard(ffn2_w_gate, ffn2_w_up, ffn2_w_down), w_in[0].T.astype(BF16), w_out[0].astype(BF16),
                   conv_shard]
    rest_rows = [s.shape[-2] for s in rest_shards]
    n_rest = len(rest_shards)
    (w1,), _ = _all_gather_rows([ffn_shard(ffn1_w_gate, ffn1_w_up, ffn1_w_down)], name="gather_ffn1")

    fulls = [lax.empty(s.shape[:-2] + (N_DEV * s.shape[-2], s.shape[-1]), s.dtype) for s in rest_shards]
    fulls = _place_own(fulls, rest_shards, my_index.astype(jnp.int32).reshape(1), name="place_own_weights")
    send_plan = _gather_send_plan(n_rest)
    ssem, rsem, bufs, token = _split_start(rest_shards + list(fulls), 4 * n_rest, send_plan, name="gather_rest_start",
                                           after=[w1])
    x1, h1, s1, sa1, sb1 = _ffn_fwd(xs, ffn1_norm, w1, name="ffn1_fwd", after=[token])
    bufs = _split_wait(ssem, rsem, bufs, x1, send_plan, name="gather_rest_wait")
    w2_part, mixer_parts = bufs[n_rest], bufs[n_rest + 1:]
    fwd_mixer = _gather_forward_plan(rest_rows[1:])
    ssem, rsem, bufs, token = _split_start(mixer_parts, 3 * (n_rest - 1), fwd_mixer, name="forward_mixer_start")
    win_t, wout, conv_all = _split_wait(ssem, rsem, bufs, token, fwd_mixer, name="forward_mixer_wait")
    conv_full = conv_all.reshape(N_DEV, 8, 128)[:, :3, :conv_cols].transpose(1, 0, 2).reshape(3, CONV_W)
    fwd_ffn2 = _gather_forward_plan(rest_rows[:1])
    ssem, rsem, bufs, token = _split_start([w2_part], 3, fwd_ffn2, name="forward_ffn2_start", after=[win_t])
    rope = _rope_tables(xs.shape[0])
    x2, hm, z, y = _mixer_fwd(x1, mix_norm, win_t, wout, conv_full, attn_sinks, rope, name="mixer_fwd", after=[token])
    (w2,) = _split_wait(ssem, rsem, bufs, x2, fwd_ffn2, name="forward_ffn2_wait")
    x3, h2, s2, sa2, sb2 = _ffn_fwd(x2, ffn2_norm, w2, name="ffn2_fwd")

    def to_sibling_start(grads, tag, after=()):
        views = [g.reshape(4, 2, g.shape[0] // N_DEV, D_MODEL) for g in grads]
        lands = [lax.empty((4,) + v.shape[2:], F32) for v in views]
        plan = _sibling_plan(len(views))
        ssem, rsem, bufs, token = _split_start(views + lands, len(views), plan, name=f"{tag}_sibling_start", after=after)
        return (ssem, rsem, bufs, plan, tag), token

    def to_sibling_finish(handle, after, names):
        ssem, rsem, bufs, plan, tag = handle
        bufs = _split_wait(ssem, rsem, bufs, after, plan, name=f"{tag}_sibling_wait")
        n = len(names)
        return [_add_sibling(v, r, core, name=f"add_sibling_{nm}", tr=v.shape[2] // 2)
                for v, r, nm in zip(bufs[:n], bufs[n:], names)]

    def to_chips_start(partials, tag, small_all=None, after=()):
        p16 = [p for _, p in partials]
        lands = [lax.empty((3,) + p.shape[1:], BF16) for p in p16]
        extra = [] if small_all is None else [small_all]
        plan = _chips_plan(len(p16), small_all is not None)
        ssem, rsem, bufs, token = _split_start(p16 + lands + extra, 3 * len(p16) + 7 * len(extra), plan,
                                               name=f"{tag}_chips_start", after=after)
        return (ssem, rsem, bufs, plan, tag), token

    def to_chips_finish(handle, partials, after, names):
        ssem, rsem, bufs, plan, tag = handle
        bufs = _split_wait(ssem, rsem, bufs, after, plan, name=f"{tag}_chips_wait")
        n = len(names)
        return [(p32, r) for (p32, _), r in zip(partials, bufs[n:2 * n])], bufs[2 * n:]

    half_ff = D_FF // 2
    names2, namesm = ["ffn2_w_gate", "ffn2_w_up", "ffn2_w_down"], ["w_in", "w_out"]
    transposed = {"ffn1_w_gate", "ffn1_w_up", "w_in", "ffn2_w_gate", "ffn2_w_up"}
    grad, delta, new_m, new_v = {}, {}, {}, {}

    def adam_big(nm, parts, after=()):
        to_rows = (lambda a: jnp.swapaxes(a, 1, 2)[0]) if nm in transposed else (lambda a: a[0])
        from_rows = (lambda a: jnp.swapaxes(a[None], 1, 2)) if nm in transposed else (lambda a: a[None])
        p32, recv = parts
        *outs, token = _reduce_adamw(p32, recv, chip, to_rows(given[nm]), to_rows(moments_m[nm]),
                                     to_rows(moments_v[nm]), name=f"adamw_{nm}", tr=p32.shape[1] // 2, after=after)
        grad[nm], delta[nm], new_m[nm], new_v[nm] = (from_rows(a) for a in outs)
        return token

    dx2, da2, db2, g2b, d_norm2, loss_local, d_final = _ffn_dgrad(
        x3, x2, ffn2_norm, sa2, sb2, w2, head=(final_gain, target), name="ffn2_dgrad")
    gw2 = [_tn_matmul(da2, h2, name="ffn2_wgrad_gate", bm=half_ff), _tn_matmul(db2, h2, name="ffn2_wgrad_up", bm=half_ff),
           _tn_matmul(s2, g2b, name="ffn2_wgrad_down", bm=half_ff)]
    sib2, tok = to_sibling_start(gw2, "ffn2")
    dx1, dz, gmb, d_conv, d_sink, d_normm = _mixer_bwd(dx2, x1, mix_norm, y, z, win_t, wout, conv_full, attn_sinks,
                                                       rope, name="mixer_bwd", after=[tok])
    p2 = to_sibling_finish(sib2, dx1, names2)
    chips2, tok = to_chips_start(p2, "ffn2")
    gwm = [_tn_matmul(dz, hm, name="mixer_wgrad_in", bm=Z_W // 3, after=[tok]),
           _tn_matmul(y, gmb, name="mixer_wgrad_out", bm=D_MODEL // 2, after=[tok])]
    sibm, tok = to_sibling_start(gwm, "mixer")
    dx0, da1, db1, g1b, d_norm1 = _ffn_dgrad(dx1, xs, ffn1_norm, sa1, sb1, w1, name="ffn1_dgrad", after=[tok], tm=512)
    r2, _ = to_chips_finish(chips2, p2, dx0, names2)
    pm = to_sibling_finish(sibm, dx0, namesm)
    chipsm, tok = to_chips_start(pm, "mixer")
    gw_gate = _tn_matmul(da1, h1, name="ffn1_wgrad_gate", bm=half_ff, after=[tok])
    sib_gate, tok = to_sibling_start([gw_gate], "ffn1_gate")
    gw_up = _tn_matmul(db1, h1, name="ffn1_wgrad_up", bm=half_ff, after=[tok])
    rm, _ = to_chips_finish(chipsm, pm, gw_up, namesm)
    p_gate = to_sibling_finish(sib_gate, gw_up, ["ffn1_w_gate"])
    chips_gate, tok_a = to_chips_start(p_gate, "ffn1_gate")
    sib_up, tok_b = to_sibling_start([gw_up], "ffn1_up", after=[tok_a])
    gw_down = _tn_matmul(s1, g1b, name="ffn1_wgrad_down", bm=half_ff, after=[tok_a, tok_b])
    p_up = to_sibling_finish(sib_up, gw_down, ["ffn1_w_up"])
    chips_up, tok_a = to_chips_start(p_up, "ffn1_up")
    sib_down, tok_b = to_sibling_start([gw_down], "ffn1_down", after=[tok_a])
    p_down = to_sibling_finish(sib_down, tok_b, ["ffn1_w_down"])
    last_row = (jnp.pad(d_sink, ((0, 0), (0, D_MODEL - 128)))
                + jnp.pad(loss_local, ((0, 0), (LOSS_LANE, D_MODEL - LOSS_LANE - 1))))
    small = jnp.concatenate([
        d_norm1, d_normm, d_norm2, d_final, jnp.pad(d_conv[0:3], ((0, 0), (0, D_MODEL - CONV_W))), last_row], axis=0)
    (small_all,) = _place_own([lax.empty((N_DEV * 8, D_MODEL), F32)], [small], my_index.astype(jnp.int32).reshape(1),
                              name="place_own_small")
    chips_down, tok = to_chips_start(p_down, "ffn1_down", small_all)
    for nm, g in zip(names2 + namesm, r2 + rm):
        tok = adam_big(nm, g, after=[tok])
    r_gate, _ = to_chips_finish(chips_gate, p_gate, tok, ["ffn1_w_gate"])
    tok = adam_big("ffn1_w_gate", r_gate[0])
    r_up, _ = to_chips_finish(chips_up, p_up, tok, ["ffn1_w_up"])
    tok = adam_big("ffn1_w_up", r_up[0])
    r_down, (small_all,) = to_chips_finish(chips_down, p_down, tok, ["ffn1_w_down"])
    adam_big("ffn1_w_down", r_down[0])
    small_sum = _sum_small(small_all.reshape(N_DEV, 8, D_MODEL), name="sum_small")
    loss = small_sum[7, LOSS_LANE]
    _update_small(given, moments_m, moments_v, small_sum, my_index, grad, delta, new_m, new_v)

    order = list(given)
    return (loss, dx0[None], *[grad[n] for n in order], *[delta[n] for n in order],
            *[new_m[n] for n in order], *[new_v[n] for n in order])


def _update_small(given, moments_m, moments_v, small_sum, my_index, grad, delta, new_m, new_v):
    conv_cols = given["conv_w"].shape[2]
    small_g = {
        "ffn1_norm": small_sum[0:1], "mix_norm": small_sum[1:2], "ffn2_norm": small_sum[2:3],
        "final_norm": small_sum[3:4],
        "conv_w": lax.dynamic_slice(small_sum[4:7, :CONV_W], (0, my_index * conv_cols), (3, conv_cols)),
        "attn_sinks": small_sum[7:8, :N_Q_HEADS],
    }

    small_names = ["ffn1_norm", "mix_norm", "ffn2_norm", "final_norm", "conv_w", "attn_sinks"]

    def pack(parts):
        rows = []
        for nm in small_names:
            p = parts[nm]
            p2 = p.reshape(3, conv_cols) if nm == "conv_w" else p.reshape(1, -1)
            rows.append(jnp.pad(p2, ((0, 0), (0, D_MODEL - p2.shape[1]))))
        rows.append(jnp.zeros((8, D_MODEL), F32))
        return jnp.concatenate(rows, axis=0)

    sd, sm, sv = _adamw(pack(given), pack(small_g), pack(moments_m), pack(moments_v), name="adamw_small", tr=16)
    row = 0
    for nm in small_names:
        shape = given[nm].shape
        nrow = 3 if nm == "conv_w" else 1
        ncol = conv_cols if nm == "conv_w" else given[nm].size
        grad[nm] = small_g[nm].reshape(shape)
        delta[nm], new_m[nm], new_v[nm] = (a[row:row + nrow, :ncol].reshape(shape) for a in (sd, sm, sv))
        row += nrow
```

```python
import functools

import jax
import jax.numpy as jnp
from jax import lax
from jax.experimental import pallas as pl
from jax.experimental.pallas import tpu as pltpu

F32 = jnp.float32
BF16 = jnp.bfloat16
MESH = pl.DeviceIdType.MESH
ANY = pl.BlockSpec(memory_space=pl.ANY)
HBM_SPEC = pl.BlockSpec(memory_space=pltpu.HBM)
SEM_SPEC = pl.BlockSpec(memory_space=pltpu.SEMAPHORE)
DATAFLOW = pltpu.SideEffectType.DATAFLOW_SIDE_EFFECTING

N_DEV = 8
LOSS_LANE = 128
D_MODEL = 1024
D_FF = 2816
CONV_W = 512
ATTN_W = 512
KV_W = 128
HEAD_DIM = 64
N_Q_HEADS = 8
N_KV_HEADS = 2
Q_PER_KV = N_Q_HEADS // N_KV_HEADS
BLOCK = 128
ROT_DIM = 16
ROPE_THETA = 500000.0
Z_W = 3 * CONV_W + ATTN_W + 2 * KV_W
Q_OFF = 3 * CONV_W
K_OFF = Q_OFF + ATTN_W
V_OFF = K_OFF + KV_W
RMS_EPS = 1e-5
MASK_VALUE = -1e30
SM_SCALE = HEAD_DIM ** -0.5
FFN_RES_SCALE = 0.5

ADAM_LR = 0.001
ADAM_B1 = 0.9
ADAM_B2 = 0.999
ADAM_EPS = 1e-08
ADAM_WD = 0.01
ADAM_STEP = 10

NT_DIMS = (((1,), (1,)), ((), ()))
TN_DIMS = (((0,), (0,)), ((), ()))

VMEM_LIMIT = 62 * 1024 * 1024
FF_CHUNK = 256


def _params(sem, vmem=None):
    return pltpu.CompilerParams(dimension_semantics=sem, vmem_limit_bytes=vmem)


def _behind(body, n_in, after):
    k = len(after)
    if k == 0:
        return body
    return lambda *refs: body(*refs[:n_in], *refs[n_in + k:])


def _rms_stats(xf):
    inv = lax.rsqrt(jnp.mean(xf * xf, axis=-1, keepdims=True) + RMS_EPS)
    return xf * inv, inv


def _rms_bwd(dh, xhat, inv, gain):
    dxhat = dh * gain
    dx = inv * (dxhat - xhat * jnp.mean(dxhat * xhat, axis=-1, keepdims=True))
    dgain = jnp.sum(dh * xhat, axis=0, keepdims=True)
    return dx, dgain


def _load_resident(w_hbm, w_ref, sem):
    @pl.when(pl.program_id(0) == 0)
    def _():
        cp = pltpu.make_async_copy(w_hbm, w_ref, sem)
        cp.start()
        cp.wait()


def _ffn_fwd(x, gain, w3, *, name, head=None, after=(), tm=256, tf=FF_CHUNK):
    t = x.shape[0]
    tm = min(tm, t)
    n_head = 0 if head is None else 2

    def body(*refs):
        x_ref, g_ref, w_hbm = refs[:3]
        head_refs = refs[3:3 + n_head]
        xo_ref, h_ref, s_ref, sa_ref, sb_ref = refs[3 + n_head:8 + n_head]
        head_outs = refs[8 + n_head:8 + 2 * n_head]
        w_ref, sem = refs[8 + 2 * n_head:]
        _load_resident(w_hbm, w_ref, sem)

        @pl.when(pl.program_id(0) == 0)
        def _():
            for ref in head_outs:
                ref[...] = jnp.zeros_like(ref)

        xf = x_ref[...]
        xhat, _ = _rms_stats(xf)
        h = (xhat * g_ref[...]).astype(BF16)
        h_ref[...] = h
        for c in range(0, D_FF, tf):
            cols = slice(c, min(c + tf, D_FF))
            a = lax.dot_general(h, w_ref[0, cols, :], NT_DIMS, preferred_element_type=F32)
            b = lax.dot_general(h, w_ref[1, cols, :], NT_DIMS, preferred_element_type=F32)
            sig = jax.nn.sigmoid(a)
            silu = a * sig
            s_ref[:, cols] = (silu * b).astype(BF16)
            sa_ref[:, cols] = (b * (sig * (1.0 + a * (1.0 - sig)))).astype(BF16)
            sb_ref[:, cols] = silu.astype(BF16)
        xo = xf + FFN_RES_SCALE * jnp.dot(s_ref[...], w_ref[2], preferred_element_type=F32)
        if head is None:
            xo_ref[...] = xo
        else:
            fg_ref, t_ref = head_refs
            loss_ref, dfg_ref = head_outs
            xhat_o, inv_o = _rms_stats(xo)
            err = xhat_o * fg_ref[...] - t_ref[...]
            loss_ref[...] += 0.5 * jnp.sum(jnp.mean(err * err, axis=-1, keepdims=True), axis=0, keepdims=True)
            xo_ref[...], dfg = _rms_bwd(err * (1.0 / D_MODEL), xhat_o, inv_o, fg_ref[...])
            dfg_ref[...] += dfg

    row = pl.BlockSpec((tm, D_MODEL), lambda i: (i, 0))
    hid = pl.BlockSpec((tm, D_FF), lambda i: (i, 0))
    vec = pl.BlockSpec((1, D_MODEL), lambda i: (0, 0))
    head_in = [] if head is None else [vec, row]
    head_out = [] if head is None else [pl.BlockSpec((1, 1), lambda i: (0, 0)), vec]
    head_shape = [] if head is None else [jax.ShapeDtypeStruct((1, 1), F32), jax.ShapeDtypeStruct((1, D_MODEL), F32)]
    return pl.pallas_call(
        _behind(body, 3 + n_head, after), name=name, grid=(t // tm,),
        in_specs=[row, vec, ANY] + head_in + [ANY] * len(after),
        out_specs=[row, row, hid, hid, hid] + head_out,
        out_shape=[jax.ShapeDtypeStruct((t, D_MODEL), F32), jax.ShapeDtypeStruct((t, D_MODEL), BF16)]
        + [jax.ShapeDtypeStruct((t, D_FF), BF16)] * 3 + head_shape,
        scratch_shapes=[pltpu.VMEM((3, D_FF, D_MODEL), BF16), pltpu.SemaphoreType.DMA(())],
        compiler_params=_params(("arbitrary",), VMEM_LIMIT),
    )(x, gain, w3, *(head or ()), *after)


def _ffn_dgrad(dxo, x, gain, sa, sb, w3, *, name, after=(), tm=512, tf=FF_CHUNK):
    t = x.shape[0]
    tm = min(tm, t)

    def body(dxo_ref, x_ref, g_ref, sa_ref, sb_ref, w_hbm, dxi_ref, da_ref, db_ref, gb_ref, dg_ref, w_ref, sem):
        _load_resident(w_hbm, w_ref, sem)

        @pl.when(pl.program_id(0) == 0)
        def _():
            dg_ref[...] = jnp.zeros_like(dg_ref)

        go = dxo_ref[...]
        gb = (FFN_RES_SCALE * go).astype(BF16)
        gb_ref[...] = gb
        for c in range(0, D_FF, tf):
            cols = slice(c, min(c + tf, D_FF))
            ds = lax.dot_general(gb, w_ref[2, cols, :], NT_DIMS, preferred_element_type=F32)
            da_ref[:, cols] = (ds * sa_ref[:, cols].astype(F32)).astype(BF16)
            db_ref[:, cols] = (ds * sb_ref[:, cols].astype(F32)).astype(BF16)
        dh = (jnp.dot(da_ref[...], w_ref[0], preferred_element_type=F32)
              + jnp.dot(db_ref[...], w_ref[1], preferred_element_type=F32))
        xhat, inv = _rms_stats(x_ref[...])
        dx, dgain = _rms_bwd(dh, xhat, inv, g_ref[...])
        dxi_ref[...] = go + dx
        dg_ref[...] += dgain

    row = pl.BlockSpec((tm, D_MODEL), lambda i: (i, 0))
    hid = pl.BlockSpec((tm, D_FF), lambda i: (i, 0))
    vec = pl.BlockSpec((1, D_MODEL), lambda i: (0, 0))
    return pl.pallas_call(
        _behind(body, 6, after), name=name, grid=(t // tm,),
        in_specs=[row, row, vec, hid, hid, ANY] + [ANY] * len(after),
        out_specs=[row, hid, hid, row, vec],
        out_shape=[jax.ShapeDtypeStruct((t, D_MODEL), F32), jax.ShapeDtypeStruct((t, D_FF), BF16),
                   jax.ShapeDtypeStruct((t, D_FF), BF16),
                   jax.ShapeDtypeStruct((t, D_MODEL), BF16), jax.ShapeDtypeStruct((1, D_MODEL), F32)],
        scratch_shapes=[pltpu.VMEM((3, D_FF, D_MODEL), BF16), pltpu.SemaphoreType.DMA(())],
        compiler_params=_params(("arbitrary",), VMEM_LIMIT),
    )(dxo, x, gain, sa, sb, w3, *after)


def _tn_matmul(a, b, *, name, bm, after=(), tk=2048):
    t, m = a.shape
    n = b.shape[1]
    tk = min(tk, t)
    nk = t // tk

    def body(a_ref, b_ref, o_ref):
        @pl.when(pl.program_id(1) == 0)
        def _():
            o_ref[...] = jnp.zeros_like(o_ref)

        o_ref[...] += lax.dot_general(a_ref[...], b_ref[...], TN_DIMS, preferred_element_type=F32)

    return pl.pallas_call(
        _behind(body, 2, after), name=name, grid=(m // bm, nk),
        in_specs=[pl.BlockSpec((tk, bm), lambda i, k: (k, i)), pl.BlockSpec((tk, n), lambda i, k: (k, 0))]
        + [ANY] * len(after),
        out_specs=pl.BlockSpec((bm, n), lambda i, k: (i, 0)),
        out_shape=jax.ShapeDtypeStruct((m, n), F32),
        compiler_params=_params(("parallel", "arbitrary"), VMEM_LIMIT),
    )(a, b, *after)


def _rope_tables(t):
    inv_freq = ROPE_THETA ** (-jnp.arange(0, ROT_DIM, 2, dtype=F32) / ROT_DIM)
    ang = inv_freq[:, None] * jnp.arange(t, dtype=F32)[None, :]
    return jnp.stack([jnp.tile(jnp.cos(ang).T, (1, 16)), jnp.tile(jnp.sin(ang).T, (1, 16))])


def _rope_multipliers(cos_sin):
    half = ROT_DIM // 2
    cos, sin = cos_sin[0], cos_sin[1]
    d = lax.broadcasted_iota(jnp.int32, cos.shape, 1) & (HEAD_DIM - 1)
    mult = jnp.where(d < ROT_DIM, cos, 1.0)
    from_lo = jnp.where((d >= half) & (d < ROT_DIM), sin, 0.0)
    from_hi = jnp.where(d < half, -sin, 0.0)
    return mult, from_lo, from_hi


def _tile_lanes(tab, width):
    return jnp.tile(tab, (1, width // tab.shape[1]))


def _rope(v, tab):
    w = v.shape[1]
    half_rot = ROT_DIM // 2
    return (v * _tile_lanes(tab[0], w)
            + pltpu.roll(v, half_rot, axis=1) * _tile_lanes(tab[1], w)
            + pltpu.roll(v, w - half_rot, axis=1) * _tile_lanes(tab[2], w))


def _rope_bwd(dv, tab):
    w = dv.shape[1]
    half_rot = ROT_DIM // 2
    return (dv * _tile_lanes(tab[0], w)
            + pltpu.roll(dv * _tile_lanes(tab[1], w), w - half_rot, axis=1)
            + pltpu.roll(dv * _tile_lanes(tab[2], w), half_rot, axis=1))


def _shift_rows(v, prev8_ref, n):
    r = lax.broadcasted_iota(jnp.int32, v.shape, 0)
    rolled = pltpu.roll(v, n, axis=0)
    last = prev8_ref[7:8, :]
    if n == 1:
        return jnp.where(r >= 1, rolled, last)
    return jnp.where(r >= 2, rolled, jnp.where(r == 0, prev8_ref[6:7, :], last))


def _shift_rows_up(v, next8_ref, n):
    rows = v.shape[0]
    r = lax.broadcasted_iota(jnp.int32, v.shape, 0)
    rolled = pltpu.roll(v, rows - n, axis=0)
    first = next8_ref[0:1, :]
    if n == 1:
        return jnp.where(r <= rows - 2, rolled, first)
    return jnp.where(r <= rows - 3, rolled, jnp.where(r == rows - 2, first, next8_ref[1:2, :]))


def _lane_half_mask(shape, half):
    lane = lax.broadcasted_iota(jnp.int32, shape, 1)
    return (lane >= HEAD_DIM) if half else (lane < HEAD_DIM)


def _to_kv_lanes(chunk, head, kv):
    if head % 2 != kv:
        chunk = pltpu.roll(chunk, HEAD_DIM, axis=1)
    return jnp.where(_lane_half_mask(chunk.shape, kv), chunk, 0.0)


def _from_kv_lanes(chunk, head, kv):
    chunk = jnp.where(_lane_half_mask(chunk.shape, kv), chunk, 0.0)
    if head % 2 != kv:
        chunk = pltpu.roll(chunk, HEAD_DIM, axis=1)
    return chunk


def _stack_heads(wide):
    parts = []
    for head in range(N_Q_HEADS):
        chunk = wide[:, (head // 2) * 128:(head // 2 + 1) * 128]
        parts.append(_to_kv_lanes(chunk, head, head // Q_PER_KV))
    return jnp.concatenate(parts, axis=0)


def _window_mask(has_prev):
    shape = (N_Q_HEADS * BLOCK, 2 * BLOCK)
    qi = lax.broadcasted_iota(jnp.int32, shape, 0) & (BLOCK - 1)
    kj = lax.broadcasted_iota(jnp.int32, shape, 1)
    first_key = BLOCK - has_prev * BLOCK
    in_prev = (kj < BLOCK) & (kj > qi) & (kj >= first_key)
    in_own = (kj >= BLOCK) & ((kj - BLOCK) <= qi)
    return in_prev | in_own


def _sink_column(sink_ref):
    row = lax.broadcasted_iota(jnp.int32, (N_Q_HEADS * BLOCK, 1), 0)
    col = jnp.full((N_Q_HEADS * BLOCK, 1), sink_ref[0, 0], F32)
    for head in range(1, N_Q_HEADS):
        col = jnp.where(row >= head * BLOCK, sink_ref[0, head], col)
    return col


def _softmax_with_sink(q4, k2, mask, sink):
    s = lax.dot_general(q4, k2, NT_DIMS, preferred_element_type=F32) * SM_SCALE
    s = jnp.where(mask, s, MASK_VALUE)
    m = jnp.maximum(jnp.max(s, axis=-1, keepdims=True), sink)
    p = jnp.exp(s - m)
    e_sink = jnp.exp(sink - m)
    inv_den = 1.0 / (jnp.sum(p, axis=-1, keepdims=True) + e_sink)
    return p * inv_den, e_sink * inv_den


def _conv_terms(zf, prev8_ref, w_ref):
    b_gate, c_gate, u = zf[:, 0:CONV_W], zf[:, CONV_W:2 * CONV_W], zf[:, 2 * CONV_W:3 * CONV_W]
    vc = c_gate * u
    vm1 = _shift_rows(vc, prev8_ref, 1)
    vm2 = _shift_rows(vc, prev8_ref, 2)
    conv = w_ref[0:1, :] * vm2 + w_ref[1:2, :] * vm1 + w_ref[2:3, :] * vc
    return b_gate, c_gate, u, vc, vm1, vm2, conv


def _mixer_fwd(x, gain, win_t, wout, conv_w, sinks, rope, *, name, after=(), tq=512):
    t = x.shape[0]
    tq = min(tq, t)
    nblk = tq // BLOCK

    def body(x_ref, g_ref, win_hbm, wout_hbm, cw_ref, sink_ref, rope_ref,
             xo_ref, h_ref, z_ref, y_ref, kprev_ref, vprev_ref, cprev_ref, win_ref, wout_ref, sems):
        i = pl.program_id(0)
        _load_resident(win_hbm, win_ref, sems.at[0])
        _load_resident(wout_hbm, wout_ref, sems.at[1])

        @pl.when(i == 0)
        def _():
            kprev_ref[...] = jnp.zeros_like(kprev_ref)
            vprev_ref[...] = jnp.zeros_like(vprev_ref)
            cprev_ref[...] = jnp.zeros_like(cprev_ref)

        xf = x_ref[...]
        xhat, _ = _rms_stats(xf)
        h = (xhat * g_ref[...]).astype(BF16)
        h_ref[...] = h
        zb = lax.dot_general(h, win_ref[...], NT_DIMS, preferred_element_type=F32).astype(BF16)
        z_ref[...] = zb
        zf = zb.astype(F32)

        b_gate, _, _, vc, _, _, conv = _conv_terms(zf, cprev_ref, cw_ref)
        y_conv = b_gate * conv
        cprev_ref[...] = vc[tq - 8:tq, :]

        tab = _rope_multipliers(rope_ref[...])
        qr = _rope(zf[:, Q_OFF:K_OFF], tab)
        kr = _rope(zf[:, K_OFF:V_OFF], tab).astype(BF16)
        vb = zb[:, V_OFF:Z_W]

        y_attn = []
        for j in range(nblk):
            rows = slice(j * BLOCK, (j + 1) * BLOCK)
            prev = slice((j - 1) * BLOCK, j * BLOCK)
            k2 = jnp.concatenate([kprev_ref[...] if j == 0 else kr[prev], kr[rows]], axis=0)
            v2 = jnp.concatenate([vprev_ref[...] if j == 0 else vb[prev], vb[rows]], axis=0)
            mask = _window_mask(jnp.minimum(i, 1) if j == 0 else 1)
            q8 = _stack_heads(qr[rows]).astype(BF16)
            probs, _ = _softmax_with_sink(q8, k2, mask, _sink_column(sink_ref))
            o8 = jnp.dot(probs.astype(BF16), v2, preferred_element_type=F32)
            chunks = [jnp.zeros((BLOCK, 128), F32) for _ in range(ATTN_W // 128)]
            for head in range(N_Q_HEADS):
                chunks[head // 2] += _from_kv_lanes(o8[head * BLOCK:(head + 1) * BLOCK], head, head // Q_PER_KV)
            y_attn.append(jnp.concatenate(chunks, axis=1))
        kprev_ref[...] = kr[tq - BLOCK:tq]
        vprev_ref[...] = vb[tq - BLOCK:tq]
        y = jnp.concatenate([y_conv, jnp.concatenate(y_attn, axis=0)], axis=1).astype(BF16)
        y_ref[...] = y
        xo_ref[...] = xf + jnp.dot(y, wout_ref[...], preferred_element_type=F32)

    row = pl.BlockSpec((tq, D_MODEL), lambda i: (i, 0))
    full = lambda shape: pl.BlockSpec(shape, lambda i: (0,) * len(shape))
    return pl.pallas_call(
        _behind(body, 7, after), name=name, grid=(t // tq,),
        in_specs=[row, full((1, D_MODEL)), ANY, ANY, full((3, CONV_W)),
                  pl.BlockSpec(memory_space=pltpu.SMEM), pl.BlockSpec((2, tq, 128), lambda i: (0, i, 0))]
        + [ANY] * len(after),
        out_specs=[row, row, pl.BlockSpec((tq, Z_W), lambda i: (i, 0)), row],
        out_shape=[jax.ShapeDtypeStruct((t, D_MODEL), F32), jax.ShapeDtypeStruct((t, D_MODEL), BF16),
                   jax.ShapeDtypeStruct((t, Z_W), BF16), jax.ShapeDtypeStruct((t, D_MODEL), BF16)],
        scratch_shapes=[pltpu.VMEM((BLOCK, KV_W), BF16), pltpu.VMEM((BLOCK, KV_W), BF16),
                        pltpu.VMEM((8, CONV_W), F32), pltpu.VMEM((Z_W, D_MODEL), BF16),
                        pltpu.VMEM((D_MODEL, D_MODEL), BF16), pltpu.SemaphoreType.DMA((2,))],
        compiler_params=_params(("arbitrary",), VMEM_LIMIT),
    )(x, gain, win_t, wout, conv_w, sinks, rope, *after)


def _mixer_bwd(dxo, x, gain, y, z, win_t, wout, conv_w, sinks, rope, *, name, after=(), tq=256):
    t = x.shape[0]
    tq = min(tq, t)
    nt, nblk = t // tq, tq // BLOCK

    def body(dxo_ref, x_ref, g_ref, y_ref, z_ref, zp_ref, win_hbm, wout_hbm, cw_ref, sink_ref, rope_ref, ropep_ref,
             dxi_ref, dz_ref, gb_ref, dcw_ref, dsink_ref, dg_ref, dk_ref, dv_ref, dcn_ref, pvc_ref,
             win_ref, wout_ref, sems):
        i = pl.program_id(0)
        tile = nt - 1 - i
        _load_resident(win_hbm, win_ref, sems.at[0])
        _load_resident(wout_hbm, wout_ref, sems.at[1])

        @pl.when(i == 0)
        def _():
            dk_ref[...] = jnp.zeros_like(dk_ref)
            dv_ref[...] = jnp.zeros_like(dv_ref)
            dcn_ref[...] = jnp.zeros_like(dcn_ref)
            dcw_ref[...] = jnp.zeros_like(dcw_ref)
            dsink_ref[...] = jnp.zeros_like(dsink_ref)
            dg_ref[...] = jnp.zeros_like(dg_ref)

        has_prev = jnp.minimum(tile, 1)
        go = dxo_ref[...]
        gb = go.astype(BF16)
        gb_ref[...] = gb
        dy = lax.dot_general(gb, wout_ref[...], NT_DIMS, preferred_element_type=F32)
        dy_conv, dy_attn = dy[:, 0:CONV_W], dy[:, CONV_W:D_MODEL]
        zb, zpb = z_ref[...], zp_ref[...]
        zf = zb.astype(F32)
        zpf = zpb.astype(F32) * has_prev.astype(F32)

        pvc_ref[...] = (zpf[:, CONV_W:2 * CONV_W] * zpf[:, 2 * CONV_W:3 * CONV_W])[BLOCK - 8:BLOCK, :]
        b_gate, c_gate, u, vc, vm1, vm2, conv = _conv_terms(zf, pvc_ref, cw_ref)
        d_bgate = dy_conv * conv
        dc = dy_conv * b_gate
        tap = lax.broadcasted_iota(jnp.int32, (8, CONV_W), 0)
        dcw_ref[...] += jnp.where(tap == 0, jnp.sum(dc * vm2, axis=0, keepdims=True),
                                  jnp.where(tap == 1, jnp.sum(dc * vm1, axis=0, keepdims=True),
                                            jnp.where(tap == 2, jnp.sum(dc * vc, axis=0, keepdims=True), 0.0)))
        dvc = (cw_ref[2:3, :] * dc + cw_ref[1:2, :] * _shift_rows_up(dc, dcn_ref, 1)
               + cw_ref[0:1, :] * _shift_rows_up(dc, dcn_ref, 2))
        dcn_ref[...] = dc[0:8, :]
        d_cgate = dvc * u
        d_u = dvc * c_gate

        tab, tabp = _rope_multipliers(rope_ref[...]), _rope_multipliers(ropep_ref[...])
        qr = _rope(zf[:, Q_OFF:K_OFF], tab)
        kr = _rope(zf[:, K_OFF:V_OFF], tab).astype(BF16)
        kpr = _rope(zpf[:, K_OFF:V_OFF], tabp).astype(BF16)
        vb, vpb = zb[:, V_OFF:Z_W], zpb[:, V_OFF:Z_W]
        out = y_ref[:, CONV_W:D_MODEL].astype(F32)
        do_out = dy_attn * out
        lane = lax.broadcasted_iota(jnp.int32, (1, 128), 1)
        dsink = jnp.zeros((1, 128), F32)
        dk_next, dv_next = dk_ref[...], dv_ref[...]
        dq_rows, dk_rows, dv_rows = [None] * nblk, [None] * nblk, [None] * nblk
        for j in reversed(range(nblk)):
            rows = slice(j * BLOCK, (j + 1) * BLOCK)
            prev = slice((j - 1) * BLOCK, j * BLOCK)
            k2 = jnp.concatenate([kpr if j == 0 else kr[prev], kr[rows]], axis=0)
            v2 = jnp.concatenate([vpb if j == 0 else vb[prev], vb[rows]], axis=0)
            mask = _window_mask(has_prev if j == 0 else 1)
            q8 = _stack_heads(qr[rows]).astype(BF16)
            do8 = _stack_heads(dy_attn[rows]).astype(BF16)
            delta = jnp.sum(_stack_heads(do_out[rows]), axis=-1, keepdims=True)
            probs, p_sink = _softmax_with_sink(q8, k2, mask, _sink_column(sink_ref))
            dp = lax.dot_general(do8, v2, NT_DIMS, preferred_element_type=F32)
            ds = (probs * (dp - delta) * SM_SCALE).astype(BF16)
            dq8 = jnp.dot(ds, k2, preferred_element_type=F32)
            dk2 = lax.dot_general(ds, q8, TN_DIMS, preferred_element_type=F32)
            dv2 = lax.dot_general(probs.astype(BF16), do8, TN_DIMS, preferred_element_type=F32)
            sink_terms = p_sink * delta
            dq_chunks = [jnp.zeros((BLOCK, 128), F32) for _ in range(ATTN_W // 128)]
            for head in range(N_Q_HEADS):
                grp = slice(head * BLOCK, (head + 1) * BLOCK)
                dq_chunks[head // 2] += _from_kv_lanes(dq8[grp], head, head // Q_PER_KV)
                dsink = dsink - jnp.where(lane == head, jnp.sum(sink_terms[grp], axis=0, keepdims=True), 0.0)
            dq_rows[j] = jnp.concatenate(dq_chunks, axis=1)
            dk_rows[j] = dk2[BLOCK:] + dk_next
            dv_rows[j] = dv2[BLOCK:] + dv_next
            dk_next, dv_next = dk2[:BLOCK], dv2[:BLOCK]
        dk_ref[...] = dk_next
        dv_ref[...] = dv_next
        dsink_ref[...] += dsink
        dq = _rope_bwd(jnp.concatenate(dq_rows, axis=0), tab)
        dk = _rope_bwd(jnp.concatenate(dk_rows, axis=0), tab)
        dv = jnp.concatenate(dv_rows, axis=0)

        dzb = jnp.concatenate([d_bgate, d_cgate, d_u, dq, dk, dv], axis=1).astype(BF16)
        dz_ref[...] = dzb
        dh = jnp.dot(dzb, win_ref[...], preferred_element_type=F32)
        xhat, inv = _rms_stats(x_ref[...])
        dx, dgain = _rms_bwd(dh, xhat, inv, g_ref[...])
        dxi_ref[...] = go + dx
        dg_ref[...] += dgain

    rev = lambda i: (nt - 1 - i, 0)
    block_before = lambda i: jnp.maximum((nt - 1 - i) * nblk - 1, 0)
    row = pl.BlockSpec((tq, D_MODEL), rev)
    full = lambda shape: pl.BlockSpec(shape, lambda i: (0,) * len(shape))
    return pl.pallas_call(
        _behind(body, 12, after), name=name, grid=(nt,),
        in_specs=[row, row, full((1, D_MODEL)), row,
                  pl.BlockSpec((tq, Z_W), rev), pl.BlockSpec((BLOCK, Z_W), lambda i: (block_before(i), 0)),
                  ANY, ANY, full((3, CONV_W)),
                  pl.BlockSpec(memory_space=pltpu.SMEM),
                  pl.BlockSpec((2, tq, 128), lambda i: (0, nt - 1 - i, 0)),
                  pl.BlockSpec((2, BLOCK, 128), lambda i: (0, block_before(i), 0))] + [ANY] * len(after),
        out_specs=[row, pl.BlockSpec((tq, Z_W), rev), row, full((8, CONV_W)), full((1, 128)), full((1, D_MODEL))],
        out_shape=[jax.ShapeDtypeStruct((t, D_MODEL), F32), jax.ShapeDtypeStruct((t, Z_W), BF16),
                   jax.ShapeDtypeStruct((t, D_MODEL), BF16), jax.ShapeDtypeStruct((8, CONV_W), F32),
                   jax.ShapeDtypeStruct((1, 128), F32), jax.ShapeDtypeStruct((1, D_MODEL), F32)],
        scratch_shapes=[pltpu.VMEM((BLOCK, KV_W), F32), pltpu.VMEM((BLOCK, KV_W), F32), pltpu.VMEM((8, CONV_W), F32),
                        pltpu.VMEM((8, CONV_W), F32), pltpu.VMEM((Z_W, D_MODEL), BF16),
                        pltpu.VMEM((D_MODEL, D_MODEL), BF16), pltpu.SemaphoreType.DMA((2,))],
        compiler_params=_params(("arbitrary",), VMEM_LIMIT),
    )(dxo, x, gain, y, z, z, win_t, wout, conv_w, sinks, rope, rope, *after)


def _place():
    x, y, c = lax.axis_index("x"), lax.axis_index("y"), lax.axis_index("c")
    other_chips = [(1 - x, y), (x, 1 - y), (1 - x, 1 - y)]
    return x, y, c, other_chips


def _all_gather_rows(shards, place=(), *, name):
    n, p = len(shards), len(place)

    def body(*refs):
        srcs, place_srcs = refs[:n], refs[n:n + p]
        outs, place_outs = refs[n + p:2 * n + p], refs[2 * n + p:2 * (n + p)]
        send_sems, recv_sems, local_sems = refs[2 * (n + p):]
        x, y, c, chips = _place()
        me, sibling = (x, y, c), (x, y, 1 - c)

        def rows(t, px, py, pc):
            r = srcs[t].shape[-2]
            start = pl.multiple_of((4 * px + 2 * py + pc) * r, 16 if r % 16 == 0 else 8)
            if len(srcs[t].shape) == 3:
                return outs[t].at[:, pl.ds(start, r), :]
            return outs[t].at[pl.ds(start, r), :]

        def copy(t, k, block, to, own=False):
            return pltpu.make_async_remote_copy(
                src_ref=srcs[t] if own else rows(t, *block), dst_ref=rows(t, *block),
                send_sem=send_sems.at[t, k], recv_sem=recv_sems.at[t, k], device_id=to, device_id_type=MESH)

        mine = [pltpu.make_async_copy(srcs[t], rows(t, *me), local_sems.at[t]) for t in range(n)]
        mine += [pltpu.make_async_copy(place_srcs[q],
                                       _block_rows(place_outs[q], place_srcs[q].shape[-2], 4 * x + 2 * y + c),
                                       local_sems.at[n + q]) for q in range(p)]
        for q in range(p):
            mine[n + q].start()
        first = []
        for t in range(n):
            mine[t].start()
            first.append(copy(t, 0, me, sibling, own=True))
            first += [copy(t, 1 + j, me, (*chip, c), own=True) for j, chip in enumerate(chips)]
        for cp in first:
            cp.start()
        passed = []
        for j, chip in enumerate(chips):
            for t in range(n):
                copy(t, 1 + j, (*chip, c), me).wait_recv()
                fwd = copy(t, 4 + j, (*chip, c), sibling)
                fwd.start()
                passed.append(fwd)
        for t in range(n):
            copy(t, 0, sibling, me).wait_recv()
            for j, chip in enumerate(chips):
                copy(t, 4 + j, (*chip, 1 - c), me).wait_recv()
        for cp in first + passed:
            cp.wait_send()
        for cp in mine:
            cp.wait()

    out_shape = [jax.ShapeDtypeStruct(s.shape[:-2] + (N_DEV * s.shape[-2], s.shape[-1]), s.dtype)
                 for s in list(shards) + list(place)]
    res = pl.pallas_call(
        body, name=name, in_specs=[ANY] * (n + p), out_specs=[ANY] * (n + p), out_shape=out_shape,
        scratch_shapes=[pltpu.SemaphoreType.DMA((n, 7)), pltpu.SemaphoreType.DMA((n, 7)),
                        pltpu.SemaphoreType.DMA((n + p,))],
    )(*shards, *place)
    return res[:n], res[n:]


def _split_start(bufs, n_copies, plan, *, name, after=()):
    n = len(bufs)

    def body(*refs):
        token = refs[-1]
        for cp in plan(refs[:n], refs[n], refs[n + 1]):
            cp.start()
        token[...] = jnp.zeros_like(token)

    res = pl.pallas_call(
        _behind(body, n, after), name=name, in_specs=[HBM_SPEC] * n + [ANY] * len(after),
        out_specs=(SEM_SPEC, SEM_SPEC, *[HBM_SPEC] * n, pl.BlockSpec(memory_space=pltpu.VMEM)),
        out_shape=(pltpu.SemaphoreType.DMA((n_copies,)), pltpu.SemaphoreType.DMA((n_copies,)),
                   *[pltpu.HBM(b.shape, b.dtype) for b in bufs], jax.ShapeDtypeStruct((8, 128), F32)),
        input_output_aliases={i: 2 + i for i in range(n)},
        compiler_params=pltpu.CompilerParams(has_side_effects=DATAFLOW),
    )(*[pltpu.with_memory_space_constraint(b, pltpu.HBM) for b in bufs], *after)
    return res[0], res[1], list(res[2:2 + n]), res[-1]


def _split_wait(send_sems, recv_sems, bufs, after, plan, *, name):
    n = len(bufs)

    def body(*refs):
        for cp in plan(refs[:n], refs[n], refs[n + 1]):
            cp.wait_send()
            cp.wait_recv()

    return list(pl.pallas_call(
        body, name=name, in_specs=[HBM_SPEC] * n + [SEM_SPEC, SEM_SPEC, ANY], out_specs=[HBM_SPEC] * n,
        out_shape=tuple(pltpu.HBM(b.shape, b.dtype) for b in bufs),
        input_output_aliases={i: i for i in range(n)},
        compiler_params=pltpu.CompilerParams(has_side_effects=DATAFLOW),
    )(*bufs, send_sems, recv_sems, after))


def _sibling_plan(n):
    def plan(bufs, send_sems, recv_sems):
        x, y, c, _ = _place()
        return [pltpu.make_async_remote_copy(
            src_ref=bufs[t].at[:, 1 - c], dst_ref=bufs[n + t], send_sem=send_sems.at[t], recv_sem=recv_sems.at[t],
            device_id=(x, y, 1 - c), device_id_type=MESH) for t in range(n)]
    return plan


def _block_rows(ref, r, blk):
    start = pl.multiple_of(blk * r, 16 if r % 16 == 0 else 8)
    return ref.at[(slice(None),) * (len(ref.shape) - 2) + (pl.ds(start, r), slice(None))]


def _remote(src, dst, send_sems, recv_sems, k, peer):
    return pltpu.make_async_remote_copy(src_ref=src, dst_ref=dst, send_sem=send_sems.at[k], recv_sem=recv_sems.at[k],
                                        device_id=peer, device_id_type=MESH)


def _gather_send_plan(n):
    def plan(bufs, send_sems, recv_sems):
        x, y, c, chips = _place()
        peers = [(x, y, 1 - c)] + [(px, py, c) for px, py in chips]
        copies = []
        for t in range(n):
            dst = _block_rows(bufs[n + t], bufs[t].shape[-2], 4 * x + 2 * y + c)
            copies += [_remote(bufs[t], dst, send_sems, recv_sems, 4 * t + k, peer) for k, peer in enumerate(peers)]
        return copies
    return plan


def _gather_forward_plan(rows):
    def plan(bufs, send_sems, recv_sems):
        x, y, c, chips = _place()
        copies = []
        for t, r in enumerate(rows):
            for j, (px, py) in enumerate(chips):
                blk = _block_rows(bufs[t], r, 4 * px + 2 * py + c)
                copies.append(_remote(blk, blk, send_sems, recv_sems, 3 * t + j, (x, y, 1 - c)))
        return copies
    return plan


def _chips_plan(n, with_small):
    def plan(bufs, send_sems, recv_sems):
        x, y, c, chips = _place()
        copies = []
        for t in range(n):
            for j, (px, py) in enumerate(chips):
                copies.append(_remote(bufs[t].at[2 * px + py], bufs[n + t].at[j], send_sems, recv_sems, 3 * t + j,
                                      (px, py, c)))
        if with_small:
            mine = _block_rows(bufs[2 * n], 8, 4 * x + 2 * y + c)
            flips = [(fx, fy, fc) for fx in range(2) for fy in range(2) for fc in range(2)][1:]
            for k, (fx, fy, fc) in enumerate(flips):
                peer = (x + fx - 2 * x * fx, y + fy - 2 * y * fy, c + fc - 2 * c * fc)
                copies.append(_remote(mine, mine, send_sems, recv_sems, 3 * n + k, peer))
        return copies
    return plan


def _place_own(fulls, shards, index, *, name):
    n = len(fulls)

    def body(index_ref, *refs):
        for t in range(n):
            refs[2 * n + t][...] = refs[n + t][...]

    def block_of(shard):
        lead = len(shard.shape) - 2
        return pl.BlockSpec(shard.shape, lambda i, index_ref: (0,) * lead + (index_ref[0], 0))

    def whole(shard):
        return pl.BlockSpec(shard.shape, lambda i, index_ref: (0,) * len(shard.shape))

    return list(pl.pallas_call(
        body, name=name,
        grid_spec=pltpu.PrefetchScalarGridSpec(
            num_scalar_prefetch=1, grid=(1,),
            in_specs=[ANY] * n + [whole(s) for s in shards], out_specs=[block_of(s) for s in shards]),
        out_shape=[jax.ShapeDtypeStruct(f.shape, f.dtype) for f in fulls],
        input_output_aliases={1 + t: t for t in range(n)},
        compiler_params=_params(("arbitrary",)),
    )(index, *fulls, *shards))


def _add_sibling(grad, recv, core, *, name, tr):
    rows = grad.shape[2]

    def body(core_ref, g_ref, r_ref, o_ref, ob_ref):
        p = g_ref[:, 0] + r_ref[...]
        o_ref[...] = p
        ob_ref[...] = p.astype(BF16)

    out = pl.BlockSpec((4, tr, D_MODEL), lambda i, core_ref: (0, i, 0))
    return pl.pallas_call(
        body, name=name,
        grid_spec=pltpu.PrefetchScalarGridSpec(
            num_scalar_prefetch=1, grid=(rows // tr,),
            in_specs=[pl.BlockSpec((4, 1, tr, D_MODEL), lambda i, core_ref: (0, core_ref[0], i, 0)), out],
            out_specs=[out, out]),
        out_shape=[jax.ShapeDtypeStruct(recv.shape, F32), jax.ShapeDtypeStruct(recv.shape, BF16)],
        compiler_params=_params(("arbitrary",)),
    )(core, grad, recv)


def _reduce_adamw(partial, recv, chip, w, m, v, *, name, tr, after=()):
    rows = partial.shape[1]

    def body(chip_ref, p_ref, r_ref, w_ref, m_ref, v_ref, g_ref, d_ref, mo_ref, vo_ref, token_ref):
        g = p_ref[0] + r_ref[0].astype(F32) + r_ref[1].astype(F32) + r_ref[2].astype(F32)
        g_ref[...] = g
        d_ref[...], mo_ref[...], vo_ref[...] = _adamw_math(w_ref[...], g, m_ref[...], v_ref[...])
        token_ref[...] = jnp.zeros_like(token_ref)

    spec = pl.BlockSpec((tr, D_MODEL), lambda i, chip_ref: (i, 0))
    return pl.pallas_call(
        _behind(body, 6, after), name=name,
        grid_spec=pltpu.PrefetchScalarGridSpec(
            num_scalar_prefetch=1, grid=(rows // tr,),
            in_specs=[pl.BlockSpec((1, tr, D_MODEL), lambda i, chip_ref: (chip_ref[0], i, 0)),
                      pl.BlockSpec((3, tr, D_MODEL), lambda i, chip_ref: (0, i, 0)), spec, spec, spec]
            + [ANY] * len(after),
            out_specs=[spec] * 4 + [pl.BlockSpec((8, 128), lambda i, chip_ref: (0, 0))]),
        out_shape=[jax.ShapeDtypeStruct((rows, D_MODEL), F32)] * 4 + [jax.ShapeDtypeStruct((8, 128), F32)],
        compiler_params=_params(("arbitrary",)),
    )(chip, partial, recv, w, m, v, *after)


def _adamw_math(w, g, m, v):
    m = ADAM_B1 * m + (1.0 - ADAM_B1) * g
    v = ADAM_B2 * v + (1.0 - ADAM_B2) * (g * g)
    m_hat = m / (1.0 - ADAM_B1 ** ADAM_STEP)
    v_hat = v / (1.0 - ADAM_B2 ** ADAM_STEP)
    delta = -ADAM_LR * (m_hat / (jnp.sqrt(v_hat) + ADAM_EPS) + ADAM_WD * w)
    return delta, m, v


def _adamw(w, g, m, v, *, name, tr, after=()):
    rows, cols = w.shape

    def body(w_ref, g_ref, m_ref, v_ref, d_ref, mo_ref, vo_ref):
        d_ref[...], mo_ref[...], vo_ref[...] = _adamw_math(w_ref[...], g_ref[...], m_ref[...], v_ref[...])

    spec = pl.BlockSpec((tr, cols), lambda i: (i, 0))
    return pl.pallas_call(
        _behind(body, 4, after), name=name, grid=(rows // tr,), in_specs=[spec] * 4 + [ANY] * len(after),
        out_specs=[spec] * 3, out_shape=[jax.ShapeDtypeStruct(w.shape, F32)] * 3,
        compiler_params=_params(("parallel",)),
    )(w, g, m, v, *after)


def _sum_small(gathered, *, name):
    def body(g_ref, o_ref):
        acc = g_ref[0]
        for k in range(1, N_DEV):
            acc = acc + g_ref[k]
        o_ref[...] = acc

    return pl.pallas_call(body, name=name, out_shape=jax.ShapeDtypeStruct(gathered.shape[1:], F32))(gathered)


def kernel(x, ffn1_norm, ffn1_w_gate, ffn1_w_up, ffn1_w_down, mix_norm, w_in, conv_w, attn_sinks, w_out, ffn2_norm, ffn2_w_gate, ffn2_w_up, ffn2_w_down, final_norm, loss_target, m_ffn1_norm, m_ffn1_w_gate, m_ffn1_w_up, m_ffn1_w_down, m_mix_norm, m_w_in, m_conv_w, m_attn_sinks, m_w_out, m_ffn2_norm, m_ffn2_w_gate, m_ffn2_w_up, m_ffn2_w_down, m_final_norm, v_ffn1_norm, v_ffn1_w_gate, v_ffn1_w_up, v_ffn1_w_down, v_mix_norm, v_w_in, v_conv_w, v_attn_sinks, v_w_out, v_ffn2_norm, v_ffn2_w_gate, v_ffn2_w_up, v_ffn2_w_down, v_final_norm):
    ix, iy, ic = lax.axis_index("x"), lax.axis_index("y"), lax.axis_index("c")
    my_index = 4 * ix + 2 * iy + ic
    core = ic.astype(jnp.int32).reshape(1)
    chip = (2 * ix + iy).astype(jnp.int32).reshape(1)

    given = dict(ffn1_norm=ffn1_norm, ffn1_w_gate=ffn1_w_gate, ffn1_w_up=ffn1_w_up, ffn1_w_down=ffn1_w_down,
                 mix_norm=mix_norm, w_in=w_in, conv_w=conv_w, attn_sinks=attn_sinks, w_out=w_out, ffn2_norm=ffn2_norm,
                 ffn2_w_gate=ffn2_w_gate, ffn2_w_up=ffn2_w_up, ffn2_w_down=ffn2_w_down, final_norm=final_norm)
    moments_m = dict(ffn1_norm=m_ffn1_norm, ffn1_w_gate=m_ffn1_w_gate, ffn1_w_up=m_ffn1_w_up, ffn1_w_down=m_ffn1_w_down,
                     mix_norm=m_mix_norm, w_in=m_w_in, conv_w=m_conv_w, attn_sinks=m_attn_sinks, w_out=m_w_out,
                     ffn2_norm=m_ffn2_norm, ffn2_w_gate=m_ffn2_w_gate, ffn2_w_up=m_ffn2_w_up, ffn2_w_down=m_ffn2_w_down,
                     final_norm=m_final_norm)
    moments_v = dict(ffn1_norm=v_ffn1_norm, ffn1_w_gate=v_ffn1_w_gate, ffn1_w_up=v_ffn1_w_up, ffn1_w_down=v_ffn1_w_down,
                     mix_norm=v_mix_norm, w_in=v_w_in, conv_w=v_conv_w, attn_sinks=v_attn_sinks, w_out=v_w_out,
                     ffn2_norm=v_ffn2_norm, ffn2_w_gate=v_ffn2_w_gate, ffn2_w_up=v_ffn2_w_up, ffn2_w_down=v_ffn2_w_down,
                     final_norm=v_final_norm)

    xs = x[0]
    target = loss_target[0]
    final_gain = final_norm.reshape(1, D_MODEL)

    def ffn_shard(wg, wu, wd):
        return jnp.stack([wg[0].T, wu[0].T, wd[0]]).astype(BF16)

    conv_cols = conv_w.shape[2]
    conv_shard = jnp.pad(conv_w[0], ((0, 5), (0, 128 - conv_cols)))
    rest_shards = [ffn_shard(ffn2_w_gate, ffn2_w_up, ffn2_w_down), w_in[0].T.astype(BF16), w_out[0].astype(BF16),
                   conv_shard]
    rest_rows = [s.shape[-2] for s in rest_shards]
    n_rest = len(rest_shards)
    (w1,), _ = _all_gather_rows([ffn_shard(ffn1_w_gate, ffn1_w_up, ffn1_w_down)], name="gather_ffn1")

    fulls = [lax.empty(s.shape[:-2] + (N_DEV * s.shape[-2], s.shape[-1]), s.dtype) for s in rest_shards]
    fulls = _place_own(fulls, rest_shards, my_index.astype(jnp.int32).reshape(1), name="place_own_weights")
    send_plan = _gather_send_plan(n_rest)
    ssem, rsem, bufs, token = _split_start(rest_shards + list(fulls), 4 * n_rest, send_plan, name="gather_rest_start",
                                           after=[w1])
    x1, h1, s1, sa1, sb1 = _ffn_fwd(xs, ffn1_norm, w1, name="ffn1_fwd", after=[token])
    bufs = _split_wait(ssem, rsem, bufs, x1, send_plan, name="gather_rest_wait")
    w2_part, mixer_parts = bufs[n_rest], bufs[n_rest + 1:]
    fwd_mixer = _gather_forward_plan(rest_rows[1:])
    ssem, rsem, bufs, token = _split_start(mixer_parts, 3 * (n_rest - 1), fwd_mixer, name="forward_mixer_start")
    win_t, wout, conv_all = _split_wait(ssem, rsem, bufs, token, fwd_mixer, name="forward_mixer_wait")
    conv_full = conv_all.reshape(N_DEV, 8, 128)[:, :3, :conv_cols].transpose(1, 0, 2).reshape(3, CONV_W)
    fwd_ffn2 = _gather_forward_plan(rest_rows[:1])
    ssem, rsem, bufs, token = _split_start([w2_part], 3, fwd_ffn2, name="forward_ffn2_start", after=[win_t])
    rope = _rope_tables(xs.shape[0])
    x2, hm, z, y = _mixer_fwd(x1, mix_norm, win_t, wout, conv_full, attn_sinks, rope, name="mixer_fwd", after=[token])
    (w2,) = _split_wait(ssem, rsem, bufs, x2, fwd_ffn2, name="forward_ffn2_wait")
    dx3, h2, s2, sa2, sb2, loss_local, d_final = _ffn_fwd(x2, ffn2_norm, w2, head=(final_gain, target),
                                                          name="ffn2_fwd")

    def to_sibling_start(grads, tag, after=()):
        views = [g.reshape(4, 2, g.shape[0] // N_DEV, D_MODEL) for g in grads]
        lands = [lax.empty((4,) + v.shape[2:], F32) for v in views]
        plan = _sibling_plan(len(views))
        ssem, rsem, bufs, token = _split_start(views + lands, len(views), plan, name=f"{tag}_sibling_start", after=after)
        return (ssem, rsem, bufs, plan, tag), token

    def to_sibling_finish(handle, after, names):
        ssem, rsem, bufs, plan, tag = handle
        bufs = _split_wait(ssem, rsem, bufs, after, plan, name=f"{tag}_sibling_wait")
        n = len(names)
        return [_add_sibling(v, r, core, name=f"add_sibling_{nm}", tr=v.shape[2] // 2)
                for v, r, nm in zip(bufs[:n], bufs[n:], names)]

    def to_chips_start(partials, tag, small_all=None, after=()):
        p16 = [p for _, p in partials]
        lands = [lax.empty((3,) + p.shape[1:], BF16) for p in p16]
        extra = [] if small_all is None else [small_all]
        plan = _chips_plan(len(p16), small_all is not None)
        ssem, rsem, bufs, token = _split_start(p16 + lands + extra, 3 * len(p16) + 7 * len(extra), plan,
                                               name=f"{tag}_chips_start", after=after)
        return (ssem, rsem, bufs, plan, tag), token

    def to_chips_finish(handle, partials, after, names):
        ssem, rsem, bufs, plan, tag = handle
        bufs = _split_wait(ssem, rsem, bufs, after, plan, name=f"{tag}_chips_wait")
        n = len(names)
        return [(p32, r) for (p32, _), r in zip(partials, bufs[n:2 * n])], bufs[2 * n:]

    half_ff = D_FF // 2
    names2, namesm = ["ffn2_w_gate", "ffn2_w_up", "ffn2_w_down"], ["w_in", "w_out"]
    transposed = {"ffn1_w_gate", "ffn1_w_up", "w_in", "ffn2_w_gate", "ffn2_w_up"}
    grad, delta, new_m, new_v = {}, {}, {}, {}

    def adam_big(nm, parts, after=()):
        to_rows = (lambda a: jnp.swapaxes(a, 1, 2)[0]) if nm in transposed else (lambda a: a[0])
        from_rows = (lambda a: jnp.swapaxes(a[None], 1, 2)) if nm in transposed else (lambda a: a[None])
        p32, recv = parts
        *outs, token = _reduce_adamw(p32, recv, chip, to_rows(given[nm]), to_rows(moments_m[nm]),
                                     to_rows(moments_v[nm]), name=f"adamw_{nm}", tr=p32.shape[1] // 2, after=after)
        grad[nm], delta[nm], new_m[nm], new_v[nm] = (from_rows(a) for a in outs)
        return token

    dx2, da2, db2, g2b, d_norm2 = _ffn_dgrad(dx3, x2, ffn2_norm, sa2, sb2, w2, name="ffn2_dgrad")
    gw2 = [_tn_matmul(da2, h2, name="ffn2_wgrad_gate", bm=half_ff), _tn_matmul(db2, h2, name="ffn2_wgrad_up", bm=half_ff),
           _tn_matmul(s2, g2b, name="ffn2_wgrad_down", bm=half_ff)]
    sib2, tok = to_sibling_start(gw2, "ffn2")
    dx1, dz, gmb, d_conv, d_sink, d_normm = _mixer_bwd(dx2, x1, mix_norm, y, z, win_t, wout, conv_full, attn_sinks,
                                                       rope, name="mixer_bwd", after=[tok])
    p2 = to_sibling_finish(sib2, dx1, names2)
    chips2, tok = to_chips_start(p2, "ffn2")
    gwm = [_tn_matmul(dz, hm, name="mixer_wgrad_in", bm=Z_W // 3, after=[tok]),
           _tn_matmul(y, gmb, name="mixer_wgrad_out", bm=D_MODEL // 2, after=[tok])]
    sibm, tok = to_sibling_start(gwm, "mixer")
    dx0, da1, db1, g1b, d_norm1 = _ffn_dgrad(dx1, xs, ffn1_norm, sa1, sb1, w1, name="ffn1_dgrad", after=[tok])
    r2, _ = to_chips_finish(chips2, p2, dx0, names2)
    pm = to_sibling_finish(sibm, dx0, namesm)
    chipsm, tok = to_chips_start(pm, "mixer")
    gw_gate = _tn_matmul(da1, h1, name="ffn1_wgrad_gate", bm=half_ff, after=[tok])
    sib_gate, tok = to_sibling_start([gw_gate], "ffn1_gate")
    gw_up = _tn_matmul(db1, h1, name="ffn1_wgrad_up", bm=half_ff, after=[tok])
    rm, _ = to_chips_finish(chipsm, pm, gw_up, namesm)
    p_gate = to_sibling_finish(sib_gate, gw_up, ["ffn1_w_gate"])
    chips_gate, tok_a = to_chips_start(p_gate, "ffn1_gate")
    sib_up, tok_b = to_sibling_start([gw_up], "ffn1_up", after=[tok_a])
    gw_down = _tn_matmul(s1, g1b, name="ffn1_wgrad_down", bm=half_ff, after=[tok_a, tok_b])
    p_up = to_sibling_finish(sib_up, gw_down, ["ffn1_w_up"])
    chips_up, tok_a = to_chips_start(p_up, "ffn1_up")
    sib_down, tok_b = to_sibling_start([gw_down], "ffn1_down", after=[tok_a])
    p_down = to_sibling_finish(sib_down, tok_b, ["ffn1_w_down"])
    last_row = (jnp.pad(d_sink, ((0, 0), (0, D_MODEL - 128)))
                + jnp.pad(loss_local, ((0, 0), (LOSS_LANE, D_MODEL - LOSS_LANE - 1))))
    small = jnp.concatenate([
        d_norm1, d_normm, d_norm2, d_final, jnp.pad(d_conv[0:3], ((0, 0), (0, D_MODEL - CONV_W))), last_row], axis=0)
    (small_all,) = _place_own([lax.empty((N_DEV * 8, D_MODEL), F32)], [small], my_index.astype(jnp.int32).reshape(1),
                              name="place_own_small")
    chips_down, tok = to_chips_start(p_down, "ffn1_down", small_all)
    for nm, g in zip(names2 + namesm, r2 + rm):
        tok = adam_big(nm, g, after=[tok])
    r_gate, _ = to_chips_finish(chips_gate, p_gate, tok, ["ffn1_w_gate"])
    tok = adam_big("ffn1_w_gate", r_gate[0])
    r_up, _ = to_chips_finish(chips_up, p_up, tok, ["ffn1_w_up"])
    tok = adam_big("ffn1_w_up", r_up[0])
    r_down, (small_all,) = to_chips_finish(chips_down, p_down, tok, ["ffn1_w_down"])
    adam_big("ffn1_w_down", r_down[0])
    small_sum = _sum_small(small_all.reshape(N_DEV, 8, D_MODEL), name="sum_small")
    loss = small_sum[7, LOSS_LANE]
    _update_small(given, moments_m, moments_v, small_sum, my_index, grad, delta, new_m, new_v)

    order = list(given)
    return (loss, dx0[None], *[grad[n] for n in order], *[delta[n] for n in order],
            *[new_m[n] for n in order], *[new_v[n] for n in order])


def _update_small(given, moments_m, moments_v, small_sum, my_index, grad, delta, new_m, new_v):
    conv_cols = given["conv_w"].shape[2]
    small_g = {
        "ffn1_norm": small_sum[0:1], "mix_norm": small_sum[1:2], "ffn2_norm": small_sum[2:3],
        "final_norm": small_sum[3:4],
        "conv_w": lax.dynamic_slice(small_sum[4:7, :CONV_W], (0, my_index * conv_cols), (3, conv_cols)),
        "attn_sinks": small_sum[7:8, :N_Q_HEADS],
    }

    small_names = ["ffn1_norm", "mix_norm", "ffn2_norm", "final_norm", "conv_w", "attn_sinks"]

    def pack(parts):
        rows = []
        for nm in small_names:
            p = parts[nm]
            p2 = p.reshape(3, conv_cols) if nm == "conv_w" else p.reshape(1, -1)
            rows.append(jnp.pad(p2, ((0, 0), (0, D_MODEL - p2.shape[1]))))
        rows.append(jnp.zeros((8, D_MODEL), F32))
        return jnp.concatenate(rows, axis=0)

    sd, sm, sv = _adamw(pack(given), pack(small_g), pack(moments_m), pack(moments_v), name="adamw_small", tr=16)
    row = 0
    for nm in small_names:
        shape = given[nm].shape
        nrow = 3 if nm == "conv_w" else 1
        ncol = conv_cols if nm == "conv_w" else given[nm].size
        grad[nm] = small_g[nm].reshape(shape)
        delta[nm], new_m[nm], new_v[nm] = (a[row:row + nrow, :ncol].reshape(shape) for a in (sd, sm, sv))
        row += nrow
```

```python
import functools

import jax
import jax.numpy as jnp
from jax import lax
from jax.experimental import pallas as pl
from jax.experimental.pallas import tpu as pltpu

F32 = jnp.float32
BF16 = jnp.bfloat16
MESH = pl.DeviceIdType.MESH
ANY = pl.BlockSpec(memory_space=pl.ANY)
HBM_SPEC = pl.BlockSpec(memory_space=pltpu.HBM)
SEM_SPEC = pl.BlockSpec(memory_space=pltpu.SEMAPHORE)
DATAFLOW = pltpu.SideEffectType.DATAFLOW_SIDE_EFFECTING

N_DEV = 8
LOSS_LANE = 128
D_MODEL = 1024
D_FF = 2816
CONV_W = 512
ATTN_W = 512
KV_W = 128
HEAD_DIM = 64
N_Q_HEADS = 8
N_KV_HEADS = 2
Q_PER_KV = N_Q_HEADS // N_KV_HEADS
BLOCK = 128
ROT_DIM = 16
ROPE_THETA = 500000.0
Z_W = 3 * CONV_W + ATTN_W + 2 * KV_W
Q_OFF = 3 * CONV_W
K_OFF = Q_OFF + ATTN_W
V_OFF = K_OFF + KV_W
RMS_EPS = 1e-5
MASK_VALUE = -1e30
SM_SCALE = HEAD_DIM ** -0.5
FFN_RES_SCALE = 0.5

ADAM_LR = 0.001
ADAM_B1 = 0.9
ADAM_B2 = 0.999
ADAM_EPS = 1e-08
ADAM_WD = 0.01
ADAM_STEP = 10

NT_DIMS = (((1,), (1,)), ((), ()))
TN_DIMS = (((0,), (0,)), ((), ()))

VMEM_LIMIT = 62 * 1024 * 1024
FF_CHUNK = 256


def _params(sem, vmem=None):
    return pltpu.CompilerParams(dimension_semantics=sem, vmem_limit_bytes=vmem)


def _behind(body, n_in, after):
    k = len(after)
    if k == 0:
        return body
    return lambda *refs: body(*refs[:n_in], *refs[n_in + k:])


def _rms_stats(xf):
    inv = lax.rsqrt(jnp.mean(xf * xf, axis=-1, keepdims=True) + RMS_EPS)
    return xf * inv, inv


def _rms_bwd(dh, xhat, inv, gain):
    dxhat = dh * gain
    dx = inv * (dxhat - xhat * jnp.mean(dxhat * xhat, axis=-1, keepdims=True))
    dgain = jnp.sum(dh * xhat, axis=0, keepdims=True)
    return dx, dgain


def _load_resident(w_hbm, w_ref, sem):
    @pl.when(pl.program_id(0) == 0)
    def _():
        cp = pltpu.make_async_copy(w_hbm, w_ref, sem)
        cp.start()
        cp.wait()


def _ffn_fwd(x, gain, w3, *, name, head=None, after=(), tm=256, sub=256, tf=FF_CHUNK):
    t = x.shape[0]
    tm = min(tm, t)
    sub = min(sub, tm)
    n_head = 0 if head is None else 2

    def body(*refs):
        x_ref, g_ref, w_hbm = refs[:3]
        head_refs = refs[3:3 + n_head]
        xo_ref, h_ref, s_ref, sa_ref, sb_ref = refs[3 + n_head:8 + n_head]
        head_outs = refs[8 + n_head:8 + 2 * n_head]
        w_ref, sem = refs[8 + 2 * n_head:]
        _load_resident(w_hbm, w_ref, sem)

        @pl.when(pl.program_id(0) == 0)
        def _():
            for ref in head_outs:
                ref[...] = jnp.zeros_like(ref)

        for r0 in range(0, tm, sub):
            rows = slice(r0, r0 + sub)
            xf = x_ref[rows, :]
            xhat, _ = _rms_stats(xf)
            h = (xhat * g_ref[...]).astype(BF16)
            h_ref[rows, :] = h
            for c in range(0, D_FF, tf):
                cols = slice(c, min(c + tf, D_FF))
                a = lax.dot_general(h, w_ref[0, cols, :], NT_DIMS, preferred_element_type=F32)
                b = lax.dot_general(h, w_ref[1, cols, :], NT_DIMS, preferred_element_type=F32)
                sig = jax.nn.sigmoid(a)
                silu = a * sig
                s_ref[rows, cols] = (silu * b).astype(BF16)
                sa_ref[rows, cols] = (b * (sig * (1.0 + a * (1.0 - sig)))).astype(BF16)
                sb_ref[rows, cols] = silu.astype(BF16)
            xo = xf + FFN_RES_SCALE * jnp.dot(s_ref[rows, :], w_ref[2], preferred_element_type=F32)
            if head is None:
                xo_ref[rows, :] = xo
            else:
                fg_ref, t_ref = head_refs
                loss_ref, dfg_ref = head_outs
                xhat_o, inv_o = _rms_stats(xo)
                err = xhat_o * fg_ref[...] - t_ref[rows, :]
                loss_ref[...] += 0.5 * jnp.sum(jnp.mean(err * err, axis=-1, keepdims=True), axis=0, keepdims=True)
                xo_ref[rows, :], dfg = _rms_bwd(err * (1.0 / D_MODEL), xhat_o, inv_o, fg_ref[...])
                dfg_ref[...] += dfg

    row = pl.BlockSpec((tm, D_MODEL), lambda i: (i, 0))
    hid = pl.BlockSpec((tm, D_FF), lambda i: (i, 0))
    vec = pl.BlockSpec((1, D_MODEL), lambda i: (0, 0))
    head_in = [] if head is None else [vec, row]
    head_out = [] if head is None else [pl.BlockSpec((1, 1), lambda i: (0, 0)), vec]
    head_shape = [] if head is None else [jax.ShapeDtypeStruct((1, 1), F32), jax.ShapeDtypeStruct((1, D_MODEL), F32)]
    return pl.pallas_call(
        _behind(body, 3 + n_head, after), name=name, grid=(t // tm,),
        in_specs=[row, vec, ANY] + head_in + [ANY] * len(after),
        out_specs=[row, row, hid, hid, hid] + head_out,
        out_shape=[jax.ShapeDtypeStruct((t, D_MODEL), F32), jax.ShapeDtypeStruct((t, D_MODEL), BF16)]
        + [jax.ShapeDtypeStruct((t, D_FF), BF16)] * 3 + head_shape,
        scratch_shapes=[pltpu.VMEM((3, D_FF, D_MODEL), BF16), pltpu.SemaphoreType.DMA(())],
        compiler_params=_params(("arbitrary",), VMEM_LIMIT),
    )(x, gain, w3, *(head or ()), *after)


def _ffn_dgrad(dxo, x, gain, sa, sb, w3, *, name, after=(), tm=512, sub=512, tf=FF_CHUNK):
    t = x.shape[0]
    tm = min(tm, t)
    sub = min(sub, tm)

    def body(dxo_ref, x_ref, g_ref, sa_ref, sb_ref, w_hbm, dxi_ref, da_ref, db_ref, gb_ref, dg_ref, w_ref, sem):
        _load_resident(w_hbm, w_ref, sem)

        @pl.when(pl.program_id(0) == 0)
        def _():
            dg_ref[...] = jnp.zeros_like(dg_ref)

        for r0 in range(0, tm, sub):
            rows = slice(r0, r0 + sub)
            go = dxo_ref[rows, :]
            gb = (FFN_RES_SCALE * go).astype(BF16)
            gb_ref[rows, :] = gb
            for c in range(0, D_FF, tf):
                cols = slice(c, min(c + tf, D_FF))
                ds = lax.dot_general(gb, w_ref[2, cols, :], NT_DIMS, preferred_element_type=F32)
                da_ref[rows, cols] = (ds * sa_ref[rows, cols].astype(F32)).astype(BF16)
                db_ref[rows, cols] = (ds * sb_ref[rows, cols].astype(F32)).astype(BF16)
            dh = (jnp.dot(da_ref[rows, :], w_ref[0], preferred_element_type=F32)
                  + jnp.dot(db_ref[rows, :], w_ref[1], preferred_element_type=F32))
            xhat, inv = _rms_stats(x_ref[rows, :])
            dx, dgain = _rms_bwd(dh, xhat, inv, g_ref[...])
            dxi_ref[rows, :] = go + dx
            dg_ref[...] += dgain

    row = pl.BlockSpec((tm, D_MODEL), lambda i: (i, 0))
    hid = pl.BlockSpec((tm, D_FF), lambda i: (i, 0))
    vec = pl.BlockSpec((1, D_MODEL), lambda i: (0, 0))
    return pl.pallas_call(
        _behind(body, 6, after), name=name, grid=(t // tm,),
        in_specs=[row, row, vec, hid, hid, ANY] + [ANY] * len(after),
        out_specs=[row, hid, hid, row, vec],
        out_shape=[jax.ShapeDtypeStruct((t, D_MODEL), F32), jax.ShapeDtypeStruct((t, D_FF), BF16),
                   jax.ShapeDtypeStruct((t, D_FF), BF16),
                   jax.ShapeDtypeStruct((t, D_MODEL), BF16), jax.ShapeDtypeStruct((1, D_MODEL), F32)],
        scratch_shapes=[pltpu.VMEM((3, D_FF, D_MODEL), BF16), pltpu.SemaphoreType.DMA(())],
        compiler_params=_params(("arbitrary",), VMEM_LIMIT),
    )(dxo, x, gain, sa, sb, w3, *after)


def _tn_matmul(a, b, *, name, bm, after=(), tk=2048):
    t, m = a.shape
    n = b.shape[1]
    tk = min(tk, t)
    nk = t // tk

    def body(a_ref, b_ref, o_ref):
        @pl.when(pl.program_id(1) == 0)
        def _():
            o_ref[...] = jnp.zeros_like(o_ref)

        o_ref[...] += lax.dot_general(a_ref[...], b_ref[...], TN_DIMS, preferred_element_type=F32)

    return pl.pallas_call(
        _behind(body, 2, after), name=name, grid=(m // bm, nk),
        in_specs=[pl.BlockSpec((tk, bm), lambda i, k: (k, i)), pl.BlockSpec((tk, n), lambda i, k: (k, 0))]
        + [ANY] * len(after),
        out_specs=pl.BlockSpec((bm, n), lambda i, k: (i, 0)),
        out_shape=jax.ShapeDtypeStruct((m, n), F32),
        compiler_params=_params(("parallel", "arbitrary"), VMEM_LIMIT),
    )(a, b, *after)


def _rope_tables(t):
    inv_freq = ROPE_THETA ** (-jnp.arange(0, ROT_DIM, 2, dtype=F32) / ROT_DIM)
    ang = inv_freq[:, None] * jnp.arange(t, dtype=F32)[None, :]
    return jnp.stack([jnp.tile(jnp.cos(ang).T, (1, 16)), jnp.tile(jnp.sin(ang).T, (1, 16))])


def _rope_multipliers(cos_sin):
    half = ROT_DIM // 2
    cos, sin = cos_sin[0], cos_sin[1]
    d = lax.broadcasted_iota(jnp.int32, cos.shape, 1) & (HEAD_DIM - 1)
    mult = jnp.where(d < ROT_DIM, cos, 1.0)
    from_lo = jnp.where((d >= half) & (d < ROT_DIM), sin, 0.0)
    from_hi = jnp.where(d < half, -sin, 0.0)
    return mult, from_lo, from_hi


def _tile_lanes(tab, width):
    return jnp.tile(tab, (1, width // tab.shape[1]))


def _rope(v, tab):
    w = v.shape[1]
    half_rot = ROT_DIM // 2
    return (v * _tile_lanes(tab[0], w)
            + pltpu.roll(v, half_rot, axis=1) * _tile_lanes(tab[1], w)
            + pltpu.roll(v, w - half_rot, axis=1) * _tile_lanes(tab[2], w))


def _rope_bwd(dv, tab):
    w = dv.shape[1]
    half_rot = ROT_DIM // 2
    return (dv * _tile_lanes(tab[0], w)
            + pltpu.roll(dv * _tile_lanes(tab[1], w), w - half_rot, axis=1)
            + pltpu.roll(dv * _tile_lanes(tab[2], w), half_rot, axis=1))


def _shift_rows(v, prev8_ref, n):
    r = lax.broadcasted_iota(jnp.int32, v.shape, 0)
    rolled = pltpu.roll(v, n, axis=0)
    last = prev8_ref[7:8, :]
    if n == 1:
        return jnp.where(r >= 1, rolled, last)
    return jnp.where(r >= 2, rolled, jnp.where(r == 0, prev8_ref[6:7, :], last))


def _shift_rows_up(v, next8_ref, n):
    rows = v.shape[0]
    r = lax.broadcasted_iota(jnp.int32, v.shape, 0)
    rolled = pltpu.roll(v, rows - n, axis=0)
    first = next8_ref[0:1, :]
    if n == 1:
        return jnp.where(r <= rows - 2, rolled, first)
    return jnp.where(r <= rows - 3, rolled, jnp.where(r == rows - 2, first, next8_ref[1:2, :]))


def _lane_half_mask(shape, half):
    lane = lax.broadcasted_iota(jnp.int32, shape, 1)
    return (lane >= HEAD_DIM) if half else (lane < HEAD_DIM)


def _to_kv_lanes(chunk, head, kv):
    if head % 2 != kv:
        chunk = pltpu.roll(chunk, HEAD_DIM, axis=1)
    return jnp.where(_lane_half_mask(chunk.shape, kv), chunk, 0.0)


def _from_kv_lanes(chunk, head, kv):
    chunk = jnp.where(_lane_half_mask(chunk.shape, kv), chunk, 0.0)
    if head % 2 != kv:
        chunk = pltpu.roll(chunk, HEAD_DIM, axis=1)
    return chunk


def _stack_heads(wide):
    parts = []
    for head in range(N_Q_HEADS):
        chunk = wide[:, (head // 2) * 128:(head // 2 + 1) * 128]
        parts.append(_to_kv_lanes(chunk, head, head // Q_PER_KV))
    return jnp.concatenate(parts, axis=0)


def _window_mask(has_prev):
    shape = (N_Q_HEADS * BLOCK, 2 * BLOCK)
    qi = lax.broadcasted_iota(jnp.int32, shape, 0) & (BLOCK - 1)
    kj = lax.broadcasted_iota(jnp.int32, shape, 1)
    first_key = BLOCK - has_prev * BLOCK
    in_prev = (kj < BLOCK) & (kj > qi) & (kj >= first_key)
    in_own = (kj >= BLOCK) & ((kj - BLOCK) <= qi)
    return in_prev | in_own


def _sink_column(sink_ref):
    row = lax.broadcasted_iota(jnp.int32, (N_Q_HEADS * BLOCK, 1), 0)
    col = jnp.full((N_Q_HEADS * BLOCK, 1), sink_ref[0, 0], F32)
    for head in range(1, N_Q_HEADS):
        col = jnp.where(row >= head * BLOCK, sink_ref[0, head], col)
    return col


def _softmax_with_sink(q4, k2, mask, sink):
    s = lax.dot_general(q4, k2, NT_DIMS, preferred_element_type=F32) * SM_SCALE
    s = jnp.where(mask, s, MASK_VALUE)
    m = jnp.maximum(jnp.max(s, axis=-1, keepdims=True), sink)
    p = jnp.exp(s - m)
    e_sink = jnp.exp(sink - m)
    inv_den = 1.0 / (jnp.sum(p, axis=-1, keepdims=True) + e_sink)
    return p * inv_den, e_sink * inv_den


def _conv_terms(zf, prev8_ref, w_ref):
    b_gate, c_gate, u = zf[:, 0:CONV_W], zf[:, CONV_W:2 * CONV_W], zf[:, 2 * CONV_W:3 * CONV_W]
    vc = c_gate * u
    vm1 = _shift_rows(vc, prev8_ref, 1)
    vm2 = _shift_rows(vc, prev8_ref, 2)
    conv = w_ref[0:1, :] * vm2 + w_ref[1:2, :] * vm1 + w_ref[2:3, :] * vc
    return b_gate, c_gate, u, vc, vm1, vm2, conv


def _mixer_fwd(x, gain, win_t, wout, conv_w, sinks, rope, *, name, after=(), tq=512):
    t = x.shape[0]
    tq = min(tq, t)
    nblk = tq // BLOCK

    def body(x_ref, g_ref, win_hbm, wout_hbm, cw_ref, sink_ref, rope_ref,
             xo_ref, h_ref, z_ref, y_ref, kprev_ref, vprev_ref, cprev_ref, win_ref, wout_ref, sems):
        i = pl.program_id(0)
        _load_resident(win_hbm, win_ref, sems.at[0])
        _load_resident(wout_hbm, wout_ref, sems.at[1])

        @pl.when(i == 0)
        def _():
            kprev_ref[...] = jnp.zeros_like(kprev_ref)
            vprev_ref[...] = jnp.zeros_like(vprev_ref)
            cprev_ref[...] = jnp.zeros_like(cprev_ref)

        xf = x_ref[...]
        xhat, _ = _rms_stats(xf)
        h = (xhat * g_ref[...]).astype(BF16)
        h_ref[...] = h
        zb = lax.dot_general(h, win_ref[...], NT_DIMS, preferred_element_type=F32).astype(BF16)
        z_ref[...] = zb
        zf = zb.astype(F32)

        b_gate, _, _, vc, _, _, conv = _conv_terms(zf, cprev_ref, cw_ref)
        y_conv = b_gate * conv
        cprev_ref[...] = vc[tq - 8:tq, :]

        tab = _rope_multipliers(rope_ref[...])
        qr = _rope(zf[:, Q_OFF:K_OFF], tab)
        kr = _rope(zf[:, K_OFF:V_OFF], tab).astype(BF16)
        vb = zb[:, V_OFF:Z_W]

        y_attn = []
        for j in range(nblk):
            rows = slice(j * BLOCK, (j + 1) * BLOCK)
            prev = slice((j - 1) * BLOCK, j * BLOCK)
            k2 = jnp.concatenate([kprev_ref[...] if j == 0 else kr[prev], kr[rows]], axis=0)
            v2 = jnp.concatenate([vprev_ref[...] if j == 0 else vb[prev], vb[rows]], axis=0)
            mask = _window_mask(jnp.minimum(i, 1) if j == 0 else 1)
            q8 = _stack_heads(qr[rows]).astype(BF16)
            probs, _ = _softmax_with_sink(q8, k2, mask, _sink_column(sink_ref))
            o8 = jnp.dot(probs.astype(BF16), v2, preferred_element_type=F32)
            chunks = [jnp.zeros((BLOCK, 128), F32) for _ in range(ATTN_W // 128)]
            for head in range(N_Q_HEADS):
                chunks[head // 2] += _from_kv_lanes(o8[head * BLOCK:(head + 1) * BLOCK], head, head // Q_PER_KV)
            y_attn.append(jnp.concatenate(chunks, axis=1))
        kprev_ref[...] = kr[tq - BLOCK:tq]
        vprev_ref[...] = vb[tq - BLOCK:tq]
        y = jnp.concatenate([y_conv, jnp.concatenate(y_attn, axis=0)], axis=1).astype(BF16)
        y_ref[...] = y
        xo_ref[...] = xf + jnp.dot(y, wout_ref[...], preferred_element_type=F32)

    row = pl.BlockSpec((tq, D_MODEL), lambda i: (i, 0))
    full = lambda shape: pl.BlockSpec(shape, lambda i: (0,) * len(shape))
    return pl.pallas_call(
        _behind(body, 7, after), name=name, grid=(t // tq,),
        in_specs=[row, full((1, D_MODEL)), ANY, ANY, full((3, CONV_W)),
                  pl.BlockSpec(memory_space=pltpu.SMEM), pl.BlockSpec((2, tq, 128), lambda i: (0, i, 0))]
        + [ANY] * len(after),
        out_specs=[row, row, pl.BlockSpec((tq, Z_W), lambda i: (i, 0)), row],
        out_shape=[jax.ShapeDtypeStruct((t, D_MODEL), F32), jax.ShapeDtypeStruct((t, D_MODEL), BF16),
                   jax.ShapeDtypeStruct((t, Z_W), BF16), jax.ShapeDtypeStruct((t, D_MODEL), BF16)],
        scratch_shapes=[pltpu.VMEM((BLOCK, KV_W), BF16), pltpu.VMEM((BLOCK, KV_W), BF16),
                        pltpu.VMEM((8, CONV_W), F32), pltpu.VMEM((Z_W, D_MODEL), BF16),
                        pltpu.VMEM((D_MODEL, D_MODEL), BF16), pltpu.SemaphoreType.DMA((2,))],
        compiler_params=_params(("arbitrary",), VMEM_LIMIT),
    )(x, gain, win_t, wout, conv_w, sinks, rope, *after)


def _mixer_bwd(dxo, x, gain, y, z, win_t, wout, conv_w, sinks, rope, *, name, after=(), tq=256):
    t = x.shape[0]
    tq = min(tq, t)
    nt, nblk = t // tq, tq // BLOCK

    def body(dxo_ref, x_ref, g_ref, y_ref, z_ref, zp_ref, win_hbm, wout_hbm, cw_ref, sink_ref, rope_ref, ropep_ref,
             dxi_ref, dz_ref, gb_ref, dcw_ref, dsink_ref, dg_ref, dk_ref, dv_ref, dcn_ref, pvc_ref,
             win_ref, wout_ref, sems):
        i = pl.program_id(0)
        tile = nt - 1 - i
        _load_resident(win_hbm, win_ref, sems.at[0])
        _load_resident(wout_hbm, wout_ref, sems.at[1])

        @pl.when(i == 0)
        def _():
            dk_ref[...] = jnp.zeros_like(dk_ref)
            dv_ref[...] = jnp.zeros_like(dv_ref)
            dcn_ref[...] = jnp.zeros_like(dcn_ref)
            dcw_ref[...] = jnp.zeros_like(dcw_ref)
            dsink_ref[...] = jnp.zeros_like(dsink_ref)
            dg_ref[...] = jnp.zeros_like(dg_ref)

        has_prev = jnp.minimum(tile, 1)
        go = dxo_ref[...]
        gb = go.astype(BF16)
        gb_ref[...] = gb
        dy = lax.dot_general(gb, wout_ref[...], NT_DIMS, preferred_element_type=F32)
        dy_conv, dy_attn = dy[:, 0:CONV_W], dy[:, CONV_W:D_MODEL]
        zb, zpb = z_ref[...], zp_ref[...]
        zf = zb.astype(F32)
        zpf = zpb.astype(F32) * has_prev.astype(F32)

        pvc_ref[...] = (zpf[:, CONV_W:2 * CONV_W] * zpf[:, 2 * CONV_W:3 * CONV_W])[BLOCK - 8:BLOCK, :]
        b_gate, c_gate, u, vc, vm1, vm2, conv = _conv_terms(zf, pvc_ref, cw_ref)
        d_bgate = dy_conv * conv
        dc = dy_conv * b_gate
        tap = lax.broadcasted_iota(jnp.int32, (8, CONV_W), 0)
        dcw_ref[...] += jnp.where(tap == 0, jnp.sum(dc * vm2, axis=0, keepdims=True),
                                  jnp.where(tap == 1, jnp.sum(dc * vm1, axis=0, keepdims=True),
                                            jnp.where(tap == 2, jnp.sum(dc * vc, axis=0, keepdims=True), 0.0)))
        dvc = (cw_ref[2:3, :] * dc + cw_ref[1:2, :] * _shift_rows_up(dc, dcn_ref, 1)
               + cw_ref[0:1, :] * _shift_rows_up(dc, dcn_ref, 2))
        dcn_ref[...] = dc[0:8, :]
        d_cgate = dvc * u
        d_u = dvc * c_gate

        tab, tabp = _rope_multipliers(rope_ref[...]), _rope_multipliers(ropep_ref[...])
        qr = _rope(zf[:, Q_OFF:K_OFF], tab)
        kr = _rope(zf[:, K_OFF:V_OFF], tab).astype(BF16)
        kpr = _rope(zpf[:, K_OFF:V_OFF], tabp).astype(BF16)
        vb, vpb = zb[:, V_OFF:Z_W], zpb[:, V_OFF:Z_W]
        out = y_ref[:, CONV_W:D_MODEL].astype(F32)
        do_out = dy_attn * out
        lane = lax.broadcasted_iota(jnp.int32, (1, 128), 1)
        dsink = jnp.zeros((1, 128), F32)
        dk_next, dv_next = dk_ref[...], dv_ref[...]
        dq_rows, dk_rows, dv_rows = [None] * nblk, [None] * nblk, [None] * nblk
        for j in reversed(range(nblk)):
            rows = slice(j * BLOCK, (j + 1) * BLOCK)
            prev = slice((j - 1) * BLOCK, j * BLOCK)
            k2 = jnp.concatenate([kpr if j == 0 else kr[prev], kr[rows]], axis=0)
            v2 = jnp.concatenate([vpb if j == 0 else vb[prev], vb[rows]], axis=0)
            mask = _window_mask(has_prev if j == 0 else 1)
            q8 = _stack_heads(qr[rows]).astype(BF16)
            do8 = _stack_heads(dy_attn[rows]).astype(BF16)
            delta = jnp.sum(_stack_heads(do_out[rows]), axis=-1, keepdims=True)
            probs, p_sink = _softmax_with_sink(q8, k2, mask, _sink_column(sink_ref))
            dp = lax.dot_general(do8, v2, NT_DIMS, preferred_element_type=F32)
            ds = (probs * (dp - delta) * SM_SCALE).astype(BF16)
            dq8 = jnp.dot(ds, k2, preferred_element_type=F32)
            dk2 = lax.dot_general(ds, q8, TN_DIMS, preferred_element_type=F32)
            dv2 = lax.dot_general(probs.astype(BF16), do8, TN_DIMS, preferred_element_type=F32)
            sink_terms = p_sink * delta
            dq_chunks = [jnp.zeros((BLOCK, 128), F32) for _ in range(ATTN_W // 128)]
            for head in range(N_Q_HEADS):
                grp = slice(head * BLOCK, (head + 1) * BLOCK)
                dq_chunks[head // 2] += _from_kv_lanes(dq8[grp], head, head // Q_PER_KV)
                dsink = dsink - jnp.where(lane == head, jnp.sum(sink_terms[grp], axis=0, keepdims=True), 0.0)
            dq_rows[j] = jnp.concatenate(dq_chunks, axis=1)
            dk_rows[j] = dk2[BLOCK:] + dk_next
            dv_rows[j] = dv2[BLOCK:] + dv_next
            dk_next, dv_next = dk2[:BLOCK], dv2[:BLOCK]
        dk_ref[...] = dk_next
        dv_ref[...] = dv_next
        dsink_ref[...] += dsink
        dq = _rope_bwd(jnp.concatenate(dq_rows, axis=0), tab)
        dk = _rope_bwd(jnp.concatenate(dk_rows, axis=0), tab)
        dv = jnp.concatenate(dv_rows, axis=0)

        dzb = jnp.concatenate([d_bgate, d_cgate, d_u, dq, dk, dv], axis=1).astype(BF16)
        dz_ref[...] = dzb
        dh = jnp.dot(dzb, win_ref[...], preferred_element_type=F32)
        xhat, inv = _rms_stats(x_ref[...])
        dx, dgain = _rms_bwd(dh, xhat, inv, g_ref[...])
        dxi_ref[...] = go + dx
        dg_ref[...] += dgain

    rev = lambda i: (nt - 1 - i, 0)
    block_before = lambda i: jnp.maximum((nt - 1 - i) * nblk - 1, 0)
    row = pl.BlockSpec((tq, D_MODEL), rev)
    full = lambda shape: pl.BlockSpec(shape, lambda i: (0,) * len(shape))
    return pl.pallas_call(
        _behind(body, 12, after), name=name, grid=(nt,),
        in_specs=[row, row, full((1, D_MODEL)), row,
                  pl.BlockSpec((tq, Z_W), rev), pl.BlockSpec((BLOCK, Z_W), lambda i: (block_before(i), 0)),
                  ANY, ANY, full((3, CONV_W)),
                  pl.BlockSpec(memory_space=pltpu.SMEM),
                  pl.BlockSpec((2, tq, 128), lambda i: (0, nt - 1 - i, 0)),
                  pl.BlockSpec((2, BLOCK, 128), lambda i: (0, block_before(i), 0))] + [ANY] * len(after),
        out_specs=[row, pl.BlockSpec((tq, Z_W), rev), row, full((8, CONV_W)), full((1, 128)), full((1, D_MODEL))],
        out_shape=[jax.ShapeDtypeStruct((t, D_MODEL), F32), jax.ShapeDtypeStruct((t, Z_W), BF16),
                   jax.ShapeDtypeStruct((t, D_MODEL), BF16), jax.ShapeDtypeStruct((8, CONV_W), F32),
                   jax.ShapeDtypeStruct((1, 128), F32), jax.ShapeDtypeStruct((1, D_MODEL), F32)],
        scratch_shapes=[pltpu.VMEM((BLOCK, KV_W), F32), pltpu.VMEM((BLOCK, KV_W), F32), pltpu.VMEM((8, CONV_W), F32),
                        pltpu.VMEM((8, CONV_W), F32), pltpu.VMEM((Z_W, D_MODEL), BF16),
                        pltpu.VMEM((D_MODEL, D_MODEL), BF16), pltpu.SemaphoreType.DMA((2,))],
        compiler_params=_params(("arbitrary",), VMEM_LIMIT),
    )(dxo, x, gain, y, z, z, win_t, wout, conv_w, sinks, rope, rope, *after)


def _place():
    x, y, c = lax.axis_index("x"), lax.axis_index("y"), lax.axis_index("c")
    other_chips = [(1 - x, y), (x, 1 - y), (1 - x, 1 - y)]
    return x, y, c, other_chips


def _all_gather_rows(shards, place=(), *, name):
    n, p = len(shards), len(place)

    def body(*refs):
        srcs, place_srcs = refs[:n], refs[n:n + p]
        outs, place_outs = refs[n + p:2 * n + p], refs[2 * n + p:2 * (n + p)]
        send_sems, recv_sems, local_sems = refs[2 * (n + p):]
        x, y, c, chips = _place()
        me, sibling = (x, y, c), (x, y, 1 - c)

        def rows(t, px, py, pc):
            r = srcs[t].shape[-2]
            start = pl.multiple_of((4 * px + 2 * py + pc) * r, 16 if r % 16 == 0 else 8)
            if len(srcs[t].shape) == 3:
                return outs[t].at[:, pl.ds(start, r), :]
            return outs[t].at[pl.ds(start, r), :]

        def copy(t, k, block, to, own=False):
            return pltpu.make_async_remote_copy(
                src_ref=srcs[t] if own else rows(t, *block), dst_ref=rows(t, *block),
                send_sem=send_sems.at[t, k], recv_sem=recv_sems.at[t, k], device_id=to, device_id_type=MESH)

        mine = [pltpu.make_async_copy(srcs[t], rows(t, *me), local_sems.at[t]) for t in range(n)]
        mine += [pltpu.make_async_copy(place_srcs[q],
                                       _block_rows(place_outs[q], place_srcs[q].shape[-2], 4 * x + 2 * y + c),
                                       local_sems.at[n + q]) for q in range(p)]
        for q in range(p):
            mine[n + q].start()
        first = []
        for t in range(n):
            mine[t].start()
            first.append(copy(t, 0, me, sibling, own=True))
            first += [copy(t, 1 + j, me, (*chip, c), own=True) for j, chip in enumerate(chips)]
        for cp in first:
            cp.start()
        passed = []
        for j, chip in enumerate(chips):
            for t in range(n):
                copy(t, 1 + j, (*chip, c), me).wait_recv()
                fwd = copy(t, 4 + j, (*chip, c), sibling)
                fwd.start()
                passed.append(fwd)
        for t in range(n):
            copy(t, 0, sibling, me).wait_recv()
            for j, chip in enumerate(chips):
                copy(t, 4 + j, (*chip, 1 - c), me).wait_recv()
        for cp in first + passed:
            cp.wait_send()
        for cp in mine:
            cp.wait()

    out_shape = [jax.ShapeDtypeStruct(s.shape[:-2] + (N_DEV * s.shape[-2], s.shape[-1]), s.dtype)
                 for s in list(shards) + list(place)]
    res = pl.pallas_call(
        body, name=name, in_specs=[ANY] * (n + p), out_specs=[ANY] * (n + p), out_shape=out_shape,
        scratch_shapes=[pltpu.SemaphoreType.DMA((n, 7)), pltpu.SemaphoreType.DMA((n, 7)),
                        pltpu.SemaphoreType.DMA((n + p,))],
    )(*shards, *place)
    return res[:n], res[n:]


def _split_start(bufs, n_copies, plan, *, name, after=()):
    n = len(bufs)

    def body(*refs):
        token = refs[-1]
        for cp in plan(refs[:n], refs[n], refs[n + 1]):
            cp.start()
        token[...] = jnp.zeros_like(token)

    res = pl.pallas_call(
        _behind(body, n, after), name=name, in_specs=[HBM_SPEC] * n + [ANY] * len(after),
        out_specs=(SEM_SPEC, SEM_SPEC, *[HBM_SPEC] * n, pl.BlockSpec(memory_space=pltpu.VMEM)),
        out_shape=(pltpu.SemaphoreType.DMA((n_copies,)), pltpu.SemaphoreType.DMA((n_copies,)),
                   *[pltpu.HBM(b.shape, b.dtype) for b in bufs], jax.ShapeDtypeStruct((8, 128), F32)),
        input_output_aliases={i: 2 + i for i in range(n)},
        compiler_params=pltpu.CompilerParams(has_side_effects=DATAFLOW),
    )(*[pltpu.with_memory_space_constraint(b, pltpu.HBM) for b in bufs], *after)
    return res[0], res[1], list(res[2:2 + n]), res[-1]


def _split_wait(send_sems, recv_sems, bufs, after, plan, *, name):
    n = len(bufs)

    def body(*refs):
        for cp in plan(refs[:n], refs[n], refs[n + 1]):
            cp.wait_send()
            cp.wait_recv()

    return list(pl.pallas_call(
        body, name=name, in_specs=[HBM_SPEC] * n + [SEM_SPEC, SEM_SPEC, ANY], out_specs=[HBM_SPEC] * n,
        out_shape=tuple(pltpu.HBM(b.shape, b.dtype) for b in bufs),
        input_output_aliases={i: i for i in range(n)},
        compiler_params=pltpu.CompilerParams(has_side_effects=DATAFLOW),
    )(*bufs, send_sems, recv_sems, after))


def _sibling_plan(n):
    def plan(bufs, send_sems, recv_sems):
        x, y, c, _ = _place()
        return [pltpu.make_async_remote_copy(
            src_ref=bufs[t].at[:, 1 - c], dst_ref=bufs[n + t], send_sem=send_sems.at[t], recv_sem=recv_sems.at[t],
            device_id=(x, y, 1 - c), device_id_type=MESH) for t in range(n)]
    return plan


def _block_rows(ref, r, blk):
    start = pl.multiple_of(blk * r, 16 if r % 16 == 0 else 8)
    return ref.at[(slice(None),) * (len(ref.shape) - 2) + (pl.ds(start, r), slice(None))]


def _remote(src, dst, send_sems, recv_sems, k, peer):
    return pltpu.make_async_remote_copy(src_ref=src, dst_ref=dst, send_sem=send_sems.at[k], recv_sem=recv_sems.at[k],
                                        device_id=peer, device_id_type=MESH)


def _gather_send_plan(n):
    def plan(bufs, send_sems, recv_sems):
        x, y, c, chips = _place()
        peers = [(x, y, 1 - c)] + [(px, py, c) for px, py in chips]
        copies = []
        for t in range(n):
            dst = _block_rows(bufs[n + t], bufs[t].shape[-2], 4 * x + 2 * y + c)
            copies += [_remote(bufs[t], dst, send_sems, recv_sems, 4 * t + k, peer) for k, peer in enumerate(peers)]
        return copies
    return plan


def _gather_forward_plan(rows):
    def plan(bufs, send_sems, recv_sems):
        x, y, c, chips = _place()
        copies = []
        for t, r in enumerate(rows):
            for j, (px, py) in enumerate(chips):
                blk = _block_rows(bufs[t], r, 4 * px + 2 * py + c)
                copies.append(_remote(blk, blk, send_sems, recv_sems, 3 * t + j, (x, y, 1 - c)))
        return copies
    return plan


def _chips_plan(n, with_small):
    def plan(bufs, send_sems, recv_sems):
        x, y, c, chips = _place()
        copies = []
        for t in range(n):
            for j, (px, py) in enumerate(chips):
                copies.append(_remote(bufs[t].at[2 * px + py], bufs[n + t].at[j], send_sems, recv_sems, 3 * t + j,
                                      (px, py, c)))
        if with_small:
            mine = _block_rows(bufs[2 * n], 8, 4 * x + 2 * y + c)
            flips = [(fx, fy, fc) for fx in range(2) for fy in range(2) for fc in range(2)][1:]
            for k, (fx, fy, fc) in enumerate(flips):
                peer = (x + fx - 2 * x * fx, y + fy - 2 * y * fy, c + fc - 2 * c * fc)
                copies.append(_remote(mine, mine, send_sems, recv_sems, 3 * n + k, peer))
        return copies
    return plan


def _place_own(fulls, shards, index, *, name):
    n = len(fulls)

    def body(index_ref, *refs):
        for t in range(n):
            refs[2 * n + t][...] = refs[n + t][...]

    def block_of(shard):
        lead = len(shard.shape) - 2
        return pl.BlockSpec(shard.shape, lambda i, index_ref: (0,) * lead + (index_ref[0], 0))

    def whole(shard):
        return pl.BlockSpec(shard.shape, lambda i, index_ref: (0,) * len(shard.shape))

    return list(pl.pallas_call(
        body, name=name,
        grid_spec=pltpu.PrefetchScalarGridSpec(
            num_scalar_prefetch=1, grid=(1,),
            in_specs=[ANY] * n + [whole(s) for s in shards], out_specs=[block_of(s) for s in shards]),
        out_shape=[jax.ShapeDtypeStruct(f.shape, f.dtype) for f in fulls],
        input_output_aliases={1 + t: t for t in range(n)},
        compiler_params=_params(("arbitrary",)),
    )(index, *fulls, *shards))


def _add_sibling(grad, recv, place, *, name):
    rows = grad.shape[2]
    tr = rows // 2

    def body(place_ref, g_ref, r_ref, own_ref, ob_ref):
        chip = place_ref[1]
        own = jnp.zeros(own_ref.shape, F32)
        for m in range(4):
            p = g_ref[m, 0] + r_ref[m]
            ob_ref[m] = p.astype(BF16)
            own = jnp.where(chip == m, p, own)
        own_ref[...] = own

    blocks = pl.BlockSpec((4, tr, D_MODEL), lambda i, place_ref: (0, i, 0))
    return pl.pallas_call(
        body, name=name,
        grid_spec=pltpu.PrefetchScalarGridSpec(
            num_scalar_prefetch=1, grid=(rows // tr,),
            in_specs=[pl.BlockSpec((4, 1, tr, D_MODEL), lambda i, place_ref: (0, place_ref[0], i, 0)), blocks],
            out_specs=[pl.BlockSpec((tr, D_MODEL), lambda i, place_ref: (i, 0)), blocks]),
        out_shape=[jax.ShapeDtypeStruct(recv.shape[1:], F32), jax.ShapeDtypeStruct(recv.shape, BF16)],
        compiler_params=_params(("arbitrary",)),
    )(place, grad, recv)


def _reduce_adamw(own, recv, w, m, v, *, name, tr, after=()):
    rows = own.shape[0]

    def body(p_ref, r_ref, w_ref, m_ref, v_ref, g_ref, d_ref, mo_ref, vo_ref, token_ref):
        g = p_ref[...] + r_ref[0].astype(F32) + r_ref[1].astype(F32) + r_ref[2].astype(F32)
        g_ref[...] = g
        d_ref[...], mo_ref[...], vo_ref[...] = _adamw_math(w_ref[...], g, m_ref[...], v_ref[...])
        token_ref[...] = jnp.zeros_like(token_ref)

    spec = pl.BlockSpec((tr, D_MODEL), lambda i: (i, 0))
    return pl.pallas_call(
        _behind(body, 5, after), name=name, grid=(rows // tr,),
        in_specs=[spec, pl.BlockSpec((3, tr, D_MODEL), lambda i: (0, i, 0)), spec, spec, spec] + [ANY] * len(after),
        out_specs=[spec] * 4 + [pl.BlockSpec((8, 128), lambda i: (0, 0))],
        out_shape=[jax.ShapeDtypeStruct((rows, D_MODEL), F32)] * 4 + [jax.ShapeDtypeStruct((8, 128), F32)],
        compiler_params=_params(("arbitrary",)),
    )(own, recv, w, m, v, *after)


def _adamw_math(w, g, m, v):
    m = ADAM_B1 * m + (1.0 - ADAM_B1) * g
    v = ADAM_B2 * v + (1.0 - ADAM_B2) * (g * g)
    m_hat = m / (1.0 - ADAM_B1 ** ADAM_STEP)
    v_hat = v / (1.0 - ADAM_B2 ** ADAM_STEP)
    delta = -ADAM_LR * (m_hat / (jnp.sqrt(v_hat) + ADAM_EPS) + ADAM_WD * w)
    return delta, m, v


def _adamw(w, g, m, v, *, name, tr, after=()):
    rows, cols = w.shape

    def body(w_ref, g_ref, m_ref, v_ref, d_ref, mo_ref, vo_ref):
        d_ref[...], mo_ref[...], vo_ref[...] = _adamw_math(w_ref[...], g_ref[...], m_ref[...], v_ref[...])

    spec = pl.BlockSpec((tr, cols), lambda i: (i, 0))
    return pl.pallas_call(
        _behind(body, 4, after), name=name, grid=(rows // tr,), in_specs=[spec] * 4 + [ANY] * len(after),
        out_specs=[spec] * 3, out_shape=[jax.ShapeDtypeStruct(w.shape, F32)] * 3,
        compiler_params=_params(("parallel",)),
    )(w, g, m, v, *after)


def _sum_small(gathered, *, name):
    def body(g_ref, o_ref):
        acc = g_ref[0]
        for k in range(1, N_DEV):
            acc = acc + g_ref[k]
        o_ref[...] = acc

    return pl.pallas_call(body, name=name, out_shape=jax.ShapeDtypeStruct(gathered.shape[1:], F32))(gathered)


def kernel(x, ffn1_norm, ffn1_w_gate, ffn1_w_up, ffn1_w_down, mix_norm, w_in, conv_w, attn_sinks, w_out, ffn2_norm, ffn2_w_gate, ffn2_w_up, ffn2_w_down, final_norm, loss_target, m_ffn1_norm, m_ffn1_w_gate, m_ffn1_w_up, m_ffn1_w_down, m_mix_norm, m_w_in, m_conv_w, m_attn_sinks, m_w_out, m_ffn2_norm, m_ffn2_w_gate, m_ffn2_w_up, m_ffn2_w_down, m_final_norm, v_ffn1_norm, v_ffn1_w_gate, v_ffn1_w_up, v_ffn1_w_down, v_mix_norm, v_w_in, v_conv_w, v_attn_sinks, v_w_out, v_ffn2_norm, v_ffn2_w_gate, v_ffn2_w_up, v_ffn2_w_down, v_final_norm):
    ix, iy, ic = lax.axis_index("x"), lax.axis_index("y"), lax.axis_index("c")
    my_index = 4 * ix + 2 * iy + ic
    place = jnp.stack([ic, 2 * ix + iy]).astype(jnp.int32)

    given = dict(ffn1_norm=ffn1_norm, ffn1_w_gate=ffn1_w_gate, ffn1_w_up=ffn1_w_up, ffn1_w_down=ffn1_w_down,
                 mix_norm=mix_norm, w_in=w_in, conv_w=conv_w, attn_sinks=attn_sinks, w_out=w_out, ffn2_norm=ffn2_norm,
                 ffn2_w_gate=ffn2_w_gate, ffn2_w_up=ffn2_w_up, ffn2_w_down=ffn2_w_down, final_norm=final_norm)
    moments_m = dict(ffn1_norm=m_ffn1_norm, ffn1_w_gate=m_ffn1_w_gate, ffn1_w_up=m_ffn1_w_up, ffn1_w_down=m_ffn1_w_down,
                     mix_norm=m_mix_norm, w_in=m_w_in, conv_w=m_conv_w, attn_sinks=m_attn_sinks, w_out=m_w_out,
                     ffn2_norm=m_ffn2_norm, ffn2_w_gate=m_ffn2_w_gate, ffn2_w_up=m_ffn2_w_up, ffn2_w_down=m_ffn2_w_down,
                     final_norm=m_final_norm)
    moments_v = dict(ffn1_norm=v_ffn1_norm, ffn1_w_gate=v_ffn1_w_gate, ffn1_w_up=v_ffn1_w_up, ffn1_w_down=v_ffn1_w_down,
                     mix_norm=v_mix_norm, w_in=v_w_in, conv_w=v_conv_w, attn_sinks=v_attn_sinks, w_out=v_w_out,
                     ffn2_norm=v_ffn2_norm, ffn2_w_gate=v_ffn2_w_gate, ffn2_w_up=v_ffn2_w_up, ffn2_w_down=v_ffn2_w_down,
                     final_norm=v_final_norm)

    xs = x[0]
    target = loss_target[0]
    final_gain = final_norm.reshape(1, D_MODEL)

    def ffn_shard(wg, wu, wd):
        return jnp.stack([wg[0].T, wu[0].T, wd[0]]).astype(BF16)

    conv_cols = conv_w.shape[2]
    conv_shard = jnp.pad(conv_w[0], ((0, 5), (0, 128 - conv_cols)))
    rest_shards = [ffn_shard(ffn2_w_gate, ffn2_w_up, ffn2_w_down), w_in[0].T.astype(BF16), w_out[0].astype(BF16),
                   conv_shard]
    rest_rows = [s.shape[-2] for s in rest_shards]
    n_rest = len(rest_shards)
    (w1,), _ = _all_gather_rows([ffn_shard(ffn1_w_gate, ffn1_w_up, ffn1_w_down)], name="gather_ffn1")

    fulls = [lax.empty(s.shape[:-2] + (N_DEV * s.shape[-2], s.shape[-1]), s.dtype) for s in rest_shards]
    fulls = _place_own(fulls, rest_shards, my_index.astype(jnp.int32).reshape(1), name="place_own_weights")
    send_plan = _gather_send_plan(n_rest)
    ssem, rsem, bufs, token = _split_start(rest_shards + list(fulls), 4 * n_rest, send_plan, name="gather_rest_start",
                                           after=[w1])
    x1, h1, s1, sa1, sb1 = _ffn_fwd(xs, ffn1_norm, w1, name="ffn1_fwd", after=[token])
    bufs = _split_wait(ssem, rsem, bufs, x1, send_plan, name="gather_rest_wait")
    w2_part, mixer_parts = bufs[n_rest], bufs[n_rest + 1:]
    fwd_mixer = _gather_forward_plan(rest_rows[1:])
    ssem, rsem, bufs, token = _split_start(mixer_parts, 3 * (n_rest - 1), fwd_mixer, name="forward_mixer_start")
    win_t, wout, conv_all = _split_wait(ssem, rsem, bufs, token, fwd_mixer, name="forward_mixer_wait")
    conv_full = conv_all.reshape(N_DEV, 8, 128)[:, :3, :conv_cols].transpose(1, 0, 2).reshape(3, CONV_W)
    fwd_ffn2 = _gather_forward_plan(rest_rows[:1])
    ssem, rsem, bufs, token = _split_start([w2_part], 3, fwd_ffn2, name="forward_ffn2_start", after=[win_t])
    rope = _rope_tables(xs.shape[0])
    x2, hm, z, y = _mixer_fwd(x1, mix_norm, win_t, wout, conv_full, attn_sinks, rope, name="mixer_fwd", after=[token])
    (w2,) = _split_wait(ssem, rsem, bufs, x2, fwd_ffn2, name="forward_ffn2_wait")
    dx3, h2, s2, sa2, sb2, loss_local, d_final = _ffn_fwd(x2, ffn2_norm, w2, head=(final_gain, target),
                                                          name="ffn2_fwd")

    def to_sibling_start(grads, tag, after=()):
        views = [g.reshape(4, 2, g.shape[0] // N_DEV, D_MODEL) for g in grads]
        lands = [lax.empty((4,) + v.shape[2:], F32) for v in views]
        plan = _sibling_plan(len(views))
        ssem, rsem, bufs, token = _split_start(views + lands, len(views), plan, name=f"{tag}_sibling_start", after=after)
        return (ssem, rsem, bufs, plan, tag), token

    def to_sibling_finish(handle, after, names):
        ssem, rsem, bufs, plan, tag = handle
        bufs = _split_wait(ssem, rsem, bufs, after, plan, name=f"{tag}_sibling_wait")
        n = len(names)
        return [_add_sibling(v, r, place, name=f"add_sibling_{nm}")
                for v, r, nm in zip(bufs[:n], bufs[n:], names)]

    def to_chips_start(partials, tag, small_all=None, after=()):
        p16 = [p for _, p in partials]
        lands = [lax.empty((3,) + p.shape[1:], BF16) for p in p16]
        extra = [] if small_all is None else [small_all]
        plan = _chips_plan(len(p16), small_all is not None)
        ssem, rsem, bufs, token = _split_start(p16 + lands + extra, 3 * len(p16) + 7 * len(extra), plan,
                                               name=f"{tag}_chips_start", after=after)
        return (ssem, rsem, bufs, plan, tag), token

    def to_chips_finish(handle, partials, after, names):
        ssem, rsem, bufs, plan, tag = handle
        bufs = _split_wait(ssem, rsem, bufs, after, plan, name=f"{tag}_chips_wait")
        n = len(names)
        return [(p32, r) for (p32, _), r in zip(partials, bufs[n:2 * n])], bufs[2 * n:]

    half_ff = D_FF // 2
    names2, namesm = ["ffn2_w_gate", "ffn2_w_up", "ffn2_w_down"], ["w_in", "w_out"]
    transposed = {"ffn1_w_gate", "ffn1_w_up", "w_in", "ffn2_w_gate", "ffn2_w_up"}
    grad, delta, new_m, new_v = {}, {}, {}, {}

    def adam_big(nm, parts, after=()):
        to_rows = (lambda a: jnp.swapaxes(a, 1, 2)[0]) if nm in transposed else (lambda a: a[0])
        from_rows = (lambda a: jnp.swapaxes(a[None], 1, 2)) if nm in transposed else (lambda a: a[None])
        p32, recv = parts
        *outs, token = _reduce_adamw(p32, recv, to_rows(given[nm]), to_rows(moments_m[nm]),
                                     to_rows(moments_v[nm]), name=f"adamw_{nm}", tr=p32.shape[0] // 2, after=after)
        grad[nm], delta[nm], new_m[nm], new_v[nm] = (from_rows(a) for a in outs)
        return token

    dx2, da2, db2, g2b, d_norm2 = _ffn_dgrad(dx3, x2, ffn2_norm, sa2, sb2, w2, name="ffn2_dgrad")
    gw2 = [_tn_matmul(da2, h2, name="ffn2_wgrad_gate", bm=half_ff), _tn_matmul(db2, h2, name="ffn2_wgrad_up", bm=half_ff),
           _tn_matmul(s2, g2b, name="ffn2_wgrad_down", bm=half_ff)]
    sib2, tok = to_sibling_start(gw2, "ffn2")
    dx1, dz, gmb, d_conv, d_sink, d_normm = _mixer_bwd(dx2, x1, mix_norm, y, z, win_t, wout, conv_full, attn_sinks,
                                                       rope, name="mixer_bwd", after=[tok])
    p2 = to_sibling_finish(sib2, dx1, names2)
    chips2, tok = to_chips_start(p2, "ffn2")
    gwm = [_tn_matmul(dz, hm, name="mixer_wgrad_in", bm=Z_W // 3, after=[tok]),
           _tn_matmul(y, gmb, name="mixer_wgrad_out", bm=D_MODEL // 2, after=[tok])]
    sibm, tok = to_sibling_start(gwm, "mixer")
    dx0, da1, db1, g1b, d_norm1 = _ffn_dgrad(dx1, xs, ffn1_norm, sa1, sb1, w1, name="ffn1_dgrad", after=[tok])
    r2, _ = to_chips_finish(chips2, p2, dx0, names2)
    pm = to_sibling_finish(sibm, dx0, namesm)
    chipsm, tok = to_chips_start(pm, "mixer")
    gw_gate = _tn_matmul(da1, h1, name="ffn1_wgrad_gate", bm=half_ff, after=[tok])
    sib_gate, tok = to_sibling_start([gw_gate], "ffn1_gate")
    gw_up = _tn_matmul(db1, h1, name="ffn1_wgrad_up", bm=half_ff, after=[tok])
    rm, _ = to_chips_finish(chipsm, pm, gw_up, namesm)
    p_gate = to_sibling_finish(sib_gate, gw_up, ["ffn1_w_gate"])
    chips_gate, tok_a = to_chips_start(p_gate, "ffn1_gate")
    sib_up, tok_b = to_sibling_start([gw_up], "ffn1_up", after=[tok_a])
    gw_down = _tn_matmul(s1, g1b, name="ffn1_wgrad_down", bm=half_ff, after=[tok_a, tok_b])
    p_up = to_sibling_finish(sib_up, gw_down, ["ffn1_w_up"])
    chips_up, tok_a = to_chips_start(p_up, "ffn1_up")
    sib_down, tok_b = to_sibling_start([gw_down], "ffn1_down", after=[tok_a])
    p_down = to_sibling_finish(sib_down, tok_b, ["ffn1_w_down"])
    last_row = (jnp.pad(d_sink, ((0, 0), (0, D_MODEL - 128)))
                + jnp.pad(loss_local, ((0, 0), (LOSS_LANE, D_MODEL - LOSS_LANE - 1))))
    small = jnp.concatenate([
        d_norm1, d_normm, d_norm2, d_final, jnp.pad(d_conv[0:3], ((0, 0), (0, D_MODEL - CONV_W))), last_row], axis=0)
    (small_all,) = _place_own([lax.empty((N_DEV * 8, D_MODEL), F32)], [small], my_index.astype(jnp.int32).reshape(1),
                              name="place_own_small")
    chips_down, tok = to_chips_start(p_down, "ffn1_down", small_all)
    for nm, g in zip(names2 + namesm, r2 + rm):
        tok = adam_big(nm, g, after=[tok])
    r_gate, _ = to_chips_finish(chips_gate, p_gate, tok, ["ffn1_w_gate"])
    tok = adam_big("ffn1_w_gate", r_gate[0])
    r_up, _ = to_chips_finish(chips_up, p_up, tok, ["ffn1_w_up"])
    tok = adam_big("ffn1_w_up", r_up[0])
    r_down, (small_all,) = to_chips_finish(chips_down, p_down, tok, ["ffn1_w_down"])
    adam_big("ffn1_w_down", r_down[0])
    small_sum = _sum_small(small_all.reshape(N_DEV, 8, D_MODEL), name="sum_small")
    loss = small_sum[7, LOSS_LANE]
    _update_small(given, moments_m, moments_v, small_sum, my_index, grad, delta, new_m, new_v)

    order = list(given)
    return (loss, dx0[None], *[grad[n] for n in order], *[delta[n] for n in order],
            *[new_m[n] for n in order], *[new_v[n] for n in order])


def _update_small(given, moments_m, moments_v, small_sum, my_index, grad, delta, new_m, new_v):
    conv_cols = given["conv_w"].shape[2]
    small_g = {
        "ffn1_norm": small_sum[0:1], "mix_norm": small_sum[1:2], "ffn2_norm": small_sum[2:3],
        "final_norm": small_sum[3:4],
        "conv_w": lax.dynamic_slice(small_sum[4:7, :CONV_W], (0, my_index * conv_cols), (3, conv_cols)),
        "attn_sinks": small_sum[7:8, :N_Q_HEADS],
    }

    small_names = ["ffn1_norm", "mix_norm", "ffn2_norm", "final_norm", "conv_w", "attn_sinks"]

    def pack(parts):
        rows = []
        for nm in small_names:
            p = parts[nm]
            p2 = p.reshape(3, conv_cols) if nm == "conv_w" else p.reshape(1, -1)
            rows.append(jnp.pad(p2, ((0, 0), (0, D_MODEL - p2.shape[1]))))
        rows.append(jnp.zeros((8, D_MODEL), F32))
        return jnp.concatenate(rows, axis=0)

    sd, sm, sv = _adamw(pack(given), pack(small_g), pack(moments_m), pack(moments_v), name="adamw_small", tr=16)
    row = 0
    for nm in small_names:
        shape = given[nm].shape
        nrow = 3 if nm == "conv_w" else 1
        ncol = conv_cols if nm == "conv_w" else given[nm].size
        grad[nm] = small_g[nm].reshape(shape)
        delta[nm], new_m[nm], new_v[nm] = (a[row:row + nrow, :ncol].reshape(shape) for a in (sd, sm, sv))
        row += nrow
```

```python
import functools

import jax
import jax.numpy as jnp
from jax import lax
from jax.experimental import pallas as pl
from jax.experimental.pallas import tpu as pltpu

F32 = jnp.float32
BF16 = jnp.bfloat16
MESH = pl.DeviceIdType.MESH
ANY = pl.BlockSpec(memory_space=pl.ANY)
HBM_SPEC = pl.BlockSpec(memory_space=pltpu.HBM)
SEM_SPEC = pl.BlockSpec(memory_space=pltpu.SEMAPHORE)
DATAFLOW = pltpu.SideEffectType.DATAFLOW_SIDE_EFFECTING

N_DEV = 8
LOSS_LANE = 128
D_MODEL = 1024
D_FF = 2816
CONV_W = 512
ATTN_W = 512
KV_W = 128
HEAD_DIM = 64
N_Q_HEADS = 8
N_KV_HEADS = 2
Q_PER_KV = N_Q_HEADS // N_KV_HEADS
BLOCK = 128
ROT_DIM = 16
ROPE_THETA = 500000.0
Z_W = 3 * CONV_W + ATTN_W + 2 * KV_W
Q_OFF = 3 * CONV_W
K_OFF = Q_OFF + ATTN_W
V_OFF = K_OFF + KV_W
RMS_EPS = 1e-5
MASK_VALUE = -1e30
SM_SCALE = HEAD_DIM ** -0.5
FFN_RES_SCALE = 0.5

ADAM_LR = 0.001
ADAM_B1 = 0.9
ADAM_B2 = 0.999
ADAM_EPS = 1e-08
ADAM_WD = 0.01
ADAM_STEP = 10

NT_DIMS = (((1,), (1,)), ((), ()))
TN_DIMS = (((0,), (0,)), ((), ()))

VMEM_LIMIT = 62 * 1024 * 1024
FF_CHUNK = 256


def _params(sem, vmem=None):
    return pltpu.CompilerParams(dimension_semantics=sem, vmem_limit_bytes=vmem)


def _behind(body, n_in, after):
    k = len(after)
    if k == 0:
        return body
    return lambda *refs: body(*refs[:n_in], *refs[n_in + k:])


def _rms_stats(xf):
    inv = lax.rsqrt(jnp.mean(xf * xf, axis=-1, keepdims=True) + RMS_EPS)
    return xf * inv, inv


def _rms_bwd(dh, xhat, inv, gain):
    dxhat = dh * gain
    dx = inv * (dxhat - xhat * jnp.mean(dxhat * xhat, axis=-1, keepdims=True))
    dgain = jnp.sum(dh * xhat, axis=0, keepdims=True)
    return dx, dgain


def _load_resident(w_hbm, w_ref, sem):
    @pl.when(pl.program_id(0) == 0)
    def _():
        cp = pltpu.make_async_copy(w_hbm, w_ref, sem)
        cp.start()
        cp.wait()


def _ffn_fwd(x, gain, w3, *, name, head=None, after=(), tm=256, sub=256, tf=FF_CHUNK):
    t = x.shape[0]
    tm = min(tm, t)
    sub = min(sub, tm)
    n_head = 0 if head is None else 2

    def body(*refs):
        x_ref, g_ref, w_hbm = refs[:3]
        head_refs = refs[3:3 + n_head]
        xo_ref, h_ref, s_ref, sa_ref, sb_ref = refs[3 + n_head:8 + n_head]
        head_outs = refs[8 + n_head:8 + 2 * n_head]
        w_ref, sem = refs[8 + 2 * n_head:]
        _load_resident(w_hbm, w_ref, sem)

        @pl.when(pl.program_id(0) == 0)
        def _():
            for ref in head_outs:
                ref[...] = jnp.zeros_like(ref)

        for r0 in range(0, tm, sub):
            rows = slice(r0, r0 + sub)
            xf = x_ref[rows, :]
            xhat, _ = _rms_stats(xf)
            h = (xhat * g_ref[...]).astype(BF16)
            h_ref[rows, :] = h
            for c in range(0, D_FF, tf):
                cols = slice(c, min(c + tf, D_FF))
                a = lax.dot_general(h, w_ref[0, cols, :], NT_DIMS, preferred_element_type=F32)
                b = lax.dot_general(h, w_ref[1, cols, :], NT_DIMS, preferred_element_type=F32)
                sig = jax.nn.sigmoid(a)
                silu = a * sig
                s_ref[rows, cols] = (silu * b).astype(BF16)
                sa_ref[rows, cols] = (b * (sig * (1.0 + a * (1.0 - sig)))).astype(BF16)
                sb_ref[rows, cols] = silu.astype(BF16)
            xo = xf + FFN_RES_SCALE * jnp.dot(s_ref[rows, :], w_ref[2], preferred_element_type=F32)
            if head is None:
                xo_ref[rows, :] = xo
            else:
                fg_ref, t_ref = head_refs
                loss_ref, dfg_ref = head_outs
                xhat_o, inv_o = _rms_stats(xo)
                err = xhat_o * fg_ref[...] - t_ref[rows, :]
                loss_ref[...] += 0.5 * jnp.sum(jnp.mean(err * err, axis=-1, keepdims=True), axis=0, keepdims=True)
                xo_ref[rows, :], dfg = _rms_bwd(err * (1.0 / D_MODEL), xhat_o, inv_o, fg_ref[...])
                dfg_ref[...] += dfg

    row = pl.BlockSpec((tm, D_MODEL), lambda i: (i, 0))
    hid = pl.BlockSpec((tm, D_FF), lambda i: (i, 0))
    vec = pl.BlockSpec((1, D_MODEL), lambda i: (0, 0))
    head_in = [] if head is None else [vec, row]
    head_out = [] if head is None else [pl.BlockSpec((1, 1), lambda i: (0, 0)), vec]
    head_shape = [] if head is None else [jax.ShapeDtypeStruct((1, 1), F32), jax.ShapeDtypeStruct((1, D_MODEL), F32)]
    return pl.pallas_call(
        _behind(body, 3 + n_head, after), name=name, grid=(t // tm,),
        in_specs=[row, vec, ANY] + head_in + [ANY] * len(after),
        out_specs=[row, row, hid, hid, hid] + head_out,
        out_shape=[jax.ShapeDtypeStruct((t, D_MODEL), F32), jax.ShapeDtypeStruct((t, D_MODEL), BF16)]
        + [jax.ShapeDtypeStruct((t, D_FF), BF16)] * 3 + head_shape,
        scratch_shapes=[pltpu.VMEM((3, D_FF, D_MODEL), BF16), pltpu.SemaphoreType.DMA(())],
        compiler_params=_params(("arbitrary",), VMEM_LIMIT),
    )(x, gain, w3, *(head or ()), *after)


def _ffn_dgrad(dxo, x, gain, sa, sb, w3, *, name, after=(), tm=512, sub=512, tf=FF_CHUNK):
    t = x.shape[0]
    tm = min(tm, t)
    sub = min(sub, tm)

    def body(dxo_ref, x_ref, g_ref, sa_ref, sb_ref, w_hbm, dxi_ref, da_ref, db_ref, gb_ref, dg_ref, w_ref, sem):
        _load_resident(w_hbm, w_ref, sem)

        @pl.when(pl.program_id(0) == 0)
        def _():
            dg_ref[...] = jnp.zeros_like(dg_ref)

        for r0 in range(0, tm, sub):
            rows = slice(r0, r0 + sub)
            go = dxo_ref[rows, :]
            gb = (FFN_RES_SCALE * go).astype(BF16)
            gb_ref[rows, :] = gb
            for c in range(0, D_FF, tf):
                cols = slice(c, min(c + tf, D_FF))
                ds = lax.dot_general(gb, w_ref[2, cols, :], NT_DIMS, preferred_element_type=F32)
                da_ref[rows, cols] = (ds * sa_ref[rows, cols].astype(F32)).astype(BF16)
                db_ref[rows, cols] = (ds * sb_ref[rows, cols].astype(F32)).astype(BF16)
            dh = (jnp.dot(da_ref[rows, :], w_ref[0], preferred_element_type=F32)
                  + jnp.dot(db_ref[rows, :], w_ref[1], preferred_element_type=F32))
            xhat, inv = _rms_stats(x_ref[rows, :])
            dx, dgain = _rms_bwd(dh, xhat, inv, g_ref[...])
            dxi_ref[rows, :] = go + dx
            dg_ref[...] += dgain

    row = pl.BlockSpec((tm, D_MODEL), lambda i: (i, 0))
    hid = pl.BlockSpec((tm, D_FF), lambda i: (i, 0))
    vec = pl.BlockSpec((1, D_MODEL), lambda i: (0, 0))
    return pl.pallas_call(
        _behind(body, 6, after), name=name, grid=(t // tm,),
        in_specs=[row, row, vec, hid, hid, ANY] + [ANY] * len(after),
        out_specs=[row, hid, hid, row, vec],
        out_shape=[jax.ShapeDtypeStruct((t, D_MODEL), F32), jax.ShapeDtypeStruct((t, D_FF), BF16),
                   jax.ShapeDtypeStruct((t, D_FF), BF16),
                   jax.ShapeDtypeStruct((t, D_MODEL), BF16), jax.ShapeDtypeStruct((1, D_MODEL), F32)],
        scratch_shapes=[pltpu.VMEM((3, D_FF, D_MODEL), BF16), pltpu.SemaphoreType.DMA(())],
        compiler_params=_params(("arbitrary",), VMEM_LIMIT),
    )(dxo, x, gain, sa, sb, w3, *after)


def _tn_matmul(a, b, *, name, bm, after=(), tk=2048):
    t, m = a.shape
    n = b.shape[1]
    tk = min(tk, t)
    nk = t // tk

    def body(a_ref, b_ref, o_ref):
        @pl.when(pl.program_id(1) == 0)
        def _():
            o_ref[...] = jnp.zeros_like(o_ref)

        o_ref[...] += lax.dot_general(a_ref[...], b_ref[...], TN_DIMS, preferred_element_type=F32)

    return pl.pallas_call(
        _behind(body, 2, after), name=name, grid=(m // bm, nk),
        in_specs=[pl.BlockSpec((tk, bm), lambda i, k: (k, i)), pl.BlockSpec((tk, n), lambda i, k: (k, 0))]
        + [ANY] * len(after),
        out_specs=pl.BlockSpec((bm, n), lambda i, k: (i, 0)),
        out_shape=jax.ShapeDtypeStruct((m, n), F32),
        compiler_params=_params(("parallel", "arbitrary"), VMEM_LIMIT),
    )(a, b, *after)


def _rope_tables(t):
    inv_freq = ROPE_THETA ** (-jnp.arange(0, ROT_DIM, 2, dtype=F32) / ROT_DIM)
    ang = inv_freq[:, None] * jnp.arange(t, dtype=F32)[None, :]
    return jnp.stack([jnp.tile(jnp.cos(ang).T, (1, 16)), jnp.tile(jnp.sin(ang).T, (1, 16))])


def _rope_multipliers(cos_sin):
    half = ROT_DIM // 2
    cos, sin = cos_sin[0], cos_sin[1]
    d = lax.broadcasted_iota(jnp.int32, cos.shape, 1) & (HEAD_DIM - 1)
    mult = jnp.where(d < ROT_DIM, cos, 1.0)
    from_lo = jnp.where((d >= half) & (d < ROT_DIM), sin, 0.0)
    from_hi = jnp.where(d < half, -sin, 0.0)
    return mult, from_lo, from_hi


def _tile_lanes(tab, width):
    return jnp.tile(tab, (1, width // tab.shape[1]))


def _rope(v, tab):
    w = v.shape[1]
    half_rot = ROT_DIM // 2
    return (v * _tile_lanes(tab[0], w)
            + pltpu.roll(v, half_rot, axis=1) * _tile_lanes(tab[1], w)
            + pltpu.roll(v, w - half_rot, axis=1) * _tile_lanes(tab[2], w))


def _rope_bwd(dv, tab):
    w = dv.shape[1]
    half_rot = ROT_DIM // 2
    return (dv * _tile_lanes(tab[0], w)
            + pltpu.roll(dv * _tile_lanes(tab[1], w), w - half_rot, axis=1)
            + pltpu.roll(dv * _tile_lanes(tab[2], w), half_rot, axis=1))


def _shift_rows(v, prev8_ref, n):
    r = lax.broadcasted_iota(jnp.int32, v.shape, 0)
    rolled = pltpu.roll(v, n, axis=0)
    last = prev8_ref[7:8, :]
    if n == 1:
        return jnp.where(r >= 1, rolled, last)
    return jnp.where(r >= 2, rolled, jnp.where(r == 0, prev8_ref[6:7, :], last))


def _shift_rows_up(v, next8_ref, n):
    rows = v.shape[0]
    r = lax.broadcasted_iota(jnp.int32, v.shape, 0)
    rolled = pltpu.roll(v, rows - n, axis=0)
    first = next8_ref[0:1, :]
    if n == 1:
        return jnp.where(r <= rows - 2, rolled, first)
    return jnp.where(r <= rows - 3, rolled, jnp.where(r == rows - 2, first, next8_ref[1:2, :]))


def _lane_half_mask(shape, half):
    lane = lax.broadcasted_iota(jnp.int32, shape, 1)
    return (lane >= HEAD_DIM) if half else (lane < HEAD_DIM)


def _to_kv_lanes(chunk, head, kv):
    if head % 2 != kv:
        chunk = pltpu.roll(chunk, HEAD_DIM, axis=1)
    return jnp.where(_lane_half_mask(chunk.shape, kv), chunk, 0.0)


def _from_kv_lanes(chunk, head, kv):
    chunk = jnp.where(_lane_half_mask(chunk.shape, kv), chunk, 0.0)
    if head % 2 != kv:
        chunk = pltpu.roll(chunk, HEAD_DIM, axis=1)
    return chunk


def _stack_heads(wide):
    parts = []
    for head in range(N_Q_HEADS):
        chunk = wide[:, (head // 2) * 128:(head // 2 + 1) * 128]
        parts.append(_to_kv_lanes(chunk, head, head // Q_PER_KV))
    return jnp.concatenate(parts, axis=0)


def _window_mask(has_prev):
    shape = (N_Q_HEADS * BLOCK, 2 * BLOCK)
    qi = lax.broadcasted_iota(jnp.int32, shape, 0) & (BLOCK - 1)
    kj = lax.broadcasted_iota(jnp.int32, shape, 1)
    first_key = BLOCK - has_prev * BLOCK
    in_prev = (kj < BLOCK) & (kj > qi) & (kj >= first_key)
    in_own = (kj >= BLOCK) & ((kj - BLOCK) <= qi)
    return in_prev | in_own


def _sink_column(sink_ref):
    row = lax.broadcasted_iota(jnp.int32, (N_Q_HEADS * BLOCK, 1), 0)
    col = jnp.full((N_Q_HEADS * BLOCK, 1), sink_ref[0, 0], F32)
    for head in range(1, N_Q_HEADS):
        col = jnp.where(row >= head * BLOCK, sink_ref[0, head], col)
    return col


def _softmax_with_sink(q4, k2, mask, sink):
    s = lax.dot_general(q4, k2, NT_DIMS, preferred_element_type=F32) * SM_SCALE
    s = jnp.where(mask, s, MASK_VALUE)
    m = jnp.maximum(jnp.max(s, axis=-1, keepdims=True), sink)
    p = jnp.exp(s - m)
    e_sink = jnp.exp(sink - m)
    inv_den = 1.0 / (jnp.sum(p, axis=-1, keepdims=True) + e_sink)
    return p * inv_den, e_sink * inv_den


def _conv_terms(zf, prev8_ref, w_ref):
    b_gate, c_gate, u = zf[:, 0:CONV_W], zf[:, CONV_W:2 * CONV_W], zf[:, 2 * CONV_W:3 * CONV_W]
    vc = c_gate * u
    vm1 = _shift_rows(vc, prev8_ref, 1)
    vm2 = _shift_rows(vc, prev8_ref, 2)
    conv = w_ref[0:1, :] * vm2 + w_ref[1:2, :] * vm1 + w_ref[2:3, :] * vc
    return b_gate, c_gate, u, vc, vm1, vm2, conv


def _mixer_fwd(x, gain, win_t, wout, conv_w, sinks, rope, *, name, after=(), tq=512):
    t = x.shape[0]
    tq = min(tq, t)
    nblk = tq // BLOCK

    def body(x_ref, g_ref, win_hbm, wout_hbm, cw_ref, sink_ref, rope_ref,
             xo_ref, h_ref, z_ref, y_ref, kprev_ref, vprev_ref, cprev_ref, win_ref, wout_ref, sems):
        i = pl.program_id(0)
        _load_resident(win_hbm, win_ref, sems.at[0])
        _load_resident(wout_hbm, wout_ref, sems.at[1])

        @pl.when(i == 0)
        def _():
            kprev_ref[...] = jnp.zeros_like(kprev_ref)
            vprev_ref[...] = jnp.zeros_like(vprev_ref)
            cprev_ref[...] = jnp.zeros_like(cprev_ref)

        xf = x_ref[...]
        xhat, _ = _rms_stats(xf)
        h = (xhat * g_ref[...]).astype(BF16)
        h_ref[...] = h
        zb = lax.dot_general(h, win_ref[...], NT_DIMS, preferred_element_type=F32).astype(BF16)
        z_ref[...] = zb
        zf = zb.astype(F32)

        b_gate, _, _, vc, _, _, conv = _conv_terms(zf, cprev_ref, cw_ref)
        y_conv = b_gate * conv
        cprev_ref[...] = vc[tq - 8:tq, :]

        tab = _rope_multipliers(rope_ref[...])
        qr = _rope(zf[:, Q_OFF:K_OFF], tab)
        kr = _rope(zf[:, K_OFF:V_OFF], tab).astype(BF16)
        vb = zb[:, V_OFF:Z_W]

        y_attn = []
        for j in range(nblk):
            rows = slice(j * BLOCK, (j + 1) * BLOCK)
            prev = slice((j - 1) * BLOCK, j * BLOCK)
            k2 = jnp.concatenate([kprev_ref[...] if j == 0 else kr[prev], kr[rows]], axis=0)
            v2 = jnp.concatenate([vprev_ref[...] if j == 0 else vb[prev], vb[rows]], axis=0)
            mask = _window_mask(jnp.minimum(i, 1) if j == 0 else 1)
            q8 = _stack_heads(qr[rows]).astype(BF16)
            probs, _ = _softmax_with_sink(q8, k2, mask, _sink_column(sink_ref))
            o8 = jnp.dot(probs.astype(BF16), v2, preferred_element_type=F32)
            chunks = [jnp.zeros((BLOCK, 128), F32) for _ in range(ATTN_W // 128)]
            for head in range(N_Q_HEADS):
                chunks[head // 2] += _from_kv_lanes(o8[head * BLOCK:(head + 1) * BLOCK], head, head // Q_PER_KV)
            y_attn.append(jnp.concatenate(chunks, axis=1))
        kprev_ref[...] = kr[tq - BLOCK:tq]
        vprev_ref[...] = vb[tq - BLOCK:tq]
        y = jnp.concatenate([y_conv, jnp.concatenate(y_attn, axis=0)], axis=1).astype(BF16)
        y_ref[...] = y
        xo_ref[...] = xf + jnp.dot(y, wout_ref[...], preferred_element_type=F32)

    row = pl.BlockSpec((tq, D_MODEL), lambda i: (i, 0))
    full = lambda shape: pl.BlockSpec(shape, lambda i: (0,) * len(shape))
    return pl.pallas_call(
        _behind(body, 7, after), name=name, grid=(t // tq,),
        in_specs=[row, full((1, D_MODEL)), ANY, ANY, full((3, CONV_W)),
                  pl.BlockSpec(memory_space=pltpu.SMEM), pl.BlockSpec((2, tq, 128), lambda i: (0, i, 0))]
        + [ANY] * len(after),
        out_specs=[row, row, pl.BlockSpec((tq, Z_W), lambda i: (i, 0)), row],
        out_shape=[jax.ShapeDtypeStruct((t, D_MODEL), F32), jax.ShapeDtypeStruct((t, D_MODEL), BF16),
                   jax.ShapeDtypeStruct((t, Z_W), BF16), jax.ShapeDtypeStruct((t, D_MODEL), BF16)],
        scratch_shapes=[pltpu.VMEM((BLOCK, KV_W), BF16), pltpu.VMEM((BLOCK, KV_W), BF16),
                        pltpu.VMEM((8, CONV_W), F32), pltpu.VMEM((Z_W, D_MODEL), BF16),
                        pltpu.VMEM((D_MODEL, D_MODEL), BF16), pltpu.SemaphoreType.DMA((2,))],
        compiler_params=_params(("arbitrary",), VMEM_LIMIT),
    )(x, gain, win_t, wout, conv_w, sinks, rope, *after)


def _mixer_bwd(dxo, x, gain, y, z, win_t, wout, conv_w, sinks, rope, *, name, after=(), tq=256):
    t = x.shape[0]
    tq = min(tq, t)
    nt, nblk = t // tq, tq // BLOCK

    def body(dxo_ref, x_ref, g_ref, y_ref, z_ref, zp_ref, win_hbm, wout_hbm, cw_ref, sink_ref, rope_ref, ropep_ref,
             dxi_ref, dz_ref, gb_ref, dcw_ref, dsink_ref, dg_ref, dk_ref, dv_ref, dcn_ref, pvc_ref,
             win_ref, wout_ref, sems):
        i = pl.program_id(0)
        tile = nt - 1 - i
        _load_resident(win_hbm, win_ref, sems.at[0])
        _load_resident(wout_hbm, wout_ref, sems.at[1])

        @pl.when(i == 0)
        def _():
            dk_ref[...] = jnp.zeros_like(dk_ref)
            dv_ref[...] = jnp.zeros_like(dv_ref)
            dcn_ref[...] = jnp.zeros_like(dcn_ref)
            dcw_ref[...] = jnp.zeros_like(dcw_ref)
            dsink_ref[...] = jnp.zeros_like(dsink_ref)
            dg_ref[...] = jnp.zeros_like(dg_ref)

        has_prev = jnp.minimum(tile, 1)
        go = dxo_ref[...]
        gb = go.astype(BF16)
        gb_ref[...] = gb
        dy = lax.dot_general(gb, wout_ref[...], NT_DIMS, preferred_element_type=F32)
        dy_conv, dy_attn = dy[:, 0:CONV_W], dy[:, CONV_W:D_MODEL]
        zb, zpb = z_ref[...], zp_ref[...]
        zf = zb.astype(F32)
        zpf = zpb.astype(F32) * has_prev.astype(F32)

        pvc_ref[...] = (zpf[:, CONV_W:2 * CONV_W] * zpf[:, 2 * CONV_W:3 * CONV_W])[BLOCK - 8:BLOCK, :]
        b_gate, c_gate, u, vc, vm1, vm2, conv = _conv_terms(zf, pvc_ref, cw_ref)
        d_bgate = dy_conv * conv
        dc = dy_conv * b_gate
        tap = lax.broadcasted_iota(jnp.int32, (8, CONV_W), 0)
        dcw_ref[...] += jnp.where(tap == 0, jnp.sum(dc * vm2, axis=0, keepdims=True),
                                  jnp.where(tap == 1, jnp.sum(dc * vm1, axis=0, keepdims=True),
                                            jnp.where(tap == 2, jnp.sum(dc * vc, axis=0, keepdims=True), 0.0)))
        dvc = (cw_ref[2:3, :] * dc + cw_ref[1:2, :] * _shift_rows_up(dc, dcn_ref, 1)
               + cw_ref[0:1, :] * _shift_rows_up(dc, dcn_ref, 2))
        dcn_ref[...] = dc[0:8, :]
        d_cgate = dvc * u
        d_u = dvc * c_gate

        tab, tabp = _rope_multipliers(rope_ref[...]), _rope_multipliers(ropep_ref[...])
        qr = _rope(zf[:, Q_OFF:K_OFF], tab)
        kr = _rope(zf[:, K_OFF:V_OFF], tab).astype(BF16)
        kpr = _rope(zpf[:, K_OFF:V_OFF], tabp).astype(BF16)
        vb, vpb = zb[:, V_OFF:Z_W], zpb[:, V_OFF:Z_W]
        out = y_ref[:, CONV_W:D_MODEL].astype(F32)
        do_out = dy_attn * out
        lane = lax.broadcasted_iota(jnp.int32, (1, 128), 1)
        dsink = jnp.zeros((1, 128), F32)
        dk_next, dv_next = dk_ref[...], dv_ref[...]
        dq_rows, dk_rows, dv_rows = [None] * nblk, [None] * nblk, [None] * nblk
        for j in reversed(range(nblk)):
            rows = slice(j * BLOCK, (j + 1) * BLOCK)
            prev = slice((j - 1) * BLOCK, j * BLOCK)
            k2 = jnp.concatenate([kpr if j == 0 else kr[prev], kr[rows]], axis=0)
            v2 = jnp.concatenate([vpb if j == 0 else vb[prev], vb[rows]], axis=0)
            mask = _window_mask(has_prev if j == 0 else 1)
            q8 = _stack_heads(qr[rows]).astype(BF16)
            do8 = _stack_heads(dy_attn[rows]).astype(BF16)
            delta = jnp.sum(_stack_heads(do_out[rows]), axis=-1, keepdims=True)
            probs, p_sink = _softmax_with_sink(q8, k2, mask, _sink_column(sink_ref))
            dp = lax.dot_general(do8, v2, NT_DIMS, preferred_element_type=F32)
            ds = (probs * (dp - delta) * SM_SCALE).astype(BF16)
            dq8 = jnp.dot(ds, k2, preferred_element_type=F32)
            dk2 = lax.dot_general(ds, q8, TN_DIMS, preferred_element_type=F32)
            dv2 = lax.dot_general(probs.astype(BF16), do8, TN_DIMS, preferred_element_type=F32)
            sink_terms = p_sink * delta
            dq_chunks = [jnp.zeros((BLOCK, 128), F32) for _ in range(ATTN_W // 128)]
            for head in range(N_Q_HEADS):
                grp = slice(head * BLOCK, (head + 1) * BLOCK)
                dq_chunks[head // 2] += _from_kv_lanes(dq8[grp], head, head // Q_PER_KV)
                dsink = dsink - jnp.where(lane == head, jnp.sum(sink_terms[grp], axis=0, keepdims=True), 0.0)
            dq_rows[j] = jnp.concatenate(dq_chunks, axis=1)
            dk_rows[j] = dk2[BLOCK:] + dk_next
            dv_rows[j] = dv2[BLOCK:] + dv_next
            dk_next, dv_next = dk2[:BLOCK], dv2[:BLOCK]
        dk_ref[...] = dk_next
        dv_ref[...] = dv_next
        dsink_ref[...] += dsink
        dq = _rope_bwd(jnp.concatenate(dq_rows, axis=0), tab)
        dk = _rope_bwd(jnp.concatenate(dk_rows, axis=0), tab)
        dv = jnp.concatenate(dv_rows, axis=0)

        dzb = jnp.concatenate([d_bgate, d_cgate, d_u, dq, dk, dv], axis=1).astype(BF16)
        dz_ref[...] = dzb
        dh = jnp.dot(dzb, win_ref[...], preferred_element_type=F32)
        xhat, inv = _rms_stats(x_ref[...])
        dx, dgain = _rms_bwd(dh, xhat, inv, g_ref[...])
        dxi_ref[...] = go + dx
        dg_ref[...] += dgain

    rev = lambda i: (nt - 1 - i, 0)
    block_before = lambda i: jnp.maximum((nt - 1 - i) * nblk - 1, 0)
    row = pl.BlockSpec((tq, D_MODEL), rev)
    full = lambda shape: pl.BlockSpec(shape, lambda i: (0,) * len(shape))
    return pl.pallas_call(
        _behind(body, 12, after), name=name, grid=(nt,),
        in_specs=[row, row, full((1, D_MODEL)), row,
                  pl.BlockSpec((tq, Z_W), rev), pl.BlockSpec((BLOCK, Z_W), lambda i: (block_before(i), 0)),
                  ANY, ANY, full((3, CONV_W)),
                  pl.BlockSpec(memory_space=pltpu.SMEM),
                  pl.BlockSpec((2, tq, 128), lambda i: (0, nt - 1 - i, 0)),
                  pl.BlockSpec((2, BLOCK, 128), lambda i: (0, block_before(i), 0))] + [ANY] * len(after),
        out_specs=[row, pl.BlockSpec((tq, Z_W), rev), row, full((8, CONV_W)), full((1, 128)), full((1, D_MODEL))],
        out_shape=[jax.ShapeDtypeStruct((t, D_MODEL), F32), jax.ShapeDtypeStruct((t, Z_W), BF16),
                   jax.ShapeDtypeStruct((t, D_MODEL), BF16), jax.ShapeDtypeStruct((8, CONV_W), F32),
                   jax.ShapeDtypeStruct((1, 128), F32), jax.ShapeDtypeStruct((1, D_MODEL), F32)],
        scratch_shapes=[pltpu.VMEM((BLOCK, KV_W), F32), pltpu.VMEM((BLOCK, KV_W), F32), pltpu.VMEM((8, CONV_W), F32),
                        pltpu.VMEM((8, CONV_W), F32), pltpu.VMEM((Z_W, D_MODEL), BF16),
                        pltpu.VMEM((D_MODEL, D_MODEL), BF16), pltpu.SemaphoreType.DMA((2,))],
        compiler_params=_params(("arbitrary",), VMEM_LIMIT),
    )(dxo, x, gain, y, z, z, win_t, wout, conv_w, sinks, rope, rope, *after)


def _place():
    x, y, c = lax.axis_index("x"), lax.axis_index("y"), lax.axis_index("c")
    other_chips = [(1 - x, y), (x, 1 - y), (1 - x, 1 - y)]
    return x, y, c, other_chips


def _all_gather_rows(shards, place=(), *, name):
    n, p = len(shards), len(place)

    def body(*refs):
        srcs, place_srcs = refs[:n], refs[n:n + p]
        outs, place_outs = refs[n + p:2 * n + p], refs[2 * n + p:2 * (n + p)]
        send_sems, recv_sems, local_sems = refs[2 * (n + p):]
        x, y, c, chips = _place()
        me, sibling = (x, y, c), (x, y, 1 - c)

        def rows(t, px, py, pc):
            r = srcs[t].shape[-2]
            start = pl.multiple_of((4 * px + 2 * py + pc) * r, 16 if r % 16 == 0 else 8)
            if len(srcs[t].shape) == 3:
                return outs[t].at[:, pl.ds(start, r), :]
            return outs[t].at[pl.ds(start, r), :]

        def copy(t, k, block, to, own=False):
            return pltpu.make_async_remote_copy(
                src_ref=srcs[t] if own else rows(t, *block), dst_ref=rows(t, *block),
                send_sem=send_sems.at[t, k], recv_sem=recv_sems.at[t, k], device_id=to, device_id_type=MESH)

        mine = [pltpu.make_async_copy(srcs[t], rows(t, *me), local_sems.at[t]) for t in range(n)]
        mine += [pltpu.make_async_copy(place_srcs[q],
                                       _block_rows(place_outs[q], place_srcs[q].shape[-2], 4 * x + 2 * y + c),
                                       local_sems.at[n + q]) for q in range(p)]
        for q in range(p):
            mine[n + q].start()
        first = []
        for t in range(n):
            mine[t].start()
            first.append(copy(t, 0, me, sibling, own=True))
            first += [copy(t, 1 + j, me, (*chip, c), own=True) for j, chip in enumerate(chips)]
        for cp in first:
            cp.start()
        passed = []
        for j, chip in enumerate(chips):
            for t in range(n):
                copy(t, 1 + j, (*chip, c), me).wait_recv()
                fwd = copy(t, 4 + j, (*chip, c), sibling)
                fwd.start()
                passed.append(fwd)
        for t in range(n):
            copy(t, 0, sibling, me).wait_recv()
            for j, chip in enumerate(chips):
                copy(t, 4 + j, (*chip, 1 - c), me).wait_recv()
        for cp in first + passed:
            cp.wait_send()
        for cp in mine:
            cp.wait()

    out_shape = [jax.ShapeDtypeStruct(s.shape[:-2] + (N_DEV * s.shape[-2], s.shape[-1]), s.dtype)
                 for s in list(shards) + list(place)]
    res = pl.pallas_call(
        body, name=name, in_specs=[ANY] * (n + p), out_specs=[ANY] * (n + p), out_shape=out_shape,
        scratch_shapes=[pltpu.SemaphoreType.DMA((n, 7)), pltpu.SemaphoreType.DMA((n, 7)),
                        pltpu.SemaphoreType.DMA((n + p,))],
    )(*shards, *place)
    return res[:n], res[n:]


def _split_start(bufs, n_copies, plan, *, name, after=()):
    n = len(bufs)

    def body(*refs):
        token = refs[-1]
        for cp in plan(refs[:n], refs[n], refs[n + 1]):
            cp.start()
        token[...] = jnp.zeros_like(token)

    res = pl.pallas_call(
        _behind(body, n, after), name=name, in_specs=[HBM_SPEC] * n + [ANY] * len(after),
        out_specs=(SEM_SPEC, SEM_SPEC, *[HBM_SPEC] * n, pl.BlockSpec(memory_space=pltpu.VMEM)),
        out_shape=(pltpu.SemaphoreType.DMA((n_copies,)), pltpu.SemaphoreType.DMA((n_copies,)),
                   *[pltpu.HBM(b.shape, b.dtype) for b in bufs], jax.ShapeDtypeStruct((8, 128), F32)),
        input_output_aliases={i: 2 + i for i in range(n)},
        compiler_params=pltpu.CompilerParams(has_side_effects=DATAFLOW),
    )(*[pltpu.with_memory_space_constraint(b, pltpu.HBM) for b in bufs], *after)
    return res[0], res[1], list(res[2:2 + n]), res[-1]


def _split_wait(send_sems, recv_sems, bufs, after, plan, *, name):
    n = len(bufs)

    def body(*refs):
        for cp in plan(refs[:n], refs[n], refs[n + 1]):
            cp.wait_send()
            cp.wait_recv()

    return list(pl.pallas_call(
        body, name=name, in_specs=[HBM_SPEC] * n + [SEM_SPEC, SEM_SPEC, ANY], out_specs=[HBM_SPEC] * n,
        out_shape=tuple(pltpu.HBM(b.shape, b.dtype) for b in bufs),
        input_output_aliases={i: i for i in range(n)},
        compiler_params=pltpu.CompilerParams(has_side_effects=DATAFLOW),
    )(*bufs, send_sems, recv_sems, after))


def _sibling_plan(n):
    def plan(bufs, send_sems, recv_sems):
        x, y, c, _ = _place()
        return [pltpu.make_async_remote_copy(
            src_ref=bufs[t].at[:, 1 - c], dst_ref=bufs[n + t], send_sem=send_sems.at[t], recv_sem=recv_sems.at[t],
            device_id=(x, y, 1 - c), device_id_type=MESH) for t in range(n)]
    return plan


def _block_rows(ref, r, blk):
    start = pl.multiple_of(blk * r, 16 if r % 16 == 0 else 8)
    return ref.at[(slice(None),) * (len(ref.shape) - 2) + (pl.ds(start, r), slice(None))]


def _remote(src, dst, send_sems, recv_sems, k, peer):
    return pltpu.make_async_remote_copy(src_ref=src, dst_ref=dst, send_sem=send_sems.at[k], recv_sem=recv_sems.at[k],
                                        device_id=peer, device_id_type=MESH)


def _gather_send_plan(n):
    def plan(bufs, send_sems, recv_sems):
        x, y, c, chips = _place()
        peers = [(x, y, 1 - c)] + [(px, py, c) for px, py in chips]
        copies = []
        for t in range(n):
            dst = _block_rows(bufs[n + t], bufs[t].shape[-2], 4 * x + 2 * y + c)
            copies += [_remote(bufs[t], dst, send_sems, recv_sems, 4 * t + k, peer) for k, peer in enumerate(peers)]
        return copies
    return plan


def _gather_forward_plan(rows):
    def plan(bufs, send_sems, recv_sems):
        x, y, c, chips = _place()
        copies = []
        for t, r in enumerate(rows):
            for j, (px, py) in enumerate(chips):
                blk = _block_rows(bufs[t], r, 4 * px + 2 * py + c)
                copies.append(_remote(blk, blk, send_sems, recv_sems, 3 * t + j, (x, y, 1 - c)))
        return copies
    return plan


def _chips_plan(n, with_small):
    def plan(bufs, send_sems, recv_sems):
        x, y, c, chips = _place()
        copies = []
        for t in range(n):
            for j, (px, py) in enumerate(chips):
                copies.append(_remote(bufs[t].at[2 * px + py], bufs[n + t].at[j], send_sems, recv_sems, 3 * t + j,
                                      (px, py, c)))
        if with_small:
            mine = _block_rows(bufs[2 * n], 8, 4 * x + 2 * y + c)
            flips = [(fx, fy, fc) for fx in range(2) for fy in range(2) for fc in range(2)][1:]
            for k, (fx, fy, fc) in enumerate(flips):
                peer = (x + fx - 2 * x * fx, y + fy - 2 * y * fy, c + fc - 2 * c * fc)
                copies.append(_remote(mine, mine, send_sems, recv_sems, 3 * n + k, peer))
        return copies
    return plan


def _place_own(fulls, shards, index, *, name):
    n = len(fulls)

    def body(index_ref, *refs):
        for t in range(n):
            refs[2 * n + t][...] = refs[n + t][...]

    def block_of(shard):
        lead = len(shard.shape) - 2
        return pl.BlockSpec(shard.shape, lambda i, index_ref: (0,) * lead + (index_ref[0], 0))

    def whole(shard):
        return pl.BlockSpec(shard.shape, lambda i, index_ref: (0,) * len(shard.shape))

    return list(pl.pallas_call(
        body, name=name,
        grid_spec=pltpu.PrefetchScalarGridSpec(
            num_scalar_prefetch=1, grid=(1,),
            in_specs=[ANY] * n + [whole(s) for s in shards], out_specs=[block_of(s) for s in shards]),
        out_shape=[jax.ShapeDtypeStruct(f.shape, f.dtype) for f in fulls],
        input_output_aliases={1 + t: t for t in range(n)},
        compiler_params=_params(("arbitrary",)),
    )(index, *fulls, *shards))


N_STEPS_SMALL = 2


def _add_sibling(grads, recvs, place, *, name):
    n = len(grads)

    def body(place_ref, *refs):
        chip = place_ref[1]
        for t in range(n):
            g_ref, r_ref, own_ref, ob_ref = refs[2 * t], refs[2 * t + 1], refs[2 * n + 2 * t], refs[2 * n + 2 * t + 1]
            own = jnp.zeros(own_ref.shape, F32)
            for m in range(4):
                p = g_ref[m, 0] + r_ref[m]
                ob_ref[m] = p.astype(BF16)
                own = jnp.where(chip == m, p, own)
            own_ref[...] = own

    in_specs, out_specs, out_shape = [], [], []
    for g, r in zip(grads, recvs):
        tr = g.shape[2] // N_STEPS_SMALL
        blocks = pl.BlockSpec((4, tr, D_MODEL), lambda i, place_ref: (0, i, 0))
        in_specs += [pl.BlockSpec((4, 1, tr, D_MODEL), lambda i, place_ref: (0, place_ref[0], i, 0)), blocks]
        out_specs += [pl.BlockSpec((tr, D_MODEL), lambda i, place_ref: (i, 0)), blocks]
        out_shape += [jax.ShapeDtypeStruct(r.shape[1:], F32), jax.ShapeDtypeStruct(r.shape, BF16)]
    res = pl.pallas_call(
        body, name=name,
        grid_spec=pltpu.PrefetchScalarGridSpec(num_scalar_prefetch=1, grid=(N_STEPS_SMALL,), in_specs=in_specs,
                                               out_specs=out_specs),
        out_shape=out_shape, compiler_params=_params(("arbitrary",), VMEM_LIMIT),
    )(place, *[a for pair in zip(grads, recvs) for a in pair])
    return [(res[2 * t], res[2 * t + 1]) for t in range(n)]


def _reduce_adamw(parts, *, name, after=()):
    n = len(parts)

    def body(*refs):
        for t in range(n):
            p_ref, r_ref, w_ref, m_ref, v_ref = refs[5 * t:5 * t + 5]
            g_ref, d_ref, mo_ref, vo_ref = refs[5 * n + 4 * t:5 * n + 4 * t + 4]
            g = p_ref[...] + r_ref[0].astype(F32) + r_ref[1].astype(F32) + r_ref[2].astype(F32)
            g_ref[...] = g
            d_ref[...], mo_ref[...], vo_ref[...] = _adamw_math(w_ref[...], g, m_ref[...], v_ref[...])
        refs[-1][...] = jnp.zeros_like(refs[-1])

    in_specs, out_specs, out_shape = [], [], []
    for own, _, _, _, _ in parts:
        rows = own.shape[0]
        tr = rows // N_STEPS_SMALL
        spec = pl.BlockSpec((tr, D_MODEL), lambda i: (i, 0))
        in_specs += [spec, pl.BlockSpec((3, tr, D_MODEL), lambda i: (0, i, 0)), spec, spec, spec]
        out_specs += [spec] * 4
        out_shape += [jax.ShapeDtypeStruct((rows, D_MODEL), F32)] * 4
    res = pl.pallas_call(
        _behind(body, 5 * n, after), name=name, grid=(N_STEPS_SMALL,),
        in_specs=in_specs + [ANY] * len(after),
        out_specs=out_specs + [pl.BlockSpec((8, 128), lambda i: (0, 0))],
        out_shape=out_shape + [jax.ShapeDtypeStruct((8, 128), F32)],
        compiler_params=_params(("arbitrary",), VMEM_LIMIT),
    )(*[a for part in parts for a in part], *after)
    return [tuple(res[4 * t:4 * t + 4]) for t in range(n)], res[-1]


def _adamw_math(w, g, m, v):
    m = ADAM_B1 * m + (1.0 - ADAM_B1) * g
    v = ADAM_B2 * v + (1.0 - ADAM_B2) * (g * g)
    m_hat = m / (1.0 - ADAM_B1 ** ADAM_STEP)
    v_hat = v / (1.0 - ADAM_B2 ** ADAM_STEP)
    delta = -ADAM_LR * (m_hat / (jnp.sqrt(v_hat) + ADAM_EPS) + ADAM_WD * w)
    return delta, m, v


def _adamw(w, g, m, v, *, name, tr, after=()):
    rows, cols = w.shape

    def body(w_ref, g_ref, m_ref, v_ref, d_ref, mo_ref, vo_ref):
        d_ref[...], mo_ref[...], vo_ref[...] = _adamw_math(w_ref[...], g_ref[...], m_ref[...], v_ref[...])

    spec = pl.BlockSpec((tr, cols), lambda i: (i, 0))
    return pl.pallas_call(
        _behind(body, 4, after), name=name, grid=(rows // tr,), in_specs=[spec] * 4 + [ANY] * len(after),
        out_specs=[spec] * 3, out_shape=[jax.ShapeDtypeStruct(w.shape, F32)] * 3,
        compiler_params=_params(("parallel",)),
    )(w, g, m, v, *after)


def _sum_small(gathered, *, name):
    def body(g_ref, o_ref):
        acc = g_ref[0]
        for k in range(1, N_DEV):
            acc = acc + g_ref[k]
        o_ref[...] = acc

    return pl.pallas_call(body, name=name, out_shape=jax.ShapeDtypeStruct(gathered.shape[1:], F32))(gathered)


def kernel(x, ffn1_norm, ffn1_w_gate, ffn1_w_up, ffn1_w_down, mix_norm, w_in, conv_w, attn_sinks, w_out, ffn2_norm, ffn2_w_gate, ffn2_w_up, ffn2_w_down, final_norm, loss_target, m_ffn1_norm, m_ffn1_w_gate, m_ffn1_w_up, m_ffn1_w_down, m_mix_norm, m_w_in, m_conv_w, m_attn_sinks, m_w_out, m_ffn2_norm, m_ffn2_w_gate, m_ffn2_w_up, m_ffn2_w_down, m_final_norm, v_ffn1_norm, v_ffn1_w_gate, v_ffn1_w_up, v_ffn1_w_down, v_mix_norm, v_w_in, v_conv_w, v_attn_sinks, v_w_out, v_ffn2_norm, v_ffn2_w_gate, v_ffn2_w_up, v_ffn2_w_down, v_final_norm):
    ix, iy, ic = lax.axis_index("x"), lax.axis_index("y"), lax.axis_index("c")
    my_index = 4 * ix + 2 * iy + ic
    place = jnp.stack([ic, 2 * ix + iy]).astype(jnp.int32)

    given = dict(ffn1_norm=ffn1_norm, ffn1_w_gate=ffn1_w_gate, ffn1_w_up=ffn1_w_up, ffn1_w_down=ffn1_w_down,
                 mix_norm=mix_norm, w_in=w_in, conv_w=conv_w, attn_sinks=attn_sinks, w_out=w_out, ffn2_norm=ffn2_norm,
                 ffn2_w_gate=ffn2_w_gate, ffn2_w_up=ffn2_w_up, ffn2_w_down=ffn2_w_down, final_norm=final_norm)
    moments_m = dict(ffn1_norm=m_ffn1_norm, ffn1_w_gate=m_ffn1_w_gate, ffn1_w_up=m_ffn1_w_up, ffn1_w_down=m_ffn1_w_down,
                     mix_norm=m_mix_norm, w_in=m_w_in, conv_w=m_conv_w, attn_sinks=m_attn_sinks, w_out=m_w_out,
                     ffn2_norm=m_ffn2_norm, ffn2_w_gate=m_ffn2_w_gate, ffn2_w_up=m_ffn2_w_up, ffn2_w_down=m_ffn2_w_down,
                     final_norm=m_final_norm)
    moments_v = dict(ffn1_norm=v_ffn1_norm, ffn1_w_gate=v_ffn1_w_gate, ffn1_w_up=v_ffn1_w_up, ffn1_w_down=v_ffn1_w_down,
                     mix_norm=v_mix_norm, w_in=v_w_in, conv_w=v_conv_w, attn_sinks=v_attn_sinks, w_out=v_w_out,
                     ffn2_norm=v_ffn2_norm, ffn2_w_gate=v_ffn2_w_gate, ffn2_w_up=v_ffn2_w_up, ffn2_w_down=v_ffn2_w_down,
                     final_norm=v_final_norm)

    xs = x[0]
    target = loss_target[0]
    final_gain = final_norm.reshape(1, D_MODEL)

    def ffn_shard(wg, wu, wd):
        return jnp.stack([wg[0].T, wu[0].T, wd[0]]).astype(BF16)

    conv_cols = conv_w.shape[2]
    conv_shard = jnp.pad(conv_w[0], ((0, 5), (0, 128 - conv_cols)))
    rest_shards = [ffn_shard(ffn2_w_gate, ffn2_w_up, ffn2_w_down), w_in[0].T.astype(BF16), w_out[0].astype(BF16),
                   conv_shard]
    rest_rows = [s.shape[-2] for s in rest_shards]
    n_rest = len(rest_shards)
    (w1,), _ = _all_gather_rows([ffn_shard(ffn1_w_gate, ffn1_w_up, ffn1_w_down)], name="gather_ffn1")

    fulls = [lax.empty(s.shape[:-2] + (N_DEV * s.shape[-2], s.shape[-1]), s.dtype) for s in rest_shards]
    fulls = _place_own(fulls, rest_shards, my_index.astype(jnp.int32).reshape(1), name="place_own_weights")
    send_plan = _gather_send_plan(n_rest)
    ssem, rsem, bufs, token = _split_start(rest_shards + list(fulls), 4 * n_rest, send_plan, name="gather_rest_start",
                                           after=[w1])
    x1, h1, s1, sa1, sb1 = _ffn_fwd(xs, ffn1_norm, w1, name="ffn1_fwd", after=[token])
    bufs = _split_wait(ssem, rsem, bufs, x1, send_plan, name="gather_rest_wait")
    w2_part, mixer_parts = bufs[n_rest], bufs[n_rest + 1:]
    fwd_mixer = _gather_forward_plan(rest_rows[1:])
    ssem, rsem, bufs, token = _split_start(mixer_parts, 3 * (n_rest - 1), fwd_mixer, name="forward_mixer_start")
    win_t, wout, conv_all = _split_wait(ssem, rsem, bufs, token, fwd_mixer, name="forward_mixer_wait")
    conv_full = conv_all.reshape(N_DEV, 8, 128)[:, :3, :conv_cols].transpose(1, 0, 2).reshape(3, CONV_W)
    fwd_ffn2 = _gather_forward_plan(rest_rows[:1])
    ssem, rsem, bufs, token = _split_start([w2_part], 3, fwd_ffn2, name="forward_ffn2_start", after=[win_t])
    rope = _rope_tables(xs.shape[0])
    x2, hm, z, y = _mixer_fwd(x1, mix_norm, win_t, wout, conv_full, attn_sinks, rope, name="mixer_fwd", after=[token])
    (w2,) = _split_wait(ssem, rsem, bufs, x2, fwd_ffn2, name="forward_ffn2_wait")
    dx3, h2, s2, sa2, sb2, loss_local, d_final = _ffn_fwd(x2, ffn2_norm, w2, head=(final_gain, target),
                                                          name="ffn2_fwd")

    def to_sibling_start(grads, tag, after=()):
        views = [g.reshape(4, 2, g.shape[0] // N_DEV, D_MODEL) for g in grads]
        lands = [lax.empty((4,) + v.shape[2:], F32) for v in views]
        plan = _sibling_plan(len(views))
        ssem, rsem, bufs, token = _split_start(views + lands, len(views), plan, name=f"{tag}_sibling_start", after=after)
        return (ssem, rsem, bufs, plan, tag), token

    def to_sibling_finish(handle, after, names):
        ssem, rsem, bufs, plan, tag = handle
        bufs = _split_wait(ssem, rsem, bufs, after, plan, name=f"{tag}_sibling_wait")
        n = len(names)
        return _add_sibling(bufs[:n], bufs[n:], place, name=f"add_sibling_{tag}")

    def to_chips_start(partials, tag, small_all=None, after=()):
        p16 = [p for _, p in partials]
        lands = [lax.empty((3,) + p.shape[1:], BF16) for p in p16]
        extra = [] if small_all is None else [small_all]
        plan = _chips_plan(len(p16), small_all is not None)
        ssem, rsem, bufs, token = _split_start(p16 + lands + extra, 3 * len(p16) + 7 * len(extra), plan,
                                               name=f"{tag}_chips_start", after=after)
        return (ssem, rsem, bufs, plan, tag), token

    def to_chips_finish(handle, partials, after, names):
        ssem, rsem, bufs, plan, tag = handle
        bufs = _split_wait(ssem, rsem, bufs, after, plan, name=f"{tag}_chips_wait")
        n = len(names)
        return [(p32, r) for (p32, _), r in zip(partials, bufs[n:2 * n])], bufs[2 * n:]

    half_ff = D_FF // 2
    names2, namesm = ["ffn2_w_gate", "ffn2_w_up", "ffn2_w_down"], ["w_in", "w_out"]
    transposed = {"ffn1_w_gate", "ffn1_w_up", "w_in", "ffn2_w_gate", "ffn2_w_up"}
    grad, delta, new_m, new_v = {}, {}, {}, {}

    def adam_big(names, parts, tag, after=()):
        def to_rows(nm, a):
            return jnp.swapaxes(a, 1, 2)[0] if nm in transposed else a[0]

        def from_rows(nm, a):
            return jnp.swapaxes(a[None], 1, 2) if nm in transposed else a[None]

        operands = [(p32, recv, to_rows(nm, given[nm]), to_rows(nm, moments_m[nm]), to_rows(nm, moments_v[nm]))
                    for nm, (p32, recv) in zip(names, parts)]
        results, token = _reduce_adamw(operands, name=f"adamw_{tag}", after=after)
        for nm, outs in zip(names, results):
            grad[nm], delta[nm], new_m[nm], new_v[nm] = (from_rows(nm, a) for a in outs)
        return token

    dx2, da2, db2, g2b, d_norm2 = _ffn_dgrad(dx3, x2, ffn2_norm, sa2, sb2, w2, name="ffn2_dgrad")
    gw2 = [_tn_matmul(da2, h2, name="ffn2_wgrad_gate", bm=half_ff), _tn_matmul(db2, h2, name="ffn2_wgrad_up", bm=half_ff),
           _tn_matmul(s2, g2b, name="ffn2_wgrad_down", bm=half_ff)]
    sib2, tok = to_sibling_start(gw2, "ffn2")
    dx1, dz, gmb, d_conv, d_sink, d_normm = _mixer_bwd(dx2, x1, mix_norm, y, z, win_t, wout, conv_full, attn_sinks,
                                                       rope, name="mixer_bwd", after=[tok])
    p2 = to_sibling_finish(sib2, dx1, names2)
    chips2, tok = to_chips_start(p2, "ffn2")
    gwm = [_tn_matmul(dz, hm, name="mixer_wgrad_in", bm=Z_W // 3, after=[tok]),
           _tn_matmul(y, gmb, name="mixer_wgrad_out", bm=D_MODEL // 2, after=[tok])]
    sibm, tok = to_sibling_start(gwm, "mixer")
    dx0, da1, db1, g1b, d_norm1 = _ffn_dgrad(dx1, xs, ffn1_norm, sa1, sb1, w1, name="ffn1_dgrad", after=[tok])
    r2, _ = to_chips_finish(chips2, p2, dx0, names2)
    pm = to_sibling_finish(sibm, dx0, namesm)
    chipsm, tok = to_chips_start(pm, "mixer")
    gw_gate = _tn_matmul(da1, h1, name="ffn1_wgrad_gate", bm=half_ff, after=[tok])
    sib_gate, tok = to_sibling_start([gw_gate], "ffn1_gate")
    gw_up = _tn_matmul(db1, h1, name="ffn1_wgrad_up", bm=half_ff, after=[tok])
    rm, _ = to_chips_finish(chipsm, pm, gw_up, namesm)
    p_gate = to_sibling_finish(sib_gate, gw_up, ["ffn1_w_gate"])
    chips_gate, tok_a = to_chips_start(p_gate, "ffn1_gate")
    sib_up, tok_b = to_sibling_start([gw_up], "ffn1_up", after=[tok_a])
    gw_down = _tn_matmul(s1, g1b, name="ffn1_wgrad_down", bm=half_ff, after=[tok_a, tok_b])
    p_up = to_sibling_finish(sib_up, gw_down, ["ffn1_w_up"])
    chips_up, tok_a = to_chips_start(p_up, "ffn1_up")
    sib_down, tok_b = to_sibling_start([gw_down], "ffn1_down", after=[tok_a])
    p_down = to_sibling_finish(sib_down, tok_b, ["ffn1_w_down"])
    last_row = (jnp.pad(d_sink, ((0, 0), (0, D_MODEL - 128)))
                + jnp.pad(loss_local, ((0, 0), (LOSS_LANE, D_MODEL - LOSS_LANE - 1))))
    small = jnp.concatenate([
        d_norm1, d_normm, d_norm2, d_final, jnp.pad(d_conv[0:3], ((0, 0), (0, D_MODEL - CONV_W))), last_row], axis=0)
    (small_all,) = _place_own([lax.empty((N_DEV * 8, D_MODEL), F32)], [small], my_index.astype(jnp.int32).reshape(1),
                              name="place_own_small")
    chips_down, tok = to_chips_start(p_down, "ffn1_down", small_all)
    tok = adam_big(names2, r2, "ffn2", after=[tok])
    tok = adam_big(namesm, rm, "mixer", after=[tok])
    r_gate, _ = to_chips_finish(chips_gate, p_gate, tok, ["ffn1_w_gate"])
    tok = adam_big(["ffn1_w_gate"], r_gate, "ffn1_gate")
    r_up, _ = to_chips_finish(chips_up, p_up, tok, ["ffn1_w_up"])
    tok = adam_big(["ffn1_w_up"], r_up, "ffn1_up")
    r_down, (small_all,) = to_chips_finish(chips_down, p_down, tok, ["ffn1_w_down"])
    adam_big(["ffn1_w_down"], r_down, "ffn1_down")
    small_sum = _sum_small(small_all.reshape(N_DEV, 8, D_MODEL), name="sum_small")
    loss = small_sum[7, LOSS_LANE]
    _update_small(given, moments_m, moments_v, small_sum, my_index, grad, delta, new_m, new_v)

    order = list(given)
    return (loss, dx0[None], *[grad[n] for n in order], *[delta[n] for n in order],
            *[new_m[n] for n in order], *[new_v[n] for n in order])


def _update_small(given, moments_m, moments_v, small_sum, my_index, grad, delta, new_m, new_v):
    conv_cols = given["conv_w"].shape[2]
    small_g = {
        "ffn1_norm": small_sum[0:1], "mix_norm": small_sum[1:2], "ffn2_norm": small_sum[2:3],
        "final_norm": small_sum[3:4],
        "conv_w": lax.dynamic_slice(small_sum[4:7, :CONV_W], (0, my_index * conv_cols), (3, conv_cols)),
        "attn_sinks": small_sum[7:8, :N_Q_HEADS],
    }

    small_names = ["ffn1_norm", "mix_norm", "ffn2_norm", "final_norm", "conv_w", "attn_sinks"]

    def pack(parts):
        rows = []
        for nm in small_names:
            p = parts[nm]
            p2 = p.reshape(3, conv_cols) if nm == "conv_w" else p.reshape(1, -1)
            rows.append(jnp.pad(p2, ((0, 0), (0, D_MODEL - p2.shape[1]))))
        rows.append(jnp.zeros((8, D_MODEL), F32))
        return jnp.concatenate(rows, axis=0)

    sd, sm, sv = _adamw(pack(given), pack(small_g), pack(moments_m), pack(moments_v), name="adamw_small", tr=16)
    row = 0
    for nm in small_names:
        shape = given[nm].shape
        nrow = 3 if nm == "conv_w" else 1
        ncol = conv_cols if nm == "conv_w" else given[nm].size
        grad[nm] = small_g[nm].reshape(shape)
        delta[nm], new_m[nm], new_v[nm] = (a[row:row + nrow, :ncol].reshape(shape) for a in (sd, sm, sv))
        row += nrow
```

```python
import functools

import jax
import jax.numpy as jnp
from jax import lax
from jax.experimental import pallas as pl
from jax.experimental.pallas import tpu as pltpu

F32 = jnp.float32
BF16 = jnp.bfloat16
MESH = pl.DeviceIdType.MESH
ANY = pl.BlockSpec(memory_space=pl.ANY)
HBM_SPEC = pl.BlockSpec(memory_space=pltpu.HBM)
SEM_SPEC = pl.BlockSpec(memory_space=pltpu.SEMAPHORE)
DATAFLOW = pltpu.SideEffectType.DATAFLOW_SIDE_EFFECTING

N_DEV = 8
LOSS_LANE = 128
D_MODEL = 1024
D_FF = 2816
CONV_W = 512
ATTN_W = 512
KV_W = 128
HEAD_DIM = 64
N_Q_HEADS = 8
N_KV_HEADS = 2
Q_PER_KV = N_Q_HEADS // N_KV_HEADS
BLOCK = 128
ROT_DIM = 16
ROPE_THETA = 500000.0
Z_W = 3 * CONV_W + ATTN_W + 2 * KV_W
Q_OFF = 3 * CONV_W
K_OFF = Q_OFF + ATTN_W
V_OFF = K_OFF + KV_W
RMS_EPS = 1e-5
MASK_VALUE = -1e30
SM_SCALE = HEAD_DIM ** -0.5
FFN_RES_SCALE = 0.5

ADAM_LR = 0.001
ADAM_B1 = 0.9
ADAM_B2 = 0.999
ADAM_EPS = 1e-08
ADAM_WD = 0.01
ADAM_STEP = 10

NT_DIMS = (((1,), (1,)), ((), ()))
TN_DIMS = (((0,), (0,)), ((), ()))

VMEM_LIMIT = 62 * 1024 * 1024
FF_CHUNK = 256


def _params(sem, vmem=None):
    return pltpu.CompilerParams(dimension_semantics=sem, vmem_limit_bytes=vmem)


def _behind(body, n_in, after):
    k = len(after)
    if k == 0:
        return body
    return lambda *refs: body(*refs[:n_in], *refs[n_in + k:])


def _rms_stats(xf):
    inv = lax.rsqrt(jnp.mean(xf * xf, axis=-1, keepdims=True) + RMS_EPS)
    return xf * inv, inv


def _rms_bwd(dh, xhat, inv, gain):
    dxhat = dh * gain
    dx = inv * (dxhat - xhat * jnp.mean(dxhat * xhat, axis=-1, keepdims=True))
    dgain = jnp.sum(dh * xhat, axis=0, keepdims=True)
    return dx, dgain


def _load_resident(w_hbm, w_ref, sem):
    @pl.when(pl.program_id(0) == 0)
    def _():
        cp = pltpu.make_async_copy(w_hbm, w_ref, sem)
        cp.start()
        cp.wait()


def _ffn_fwd(x, gain, w_gu, w_d=None, *, name, head=None, after=(), tm=256, sub=256, tf=FF_CHUNK):
    t = x.shape[0]
    tm = min(tm, t)
    sub = min(sub, tm)
    n_down = 0 if w_d is None else 1
    n_head = 0 if head is None else 2
    assert n_down or not n_head
    n_in = 3 + n_down + n_head

    def body(*refs):
        x_ref, g_ref = refs[:2]
        w_hbms, head_refs = refs[2:3 + n_down], refs[3 + n_down:n_in]
        xo_refs = refs[n_in:n_in + n_down]
        h_ref, s_ref, sa_ref, sb_ref = refs[n_in + n_down:n_in + n_down + 4]
        head_outs = refs[n_in + n_down + 4:n_in + n_down + 4 + n_head]
        w_refs, sems = refs[n_in + n_down + 4 + n_head:-1], refs[-1]
        for k in range(1 + n_down):
            _load_resident(w_hbms[k], w_refs[k], sems.at[k])

        @pl.when(pl.program_id(0) == 0)
        def _():
            for ref in head_outs:
                ref[...] = jnp.zeros_like(ref)

        for r0 in range(0, tm, sub):
            rows = slice(r0, r0 + sub)
            xf = x_ref[rows, :]
            xhat, _ = _rms_stats(xf)
            h = (xhat * g_ref[...]).astype(BF16)
            h_ref[rows, :] = h
            for c in range(0, D_FF, tf):
                cols = slice(c, min(c + tf, D_FF))
                a = lax.dot_general(h, w_refs[0][0, cols, :], NT_DIMS, preferred_element_type=F32)
                b = lax.dot_general(h, w_refs[0][1, cols, :], NT_DIMS, preferred_element_type=F32)
                sig = jax.nn.sigmoid(a)
                silu = a * sig
                s_ref[rows, cols] = (silu * b).astype(BF16)
                sa_ref[rows, cols] = (b * (sig * (1.0 + a * (1.0 - sig)))).astype(BF16)
                sb_ref[rows, cols] = silu.astype(BF16)
            if not n_down:
                continue
            xo = xf + FFN_RES_SCALE * jnp.dot(s_ref[rows, :], w_refs[1][...], preferred_element_type=F32)
            if head is None:
                xo_refs[0][rows, :] = xo
            else:
                fg_ref, t_ref = head_refs
                loss_ref, dfg_ref = head_outs
                xhat_o, inv_o = _rms_stats(xo)
                err = xhat_o * fg_ref[...] - t_ref[rows, :]
                loss_ref[...] += 0.5 * jnp.sum(jnp.mean(err * err, axis=-1, keepdims=True), axis=0, keepdims=True)
                xo_refs[0][rows, :], dfg = _rms_bwd(err * (1.0 / D_MODEL), xhat_o, inv_o, fg_ref[...])
                dfg_ref[...] += dfg

    row = pl.BlockSpec((tm, D_MODEL), lambda i: (i, 0))
    hid = pl.BlockSpec((tm, D_FF), lambda i: (i, 0))
    vec = pl.BlockSpec((1, D_MODEL), lambda i: (0, 0))
    head_in = [] if head is None else [vec, row]
    head_out = [] if head is None else [pl.BlockSpec((1, 1), lambda i: (0, 0)), vec]
    head_shape = [] if head is None else [jax.ShapeDtypeStruct((1, 1), F32), jax.ShapeDtypeStruct((1, D_MODEL), F32)]
    return pl.pallas_call(
        _behind(body, n_in, after), name=name, grid=(t // tm,),
        in_specs=[row, vec] + [ANY] * (1 + n_down) + head_in + [ANY] * len(after),
        out_specs=[row] * (n_down + 1) + [hid, hid, hid] + head_out,
        out_shape=[jax.ShapeDtypeStruct((t, D_MODEL), F32)] * n_down + [jax.ShapeDtypeStruct((t, D_MODEL), BF16)]
        + [jax.ShapeDtypeStruct((t, D_FF), BF16)] * 3 + head_shape,
        scratch_shapes=[pltpu.VMEM((2, D_FF, D_MODEL), BF16)] + [pltpu.VMEM((D_FF, D_MODEL), BF16)] * n_down
        + [pltpu.SemaphoreType.DMA((2,))],
        compiler_params=_params(("arbitrary",), VMEM_LIMIT),
    )(x, gain, w_gu, *([] if w_d is None else [w_d]), *(head or ()), *after)


def _ffn_down(x, s, w_d, *, name, after=(), tm=512):
    t = x.shape[0]
    tm = min(tm, t)

    def body(x_ref, s_ref, w_hbm, xo_ref, w_ref, sem):
        _load_resident(w_hbm, w_ref, sem)
        xo_ref[...] = x_ref[...] + FFN_RES_SCALE * jnp.dot(s_ref[...], w_ref[...], preferred_element_type=F32)

    row = pl.BlockSpec((tm, D_MODEL), lambda i: (i, 0))
    return pl.pallas_call(
        _behind(body, 3, after), name=name, grid=(t // tm,),
        in_specs=[row, pl.BlockSpec((tm, D_FF), lambda i: (i, 0)), ANY] + [ANY] * len(after), out_specs=row,
        out_shape=jax.ShapeDtypeStruct((t, D_MODEL), F32),
        scratch_shapes=[pltpu.VMEM((D_FF, D_MODEL), BF16), pltpu.SemaphoreType.DMA(())],
        compiler_params=_params(("arbitrary",), VMEM_LIMIT),
    )(x, s, w_d, *after)


def _ffn_dgrad(dxo, x, gain, sa, sb, w_gu, w_d, *, name, after=(), tm=512, sub=512, tf=FF_CHUNK):
    t = x.shape[0]
    tm = min(tm, t)
    sub = min(sub, tm)

    def body(dxo_ref, x_ref, g_ref, sa_ref, sb_ref, wgu_hbm, wd_hbm, dxi_ref, da_ref, db_ref, gb_ref, dg_ref,
             wgu_ref, wd_ref, sems):
        _load_resident(wgu_hbm, wgu_ref, sems.at[0])
        _load_resident(wd_hbm, wd_ref, sems.at[1])

        @pl.when(pl.program_id(0) == 0)
        def _():
            dg_ref[...] = jnp.zeros_like(dg_ref)

        for r0 in range(0, tm, sub):
            rows = slice(r0, r0 + sub)
            go = dxo_ref[rows, :]
            gb = (FFN_RES_SCALE * go).astype(BF16)
            gb_ref[rows, :] = gb
            for c in range(0, D_FF, tf):
                cols = slice(c, min(c + tf, D_FF))
                ds = lax.dot_general(gb, wd_ref[cols, :], NT_DIMS, preferred_element_type=F32)
                da_ref[rows, cols] = (ds * sa_ref[rows, cols].astype(F32)).astype(BF16)
                db_ref[rows, cols] = (ds * sb_ref[rows, cols].astype(F32)).astype(BF16)
            dh = (jnp.dot(da_ref[rows, :], wgu_ref[0], preferred_element_type=F32)
                  + jnp.dot(db_ref[rows, :], wgu_ref[1], preferred_element_type=F32))
            xhat, inv = _rms_stats(x_ref[rows, :])
            dx, dgain = _rms_bwd(dh, xhat, inv, g_ref[...])
            dxi_ref[rows, :] = go + dx
            dg_ref[...] += dgain

    row = pl.BlockSpec((tm, D_MODEL), lambda i: (i, 0))
    hid = pl.BlockSpec((tm, D_FF), lambda i: (i, 0))
    vec = pl.BlockSpec((1, D_MODEL), lambda i: (0, 0))
    return pl.pallas_call(
        _behind(body, 7, after), name=name, grid=(t // tm,),
        in_specs=[row, row, vec, hid, hid, ANY, ANY] + [ANY] * len(after),
        out_specs=[row, hid, hid, row, vec],
        out_shape=[jax.ShapeDtypeStruct((t, D_MODEL), F32), jax.ShapeDtypeStruct((t, D_FF), BF16),
                   jax.ShapeDtypeStruct((t, D_FF), BF16),
                   jax.ShapeDtypeStruct((t, D_MODEL), BF16), jax.ShapeDtypeStruct((1, D_MODEL), F32)],
        scratch_shapes=[pltpu.VMEM((2, D_FF, D_MODEL), BF16), pltpu.VMEM((D_FF, D_MODEL), BF16),
                        pltpu.SemaphoreType.DMA((2,))],
        compiler_params=_params(("arbitrary",), VMEM_LIMIT),
    )(dxo, x, gain, sa, sb, w_gu, w_d, *after)


def _tn_matmul(a, b, *, name, bm, after=(), tk=2048):
    t, m = a.shape
    n = b.shape[1]
    tk = min(tk, t)
    nk = t // tk

    def body(a_ref, b_ref, o_ref):
        @pl.when(pl.program_id(1) == 0)
        def _():
            o_ref[...] = jnp.zeros_like(o_ref)

        o_ref[...] += lax.dot_general(a_ref[...], b_ref[...], TN_DIMS, preferred_element_type=F32)

    return pl.pallas_call(
        _behind(body, 2, after), name=name, grid=(m // bm, nk),
        in_specs=[pl.BlockSpec((tk, bm), lambda i, k: (k, i)), pl.BlockSpec((tk, n), lambda i, k: (k, 0))]
        + [ANY] * len(after),
        out_specs=pl.BlockSpec((bm, n), lambda i, k: (i, 0)),
        out_shape=jax.ShapeDtypeStruct((m, n), F32),
        compiler_params=_params(("parallel", "arbitrary"), VMEM_LIMIT),
    )(a, b, *after)


def _rope_tables(t):
    inv_freq = ROPE_THETA ** (-jnp.arange(0, ROT_DIM, 2, dtype=F32) / ROT_DIM)
    ang = inv_freq[:, None] * jnp.arange(t, dtype=F32)[None, :]
    compact = jnp.stack([jnp.cos(ang), jnp.sin(ang)])
    tr = min(1024, t)

    def body(c_ref, o_ref):
        for k in range(2):
            o_ref[k] = jnp.tile(c_ref[k], (128 // inv_freq.shape[0], 1)).T

    return pl.pallas_call(
        body, name="rope_tables", grid=(t // tr,),
        in_specs=[pl.BlockSpec((2, inv_freq.shape[0], tr), lambda i: (0, 0, i))],
        out_specs=pl.BlockSpec((2, tr, 128), lambda i: (0, i, 0)),
        out_shape=jax.ShapeDtypeStruct((2, t, 128), F32), compiler_params=_params(("parallel",)),
    )(compact)


def _rope_multipliers(cos_sin):
    half = ROT_DIM // 2
    cos, sin = cos_sin[0], cos_sin[1]
    d = lax.broadcasted_iota(jnp.int32, cos.shape, 1) & (HEAD_DIM - 1)
    mult = jnp.where(d < ROT_DIM, cos, 1.0)
    from_lo = jnp.where((d >= half) & (d < ROT_DIM), sin, 0.0)
    from_hi = jnp.where(d < half, -sin, 0.0)
    return mult, from_lo, from_hi


def _tile_lanes(tab, width):
    return jnp.tile(tab, (1, width // tab.shape[1]))


def _rope(v, tab):
    w = v.shape[1]
    half_rot = ROT_DIM // 2
    return (v * _tile_lanes(tab[0], w)
            + pltpu.roll(v, half_rot, axis=1) * _tile_lanes(tab[1], w)
            + pltpu.roll(v, w - half_rot, axis=1) * _tile_lanes(tab[2], w))


def _rope_bwd(dv, tab):
    w = dv.shape[1]
    half_rot = ROT_DIM // 2
    return (dv * _tile_lanes(tab[0], w)
            + pltpu.roll(dv * _tile_lanes(tab[1], w), w - half_rot, axis=1)
            + pltpu.roll(dv * _tile_lanes(tab[2], w), half_rot, axis=1))


def _shift_rows(v, prev8_ref, n):
    r = lax.broadcasted_iota(jnp.int32, v.shape, 0)
    rolled = pltpu.roll(v, n, axis=0)
    last = prev8_ref[7:8, :]
    if n == 1:
        return jnp.where(r >= 1, rolled, last)
    return jnp.where(r >= 2, rolled, jnp.where(r == 0, prev8_ref[6:7, :], last))


def _shift_rows_up(v, next8_ref, n):
    rows = v.shape[0]
    r = lax.broadcasted_iota(jnp.int32, v.shape, 0)
    rolled = pltpu.roll(v, rows - n, axis=0)
    first = next8_ref[0:1, :]
    if n == 1:
        return jnp.where(r <= rows - 2, rolled, first)
    return jnp.where(r <= rows - 3, rolled, jnp.where(r == rows - 2, first, next8_ref[1:2, :]))


def _lane_half_mask(shape, half):
    lane = lax.broadcasted_iota(jnp.int32, shape, 1)
    return (lane >= HEAD_DIM) if half else (lane < HEAD_DIM)


def _to_kv_lanes(chunk, head, kv):
    if head % 2 != kv:
        chunk = pltpu.roll(chunk, HEAD_DIM, axis=1)
    return jnp.where(_lane_half_mask(chunk.shape, kv), chunk, 0.0)


def _from_kv_lanes(chunk, head, kv):
    chunk = jnp.where(_lane_half_mask(chunk.shape, kv), chunk, 0.0)
    if head % 2 != kv:
        chunk = pltpu.roll(chunk, HEAD_DIM, axis=1)
    return chunk


def _stack_heads(wide):
    parts = []
    for head in range(N_Q_HEADS):
        chunk = wide[:, (head // 2) * 128:(head // 2 + 1) * 128]
        parts.append(_to_kv_lanes(chunk, head, head // Q_PER_KV))
    return jnp.concatenate(parts, axis=0)


def _window_mask(has_prev):
    shape = (N_Q_HEADS * BLOCK, 2 * BLOCK)
    qi = lax.broadcasted_iota(jnp.int32, shape, 0) & (BLOCK - 1)
    kj = lax.broadcasted_iota(jnp.int32, shape, 1)
    first_key = BLOCK - has_prev * BLOCK
    in_prev = (kj < BLOCK) & (kj > qi) & (kj >= first_key)
    in_own = (kj >= BLOCK) & ((kj - BLOCK) <= qi)
    return in_prev | in_own


def _sink_column(sink_ref):
    row = lax.broadcasted_iota(jnp.int32, (N_Q_HEADS * BLOCK, 1), 0)
    col = jnp.full((N_Q_HEADS * BLOCK, 1), sink_ref[0, 0], F32)
    for head in range(1, N_Q_HEADS):
        col = jnp.where(row >= head * BLOCK, sink_ref[0, head], col)
    return col


def _softmax_with_sink(q4, k2, mask, sink):
    s = lax.dot_general(q4, k2, NT_DIMS, preferred_element_type=F32) * SM_SCALE
    s = jnp.where(mask, s, MASK_VALUE)
    m = jnp.maximum(jnp.max(s, axis=-1, keepdims=True), sink)
    p = jnp.exp(s - m)
    e_sink = jnp.exp(sink - m)
    inv_den = 1.0 / (jnp.sum(p, axis=-1, keepdims=True) + e_sink)
    return p * inv_den, e_sink * inv_den


def _conv_terms(zf, prev8_ref, w_ref):
    b_gate, c_gate, u = zf[:, 0:CONV_W], zf[:, CONV_W:2 * CONV_W], zf[:, 2 * CONV_W:3 * CONV_W]
    vc = c_gate * u
    vm1 = _shift_rows(vc, prev8_ref, 1)
    vm2 = _shift_rows(vc, prev8_ref, 2)
    conv = w_ref[0:1, :] * vm2 + w_ref[1:2, :] * vm1 + w_ref[2:3, :] * vc
    return b_gate, c_gate, u, vc, vm1, vm2, conv


def _mixer_fwd(x, gain, win_t, wout, conv_w, sinks, rope, *, name, after=(), tq=512):
    t = x.shape[0]
    tq = min(tq, t)
    nblk = tq // BLOCK

    def body(x_ref, g_ref, win_hbm, wout_hbm, cw_ref, sink_ref, rope_ref,
             xo_ref, h_ref, z_ref, y_ref, kprev_ref, vprev_ref, cprev_ref, win_ref, wout_ref, sems):
        i = pl.program_id(0)
        _load_resident(win_hbm, win_ref, sems.at[0])
        _load_resident(wout_hbm, wout_ref, sems.at[1])

        @pl.when(i == 0)
        def _():
            kprev_ref[...] = jnp.zeros_like(kprev_ref)
            vprev_ref[...] = jnp.zeros_like(vprev_ref)
            cprev_ref[...] = jnp.zeros_like(cprev_ref)

        xf = x_ref[...]
        xhat, _ = _rms_stats(xf)
        h = (xhat * g_ref[...]).astype(BF16)
        h_ref[...] = h
        zb = lax.dot_general(h, win_ref[...], NT_DIMS, preferred_element_type=F32).astype(BF16)
        z_ref[...] = zb
        zf = zb.astype(F32)

        b_gate, _, _, vc, _, _, conv = _conv_terms(zf, cprev_ref, cw_ref)
        y_conv = b_gate * conv
        cprev_ref[...] = vc[tq - 8:tq, :]

        tab = _rope_multipliers(rope_ref[...])
        qr = _rope(zf[:, Q_OFF:K_OFF], tab)
        kr = _rope(zf[:, K_OFF:V_OFF], tab).astype(BF16)
        vb = zb[:, V_OFF:Z_W]

        y_attn = []
        for j in range(nblk):
            rows = slice(j * BLOCK, (j + 1) * BLOCK)
            prev = slice((j - 1) * BLOCK, j * BLOCK)
            k2 = jnp.concatenate([kprev_ref[...] if j == 0 else kr[prev], kr[rows]], axis=0)
            v2 = jnp.concatenate([vprev_ref[...] if j == 0 else vb[prev], vb[rows]], axis=0)
            mask = _window_mask(jnp.minimum(i, 1) if j == 0 else 1)
            q8 = _stack_heads(qr[rows]).astype(BF16)
            probs, _ = _softmax_with_sink(q8, k2, mask, _sink_column(sink_ref))
            o8 = jnp.dot(probs.astype(BF16), v2, preferred_element_type=F32)
            chunks = [jnp.zeros((BLOCK, 128), F32) for _ in range(ATTN_W // 128)]
            for head in range(N_Q_HEADS):
                chunks[head // 2] += _from_kv_lanes(o8[head * BLOCK:(head + 1) * BLOCK], head, head // Q_PER_KV)
            y_attn.append(jnp.concatenate(chunks, axis=1))
        kprev_ref[...] = kr[tq - BLOCK:tq]
        vprev_ref[...] = vb[tq - BLOCK:tq]
        y = jnp.concatenate([y_conv, jnp.concatenate(y_attn, axis=0)], axis=1).astype(BF16)
        y_ref[...] = y
        xo_ref[...] = xf + jnp.dot(y, wout_ref[...], preferred_element_type=F32)

    row = pl.BlockSpec((tq, D_MODEL), lambda i: (i, 0))
    full = lambda shape: pl.BlockSpec(shape, lambda i: (0,) * len(shape))
    return pl.pallas_call(
        _behind(body, 7, after), name=name, grid=(t // tq,),
        in_specs=[row, full((1, D_MODEL)), ANY, ANY, full((3, CONV_W)),
                  pl.BlockSpec(memory_space=pltpu.SMEM), pl.BlockSpec((2, tq, 128), lambda i: (0, i, 0))]
        + [ANY] * len(after),
        out_specs=[row, row, pl.BlockSpec((tq, Z_W), lambda i: (i, 0)), row],
        out_shape=[jax.ShapeDtypeStruct((t, D_MODEL), F32), jax.ShapeDtypeStruct((t, D_MODEL), BF16),
                   jax.ShapeDtypeStruct((t, Z_W), BF16), jax.ShapeDtypeStruct((t, D_MODEL), BF16)],
        scratch_shapes=[pltpu.VMEM((BLOCK, KV_W), BF16), pltpu.VMEM((BLOCK, KV_W), BF16),
                        pltpu.VMEM((8, CONV_W), F32), pltpu.VMEM((Z_W, D_MODEL), BF16),
                        pltpu.VMEM((D_MODEL, D_MODEL), BF16), pltpu.SemaphoreType.DMA((2,))],
        compiler_params=_params(("arbitrary",), VMEM_LIMIT),
    )(x, gain, win_t, wout, conv_w, sinks, rope, *after)


def _mixer_bwd(dxo, x, gain, y, z, win_t, wout, conv_w, sinks, rope, *, name, after=(), tq=256):
    t = x.shape[0]
    tq = min(tq, t)
    nt, nblk = t // tq, tq // BLOCK

    def body(dxo_ref, x_ref, g_ref, y_ref, z_ref, zp_ref, win_hbm, wout_hbm, cw_ref, sink_ref, rope_ref, ropep_ref,
             dxi_ref, dz_ref, gb_ref, dcw_ref, dsink_ref, dg_ref, dk_ref, dv_ref, dcn_ref, pvc_ref,
             win_ref, wout_ref, sems):
        i = pl.program_id(0)
        tile = nt - 1 - i
        _load_resident(win_hbm, win_ref, sems.at[0])
        _load_resident(wout_hbm, wout_ref, sems.at[1])

        @pl.when(i == 0)
        def _():
            dk_ref[...] = jnp.zeros_like(dk_ref)
            dv_ref[...] = jnp.zeros_like(dv_ref)
            dcn_ref[...] = jnp.zeros_like(dcn_ref)
            dcw_ref[...] = jnp.zeros_like(dcw_ref)
            dsink_ref[...] = jnp.zeros_like(dsink_ref)
            dg_ref[...] = jnp.zeros_like(dg_ref)

        has_prev = jnp.minimum(tile, 1)
        go = dxo_ref[...]
        gb = go.astype(BF16)
        gb_ref[...] = gb
        dy = lax.dot_general(gb, wout_ref[...], NT_DIMS, preferred_element_type=F32)
        dy_conv, dy_attn = dy[:, 0:CONV_W], dy[:, CONV_W:D_MODEL]
        zb, zpb = z_ref[...], zp_ref[...]
        zf = zb.astype(F32)
        zpf = zpb.astype(F32) * has_prev.astype(F32)

        pvc_ref[...] = (zpf[:, CONV_W:2 * CONV_W] * zpf[:, 2 * CONV_W:3 * CONV_W])[BLOCK - 8:BLOCK, :]
        b_gate, c_gate, u, vc, vm1, vm2, conv = _conv_terms(zf, pvc_ref, cw_ref)
        d_bgate = dy_conv * conv
        dc = dy_conv * b_gate
        tap = lax.broadcasted_iota(jnp.int32, (8, CONV_W), 0)
        dcw_ref[...] += jnp.where(tap == 0, jnp.sum(dc * vm2, axis=0, keepdims=True),
                                  jnp.where(tap == 1, jnp.sum(dc * vm1, axis=0, keepdims=True),
                                            jnp.where(tap == 2, jnp.sum(dc * vc, axis=0, keepdims=True), 0.0)))
        dvc = (cw_ref[2:3, :] * dc + cw_ref[1:2, :] * _shift_rows_up(dc, dcn_ref, 1)
               + cw_ref[0:1, :] * _shift_rows_up(dc, dcn_ref, 2))
        dcn_ref[...] = dc[0:8, :]
        d_cgate = dvc * u
        d_u = dvc * c_gate

        tab, tabp = _rope_multipliers(rope_ref[...]), _rope_multipliers(ropep_ref[...])
        qr = _rope(zf[:, Q_OFF:K_OFF], tab)
        kr = _rope(zf[:, K_OFF:V_OFF], tab).astype(BF16)
        kpr = _rope(zpf[:, K_OFF:V_OFF], tabp).astype(BF16)
        vb, vpb = zb[:, V_OFF:Z_W], zpb[:, V_OFF:Z_W]
        out = y_ref[:, CONV_W:D_MODEL].astype(F32)
        do_out = dy_attn * out
        lane = lax.broadcasted_iota(jnp.int32, (1, 128), 1)
        dsink = jnp.zeros((1, 128), F32)
        dk_next, dv_next = dk_ref[...], dv_ref[...]
        dq_rows, dk_rows, dv_rows = [None] * nblk, [None] * nblk, [None] * nblk
        for j in reversed(range(nblk)):
            rows = slice(j * BLOCK, (j + 1) * BLOCK)
            prev = slice((j - 1) * BLOCK, j * BLOCK)
            k2 = jnp.concatenate([kpr if j == 0 else kr[prev], kr[rows]], axis=0)
            v2 = jnp.concatenate([vpb if j == 0 else vb[prev], vb[rows]], axis=0)
            mask = _window_mask(has_prev if j == 0 else 1)
            q8 = _stack_heads(qr[rows]).astype(BF16)
            do8 = _stack_heads(dy_attn[rows]).astype(BF16)
            delta = jnp.sum(_stack_heads(do_out[rows]), axis=-1, keepdims=True)
            probs, p_sink = _softmax_with_sink(q8, k2, mask, _sink_column(sink_ref))
            dp = lax.dot_general(do8, v2, NT_DIMS, preferred_element_type=F32)
            ds = (probs * (dp - delta) * SM_SCALE).astype(BF16)
            dq8 = jnp.dot(ds, k2, preferred_element_type=F32)
            dk2 = lax.dot_general(ds, q8, TN_DIMS, preferred_element_type=F32)
            dv2 = lax.dot_general(probs.astype(BF16), do8, TN_DIMS, preferred_element_type=F32)
            sink_terms = p_sink * delta
            dq_chunks = [jnp.zeros((BLOCK, 128), F32) for _ in range(ATTN_W // 128)]
            for head in range(N_Q_HEADS):
                grp = slice(head * BLOCK, (head + 1) * BLOCK)
                dq_chunks[head // 2] += _from_kv_lanes(dq8[grp], head, head // Q_PER_KV)
                dsink = dsink - jnp.where(lane == head, jnp.sum(sink_terms[grp], axis=0, keepdims=True), 0.0)
            dq_rows[j] = jnp.concatenate(dq_chunks, axis=1)
            dk_rows[j] = dk2[BLOCK:] + dk_next
            dv_rows[j] = dv2[BLOCK:] + dv_next
            dk_next, dv_next = dk2[:BLOCK], dv2[:BLOCK]
        dk_ref[...] = dk_next
        dv_ref[...] = dv_next
        dsink_ref[...] += dsink
        dq = _rope_bwd(jnp.concatenate(dq_rows, axis=0), tab)
        dk = _rope_bwd(jnp.concatenate(dk_rows, axis=0), tab)
        dv = jnp.concatenate(dv_rows, axis=0)

        dzb = jnp.concatenate([d_bgate, d_cgate, d_u, dq, dk, dv], axis=1).astype(BF16)
        dz_ref[...] = dzb
        dh = jnp.dot(dzb, win_ref[...], preferred_element_type=F32)
        xhat, inv = _rms_stats(x_ref[...])
        dx, dgain = _rms_bwd(dh, xhat, inv, g_ref[...])
        dxi_ref[...] = go + dx
        dg_ref[...] += dgain

    rev = lambda i: (nt - 1 - i, 0)
    block_before = lambda i: jnp.maximum((nt - 1 - i) * nblk - 1, 0)
    row = pl.BlockSpec((tq, D_MODEL), rev)
    full = lambda shape: pl.BlockSpec(shape, lambda i: (0,) * len(shape))
    return pl.pallas_call(
        _behind(body, 12, after), name=name, grid=(nt,),
        in_specs=[row, row, full((1, D_MODEL)), row,
                  pl.BlockSpec((tq, Z_W), rev), pl.BlockSpec((BLOCK, Z_W), lambda i: (block_before(i), 0)),
                  ANY, ANY, full((3, CONV_W)),
                  pl.BlockSpec(memory_space=pltpu.SMEM),
                  pl.BlockSpec((2, tq, 128), lambda i: (0, nt - 1 - i, 0)),
                  pl.BlockSpec((2, BLOCK, 128), lambda i: (0, block_before(i), 0))] + [ANY] * len(after),
        out_specs=[row, pl.BlockSpec((tq, Z_W), rev), row, full((8, CONV_W)), full((1, 128)), full((1, D_MODEL))],
        out_shape=[jax.ShapeDtypeStruct((t, D_MODEL), F32), jax.ShapeDtypeStruct((t, Z_W), BF16),
                   jax.ShapeDtypeStruct((t, D_MODEL), BF16), jax.ShapeDtypeStruct((8, CONV_W), F32),
                   jax.ShapeDtypeStruct((1, 128), F32), jax.ShapeDtypeStruct((1, D_MODEL), F32)],
        scratch_shapes=[pltpu.VMEM((BLOCK, KV_W), F32), pltpu.VMEM((BLOCK, KV_W), F32), pltpu.VMEM((8, CONV_W), F32),
                        pltpu.VMEM((8, CONV_W), F32), pltpu.VMEM((Z_W, D_MODEL), BF16),
                        pltpu.VMEM((D_MODEL, D_MODEL), BF16), pltpu.SemaphoreType.DMA((2,))],
        compiler_params=_params(("arbitrary",), VMEM_LIMIT),
    )(dxo, x, gain, y, z, z, win_t, wout, conv_w, sinks, rope, rope, *after)


def _place():
    x, y, c = lax.axis_index("x"), lax.axis_index("y"), lax.axis_index("c")
    other_chips = [(1 - x, y), (x, 1 - y), (1 - x, 1 - y)]
    return x, y, c, other_chips


def _all_gather_rows(shards, place=(), *, name):
    n, p = len(shards), len(place)

    def body(*refs):
        srcs, place_srcs = refs[:n], refs[n:n + p]
        outs, place_outs = refs[n + p:2 * n + p], refs[2 * n + p:2 * (n + p)]
        send_sems, recv_sems, local_sems = refs[2 * (n + p):]
        x, y, c, chips = _place()
        me, sibling = (x, y, c), (x, y, 1 - c)

        def rows(t, px, py, pc):
            r = srcs[t].shape[-2]
            start = pl.multiple_of((4 * px + 2 * py + pc) * r, 16 if r % 16 == 0 else 8)
            if len(srcs[t].shape) == 3:
                return outs[t].at[:, pl.ds(start, r), :]
            return outs[t].at[pl.ds(start, r), :]

        def copy(t, k, block, to, own=False):
            return pltpu.make_async_remote_copy(
                src_ref=srcs[t] if own else rows(t, *block), dst_ref=rows(t, *block),
                send_sem=send_sems.at[t, k], recv_sem=recv_sems.at[t, k], device_id=to, device_id_type=MESH)

        mine = [pltpu.make_async_copy(srcs[t], rows(t, *me), local_sems.at[t]) for t in range(n)]
        mine += [pltpu.make_async_copy(place_srcs[q],
                                       _block_rows(place_outs[q], place_srcs[q].shape[-2], 4 * x + 2 * y + c),
                                       local_sems.at[n + q]) for q in range(p)]
        for q in range(p):
            mine[n + q].start()
        first = []
        for t in range(n):
            mine[t].start()
            first.append(copy(t, 0, me, sibling, own=True))
            first += [copy(t, 1 + j, me, (*chip, c), own=True) for j, chip in enumerate(chips)]
        for cp in first:
            cp.start()
        passed = []
        for j, chip in enumerate(chips):
            for t in range(n):
                copy(t, 1 + j, (*chip, c), me).wait_recv()
                fwd = copy(t, 4 + j, (*chip, c), sibling)
                fwd.start()
                passed.append(fwd)
        for t in range(n):
            copy(t, 0, sibling, me).wait_recv()
            for j, chip in enumerate(chips):
                copy(t, 4 + j, (*chip, 1 - c), me).wait_recv()
        for cp in first + passed:
            cp.wait_send()
        for cp in mine:
            cp.wait()

    out_shape = [jax.ShapeDtypeStruct(s.shape[:-2] + (N_DEV * s.shape[-2], s.shape[-1]), s.dtype)
                 for s in list(shards) + list(place)]
    res = pl.pallas_call(
        body, name=name, in_specs=[ANY] * (n + p), out_specs=[ANY] * (n + p), out_shape=out_shape,
        scratch_shapes=[pltpu.SemaphoreType.DMA((n, 7)), pltpu.SemaphoreType.DMA((n, 7)),
                        pltpu.SemaphoreType.DMA((n + p,))],
    )(*shards, *place)
    return res[:n], res[n:]


def _split_start(bufs, n_copies, plan, *, name, after=()):
    n = len(bufs)

    def body(*refs):
        token = refs[-1]
        for cp in plan(refs[:n], refs[n], refs[n + 1]):
            cp.start()
        token[...] = jnp.zeros_like(token)

    res = pl.pallas_call(
        _behind(body, n, after), name=name, in_specs=[HBM_SPEC] * n + [ANY] * len(after),
        out_specs=(SEM_SPEC, SEM_SPEC, *[HBM_SPEC] * n, pl.BlockSpec(memory_space=pltpu.VMEM)),
        out_shape=(pltpu.SemaphoreType.DMA((n_copies,)), pltpu.SemaphoreType.DMA((n_copies,)),
                   *[pltpu.HBM(b.shape, b.dtype) for b in bufs], jax.ShapeDtypeStruct((8, 128), F32)),
        input_output_aliases={i: 2 + i for i in range(n)},
        compiler_params=pltpu.CompilerParams(has_side_effects=DATAFLOW),
    )(*[pltpu.with_memory_space_constraint(b, pltpu.HBM) for b in bufs], *after)
    return res[0], res[1], list(res[2:2 + n]), res[-1]


def _split_wait(send_sems, recv_sems, bufs, after, plan, *, name):
    n = len(bufs)

    def body(*refs):
        for cp in plan(refs[:n], refs[n], refs[n + 1]):
            cp.wait_send()
            cp.wait_recv()

    return list(pl.pallas_call(
        body, name=name, in_specs=[HBM_SPEC] * n + [SEM_SPEC, SEM_SPEC, ANY], out_specs=[HBM_SPEC] * n,
        out_shape=tuple(pltpu.HBM(b.shape, b.dtype) for b in bufs),
        input_output_aliases={i: i for i in range(n)},
        compiler_params=pltpu.CompilerParams(has_side_effects=DATAFLOW),
    )(*bufs, send_sems, recv_sems, after))


def _sibling_plan(n):
    def plan(bufs, send_sems, recv_sems):
        x, y, c, _ = _place()
        return [pltpu.make_async_remote_copy(
            src_ref=bufs[t].at[:, 1 - c], dst_ref=bufs[n + t], send_sem=send_sems.at[t], recv_sem=recv_sems.at[t],
            device_id=(x, y, 1 - c), device_id_type=MESH) for t in range(n)]
    return plan


def _block_rows(ref, r, blk):
    start = pl.multiple_of(blk * r, 16 if r % 16 == 0 else 8)
    return ref.at[(slice(None),) * (len(ref.shape) - 2) + (pl.ds(start, r), slice(None))]


def _remote(src, dst, send_sems, recv_sems, k, peer):
    return pltpu.make_async_remote_copy(src_ref=src, dst_ref=dst, send_sem=send_sems.at[k], recv_sem=recv_sems.at[k],
                                        device_id=peer, device_id_type=MESH)


def _gather_send_plan(n, first=0):
    def plan(bufs, send_sems, recv_sems):
        x, y, c, chips = _place()
        peers = [(x, y, 1 - c)] + [(px, py, c) for px, py in chips]
        copies = []
        for t in range(n):
            dst = _block_rows(bufs[n + t], bufs[t].shape[-2], 4 * x + 2 * y + c)
            copies += [_remote(bufs[t], dst, send_sems, recv_sems, 4 * (first + t) + k, peer)
                       for k, peer in enumerate(peers)]
        return copies
    return plan


def _gather_forward_plan(rows):
    def plan(bufs, send_sems, recv_sems):
        x, y, c, chips = _place()
        copies = []
        for t, r in enumerate(rows):
            for j, (px, py) in enumerate(chips):
                blk = _block_rows(bufs[t], r, 4 * px + 2 * py + c)
                copies.append(_remote(blk, blk, send_sems, recv_sems, 3 * t + j, (x, y, 1 - c)))
        return copies
    return plan


def _chips_plan(n, with_small):
    def plan(bufs, send_sems, recv_sems):
        x, y, c, chips = _place()
        copies = []
        for t in range(n):
            for j, (px, py) in enumerate(chips):
                copies.append(_remote(bufs[t].at[2 * px + py], bufs[n + t].at[j], send_sems, recv_sems, 3 * t + j,
                                      (px, py, c)))
        if with_small:
            mine = _block_rows(bufs[2 * n], 8, 4 * x + 2 * y + c)
            flips = [(fx, fy, fc) for fx in range(2) for fy in range(2) for fc in range(2)][1:]
            for k, (fx, fy, fc) in enumerate(flips):
                peer = (x + fx - 2 * x * fx, y + fy - 2 * y * fy, c + fc - 2 * c * fc)
                copies.append(_remote(mine, mine, send_sems, recv_sems, 3 * n + k, peer))
        return copies
    return plan


def _place_own(fulls, shards, index, *, name):
    n = len(fulls)

    def body(index_ref, *refs):
        for t in range(n):
            refs[2 * n + t][...] = refs[n + t][...]

    def block_of(shard):
        lead = len(shard.shape) - 2
        return pl.BlockSpec(shard.shape, lambda i, index_ref: (0,) * lead + (index_ref[0], 0))

    def whole(shard):
        return pl.BlockSpec(shard.shape, lambda i, index_ref: (0,) * len(shard.shape))

    return list(pl.pallas_call(
        body, name=name,
        grid_spec=pltpu.PrefetchScalarGridSpec(
            num_scalar_prefetch=1, grid=(1,),
            in_specs=[ANY] * n + [whole(s) for s in shards], out_specs=[block_of(s) for s in shards]),
        out_shape=[jax.ShapeDtypeStruct(f.shape, f.dtype) for f in fulls],
        input_output_aliases={1 + t: t for t in range(n)},
        compiler_params=_params(("arbitrary",)),
    )(index, *fulls, *shards))


N_STEPS_SMALL = 2


def _add_sibling(grads, recvs, place, *, name):
    n = len(grads)

    def body(place_ref, *refs):
        chip = place_ref[1]
        for t in range(n):
            g_ref, r_ref, own_ref, ob_ref = refs[2 * t], refs[2 * t + 1], refs[2 * n + 2 * t], refs[2 * n + 2 * t + 1]
            own = jnp.zeros(own_ref.shape, F32)
            for m in range(4):
                p = g_ref[m, 0] + r_ref[m]
                ob_ref[m] = p.astype(BF16)
                own = jnp.where(chip == m, p, own)
            own_ref[...] = own

    in_specs, out_specs, out_shape = [], [], []
    for g, r in zip(grads, recvs):
        tr = g.shape[2] // N_STEPS_SMALL
        blocks = pl.BlockSpec((4, tr, D_MODEL), lambda i, place_ref: (0, i, 0))
        in_specs += [pl.BlockSpec((4, 1, tr, D_MODEL), lambda i, place_ref: (0, place_ref[0], i, 0)), blocks]
        out_specs += [pl.BlockSpec((tr, D_MODEL), lambda i, place_ref: (i, 0)), blocks]
        out_shape += [jax.ShapeDtypeStruct(r.shape[1:], F32), jax.ShapeDtypeStruct(r.shape, BF16)]
    res = pl.pallas_call(
        body, name=name,
        grid_spec=pltpu.PrefetchScalarGridSpec(num_scalar_prefetch=1, grid=(N_STEPS_SMALL,), in_specs=in_specs,
                                               out_specs=out_specs),
        out_shape=out_shape, compiler_params=_params(("arbitrary",), VMEM_LIMIT),
    )(place, *[a for pair in zip(grads, recvs) for a in pair])
    return [(res[2 * t], res[2 * t + 1]) for t in range(n)]


def _reduce_adamw(parts, *, name, after=()):
    n = len(parts)

    def body(*refs):
        for t in range(n):
            p_ref, r_ref, w_ref, m_ref, v_ref = refs[5 * t:5 * t + 5]
            g_ref, d_ref, mo_ref, vo_ref = refs[5 * n + 4 * t:5 * n + 4 * t + 4]
            g = p_ref[...] + r_ref[0].astype(F32) + r_ref[1].astype(F32) + r_ref[2].astype(F32)
            g_ref[...] = g
            d_ref[...], mo_ref[...], vo_ref[...] = _adamw_math(w_ref[...], g, m_ref[...], v_ref[...])
        refs[-1][...] = jnp.zeros_like(refs[-1])

    in_specs, out_specs, out_shape = [], [], []
    for own, _, _, _, _ in parts:
        rows = own.shape[0]
        tr = rows // N_STEPS_SMALL
        spec = pl.BlockSpec((tr, D_MODEL), lambda i: (i, 0))
        in_specs += [spec, pl.BlockSpec((3, tr, D_MODEL), lambda i: (0, i, 0)), spec, spec, spec]
        out_specs += [spec] * 4
        out_shape += [jax.ShapeDtypeStruct((rows, D_MODEL), F32)] * 4
    res = pl.pallas_call(
        _behind(body, 5 * n, after), name=name, grid=(N_STEPS_SMALL,),
        in_specs=in_specs + [ANY] * len(after),
        out_specs=out_specs + [pl.BlockSpec((8, 128), lambda i: (0, 0))],
        out_shape=out_shape + [jax.ShapeDtypeStruct((8, 128), F32)],
        compiler_params=_params(("arbitrary",), VMEM_LIMIT),
    )(*[a for part in parts for a in part], *after)
    return [tuple(res[4 * t:4 * t + 4]) for t in range(n)], res[-1]


def _adamw_math(w, g, m, v):
    m = ADAM_B1 * m + (1.0 - ADAM_B1) * g
    v = ADAM_B2 * v + (1.0 - ADAM_B2) * (g * g)
    m_hat = m / (1.0 - ADAM_B1 ** ADAM_STEP)
    v_hat = v / (1.0 - ADAM_B2 ** ADAM_STEP)
    delta = -ADAM_LR * (m_hat / (jnp.sqrt(v_hat) + ADAM_EPS) + ADAM_WD * w)
    return delta, m, v


def _adamw(w, g, m, v, *, name, tr, after=()):
    rows, cols = w.shape

    def body(w_ref, g_ref, m_ref, v_ref, d_ref, mo_ref, vo_ref):
        d_ref[...], mo_ref[...], vo_ref[...] = _adamw_math(w_ref[...], g_ref[...], m_ref[...], v_ref[...])

    spec = pl.BlockSpec((tr, cols), lambda i: (i, 0))
    return pl.pallas_call(
        _behind(body, 4, after), name=name, grid=(rows // tr,), in_specs=[spec] * 4 + [ANY] * len(after),
        out_specs=[spec] * 3, out_shape=[jax.ShapeDtypeStruct(w.shape, F32)] * 3,
        compiler_params=_params(("parallel",)),
    )(w, g, m, v, *after)


def _sum_small(gathered, *, name):
    def body(g_ref, o_ref):
        acc = g_ref[0]
        for k in range(1, N_DEV):
            acc = acc + g_ref[k]
        o_ref[...] = acc

    return pl.pallas_call(body, name=name, out_shape=jax.ShapeDtypeStruct(gathered.shape[1:], F32))(gathered)


def kernel(x, ffn1_norm, ffn1_w_gate, ffn1_w_up, ffn1_w_down, mix_norm, w_in, conv_w, attn_sinks, w_out, ffn2_norm, ffn2_w_gate, ffn2_w_up, ffn2_w_down, final_norm, loss_target, m_ffn1_norm, m_ffn1_w_gate, m_ffn1_w_up, m_ffn1_w_down, m_mix_norm, m_w_in, m_conv_w, m_attn_sinks, m_w_out, m_ffn2_norm, m_ffn2_w_gate, m_ffn2_w_up, m_ffn2_w_down, m_final_norm, v_ffn1_norm, v_ffn1_w_gate, v_ffn1_w_up, v_ffn1_w_down, v_mix_norm, v_w_in, v_conv_w, v_attn_sinks, v_w_out, v_ffn2_norm, v_ffn2_w_gate, v_ffn2_w_up, v_ffn2_w_down, v_final_norm):
    ix, iy, ic = lax.axis_index("x"), lax.axis_index("y"), lax.axis_index("c")
    my_index = 4 * ix + 2 * iy + ic
    place = jnp.stack([ic, 2 * ix + iy]).astype(jnp.int32)

    given = dict(ffn1_norm=ffn1_norm, ffn1_w_gate=ffn1_w_gate, ffn1_w_up=ffn1_w_up, ffn1_w_down=ffn1_w_down,
                 mix_norm=mix_norm, w_in=w_in, conv_w=conv_w, attn_sinks=attn_sinks, w_out=w_out, ffn2_norm=ffn2_norm,
                 ffn2_w_gate=ffn2_w_gate, ffn2_w_up=ffn2_w_up, ffn2_w_down=ffn2_w_down, final_norm=final_norm)
    moments_m = dict(ffn1_norm=m_ffn1_norm, ffn1_w_gate=m_ffn1_w_gate, ffn1_w_up=m_ffn1_w_up, ffn1_w_down=m_ffn1_w_down,
                     mix_norm=m_mix_norm, w_in=m_w_in, conv_w=m_conv_w, attn_sinks=m_attn_sinks, w_out=m_w_out,
                     ffn2_norm=m_ffn2_norm, ffn2_w_gate=m_ffn2_w_gate, ffn2_w_up=m_ffn2_w_up, ffn2_w_down=m_ffn2_w_down,
                     final_norm=m_final_norm)
    moments_v = dict(ffn1_norm=v_ffn1_norm, ffn1_w_gate=v_ffn1_w_gate, ffn1_w_up=v_ffn1_w_up, ffn1_w_down=v_ffn1_w_down,
                     mix_norm=v_mix_norm, w_in=v_w_in, conv_w=v_conv_w, attn_sinks=v_attn_sinks, w_out=v_w_out,
                     ffn2_norm=v_ffn2_norm, ffn2_w_gate=v_ffn2_w_gate, ffn2_w_up=v_ffn2_w_up, ffn2_w_down=v_ffn2_w_down,
                     final_norm=v_final_norm)

    xs = x[0]
    target = loss_target[0]
    final_gain = final_norm.reshape(1, D_MODEL)

    def ffn_shards(wg, wu, wd):
        return jnp.stack([wg[0].T, wu[0].T]).astype(BF16), wd[0].astype(BF16)

    conv_cols = conv_w.shape[2]
    conv_shard = jnp.pad(conv_w[0], ((0, 5), (0, 128 - conv_cols)))
    gate_up1, down1 = ffn_shards(ffn1_w_gate, ffn1_w_up, ffn1_w_down)
    rest_shards = [down1, w_in[0].T.astype(BF16), w_out[0].astype(BF16), conv_shard,
                   *ffn_shards(ffn2_w_gate, ffn2_w_up, ffn2_w_down)]
    rest_rows = [s.shape[-2] for s in rest_shards]
    n_rest, n_early = len(rest_shards), 4
    (w1_gu,), _ = _all_gather_rows([gate_up1], name="gather_ffn1")

    fulls = [lax.empty(s.shape[:-2] + (N_DEV * s.shape[-2], s.shape[-1]), s.dtype) for s in rest_shards]
    fulls = _place_own(fulls, rest_shards, my_index.astype(jnp.int32).reshape(1), name="place_own_weights")
    ssem, rsem, bufs, token = _split_start(rest_shards + list(fulls), 4 * n_rest, _gather_send_plan(n_rest),
                                           name="gather_rest_start", after=[w1_gu])
    early = bufs[:n_early] + bufs[n_rest:n_rest + n_early]
    late = bufs[n_early:n_rest] + bufs[n_rest + n_early:]
    h1, s1, sa1, sb1 = _ffn_fwd(xs, ffn1_norm, w1_gu, name="ffn1_hidden", after=[token])
    early = _split_wait(ssem, rsem, early, h1, _gather_send_plan(n_early), name="gather_early_wait")
    fwd_early = _gather_forward_plan(rest_rows[:n_early])
    ssem_e, rsem_e, parts, token = _split_start(early[n_early:], 3 * n_early, fwd_early, name="forward_early_start")
    w1_d, win_t, wout, conv_all = _split_wait(ssem_e, rsem_e, parts, token, fwd_early, name="forward_early_wait")
    x1 = _ffn_down(xs, s1, w1_d, name="ffn1_down")
    conv_full = conv_all.reshape(N_DEV, 8, 128)[:, :3, :conv_cols].transpose(1, 0, 2).reshape(3, CONV_W)
    late = _split_wait(ssem, rsem, late, x1, _gather_send_plan(n_rest - n_early, first=n_early),
                       name="gather_late_wait")
    fwd_ffn2 = _gather_forward_plan(rest_rows[n_early:])
    ssem, rsem, parts, token = _split_start(late[n_rest - n_early:], 3 * (n_rest - n_early), fwd_ffn2,
                                            name="forward_ffn2_start")
    rope = _rope_tables(xs.shape[0])
    x2, hm, z, y = _mixer_fwd(x1, mix_norm, win_t, wout, conv_full, attn_sinks, rope, name="mixer_fwd", after=[token])
    w2_gu, w2_d = _split_wait(ssem, rsem, parts, x2, fwd_ffn2, name="forward_ffn2_wait")
    dx3, h2, s2, sa2, sb2, loss_local, d_final = _ffn_fwd(x2, ffn2_norm, w2_gu, w2_d, head=(final_gain, target),
                                                          name="ffn2_fwd")

    def to_sibling_start(grads, tag, after=()):
        views = [g.reshape(4, 2, g.shape[0] // N_DEV, D_MODEL) for g in grads]
        lands = [lax.empty((4,) + v.shape[2:], F32) for v in views]
        plan = _sibling_plan(len(views))
        ssem, rsem, bufs, token = _split_start(views + lands, len(views), plan, name=f"{tag}_sibling_start", after=after)
        return (ssem, rsem, bufs, plan, tag), token

    def to_sibling_finish(handle, after, names):
        ssem, rsem, bufs, plan, tag = handle
        bufs = _split_wait(ssem, rsem, bufs, after, plan, name=f"{tag}_sibling_wait")
        n = len(names)
        return _add_sibling(bufs[:n], bufs[n:], place, name=f"add_sibling_{tag}")

    def to_chips_start(partials, tag, small_all=None, after=()):
        p16 = [p for _, p in partials]
        lands = [lax.empty((3,) + p.shape[1:], BF16) for p in p16]
        extra = [] if small_all is None else [small_all]
        plan = _chips_plan(len(p16), small_all is not None)
        ssem, rsem, bufs, token = _split_start(p16 + lands + extra, 3 * len(p16) + 7 * len(extra), plan,
                                               name=f"{tag}_chips_start", after=after)
        return (ssem, rsem, bufs, plan, tag), token

    def to_chips_finish(handle, partials, after, names):
        ssem, rsem, bufs, plan, tag = handle
        bufs = _split_wait(ssem, rsem, bufs, after, plan, name=f"{tag}_chips_wait")
        n = len(names)
        return [(p32, r) for (p32, _), r in zip(partials, bufs[n:2 * n])], bufs[2 * n:]

    half_ff = D_FF // 2
    names2, namesm = ["ffn2_w_gate", "ffn2_w_up", "ffn2_w_down"], ["w_in", "w_out"]
    transposed = {"ffn1_w_gate", "ffn1_w_up", "w_in", "ffn2_w_gate", "ffn2_w_up"}
    grad, delta, new_m, new_v = {}, {}, {}, {}

    def adam_big(names, parts, tag, after=()):
        def to_rows(nm, a):
            return jnp.swapaxes(a, 1, 2)[0] if nm in transposed else a[0]

        def from_rows(nm, a):
            return jnp.swapaxes(a[None], 1, 2) if nm in transposed else a[None]

        operands = [(p32, recv, to_rows(nm, given[nm]), to_rows(nm, moments_m[nm]), to_rows(nm, moments_v[nm]))
                    for nm, (p32, recv) in zip(names, parts)]
        results, token = _reduce_adamw(operands, name=f"adamw_{tag}", after=after)
        for nm, outs in zip(names, results):
            grad[nm], delta[nm], new_m[nm], new_v[nm] = (from_rows(nm, a) for a in outs)
        return token

    dx2, da2, db2, g2b, d_norm2 = _ffn_dgrad(dx3, x2, ffn2_norm, sa2, sb2, w2_gu, w2_d, name="ffn2_dgrad")
    gw2 = [_tn_matmul(da2, h2, name="ffn2_wgrad_gate", bm=half_ff), _tn_matmul(db2, h2, name="ffn2_wgrad_up", bm=half_ff),
           _tn_matmul(s2, g2b, name="ffn2_wgrad_down", bm=half_ff)]
    sib2, tok = to_sibling_start(gw2, "ffn2")
    dx1, dz, gmb, d_conv, d_sink, d_normm = _mixer_bwd(dx2, x1, mix_norm, y, z, win_t, wout, conv_full, attn_sinks,
                                                       rope, name="mixer_bwd", after=[tok])
    p2 = to_sibling_finish(sib2, dx1, names2)
    chips2, tok = to_chips_start(p2, "ffn2")
    gwm = [_tn_matmul(dz, hm, name="mixer_wgrad_in", bm=Z_W // 3, after=[tok]),
           _tn_matmul(y, gmb, name="mixer_wgrad_out", bm=D_MODEL // 2, after=[tok])]
    sibm, tok = to_sibling_start(gwm, "mixer")
    dx0, da1, db1, g1b, d_norm1 = _ffn_dgrad(dx1, xs, ffn1_norm, sa1, sb1, w1_gu, w1_d, name="ffn1_dgrad", after=[tok])
    r2, _ = to_chips_finish(chips2, p2, dx0, names2)
    pm = to_sibling_finish(sibm, dx0, namesm)
    chipsm, tok = to_chips_start(pm, "mixer")
    gw_gate = _tn_matmul(da1, h1, name="ffn1_wgrad_gate", bm=half_ff, after=[tok])
    sib_gate, tok = to_sibling_start([gw_gate], "ffn1_gate")
    gw_up = _tn_matmul(db1, h1, name="ffn1_wgrad_up", bm=half_ff, after=[tok])
    rm, _ = to_chips_finish(chipsm, pm, gw_up, namesm)
    p_gate = to_sibling_finish(sib_gate, gw_up, ["ffn1_w_gate"])
    chips_gate, tok_a = to_chips_start(p_gate, "ffn1_gate")
    sib_up, tok_b = to_sibling_start([gw_up], "ffn1_up", after=[tok_a])
    gw_down = _tn_matmul(s1, g1b, name="ffn1_wgrad_down", bm=half_ff, after=[tok_a, tok_b])
    p_up = to_sibling_finish(sib_up, gw_down, ["ffn1_w_up"])
    chips_up, tok_a = to_chips_start(p_up, "ffn1_up")
    sib_down, tok_b = to_sibling_start([gw_down], "ffn1_down", after=[tok_a])
    p_down = to_sibling_finish(sib_down, tok_b, ["ffn1_w_down"])
    last_row = (jnp.pad(d_sink, ((0, 0), (0, D_MODEL - 128)))
                + jnp.pad(loss_local, ((0, 0), (LOSS_LANE, D_MODEL - LOSS_LANE - 1))))
    small = jnp.concatenate([
        d_norm1, d_normm, d_norm2, d_final, jnp.pad(d_conv[0:3], ((0, 0), (0, D_MODEL - CONV_W))), last_row], axis=0)
    (small_all,) = _place_own([lax.empty((N_DEV * 8, D_MODEL), F32)], [small], my_index.astype(jnp.int32).reshape(1),
                              name="place_own_small")
    chips_down, tok = to_chips_start(p_down, "ffn1_down", small_all)
    tok = adam_big(names2, r2, "ffn2", after=[tok])
    tok = adam_big(namesm, rm, "mixer", after=[tok])
    r_gate, _ = to_chips_finish(chips_gate, p_gate, tok, ["ffn1_w_gate"])
    tok = adam_big(["ffn1_w_gate"], r_gate, "ffn1_gate")
    r_up, _ = to_chips_finish(chips_up, p_up, tok, ["ffn1_w_up"])
    tok = adam_big(["ffn1_w_up"], r_up, "ffn1_up")
    r_down, (small_all,) = to_chips_finish(chips_down, p_down, tok, ["ffn1_w_down"])
    adam_big(["ffn1_w_down"], r_down, "ffn1_down")
    small_sum = _sum_small(small_all.reshape(N_DEV, 8, D_MODEL), name="sum_small")
    loss = small_sum[7, LOSS_LANE]
    _update_small(given, moments_m, moments_v, small_sum, my_index, grad, delta, new_m, new_v)

    order = list(given)
    return (loss, dx0[None], *[grad[n] for n in order], *[delta[n] for n in order],
            *[new_m[n] for n in order], *[new_v[n] for n in order])


def _update_small(given, moments_m, moments_v, small_sum, my_index, grad, delta, new_m, new_v):
    conv_cols = given["conv_w"].shape[2]
    small_g = {
        "ffn1_norm": small_sum[0:1], "mix_norm": small_sum[1:2], "ffn2_norm": small_sum[2:3],
        "final_norm": small_sum[3:4],
        "conv_w": lax.dynamic_slice(small_sum[4:7, :CONV_W], (0, my_index * conv_cols), (3, conv_cols)),
        "attn_sinks": small_sum[7:8, :N_Q_HEADS],
    }

    small_names = ["ffn1_norm", "mix_norm", "ffn2_norm", "final_norm", "conv_w", "attn_sinks"]

    def pack(parts):
        rows = []
        for nm in small_names:
            p = parts[nm]
            p2 = p.reshape(3, conv_cols) if nm == "conv_w" else p.reshape(1, -1)
            rows.append(jnp.pad(p2, ((0, 0), (0, D_MODEL - p2.shape[1]))))
        rows.append(jnp.zeros((8, D_MODEL), F32))
        return jnp.concatenate(rows, axis=0)

    sd, sm, sv = _adamw(pack(given), pack(small_g), pack(moments_m), pack(moments_v), name="adamw_small", tr=16)
    row = 0
    for nm in small_names:
        shape = given[nm].shape
        nrow = 3 if nm == "conv_w" else 1
        ncol = conv_cols if nm == "conv_w" else given[nm].size
        grad[nm] = small_g[nm].reshape(shape)
        delta[nm], new_m[nm], new_v[nm] = (a[row:row + nrow, :ncol].reshape(shape) for a in (sd, sm, sv))
        row += nrow
```

```python
import functools

import jax
import jax.numpy as jnp
from jax import lax
from jax.experimental import pallas as pl
from jax.experimental.pallas import tpu as pltpu

F32 = jnp.float32
BF16 = jnp.bfloat16
MESH = pl.DeviceIdType.MESH
ANY = pl.BlockSpec(memory_space=pl.ANY)
HBM_SPEC = pl.BlockSpec(memory_space=pltpu.HBM)
SEM_SPEC = pl.BlockSpec(memory_space=pltpu.SEMAPHORE)
DATAFLOW = pltpu.SideEffectType.DATAFLOW_SIDE_EFFECTING

N_DEV = 8
LOSS_LANE = 128
D_MODEL = 1024
D_FF = 2816
CONV_W = 512
ATTN_W = 512
KV_W = 128
HEAD_DIM = 64
N_Q_HEADS = 8
N_KV_HEADS = 2
Q_PER_KV = N_Q_HEADS // N_KV_HEADS
BLOCK = 128
ROT_DIM = 16
ROPE_THETA = 500000.0
Z_W = 3 * CONV_W + ATTN_W + 2 * KV_W
Q_OFF = 3 * CONV_W
K_OFF = Q_OFF + ATTN_W
V_OFF = K_OFF + KV_W
RMS_EPS = 1e-5
MASK_VALUE = -1e30
SM_SCALE = HEAD_DIM ** -0.5
FFN_RES_SCALE = 0.5

ADAM_LR = 0.001
ADAM_B1 = 0.9
ADAM_B2 = 0.999
ADAM_EPS = 1e-08
ADAM_WD = 0.01
ADAM_STEP = 10

NT_DIMS = (((1,), (1,)), ((), ()))
TN_DIMS = (((0,), (0,)), ((), ()))

VMEM_LIMIT = 62 * 1024 * 1024
FF_CHUNK = 256


def _params(sem, vmem=None):
    return pltpu.CompilerParams(dimension_semantics=sem, vmem_limit_bytes=vmem)


def _behind(body, n_in, after):
    k = len(after)
    if k == 0:
        return body
    return lambda *refs: body(*refs[:n_in], *refs[n_in + k:])


def _rms_stats(xf):
    inv = lax.rsqrt(jnp.mean(xf * xf, axis=-1, keepdims=True) + RMS_EPS)
    return xf * inv, inv


def _rms_bwd(dh, xhat, inv, gain):
    dxhat = dh * gain
    dx = inv * (dxhat - xhat * jnp.mean(dxhat * xhat, axis=-1, keepdims=True))
    dgain = jnp.sum(dh * xhat, axis=0, keepdims=True)
    return dx, dgain


def _load_resident(w_hbm, w_ref, sem):
    @pl.when(pl.program_id(0) == 0)
    def _():
        cp = pltpu.make_async_copy(w_hbm, w_ref, sem)
        cp.start()
        cp.wait()


def _ffn_fwd(x, gain, w_gu, w_d=None, *, name, head=None, after=(), tm=256, sub=256, tf=FF_CHUNK):
    t = x.shape[0]
    tm = min(tm, t)
    sub = min(sub, tm)
    n_down = 0 if w_d is None else 1
    n_head = 0 if head is None else 2
    assert n_down or not n_head
    n_in = 3 + n_down + n_head

    def body(*refs):
        x_ref, g_ref = refs[:2]
        w_hbms, head_refs = refs[2:3 + n_down], refs[3 + n_down:n_in]
        xo_refs = refs[n_in:n_in + n_down]
        h_ref, s_ref, sa_ref, sb_ref = refs[n_in + n_down:n_in + n_down + 4]
        head_outs = refs[n_in + n_down + 4:n_in + n_down + 4 + n_head]
        w_refs, sems = refs[n_in + n_down + 4 + n_head:-1], refs[-1]
        for k in range(1 + n_down):
            _load_resident(w_hbms[k], w_refs[k], sems.at[k])

        @pl.when(pl.program_id(0) == 0)
        def _():
            for ref in head_outs:
                ref[...] = jnp.zeros_like(ref)

        for r0 in range(0, tm, sub):
            rows = slice(r0, r0 + sub)
            xf = x_ref[rows, :]
            xhat, _ = _rms_stats(xf)
            h = (xhat * g_ref[...]).astype(BF16)
            h_ref[rows, :] = h
            for c in range(0, D_FF, tf):
                cols = slice(c, min(c + tf, D_FF))
                a = lax.dot_general(h, w_refs[0][0, cols, :], NT_DIMS, preferred_element_type=F32)
                b = lax.dot_general(h, w_refs[0][1, cols, :], NT_DIMS, preferred_element_type=F32)
                sig = jax.nn.sigmoid(a)
                silu = a * sig
                s_ref[rows, cols] = (silu * b).astype(BF16)
                sa_ref[rows, cols] = (b * (sig * (1.0 + a * (1.0 - sig)))).astype(BF16)
                sb_ref[rows, cols] = silu.astype(BF16)
            if not n_down:
                continue
            xo = xf + FFN_RES_SCALE * jnp.dot(s_ref[rows, :], w_refs[1][...], preferred_element_type=F32)
            if head is None:
                xo_refs[0][rows, :] = xo
            else:
                fg_ref, t_ref = head_refs
                loss_ref, dfg_ref = head_outs
                xhat_o, inv_o = _rms_stats(xo)
                err = xhat_o * fg_ref[...] - t_ref[rows, :]
                loss_ref[...] += 0.5 * jnp.sum(jnp.mean(err * err, axis=-1, keepdims=True), axis=0, keepdims=True)
                xo_refs[0][rows, :], dfg = _rms_bwd(err * (1.0 / D_MODEL), xhat_o, inv_o, fg_ref[...])
                dfg_ref[...] += dfg

    row = pl.BlockSpec((tm, D_MODEL), lambda i: (i, 0))
    hid = pl.BlockSpec((tm, D_FF), lambda i: (i, 0))
    vec = pl.BlockSpec((1, D_MODEL), lambda i: (0, 0))
    head_in = [] if head is None else [vec, row]
    head_out = [] if head is None else [pl.BlockSpec((1, 1), lambda i: (0, 0)), vec]
    head_shape = [] if head is None else [jax.ShapeDtypeStruct((1, 1), F32), jax.ShapeDtypeStruct((1, D_MODEL), F32)]
    return pl.pallas_call(
        _behind(body, n_in, after), name=name, grid=(t // tm,),
        in_specs=[row, vec] + [ANY] * (1 + n_down) + head_in + [ANY] * len(after),
        out_specs=[row] * (n_down + 1) + [hid, hid, hid] + head_out,
        out_shape=[jax.ShapeDtypeStruct((t, D_MODEL), F32)] * n_down + [jax.ShapeDtypeStruct((t, D_MODEL), BF16)]
        + [jax.ShapeDtypeStruct((t, D_FF), BF16)] * 3 + head_shape,
        scratch_shapes=[pltpu.VMEM((2, D_FF, D_MODEL), BF16)] + [pltpu.VMEM((D_FF, D_MODEL), BF16)] * n_down
        + [pltpu.SemaphoreType.DMA((2,))],
        compiler_params=_params(("arbitrary",), VMEM_LIMIT),
    )(x, gain, w_gu, *([] if w_d is None else [w_d]), *(head or ()), *after)


def _ffn_down(x, s, w_d, *, name, after=(), tm=512):
    t = x.shape[0]
    tm = min(tm, t)

    def body(x_ref, s_ref, w_hbm, xo_ref, w_ref, sem):
        _load_resident(w_hbm, w_ref, sem)
        xo_ref[...] = x_ref[...] + FFN_RES_SCALE * jnp.dot(s_ref[...], w_ref[...], preferred_element_type=F32)

    row = pl.BlockSpec((tm, D_MODEL), lambda i: (i, 0))
    return pl.pallas_call(
        _behind(body, 3, after), name=name, grid=(t // tm,),
        in_specs=[row, pl.BlockSpec((tm, D_FF), lambda i: (i, 0)), ANY] + [ANY] * len(after), out_specs=row,
        out_shape=jax.ShapeDtypeStruct((t, D_MODEL), F32),
        scratch_shapes=[pltpu.VMEM((D_FF, D_MODEL), BF16), pltpu.SemaphoreType.DMA(())],
        compiler_params=_params(("arbitrary",), VMEM_LIMIT),
    )(x, s, w_d, *after)


def _ffn_dgrad(dxo, x, gain, sa, sb, w_gu, w_d, *, name, after=(), tm=512, sub=512, tf=FF_CHUNK):
    t = x.shape[0]
    tm = min(tm, t)
    sub = min(sub, tm)

    def body(dxo_ref, x_ref, g_ref, sa_ref, sb_ref, wgu_hbm, wd_hbm, dxi_ref, da_ref, db_ref, gb_ref, dg_ref,
             wgu_ref, wd_ref, sems):
        _load_resident(wgu_hbm, wgu_ref, sems.at[0])
        _load_resident(wd_hbm, wd_ref, sems.at[1])

        @pl.when(pl.program_id(0) == 0)
        def _():
            dg_ref[...] = jnp.zeros_like(dg_ref)

        for r0 in range(0, tm, sub):
            rows = slice(r0, r0 + sub)
            go = dxo_ref[rows, :]
            gb = (FFN_RES_SCALE * go).astype(BF16)
            gb_ref[rows, :] = gb
            for c in range(0, D_FF, tf):
                cols = slice(c, min(c + tf, D_FF))
                ds = lax.dot_general(gb, wd_ref[cols, :], NT_DIMS, preferred_element_type=F32)
                da_ref[rows, cols] = (ds * sa_ref[rows, cols].astype(F32)).astype(BF16)
                db_ref[rows, cols] = (ds * sb_ref[rows, cols].astype(F32)).astype(BF16)
            dh = (jnp.dot(da_ref[rows, :], wgu_ref[0], preferred_element_type=F32)
                  + jnp.dot(db_ref[rows, :], wgu_ref[1], preferred_element_type=F32))
            xhat, inv = _rms_stats(x_ref[rows, :])
            dx, dgain = _rms_bwd(dh, xhat, inv, g_ref[...])
            dxi_ref[rows, :] = go + dx
            dg_ref[...] += dgain

    row = pl.BlockSpec((tm, D_MODEL), lambda i: (i, 0))
    hid = pl.BlockSpec((tm, D_FF), lambda i: (i, 0))
    vec = pl.BlockSpec((1, D_MODEL), lambda i: (0, 0))
    return pl.pallas_call(
        _behind(body, 7, after), name=name, grid=(t // tm,),
        in_specs=[row, row, vec, hid, hid, ANY, ANY] + [ANY] * len(after),
        out_specs=[row, hid, hid, row, vec],
        out_shape=[jax.ShapeDtypeStruct((t, D_MODEL), F32), jax.ShapeDtypeStruct((t, D_FF), BF16),
                   jax.ShapeDtypeStruct((t, D_FF), BF16),
                   jax.ShapeDtypeStruct((t, D_MODEL), BF16), jax.ShapeDtypeStruct((1, D_MODEL), F32)],
        scratch_shapes=[pltpu.VMEM((2, D_FF, D_MODEL), BF16), pltpu.VMEM((D_FF, D_MODEL), BF16),
                        pltpu.SemaphoreType.DMA((2,))],
        compiler_params=_params(("arbitrary",), VMEM_LIMIT),
    )(dxo, x, gain, sa, sb, w_gu, w_d, *after)


def _tn_matmul(a, b, *, name, bm, after=(), tk=2048, blocks=None, into=None):
    t, m = a.shape
    n = b.shape[1]
    tk = min(tk, t)
    nk = t // tk
    first, count = blocks or (0, m // bm)
    behind = ([] if into is None else [into]) + list(after)

    def body(a_ref, b_ref, o_ref):
        @pl.when(pl.program_id(1) == 0)
        def _():
            o_ref[...] = jnp.zeros_like(o_ref)

        o_ref[...] += lax.dot_general(a_ref[...], b_ref[...], TN_DIMS, preferred_element_type=F32)

    return pl.pallas_call(
        _behind(body, 2, behind), name=name, grid=(count, nk),
        in_specs=[pl.BlockSpec((tk, bm), lambda i, k: (k, first + i)), pl.BlockSpec((tk, n), lambda i, k: (k, 0))]
        + [ANY] * len(behind),
        out_specs=pl.BlockSpec((bm, n), lambda i, k: (first + i, 0)),
        out_shape=jax.ShapeDtypeStruct((m, n), F32),
        input_output_aliases={} if into is None else {2: 0},
        compiler_params=_params(("parallel", "arbitrary"), VMEM_LIMIT),
    )(a, b, *behind)


def _rope_tables(t):
    inv_freq = ROPE_THETA ** (-jnp.arange(0, ROT_DIM, 2, dtype=F32) / ROT_DIM)
    ang = inv_freq[:, None] * jnp.arange(t, dtype=F32)[None, :]
    compact = jnp.stack([jnp.cos(ang), jnp.sin(ang)])
    tr = min(1024, t)

    def body(c_ref, o_ref):
        for k in range(2):
            o_ref[k] = jnp.tile(c_ref[k], (128 // inv_freq.shape[0], 1)).T

    return pl.pallas_call(
        body, name="rope_tables", grid=(t // tr,),
        in_specs=[pl.BlockSpec((2, inv_freq.shape[0], tr), lambda i: (0, 0, i))],
        out_specs=pl.BlockSpec((2, tr, 128), lambda i: (0, i, 0)),
        out_shape=jax.ShapeDtypeStruct((2, t, 128), F32), compiler_params=_params(("parallel",)),
    )(compact)


def _rope_multipliers(cos_sin):
    half = ROT_DIM // 2
    cos, sin = cos_sin[0], cos_sin[1]
    d = lax.broadcasted_iota(jnp.int32, cos.shape, 1) & (HEAD_DIM - 1)
    mult = jnp.where(d < ROT_DIM, cos, 1.0)
    from_lo = jnp.where((d >= half) & (d < ROT_DIM), sin, 0.0)
    from_hi = jnp.where(d < half, -sin, 0.0)
    return mult, from_lo, from_hi


def _tile_lanes(tab, width):
    return jnp.tile(tab, (1, width // tab.shape[1]))


def _rope(v, tab):
    w = v.shape[1]
    half_rot = ROT_DIM // 2
    return (v * _tile_lanes(tab[0], w)
            + pltpu.roll(v, half_rot, axis=1) * _tile_lanes(tab[1], w)
            + pltpu.roll(v, w - half_rot, axis=1) * _tile_lanes(tab[2], w))


def _rope_bwd(dv, tab):
    w = dv.shape[1]
    half_rot = ROT_DIM // 2
    return (dv * _tile_lanes(tab[0], w)
            + pltpu.roll(dv * _tile_lanes(tab[1], w), w - half_rot, axis=1)
            + pltpu.roll(dv * _tile_lanes(tab[2], w), half_rot, axis=1))


def _shift_rows(v, prev8_ref, n):
    r = lax.broadcasted_iota(jnp.int32, v.shape, 0)
    rolled = pltpu.roll(v, n, axis=0)
    last = prev8_ref[7:8, :]
    if n == 1:
        return jnp.where(r >= 1, rolled, last)
    return jnp.where(r >= 2, rolled, jnp.where(r == 0, prev8_ref[6:7, :], last))


def _shift_rows_up(v, next8_ref, n):
    rows = v.shape[0]
    r = lax.broadcasted_iota(jnp.int32, v.shape, 0)
    rolled = pltpu.roll(v, rows - n, axis=0)
    first = next8_ref[0:1, :]
    if n == 1:
        return jnp.where(r <= rows - 2, rolled, first)
    return jnp.where(r <= rows - 3, rolled, jnp.where(r == rows - 2, first, next8_ref[1:2, :]))


def _lane_half_mask(shape, half):
    lane = lax.broadcasted_iota(jnp.int32, shape, 1)
    return (lane >= HEAD_DIM) if half else (lane < HEAD_DIM)


def _to_kv_lanes(chunk, head, kv):
    if head % 2 != kv:
        chunk = pltpu.roll(chunk, HEAD_DIM, axis=1)
    return jnp.where(_lane_half_mask(chunk.shape, kv), chunk, 0.0)


def _from_kv_lanes(chunk, head, kv):
    chunk = jnp.where(_lane_half_mask(chunk.shape, kv), chunk, 0.0)
    if head % 2 != kv:
        chunk = pltpu.roll(chunk, HEAD_DIM, axis=1)
    return chunk


def _stack_heads(wide):
    parts = []
    for head in range(N_Q_HEADS):
        chunk = wide[:, (head // 2) * 128:(head // 2 + 1) * 128]
        parts.append(_to_kv_lanes(chunk, head, head // Q_PER_KV))
    return jnp.concatenate(parts, axis=0)


def _window_mask(has_prev):
    shape = (N_Q_HEADS * BLOCK, 2 * BLOCK)
    qi = lax.broadcasted_iota(jnp.int32, shape, 0) & (BLOCK - 1)
    kj = lax.broadcasted_iota(jnp.int32, shape, 1)
    first_key = BLOCK - has_prev * BLOCK
    in_prev = (kj < BLOCK) & (kj > qi) & (kj >= first_key)
    in_own = (kj >= BLOCK) & ((kj - BLOCK) <= qi)
    return in_prev | in_own


def _sink_column(sink_ref):
    row = lax.broadcasted_iota(jnp.int32, (N_Q_HEADS * BLOCK, 1), 0)
    col = jnp.full((N_Q_HEADS * BLOCK, 1), sink_ref[0, 0], F32)
    for head in range(1, N_Q_HEADS):
        col = jnp.where(row >= head * BLOCK, sink_ref[0, head], col)
    return col


def _softmax_with_sink(q4, k2, mask, sink):
    s = lax.dot_general(q4, k2, NT_DIMS, preferred_element_type=F32) * SM_SCALE
    s = jnp.where(mask, s, MASK_VALUE)
    m = jnp.maximum(jnp.max(s, axis=-1, keepdims=True), sink)
    p = jnp.exp(s - m)
    e_sink = jnp.exp(sink - m)
    inv_den = 1.0 / (jnp.sum(p, axis=-1, keepdims=True) + e_sink)
    return p * inv_den, e_sink * inv_den


def _conv_terms(zf, prev8_ref, w_ref):
    b_gate, c_gate, u = zf[:, 0:CONV_W], zf[:, CONV_W:2 * CONV_W], zf[:, 2 * CONV_W:3 * CONV_W]
    vc = c_gate * u
    vm1 = _shift_rows(vc, prev8_ref, 1)
    vm2 = _shift_rows(vc, prev8_ref, 2)
    conv = w_ref[0:1, :] * vm2 + w_ref[1:2, :] * vm1 + w_ref[2:3, :] * vc
    return b_gate, c_gate, u, vc, vm1, vm2, conv


def _mixer_fwd(x, gain, win_t, wout, conv_w, sinks, rope, *, name, after=(), tq=512):
    t = x.shape[0]
    tq = min(tq, t)
    nblk = tq // BLOCK

    def body(x_ref, g_ref, win_hbm, wout_hbm, cw_ref, sink_ref, rope_ref,
             xo_ref, h_ref, z_ref, y_ref, kprev_ref, vprev_ref, cprev_ref, win_ref, wout_ref, sems):
        i = pl.program_id(0)
        _load_resident(win_hbm, win_ref, sems.at[0])
        _load_resident(wout_hbm, wout_ref, sems.at[1])

        @pl.when(i == 0)
        def _():
            kprev_ref[...] = jnp.zeros_like(kprev_ref)
            vprev_ref[...] = jnp.zeros_like(vprev_ref)
            cprev_ref[...] = jnp.zeros_like(cprev_ref)

        xf = x_ref[...]
        xhat, _ = _rms_stats(xf)
        h = (xhat * g_ref[...]).astype(BF16)
        h_ref[...] = h
        zb = lax.dot_general(h, win_ref[...], NT_DIMS, preferred_element_type=F32).astype(BF16)
        z_ref[...] = zb
        zf = zb.astype(F32)

        b_gate, _, _, vc, _, _, conv = _conv_terms(zf, cprev_ref, cw_ref)
        y_conv = b_gate * conv
        cprev_ref[...] = vc[tq - 8:tq, :]

        tab = _rope_multipliers(rope_ref[...])
        qr = _rope(zf[:, Q_OFF:K_OFF], tab)
        kr = _rope(zf[:, K_OFF:V_OFF], tab).astype(BF16)
        vb = zb[:, V_OFF:Z_W]

        y_attn = []
        for j in range(nblk):
            rows = slice(j * BLOCK, (j + 1) * BLOCK)
            prev = slice((j - 1) * BLOCK, j * BLOCK)
            k2 = jnp.concatenate([kprev_ref[...] if j == 0 else kr[prev], kr[rows]], axis=0)
            v2 = jnp.concatenate([vprev_ref[...] if j == 0 else vb[prev], vb[rows]], axis=0)
            mask = _window_mask(jnp.minimum(i, 1) if j == 0 else 1)
            q8 = _stack_heads(qr[rows]).astype(BF16)
            probs, _ = _softmax_with_sink(q8, k2, mask, _sink_column(sink_ref))
            o8 = jnp.dot(probs.astype(BF16), v2, preferred_element_type=F32)
            chunks = [jnp.zeros((BLOCK, 128), F32) for _ in range(ATTN_W // 128)]
            for head in range(N_Q_HEADS):
                chunks[head // 2] += _from_kv_lanes(o8[head * BLOCK:(head + 1) * BLOCK], head, head // Q_PER_KV)
            y_attn.append(jnp.concatenate(chunks, axis=1))
        kprev_ref[...] = kr[tq - BLOCK:tq]
        vprev_ref[...] = vb[tq - BLOCK:tq]
        y = jnp.concatenate([y_conv, jnp.concatenate(y_attn, axis=0)], axis=1).astype(BF16)
        y_ref[...] = y
        xo_ref[...] = xf + jnp.dot(y, wout_ref[...], preferred_element_type=F32)

    row = pl.BlockSpec((tq, D_MODEL), lambda i: (i, 0))
    full = lambda shape: pl.BlockSpec(shape, lambda i: (0,) * len(shape))
    return pl.pallas_call(
        _behind(body, 7, after), name=name, grid=(t // tq,),
        in_specs=[row, full((1, D_MODEL)), ANY, ANY, full((3, CONV_W)),
                  pl.BlockSpec(memory_space=pltpu.SMEM), pl.BlockSpec((2, tq, 128), lambda i: (0, i, 0))]
        + [ANY] * len(after),
        out_specs=[row, row, pl.BlockSpec((tq, Z_W), lambda i: (i, 0)), row],
        out_shape=[jax.ShapeDtypeStruct((t, D_MODEL), F32), jax.ShapeDtypeStruct((t, D_MODEL), BF16),
                   jax.ShapeDtypeStruct((t, Z_W), BF16), jax.ShapeDtypeStruct((t, D_MODEL), BF16)],
        scratch_shapes=[pltpu.VMEM((BLOCK, KV_W), BF16), pltpu.VMEM((BLOCK, KV_W), BF16),
                        pltpu.VMEM((8, CONV_W), F32), pltpu.VMEM((Z_W, D_MODEL), BF16),
                        pltpu.VMEM((D_MODEL, D_MODEL), BF16), pltpu.SemaphoreType.DMA((2,))],
        compiler_params=_params(("arbitrary",), VMEM_LIMIT),
    )(x, gain, win_t, wout, conv_w, sinks, rope, *after)


def _mixer_bwd(dxo, x, gain, y, z, win_t, wout, conv_w, sinks, rope, *, name, after=(), tq=256):
    t = x.shape[0]
    tq = min(tq, t)
    nt, nblk = t // tq, tq // BLOCK

    def body(dxo_ref, x_ref, g_ref, y_ref, z_ref, zp_ref, win_hbm, wout_hbm, cw_ref, sink_ref, rope_ref, ropep_ref,
             dxi_ref, dz_ref, gb_ref, dcw_ref, dsink_ref, dg_ref, dk_ref, dv_ref, dcn_ref, pvc_ref,
             win_ref, wout_ref, sems):
        i = pl.program_id(0)
        tile = nt - 1 - i
        _load_resident(win_hbm, win_ref, sems.at[0])
        _load_resident(wout_hbm, wout_ref, sems.at[1])

        @pl.when(i == 0)
        def _():
            dk_ref[...] = jnp.zeros_like(dk_ref)
            dv_ref[...] = jnp.zeros_like(dv_ref)
            dcn_ref[...] = jnp.zeros_like(dcn_ref)
            dcw_ref[...] = jnp.zeros_like(dcw_ref)
            dsink_ref[...] = jnp.zeros_like(dsink_ref)
            dg_ref[...] = jnp.zeros_like(dg_ref)

        has_prev = jnp.minimum(tile, 1)
        go = dxo_ref[...]
        gb = go.astype(BF16)
        gb_ref[...] = gb
        dy = lax.dot_general(gb, wout_ref[...], NT_DIMS, preferred_element_type=F32)
        dy_conv, dy_attn = dy[:, 0:CONV_W], dy[:, CONV_W:D_MODEL]
        zb, zpb = z_ref[...], zp_ref[...]
        zf = zb.astype(F32)
        zpf = zpb.astype(F32) * has_prev.astype(F32)

        pvc_ref[...] = (zpf[:, CONV_W:2 * CONV_W] * zpf[:, 2 * CONV_W:3 * CONV_W])[BLOCK - 8:BLOCK, :]
        b_gate, c_gate, u, vc, vm1, vm2, conv = _conv_terms(zf, pvc_ref, cw_ref)
        d_bgate = dy_conv * conv
        dc = dy_conv * b_gate
        tap = lax.broadcasted_iota(jnp.int32, (8, CONV_W), 0)
        dcw_ref[...] += jnp.where(tap == 0, jnp.sum(dc * vm2, axis=0, keepdims=True),
                                  jnp.where(tap == 1, jnp.sum(dc * vm1, axis=0, keepdims=True),
                                            jnp.where(tap == 2, jnp.sum(dc * vc, axis=0, keepdims=True), 0.0)))
        dvc = (cw_ref[2:3, :] * dc + cw_ref[1:2, :] * _shift_rows_up(dc, dcn_ref, 1)
               + cw_ref[0:1, :] * _shift_rows_up(dc, dcn_ref, 2))
        dcn_ref[...] = dc[0:8, :]
        d_cgate = dvc * u
        d_u = dvc * c_gate

        tab, tabp = _rope_multipliers(rope_ref[...]), _rope_multipliers(ropep_ref[...])
        qr = _rope(zf[:, Q_OFF:K_OFF], tab)
        kr = _rope(zf[:, K_OFF:V_OFF], tab).astype(BF16)
        kpr = _rope(zpf[:, K_OFF:V_OFF], tabp).astype(BF16)
        vb, vpb = zb[:, V_OFF:Z_W], zpb[:, V_OFF:Z_W]
        out = y_ref[:, CONV_W:D_MODEL].astype(F32)
        do_out = dy_attn * out
        lane = lax.broadcasted_iota(jnp.int32, (1, 128), 1)
        dsink = jnp.zeros((1, 128), F32)
        dk_next, dv_next = dk_ref[...], dv_ref[...]
        dq_rows, dk_rows, dv_rows = [None] * nblk, [None] * nblk, [None] * nblk
        for j in reversed(range(nblk)):
            rows = slice(j * BLOCK, (j + 1) * BLOCK)
            prev = slice((j - 1) * BLOCK, j * BLOCK)
            k2 = jnp.concatenate([kpr if j == 0 else kr[prev], kr[rows]], axis=0)
            v2 = jnp.concatenate([vpb if j == 0 else vb[prev], vb[rows]], axis=0)
            mask = _window_mask(has_prev if j == 0 else 1)
            q8 = _stack_heads(qr[rows]).astype(BF16)
            do8 = _stack_heads(dy_attn[rows]).astype(BF16)
            delta = jnp.sum(_stack_heads(do_out[rows]), axis=-1, keepdims=True)
            probs, p_sink = _softmax_with_sink(q8, k2, mask, _sink_column(sink_ref))
            dp = lax.dot_general(do8, v2, NT_DIMS, preferred_element_type=F32)
            ds = (probs * (dp - delta) * SM_SCALE).astype(BF16)
            dq8 = jnp.dot(ds, k2, preferred_element_type=F32)
            dk2 = lax.dot_general(ds, q8, TN_DIMS, preferred_element_type=F32)
            dv2 = lax.dot_general(probs.astype(BF16), do8, TN_DIMS, preferred_element_type=F32)
            sink_terms = p_sink * delta
            dq_chunks = [jnp.zeros((BLOCK, 128), F32) for _ in range(ATTN_W // 128)]
            for head in range(N_Q_HEADS):
                grp = slice(head * BLOCK, (head + 1) * BLOCK)
                dq_chunks[head // 2] += _from_kv_lanes(dq8[grp], head, head // Q_PER_KV)
                dsink = dsink - jnp.where(lane == head, jnp.sum(sink_terms[grp], axis=0, keepdims=True), 0.0)
            dq_rows[j] = jnp.concatenate(dq_chunks, axis=1)
            dk_rows[j] = dk2[BLOCK:] + dk_next
            dv_rows[j] = dv2[BLOCK:] + dv_next
            dk_next, dv_next = dk2[:BLOCK], dv2[:BLOCK]
        dk_ref[...] = dk_next
        dv_ref[...] = dv_next
        dsink_ref[...] += dsink
        dq = _rope_bwd(jnp.concatenate(dq_rows, axis=0), tab)
        dk = _rope_bwd(jnp.concatenate(dk_rows, axis=0), tab)
        dv = jnp.concatenate(dv_rows, axis=0)

        dzb = jnp.concatenate([d_bgate, d_cgate, d_u, dq, dk, dv], axis=1).astype(BF16)
        dz_ref[...] = dzb
        dh = jnp.dot(dzb, win_ref[...], preferred_element_type=F32)
        xhat, inv = _rms_stats(x_ref[...])
        dx, dgain = _rms_bwd(dh, xhat, inv, g_ref[...])
        dxi_ref[...] = go + dx
        dg_ref[...] += dgain

    rev = lambda i: (nt - 1 - i, 0)
    block_before = lambda i: jnp.maximum((nt - 1 - i) * nblk - 1, 0)
    row = pl.BlockSpec((tq, D_MODEL), rev)
    full = lambda shape: pl.BlockSpec(shape, lambda i: (0,) * len(shape))
    return pl.pallas_call(
        _behind(body, 12, after), name=name, grid=(nt,),
        in_specs=[row, row, full((1, D_MODEL)), row,
                  pl.BlockSpec((tq, Z_W), rev), pl.BlockSpec((BLOCK, Z_W), lambda i: (block_before(i), 0)),
                  ANY, ANY, full((3, CONV_W)),
                  pl.BlockSpec(memory_space=pltpu.SMEM),
                  pl.BlockSpec((2, tq, 128), lambda i: (0, nt - 1 - i, 0)),
                  pl.BlockSpec((2, BLOCK, 128), lambda i: (0, block_before(i), 0))] + [ANY] * len(after),
        out_specs=[row, pl.BlockSpec((tq, Z_W), rev), row, full((8, CONV_W)), full((1, 128)), full((1, D_MODEL))],
        out_shape=[jax.ShapeDtypeStruct((t, D_MODEL), F32), jax.ShapeDtypeStruct((t, Z_W), BF16),
                   jax.ShapeDtypeStruct((t, D_MODEL), BF16), jax.ShapeDtypeStruct((8, CONV_W), F32),
                   jax.ShapeDtypeStruct((1, 128), F32), jax.ShapeDtypeStruct((1, D_MODEL), F32)],
        scratch_shapes=[pltpu.VMEM((BLOCK, KV_W), F32), pltpu.VMEM((BLOCK, KV_W), F32), pltpu.VMEM((8, CONV_W), F32),
                        pltpu.VMEM((8, CONV_W), F32), pltpu.VMEM((Z_W, D_MODEL), BF16),
                        pltpu.VMEM((D_MODEL, D_MODEL), BF16), pltpu.SemaphoreType.DMA((2,))],
        compiler_params=_params(("arbitrary",), VMEM_LIMIT),
    )(dxo, x, gain, y, z, z, win_t, wout, conv_w, sinks, rope, rope, *after)


def _place():
    x, y, c = lax.axis_index("x"), lax.axis_index("y"), lax.axis_index("c")
    other_chips = [(1 - x, y), (x, 1 - y), (1 - x, 1 - y)]
    return x, y, c, other_chips


def _all_gather_rows(shards, place=(), *, name):
    n, p = len(shards), len(place)

    def body(*refs):
        srcs, place_srcs = refs[:n], refs[n:n + p]
        outs, place_outs = refs[n + p:2 * n + p], refs[2 * n + p:2 * (n + p)]
        send_sems, recv_sems, local_sems = refs[2 * (n + p):]
        x, y, c, chips = _place()
        me, sibling = (x, y, c), (x, y, 1 - c)

        def rows(t, px, py, pc):
            r = srcs[t].shape[-2]
            start = pl.multiple_of((4 * px + 2 * py + pc) * r, 16 if r % 16 == 0 else 8)
            if len(srcs[t].shape) == 3:
                return outs[t].at[:, pl.ds(start, r), :]
            return outs[t].at[pl.ds(start, r), :]

        def copy(t, k, block, to, own=False):
            return pltpu.make_async_remote_copy(
                src_ref=srcs[t] if own else rows(t, *block), dst_ref=rows(t, *block),
                send_sem=send_sems.at[t, k], recv_sem=recv_sems.at[t, k], device_id=to, device_id_type=MESH)

        mine = [pltpu.make_async_copy(srcs[t], rows(t, *me), local_sems.at[t]) for t in range(n)]
        mine += [pltpu.make_async_copy(place_srcs[q],
                                       _block_rows(place_outs[q], place_srcs[q].shape[-2], 4 * x + 2 * y + c),
                                       local_sems.at[n + q]) for q in range(p)]
        for q in range(p):
            mine[n + q].start()
        first = []
        for t in range(n):
            mine[t].start()
            first.append(copy(t, 0, me, sibling, own=True))
            first += [copy(t, 1 + j, me, (*chip, c), own=True) for j, chip in enumerate(chips)]
        for cp in first:
            cp.start()
        passed = []
        for j, chip in enumerate(chips):
            for t in range(n):
                copy(t, 1 + j, (*chip, c), me).wait_recv()
                fwd = copy(t, 4 + j, (*chip, c), sibling)
                fwd.start()
                passed.append(fwd)
        for t in range(n):
            copy(t, 0, sibling, me).wait_recv()
            for j, chip in enumerate(chips):
                copy(t, 4 + j, (*chip, 1 - c), me).wait_recv()
        for cp in first + passed:
            cp.wait_send()
        for cp in mine:
            cp.wait()

    out_shape = [jax.ShapeDtypeStruct(s.shape[:-2] + (N_DEV * s.shape[-2], s.shape[-1]), s.dtype)
                 for s in list(shards) + list(place)]
    res = pl.pallas_call(
        body, name=name, in_specs=[ANY] * (n + p), out_specs=[ANY] * (n + p), out_shape=out_shape,
        scratch_shapes=[pltpu.SemaphoreType.DMA((n, 7)), pltpu.SemaphoreType.DMA((n, 7)),
                        pltpu.SemaphoreType.DMA((n + p,))],
    )(*shards, *place)
    return res[:n], res[n:]


def _split_start(bufs, n_copies, plan, *, name, after=()):
    n = len(bufs)

    def body(*refs):
        token = refs[-1]
        for cp in plan(refs[:n], refs[n], refs[n + 1]):
            cp.start()
        token[...] = jnp.zeros_like(token)

    res = pl.pallas_call(
        _behind(body, n, after), name=name, in_specs=[HBM_SPEC] * n + [ANY] * len(after),
        out_specs=(SEM_SPEC, SEM_SPEC, *[HBM_SPEC] * n, pl.BlockSpec(memory_space=pltpu.VMEM)),
        out_shape=(pltpu.SemaphoreType.DMA((n_copies,)), pltpu.SemaphoreType.DMA((n_copies,)),
                   *[pltpu.HBM(b.shape, b.dtype) for b in bufs], jax.ShapeDtypeStruct((8, 128), F32)),
        input_output_aliases={i: 2 + i for i in range(n)},
        compiler_params=pltpu.CompilerParams(has_side_effects=DATAFLOW),
    )(*[pltpu.with_memory_space_constraint(b, pltpu.HBM) for b in bufs], *after)
    return res[0], res[1], list(res[2:2 + n]), res[-1]


def _split_wait(send_sems, recv_sems, bufs, after, plan, *, name):
    n = len(bufs)

    def body(*refs):
        for cp in plan(refs[:n], refs[n], refs[n + 1]):
            cp.wait_send()
            cp.wait_recv()

    return list(pl.pallas_call(
        body, name=name, in_specs=[HBM_SPEC] * n + [SEM_SPEC, SEM_SPEC, ANY], out_specs=[HBM_SPEC] * n,
        out_shape=tuple(pltpu.HBM(b.shape, b.dtype) for b in bufs),
        input_output_aliases={i: i for i in range(n)},
        compiler_params=pltpu.CompilerParams(has_side_effects=DATAFLOW),
    )(*bufs, send_sems, recv_sems, after))


def _sibling_plan(n):
    def plan(bufs, send_sems, recv_sems):
        x, y, c, _ = _place()
        return [pltpu.make_async_remote_copy(
            src_ref=bufs[t].at[:, 1 - c], dst_ref=bufs[n + t], send_sem=send_sems.at[t], recv_sem=recv_sems.at[t],
            device_id=(x, y, 1 - c), device_id_type=MESH) for t in range(n)]
    return plan


def _block_rows(ref, r, blk):
    start = pl.multiple_of(blk * r, 16 if r % 16 == 0 else 8)
    return ref.at[(slice(None),) * (len(ref.shape) - 2) + (pl.ds(start, r), slice(None))]


def _remote(src, dst, send_sems, recv_sems, k, peer):
    return pltpu.make_async_remote_copy(src_ref=src, dst_ref=dst, send_sem=send_sems.at[k], recv_sem=recv_sems.at[k],
                                        device_id=peer, device_id_type=MESH)


def _gather_send_plan(n, first=0):
    def plan(bufs, send_sems, recv_sems):
        x, y, c, chips = _place()
        peers = [(x, y, 1 - c)] + [(px, py, c) for px, py in chips]
        copies = []
        for t in range(n):
            dst = _block_rows(bufs[n + t], bufs[t].shape[-2], 4 * x + 2 * y + c)
            copies += [_remote(bufs[t], dst, send_sems, recv_sems, 4 * (first + t) + k, peer)
                       for k, peer in enumerate(peers)]
        return copies
    return plan


def _gather_forward_plan(rows):
    def plan(bufs, send_sems, recv_sems):
        x, y, c, chips = _place()
        copies = []
        for t, r in enumerate(rows):
            for j, (px, py) in enumerate(chips):
                blk = _block_rows(bufs[t], r, 4 * px + 2 * py + c)
                copies.append(_remote(blk, blk, send_sems, recv_sems, 3 * t + j, (x, y, 1 - c)))
        return copies
    return plan


def _chips_plan(n, with_small):
    def plan(bufs, send_sems, recv_sems):
        x, y, c, chips = _place()
        copies = []
        for t in range(n):
            for j, (px, py) in enumerate(chips):
                copies.append(_remote(bufs[t].at[2 * px + py], bufs[n + t].at[j], send_sems, recv_sems, 3 * t + j,
                                      (px, py, c)))
        if with_small:
            mine = _block_rows(bufs[2 * n], 8, 4 * x + 2 * y + c)
            flips = [(fx, fy, fc) for fx in range(2) for fy in range(2) for fc in range(2)][1:]
            for k, (fx, fy, fc) in enumerate(flips):
                peer = (x + fx - 2 * x * fx, y + fy - 2 * y * fy, c + fc - 2 * c * fc)
                copies.append(_remote(mine, mine, send_sems, recv_sems, 3 * n + k, peer))
        return copies
    return plan


def _place_own(fulls, shards, index, *, name):
    n = len(fulls)

    def body(index_ref, *refs):
        for t in range(n):
            refs[2 * n + t][...] = refs[n + t][...]

    def block_of(shard):
        lead = len(shard.shape) - 2
        return pl.BlockSpec(shard.shape, lambda i, index_ref: (0,) * lead + (index_ref[0], 0))

    def whole(shard):
        return pl.BlockSpec(shard.shape, lambda i, index_ref: (0,) * len(shard.shape))

    return list(pl.pallas_call(
        body, name=name,
        grid_spec=pltpu.PrefetchScalarGridSpec(
            num_scalar_prefetch=1, grid=(1,),
            in_specs=[ANY] * n + [whole(s) for s in shards], out_specs=[block_of(s) for s in shards]),
        out_shape=[jax.ShapeDtypeStruct(f.shape, f.dtype) for f in fulls],
        input_output_aliases={1 + t: t for t in range(n)},
        compiler_params=_params(("arbitrary",)),
    )(index, *fulls, *shards))


N_STEPS_SMALL = 2


def _add_sibling(grads, recvs, place, *, name):
    n = len(grads)

    def body(place_ref, *refs):
        chip = place_ref[1]
        for t in range(n):
            g_ref, r_ref, own_ref, ob_ref = refs[2 * t], refs[2 * t + 1], refs[2 * n + 2 * t], refs[2 * n + 2 * t + 1]
            own = jnp.zeros(own_ref.shape, F32)
            for m in range(4):
                p = g_ref[m, 0] + r_ref[m]
                ob_ref[m] = p.astype(BF16)
                own = jnp.where(chip == m, p, own)
            own_ref[...] = own

    in_specs, out_specs, out_shape = [], [], []
    for g, r in zip(grads, recvs):
        tr = g.shape[2] // N_STEPS_SMALL
        blocks = pl.BlockSpec((4, tr, D_MODEL), lambda i, place_ref: (0, i, 0))
        in_specs += [pl.BlockSpec((4, 1, tr, D_MODEL), lambda i, place_ref: (0, place_ref[0], i, 0)), blocks]
        out_specs += [pl.BlockSpec((tr, D_MODEL), lambda i, place_ref: (i, 0)), blocks]
        out_shape += [jax.ShapeDtypeStruct(r.shape[1:], F32), jax.ShapeDtypeStruct(r.shape, BF16)]
    res = pl.pallas_call(
        body, name=name,
        grid_spec=pltpu.PrefetchScalarGridSpec(num_scalar_prefetch=1, grid=(N_STEPS_SMALL,), in_specs=in_specs,
                                               out_specs=out_specs),
        out_shape=out_shape, compiler_params=_params(("arbitrary",), VMEM_LIMIT),
    )(place, *[a for pair in zip(grads, recvs) for a in pair])
    return [(res[2 * t], res[2 * t + 1]) for t in range(n)]


def _reduce_adamw(parts, *, name, after=()):
    n = len(parts)

    def body(*refs):
        for t in range(n):
            p_ref, r_ref, w_ref, m_ref, v_ref = refs[5 * t:5 * t + 5]
            g_ref, d_ref, mo_ref, vo_ref = refs[5 * n + 4 * t:5 * n + 4 * t + 4]
            g = p_ref[...] + r_ref[0].astype(F32) + r_ref[1].astype(F32) + r_ref[2].astype(F32)
            g_ref[...] = g
            d_ref[...], mo_ref[...], vo_ref[...] = _adamw_math(w_ref[...], g, m_ref[...], v_ref[...])
        refs[-1][...] = jnp.zeros_like(refs[-1])

    in_specs, out_specs, out_shape = [], [], []
    for own, _, _, _, _ in parts:
        rows = own.shape[0]
        tr = rows // N_STEPS_SMALL
        spec = pl.BlockSpec((tr, D_MODEL), lambda i: (i, 0))
        in_specs += [spec, pl.BlockSpec((3, tr, D_MODEL), lambda i: (0, i, 0)), spec, spec, spec]
        out_specs += [spec] * 4
        out_shape += [jax.ShapeDtypeStruct((rows, D_MODEL), F32)] * 4
    res = pl.pallas_call(
        _behind(body, 5 * n, after), name=name, grid=(N_STEPS_SMALL,),
        in_specs=in_specs + [ANY] * len(after),
        out_specs=out_specs + [pl.BlockSpec((8, 128), lambda i: (0, 0))],
        out_shape=out_shape + [jax.ShapeDtypeStruct((8, 128), F32)],
        compiler_params=_params(("arbitrary",), VMEM_LIMIT),
    )(*[a for part in parts for a in part], *after)
    return [tuple(res[4 * t:4 * t + 4]) for t in range(n)], res[-1]


def _adamw_math(w, g, m, v):
    m = ADAM_B1 * m + (1.0 - ADAM_B1) * g
    v = ADAM_B2 * v + (1.0 - ADAM_B2) * (g * g)
    m_hat = m / (1.0 - ADAM_B1 ** ADAM_STEP)
    v_hat = v / (1.0 - ADAM_B2 ** ADAM_STEP)
    delta = -ADAM_LR * (m_hat / (jnp.sqrt(v_hat) + ADAM_EPS) + ADAM_WD * w)
    return delta, m, v


SMALL_NAMES = ["ffn1_norm", "mix_norm", "ffn2_norm", "final_norm", "conv_w", "attn_sinks"]


def _update_small(given, moments_m, moments_v, small_all, my_index, *, name):
    conv_cols = given["conv_w"].shape[2]
    per_block = 128 // conv_cols

    def two_d(nm, a):
        return a.reshape(1, D_MODEL) if nm == "final_norm" else a

    operands = [two_d(nm, src[nm]) for nm in SMALL_NAMES for src in (given, moments_m, moments_v)]
    n = len(SMALL_NAMES)

    def body(index_ref, all_ref, conv_ref, *refs):
        ins, outs = refs[:3 * n], refs[3 * n:]
        total, conv_total = all_ref[0], conv_ref[0]
        for k in range(1, N_DEV):
            total, conv_total = total + all_ref[k], conv_total + conv_ref[k]
        which = index_ref[0] % per_block
        conv_g = conv_total[4:7, :conv_cols]
        for j in range(1, per_block):
            conv_g = jnp.where(which == j, conv_total[4:7, j * conv_cols:(j + 1) * conv_cols], conv_g)
        grads = [total[0:1], total[1:2], total[2:3], total[3:4], conv_g[None], total[7:8, :N_Q_HEADS]]
        for t, g in enumerate(grads):
            w_ref, m_ref, v_ref = ins[3 * t:3 * t + 3]
            g_ref, d_ref, mo_ref, vo_ref = outs[4 * t:4 * t + 4]
            g_ref[...] = g
            d_ref[...], mo_ref[...], vo_ref[...] = _adamw_math(w_ref[...], g, m_ref[...], v_ref[...])
        outs[-1][...] = total[7:8, LOSS_LANE:LOSS_LANE + 1]

    def whole(shape):
        return pl.BlockSpec(shape, lambda i, index_ref: (0,) * len(shape))

    shapes = [a.shape for a in operands[::3] for _ in range(4)] + [(1, 1)]
    res = pl.pallas_call(
        body, name=name,
        grid_spec=pltpu.PrefetchScalarGridSpec(
            num_scalar_prefetch=1, grid=(1,),
            in_specs=[whole(small_all.shape),
                      pl.BlockSpec((N_DEV, 8, 128), lambda i, index_ref: (0, 0, index_ref[0] // per_block))]
            + [whole(a.shape) for a in operands],
            out_specs=[whole(s) for s in shapes]),
        out_shape=[jax.ShapeDtypeStruct(s, F32) for s in shapes],
        compiler_params=_params(("arbitrary",)),
    )(my_index.astype(jnp.int32).reshape(1), small_all, small_all, *operands)
    results = {nm: tuple(a.reshape(given[nm].shape) for a in res[4 * t:4 * t + 4]) for t, nm in enumerate(SMALL_NAMES)}
    return results, res[-1]


def kernel(x, ffn1_norm, ffn1_w_gate, ffn1_w_up, ffn1_w_down, mix_norm, w_in, conv_w, attn_sinks, w_out, ffn2_norm, ffn2_w_gate, ffn2_w_up, ffn2_w_down, final_norm, loss_target, m_ffn1_norm, m_ffn1_w_gate, m_ffn1_w_up, m_ffn1_w_down, m_mix_norm, m_w_in, m_conv_w, m_attn_sinks, m_w_out, m_ffn2_norm, m_ffn2_w_gate, m_ffn2_w_up, m_ffn2_w_down, m_final_norm, v_ffn1_norm, v_ffn1_w_gate, v_ffn1_w_up, v_ffn1_w_down, v_mix_norm, v_w_in, v_conv_w, v_attn_sinks, v_w_out, v_ffn2_norm, v_ffn2_w_gate, v_ffn2_w_up, v_ffn2_w_down, v_final_norm):
    ix, iy, ic = lax.axis_index("x"), lax.axis_index("y"), lax.axis_index("c")
    my_index = 4 * ix + 2 * iy + ic
    place = jnp.stack([ic, 2 * ix + iy]).astype(jnp.int32)

    given = dict(ffn1_norm=ffn1_norm, ffn1_w_gate=ffn1_w_gate, ffn1_w_up=ffn1_w_up, ffn1_w_down=ffn1_w_down,
                 mix_norm=mix_norm, w_in=w_in, conv_w=conv_w, attn_sinks=attn_sinks, w_out=w_out, ffn2_norm=ffn2_norm,
                 ffn2_w_gate=ffn2_w_gate, ffn2_w_up=ffn2_w_up, ffn2_w_down=ffn2_w_down, final_norm=final_norm)
    moments_m = dict(ffn1_norm=m_ffn1_norm, ffn1_w_gate=m_ffn1_w_gate, ffn1_w_up=m_ffn1_w_up, ffn1_w_down=m_ffn1_w_down,
                     mix_norm=m_mix_norm, w_in=m_w_in, conv_w=m_conv_w, attn_sinks=m_attn_sinks, w_out=m_w_out,
                     ffn2_norm=m_ffn2_norm, ffn2_w_gate=m_ffn2_w_gate, ffn2_w_up=m_ffn2_w_up, ffn2_w_down=m_ffn2_w_down,
                     final_norm=m_final_norm)
    moments_v = dict(ffn1_norm=v_ffn1_norm, ffn1_w_gate=v_ffn1_w_gate, ffn1_w_up=v_ffn1_w_up, ffn1_w_down=v_ffn1_w_down,
                     mix_norm=v_mix_norm, w_in=v_w_in, conv_w=v_conv_w, attn_sinks=v_attn_sinks, w_out=v_w_out,
                     ffn2_norm=v_ffn2_norm, ffn2_w_gate=v_ffn2_w_gate, ffn2_w_up=v_ffn2_w_up, ffn2_w_down=v_ffn2_w_down,
                     final_norm=v_final_norm)

    xs = x[0]
    target = loss_target[0]
    final_gain = final_norm.reshape(1, D_MODEL)

    def ffn_shards(wg, wu, wd):
        return jnp.stack([wg[0].T, wu[0].T]).astype(BF16), wd[0].astype(BF16)

    conv_cols = conv_w.shape[2]
    conv_shard = jnp.pad(conv_w[0], ((0, 5), (0, 128 - conv_cols)))
    gate_up1, down1 = ffn_shards(ffn1_w_gate, ffn1_w_up, ffn1_w_down)
    rest_shards = [down1, w_in[0].T.astype(BF16), w_out[0].astype(BF16), conv_shard,
                   *ffn_shards(ffn2_w_gate, ffn2_w_up, ffn2_w_down)]
    rest_rows = [s.shape[-2] for s in rest_shards]
    n_rest, n_early = len(rest_shards), 4
    (w1_gu,), _ = _all_gather_rows([gate_up1], name="gather_ffn1")

    fulls = [lax.empty(s.shape[:-2] + (N_DEV * s.shape[-2], s.shape[-1]), s.dtype) for s in rest_shards]
    fulls = _place_own(fulls, rest_shards, my_index.astype(jnp.int32).reshape(1), name="place_own_weights")
    ssem, rsem, bufs, token = _split_start(rest_shards + list(fulls), 4 * n_rest, _gather_send_plan(n_rest),
                                           name="gather_rest_start", after=[w1_gu])
    early = bufs[:n_early] + bufs[n_rest:n_rest + n_early]
    late = bufs[n_early:n_rest] + bufs[n_rest + n_early:]
    h1, s1, sa1, sb1 = _ffn_fwd(xs, ffn1_norm, w1_gu, name="ffn1_hidden", after=[token])
    early = _split_wait(ssem, rsem, early, h1, _gather_send_plan(n_early), name="gather_early_wait")
    fwd_early = _gather_forward_plan(rest_rows[:n_early])
    ssem_e, rsem_e, parts, token = _split_start(early[n_early:], 3 * n_early, fwd_early, name="forward_early_start")
    w1_d, win_t, wout, conv_all = _split_wait(ssem_e, rsem_e, parts, token, fwd_early, name="forward_early_wait")
    x1 = _ffn_down(xs, s1, w1_d, name="ffn1_down")
    conv_full = conv_all.reshape(N_DEV, 8, 128)[:, :3, :conv_cols].transpose(1, 0, 2).reshape(3, CONV_W)
    late = _split_wait(ssem, rsem, late, x1, _gather_send_plan(n_rest - n_early, first=n_early),
                       name="gather_late_wait")
    fwd_ffn2 = _gather_forward_plan(rest_rows[n_early:])
    ssem, rsem, parts, token = _split_start(late[n_rest - n_early:], 3 * (n_rest - n_early), fwd_ffn2,
                                            name="forward_ffn2_start")
    rope = _rope_tables(xs.shape[0])
    x2, hm, z, y = _mixer_fwd(x1, mix_norm, win_t, wout, conv_full, attn_sinks, rope, name="mixer_fwd", after=[token])
    w2_gu, w2_d = _split_wait(ssem, rsem, parts, x2, fwd_ffn2, name="forward_ffn2_wait")
    dx3, h2, s2, sa2, sb2, loss_local, d_final = _ffn_fwd(x2, ffn2_norm, w2_gu, w2_d, head=(final_gain, target),
                                                          name="ffn2_fwd")

    def to_sibling_start(grads, tag, after=()):
        views = [g.reshape(4, 2, g.shape[0] // N_DEV, D_MODEL) for g in grads]
        lands = [lax.empty((4,) + v.shape[2:], F32) for v in views]
        plan = _sibling_plan(len(views))
        ssem, rsem, bufs, token = _split_start(views + lands, len(views), plan, name=f"{tag}_sibling_start", after=after)
        return (ssem, rsem, bufs, plan, tag), token

    def to_sibling_finish(handle, after, names):
        ssem, rsem, bufs, plan, tag = handle
        bufs = _split_wait(ssem, rsem, bufs, after, plan, name=f"{tag}_sibling_wait")
        n = len(names)
        return _add_sibling(bufs[:n], bufs[n:], place, name=f"add_sibling_{tag}")

    def to_chips_start(partials, tag, small_all=None, after=()):
        p16 = [p for _, p in partials]
        lands = [lax.empty((3,) + p.shape[1:], BF16) for p in p16]
        extra = [] if small_all is None else [small_all]
        plan = _chips_plan(len(p16), small_all is not None)
        ssem, rsem, bufs, token = _split_start(p16 + lands + extra, 3 * len(p16) + 7 * len(extra), plan,
                                               name=f"{tag}_chips_start", after=after)
        return (ssem, rsem, bufs, plan, tag), token

    def to_chips_finish(handle, partials, after, names):
        ssem, rsem, bufs, plan, tag = handle
        bufs = _split_wait(ssem, rsem, bufs, after, plan, name=f"{tag}_chips_wait")
        n = len(names)
        return [(p32, r) for (p32, _), r in zip(partials, bufs[n:2 * n])], bufs[2 * n:]

    half_ff = D_FF // 2
    names2, namesm = ["ffn2_w_gate", "ffn2_w_up", "ffn2_w_down"], ["w_in", "w_out"]
    transposed = {"ffn1_w_gate", "ffn1_w_up", "w_in", "ffn2_w_gate", "ffn2_w_up"}
    grad, delta, new_m, new_v = {}, {}, {}, {}

    def adam_big(names, parts, tag, after=()):
        def to_rows(nm, a):
            return jnp.swapaxes(a, 1, 2)[0] if nm in transposed else a[0]

        def from_rows(nm, a):
            return jnp.swapaxes(a[None], 1, 2) if nm in transposed else a[None]

        operands = [(p32, recv, to_rows(nm, given[nm]), to_rows(nm, moments_m[nm]), to_rows(nm, moments_v[nm]))
                    for nm, (p32, recv) in zip(names, parts)]
        results, token = _reduce_adamw(operands, name=f"adamw_{tag}", after=after)
        for nm, outs in zip(names, results):
            grad[nm], delta[nm], new_m[nm], new_v[nm] = (from_rows(nm, a) for a in outs)
        return token

    dx2, da2, db2, g2b, d_norm2 = _ffn_dgrad(dx3, x2, ffn2_norm, sa2, sb2, w2_gu, w2_d, name="ffn2_dgrad")
    gw2 = [_tn_matmul(da2, h2, name="ffn2_wgrad_gate", bm=half_ff), _tn_matmul(db2, h2, name="ffn2_wgrad_up", bm=half_ff),
           _tn_matmul(s2, g2b, name="ffn2_wgrad_down", bm=half_ff)]
    sib2, tok = to_sibling_start(gw2, "ffn2")
    dx1, dz, gmb, d_conv, d_sink, d_normm = _mixer_bwd(dx2, x1, mix_norm, y, z, win_t, wout, conv_full, attn_sinks,
                                                       rope, name="mixer_bwd", after=[tok])
    p2 = to_sibling_finish(sib2, dx1, names2)
    chips2, tok = to_chips_start(p2, "ffn2")
    gwm = [_tn_matmul(dz, hm, name="mixer_wgrad_in", bm=Z_W // 3, after=[tok]),
           _tn_matmul(y, gmb, name="mixer_wgrad_out", bm=D_MODEL // 2, after=[tok])]
    sibm, tok = to_sibling_start(gwm, "mixer")
    dx0, da1, db1, g1b, d_norm1 = _ffn_dgrad(dx1, xs, ffn1_norm, sa1, sb1, w1_gu, w1_d, name="ffn1_dgrad", after=[tok])
    r2, _ = to_chips_finish(chips2, p2, dx0, names2)
    pm = to_sibling_finish(sibm, dx0, namesm)
    chipsm, tok = to_chips_start(pm, "mixer")
    gw_gate = _tn_matmul(da1, h1, name="ffn1_wgrad_gate", bm=half_ff, after=[tok])
    sib_gate, tok = to_sibling_start([gw_gate], "ffn1_gate")
    gw_up = _tn_matmul(db1, h1, name="ffn1_wgrad_up", bm=half_ff, after=[tok])
    rm, _ = to_chips_finish(chipsm, pm, gw_up, namesm)
    p_gate = to_sibling_finish(sib_gate, gw_up, ["ffn1_w_gate"])
    chips_gate, tok_a = to_chips_start(p_gate, "ffn1_gate")
    sib_up, tok_b = to_sibling_start([gw_up], "ffn1_up", after=[tok_a])
    gw_down = _tn_matmul(s1, g1b, name="ffn1_wgrad_down_first", bm=half_ff, blocks=(0, 1), after=[tok_a, tok_b])
    p_up = to_sibling_finish(sib_up, gw_down, ["ffn1_w_up"])
    chips_up, tok_a = to_chips_start(p_up, "ffn1_up")
    gw_down = _tn_matmul(s1, g1b, name="ffn1_wgrad_down_second", bm=half_ff, blocks=(1, 1), into=gw_down, after=[tok_a])
    sib_down, tok_b = to_sibling_start([gw_down], "ffn1_down")
    p_down = to_sibling_finish(sib_down, tok_b, ["ffn1_w_down"])
    last_row = (jnp.pad(d_sink, ((0, 0), (0, D_MODEL - 128)))
                + jnp.pad(loss_local, ((0, 0), (LOSS_LANE, D_MODEL - LOSS_LANE - 1))))
    small = jnp.concatenate([
        d_norm1, d_normm, d_norm2, d_final, jnp.pad(d_conv[0:3], ((0, 0), (0, D_MODEL - CONV_W))), last_row], axis=0)
    (small_all,) = _place_own([lax.empty((N_DEV * 8, D_MODEL), F32)], [small], my_index.astype(jnp.int32).reshape(1),
                              name="place_own_small")
    chips_down, tok = to_chips_start(p_down, "ffn1_down", small_all)
    tok = adam_big(names2, r2, "ffn2", after=[tok])
    tok = adam_big(namesm, rm, "mixer", after=[tok])
    r_gate, _ = to_chips_finish(chips_gate, p_gate, tok, ["ffn1_w_gate"])
    tok = adam_big(["ffn1_w_gate"], r_gate, "ffn1_gate")
    r_up, _ = to_chips_finish(chips_up, p_up, tok, ["ffn1_w_up"])
    tok = adam_big(["ffn1_w_up"], r_up, "ffn1_up")
    r_down, (small_all,) = to_chips_finish(chips_down, p_down, tok, ["ffn1_w_down"])
    adam_big(["ffn1_w_down"], r_down, "ffn1_down")
    results, loss = _update_small(given, moments_m, moments_v, small_all.reshape(N_DEV, 8, D_MODEL), my_index,
                                  name="update_small")
    for nm, outs in results.items():
        grad[nm], delta[nm], new_m[nm], new_v[nm] = outs

    order = list(given)
    return (loss.reshape(()), dx0[None], *[grad[n] for n in order], *[delta[n] for n in order],
            *[new_m[n] for n in order], *[new_v[n] for n in order])
```

```python
import functools

import jax
import jax.numpy as jnp
from jax import lax
from jax.experimental import pallas as pl
from jax.experimental.pallas import tpu as pltpu

F32 = jnp.float32
BF16 = jnp.bfloat16
MESH = pl.DeviceIdType.MESH
ANY = pl.BlockSpec(memory_space=pl.ANY)
HBM_SPEC = pl.BlockSpec(memory_space=pltpu.HBM)
SEM_SPEC = pl.BlockSpec(memory_space=pltpu.SEMAPHORE)
DATAFLOW = pltpu.SideEffectType.DATAFLOW_SIDE_EFFECTING

N_DEV = 8
LOSS_LANE = 128
D_MODEL = 1024
D_FF = 2816
CONV_W = 512
ATTN_W = 512
KV_W = 128
HEAD_DIM = 64
N_Q_HEADS = 8
N_KV_HEADS = 2
Q_PER_KV = N_Q_HEADS // N_KV_HEADS
BLOCK = 128
ROT_DIM = 16
ROPE_THETA = 500000.0
Z_W = 3 * CONV_W + ATTN_W + 2 * KV_W
Q_OFF = 3 * CONV_W
K_OFF = Q_OFF + ATTN_W
V_OFF = K_OFF + KV_W
RMS_EPS = 1e-5
MASK_VALUE = -1e30
SM_SCALE = HEAD_DIM ** -0.5
FFN_RES_SCALE = 0.5

ADAM_LR = 0.001
ADAM_B1 = 0.9
ADAM_B2 = 0.999
ADAM_EPS = 1e-08
ADAM_WD = 0.01
ADAM_STEP = 10

NT_DIMS = (((1,), (1,)), ((), ()))
TN_DIMS = (((0,), (0,)), ((), ()))

VMEM_LIMIT = 62 * 1024 * 1024
FF_CHUNK = 256


def _params(sem, vmem=None):
    return pltpu.CompilerParams(dimension_semantics=sem, vmem_limit_bytes=vmem)


def _behind(body, n_in, after):
    k = len(after)
    if k == 0:
        return body
    return lambda *refs: body(*refs[:n_in], *refs[n_in + k:])


def _rms_stats(xf):
    inv = lax.rsqrt(jnp.mean(xf * xf, axis=-1, keepdims=True) + RMS_EPS)
    return xf * inv, inv


def _rms_bwd(dh, xhat, inv, gain):
    dxhat = dh * gain
    dx = inv * (dxhat - xhat * jnp.mean(dxhat * xhat, axis=-1, keepdims=True))
    dgain = jnp.sum(dh * xhat, axis=0, keepdims=True)
    return dx, dgain


def _load_resident(w_hbm, w_ref, sem):
    @pl.when(pl.program_id(0) == 0)
    def _():
        cp = pltpu.make_async_copy(w_hbm, w_ref, sem)
        cp.start()
        cp.wait()


def _ffn_fwd(x, gain, w_gu, w_d=None, *, name, head=None, after=(), tm=256, sub=256, tf=FF_CHUNK):
    t = x.shape[0]
    tm = min(tm, t)
    sub = min(sub, tm)
    n_down = 0 if w_d is None else 1
    n_head = 0 if head is None else 2
    assert n_down or not n_head
    n_in = 3 + n_down + n_head

    def body(*refs):
        x_ref, g_ref = refs[:2]
        w_hbms, head_refs = refs[2:3 + n_down], refs[3 + n_down:n_in]
        xo_refs = refs[n_in:n_in + n_down]
        h_ref, s_ref, sa_ref, sb_ref = refs[n_in + n_down:n_in + n_down + 4]
        head_outs = refs[n_in + n_down + 4:n_in + n_down + 4 + n_head]
        w_refs, sems = refs[n_in + n_down + 4 + n_head:-1], refs[-1]
        for k in range(1 + n_down):
            _load_resident(w_hbms[k], w_refs[k], sems.at[k])

        @pl.when(pl.program_id(0) == 0)
        def _():
            for ref in head_outs:
                ref[...] = jnp.zeros_like(ref)

        for r0 in range(0, tm, sub):
            rows = slice(r0, r0 + sub)
            xf = x_ref[rows, :]
            xhat, _ = _rms_stats(xf)
            h = (xhat * g_ref[...]).astype(BF16)
            h_ref[rows, :] = h
            for c in range(0, D_FF, tf):
                cols = slice(c, min(c + tf, D_FF))
                a = lax.dot_general(h, w_refs[0][0, cols, :], NT_DIMS, preferred_element_type=F32)
                b = lax.dot_general(h, w_refs[0][1, cols, :], NT_DIMS, preferred_element_type=F32)
                sig = jax.nn.sigmoid(a)
                silu = a * sig
                s_ref[rows, cols] = (silu * b).astype(BF16)
                sa_ref[rows, cols] = (b * (sig * (1.0 + a * (1.0 - sig)))).astype(BF16)
                sb_ref[rows, cols] = silu.astype(BF16)
            if not n_down:
                continue
            xo = xf + FFN_RES_SCALE * jnp.dot(s_ref[rows, :], w_refs[1][...], preferred_element_type=F32)
            if head is None:
                xo_refs[0][rows, :] = xo
            else:
                fg_ref, t_ref = head_refs
                loss_ref, dfg_ref = head_outs
                xhat_o, inv_o = _rms_stats(xo)
                err = xhat_o * fg_ref[...] - t_ref[rows, :]
                loss_ref[...] += 0.5 * jnp.sum(jnp.mean(err * err, axis=-1, keepdims=True), axis=0, keepdims=True)
                xo_refs[0][rows, :], dfg = _rms_bwd(err * (1.0 / D_MODEL), xhat_o, inv_o, fg_ref[...])
                dfg_ref[...] += dfg

    row = pl.BlockSpec((tm, D_MODEL), lambda i: (i, 0))
    hid = pl.BlockSpec((tm, D_FF), lambda i: (i, 0))
    vec = pl.BlockSpec((1, D_MODEL), lambda i: (0, 0))
    head_in = [] if head is None else [vec, row]
    head_out = [] if head is None else [pl.BlockSpec((1, 1), lambda i: (0, 0)), vec]
    head_shape = [] if head is None else [jax.ShapeDtypeStruct((1, 1), F32), jax.ShapeDtypeStruct((1, D_MODEL), F32)]
    return pl.pallas_call(
        _behind(body, n_in, after), name=name, grid=(t // tm,),
        in_specs=[row, vec] + [ANY] * (1 + n_down) + head_in + [ANY] * len(after),
        out_specs=[row] * (n_down + 1) + [hid, hid, hid] + head_out,
        out_shape=[jax.ShapeDtypeStruct((t, D_MODEL), F32)] * n_down + [jax.ShapeDtypeStruct((t, D_MODEL), BF16)]
        + [jax.ShapeDtypeStruct((t, D_FF), BF16)] * 3 + head_shape,
        scratch_shapes=[pltpu.VMEM((2, D_FF, D_MODEL), BF16)] + [pltpu.VMEM((D_FF, D_MODEL), BF16)] * n_down
        + [pltpu.SemaphoreType.DMA((2,))],
        compiler_params=_params(("arbitrary",), VMEM_LIMIT),
    )(x, gain, w_gu, *([] if w_d is None else [w_d]), *(head or ()), *after)


def _ffn_down(x, s, w_d, *, name, after=(), tm=512):
    t = x.shape[0]
    tm = min(tm, t)

    def body(x_ref, s_ref, w_hbm, xo_ref, w_ref, sem):
        _load_resident(w_hbm, w_ref, sem)
        xo_ref[...] = x_ref[...] + FFN_RES_SCALE * jnp.dot(s_ref[...], w_ref[...], preferred_element_type=F32)

    row = pl.BlockSpec((tm, D_MODEL), lambda i: (i, 0))
    return pl.pallas_call(
        _behind(body, 3, after), name=name, grid=(t // tm,),
        in_specs=[row, pl.BlockSpec((tm, D_FF), lambda i: (i, 0)), ANY] + [ANY] * len(after), out_specs=row,
        out_shape=jax.ShapeDtypeStruct((t, D_MODEL), F32),
        scratch_shapes=[pltpu.VMEM((D_FF, D_MODEL), BF16), pltpu.SemaphoreType.DMA(())],
        compiler_params=_params(("arbitrary",), VMEM_LIMIT),
    )(x, s, w_d, *after)


def _ffn_dgrad(dxo, x, gain, sa, sb, w_gu, w_d, *, name, after=(), tm=512, sub=512, tf=FF_CHUNK):
    t = x.shape[0]
    tm = min(tm, t)
    sub = min(sub, tm)

    def body(dxo_ref, x_ref, g_ref, sa_ref, sb_ref, wgu_hbm, wd_hbm, dxi_ref, da_ref, db_ref, gb_ref, dg_ref,
             wgu_ref, wd_ref, sems):
        _load_resident(wgu_hbm, wgu_ref, sems.at[0])
        _load_resident(wd_hbm, wd_ref, sems.at[1])

        @pl.when(pl.program_id(0) == 0)
        def _():
            dg_ref[...] = jnp.zeros_like(dg_ref)

        for r0 in range(0, tm, sub):
            rows = slice(r0, r0 + sub)
            go = dxo_ref[rows, :]
            gb = (FFN_RES_SCALE * go).astype(BF16)
            gb_ref[rows, :] = gb
            for c in range(0, D_FF, tf):
                cols = slice(c, min(c + tf, D_FF))
                ds = lax.dot_general(gb, wd_ref[cols, :], NT_DIMS, preferred_element_type=F32)
                da_ref[rows, cols] = (ds * sa_ref[rows, cols].astype(F32)).astype(BF16)
                db_ref[rows, cols] = (ds * sb_ref[rows, cols].astype(F32)).astype(BF16)
            dh = (jnp.dot(da_ref[rows, :], wgu_ref[0], preferred_element_type=F32)
                  + jnp.dot(db_ref[rows, :], wgu_ref[1], preferred_element_type=F32))
            xhat, inv = _rms_stats(x_ref[rows, :])
            dx, dgain = _rms_bwd(dh, xhat, inv, g_ref[...])
            dxi_ref[rows, :] = go + dx
            dg_ref[...] += dgain

    row = pl.BlockSpec((tm, D_MODEL), lambda i: (i, 0))
    hid = pl.BlockSpec((tm, D_FF), lambda i: (i, 0))
    vec = pl.BlockSpec((1, D_MODEL), lambda i: (0, 0))
    return pl.pallas_call(
        _behind(body, 7, after), name=name, grid=(t // tm,),
        in_specs=[row, row, vec, hid, hid, ANY, ANY] + [ANY] * len(after),
        out_specs=[row, hid, hid, row, vec],
        out_shape=[jax.ShapeDtypeStruct((t, D_MODEL), F32), jax.ShapeDtypeStruct((t, D_FF), BF16),
                   jax.ShapeDtypeStruct((t, D_FF), BF16),
                   jax.ShapeDtypeStruct((t, D_MODEL), BF16), jax.ShapeDtypeStruct((1, D_MODEL), F32)],
        scratch_shapes=[pltpu.VMEM((2, D_FF, D_MODEL), BF16), pltpu.VMEM((D_FF, D_MODEL), BF16),
                        pltpu.SemaphoreType.DMA((2,))],
        compiler_params=_params(("arbitrary",), VMEM_LIMIT),
    )(dxo, x, gain, sa, sb, w_gu, w_d, *after)


def _tn_matmul(a, b, *, name, bm, after=(), tk=2048, blocks=None, into=None):
    t, m = a.shape
    n = b.shape[1]
    tk = min(tk, t)
    nk = t // tk
    first, count = blocks or (0, m // bm)
    behind = ([] if into is None else [into]) + list(after)

    def body(a_ref, b_ref, o_ref):
        @pl.when(pl.program_id(1) == 0)
        def _():
            o_ref[...] = jnp.zeros_like(o_ref)

        o_ref[...] += lax.dot_general(a_ref[...], b_ref[...], TN_DIMS, preferred_element_type=F32)

    return pl.pallas_call(
        _behind(body, 2, behind), name=name, grid=(count, nk),
        in_specs=[pl.BlockSpec((tk, bm), lambda i, k: (k, first + i)), pl.BlockSpec((tk, n), lambda i, k: (k, 0))]
        + [ANY] * len(behind),
        out_specs=pl.BlockSpec((bm, n), lambda i, k: (first + i, 0)),
        out_shape=jax.ShapeDtypeStruct((m, n), F32),
        input_output_aliases={} if into is None else {2: 0},
        compiler_params=_params(("parallel", "arbitrary"), VMEM_LIMIT),
    )(a, b, *behind)


def _rope_tables(t, after=()):
    inv_freq = ROPE_THETA ** (-jnp.arange(0, ROT_DIM, 2, dtype=F32) / ROT_DIM)
    ang = inv_freq[:, None] * jnp.arange(t, dtype=F32)[None, :]
    compact = jnp.stack([jnp.cos(ang), jnp.sin(ang)])
    tr = min(1024, t)

    def body(c_ref, o_ref):
        for k in range(2):
            o_ref[k] = jnp.tile(c_ref[k], (128 // inv_freq.shape[0], 1)).T

    return pl.pallas_call(
        _behind(body, 1, after), name="rope_tables", grid=(t // tr,),
        in_specs=[pl.BlockSpec((2, inv_freq.shape[0], tr), lambda i: (0, 0, i))] + [ANY] * len(after),
        out_specs=pl.BlockSpec((2, tr, 128), lambda i: (0, i, 0)),
        out_shape=jax.ShapeDtypeStruct((2, t, 128), F32), compiler_params=_params(("parallel",)),
    )(compact, *after)


def _rope_multipliers(cos_sin):
    half = ROT_DIM // 2
    cos, sin = cos_sin[0], cos_sin[1]
    d = lax.broadcasted_iota(jnp.int32, cos.shape, 1) & (HEAD_DIM - 1)
    mult = jnp.where(d < ROT_DIM, cos, 1.0)
    from_lo = jnp.where((d >= half) & (d < ROT_DIM), sin, 0.0)
    from_hi = jnp.where(d < half, -sin, 0.0)
    return mult, from_lo, from_hi


def _tile_lanes(tab, width):
    return jnp.tile(tab, (1, width // tab.shape[1]))


def _rope(v, tab):
    w = v.shape[1]
    half_rot = ROT_DIM // 2
    return (v * _tile_lanes(tab[0], w)
            + pltpu.roll(v, half_rot, axis=1) * _tile_lanes(tab[1], w)
            + pltpu.roll(v, w - half_rot, axis=1) * _tile_lanes(tab[2], w))


def _rope_bwd(dv, tab):
    w = dv.shape[1]
    half_rot = ROT_DIM // 2
    return (dv * _tile_lanes(tab[0], w)
            + pltpu.roll(dv * _tile_lanes(tab[1], w), w - half_rot, axis=1)
            + pltpu.roll(dv * _tile_lanes(tab[2], w), half_rot, axis=1))


def _shift_rows(v, prev8_ref, n):
    r = lax.broadcasted_iota(jnp.int32, v.shape, 0)
    rolled = pltpu.roll(v, n, axis=0)
    last = prev8_ref[7:8, :]
    if n == 1:
        return jnp.where(r >= 1, rolled, last)
    return jnp.where(r >= 2, rolled, jnp.where(r == 0, prev8_ref[6:7, :], last))


def _shift_rows_up(v, next8_ref, n):
    rows = v.shape[0]
    r = lax.broadcasted_iota(jnp.int32, v.shape, 0)
    rolled = pltpu.roll(v, rows - n, axis=0)
    first = next8_ref[0:1, :]
    if n == 1:
        return jnp.where(r <= rows - 2, rolled, first)
    return jnp.where(r <= rows - 3, rolled, jnp.where(r == rows - 2, first, next8_ref[1:2, :]))


def _lane_half_mask(shape, half):
    lane = lax.broadcasted_iota(jnp.int32, shape, 1)
    return (lane >= HEAD_DIM) if half else (lane < HEAD_DIM)


def _to_kv_lanes(chunk, head, kv):
    if head % 2 != kv:
        chunk = pltpu.roll(chunk, HEAD_DIM, axis=1)
    return jnp.where(_lane_half_mask(chunk.shape, kv), chunk, 0.0)


def _from_kv_lanes(chunk, head, kv):
    chunk = jnp.where(_lane_half_mask(chunk.shape, kv), chunk, 0.0)
    if head % 2 != kv:
        chunk = pltpu.roll(chunk, HEAD_DIM, axis=1)
    return chunk


def _stack_heads(wide):
    parts = []
    for head in range(N_Q_HEADS):
        chunk = wide[:, (head // 2) * 128:(head // 2 + 1) * 128]
        parts.append(_to_kv_lanes(chunk, head, head // Q_PER_KV))
    return jnp.concatenate(parts, axis=0)


def _window_mask(has_prev):
    shape = (N_Q_HEADS * BLOCK, 2 * BLOCK)
    qi = lax.broadcasted_iota(jnp.int32, shape, 0) & (BLOCK - 1)
    kj = lax.broadcasted_iota(jnp.int32, shape, 1)
    first_key = BLOCK - has_prev * BLOCK
    in_prev = (kj < BLOCK) & (kj > qi) & (kj >= first_key)
    in_own = (kj >= BLOCK) & ((kj - BLOCK) <= qi)
    return in_prev | in_own


def _sink_column(sink_ref):
    row = lax.broadcasted_iota(jnp.int32, (N_Q_HEADS * BLOCK, 1), 0)
    col = jnp.full((N_Q_HEADS * BLOCK, 1), sink_ref[0, 0], F32)
    for head in range(1, N_Q_HEADS):
        col = jnp.where(row >= head * BLOCK, sink_ref[0, head], col)
    return col


def _softmax_with_sink(q4, k2, mask, sink):
    s = lax.dot_general(q4, k2, NT_DIMS, preferred_element_type=F32) * SM_SCALE
    s = jnp.where(mask, s, MASK_VALUE)
    m = jnp.maximum(jnp.max(s, axis=-1, keepdims=True), sink)
    p = jnp.exp(s - m)
    e_sink = jnp.exp(sink - m)
    inv_den = 1.0 / (jnp.sum(p, axis=-1, keepdims=True) + e_sink)
    return p * inv_den, e_sink * inv_den


def _conv_terms(zf, prev8_ref, w_ref):
    b_gate, c_gate, u = zf[:, 0:CONV_W], zf[:, CONV_W:2 * CONV_W], zf[:, 2 * CONV_W:3 * CONV_W]
    vc = c_gate * u
    vm1 = _shift_rows(vc, prev8_ref, 1)
    vm2 = _shift_rows(vc, prev8_ref, 2)
    conv = w_ref[0:1, :] * vm2 + w_ref[1:2, :] * vm1 + w_ref[2:3, :] * vc
    return b_gate, c_gate, u, vc, vm1, vm2, conv


def _mixer_fwd(x, gain, win_t, wout, conv_w, sinks, rope, *, name, after=(), tq=512):
    t = x.shape[0]
    tq = min(tq, t)
    nblk = tq // BLOCK

    def body(x_ref, g_ref, win_hbm, wout_hbm, cw_ref, sink_ref, rope_ref,
             xo_ref, h_ref, z_ref, y_ref, kprev_ref, vprev_ref, cprev_ref, win_ref, wout_ref, sems):
        i = pl.program_id(0)
        _load_resident(win_hbm, win_ref, sems.at[0])
        _load_resident(wout_hbm, wout_ref, sems.at[1])

        @pl.when(i == 0)
        def _():
            kprev_ref[...] = jnp.zeros_like(kprev_ref)
            vprev_ref[...] = jnp.zeros_like(vprev_ref)
            cprev_ref[...] = jnp.zeros_like(cprev_ref)

        xf = x_ref[...]
        xhat, _ = _rms_stats(xf)
        h = (xhat * g_ref[...]).astype(BF16)
        h_ref[...] = h
        def project(c0, c1):
            zc = lax.dot_general(h, win_ref[c0:c1, :], NT_DIMS, preferred_element_type=F32).astype(BF16)
            z_ref[:, c0:c1] = zc
            return zc

        zb = project(Q_OFF, Z_W)
        zf = zb.astype(F32)
        tab = _rope_multipliers(rope_ref[...])
        qr = _rope(zf[:, 0:ATTN_W], tab)
        kr = _rope(zf[:, K_OFF - Q_OFF:V_OFF - Q_OFF], tab).astype(BF16)
        vb = zb[:, V_OFF - Q_OFF:Z_W - Q_OFF]
        conv_cols = [(c, c + CONV_W) for c in range(0, Q_OFF, CONV_W)]
        conv_z = []

        y_attn = []
        for j in range(nblk):
            if len(conv_z) < len(conv_cols):
                conv_z.append(project(*conv_cols[len(conv_z)]))
            rows = slice(j * BLOCK, (j + 1) * BLOCK)
            prev = slice((j - 1) * BLOCK, j * BLOCK)
            k2 = jnp.concatenate([kprev_ref[...] if j == 0 else kr[prev], kr[rows]], axis=0)
            v2 = jnp.concatenate([vprev_ref[...] if j == 0 else vb[prev], vb[rows]], axis=0)
            mask = _window_mask(jnp.minimum(i, 1) if j == 0 else 1)
            q8 = _stack_heads(qr[rows]).astype(BF16)
            probs, _ = _softmax_with_sink(q8, k2, mask, _sink_column(sink_ref))
            o8 = jnp.dot(probs.astype(BF16), v2, preferred_element_type=F32)
            chunks = [jnp.zeros((BLOCK, 128), F32) for _ in range(ATTN_W // 128)]
            for head in range(N_Q_HEADS):
                chunks[head // 2] += _from_kv_lanes(o8[head * BLOCK:(head + 1) * BLOCK], head, head // Q_PER_KV)
            y_attn.append(jnp.concatenate(chunks, axis=1))
        kprev_ref[...] = kr[tq - BLOCK:tq]
        vprev_ref[...] = vb[tq - BLOCK:tq]
        while len(conv_z) < len(conv_cols):
            conv_z.append(project(*conv_cols[len(conv_z)]))
        ya = jnp.concatenate(y_attn, axis=0).astype(BF16)
        y_ref[:, CONV_W:] = ya
        xo = xf + jnp.dot(ya, wout_ref[CONV_W:, :], preferred_element_type=F32)
        b_gate, _, _, vc, _, _, conv = _conv_terms(jnp.concatenate(conv_z, axis=1).astype(F32), cprev_ref, cw_ref)
        yc = (b_gate * conv).astype(BF16)
        cprev_ref[...] = vc[tq - 8:tq, :]
        y_ref[:, :CONV_W] = yc
        xo_ref[...] = xo + jnp.dot(yc, wout_ref[:CONV_W, :], preferred_element_type=F32)

    row = pl.BlockSpec((tq, D_MODEL), lambda i: (i, 0))
    full = lambda shape: pl.BlockSpec(shape, lambda i: (0,) * len(shape))
    return pl.pallas_call(
        _behind(body, 7, after), name=name, grid=(t // tq,),
        in_specs=[row, full((1, D_MODEL)), ANY, ANY, full((3, CONV_W)),
                  pl.BlockSpec(memory_space=pltpu.SMEM), pl.BlockSpec((2, tq, 128), lambda i: (0, i, 0))]
        + [ANY] * len(after),
        out_specs=[row, row, pl.BlockSpec((tq, Z_W), lambda i: (i, 0)), row],
        out_shape=[jax.ShapeDtypeStruct((t, D_MODEL), F32), jax.ShapeDtypeStruct((t, D_MODEL), BF16),
                   jax.ShapeDtypeStruct((t, Z_W), BF16), jax.ShapeDtypeStruct((t, D_MODEL), BF16)],
        scratch_shapes=[pltpu.VMEM((BLOCK, KV_W), BF16), pltpu.VMEM((BLOCK, KV_W), BF16),
                        pltpu.VMEM((8, CONV_W), F32), pltpu.VMEM((Z_W, D_MODEL), BF16),
                        pltpu.VMEM((D_MODEL, D_MODEL), BF16), pltpu.SemaphoreType.DMA((2,))],
        compiler_params=_params(("arbitrary",), VMEM_LIMIT),
    )(x, gain, win_t, wout, conv_w, sinks, rope, *after)


def _mixer_bwd(dxo, x, gain, y, z, win_t, wout, conv_w, sinks, rope, *, name, after=(), tq=256):
    t = x.shape[0]
    tq = min(tq, t)
    nt, nblk = t // tq, tq // BLOCK

    def body(dxo_ref, x_ref, g_ref, y_ref, z_ref, zp_ref, win_hbm, wout_hbm, cw_ref, sink_ref, rope_ref, ropep_ref,
             dxi_ref, dz_ref, gb_ref, dcw_ref, dsink_ref, dg_ref, dk_ref, dv_ref, dcn_ref, pvc_ref,
             win_ref, wout_ref, sems):
        i = pl.program_id(0)
        tile = nt - 1 - i
        _load_resident(win_hbm, win_ref, sems.at[0])
        _load_resident(wout_hbm, wout_ref, sems.at[1])

        @pl.when(i == 0)
        def _():
            dk_ref[...] = jnp.zeros_like(dk_ref)
            dv_ref[...] = jnp.zeros_like(dv_ref)
            dcn_ref[...] = jnp.zeros_like(dcn_ref)
            dcw_ref[...] = jnp.zeros_like(dcw_ref)
            dsink_ref[...] = jnp.zeros_like(dsink_ref)
            dg_ref[...] = jnp.zeros_like(dg_ref)

        has_prev = jnp.minimum(tile, 1)
        go = dxo_ref[...]
        gb = go.astype(BF16)
        gb_ref[...] = gb
        dy = lax.dot_general(gb, wout_ref[...], NT_DIMS, preferred_element_type=F32)
        dy_conv, dy_attn = dy[:, 0:CONV_W], dy[:, CONV_W:D_MODEL]
        zb, zpb = z_ref[...], zp_ref[...]
        zf = zb.astype(F32)
        zpf = zpb.astype(F32) * has_prev.astype(F32)

        pvc_ref[...] = (zpf[:, CONV_W:2 * CONV_W] * zpf[:, 2 * CONV_W:3 * CONV_W])[BLOCK - 8:BLOCK, :]
        b_gate, c_gate, u, vc, vm1, vm2, conv = _conv_terms(zf, pvc_ref, cw_ref)
        d_bgate = dy_conv * conv
        dc = dy_conv * b_gate
        tap = lax.broadcasted_iota(jnp.int32, (8, CONV_W), 0)
        dcw_ref[...] += jnp.where(tap == 0, jnp.sum(dc * vm2, axis=0, keepdims=True),
                                  jnp.where(tap == 1, jnp.sum(dc * vm1, axis=0, keepdims=True),
                                            jnp.where(tap == 2, jnp.sum(dc * vc, axis=0, keepdims=True), 0.0)))
        dvc = (cw_ref[2:3, :] * dc + cw_ref[1:2, :] * _shift_rows_up(dc, dcn_ref, 1)
               + cw_ref[0:1, :] * _shift_rows_up(dc, dcn_ref, 2))
        dcn_ref[...] = dc[0:8, :]
        d_cgate = dvc * u
        d_u = dvc * c_gate

        tab, tabp = _rope_multipliers(rope_ref[...]), _rope_multipliers(ropep_ref[...])
        qr = _rope(zf[:, Q_OFF:K_OFF], tab)
        kr = _rope(zf[:, K_OFF:V_OFF], tab).astype(BF16)
        kpr = _rope(zpf[:, K_OFF:V_OFF], tabp).astype(BF16)
        vb, vpb = zb[:, V_OFF:Z_W], zpb[:, V_OFF:Z_W]
        out = y_ref[:, CONV_W:D_MODEL].astype(F32)
        do_out = dy_attn * out
        lane = lax.broadcasted_iota(jnp.int32, (1, 128), 1)
        dsink = jnp.zeros((1, 128), F32)
        dk_next, dv_next = dk_ref[...], dv_ref[...]
        dq_rows, dk_rows, dv_rows = [None] * nblk, [None] * nblk, [None] * nblk
        for j in reversed(range(nblk)):
            rows = slice(j * BLOCK, (j + 1) * BLOCK)
            prev = slice((j - 1) * BLOCK, j * BLOCK)
            k2 = jnp.concatenate([kpr if j == 0 else kr[prev], kr[rows]], axis=0)
            v2 = jnp.concatenate([vpb if j == 0 else vb[prev], vb[rows]], axis=0)
            mask = _window_mask(has_prev if j == 0 else 1)
            q8 = _stack_heads(qr[rows]).astype(BF16)
            do8 = _stack_heads(dy_attn[rows]).astype(BF16)
            delta = jnp.sum(_stack_heads(do_out[rows]), axis=-1, keepdims=True)
            probs, p_sink = _softmax_with_sink(q8, k2, mask, _sink_column(sink_ref))
            dp = lax.dot_general(do8, v2, NT_DIMS, preferred_element_type=F32)
            ds = (probs * (dp - delta) * SM_SCALE).astype(BF16)
            dq8 = jnp.dot(ds, k2, preferred_element_type=F32)
            dk2 = lax.dot_general(ds, q8, TN_DIMS, preferred_element_type=F32)
            dv2 = lax.dot_general(probs.astype(BF16), do8, TN_DIMS, preferred_element_type=F32)
            sink_terms = p_sink * delta
            dq_chunks = [jnp.zeros((BLOCK, 128), F32) for _ in range(ATTN_W // 128)]
            for head in range(N_Q_HEADS):
                grp = slice(head * BLOCK, (head + 1) * BLOCK)
                dq_chunks[head // 2] += _from_kv_lanes(dq8[grp], head, head // Q_PER_KV)
                dsink = dsink - jnp.where(lane == head, jnp.sum(sink_terms[grp], axis=0, keepdims=True), 0.0)
            dq_rows[j] = jnp.concatenate(dq_chunks, axis=1)
            dk_rows[j] = dk2[BLOCK:] + dk_next
            dv_rows[j] = dv2[BLOCK:] + dv_next
            dk_next, dv_next = dk2[:BLOCK], dv2[:BLOCK]
        dk_ref[...] = dk_next
        dv_ref[...] = dv_next
        dsink_ref[...] += dsink
        dq = _rope_bwd(jnp.concatenate(dq_rows, axis=0), tab)
        dk = _rope_bwd(jnp.concatenate(dk_rows, axis=0), tab)
        dv = jnp.concatenate(dv_rows, axis=0)

        dzb = jnp.concatenate([d_bgate, d_cgate, d_u, dq, dk, dv], axis=1).astype(BF16)
        dz_ref[...] = dzb
        dh = jnp.dot(dzb, win_ref[...], preferred_element_type=F32)
        xhat, inv = _rms_stats(x_ref[...])
        dx, dgain = _rms_bwd(dh, xhat, inv, g_ref[...])
        dxi_ref[...] = go + dx
        dg_ref[...] += dgain

    rev = lambda i: (nt - 1 - i, 0)
    block_before = lambda i: jnp.maximum((nt - 1 - i) * nblk - 1, 0)
    row = pl.BlockSpec((tq, D_MODEL), rev)
    full = lambda shape: pl.BlockSpec(shape, lambda i: (0,) * len(shape))
    return pl.pallas_call(
        _behind(body, 12, after), name=name, grid=(nt,),
        in_specs=[row, row, full((1, D_MODEL)), row,
                  pl.BlockSpec((tq, Z_W), rev), pl.BlockSpec((BLOCK, Z_W), lambda i: (block_before(i), 0)),
                  ANY, ANY, full((3, CONV_W)),
                  pl.BlockSpec(memory_space=pltpu.SMEM),
                  pl.BlockSpec((2, tq, 128), lambda i: (0, nt - 1 - i, 0)),
                  pl.BlockSpec((2, BLOCK, 128), lambda i: (0, block_before(i), 0))] + [ANY] * len(after),
        out_specs=[row, pl.BlockSpec((tq, Z_W), rev), row, full((8, CONV_W)), full((1, 128)), full((1, D_MODEL))],
        out_shape=[jax.ShapeDtypeStruct((t, D_MODEL), F32), jax.ShapeDtypeStruct((t, Z_W), BF16),
                   jax.ShapeDtypeStruct((t, D_MODEL), BF16), jax.ShapeDtypeStruct((8, CONV_W), F32),
                   jax.ShapeDtypeStruct((1, 128), F32), jax.ShapeDtypeStruct((1, D_MODEL), F32)],
        scratch_shapes=[pltpu.VMEM((BLOCK, KV_W), F32), pltpu.VMEM((BLOCK, KV_W), F32), pltpu.VMEM((8, CONV_W), F32),
                        pltpu.VMEM((8, CONV_W), F32), pltpu.VMEM((Z_W, D_MODEL), BF16),
                        pltpu.VMEM((D_MODEL, D_MODEL), BF16), pltpu.SemaphoreType.DMA((2,))],
        compiler_params=_params(("arbitrary",), VMEM_LIMIT),
    )(dxo, x, gain, y, z, z, win_t, wout, conv_w, sinks, rope, rope, *after)


def _place():
    x, y, c = lax.axis_index("x"), lax.axis_index("y"), lax.axis_index("c")
    other_chips = [(1 - x, y), (x, 1 - y), (1 - x, 1 - y)]
    return x, y, c, other_chips


def _all_gather_rows(shards, place=(), *, name):
    n, p = len(shards), len(place)

    def body(*refs):
        srcs, place_srcs = refs[:n], refs[n:n + p]
        outs, place_outs = refs[n + p:2 * n + p], refs[2 * n + p:2 * (n + p)]
        send_sems, recv_sems, local_sems = refs[2 * (n + p):]
        x, y, c, chips = _place()
        me, sibling = (x, y, c), (x, y, 1 - c)

        def rows(t, px, py, pc):
            r = srcs[t].shape[-2]
            start = pl.multiple_of((4 * px + 2 * py + pc) * r, 16 if r % 16 == 0 else 8)
            if len(srcs[t].shape) == 3:
                return outs[t].at[:, pl.ds(start, r), :]
            return outs[t].at[pl.ds(start, r), :]

        def copy(t, k, block, to, own=False):
            return pltpu.make_async_remote_copy(
                src_ref=srcs[t] if own else rows(t, *block), dst_ref=rows(t, *block),
                send_sem=send_sems.at[t, k], recv_sem=recv_sems.at[t, k], device_id=to, device_id_type=MESH)

        mine = [pltpu.make_async_copy(srcs[t], rows(t, *me), local_sems.at[t]) for t in range(n)]
        mine += [pltpu.make_async_copy(place_srcs[q],
                                       _block_rows(place_outs[q], place_srcs[q].shape[-2], 4 * x + 2 * y + c),
                                       local_sems.at[n + q]) for q in range(p)]
        for q in range(p):
            mine[n + q].start()
        first = []
        for t in range(n):
            mine[t].start()
            first.append(copy(t, 0, me, sibling, own=True))
            first += [copy(t, 1 + j, me, (*chip, c), own=True) for j, chip in enumerate(chips)]
        for cp in first:
            cp.start()
        passed = []
        for j, chip in enumerate(chips):
            for t in range(n):
                copy(t, 1 + j, (*chip, c), me).wait_recv()
                fwd = copy(t, 4 + j, (*chip, c), sibling)
                fwd.start()
                passed.append(fwd)
        for t in range(n):
            copy(t, 0, sibling, me).wait_recv()
            for j, chip in enumerate(chips):
                copy(t, 4 + j, (*chip, 1 - c), me).wait_recv()
        for cp in first + passed:
            cp.wait_send()
        for cp in mine:
            cp.wait()

    out_shape = [jax.ShapeDtypeStruct(s.shape[:-2] + (N_DEV * s.shape[-2], s.shape[-1]), s.dtype)
                 for s in list(shards) + list(place)]
    res = pl.pallas_call(
        body, name=name, in_specs=[ANY] * (n + p), out_specs=[ANY] * (n + p), out_shape=out_shape,
        scratch_shapes=[pltpu.SemaphoreType.DMA((n, 7)), pltpu.SemaphoreType.DMA((n, 7)),
                        pltpu.SemaphoreType.DMA((n + p,))],
    )(*shards, *place)
    return res[:n], res[n:]


def _split_start(bufs, n_copies, plan, *, name, after=()):
    n = len(bufs)

    def body(*refs):
        token = refs[-1]
        for cp in plan(refs[:n], refs[n], refs[n + 1]):
            cp.start()
        token[...] = jnp.zeros_like(token)

    res = pl.pallas_call(
        _behind(body, n, after), name=name, in_specs=[HBM_SPEC] * n + [ANY] * len(after),
        out_specs=(SEM_SPEC, SEM_SPEC, *[HBM_SPEC] * n, pl.BlockSpec(memory_space=pltpu.VMEM)),
        out_shape=(pltpu.SemaphoreType.DMA((n_copies,)), pltpu.SemaphoreType.DMA((n_copies,)),
                   *[pltpu.HBM(b.shape, b.dtype) for b in bufs], jax.ShapeDtypeStruct((8, 128), F32)),
        input_output_aliases={i: 2 + i for i in range(n)},
        compiler_params=pltpu.CompilerParams(has_side_effects=DATAFLOW),
    )(*[pltpu.with_memory_space_constraint(b, pltpu.HBM) for b in bufs], *after)
    return res[0], res[1], list(res[2:2 + n]), res[-1]


def _split_wait(send_sems, recv_sems, bufs, after, plan, *, name):
    n = len(bufs)

    def body(*refs):
        for cp in plan(refs[:n], refs[n], refs[n + 1]):
            cp.wait_send()
            cp.wait_recv()

    return list(pl.pallas_call(
        body, name=name, in_specs=[HBM_SPEC] * n + [SEM_SPEC, SEM_SPEC, ANY], out_specs=[HBM_SPEC] * n,
        out_shape=tuple(pltpu.HBM(b.shape, b.dtype) for b in bufs),
        input_output_aliases={i: i for i in range(n)},
        compiler_params=pltpu.CompilerParams(has_side_effects=DATAFLOW),
    )(*bufs, send_sems, recv_sems, after))


def _sibling_plan(n):
    def plan(bufs, send_sems, recv_sems):
        x, y, c, _ = _place()
        return [pltpu.make_async_remote_copy(
            src_ref=bufs[t].at[:, 1 - c], dst_ref=bufs[n + t], send_sem=send_sems.at[t], recv_sem=recv_sems.at[t],
            device_id=(x, y, 1 - c), device_id_type=MESH) for t in range(n)]
    return plan


def _block_rows(ref, r, blk):
    start = pl.multiple_of(blk * r, 16 if r % 16 == 0 else 8)
    return ref.at[(slice(None),) * (len(ref.shape) - 2) + (pl.ds(start, r), slice(None))]


def _remote(src, dst, send_sems, recv_sems, k, peer):
    return pltpu.make_async_remote_copy(src_ref=src, dst_ref=dst, send_sem=send_sems.at[k], recv_sem=recv_sems.at[k],
                                        device_id=peer, device_id_type=MESH)


def _gather_send_plan(n, first=0):
    def plan(bufs, send_sems, recv_sems):
        x, y, c, chips = _place()
        peers = [(x, y, 1 - c)] + [(px, py, c) for px, py in chips]
        copies = []
        for t in range(n):
            dst = _block_rows(bufs[n + t], bufs[t].shape[-2], 4 * x + 2 * y + c)
            copies += [_remote(bufs[t], dst, send_sems, recv_sems, 4 * (first + t) + k, peer)
                       for k, peer in enumerate(peers)]
        return copies
    return plan


def _gather_forward_plan(rows):
    def plan(bufs, send_sems, recv_sems):
        x, y, c, chips = _place()
        copies = []
        for t, r in enumerate(rows):
            for j, (px, py) in enumerate(chips):
                blk = _block_rows(bufs[t], r, 4 * px + 2 * py + c)
                copies.append(_remote(blk, blk, send_sems, recv_sems, 3 * t + j, (x, y, 1 - c)))
        return copies
    return plan


def _chips_plan(n, with_small):
    def plan(bufs, send_sems, recv_sems):
        x, y, c, chips = _place()
        copies = []
        for t in range(n):
            for j, (px, py) in enumerate(chips):
                copies.append(_remote(bufs[t].at[2 * px + py], bufs[n + t].at[j], send_sems, recv_sems, 3 * t + j,
                                      (px, py, c)))
        if with_small:
            mine = _block_rows(bufs[2 * n], 8, 4 * x + 2 * y + c)
            flips = [(fx, fy, fc) for fx in range(2) for fy in range(2) for fc in range(2)][1:]
            for k, (fx, fy, fc) in enumerate(flips):
                peer = (x + fx - 2 * x * fx, y + fy - 2 * y * fy, c + fc - 2 * c * fc)
                copies.append(_remote(mine, mine, send_sems, recv_sems, 3 * n + k, peer))
        return copies
    return plan


def _place_own(fulls, shards, index, *, name):
    n = len(fulls)

    def body(index_ref, *refs):
        for t in range(n):
            refs[2 * n + t][...] = refs[n + t][...]

    def block_of(shard):
        lead = len(shard.shape) - 2
        return pl.BlockSpec(shard.shape, lambda i, index_ref: (0,) * lead + (index_ref[0], 0))

    def whole(shard):
        return pl.BlockSpec(shard.shape, lambda i, index_ref: (0,) * len(shard.shape))

    return list(pl.pallas_call(
        body, name=name,
        grid_spec=pltpu.PrefetchScalarGridSpec(
            num_scalar_prefetch=1, grid=(1,),
            in_specs=[ANY] * n + [whole(s) for s in shards], out_specs=[block_of(s) for s in shards]),
        out_shape=[jax.ShapeDtypeStruct(f.shape, f.dtype) for f in fulls],
        input_output_aliases={1 + t: t for t in range(n)},
        compiler_params=_params(("arbitrary",)),
    )(index, *fulls, *shards))


N_STEPS_SMALL = 2


def _add_sibling(grads, recvs, place, *, name):
    n = len(grads)

    def body(place_ref, *refs):
        chip = place_ref[1]
        for t in range(n):
            g_ref, r_ref, own_ref, ob_ref = refs[2 * t], refs[2 * t + 1], refs[2 * n + 2 * t], refs[2 * n + 2 * t + 1]
            own = jnp.zeros(own_ref.shape, F32)
            for m in range(4):
                p = g_ref[m, 0] + r_ref[m]
                ob_ref[m] = p.astype(BF16)
                own = jnp.where(chip == m, p, own)
            own_ref[...] = own

    in_specs, out_specs, out_shape = [], [], []
    for g, r in zip(grads, recvs):
        tr = g.shape[2] // N_STEPS_SMALL
        blocks = pl.BlockSpec((4, tr, D_MODEL), lambda i, place_ref: (0, i, 0))
        in_specs += [pl.BlockSpec((4, 1, tr, D_MODEL), lambda i, place_ref: (0, place_ref[0], i, 0)), blocks]
        out_specs += [pl.BlockSpec((tr, D_MODEL), lambda i, place_ref: (i, 0)), blocks]
        out_shape += [jax.ShapeDtypeStruct(r.shape[1:], F32), jax.ShapeDtypeStruct(r.shape, BF16)]
    res = pl.pallas_call(
        body, name=name,
        grid_spec=pltpu.PrefetchScalarGridSpec(num_scalar_prefetch=1, grid=(N_STEPS_SMALL,), in_specs=in_specs,
                                               out_specs=out_specs),
        out_shape=out_shape, compiler_params=_params(("arbitrary",), VMEM_LIMIT),
    )(place, *[a for pair in zip(grads, recvs) for a in pair])
    return [(res[2 * t], res[2 * t + 1]) for t in range(n)]


def _reduce_adamw(parts, *, name, after=()):
    n = len(parts)

    def body(*refs):
        for t in range(n):
            p_ref, r_ref, w_ref, m_ref, v_ref = refs[5 * t:5 * t + 5]
            g_ref, d_ref, mo_ref, vo_ref = refs[5 * n + 4 * t:5 * n + 4 * t + 4]
            g = p_ref[...] + r_ref[0].astype(F32) + r_ref[1].astype(F32) + r_ref[2].astype(F32)
            g_ref[...] = g
            d_ref[...], mo_ref[...], vo_ref[...] = _adamw_math(w_ref[...], g, m_ref[...], v_ref[...])
        refs[-1][...] = jnp.zeros_like(refs[-1])

    in_specs, out_specs, out_shape = [], [], []
    for own, _, _, _, _ in parts:
        rows = own.shape[0]
        tr = rows // N_STEPS_SMALL
        spec = pl.BlockSpec((tr, D_MODEL), lambda i: (i, 0))
        in_specs += [spec, pl.BlockSpec((3, tr, D_MODEL), lambda i: (0, i, 0)), spec, spec, spec]
        out_specs += [spec] * 4
        out_shape += [jax.ShapeDtypeStruct((rows, D_MODEL), F32)] * 4
    res = pl.pallas_call(
        _behind(body, 5 * n, after), name=name, grid=(N_STEPS_SMALL,),
        in_specs=in_specs + [ANY] * len(after),
        out_specs=out_specs + [pl.BlockSpec((8, 128), lambda i: (0, 0))],
        out_shape=out_shape + [jax.ShapeDtypeStruct((8, 128), F32)],
        compiler_params=_params(("arbitrary",), VMEM_LIMIT),
    )(*[a for part in parts for a in part], *after)
    return [tuple(res[4 * t:4 * t + 4]) for t in range(n)], res[-1]


def _adamw_math(w, g, m, v):
    m = ADAM_B1 * m + (1.0 - ADAM_B1) * g
    v = ADAM_B2 * v + (1.0 - ADAM_B2) * (g * g)
    m_hat = m / (1.0 - ADAM_B1 ** ADAM_STEP)
    v_hat = v / (1.0 - ADAM_B2 ** ADAM_STEP)
    delta = -ADAM_LR * (m_hat / (jnp.sqrt(v_hat) + ADAM_EPS) + ADAM_WD * w)
    return delta, m, v


SMALL_NAMES = ["ffn1_norm", "mix_norm", "ffn2_norm", "final_norm", "conv_w", "attn_sinks"]


def _update_small(given, moments_m, moments_v, small_all, my_index, *, name):
    conv_cols = given["conv_w"].shape[2]
    per_block = 128 // conv_cols

    def two_d(nm, a):
        return a.reshape(1, D_MODEL) if nm == "final_norm" else a

    operands = [two_d(nm, src[nm]) for nm in SMALL_NAMES for src in (given, moments_m, moments_v)]
    n = len(SMALL_NAMES)

    def body(index_ref, all_ref, conv_ref, *refs):
        ins, outs = refs[:3 * n], refs[3 * n:]
        total, conv_total = all_ref[0], conv_ref[0]
        for k in range(1, N_DEV):
            total, conv_total = total + all_ref[k], conv_total + conv_ref[k]
        which = index_ref[0] % per_block
        conv_g = conv_total[4:7, :conv_cols]
        for j in range(1, per_block):
            conv_g = jnp.where(which == j, conv_total[4:7, j * conv_cols:(j + 1) * conv_cols], conv_g)
        grads = [total[0:1], total[1:2], total[2:3], total[3:4], conv_g[None], total[7:8, :N_Q_HEADS]]
        for t, g in enumerate(grads):
            w_ref, m_ref, v_ref = ins[3 * t:3 * t + 3]
            g_ref, d_ref, mo_ref, vo_ref = outs[4 * t:4 * t + 4]
            g_ref[...] = g
            d_ref[...], mo_ref[...], vo_ref[...] = _adamw_math(w_ref[...], g, m_ref[...], v_ref[...])
        outs[-1][...] = total[7:8, LOSS_LANE:LOSS_LANE + 1]

    def whole(shape):
        return pl.BlockSpec(shape, lambda i, index_ref: (0,) * len(shape))

    shapes = [a.shape for a in operands[::3] for _ in range(4)] + [(1, 1)]
    res = pl.pallas_call(
        body, name=name,
        grid_spec=pltpu.PrefetchScalarGridSpec(
            num_scalar_prefetch=1, grid=(1,),
            in_specs=[whole(small_all.shape),
                      pl.BlockSpec((N_DEV, 8, 128), lambda i, index_ref: (0, 0, index_ref[0] // per_block))]
            + [whole(a.shape) for a in operands],
            out_specs=[whole(s) for s in shapes]),
        out_shape=[jax.ShapeDtypeStruct(s, F32) for s in shapes],
        compiler_params=_params(("arbitrary",)),
    )(my_index.astype(jnp.int32).reshape(1), small_all, small_all, *operands)
    results = {nm: tuple(a.reshape(given[nm].shape) for a in res[4 * t:4 * t + 4]) for t, nm in enumerate(SMALL_NAMES)}
    return results, res[-1]


def kernel(x, ffn1_norm, ffn1_w_gate, ffn1_w_up, ffn1_w_down, mix_norm, w_in, conv_w, attn_sinks, w_out, ffn2_norm, ffn2_w_gate, ffn2_w_up, ffn2_w_down, final_norm, loss_target, m_ffn1_norm, m_ffn1_w_gate, m_ffn1_w_up, m_ffn1_w_down, m_mix_norm, m_w_in, m_conv_w, m_attn_sinks, m_w_out, m_ffn2_norm, m_ffn2_w_gate, m_ffn2_w_up, m_ffn2_w_down, m_final_norm, v_ffn1_norm, v_ffn1_w_gate, v_ffn1_w_up, v_ffn1_w_down, v_mix_norm, v_w_in, v_conv_w, v_attn_sinks, v_w_out, v_ffn2_norm, v_ffn2_w_gate, v_ffn2_w_up, v_ffn2_w_down, v_final_norm):
    ix, iy, ic = lax.axis_index("x"), lax.axis_index("y"), lax.axis_index("c")
    my_index = 4 * ix + 2 * iy + ic
    place = jnp.stack([ic, 2 * ix + iy]).astype(jnp.int32)

    given = dict(ffn1_norm=ffn1_norm, ffn1_w_gate=ffn1_w_gate, ffn1_w_up=ffn1_w_up, ffn1_w_down=ffn1_w_down,
                 mix_norm=mix_norm, w_in=w_in, conv_w=conv_w, attn_sinks=attn_sinks, w_out=w_out, ffn2_norm=ffn2_norm,
                 ffn2_w_gate=ffn2_w_gate, ffn2_w_up=ffn2_w_up, ffn2_w_down=ffn2_w_down, final_norm=final_norm)
    moments_m = dict(ffn1_norm=m_ffn1_norm, ffn1_w_gate=m_ffn1_w_gate, ffn1_w_up=m_ffn1_w_up, ffn1_w_down=m_ffn1_w_down,
                     mix_norm=m_mix_norm, w_in=m_w_in, conv_w=m_conv_w, attn_sinks=m_attn_sinks, w_out=m_w_out,
                     ffn2_norm=m_ffn2_norm, ffn2_w_gate=m_ffn2_w_gate, ffn2_w_up=m_ffn2_w_up, ffn2_w_down=m_ffn2_w_down,
                     final_norm=m_final_norm)
    moments_v = dict(ffn1_norm=v_ffn1_norm, ffn1_w_gate=v_ffn1_w_gate, ffn1_w_up=v_ffn1_w_up, ffn1_w_down=v_ffn1_w_down,
                     mix_norm=v_mix_norm, w_in=v_w_in, conv_w=v_conv_w, attn_sinks=v_attn_sinks, w_out=v_w_out,
                     ffn2_norm=v_ffn2_norm, ffn2_w_gate=v_ffn2_w_gate, ffn2_w_up=v_ffn2_w_up, ffn2_w_down=v_ffn2_w_down,
                     final_norm=v_final_norm)

    xs = x[0]
    target = loss_target[0]
    final_gain = final_norm.reshape(1, D_MODEL)

    def ffn_shards(wg, wu, wd):
        return jnp.stack([wg[0].T, wu[0].T]).astype(BF16), wd[0].astype(BF16)

    conv_cols = conv_w.shape[2]
    conv_shard = jnp.pad(conv_w[0], ((0, 5), (0, 128 - conv_cols)))
    gate_up1, down1 = ffn_shards(ffn1_w_gate, ffn1_w_up, ffn1_w_down)
    rest_shards = [down1, w_in[0].T.astype(BF16), w_out[0].astype(BF16), conv_shard,
                   *ffn_shards(ffn2_w_gate, ffn2_w_up, ffn2_w_down)]
    rest_rows = [s.shape[-2] for s in rest_shards]
    n_rest, n_early = len(rest_shards), 4
    (w1_gu,), _ = _all_gather_rows([gate_up1], name="gather_ffn1")

    fulls = [lax.empty(s.shape[:-2] + (N_DEV * s.shape[-2], s.shape[-1]), s.dtype) for s in rest_shards]
    fulls = _place_own(fulls, rest_shards, my_index.astype(jnp.int32).reshape(1), name="place_own_weights")
    ssem, rsem, bufs, token = _split_start(rest_shards + list(fulls), 4 * n_rest, _gather_send_plan(n_rest),
                                           name="gather_rest_start", after=[w1_gu])
    early = bufs[:n_early] + bufs[n_rest:n_rest + n_early]
    late = bufs[n_early:n_rest] + bufs[n_rest + n_early:]
    h1, s1, sa1, sb1 = _ffn_fwd(xs, ffn1_norm, w1_gu, name="ffn1_hidden", after=[token])
    early = _split_wait(ssem, rsem, early, h1, _gather_send_plan(n_early), name="gather_early_wait")
    fwd_early = _gather_forward_plan(rest_rows[:n_early])
    ssem_e, rsem_e, parts, token = _split_start(early[n_early:], 3 * n_early, fwd_early, name="forward_early_start")
    rope = _rope_tables(xs.shape[0], after=[token])
    w1_d, win_t, wout, conv_all = _split_wait(ssem_e, rsem_e, parts, rope, fwd_early, name="forward_early_wait")
    x1 = _ffn_down(xs, s1, w1_d, name="ffn1_down")
    conv_full = conv_all.reshape(N_DEV, 8, 128)[:, :3, :conv_cols].transpose(1, 0, 2).reshape(3, CONV_W)
    late = _split_wait(ssem, rsem, late, x1, _gather_send_plan(n_rest - n_early, first=n_early),
                       name="gather_late_wait")
    fwd_ffn2 = _gather_forward_plan(rest_rows[n_early:])
    ssem, rsem, parts, token = _split_start(late[n_rest - n_early:], 3 * (n_rest - n_early), fwd_ffn2,
                                            name="forward_ffn2_start")
    x2, hm, z, y = _mixer_fwd(x1, mix_norm, win_t, wout, conv_full, attn_sinks, rope, name="mixer_fwd", after=[token])
    w2_gu, w2_d = _split_wait(ssem, rsem, parts, x2, fwd_ffn2, name="forward_ffn2_wait")
    dx3, h2, s2, sa2, sb2, loss_local, d_final = _ffn_fwd(x2, ffn2_norm, w2_gu, w2_d, head=(final_gain, target),
                                                          name="ffn2_fwd")

    def to_sibling_start(grads, tag, after=()):
        views = [g.reshape(4, 2, g.shape[0] // N_DEV, D_MODEL) for g in grads]
        lands = [lax.empty((4,) + v.shape[2:], F32) for v in views]
        plan = _sibling_plan(len(views))
        ssem, rsem, bufs, token = _split_start(views + lands, len(views), plan, name=f"{tag}_sibling_start", after=after)
        return (ssem, rsem, bufs, plan, tag), token

    def to_sibling_finish(handle, after, names):
        ssem, rsem, bufs, plan, tag = handle
        bufs = _split_wait(ssem, rsem, bufs, after, plan, name=f"{tag}_sibling_wait")
        n = len(names)
        return _add_sibling(bufs[:n], bufs[n:], place, name=f"add_sibling_{tag}")

    def to_chips_start(partials, tag, small_all=None, after=()):
        p16 = [p for _, p in partials]
        lands = [lax.empty((3,) + p.shape[1:], BF16) for p in p16]
        extra = [] if small_all is None else [small_all]
        plan = _chips_plan(len(p16), small_all is not None)
        ssem, rsem, bufs, token = _split_start(p16 + lands + extra, 3 * len(p16) + 7 * len(extra), plan,
                                               name=f"{tag}_chips_start", after=after)
        return (ssem, rsem, bufs, plan, tag), token

    def to_chips_finish(handle, partials, after, names):
        ssem, rsem, bufs, plan, tag = handle
        bufs = _split_wait(ssem, rsem, bufs, after, plan, name=f"{tag}_chips_wait")
        n = len(names)
        return [(p32, r) for (p32, _), r in zip(partials, bufs[n:2 * n])], bufs[2 * n:]

    half_ff = D_FF // 2
    names2, namesm = ["ffn2_w_gate", "ffn2_w_up", "ffn2_w_down"], ["w_in", "w_out"]
    transposed = {"ffn1_w_gate", "ffn1_w_up", "w_in", "ffn2_w_gate", "ffn2_w_up"}
    grad, delta, new_m, new_v = {}, {}, {}, {}

    def adam_big(names, parts, tag, after=()):
        def to_rows(nm, a):
            return jnp.swapaxes(a, 1, 2)[0] if nm in transposed else a[0]

        def from_rows(nm, a):
            return jnp.swapaxes(a[None], 1, 2) if nm in transposed else a[None]

        operands = [(p32, recv, to_rows(nm, given[nm]), to_rows(nm, moments_m[nm]), to_rows(nm, moments_v[nm]))
                    for nm, (p32, recv) in zip(names, parts)]
        results, token = _reduce_adamw(operands, name=f"adamw_{tag}", after=after)
        for nm, outs in zip(names, results):
            grad[nm], delta[nm], new_m[nm], new_v[nm] = (from_rows(nm, a) for a in outs)
        return token

    dx2, da2, db2, g2b, d_norm2 = _ffn_dgrad(dx3, x2, ffn2_norm, sa2, sb2, w2_gu, w2_d, name="ffn2_dgrad")
    gw2 = [_tn_matmul(da2, h2, name="ffn2_wgrad_gate", bm=half_ff), _tn_matmul(db2, h2, name="ffn2_wgrad_up", bm=half_ff),
           _tn_matmul(s2, g2b, name="ffn2_wgrad_down", bm=half_ff)]
    sib2, tok = to_sibling_start(gw2, "ffn2")
    dx1, dz, gmb, d_conv, d_sink, d_normm = _mixer_bwd(dx2, x1, mix_norm, y, z, win_t, wout, conv_full, attn_sinks,
                                                       rope, name="mixer_bwd", after=[tok])
    p2 = to_sibling_finish(sib2, dx1, names2)
    chips2, tok = to_chips_start(p2, "ffn2")
    gwm = [_tn_matmul(dz, hm, name="mixer_wgrad_in", bm=Z_W // 2, after=[tok]),
           _tn_matmul(y, gmb, name="mixer_wgrad_out", bm=D_MODEL, after=[tok])]
    sibm, tok = to_sibling_start(gwm, "mixer")
    dx0, da1, db1, g1b, d_norm1 = _ffn_dgrad(dx1, xs, ffn1_norm, sa1, sb1, w1_gu, w1_d, name="ffn1_dgrad", after=[tok])
    r2, _ = to_chips_finish(chips2, p2, dx0, names2)
    pm = to_sibling_finish(sibm, dx0, namesm)
    chipsm, tok = to_chips_start(pm, "mixer")
    gw_gate = _tn_matmul(da1, h1, name="ffn1_wgrad_gate", bm=half_ff, after=[tok])
    sib_gate, tok = to_sibling_start([gw_gate], "ffn1_gate")
    gw_up = _tn_matmul(db1, h1, name="ffn1_wgrad_up", bm=half_ff, after=[tok])
    rm, _ = to_chips_finish(chipsm, pm, gw_up, namesm)
    p_gate = to_sibling_finish(sib_gate, gw_up, ["ffn1_w_gate"])
    chips_gate, tok_a = to_chips_start(p_gate, "ffn1_gate")
    sib_up, tok_b = to_sibling_start([gw_up], "ffn1_up", after=[tok_a])
    gw_down = _tn_matmul(s1, g1b, name="ffn1_wgrad_down_first", bm=half_ff, blocks=(0, 1), after=[tok_a, tok_b])
    p_up = to_sibling_finish(sib_up, gw_down, ["ffn1_w_up"])
    chips_up, tok_a = to_chips_start(p_up, "ffn1_up")
    gw_down = _tn_matmul(s1, g1b, name="ffn1_wgrad_down_second", bm=half_ff, blocks=(1, 1), into=gw_down, after=[tok_a])
    sib_down, tok_b = to_sibling_start([gw_down], "ffn1_down")
    p_down = to_sibling_finish(sib_down, tok_b, ["ffn1_w_down"])
    last_row = (jnp.pad(d_sink, ((0, 0), (0, D_MODEL - 128)))
                + jnp.pad(loss_local, ((0, 0), (LOSS_LANE, D_MODEL - LOSS_LANE - 1))))
    small = jnp.concatenate([
        d_norm1, d_normm, d_norm2, d_final, jnp.pad(d_conv[0:3], ((0, 0), (0, D_MODEL - CONV_W))), last_row], axis=0)
    (small_all,) = _place_own([lax.empty((N_DEV * 8, D_MODEL), F32)], [small], my_index.astype(jnp.int32).reshape(1),
                              name="place_own_small")
    chips_down, tok = to_chips_start(p_down, "ffn1_down", small_all)
    tok = adam_big(names2, r2, "ffn2", after=[tok])
    tok = adam_big(namesm, rm, "mixer", after=[tok])
    r_gate, _ = to_chips_finish(chips_gate, p_gate, tok, ["ffn1_w_gate"])
    tok = adam_big(["ffn1_w_gate"], r_gate, "ffn1_gate")
    r_up, _ = to_chips_finish(chips_up, p_up, tok, ["ffn1_w_up"])
    tok = adam_big(["ffn1_w_up"], r_up, "ffn1_up")
    r_down, (small_all,) = to_chips_finish(chips_down, p_down, tok, ["ffn1_w_down"])
    adam_big(["ffn1_w_down"], r_down, "ffn1_down")
    results, loss = _update_small(given, moments_m, moments_v, small_all.reshape(N_DEV, 8, D_MODEL), my_index,
                                  name="update_small")
    for nm, outs in results.items():
        grad[nm], delta[nm], new_m[nm], new_v[nm] = outs

    order = list(given)
    return (loss.reshape(()), dx0[None], *[grad[n] for n in order], *[delta[n] for n in order],
            *[new_m[n] for n in order], *[new_v[n] for n in order])
```

```python
import functools

import jax
import jax.numpy as jnp
from jax import lax
from jax.experimental import pallas as pl
from jax.experimental.pallas import tpu as pltpu

F32 = jnp.float32
BF16 = jnp.bfloat16
MESH = pl.DeviceIdType.MESH
ANY = pl.BlockSpec(memory_space=pl.ANY)
HBM_SPEC = pl.BlockSpec(memory_space=pltpu.HBM)
SEM_SPEC = pl.BlockSpec(memory_space=pltpu.SEMAPHORE)
DATAFLOW = pltpu.SideEffectType.DATAFLOW_SIDE_EFFECTING

N_DEV = 8
LOSS_LANE = 128
D_MODEL = 1024
D_FF = 2816
CONV_W = 512
ATTN_W = 512
KV_W = 128
HEAD_DIM = 64
N_Q_HEADS = 8
N_KV_HEADS = 2
Q_PER_KV = N_Q_HEADS // N_KV_HEADS
BLOCK = 128
ROT_DIM = 16
ROPE_THETA = 500000.0
Z_W = 3 * CONV_W + ATTN_W + 2 * KV_W
Q_OFF = 3 * CONV_W
K_OFF = Q_OFF + ATTN_W
V_OFF = K_OFF + KV_W
RMS_EPS = 1e-5
MASK_VALUE = -1e30
SM_SCALE = HEAD_DIM ** -0.5
FFN_RES_SCALE = 0.5

ADAM_LR = 0.001
ADAM_B1 = 0.9
ADAM_B2 = 0.999
ADAM_EPS = 1e-08
ADAM_WD = 0.01
ADAM_STEP = 10

NT_DIMS = (((1,), (1,)), ((), ()))
TN_DIMS = (((0,), (0,)), ((), ()))

VMEM_LIMIT = 62 * 1024 * 1024
FF_CHUNK = 256


def _params(sem, vmem=None):
    return pltpu.CompilerParams(dimension_semantics=sem, vmem_limit_bytes=vmem)


def _behind(body, n_in, after):
    k = len(after)
    if k == 0:
        return body
    return lambda *refs: body(*refs[:n_in], *refs[n_in + k:])


def _rms_stats(xf):
    inv = lax.rsqrt(jnp.mean(xf * xf, axis=-1, keepdims=True) + RMS_EPS)
    return xf * inv, inv


def _rms_bwd(dh, xhat, inv, gain):
    dxhat = dh * gain
    dx = inv * (dxhat - xhat * jnp.mean(dxhat * xhat, axis=-1, keepdims=True))
    dgain = jnp.sum(dh * xhat, axis=0, keepdims=True)
    return dx, dgain


def _load_resident(w_hbm, w_ref, sem):
    @pl.when(pl.program_id(0) == 0)
    def _():
        cp = pltpu.make_async_copy(w_hbm, w_ref, sem)
        cp.start()
        cp.wait()


def _ffn_fwd(x, gain, w_gu, w_d=None, *, name, head=None, after=(), tm=256, tf=FF_CHUNK):
    t = x.shape[0]
    tm = min(tm, t)
    n_tiles = t // tm
    assert (w_d is None) == (head is None)
    n_in = 3 if head is None else 6
    first = 0 if head is None else 1

    def body(*refs):
        x_ref, g_ref, wgu_hbm = refs[:3]
        h_ref, s_ref, sa_ref, sb_ref = refs[n_in + first:n_in + first + 4]
        scratch = refs[n_in + 3 * first + 4:]
        wgu_ref, sems = scratch[0], scratch[-1]
        i = pl.program_id(0)
        _load_resident(wgu_hbm, wgu_ref, sems.at[0])

        def hidden(midway=None):
            h = (_rms_stats(x_ref[...])[0] * g_ref[...]).astype(BF16)
            h_ref[...] = h
            for c in range(0, D_FF, tf):
                if midway is not None and c == D_FF // tf // 2 * tf:
                    midway()
                cols = slice(c, min(c + tf, D_FF))
                a = lax.dot_general(h, wgu_ref[0, cols, :], NT_DIMS, preferred_element_type=F32)
                b = lax.dot_general(h, wgu_ref[1, cols, :], NT_DIMS, preferred_element_type=F32)
                sig = jax.nn.sigmoid(a)
                silu = a * sig
                s_ref[:, cols] = (silu * b).astype(BF16)
                sa_ref[:, cols] = (b * (sig * (1.0 + a * (1.0 - sig)))).astype(BF16)
                sb_ref[:, cols] = silu.astype(BF16)

        if head is None:
            hidden()
            return

        wd_hbm, fg_ref, t_ref = refs[3:n_in]
        dxo_ref, loss_ref, dfg_ref = refs[n_in], refs[n_in + 5], refs[n_in + 6]
        wd_ref, xo_ref = scratch[1], scratch[2]
        _load_resident(wd_hbm, wd_ref, sems.at[1])

        @pl.when(i == 0)
        def _():
            loss_ref[...] = jnp.zeros_like(loss_ref)
            dfg_ref[...] = jnp.zeros_like(dfg_ref)
            xo_ref[...] = jnp.zeros_like(xo_ref)

        def head_of_previous_tile():
            xhat_o, inv_o = _rms_stats(xo_ref[...])
            err = xhat_o * fg_ref[...] - t_ref[...]
            loss = 0.5 * jnp.sum(jnp.mean(err * err, axis=-1, keepdims=True), axis=0, keepdims=True)
            dxo_ref[...], dfg = _rms_bwd(err * (1.0 / D_MODEL), xhat_o, inv_o, fg_ref[...])
            loss_ref[...] += jnp.where(i > 0, loss, 0.0)
            dfg_ref[...] += jnp.where(i > 0, dfg, 0.0)

        @pl.when(i < n_tiles)
        def _():
            hidden(midway=head_of_previous_tile)
            xo_ref[...] = x_ref[...] + FFN_RES_SCALE * jnp.dot(s_ref[...], wd_ref[...], preferred_element_type=F32)

        @pl.when(i == n_tiles)
        def _():
            head_of_previous_tile()

    this = lambda i: (jnp.minimum(i, n_tiles - 1), 0)
    previous = lambda i: (jnp.maximum(i - 1, 0), 0)
    row = pl.BlockSpec((tm, D_MODEL), this)
    hid = pl.BlockSpec((tm, D_FF), this)
    vec = pl.BlockSpec((1, D_MODEL), lambda i: (0, 0))
    late_row = pl.BlockSpec((tm, D_MODEL), previous)
    hidden_shapes = [jax.ShapeDtypeStruct((t, D_MODEL), BF16)] + [jax.ShapeDtypeStruct((t, D_FF), BF16)] * 3
    if head is None:
        extra_in, extra_out, extra_shapes, extra_scratch, steps = [], [], [], [], n_tiles
    else:
        extra_in = [ANY, vec, late_row]
        extra_out, extra_shapes = [late_row], [jax.ShapeDtypeStruct((t, D_MODEL), F32)]
        extra_scratch = [pltpu.VMEM((D_FF, D_MODEL), BF16), pltpu.VMEM((tm, D_MODEL), F32)]
        steps = n_tiles + 1
    tail_out = [] if head is None else [pl.BlockSpec((1, 1), lambda i: (0, 0)), vec]
    tail_shapes = [] if head is None else [jax.ShapeDtypeStruct((1, 1), F32), jax.ShapeDtypeStruct((1, D_MODEL), F32)]
    return pl.pallas_call(
        _behind(body, n_in, after), name=name, grid=(steps,),
        in_specs=[row, vec, ANY] + extra_in + [ANY] * len(after),
        out_specs=extra_out + [row, hid, hid, hid] + tail_out,
        out_shape=extra_shapes + hidden_shapes + tail_shapes,
        scratch_shapes=[pltpu.VMEM((2, D_FF, D_MODEL), BF16)] + extra_scratch + [pltpu.SemaphoreType.DMA((2,))],
        compiler_params=_params(("arbitrary",), VMEM_LIMIT),
    )(x, gain, w_gu, *([] if head is None else [w_d, *head]), *after)


def _ffn_down(x, s, w_d, *, name, after=(), tm=512):
    t = x.shape[0]
    tm = min(tm, t)

    def body(x_ref, s_ref, w_hbm, xo_ref, w_ref, sem):
        _load_resident(w_hbm, w_ref, sem)
        xo_ref[...] = x_ref[...] + FFN_RES_SCALE * jnp.dot(s_ref[...], w_ref[...], preferred_element_type=F32)

    row = pl.BlockSpec((tm, D_MODEL), lambda i: (i, 0))
    return pl.pallas_call(
        _behind(body, 3, after), name=name, grid=(t // tm,),
        in_specs=[row, pl.BlockSpec((tm, D_FF), lambda i: (i, 0)), ANY] + [ANY] * len(after), out_specs=row,
        out_shape=jax.ShapeDtypeStruct((t, D_MODEL), F32),
        scratch_shapes=[pltpu.VMEM((D_FF, D_MODEL), BF16), pltpu.SemaphoreType.DMA(())],
        compiler_params=_params(("arbitrary",), VMEM_LIMIT),
    )(x, s, w_d, *after)


def _ffn_dgrad(dxo, x, gain, sa, sb, w_gu, w_d, *, name, after=(), tm=512, sub=512, tf=FF_CHUNK):
    t = x.shape[0]
    tm = min(tm, t)
    sub = min(sub, tm)

    def body(dxo_ref, x_ref, g_ref, sa_ref, sb_ref, wgu_hbm, wd_hbm, dxi_ref, da_ref, db_ref, gb_ref, dg_ref,
             wgu_ref, wd_ref, sems):
        _load_resident(wgu_hbm, wgu_ref, sems.at[0])
        _load_resident(wd_hbm, wd_ref, sems.at[1])

        @pl.when(pl.program_id(0) == 0)
        def _():
            dg_ref[...] = jnp.zeros_like(dg_ref)

        for r0 in range(0, tm, sub):
            rows = slice(r0, r0 + sub)
            go = dxo_ref[rows, :]
            gb = (FFN_RES_SCALE * go).astype(BF16)
            gb_ref[rows, :] = gb
            for c in range(0, D_FF, tf):
                cols = slice(c, min(c + tf, D_FF))
                ds = lax.dot_general(gb, wd_ref[cols, :], NT_DIMS, preferred_element_type=F32)
                da_ref[rows, cols] = (ds * sa_ref[rows, cols].astype(F32)).astype(BF16)
                db_ref[rows, cols] = (ds * sb_ref[rows, cols].astype(F32)).astype(BF16)
            dh = (jnp.dot(da_ref[rows, :], wgu_ref[0], preferred_element_type=F32)
                  + jnp.dot(db_ref[rows, :], wgu_ref[1], preferred_element_type=F32))
            xhat, inv = _rms_stats(x_ref[rows, :])
            dx, dgain = _rms_bwd(dh, xhat, inv, g_ref[...])
            dxi_ref[rows, :] = go + dx
            dg_ref[...] += dgain

    row = pl.BlockSpec((tm, D_MODEL), lambda i: (i, 0))
    hid = pl.BlockSpec((tm, D_FF), lambda i: (i, 0))
    vec = pl.BlockSpec((1, D_MODEL), lambda i: (0, 0))
    return pl.pallas_call(
        _behind(body, 7, after), name=name, grid=(t // tm,),
        in_specs=[row, row, vec, hid, hid, ANY, ANY] + [ANY] * len(after),
        out_specs=[row, hid, hid, row, vec],
        out_shape=[jax.ShapeDtypeStruct((t, D_MODEL), F32), jax.ShapeDtypeStruct((t, D_FF), BF16),
                   jax.ShapeDtypeStruct((t, D_FF), BF16),
                   jax.ShapeDtypeStruct((t, D_MODEL), BF16), jax.ShapeDtypeStruct((1, D_MODEL), F32)],
        scratch_shapes=[pltpu.VMEM((2, D_FF, D_MODEL), BF16), pltpu.VMEM((D_FF, D_MODEL), BF16),
                        pltpu.SemaphoreType.DMA((2,))],
        compiler_params=_params(("arbitrary",), VMEM_LIMIT),
    )(dxo, x, gain, sa, sb, w_gu, w_d, *after)


def _tn_matmul(a, b, *, name, bm, after=(), tk=2048, blocks=None, into=None):
    t, m = a.shape
    n = b.shape[1]
    tk = min(tk, t)
    nk = t // tk
    first, count = blocks or (0, m // bm)
    behind = ([] if into is None else [into]) + list(after)

    def body(a_ref, b_ref, o_ref):
        @pl.when(pl.program_id(1) == 0)
        def _():
            o_ref[...] = jnp.zeros_like(o_ref)

        o_ref[...] += lax.dot_general(a_ref[...], b_ref[...], TN_DIMS, preferred_element_type=F32)

    return pl.pallas_call(
        _behind(body, 2, behind), name=name, grid=(count, nk),
        in_specs=[pl.BlockSpec((tk, bm), lambda i, k: (k, first + i)), pl.BlockSpec((tk, n), lambda i, k: (k, 0))]
        + [ANY] * len(behind),
        out_specs=pl.BlockSpec((bm, n), lambda i, k: (first + i, 0)),
        out_shape=jax.ShapeDtypeStruct((m, n), F32),
        input_output_aliases={} if into is None else {2: 0},
        compiler_params=_params(("parallel", "arbitrary"), VMEM_LIMIT),
    )(a, b, *behind)


def _rope_tables(t, after=()):
    inv_freq = ROPE_THETA ** (-jnp.arange(0, ROT_DIM, 2, dtype=F32) / ROT_DIM)
    ang = inv_freq[:, None] * jnp.arange(t, dtype=F32)[None, :]
    compact = jnp.stack([jnp.cos(ang), jnp.sin(ang)])
    tr = min(1024, t)

    def body(c_ref, o_ref):
        for k in range(2):
            o_ref[k] = jnp.tile(c_ref[k], (128 // inv_freq.shape[0], 1)).T

    return pl.pallas_call(
        _behind(body, 1, after), name="rope_tables", grid=(t // tr,),
        in_specs=[pl.BlockSpec((2, inv_freq.shape[0], tr), lambda i: (0, 0, i))] + [ANY] * len(after),
        out_specs=pl.BlockSpec((2, tr, 128), lambda i: (0, i, 0)),
        out_shape=jax.ShapeDtypeStruct((2, t, 128), F32), compiler_params=_params(("parallel",)),
    )(compact, *after)


def _rope_multipliers(cos_sin):
    half = ROT_DIM // 2
    cos, sin = cos_sin[0], cos_sin[1]
    d = lax.broadcasted_iota(jnp.int32, cos.shape, 1) & (HEAD_DIM - 1)
    mult = jnp.where(d < ROT_DIM, cos, 1.0)
    from_lo = jnp.where((d >= half) & (d < ROT_DIM), sin, 0.0)
    from_hi = jnp.where(d < half, -sin, 0.0)
    return mult, from_lo, from_hi


def _tile_lanes(tab, width):
    return jnp.tile(tab, (1, width // tab.shape[1]))


def _rope(v, tab):
    w = v.shape[1]
    half_rot = ROT_DIM // 2
    return (v * _tile_lanes(tab[0], w)
            + pltpu.roll(v, half_rot, axis=1) * _tile_lanes(tab[1], w)
            + pltpu.roll(v, w - half_rot, axis=1) * _tile_lanes(tab[2], w))


def _rope_bwd(dv, tab):
    w = dv.shape[1]
    half_rot = ROT_DIM // 2
    return (dv * _tile_lanes(tab[0], w)
            + pltpu.roll(dv * _tile_lanes(tab[1], w), w - half_rot, axis=1)
            + pltpu.roll(dv * _tile_lanes(tab[2], w), half_rot, axis=1))


def _shift_rows(v, prev8_ref, n):
    r = lax.broadcasted_iota(jnp.int32, v.shape, 0)
    rolled = pltpu.roll(v, n, axis=0)
    last = prev8_ref[7:8, :]
    if n == 1:
        return jnp.where(r >= 1, rolled, last)
    return jnp.where(r >= 2, rolled, jnp.where(r == 0, prev8_ref[6:7, :], last))


def _shift_rows_up(v, next8_ref, n):
    rows = v.shape[0]
    r = lax.broadcasted_iota(jnp.int32, v.shape, 0)
    rolled = pltpu.roll(v, rows - n, axis=0)
    first = next8_ref[0:1, :]
    if n == 1:
        return jnp.where(r <= rows - 2, rolled, first)
    return jnp.where(r <= rows - 3, rolled, jnp.where(r == rows - 2, first, next8_ref[1:2, :]))


def _lane_half_mask(shape, half):
    lane = lax.broadcasted_iota(jnp.int32, shape, 1)
    return (lane >= HEAD_DIM) if half else (lane < HEAD_DIM)


def _to_kv_lanes(chunk, head, kv):
    if head % 2 != kv:
        chunk = pltpu.roll(chunk, HEAD_DIM, axis=1)
    return jnp.where(_lane_half_mask(chunk.shape, kv), chunk, 0.0)


def _from_kv_lanes(chunk, head, kv):
    chunk = jnp.where(_lane_half_mask(chunk.shape, kv), chunk, 0.0)
    if head % 2 != kv:
        chunk = pltpu.roll(chunk, HEAD_DIM, axis=1)
    return chunk


def _stack_heads(wide):
    parts = []
    for head in range(N_Q_HEADS):
        chunk = wide[:, (head // 2) * 128:(head // 2 + 1) * 128]
        parts.append(_to_kv_lanes(chunk, head, head // Q_PER_KV))
    return jnp.concatenate(parts, axis=0)


def _window_mask(has_prev):
    shape = (N_Q_HEADS * BLOCK, 2 * BLOCK)
    qi = lax.broadcasted_iota(jnp.int32, shape, 0) & (BLOCK - 1)
    kj = lax.broadcasted_iota(jnp.int32, shape, 1)
    first_key = BLOCK - has_prev * BLOCK
    in_prev = (kj < BLOCK) & (kj > qi) & (kj >= first_key)
    in_own = (kj >= BLOCK) & ((kj - BLOCK) <= qi)
    return in_prev | in_own


def _sink_column(sink_ref):
    row = lax.broadcasted_iota(jnp.int32, (N_Q_HEADS * BLOCK, 1), 0)
    col = jnp.full((N_Q_HEADS * BLOCK, 1), sink_ref[0, 0], F32)
    for head in range(1, N_Q_HEADS):
        col = jnp.where(row >= head * BLOCK, sink_ref[0, head], col)
    return col


def _softmax_with_sink(q4, k2, mask, sink):
    s = lax.dot_general(q4, k2, NT_DIMS, preferred_element_type=F32) * SM_SCALE
    s = jnp.where(mask, s, MASK_VALUE)
    m = jnp.maximum(jnp.max(s, axis=-1, keepdims=True), sink)
    p = jnp.exp(s - m)
    e_sink = jnp.exp(sink - m)
    inv_den = 1.0 / (jnp.sum(p, axis=-1, keepdims=True) + e_sink)
    return p * inv_den, e_sink * inv_den


def _conv_terms(zf, prev8_ref, w_ref):
    b_gate, c_gate, u = zf[:, 0:CONV_W], zf[:, CONV_W:2 * CONV_W], zf[:, 2 * CONV_W:3 * CONV_W]
    vc = c_gate * u
    vm1 = _shift_rows(vc, prev8_ref, 1)
    vm2 = _shift_rows(vc, prev8_ref, 2)
    conv = w_ref[0:1, :] * vm2 + w_ref[1:2, :] * vm1 + w_ref[2:3, :] * vc
    return b_gate, c_gate, u, vc, vm1, vm2, conv


def _mixer_fwd(x, gain, win_t, wout, conv_w, sinks, rope, *, name, after=(), tq=512):
    t = x.shape[0]
    tq = min(tq, t)
    nblk = tq // BLOCK

    def body(x_ref, g_ref, win_hbm, wout_hbm, cw_ref, sink_ref, rope_ref,
             xo_ref, h_ref, z_ref, y_ref, kprev_ref, vprev_ref, cprev_ref, win_ref, wout_ref, sems):
        i = pl.program_id(0)
        _load_resident(win_hbm, win_ref, sems.at[0])
        _load_resident(wout_hbm, wout_ref, sems.at[1])

        @pl.when(i == 0)
        def _():
            kprev_ref[...] = jnp.zeros_like(kprev_ref)
            vprev_ref[...] = jnp.zeros_like(vprev_ref)
            cprev_ref[...] = jnp.zeros_like(cprev_ref)

        xf = x_ref[...]
        xhat, _ = _rms_stats(xf)
        h = (xhat * g_ref[...]).astype(BF16)
        h_ref[...] = h
        def project(c0, c1):
            zc = lax.dot_general(h, win_ref[c0:c1, :], NT_DIMS, preferred_element_type=F32).astype(BF16)
            z_ref[:, c0:c1] = zc
            return zc

        zb = project(Q_OFF, Z_W)
        zf = zb.astype(F32)
        tab = _rope_multipliers(rope_ref[...])
        qr = _rope(zf[:, 0:ATTN_W], tab)
        kr = _rope(zf[:, K_OFF - Q_OFF:V_OFF - Q_OFF], tab).astype(BF16)
        vb = zb[:, V_OFF - Q_OFF:Z_W - Q_OFF]
        conv_cols = [(c, c + CONV_W) for c in range(0, Q_OFF, CONV_W)]
        conv_z = []

        y_attn = []
        for j in range(nblk):
            if len(conv_z) < len(conv_cols):
                conv_z.append(project(*conv_cols[len(conv_z)]))
            rows = slice(j * BLOCK, (j + 1) * BLOCK)
            prev = slice((j - 1) * BLOCK, j * BLOCK)
            k2 = jnp.concatenate([kprev_ref[...] if j == 0 else kr[prev], kr[rows]], axis=0)
            v2 = jnp.concatenate([vprev_ref[...] if j == 0 else vb[prev], vb[rows]], axis=0)
            mask = _window_mask(jnp.minimum(i, 1) if j == 0 else 1)
            q8 = _stack_heads(qr[rows]).astype(BF16)
            probs, _ = _softmax_with_sink(q8, k2, mask, _sink_column(sink_ref))
            o8 = jnp.dot(probs.astype(BF16), v2, preferred_element_type=F32)
            chunks = [jnp.zeros((BLOCK, 128), F32) for _ in range(ATTN_W // 128)]
            for head in range(N_Q_HEADS):
                chunks[head // 2] += _from_kv_lanes(o8[head * BLOCK:(head + 1) * BLOCK], head, head // Q_PER_KV)
            y_attn.append(jnp.concatenate(chunks, axis=1))
        kprev_ref[...] = kr[tq - BLOCK:tq]
        vprev_ref[...] = vb[tq - BLOCK:tq]
        while len(conv_z) < len(conv_cols):
            conv_z.append(project(*conv_cols[len(conv_z)]))
        ya = jnp.concatenate(y_attn, axis=0).astype(BF16)
        y_ref[:, CONV_W:] = ya
        xo = xf + jnp.dot(ya, wout_ref[CONV_W:, :], preferred_element_type=F32)
        b_gate, _, _, vc, _, _, conv = _conv_terms(jnp.concatenate(conv_z, axis=1).astype(F32), cprev_ref, cw_ref)
        yc = (b_gate * conv).astype(BF16)
        cprev_ref[...] = vc[tq - 8:tq, :]
        y_ref[:, :CONV_W] = yc
        xo_ref[...] = xo + jnp.dot(yc, wout_ref[:CONV_W, :], preferred_element_type=F32)

    row = pl.BlockSpec((tq, D_MODEL), lambda i: (i, 0))
    full = lambda shape: pl.BlockSpec(shape, lambda i: (0,) * len(shape))
    return pl.pallas_call(
        _behind(body, 7, after), name=name, grid=(t // tq,),
        in_specs=[row, full((1, D_MODEL)), ANY, ANY, full((3, CONV_W)),
                  pl.BlockSpec(memory_space=pltpu.SMEM), pl.BlockSpec((2, tq, 128), lambda i: (0, i, 0))]
        + [ANY] * len(after),
        out_specs=[row, row, pl.BlockSpec((tq, Z_W), lambda i: (i, 0)), row],
        out_shape=[jax.ShapeDtypeStruct((t, D_MODEL), F32), jax.ShapeDtypeStruct((t, D_MODEL), BF16),
                   jax.ShapeDtypeStruct((t, Z_W), BF16), jax.ShapeDtypeStruct((t, D_MODEL), BF16)],
        scratch_shapes=[pltpu.VMEM((BLOCK, KV_W), BF16), pltpu.VMEM((BLOCK, KV_W), BF16),
                        pltpu.VMEM((8, CONV_W), F32), pltpu.VMEM((Z_W, D_MODEL), BF16),
                        pltpu.VMEM((D_MODEL, D_MODEL), BF16), pltpu.SemaphoreType.DMA((2,))],
        compiler_params=_params(("arbitrary",), VMEM_LIMIT),
    )(x, gain, win_t, wout, conv_w, sinks, rope, *after)


def _mixer_bwd(dxo, x, gain, y, z, win_t, wout, conv_w, sinks, rope, *, name, after=(), tq=256):
    t = x.shape[0]
    tq = min(tq, t)
    nt, nblk = t // tq, tq // BLOCK

    def body(dxo_ref, x_ref, g_ref, y_ref, z_ref, zp_ref, win_hbm, wout_hbm, cw_ref, sink_ref, rope_ref, ropep_ref,
             dxi_ref, dz_ref, gb_ref, dcw_ref, dsink_ref, dg_ref, dk_ref, dv_ref, dcn_ref, pvc_ref,
             win_ref, wout_ref, sems):
        i = pl.program_id(0)
        tile = nt - 1 - i
        _load_resident(win_hbm, win_ref, sems.at[0])
        _load_resident(wout_hbm, wout_ref, sems.at[1])

        @pl.when(i == 0)
        def _():
            dk_ref[...] = jnp.zeros_like(dk_ref)
            dv_ref[...] = jnp.zeros_like(dv_ref)
            dcn_ref[...] = jnp.zeros_like(dcn_ref)
            dcw_ref[...] = jnp.zeros_like(dcw_ref)
            dsink_ref[...] = jnp.zeros_like(dsink_ref)
            dg_ref[...] = jnp.zeros_like(dg_ref)

        has_prev = jnp.minimum(tile, 1)
        go = dxo_ref[...]
        gb = go.astype(BF16)
        gb_ref[...] = gb
        dy = lax.dot_general(gb, wout_ref[...], NT_DIMS, preferred_element_type=F32)
        dy_conv, dy_attn = dy[:, 0:CONV_W], dy[:, CONV_W:D_MODEL]
        zb, zpb = z_ref[...], zp_ref[...]
        zf = zb.astype(F32)
        zpf = zpb.astype(F32) * has_prev.astype(F32)

        pvc_ref[...] = (zpf[:, CONV_W:2 * CONV_W] * zpf[:, 2 * CONV_W:3 * CONV_W])[BLOCK - 8:BLOCK, :]
        b_gate, c_gate, u, vc, vm1, vm2, conv = _conv_terms(zf, pvc_ref, cw_ref)
        d_bgate = dy_conv * conv
        dc = dy_conv * b_gate
        tap = lax.broadcasted_iota(jnp.int32, (8, CONV_W), 0)
        dcw_ref[...] += jnp.where(tap == 0, jnp.sum(dc * vm2, axis=0, keepdims=True),
                                  jnp.where(tap == 1, jnp.sum(dc * vm1, axis=0, keepdims=True),
                                            jnp.where(tap == 2, jnp.sum(dc * vc, axis=0, keepdims=True), 0.0)))
        dvc = (cw_ref[2:3, :] * dc + cw_ref[1:2, :] * _shift_rows_up(dc, dcn_ref, 1)
               + cw_ref[0:1, :] * _shift_rows_up(dc, dcn_ref, 2))
        dcn_ref[...] = dc[0:8, :]
        d_cgate = dvc * u
        d_u = dvc * c_gate

        tab, tabp = _rope_multipliers(rope_ref[...]), _rope_multipliers(ropep_ref[...])
        qr = _rope(zf[:, Q_OFF:K_OFF], tab)
        kr = _rope(zf[:, K_OFF:V_OFF], tab).astype(BF16)
        kpr = _rope(zpf[:, K_OFF:V_OFF], tabp).astype(BF16)
        vb, vpb = zb[:, V_OFF:Z_W], zpb[:, V_OFF:Z_W]
        out = y_ref[:, CONV_W:D_MODEL].astype(F32)
        do_out = dy_attn * out
        lane = lax.broadcasted_iota(jnp.int32, (1, 128), 1)
        dsink = jnp.zeros((1, 128), F32)
        dk_next, dv_next = dk_ref[...], dv_ref[...]
        dq_rows, dk_rows, dv_rows = [None] * nblk, [None] * nblk, [None] * nblk
        for j in reversed(range(nblk)):
            rows = slice(j * BLOCK, (j + 1) * BLOCK)
            prev = slice((j - 1) * BLOCK, j * BLOCK)
            k2 = jnp.concatenate([kpr if j == 0 else kr[prev], kr[rows]], axis=0)
            v2 = jnp.concatenate([vpb if j == 0 else vb[prev], vb[rows]], axis=0)
            mask = _window_mask(has_prev if j == 0 else 1)
            q8 = _stack_heads(qr[rows]).astype(BF16)
            do8 = _stack_heads(dy_attn[rows]).astype(BF16)
            delta = jnp.sum(_stack_heads(do_out[rows]), axis=-1, keepdims=True)
            probs, p_sink = _softmax_with_sink(q8, k2, mask, _sink_column(sink_ref))
            dp = lax.dot_general(do8, v2, NT_DIMS, preferred_element_type=F32)
            ds = (probs * (dp - delta) * SM_SCALE).astype(BF16)
            dq8 = jnp.dot(ds, k2, preferred_element_type=F32)
            dk2 = lax.dot_general(ds, q8, TN_DIMS, preferred_element_type=F32)
            dv2 = lax.dot_general(probs.astype(BF16), do8, TN_DIMS, preferred_element_type=F32)
            sink_terms = p_sink * delta
            dq_chunks = [jnp.zeros((BLOCK, 128), F32) for _ in range(ATTN_W // 128)]
            for head in range(N_Q_HEADS):
                grp = slice(head * BLOCK, (head + 1) * BLOCK)
                dq_chunks[head // 2] += _from_kv_lanes(dq8[grp], head, head // Q_PER_KV)
                dsink = dsink - jnp.where(lane == head, jnp.sum(sink_terms[grp], axis=0, keepdims=True), 0.0)
            dq_rows[j] = jnp.concatenate(dq_chunks, axis=1)
            dk_rows[j] = dk2[BLOCK:] + dk_next
            dv_rows[j] = dv2[BLOCK:] + dv_next
            dk_next, dv_next = dk2[:BLOCK], dv2[:BLOCK]
        dk_ref[...] = dk_next
        dv_ref[...] = dv_next
        dsink_ref[...] += dsink
        dq = _rope_bwd(jnp.concatenate(dq_rows, axis=0), tab)
        dk = _rope_bwd(jnp.concatenate(dk_rows, axis=0), tab)
        dv = jnp.concatenate(dv_rows, axis=0)

        dzb = jnp.concatenate([d_bgate, d_cgate, d_u, dq, dk, dv], axis=1).astype(BF16)
        dz_ref[...] = dzb
        dh = jnp.dot(dzb, win_ref[...], preferred_element_type=F32)
        xhat, inv = _rms_stats(x_ref[...])
        dx, dgain = _rms_bwd(dh, xhat, inv, g_ref[...])
        dxi_ref[...] = go + dx
        dg_ref[...] += dgain

    rev = lambda i: (nt - 1 - i, 0)
    block_before = lambda i: jnp.maximum((nt - 1 - i) * nblk - 1, 0)
    row = pl.BlockSpec((tq, D_MODEL), rev)
    full = lambda shape: pl.BlockSpec(shape, lambda i: (0,) * len(shape))
    return pl.pallas_call(
        _behind(body, 12, after), name=name, grid=(nt,),
        in_specs=[row, row, full((1, D_MODEL)), row,
                  pl.BlockSpec((tq, Z_W), rev), pl.BlockSpec((BLOCK, Z_W), lambda i: (block_before(i), 0)),
                  ANY, ANY, full((3, CONV_W)),
                  pl.BlockSpec(memory_space=pltpu.SMEM),
                  pl.BlockSpec((2, tq, 128), lambda i: (0, nt - 1 - i, 0)),
                  pl.BlockSpec((2, BLOCK, 128), lambda i: (0, block_before(i), 0))] + [ANY] * len(after),
        out_specs=[row, pl.BlockSpec((tq, Z_W), rev), row, full((8, CONV_W)), full((1, 128)), full((1, D_MODEL))],
        out_shape=[jax.ShapeDtypeStruct((t, D_MODEL), F32), jax.ShapeDtypeStruct((t, Z_W), BF16),
                   jax.ShapeDtypeStruct((t, D_MODEL), BF16), jax.ShapeDtypeStruct((8, CONV_W), F32),
                   jax.ShapeDtypeStruct((1, 128), F32), jax.ShapeDtypeStruct((1, D_MODEL), F32)],
        scratch_shapes=[pltpu.VMEM((BLOCK, KV_W), F32), pltpu.VMEM((BLOCK, KV_W), F32), pltpu.VMEM((8, CONV_W), F32),
                        pltpu.VMEM((8, CONV_W), F32), pltpu.VMEM((Z_W, D_MODEL), BF16),
                        pltpu.VMEM((D_MODEL, D_MODEL), BF16), pltpu.SemaphoreType.DMA((2,))],
        compiler_params=_params(("arbitrary",), VMEM_LIMIT),
    )(dxo, x, gain, y, z, z, win_t, wout, conv_w, sinks, rope, rope, *after)


def _place():
    x, y, c = lax.axis_index("x"), lax.axis_index("y"), lax.axis_index("c")
    other_chips = [(1 - x, y), (x, 1 - y), (1 - x, 1 - y)]
    return x, y, c, other_chips


def _all_gather_rows(shards, place=(), *, name):
    n, p = len(shards), len(place)

    def body(*refs):
        srcs, place_srcs = refs[:n], refs[n:n + p]
        outs, place_outs = refs[n + p:2 * n + p], refs[2 * n + p:2 * (n + p)]
        send_sems, recv_sems, local_sems = refs[2 * (n + p):]
        x, y, c, chips = _place()
        me, sibling = (x, y, c), (x, y, 1 - c)

        def rows(t, px, py, pc):
            r = srcs[t].shape[-2]
            start = pl.multiple_of((4 * px + 2 * py + pc) * r, 16 if r % 16 == 0 else 8)
            if len(srcs[t].shape) == 3:
                return outs[t].at[:, pl.ds(start, r), :]
            return outs[t].at[pl.ds(start, r), :]

        def copy(t, k, block, to, own=False):
            return pltpu.make_async_remote_copy(
                src_ref=srcs[t] if own else rows(t, *block), dst_ref=rows(t, *block),
                send_sem=send_sems.at[t, k], recv_sem=recv_sems.at[t, k], device_id=to, device_id_type=MESH)

        mine = [pltpu.make_async_copy(srcs[t], rows(t, *me), local_sems.at[t]) for t in range(n)]
        mine += [pltpu.make_async_copy(place_srcs[q],
                                       _block_rows(place_outs[q], place_srcs[q].shape[-2], 4 * x + 2 * y + c),
                                       local_sems.at[n + q]) for q in range(p)]
        for q in range(p):
            mine[n + q].start()
        first = []
        for t in range(n):
            mine[t].start()
            first.append(copy(t, 0, me, sibling, own=True))
            first += [copy(t, 1 + j, me, (*chip, c), own=True) for j, chip in enumerate(chips)]
        for cp in first:
            cp.start()
        passed = []
        for j, chip in enumerate(chips):
            for t in range(n):
                copy(t, 1 + j, (*chip, c), me).wait_recv()
                fwd = copy(t, 4 + j, (*chip, c), sibling)
                fwd.start()
                passed.append(fwd)
        for t in range(n):
            copy(t, 0, sibling, me).wait_recv()
            for j, chip in enumerate(chips):
                copy(t, 4 + j, (*chip, 1 - c), me).wait_recv()
        for cp in first + passed:
            cp.wait_send()
        for cp in mine:
            cp.wait()

    out_shape = [jax.ShapeDtypeStruct(s.shape[:-2] + (N_DEV * s.shape[-2], s.shape[-1]), s.dtype)
                 for s in list(shards) + list(place)]
    res = pl.pallas_call(
        body, name=name, in_specs=[ANY] * (n + p), out_specs=[ANY] * (n + p), out_shape=out_shape,
        scratch_shapes=[pltpu.SemaphoreType.DMA((n, 7)), pltpu.SemaphoreType.DMA((n, 7)),
                        pltpu.SemaphoreType.DMA((n + p,))],
    )(*shards, *place)
    return res[:n], res[n:]


def _split_start(bufs, n_copies, plan, *, name, after=()):
    n = len(bufs)

    def body(*refs):
        token = refs[-1]
        for cp in plan(refs[:n], refs[n], refs[n + 1]):
            cp.start()
        token[...] = jnp.zeros_like(token)

    res = pl.pallas_call(
        _behind(body, n, after), name=name, in_specs=[HBM_SPEC] * n + [ANY] * len(after),
        out_specs=(SEM_SPEC, SEM_SPEC, *[HBM_SPEC] * n, pl.BlockSpec(memory_space=pltpu.VMEM)),
        out_shape=(pltpu.SemaphoreType.DMA((n_copies,)), pltpu.SemaphoreType.DMA((n_copies,)),
                   *[pltpu.HBM(b.shape, b.dtype) for b in bufs], jax.ShapeDtypeStruct((8, 128), F32)),
        input_output_aliases={i: 2 + i for i in range(n)},
        compiler_params=pltpu.CompilerParams(has_side_effects=DATAFLOW),
    )(*[pltpu.with_memory_space_constraint(b, pltpu.HBM) for b in bufs], *after)
    return res[0], res[1], list(res[2:2 + n]), res[-1]


def _split_wait(send_sems, recv_sems, bufs, after, plan, *, name):
    n = len(bufs)

    def body(*refs):
        for cp in plan(refs[:n], refs[n], refs[n + 1]):
            cp.wait_send()
            cp.wait_recv()

    return list(pl.pallas_call(
        body, name=name, in_specs=[HBM_SPEC] * n + [SEM_SPEC, SEM_SPEC, ANY], out_specs=[HBM_SPEC] * n,
        out_shape=tuple(pltpu.HBM(b.shape, b.dtype) for b in bufs),
        input_output_aliases={i: i for i in range(n)},
        compiler_params=pltpu.CompilerParams(has_side_effects=DATAFLOW),
    )(*bufs, send_sems, recv_sems, after))


def _sibling_plan(n):
    def plan(bufs, send_sems, recv_sems):
        x, y, c, _ = _place()
        return [pltpu.make_async_remote_copy(
            src_ref=bufs[t].at[:, 1 - c], dst_ref=bufs[n + t], send_sem=send_sems.at[t], recv_sem=recv_sems.at[t],
            device_id=(x, y, 1 - c), device_id_type=MESH) for t in range(n)]
    return plan


def _block_rows(ref, r, blk):
    start = pl.multiple_of(blk * r, 16 if r % 16 == 0 else 8)
    return ref.at[(slice(None),) * (len(ref.shape) - 2) + (pl.ds(start, r), slice(None))]


def _remote(src, dst, send_sems, recv_sems, k, peer):
    return pltpu.make_async_remote_copy(src_ref=src, dst_ref=dst, send_sem=send_sems.at[k], recv_sem=recv_sems.at[k],
                                        device_id=peer, device_id_type=MESH)


def _gather_send_plan(n, first=0):
    def plan(bufs, send_sems, recv_sems):
        x, y, c, chips = _place()
        peers = [(x, y, 1 - c)] + [(px, py, c) for px, py in chips]
        copies = []
        for t in range(n):
            dst = _block_rows(bufs[n + t], bufs[t].shape[-2], 4 * x + 2 * y + c)
            copies += [_remote(bufs[t], dst, send_sems, recv_sems, 4 * (first + t) + k, peer)
                       for k, peer in enumerate(peers)]
        return copies
    return plan


def _gather_forward_plan(rows):
    def plan(bufs, send_sems, recv_sems):
        x, y, c, chips = _place()
        copies = []
        for t, r in enumerate(rows):
            for j, (px, py) in enumerate(chips):
                blk = _block_rows(bufs[t], r, 4 * px + 2 * py + c)
                copies.append(_remote(blk, blk, send_sems, recv_sems, 3 * t + j, (x, y, 1 - c)))
        return copies
    return plan


def _chips_plan(n, with_small):
    def plan(bufs, send_sems, recv_sems):
        x, y, c, chips = _place()
        copies = []
        for t in range(n):
            for j, (px, py) in enumerate(chips):
                copies.append(_remote(bufs[t].at[2 * px + py], bufs[n + t].at[j], send_sems, recv_sems, 3 * t + j,
                                      (px, py, c)))
        if with_small:
            mine = _block_rows(bufs[2 * n], 8, 4 * x + 2 * y + c)
            flips = [(fx, fy, fc) for fx in range(2) for fy in range(2) for fc in range(2)][1:]
            for k, (fx, fy, fc) in enumerate(flips):
                peer = (x + fx - 2 * x * fx, y + fy - 2 * y * fy, c + fc - 2 * c * fc)
                copies.append(_remote(mine, mine, send_sems, recv_sems, 3 * n + k, peer))
        return copies
    return plan


def _place_own(fulls, shards, index, *, name):
    n = len(fulls)

    def body(index_ref, *refs):
        for t in range(n):
            refs[2 * n + t][...] = refs[n + t][...]

    def block_of(shard):
        lead = len(shard.shape) - 2
        return pl.BlockSpec(shard.shape, lambda i, index_ref: (0,) * lead + (index_ref[0], 0))

    def whole(shard):
        return pl.BlockSpec(shard.shape, lambda i, index_ref: (0,) * len(shard.shape))

    return list(pl.pallas_call(
        body, name=name,
        grid_spec=pltpu.PrefetchScalarGridSpec(
            num_scalar_prefetch=1, grid=(1,),
            in_specs=[ANY] * n + [whole(s) for s in shards], out_specs=[block_of(s) for s in shards]),
        out_shape=[jax.ShapeDtypeStruct(f.shape, f.dtype) for f in fulls],
        input_output_aliases={1 + t: t for t in range(n)},
        compiler_params=_params(("arbitrary",)),
    )(index, *fulls, *shards))


N_STEPS_SMALL = 2


def _add_sibling(grads, recvs, place, *, name):
    n = len(grads)

    def body(place_ref, *refs):
        chip = place_ref[1]
        for t in range(n):
            g_ref, r_ref, own_ref, ob_ref = refs[2 * t], refs[2 * t + 1], refs[2 * n + 2 * t], refs[2 * n + 2 * t + 1]
            own = jnp.zeros(own_ref.shape, F32)
            for m in range(4):
                p = g_ref[m, 0] + r_ref[m]
                ob_ref[m] = p.astype(BF16)
                own = jnp.where(chip == m, p, own)
            own_ref[...] = own

    in_specs, out_specs, out_shape = [], [], []
    for g, r in zip(grads, recvs):
        tr = g.shape[2] // N_STEPS_SMALL
        blocks = pl.BlockSpec((4, tr, D_MODEL), lambda i, place_ref: (0, i, 0))
        in_specs += [pl.BlockSpec((4, 1, tr, D_MODEL), lambda i, place_ref: (0, place_ref[0], i, 0)), blocks]
        out_specs += [pl.BlockSpec((tr, D_MODEL), lambda i, place_ref: (i, 0)), blocks]
        out_shape += [jax.ShapeDtypeStruct(r.shape[1:], F32), jax.ShapeDtypeStruct(r.shape, BF16)]
    res = pl.pallas_call(
        body, name=name,
        grid_spec=pltpu.PrefetchScalarGridSpec(num_scalar_prefetch=1, grid=(N_STEPS_SMALL,), in_specs=in_specs,
                                               out_specs=out_specs),
        out_shape=out_shape, compiler_params=_params(("arbitrary",), VMEM_LIMIT),
    )(place, *[a for pair in zip(grads, recvs) for a in pair])
    return [(res[2 * t], res[2 * t + 1]) for t in range(n)]


def _reduce_adamw(parts, *, name, after=()):
    n = len(parts)

    def body(*refs):
        for t in range(n):
            p_ref, r_ref, w_ref, m_ref, v_ref = refs[5 * t:5 * t + 5]
            g_ref, d_ref, mo_ref, vo_ref = refs[5 * n + 4 * t:5 * n + 4 * t + 4]
            g = p_ref[...] + r_ref[0].astype(F32) + r_ref[1].astype(F32) + r_ref[2].astype(F32)
            g_ref[...] = g
            d_ref[...], mo_ref[...], vo_ref[...] = _adamw_math(w_ref[...], g, m_ref[...], v_ref[...])
        refs[-1][...] = jnp.zeros_like(refs[-1])

    in_specs, out_specs, out_shape = [], [], []
    for own, _, _, _, _ in parts:
        rows = own.shape[0]
        tr = rows // N_STEPS_SMALL
        spec = pl.BlockSpec((tr, D_MODEL), lambda i: (i, 0))
        in_specs += [spec, pl.BlockSpec((3, tr, D_MODEL), lambda i: (0, i, 0)), spec, spec, spec]
        out_specs += [spec] * 4
        out_shape += [jax.ShapeDtypeStruct((rows, D_MODEL), F32)] * 4
    res = pl.pallas_call(
        _behind(body, 5 * n, after), name=name, grid=(N_STEPS_SMALL,),
        in_specs=in_specs + [ANY] * len(after),
        out_specs=out_specs + [pl.BlockSpec((8, 128), lambda i: (0, 0))],
        out_shape=out_shape + [jax.ShapeDtypeStruct((8, 128), F32)],
        compiler_params=_params(("arbitrary",), VMEM_LIMIT),
    )(*[a for part in parts for a in part], *after)
    return [tuple(res[4 * t:4 * t + 4]) for t in range(n)], res[-1]


def _adamw_math(w, g, m, v):
    m = ADAM_B1 * m + (1.0 - ADAM_B1) * g
    v = ADAM_B2 * v + (1.0 - ADAM_B2) * (g * g)
    m_hat = m / (1.0 - ADAM_B1 ** ADAM_STEP)
    v_hat = v / (1.0 - ADAM_B2 ** ADAM_STEP)
    delta = -ADAM_LR * (m_hat / (jnp.sqrt(v_hat) + ADAM_EPS) + ADAM_WD * w)
    return delta, m, v


SMALL_NAMES = ["ffn1_norm", "mix_norm", "ffn2_norm", "final_norm", "conv_w", "attn_sinks"]


def _update_small(given, moments_m, moments_v, small_all, my_index, *, name):
    conv_cols = given["conv_w"].shape[2]
    per_block = 128 // conv_cols

    def two_d(nm, a):
        return a.reshape(1, D_MODEL) if nm == "final_norm" else a

    operands = [two_d(nm, src[nm]) for nm in SMALL_NAMES for src in (given, moments_m, moments_v)]
    n = len(SMALL_NAMES)

    def body(index_ref, all_ref, conv_ref, *refs):
        ins, outs = refs[:3 * n], refs[3 * n:]
        total, conv_total = all_ref[0], conv_ref[0]
        for k in range(1, N_DEV):
            total, conv_total = total + all_ref[k], conv_total + conv_ref[k]
        which = index_ref[0] % per_block
        conv_g = conv_total[4:7, :conv_cols]
        for j in range(1, per_block):
            conv_g = jnp.where(which == j, conv_total[4:7, j * conv_cols:(j + 1) * conv_cols], conv_g)
        grads = [total[0:1], total[1:2], total[2:3], total[3:4], conv_g[None], total[7:8, :N_Q_HEADS]]
        for t, g in enumerate(grads):
            w_ref, m_ref, v_ref = ins[3 * t:3 * t + 3]
            g_ref, d_ref, mo_ref, vo_ref = outs[4 * t:4 * t + 4]
            g_ref[...] = g
            d_ref[...], mo_ref[...], vo_ref[...] = _adamw_math(w_ref[...], g, m_ref[...], v_ref[...])
        outs[-1][...] = total[7:8, LOSS_LANE:LOSS_LANE + 1]

    def whole(shape):
        return pl.BlockSpec(shape, lambda i, index_ref: (0,) * len(shape))

    shapes = [a.shape for a in operands[::3] for _ in range(4)] + [(1, 1)]
    res = pl.pallas_call(
        body, name=name,
        grid_spec=pltpu.PrefetchScalarGridSpec(
            num_scalar_prefetch=1, grid=(1,),
            in_specs=[whole(small_all.shape),
                      pl.BlockSpec((N_DEV, 8, 128), lambda i, index_ref: (0, 0, index_ref[0] // per_block))]
            + [whole(a.shape) for a in operands],
            out_specs=[whole(s) for s in shapes]),
        out_shape=[jax.ShapeDtypeStruct(s, F32) for s in shapes],
        compiler_params=_params(("arbitrary",)),
    )(my_index.astype(jnp.int32).reshape(1), small_all, small_all, *operands)
    results = {nm: tuple(a.reshape(given[nm].shape) for a in res[4 * t:4 * t + 4]) for t, nm in enumerate(SMALL_NAMES)}
    return results, res[-1]


def kernel(x, ffn1_norm, ffn1_w_gate, ffn1_w_up, ffn1_w_down, mix_norm, w_in, conv_w, attn_sinks, w_out, ffn2_norm, ffn2_w_gate, ffn2_w_up, ffn2_w_down, final_norm, loss_target, m_ffn1_norm, m_ffn1_w_gate, m_ffn1_w_up, m_ffn1_w_down, m_mix_norm, m_w_in, m_conv_w, m_attn_sinks, m_w_out, m_ffn2_norm, m_ffn2_w_gate, m_ffn2_w_up, m_ffn2_w_down, m_final_norm, v_ffn1_norm, v_ffn1_w_gate, v_ffn1_w_up, v_ffn1_w_down, v_mix_norm, v_w_in, v_conv_w, v_attn_sinks, v_w_out, v_ffn2_norm, v_ffn2_w_gate, v_ffn2_w_up, v_ffn2_w_down, v_final_norm):
    ix, iy, ic = lax.axis_index("x"), lax.axis_index("y"), lax.axis_index("c")
    my_index = 4 * ix + 2 * iy + ic
    place = jnp.stack([ic, 2 * ix + iy]).astype(jnp.int32)

    given = dict(ffn1_norm=ffn1_norm, ffn1_w_gate=ffn1_w_gate, ffn1_w_up=ffn1_w_up, ffn1_w_down=ffn1_w_down,
                 mix_norm=mix_norm, w_in=w_in, conv_w=conv_w, attn_sinks=attn_sinks, w_out=w_out, ffn2_norm=ffn2_norm,
                 ffn2_w_gate=ffn2_w_gate, ffn2_w_up=ffn2_w_up, ffn2_w_down=ffn2_w_down, final_norm=final_norm)
    moments_m = dict(ffn1_norm=m_ffn1_norm, ffn1_w_gate=m_ffn1_w_gate, ffn1_w_up=m_ffn1_w_up, ffn1_w_down=m_ffn1_w_down,
                     mix_norm=m_mix_norm, w_in=m_w_in, conv_w=m_conv_w, attn_sinks=m_attn_sinks, w_out=m_w_out,
                     ffn2_norm=m_ffn2_norm, ffn2_w_gate=m_ffn2_w_gate, ffn2_w_up=m_ffn2_w_up, ffn2_w_down=m_ffn2_w_down,
                     final_norm=m_final_norm)
    moments_v = dict(ffn1_norm=v_ffn1_norm, ffn1_w_gate=v_ffn1_w_gate, ffn1_w_up=v_ffn1_w_up, ffn1_w_down=v_ffn1_w_down,
                     mix_norm=v_mix_norm, w_in=v_w_in, conv_w=v_conv_w, attn_sinks=v_attn_sinks, w_out=v_w_out,
                     ffn2_norm=v_ffn2_norm, ffn2_w_gate=v_ffn2_w_gate, ffn2_w_up=v_ffn2_w_up, ffn2_w_down=v_ffn2_w_down,
                     final_norm=v_final_norm)

    xs = x[0]
    target = loss_target[0]
    final_gain = final_norm.reshape(1, D_MODEL)

    def ffn_shards(wg, wu, wd):
        return jnp.stack([wg[0].T, wu[0].T]).astype(BF16), wd[0].astype(BF16)

    conv_cols = conv_w.shape[2]
    conv_shard = jnp.pad(conv_w[0], ((0, 5), (0, 128 - conv_cols)))
    gate_up1, down1 = ffn_shards(ffn1_w_gate, ffn1_w_up, ffn1_w_down)
    rest_shards = [down1, w_in[0].T.astype(BF16), w_out[0].astype(BF16), conv_shard,
                   *ffn_shards(ffn2_w_gate, ffn2_w_up, ffn2_w_down)]
    rest_rows = [s.shape[-2] for s in rest_shards]
    n_rest, n_early = len(rest_shards), 4
    (w1_gu,), _ = _all_gather_rows([gate_up1], name="gather_ffn1")

    fulls = [lax.empty(s.shape[:-2] + (N_DEV * s.shape[-2], s.shape[-1]), s.dtype) for s in rest_shards]
    fulls = _place_own(fulls, rest_shards, my_index.astype(jnp.int32).reshape(1), name="place_own_weights")
    ssem, rsem, bufs, token = _split_start(rest_shards + list(fulls), 4 * n_rest, _gather_send_plan(n_rest),
                                           name="gather_rest_start", after=[w1_gu])
    early = bufs[:n_early] + bufs[n_rest:n_rest + n_early]
    late = bufs[n_early:n_rest] + bufs[n_rest + n_early:]
    h1, s1, sa1, sb1 = _ffn_fwd(xs, ffn1_norm, w1_gu, name="ffn1_hidden", after=[token])
    early = _split_wait(ssem, rsem, early, h1, _gather_send_plan(n_early), name="gather_early_wait")
    fwd_early = _gather_forward_plan(rest_rows[:n_early])
    ssem_e, rsem_e, parts, token = _split_start(early[n_early:], 3 * n_early, fwd_early, name="forward_early_start")
    rope = _rope_tables(xs.shape[0], after=[token])
    w1_d, win_t, wout, conv_all = _split_wait(ssem_e, rsem_e, parts, rope, fwd_early, name="forward_early_wait")
    x1 = _ffn_down(xs, s1, w1_d, name="ffn1_down")
    conv_full = conv_all.reshape(N_DEV, 8, 128)[:, :3, :conv_cols].transpose(1, 0, 2).reshape(3, CONV_W)
    late = _split_wait(ssem, rsem, late, x1, _gather_send_plan(n_rest - n_early, first=n_early),
                       name="gather_late_wait")
    fwd_ffn2 = _gather_forward_plan(rest_rows[n_early:])
    ssem, rsem, parts, token = _split_start(late[n_rest - n_early:], 3 * (n_rest - n_early), fwd_ffn2,
                                            name="forward_ffn2_start")
    x2, hm, z, y = _mixer_fwd(x1, mix_norm, win_t, wout, conv_full, attn_sinks, rope, name="mixer_fwd", after=[token])
    w2_gu, w2_d = _split_wait(ssem, rsem, parts, x2, fwd_ffn2, name="forward_ffn2_wait")
    dx3, h2, s2, sa2, sb2, loss_local, d_final = _ffn_fwd(x2, ffn2_norm, w2_gu, w2_d, head=(final_gain, target),
                                                          name="ffn2_fwd")

    def to_sibling_start(grads, tag, after=()):
        views = [g.reshape(4, 2, g.shape[0] // N_DEV, D_MODEL) for g in grads]
        lands = [lax.empty((4,) + v.shape[2:], F32) for v in views]
        plan = _sibling_plan(len(views))
        ssem, rsem, bufs, token = _split_start(views + lands, len(views), plan, name=f"{tag}_sibling_start", after=after)
        return (ssem, rsem, bufs, plan, tag), token

    def to_sibling_finish(handle, after, names):
        ssem, rsem, bufs, plan, tag = handle
        bufs = _split_wait(ssem, rsem, bufs, after, plan, name=f"{tag}_sibling_wait")
        n = len(names)
        return _add_sibling(bufs[:n], bufs[n:], place, name=f"add_sibling_{tag}")

    def to_chips_start(partials, tag, small_all=None, after=()):
        p16 = [p for _, p in partials]
        lands = [lax.empty((3,) + p.shape[1:], BF16) for p in p16]
        extra = [] if small_all is None else [small_all]
        plan = _chips_plan(len(p16), small_all is not None)
        ssem, rsem, bufs, token = _split_start(p16 + lands + extra, 3 * len(p16) + 7 * len(extra), plan,
                                               name=f"{tag}_chips_start", after=after)
        return (ssem, rsem, bufs, plan, tag), token

    def to_chips_finish(handle, partials, after, names):
        ssem, rsem, bufs, plan, tag = handle
        bufs = _split_wait(ssem, rsem, bufs, after, plan, name=f"{tag}_chips_wait")
        n = len(names)
        return [(p32, r) for (p32, _), r in zip(partials, bufs[n:2 * n])], bufs[2 * n:]

    half_ff = D_FF // 2
    names2, namesm = ["ffn2_w_gate", "ffn2_w_up", "ffn2_w_down"], ["w_in", "w_out"]
    transposed = {"ffn1_w_gate", "ffn1_w_up", "w_in", "ffn2_w_gate", "ffn2_w_up"}
    grad, delta, new_m, new_v = {}, {}, {}, {}

    def adam_big(names, parts, tag, after=()):
        def to_rows(nm, a):
            return jnp.swapaxes(a, 1, 2)[0] if nm in transposed else a[0]

        def from_rows(nm, a):
            return jnp.swapaxes(a[None], 1, 2) if nm in transposed else a[None]

        operands = [(p32, recv, to_rows(nm, given[nm]), to_rows(nm, moments_m[nm]), to_rows(nm, moments_v[nm]))
                    for nm, (p32, recv) in zip(names, parts)]
        results, token = _reduce_adamw(operands, name=f"adamw_{tag}", after=after)
        for nm, outs in zip(names, results):
            grad[nm], delta[nm], new_m[nm], new_v[nm] = (from_rows(nm, a) for a in outs)
        return token

    dx2, da2, db2, g2b, d_norm2 = _ffn_dgrad(dx3, x2, ffn2_norm, sa2, sb2, w2_gu, w2_d, name="ffn2_dgrad")
    gw2 = [_tn_matmul(da2, h2, name="ffn2_wgrad_gate", bm=half_ff), _tn_matmul(db2, h2, name="ffn2_wgrad_up", bm=half_ff),
           _tn_matmul(s2, g2b, name="ffn2_wgrad_down", bm=half_ff)]
    sib2, tok = to_sibling_start(gw2, "ffn2")
    dx1, dz, gmb, d_conv, d_sink, d_normm = _mixer_bwd(dx2, x1, mix_norm, y, z, win_t, wout, conv_full, attn_sinks,
                                                       rope, name="mixer_bwd", after=[tok])
    p2 = to_sibling_finish(sib2, dx1, names2)
    chips2, tok = to_chips_start(p2, "ffn2")
    gwm = [_tn_matmul(dz, hm, name="mixer_wgrad_in", bm=Z_W // 2, after=[tok]),
           _tn_matmul(y, gmb, name="mixer_wgrad_out", bm=D_MODEL, after=[tok])]
    sibm, tok = to_sibling_start(gwm, "mixer")
    dx0, da1, db1, g1b, d_norm1 = _ffn_dgrad(dx1, xs, ffn1_norm, sa1, sb1, w1_gu, w1_d, name="ffn1_dgrad", after=[tok])
    r2, _ = to_chips_finish(chips2, p2, dx0, names2)
    pm = to_sibling_finish(sibm, dx0, namesm)
    chipsm, tok = to_chips_start(pm, "mixer")
    gw_gate = _tn_matmul(da1, h1, name="ffn1_wgrad_gate", bm=half_ff, after=[tok])
    sib_gate, tok = to_sibling_start([gw_gate], "ffn1_gate")
    gw_up = _tn_matmul(db1, h1, name="ffn1_wgrad_up", bm=half_ff, after=[tok])
    rm, _ = to_chips_finish(chipsm, pm, gw_up, namesm)
    p_gate = to_sibling_finish(sib_gate, gw_up, ["ffn1_w_gate"])
    chips_gate, tok_a = to_chips_start(p_gate, "ffn1_gate")
    sib_up, tok_b = to_sibling_start([gw_up], "ffn1_up", after=[tok_a])
    gw_down = _tn_matmul(s1, g1b, name="ffn1_wgrad_down_first", bm=half_ff, blocks=(0, 1), after=[tok_a, tok_b])
    p_up = to_sibling_finish(sib_up, gw_down, ["ffn1_w_up"])
    chips_up, tok_a = to_chips_start(p_up, "ffn1_up")
    gw_down = _tn_matmul(s1, g1b, name="ffn1_wgrad_down_second", bm=half_ff, blocks=(1, 1), into=gw_down, after=[tok_a])
    sib_down, tok_b = to_sibling_start([gw_down], "ffn1_down")
    p_down = to_sibling_finish(sib_down, tok_b, ["ffn1_w_down"])
    last_row = (jnp.pad(d_sink, ((0, 0), (0, D_MODEL - 128)))
                + jnp.pad(loss_local, ((0, 0), (LOSS_LANE, D_MODEL - LOSS_LANE - 1))))
    small = jnp.concatenate([
        d_norm1, d_normm, d_norm2, d_final, jnp.pad(d_conv[0:3], ((0, 0), (0, D_MODEL - CONV_W))), last_row], axis=0)
    (small_all,) = _place_own([lax.empty((N_DEV * 8, D_MODEL), F32)], [small], my_index.astype(jnp.int32).reshape(1),
                              name="place_own_small")
    chips_down, tok = to_chips_start(p_down, "ffn1_down", small_all)
    tok = adam_big(names2, r2, "ffn2", after=[tok])
    tok = adam_big(namesm, rm, "mixer", after=[tok])
    r_gate, _ = to_chips_finish(chips_gate, p_gate, tok, ["ffn1_w_gate"])
    tok = adam_big(["ffn1_w_gate"], r_gate, "ffn1_gate")
    r_up, _ = to_chips_finish(chips_up, p_up, tok, ["ffn1_w_up"])
    tok = adam_big(["ffn1_w_up"], r_up, "ffn1_up")
    r_down, (small_all,) = to_chips_finish(chips_down, p_down, tok, ["ffn1_w_down"])
    adam_big(["ffn1_w_down"], r_down, "ffn1_down")
    results, loss = _update_small(given, moments_m, moments_v, small_all.reshape(N_DEV, 8, D_MODEL), my_index,
                                  name="update_small")
    for nm, outs in results.items():
        grad[nm], delta[nm], new_m[nm], new_v[nm] = outs

    order = list(given)
    return (loss.reshape(()), dx0[None], *[grad[n] for n in order], *[delta[n] for n in order],
            *[new_m[n] for n in order], *[new_v[n] for n in order])
```

```python
import functools

import jax
import jax.numpy as jnp
from jax import lax
from jax.experimental import pallas as pl
from jax.experimental.pallas import tpu as pltpu

F32 = jnp.float32
BF16 = jnp.bfloat16
MESH = pl.DeviceIdType.MESH
ANY = pl.BlockSpec(memory_space=pl.ANY)
HBM_SPEC = pl.BlockSpec(memory_space=pltpu.HBM)
SEM_SPEC = pl.BlockSpec(memory_space=pltpu.SEMAPHORE)
DATAFLOW = pltpu.SideEffectType.DATAFLOW_SIDE_EFFECTING

N_DEV = 8
LOSS_LANE = 128
D_MODEL = 1024
D_FF = 2816
CONV_W = 512
ATTN_W = 512
KV_W = 128
HEAD_DIM = 64
N_Q_HEADS = 8
N_KV_HEADS = 2
Q_PER_KV = N_Q_HEADS // N_KV_HEADS
BLOCK = 128
ROT_DIM = 16
ROPE_THETA = 500000.0
Z_W = 3 * CONV_W + ATTN_W + 2 * KV_W
Q_OFF = 3 * CONV_W
K_OFF = Q_OFF + ATTN_W
V_OFF = K_OFF + KV_W
RMS_EPS = 1e-5
MASK_VALUE = -1e30
SM_SCALE = HEAD_DIM ** -0.5
FFN_RES_SCALE = 0.5

ADAM_LR = 0.001
ADAM_B1 = 0.9
ADAM_B2 = 0.999
ADAM_EPS = 1e-08
ADAM_WD = 0.01
ADAM_STEP = 10

NT_DIMS = (((1,), (1,)), ((), ()))
TN_DIMS = (((0,), (0,)), ((), ()))

VMEM_LIMIT = 62 * 1024 * 1024
FF_CHUNK = 256


def _params(sem, vmem=None):
    return pltpu.CompilerParams(dimension_semantics=sem, vmem_limit_bytes=vmem)


def _behind(body, n_in, after):
    k = len(after)
    if k == 0:
        return body
    return lambda *refs: body(*refs[:n_in], *refs[n_in + k:])


def _rms_stats(xf):
    inv = lax.rsqrt(jnp.mean(xf * xf, axis=-1, keepdims=True) + RMS_EPS)
    return xf * inv, inv


def _rms_bwd(dh, xhat, inv, gain):
    dxhat = dh * gain
    dx = inv * (dxhat - xhat * jnp.mean(dxhat * xhat, axis=-1, keepdims=True))
    dgain = jnp.sum(dh * xhat, axis=0, keepdims=True)
    return dx, dgain


LOAD_PIECES = 4


def _load_resident(pairs, sems):
    @pl.when(pl.program_id(0) == 0)
    def _():
        copies = []
        for k, (w_hbm, w_ref) in enumerate(pairs):
            rows = w_hbm.shape[-2] // LOAD_PIECES
            for p in range(LOAD_PIECES):
                piece = (slice(None),) * (len(w_hbm.shape) - 2) + (pl.ds(p * rows, rows), slice(None))
                copies.append(pltpu.make_async_copy(w_hbm.at[piece], w_ref.at[piece], sems.at[k, p]))
        for cp in copies:
            cp.start()
        for cp in copies:
            cp.wait()


def _ffn_fwd(x, gain, w_gu, w_d=None, *, name, head=None, after=(), tm=256, sub=256, tf=FF_CHUNK):
    t = x.shape[0]
    tm = min(tm, t)
    sub = min(sub, tm)
    n_down = 0 if w_d is None else 1
    n_head = 0 if head is None else 2
    assert n_down or not n_head
    n_in = 3 + n_down + n_head

    def body(*refs):
        x_ref, g_ref = refs[:2]
        w_hbms, head_refs = refs[2:3 + n_down], refs[3 + n_down:n_in]
        xo_refs = refs[n_in:n_in + n_down]
        h_ref, s_ref, sa_ref, sb_ref = refs[n_in + n_down:n_in + n_down + 4]
        head_outs = refs[n_in + n_down + 4:n_in + n_down + 4 + n_head]
        w_refs, sems = refs[n_in + n_down + 4 + n_head:-1], refs[-1]
        _load_resident(list(zip(w_hbms, w_refs)), sems)

        @pl.when(pl.program_id(0) == 0)
        def _():
            for ref in head_outs:
                ref[...] = jnp.zeros_like(ref)

        for r0 in range(0, tm, sub):
            rows = slice(r0, r0 + sub)
            xf = x_ref[rows, :]
            xhat, _ = _rms_stats(xf)
            h = (xhat * g_ref[...]).astype(BF16)
            h_ref[rows, :] = h
            for c in range(0, D_FF, tf):
                cols = slice(c, min(c + tf, D_FF))
                a = lax.dot_general(h, w_refs[0][0, cols, :], NT_DIMS, preferred_element_type=F32)
                b = lax.dot_general(h, w_refs[0][1, cols, :], NT_DIMS, preferred_element_type=F32)
                sig = jax.nn.sigmoid(a)
                silu = a * sig
                s_ref[rows, cols] = (silu * b).astype(BF16)
                sa_ref[rows, cols] = (b * (sig * (1.0 + a * (1.0 - sig)))).astype(BF16)
                sb_ref[rows, cols] = silu.astype(BF16)
            if not n_down:
                continue
            xo = xf + FFN_RES_SCALE * jnp.dot(s_ref[rows, :], w_refs[1][...], preferred_element_type=F32)
            if head is None:
                xo_refs[0][rows, :] = xo
            else:
                fg_ref, t_ref = head_refs
                loss_ref, dfg_ref = head_outs
                xhat_o, inv_o = _rms_stats(xo)
                err = xhat_o * fg_ref[...] - t_ref[rows, :]
                loss_ref[...] += 0.5 * jnp.sum(jnp.mean(err * err, axis=-1, keepdims=True), axis=0, keepdims=True)
                xo_refs[0][rows, :], dfg = _rms_bwd(err * (1.0 / D_MODEL), xhat_o, inv_o, fg_ref[...])
                dfg_ref[...] += dfg

    row = pl.BlockSpec((tm, D_MODEL), lambda i: (i, 0))
    hid = pl.BlockSpec((tm, D_FF), lambda i: (i, 0))
    vec = pl.BlockSpec((1, D_MODEL), lambda i: (0, 0))
    head_in = [] if head is None else [vec, row]
    head_out = [] if head is None else [pl.BlockSpec((1, 1), lambda i: (0, 0)), vec]
    head_shape = [] if head is None else [jax.ShapeDtypeStruct((1, 1), F32), jax.ShapeDtypeStruct((1, D_MODEL), F32)]
    return pl.pallas_call(
        _behind(body, n_in, after), name=name, grid=(t // tm,),
        in_specs=[row, vec] + [ANY] * (1 + n_down) + head_in + [ANY] * len(after),
        out_specs=[row] * (n_down + 1) + [hid, hid, hid] + head_out,
        out_shape=[jax.ShapeDtypeStruct((t, D_MODEL), F32)] * n_down + [jax.ShapeDtypeStruct((t, D_MODEL), BF16)]
        + [jax.ShapeDtypeStruct((t, D_FF), BF16)] * 3 + head_shape,
        scratch_shapes=[pltpu.VMEM((2, D_FF, D_MODEL), BF16)] + [pltpu.VMEM((D_FF, D_MODEL), BF16)] * n_down
        + [pltpu.SemaphoreType.DMA((2, LOAD_PIECES))],
        compiler_params=_params(("arbitrary",), VMEM_LIMIT),
    )(x, gain, w_gu, *([] if w_d is None else [w_d]), *(head or ()), *after)


def _ffn_down(x, s, w_d, *, name, after=(), tm=512):
    t = x.shape[0]
    tm = min(tm, t)

    def body(x_ref, s_ref, w_hbm, xo_ref, w_ref, sem):
        _load_resident([(w_hbm, w_ref)], sem)
        xo_ref[...] = x_ref[...] + FFN_RES_SCALE * jnp.dot(s_ref[...], w_ref[...], preferred_element_type=F32)

    row = pl.BlockSpec((tm, D_MODEL), lambda i: (i, 0))
    return pl.pallas_call(
        _behind(body, 3, after), name=name, grid=(t // tm,),
        in_specs=[row, pl.BlockSpec((tm, D_FF), lambda i: (i, 0)), ANY] + [ANY] * len(after), out_specs=row,
        out_shape=jax.ShapeDtypeStruct((t, D_MODEL), F32),
        scratch_shapes=[pltpu.VMEM((D_FF, D_MODEL), BF16), pltpu.SemaphoreType.DMA((1, LOAD_PIECES))],
        compiler_params=_params(("arbitrary",), VMEM_LIMIT),
    )(x, s, w_d, *after)


def _ffn_dgrad(dxo, x, gain, sa, sb, w_gu, w_d, *, name, after=(), tm=512, sub=512, tf=FF_CHUNK):
    t = x.shape[0]
    tm = min(tm, t)
    sub = min(sub, tm)

    def body(dxo_ref, x_ref, g_ref, sa_ref, sb_ref, wgu_hbm, wd_hbm, dxi_ref, da_ref, db_ref, gb_ref, dg_ref,
             wgu_ref, wd_ref, sems):
        _load_resident([(wd_hbm, wd_ref), (wgu_hbm, wgu_ref)], sems)

        @pl.when(pl.program_id(0) == 0)
        def _():
            dg_ref[...] = jnp.zeros_like(dg_ref)

        for r0 in range(0, tm, sub):
            rows = slice(r0, r0 + sub)
            go = dxo_ref[rows, :]
            gb = (FFN_RES_SCALE * go).astype(BF16)
            gb_ref[rows, :] = gb
            for c in range(0, D_FF, tf):
                cols = slice(c, min(c + tf, D_FF))
                ds = lax.dot_general(gb, wd_ref[cols, :], NT_DIMS, preferred_element_type=F32)
                da_ref[rows, cols] = (ds * sa_ref[rows, cols].astype(F32)).astype(BF16)
                db_ref[rows, cols] = (ds * sb_ref[rows, cols].astype(F32)).astype(BF16)
            dh = (jnp.dot(da_ref[rows, :], wgu_ref[0], preferred_element_type=F32)
                  + jnp.dot(db_ref[rows, :], wgu_ref[1], preferred_element_type=F32))
            xhat, inv = _rms_stats(x_ref[rows, :])
            dx, dgain = _rms_bwd(dh, xhat, inv, g_ref[...])
            dxi_ref[rows, :] = go + dx
            dg_ref[...] += dgain

    row = pl.BlockSpec((tm, D_MODEL), lambda i: (i, 0))
    hid = pl.BlockSpec((tm, D_FF), lambda i: (i, 0))
    vec = pl.BlockSpec((1, D_MODEL), lambda i: (0, 0))
    return pl.pallas_call(
        _behind(body, 7, after), name=name, grid=(t // tm,),
        in_specs=[row, row, vec, hid, hid, ANY, ANY] + [ANY] * len(after),
        out_specs=[row, hid, hid, row, vec],
        out_shape=[jax.ShapeDtypeStruct((t, D_MODEL), F32), jax.ShapeDtypeStruct((t, D_FF), BF16),
                   jax.ShapeDtypeStruct((t, D_FF), BF16),
                   jax.ShapeDtypeStruct((t, D_MODEL), BF16), jax.ShapeDtypeStruct((1, D_MODEL), F32)],
        scratch_shapes=[pltpu.VMEM((2, D_FF, D_MODEL), BF16), pltpu.VMEM((D_FF, D_MODEL), BF16),
                        pltpu.SemaphoreType.DMA((2, LOAD_PIECES))],
        compiler_params=_params(("arbitrary",), VMEM_LIMIT),
    )(dxo, x, gain, sa, sb, w_gu, w_d, *after)


def _tn_matmul(a, b, *, name, bm, after=(), tk=2048, blocks=None, into=None):
    t, m = a.shape
    n = b.shape[1]
    tk = min(tk, t)
    nk = t // tk
    first, count = blocks or (0, m // bm)
    behind = ([] if into is None else [into]) + list(after)

    def body(a_ref, b_ref, o_ref):
        @pl.when(pl.program_id(1) == 0)
        def _():
            o_ref[...] = jnp.zeros_like(o_ref)

        o_ref[...] += lax.dot_general(a_ref[...], b_ref[...], TN_DIMS, preferred_element_type=F32)

    return pl.pallas_call(
        _behind(body, 2, behind), name=name, grid=(count, nk),
        in_specs=[pl.BlockSpec((tk, bm), lambda i, k: (k, first + i)), pl.BlockSpec((tk, n), lambda i, k: (k, 0))]
        + [ANY] * len(behind),
        out_specs=pl.BlockSpec((bm, n), lambda i, k: (first + i, 0)),
        out_shape=jax.ShapeDtypeStruct((m, n), F32),
        input_output_aliases={} if into is None else {2: 0},
        compiler_params=_params(("parallel", "arbitrary"), VMEM_LIMIT),
    )(a, b, *behind)


def _rope_tables(t, after=()):
    inv_freq = ROPE_THETA ** (-jnp.arange(0, ROT_DIM, 2, dtype=F32) / ROT_DIM)
    ang = inv_freq[:, None] * jnp.arange(t, dtype=F32)[None, :]
    compact = jnp.stack([jnp.cos(ang), jnp.sin(ang)])
    tr = min(1024, t)

    def body(c_ref, o_ref):
        for k in range(2):
            o_ref[k] = jnp.tile(c_ref[k], (128 // inv_freq.shape[0], 1)).T

    return pl.pallas_call(
        _behind(body, 1, after), name="rope_tables", grid=(t // tr,),
        in_specs=[pl.BlockSpec((2, inv_freq.shape[0], tr), lambda i: (0, 0, i))] + [ANY] * len(after),
        out_specs=pl.BlockSpec((2, tr, 128), lambda i: (0, i, 0)),
        out_shape=jax.ShapeDtypeStruct((2, t, 128), F32), compiler_params=_params(("parallel",)),
    )(compact, *after)


def _rope_multipliers(cos_sin):
    half = ROT_DIM // 2
    cos, sin = cos_sin[0], cos_sin[1]
    d = lax.broadcasted_iota(jnp.int32, cos.shape, 1) & (HEAD_DIM - 1)
    mult = jnp.where(d < ROT_DIM, cos, 1.0)
    from_lo = jnp.where((d >= half) & (d < ROT_DIM), sin, 0.0)
    from_hi = jnp.where(d < half, -sin, 0.0)
    return mult, from_lo, from_hi


def _tile_lanes(tab, width):
    return jnp.tile(tab, (1, width // tab.shape[1]))


def _rope(v, tab):
    w = v.shape[1]
    half_rot = ROT_DIM // 2
    return (v * _tile_lanes(tab[0], w)
            + pltpu.roll(v, half_rot, axis=1) * _tile_lanes(tab[1], w)
            + pltpu.roll(v, w - half_rot, axis=1) * _tile_lanes(tab[2], w))


def _rope_bwd(dv, tab):
    w = dv.shape[1]
    half_rot = ROT_DIM // 2
    return (dv * _tile_lanes(tab[0], w)
            + pltpu.roll(dv * _tile_lanes(tab[1], w), w - half_rot, axis=1)
            + pltpu.roll(dv * _tile_lanes(tab[2], w), half_rot, axis=1))


def _shift_rows(v, prev8_ref, n):
    r = lax.broadcasted_iota(jnp.int32, v.shape, 0)
    rolled = pltpu.roll(v, n, axis=0)
    last = prev8_ref[7:8, :]
    if n == 1:
        return jnp.where(r >= 1, rolled, last)
    return jnp.where(r >= 2, rolled, jnp.where(r == 0, prev8_ref[6:7, :], last))


def _shift_rows_up(v, next8_ref, n):
    rows = v.shape[0]
    r = lax.broadcasted_iota(jnp.int32, v.shape, 0)
    rolled = pltpu.roll(v, rows - n, axis=0)
    first = next8_ref[0:1, :]
    if n == 1:
        return jnp.where(r <= rows - 2, rolled, first)
    return jnp.where(r <= rows - 3, rolled, jnp.where(r == rows - 2, first, next8_ref[1:2, :]))


def _lane_half_mask(shape, half):
    lane = lax.broadcasted_iota(jnp.int32, shape, 1)
    return (lane >= HEAD_DIM) if half else (lane < HEAD_DIM)


def _to_kv_lanes(chunk, head, kv):
    if head % 2 != kv:
        chunk = pltpu.roll(chunk, HEAD_DIM, axis=1)
    return jnp.where(_lane_half_mask(chunk.shape, kv), chunk, 0.0)


def _from_kv_lanes(chunk, head, kv):
    chunk = jnp.where(_lane_half_mask(chunk.shape, kv), chunk, 0.0)
    if head % 2 != kv:
        chunk = pltpu.roll(chunk, HEAD_DIM, axis=1)
    return chunk


def _stack_heads(wide):
    parts = []
    for head in range(N_Q_HEADS):
        chunk = wide[:, (head // 2) * 128:(head // 2 + 1) * 128]
        parts.append(_to_kv_lanes(chunk, head, head // Q_PER_KV))
    return jnp.concatenate(parts, axis=0)


def _window_mask(has_prev):
    shape = (N_Q_HEADS * BLOCK, 2 * BLOCK)
    qi = lax.broadcasted_iota(jnp.int32, shape, 0) & (BLOCK - 1)
    kj = lax.broadcasted_iota(jnp.int32, shape, 1)
    first_key = BLOCK - has_prev * BLOCK
    in_prev = (kj < BLOCK) & (kj > qi) & (kj >= first_key)
    in_own = (kj >= BLOCK) & ((kj - BLOCK) <= qi)
    return in_prev | in_own


def _sink_column(sink_ref):
    row = lax.broadcasted_iota(jnp.int32, (N_Q_HEADS * BLOCK, 1), 0)
    col = jnp.full((N_Q_HEADS * BLOCK, 1), sink_ref[0, 0], F32)
    for head in range(1, N_Q_HEADS):
        col = jnp.where(row >= head * BLOCK, sink_ref[0, head], col)
    return col


def _softmax_with_sink(q4, k2, mask, sink):
    s = lax.dot_general(q4, k2, NT_DIMS, preferred_element_type=F32) * SM_SCALE
    s = jnp.where(mask, s, MASK_VALUE)
    m = jnp.maximum(jnp.max(s, axis=-1, keepdims=True), sink)
    p = jnp.exp(s - m)
    e_sink = jnp.exp(sink - m)
    inv_den = 1.0 / (jnp.sum(p, axis=-1, keepdims=True) + e_sink)
    return p * inv_den, e_sink * inv_den


def _conv_terms(zf, prev8_ref, w_ref):
    b_gate, c_gate, u = zf[:, 0:CONV_W], zf[:, CONV_W:2 * CONV_W], zf[:, 2 * CONV_W:3 * CONV_W]
    vc = c_gate * u
    vm1 = _shift_rows(vc, prev8_ref, 1)
    vm2 = _shift_rows(vc, prev8_ref, 2)
    conv = w_ref[0:1, :] * vm2 + w_ref[1:2, :] * vm1 + w_ref[2:3, :] * vc
    return b_gate, c_gate, u, vc, vm1, vm2, conv


def _mixer_fwd(x, gain, win_t, wout, conv_w, sinks, rope, *, name, after=(), tq=512):
    t = x.shape[0]
    tq = min(tq, t)
    nblk = tq // BLOCK

    def body(x_ref, g_ref, win_hbm, wout_hbm, cw_ref, sink_ref, rope_ref,
             xo_ref, h_ref, z_ref, y_ref, kprev_ref, vprev_ref, cprev_ref, win_ref, wout_ref, sems):
        i = pl.program_id(0)
        _load_resident([(win_hbm, win_ref), (wout_hbm, wout_ref)], sems)

        @pl.when(i == 0)
        def _():
            kprev_ref[...] = jnp.zeros_like(kprev_ref)
            vprev_ref[...] = jnp.zeros_like(vprev_ref)
            cprev_ref[...] = jnp.zeros_like(cprev_ref)

        xf = x_ref[...]
        xhat, _ = _rms_stats(xf)
        h = (xhat * g_ref[...]).astype(BF16)
        h_ref[...] = h

        def project(c0, c1):
            zc = lax.dot_general(h, win_ref[c0:c1, :], NT_DIMS, preferred_element_type=F32).astype(BF16)
            z_ref[:, c0:c1] = zc
            return zc

        zb = project(Q_OFF, Z_W)
        zf = zb.astype(F32)
        tab = _rope_multipliers(rope_ref[...])
        qr = _rope(zf[:, 0:ATTN_W], tab)
        kr = _rope(zf[:, K_OFF - Q_OFF:V_OFF - Q_OFF], tab).astype(BF16)
        vb = zb[:, V_OFF - Q_OFF:Z_W - Q_OFF]
        conv_cols = [(c, c + CONV_W) for c in range(0, Q_OFF, CONV_W)]
        conv_z = []

        y_attn = []
        for j in range(nblk):
            if len(conv_z) < len(conv_cols):
                conv_z.append(project(*conv_cols[len(conv_z)]))
            rows = slice(j * BLOCK, (j + 1) * BLOCK)
            prev = slice((j - 1) * BLOCK, j * BLOCK)
            k2 = jnp.concatenate([kprev_ref[...] if j == 0 else kr[prev], kr[rows]], axis=0)
            v2 = jnp.concatenate([vprev_ref[...] if j == 0 else vb[prev], vb[rows]], axis=0)
            mask = _window_mask(jnp.minimum(i, 1) if j == 0 else 1)
            q8 = _stack_heads(qr[rows]).astype(BF16)
            probs, _ = _softmax_with_sink(q8, k2, mask, _sink_column(sink_ref))
            o8 = jnp.dot(probs.astype(BF16), v2, preferred_element_type=F32)
            chunks = [jnp.zeros((BLOCK, 128), F32) for _ in range(ATTN_W // 128)]
            for head in range(N_Q_HEADS):
                chunks[head // 2] += _from_kv_lanes(o8[head * BLOCK:(head + 1) * BLOCK], head, head // Q_PER_KV)
            y_attn.append(jnp.concatenate(chunks, axis=1))
        kprev_ref[...] = kr[tq - BLOCK:tq]
        vprev_ref[...] = vb[tq - BLOCK:tq]
        while len(conv_z) < len(conv_cols):
            conv_z.append(project(*conv_cols[len(conv_z)]))
        ya = jnp.concatenate(y_attn, axis=0).astype(BF16)
        y_ref[:, CONV_W:] = ya
        xo = xf + jnp.dot(ya, wout_ref[CONV_W:, :], preferred_element_type=F32)
        b_gate, _, _, vc, _, _, conv = _conv_terms(jnp.concatenate(conv_z, axis=1).astype(F32), cprev_ref, cw_ref)
        yc = (b_gate * conv).astype(BF16)
        cprev_ref[...] = vc[tq - 8:tq, :]
        y_ref[:, :CONV_W] = yc
        xo_ref[...] = xo + jnp.dot(yc, wout_ref[:CONV_W, :], preferred_element_type=F32)

    row = pl.BlockSpec((tq, D_MODEL), lambda i: (i, 0))
    full = lambda shape: pl.BlockSpec(shape, lambda i: (0,) * len(shape))
    return pl.pallas_call(
        _behind(body, 7, after), name=name, grid=(t // tq,),
        in_specs=[row, full((1, D_MODEL)), ANY, ANY, full((3, CONV_W)),
                  pl.BlockSpec(memory_space=pltpu.SMEM), pl.BlockSpec((2, tq, 128), lambda i: (0, i, 0))]
        + [ANY] * len(after),
        out_specs=[row, row, pl.BlockSpec((tq, Z_W), lambda i: (i, 0)), row],
        out_shape=[jax.ShapeDtypeStruct((t, D_MODEL), F32), jax.ShapeDtypeStruct((t, D_MODEL), BF16),
                   jax.ShapeDtypeStruct((t, Z_W), BF16), jax.ShapeDtypeStruct((t, D_MODEL), BF16)],
        scratch_shapes=[pltpu.VMEM((BLOCK, KV_W), BF16), pltpu.VMEM((BLOCK, KV_W), BF16),
                        pltpu.VMEM((8, CONV_W), F32), pltpu.VMEM((Z_W, D_MODEL), BF16),
                        pltpu.VMEM((D_MODEL, D_MODEL), BF16), pltpu.SemaphoreType.DMA((2, LOAD_PIECES))],
        compiler_params=_params(("arbitrary",), VMEM_LIMIT),
    )(x, gain, win_t, wout, conv_w, sinks, rope, *after)


def _mixer_bwd(dxo, x, gain, y, z, win_t, wout, conv_w, sinks, rope, *, name, after=(), tq=256):
    t = x.shape[0]
    tq = min(tq, t)
    nt, nblk = t // tq, tq // BLOCK

    def body(dxo_ref, x_ref, g_ref, y_ref, z_ref, zp_ref, win_hbm, wout_hbm, cw_ref, sink_ref, rope_ref, ropep_ref,
             dxi_ref, dz_ref, gb_ref, dcw_ref, dsink_ref, dg_ref, dk_ref, dv_ref, dcn_ref, pvc_ref,
             win_ref, wout_ref, sems):
        i = pl.program_id(0)
        tile = nt - 1 - i
        _load_resident([(win_hbm, win_ref), (wout_hbm, wout_ref)], sems)

        @pl.when(i == 0)
        def _():
            dk_ref[...] = jnp.zeros_like(dk_ref)
            dv_ref[...] = jnp.zeros_like(dv_ref)
            dcn_ref[...] = jnp.zeros_like(dcn_ref)
            dcw_ref[...] = jnp.zeros_like(dcw_ref)
            dsink_ref[...] = jnp.zeros_like(dsink_ref)
            dg_ref[...] = jnp.zeros_like(dg_ref)

        has_prev = jnp.minimum(tile, 1)
        go = dxo_ref[...]
        gb = go.astype(BF16)
        gb_ref[...] = gb
        dy = lax.dot_general(gb, wout_ref[...], NT_DIMS, preferred_element_type=F32)
        dy_conv, dy_attn = dy[:, 0:CONV_W], dy[:, CONV_W:D_MODEL]
        zb, zpb = z_ref[...], zp_ref[...]
        zf = zb.astype(F32)
        zpf = zpb.astype(F32) * has_prev.astype(F32)

        pvc_ref[...] = (zpf[:, CONV_W:2 * CONV_W] * zpf[:, 2 * CONV_W:3 * CONV_W])[BLOCK - 8:BLOCK, :]
        b_gate, c_gate, u, vc, vm1, vm2, conv = _conv_terms(zf, pvc_ref, cw_ref)
        d_bgate = dy_conv * conv
        dc = dy_conv * b_gate
        tap = lax.broadcasted_iota(jnp.int32, (8, CONV_W), 0)
        dcw_ref[...] += jnp.where(tap == 0, jnp.sum(dc * vm2, axis=0, keepdims=True),
                                  jnp.where(tap == 1, jnp.sum(dc * vm1, axis=0, keepdims=True),
                                            jnp.where(tap == 2, jnp.sum(dc * vc, axis=0, keepdims=True), 0.0)))
        dvc = (cw_ref[2:3, :] * dc + cw_ref[1:2, :] * _shift_rows_up(dc, dcn_ref, 1)
               + cw_ref[0:1, :] * _shift_rows_up(dc, dcn_ref, 2))
        dcn_ref[...] = dc[0:8, :]
        d_cgate = dvc * u
        d_u = dvc * c_gate

        tab, tabp = _rope_multipliers(rope_ref[...]), _rope_multipliers(ropep_ref[...])
        qr = _rope(zf[:, Q_OFF:K_OFF], tab)
        kr = _rope(zf[:, K_OFF:V_OFF], tab).astype(BF16)
        kpr = _rope(zpf[:, K_OFF:V_OFF], tabp).astype(BF16)
        vb, vpb = zb[:, V_OFF:Z_W], zpb[:, V_OFF:Z_W]
        out = y_ref[:, CONV_W:D_MODEL].astype(F32)
        do_out = dy_attn * out
        lane = lax.broadcasted_iota(jnp.int32, (1, 128), 1)
        dsink = jnp.zeros((1, 128), F32)
        dk_next, dv_next = dk_ref[...], dv_ref[...]
        dq_rows, dk_rows, dv_rows = [None] * nblk, [None] * nblk, [None] * nblk
        for j in reversed(range(nblk)):
            rows = slice(j * BLOCK, (j + 1) * BLOCK)
            prev = slice((j - 1) * BLOCK, j * BLOCK)
            k2 = jnp.concatenate([kpr if j == 0 else kr[prev], kr[rows]], axis=0)
            v2 = jnp.concatenate([vpb if j == 0 else vb[prev], vb[rows]], axis=0)
            mask = _window_mask(has_prev if j == 0 else 1)
            q8 = _stack_heads(qr[rows]).astype(BF16)
            do8 = _stack_heads(dy_attn[rows]).astype(BF16)
            delta = jnp.sum(_stack_heads(do_out[rows]), axis=-1, keepdims=True)
            probs, p_sink = _softmax_with_sink(q8, k2, mask, _sink_column(sink_ref))
            dp = lax.dot_general(do8, v2, NT_DIMS, preferred_element_type=F32)
            ds = (probs * (dp - delta) * SM_SCALE).astype(BF16)
            dq8 = jnp.dot(ds, k2, preferred_element_type=F32)
            dk2 = lax.dot_general(ds, q8, TN_DIMS, preferred_element_type=F32)
            dv2 = lax.dot_general(probs.astype(BF16), do8, TN_DIMS, preferred_element_type=F32)
            sink_terms = p_sink * delta
            dq_chunks = [jnp.zeros((BLOCK, 128), F32) for _ in range(ATTN_W // 128)]
            for head in range(N_Q_HEADS):
                grp = slice(head * BLOCK, (head + 1) * BLOCK)
                dq_chunks[head // 2] += _from_kv_lanes(dq8[grp], head, head // Q_PER_KV)
                dsink = dsink - jnp.where(lane == head, jnp.sum(sink_terms[grp], axis=0, keepdims=True), 0.0)
            dq_rows[j] = jnp.concatenate(dq_chunks, axis=1)
            dk_rows[j] = dk2[BLOCK:] + dk_next
            dv_rows[j] = dv2[BLOCK:] + dv_next
            dk_next, dv_next = dk2[:BLOCK], dv2[:BLOCK]
        dk_ref[...] = dk_next
        dv_ref[...] = dv_next
        dsink_ref[...] += dsink
        dq = _rope_bwd(jnp.concatenate(dq_rows, axis=0), tab)
        dk = _rope_bwd(jnp.concatenate(dk_rows, axis=0), tab)
        dv = jnp.concatenate(dv_rows, axis=0)

        dzb = jnp.concatenate([d_bgate, d_cgate, d_u, dq, dk, dv], axis=1).astype(BF16)
        dz_ref[...] = dzb
        dh = jnp.dot(dzb, win_ref[...], preferred_element_type=F32)
        xhat, inv = _rms_stats(x_ref[...])
        dx, dgain = _rms_bwd(dh, xhat, inv, g_ref[...])
        dxi_ref[...] = go + dx
        dg_ref[...] += dgain

    rev = lambda i: (nt - 1 - i, 0)
    block_before = lambda i: jnp.maximum((nt - 1 - i) * nblk - 1, 0)
    row = pl.BlockSpec((tq, D_MODEL), rev)
    full = lambda shape: pl.BlockSpec(shape, lambda i: (0,) * len(shape))
    return pl.pallas_call(
        _behind(body, 12, after), name=name, grid=(nt,),
        in_specs=[row, row, full((1, D_MODEL)), row,
                  pl.BlockSpec((tq, Z_W), rev), pl.BlockSpec((BLOCK, Z_W), lambda i: (block_before(i), 0)),
                  ANY, ANY, full((3, CONV_W)),
                  pl.BlockSpec(memory_space=pltpu.SMEM),
                  pl.BlockSpec((2, tq, 128), lambda i: (0, nt - 1 - i, 0)),
                  pl.BlockSpec((2, BLOCK, 128), lambda i: (0, block_before(i), 0))] + [ANY] * len(after),
        out_specs=[row, pl.BlockSpec((tq, Z_W), rev), row, full((8, CONV_W)), full((1, 128)), full((1, D_MODEL))],
        out_shape=[jax.ShapeDtypeStruct((t, D_MODEL), F32), jax.ShapeDtypeStruct((t, Z_W), BF16),
                   jax.ShapeDtypeStruct((t, D_MODEL), BF16), jax.ShapeDtypeStruct((8, CONV_W), F32),
                   jax.ShapeDtypeStruct((1, 128), F32), jax.ShapeDtypeStruct((1, D_MODEL), F32)],
        scratch_shapes=[pltpu.VMEM((BLOCK, KV_W), F32), pltpu.VMEM((BLOCK, KV_W), F32), pltpu.VMEM((8, CONV_W), F32),
                        pltpu.VMEM((8, CONV_W), F32), pltpu.VMEM((Z_W, D_MODEL), BF16),
                        pltpu.VMEM((D_MODEL, D_MODEL), BF16), pltpu.SemaphoreType.DMA((2, LOAD_PIECES))],
        compiler_params=_params(("arbitrary",), VMEM_LIMIT),
    )(dxo, x, gain, y, z, z, win_t, wout, conv_w, sinks, rope, rope, *after)


def _place():
    x, y, c = lax.axis_index("x"), lax.axis_index("y"), lax.axis_index("c")
    other_chips = [(1 - x, y), (x, 1 - y), (1 - x, 1 - y)]
    return x, y, c, other_chips


def _all_gather_rows(shards, place=(), *, name):
    n, p = len(shards), len(place)

    def body(*refs):
        srcs, place_srcs = refs[:n], refs[n:n + p]
        outs, place_outs = refs[n + p:2 * n + p], refs[2 * n + p:2 * (n + p)]
        send_sems, recv_sems, local_sems = refs[2 * (n + p):]
        x, y, c, chips = _place()
        me, sibling = (x, y, c), (x, y, 1 - c)

        def rows(t, px, py, pc):
            r = srcs[t].shape[-2]
            start = pl.multiple_of((4 * px + 2 * py + pc) * r, 16 if r % 16 == 0 else 8)
            if len(srcs[t].shape) == 3:
                return outs[t].at[:, pl.ds(start, r), :]
            return outs[t].at[pl.ds(start, r), :]

        def copy(t, k, block, to, own=False):
            return pltpu.make_async_remote_copy(
                src_ref=srcs[t] if own else rows(t, *block), dst_ref=rows(t, *block),
                send_sem=send_sems.at[t, k], recv_sem=recv_sems.at[t, k], device_id=to, device_id_type=MESH)

        mine = [pltpu.make_async_copy(srcs[t], rows(t, *me), local_sems.at[t]) for t in range(n)]
        mine += [pltpu.make_async_copy(place_srcs[q],
                                       _block_rows(place_outs[q], place_srcs[q].shape[-2], 4 * x + 2 * y + c),
                                       local_sems.at[n + q]) for q in range(p)]
        for q in range(p):
            mine[n + q].start()
        first = []
        for t in range(n):
            mine[t].start()
            first.append(copy(t, 0, me, sibling, own=True))
            first += [copy(t, 1 + j, me, (*chip, c), own=True) for j, chip in enumerate(chips)]
        for cp in first:
            cp.start()
        passed = []
        for j, chip in enumerate(chips):
            for t in range(n):
                copy(t, 1 + j, (*chip, c), me).wait_recv()
                fwd = copy(t, 4 + j, (*chip, c), sibling)
                fwd.start()
                passed.append(fwd)
        for t in range(n):
            copy(t, 0, sibling, me).wait_recv()
            for j, chip in enumerate(chips):
                copy(t, 4 + j, (*chip, 1 - c), me).wait_recv()
        for cp in first + passed:
            cp.wait_send()
        for cp in mine:
            cp.wait()

    out_shape = [jax.ShapeDtypeStruct(s.shape[:-2] + (N_DEV * s.shape[-2], s.shape[-1]), s.dtype)
                 for s in list(shards) + list(place)]
    res = pl.pallas_call(
        body, name=name, in_specs=[ANY] * (n + p), out_specs=[ANY] * (n + p), out_shape=out_shape,
        scratch_shapes=[pltpu.SemaphoreType.DMA((n, 7)), pltpu.SemaphoreType.DMA((n, 7)),
                        pltpu.SemaphoreType.DMA((n + p,))],
    )(*shards, *place)
    return res[:n], res[n:]


def _split_start(bufs, n_copies, plan, *, name, after=()):
    n = len(bufs)

    def body(*refs):
        token = refs[-1]
        for cp in plan(refs[:n], refs[n], refs[n + 1]):
            cp.start()
        token[...] = jnp.zeros_like(token)

    res = pl.pallas_call(
        _behind(body, n, after), name=name, in_specs=[HBM_SPEC] * n + [ANY] * len(after),
        out_specs=(SEM_SPEC, SEM_SPEC, *[HBM_SPEC] * n, pl.BlockSpec(memory_space=pltpu.VMEM)),
        out_shape=(pltpu.SemaphoreType.DMA((n_copies,)), pltpu.SemaphoreType.DMA((n_copies,)),
                   *[pltpu.HBM(b.shape, b.dtype) for b in bufs], jax.ShapeDtypeStruct((8, 128), F32)),
        input_output_aliases={i: 2 + i for i in range(n)},
        compiler_params=pltpu.CompilerParams(has_side_effects=DATAFLOW),
    )(*[pltpu.with_memory_space_constraint(b, pltpu.HBM) for b in bufs], *after)
    return res[0], res[1], list(res[2:2 + n]), res[-1]


def _split_wait(send_sems, recv_sems, bufs, after, plan, *, name):
    n = len(bufs)

    def body(*refs):
        for cp in plan(refs[:n], refs[n], refs[n + 1]):
            cp.wait_send()
            cp.wait_recv()

    return list(pl.pallas_call(
        body, name=name, in_specs=[HBM_SPEC] * n + [SEM_SPEC, SEM_SPEC, ANY], out_specs=[HBM_SPEC] * n,
        out_shape=tuple(pltpu.HBM(b.shape, b.dtype) for b in bufs),
        input_output_aliases={i: i for i in range(n)},
        compiler_params=pltpu.CompilerParams(has_side_effects=DATAFLOW),
    )(*bufs, send_sems, recv_sems, after))


def _sibling_plan(n):
    def plan(bufs, send_sems, recv_sems):
        x, y, c, _ = _place()
        return [pltpu.make_async_remote_copy(
            src_ref=bufs[t].at[:, 1 - c], dst_ref=bufs[n + t], send_sem=send_sems.at[t], recv_sem=recv_sems.at[t],
            device_id=(x, y, 1 - c), device_id_type=MESH) for t in range(n)]
    return plan


def _block_rows(ref, r, blk):
    start = pl.multiple_of(blk * r, 16 if r % 16 == 0 else 8)
    return ref.at[(slice(None),) * (len(ref.shape) - 2) + (pl.ds(start, r), slice(None))]


def _remote(src, dst, send_sems, recv_sems, k, peer):
    return pltpu.make_async_remote_copy(src_ref=src, dst_ref=dst, send_sem=send_sems.at[k], recv_sem=recv_sems.at[k],
                                        device_id=peer, device_id_type=MESH)


def _gather_send_plan(n, first=0):
    def plan(bufs, send_sems, recv_sems):
        x, y, c, chips = _place()
        peers = [(x, y, 1 - c)] + [(px, py, c) for px, py in chips]
        copies = []
        for t in range(n):
            dst = _block_rows(bufs[n + t], bufs[t].shape[-2], 4 * x + 2 * y + c)
            copies += [_remote(bufs[t], dst, send_sems, recv_sems, 4 * (first + t) + k, peer)
                       for k, peer in enumerate(peers)]
        return copies
    return plan


def _gather_forward_plan(rows):
    def plan(bufs, send_sems, recv_sems):
        x, y, c, chips = _place()
        copies = []
        for t, r in enumerate(rows):
            for j, (px, py) in enumerate(chips):
                blk = _block_rows(bufs[t], r, 4 * px + 2 * py + c)
                copies.append(_remote(blk, blk, send_sems, recv_sems, 3 * t + j, (x, y, 1 - c)))
        return copies
    return plan


def _chips_plan(n, with_small):
    def plan(bufs, send_sems, recv_sems):
        x, y, c, chips = _place()
        copies = []
        for t in range(n):
            for j, (px, py) in enumerate(chips):
                copies.append(_remote(bufs[t].at[2 * px + py], bufs[n + t].at[j], send_sems, recv_sems, 3 * t + j,
                                      (px, py, c)))
        if with_small:
            mine = _block_rows(bufs[2 * n], 8, 4 * x + 2 * y + c)
            flips = [(fx, fy, fc) for fx in range(2) for fy in range(2) for fc in range(2)][1:]
            for k, (fx, fy, fc) in enumerate(flips):
                peer = (x + fx - 2 * x * fx, y + fy - 2 * y * fy, c + fc - 2 * c * fc)
                copies.append(_remote(mine, mine, send_sems, recv_sems, 3 * n + k, peer))
        return copies
    return plan


def _place_own(fulls, shards, index, *, name):
    n = len(fulls)

    def body(index_ref, *refs):
        for t in range(n):
            refs[2 * n + t][...] = refs[n + t][...]

    def block_of(shard):
        lead = len(shard.shape) - 2
        return pl.BlockSpec(shard.shape, lambda i, index_ref: (0,) * lead + (index_ref[0], 0))

    def whole(shard):
        return pl.BlockSpec(shard.shape, lambda i, index_ref: (0,) * len(shard.shape))

    return list(pl.pallas_call(
        body, name=name,
        grid_spec=pltpu.PrefetchScalarGridSpec(
            num_scalar_prefetch=1, grid=(1,),
            in_specs=[ANY] * n + [whole(s) for s in shards], out_specs=[block_of(s) for s in shards]),
        out_shape=[jax.ShapeDtypeStruct(f.shape, f.dtype) for f in fulls],
        input_output_aliases={1 + t: t for t in range(n)},
        compiler_params=_params(("arbitrary",)),
    )(index, *fulls, *shards))


N_STEPS_SMALL = 2


def _add_sibling(grads, recvs, place, *, name):
    n = len(grads)

    def body(place_ref, *refs):
        chip = place_ref[1]
        for t in range(n):
            g_ref, r_ref, own_ref, ob_ref = refs[2 * t], refs[2 * t + 1], refs[2 * n + 2 * t], refs[2 * n + 2 * t + 1]
            own = jnp.zeros(own_ref.shape, F32)
            for m in range(4):
                p = g_ref[m, 0] + r_ref[m]
                ob_ref[m] = p.astype(BF16)
                own = jnp.where(chip == m, p, own)
            own_ref[...] = own

    in_specs, out_specs, out_shape = [], [], []
    for g, r in zip(grads, recvs):
        tr = g.shape[2] // N_STEPS_SMALL
        blocks = pl.BlockSpec((4, tr, D_MODEL), lambda i, place_ref: (0, i, 0))
        in_specs += [pl.BlockSpec((4, 1, tr, D_MODEL), lambda i, place_ref: (0, place_ref[0], i, 0)), blocks]
        out_specs += [pl.BlockSpec((tr, D_MODEL), lambda i, place_ref: (i, 0)), blocks]
        out_shape += [jax.ShapeDtypeStruct(r.shape[1:], F32), jax.ShapeDtypeStruct(r.shape, BF16)]
    res = pl.pallas_call(
        body, name=name,
        grid_spec=pltpu.PrefetchScalarGridSpec(num_scalar_prefetch=1, grid=(N_STEPS_SMALL,), in_specs=in_specs,
                                               out_specs=out_specs),
        out_shape=out_shape, compiler_params=_params(("arbitrary",), VMEM_LIMIT),
    )(place, *[a for pair in zip(grads, recvs) for a in pair])
    return [(res[2 * t], res[2 * t + 1]) for t in range(n)]


def _reduce_adamw(parts, *, name, after=()):
    n = len(parts)

    def body(*refs):
        for t in range(n):
            p_ref, r_ref, w_ref, m_ref, v_ref = refs[5 * t:5 * t + 5]
            g_ref, d_ref, mo_ref, vo_ref = refs[5 * n + 4 * t:5 * n + 4 * t + 4]
            g = p_ref[...] + r_ref[0].astype(F32) + r_ref[1].astype(F32) + r_ref[2].astype(F32)
            g_ref[...] = g
            d_ref[...], mo_ref[...], vo_ref[...] = _adamw_math(w_ref[...], g, m_ref[...], v_ref[...])
        refs[-1][...] = jnp.zeros_like(refs[-1])

    in_specs, out_specs, out_shape = [], [], []
    for own, _, _, _, _ in parts:
        rows = own.shape[0]
        tr = rows // N_STEPS_SMALL
        spec = pl.BlockSpec((tr, D_MODEL), lambda i: (i, 0))
        in_specs += [spec, pl.BlockSpec((3, tr, D_MODEL), lambda i: (0, i, 0)), spec, spec, spec]
        out_specs += [spec] * 4
        out_shape += [jax.ShapeDtypeStruct((rows, D_MODEL), F32)] * 4
    res = pl.pallas_call(
        _behind(body, 5 * n, after), name=name, grid=(N_STEPS_SMALL,),
        in_specs=in_specs + [ANY] * len(after),
        out_specs=out_specs + [pl.BlockSpec((8, 128), lambda i: (0, 0))],
        out_shape=out_shape + [jax.ShapeDtypeStruct((8, 128), F32)],
        compiler_params=_params(("arbitrary",), VMEM_LIMIT),
    )(*[a for part in parts for a in part], *after)
    return [tuple(res[4 * t:4 * t + 4]) for t in range(n)], res[-1]


def _adamw_math(w, g, m, v):
    m = ADAM_B1 * m + (1.0 - ADAM_B1) * g
    v = ADAM_B2 * v + (1.0 - ADAM_B2) * (g * g)
    m_hat = m / (1.0 - ADAM_B1 ** ADAM_STEP)
    v_hat = v / (1.0 - ADAM_B2 ** ADAM_STEP)
    delta = -ADAM_LR * (m_hat / (jnp.sqrt(v_hat) + ADAM_EPS) + ADAM_WD * w)
    return delta, m, v


SMALL_NAMES = ["ffn1_norm", "mix_norm", "ffn2_norm", "final_norm", "conv_w", "attn_sinks"]


def _update_small(given, moments_m, moments_v, small_all, my_index, *, name):
    conv_cols = given["conv_w"].shape[2]
    per_block = 128 // conv_cols

    def two_d(nm, a):
        return a.reshape(1, D_MODEL) if nm == "final_norm" else a

    operands = [two_d(nm, src[nm]) for nm in SMALL_NAMES for src in (given, moments_m, moments_v)]
    n = len(SMALL_NAMES)

    def body(index_ref, all_ref, conv_ref, *refs):
        ins, outs = refs[:3 * n], refs[3 * n:]
        total, conv_total = all_ref[0], conv_ref[0]
        for k in range(1, N_DEV):
            total, conv_total = total + all_ref[k], conv_total + conv_ref[k]
        which = index_ref[0] % per_block
        conv_g = conv_total[4:7, :conv_cols]
        for j in range(1, per_block):
            conv_g = jnp.where(which == j, conv_total[4:7, j * conv_cols:(j + 1) * conv_cols], conv_g)
        grads = [total[0:1], total[1:2], total[2:3], total[3:4], conv_g[None], total[7:8, :N_Q_HEADS]]
        for t, g in enumerate(grads):
            w_ref, m_ref, v_ref = ins[3 * t:3 * t + 3]
            g_ref, d_ref, mo_ref, vo_ref = outs[4 * t:4 * t + 4]
            g_ref[...] = g
            d_ref[...], mo_ref[...], vo_ref[...] = _adamw_math(w_ref[...], g, m_ref[...], v_ref[...])
        outs[-1][...] = total[7:8, LOSS_LANE:LOSS_LANE + 1]

    def whole(shape):
        return pl.BlockSpec(shape, lambda i, index_ref: (0,) * len(shape))

    shapes = [a.shape for a in operands[::3] for _ in range(4)] + [(1, 1)]
    res = pl.pallas_call(
        body, name=name,
        grid_spec=pltpu.PrefetchScalarGridSpec(
            num_scalar_prefetch=1, grid=(1,),
            in_specs=[whole(small_all.shape),
                      pl.BlockSpec((N_DEV, 8, 128), lambda i, index_ref: (0, 0, index_ref[0] // per_block))]
            + [whole(a.shape) for a in operands],
            out_specs=[whole(s) for s in shapes]),
        out_shape=[jax.ShapeDtypeStruct(s, F32) for s in shapes],
        compiler_params=_params(("arbitrary",)),
    )(my_index.astype(jnp.int32).reshape(1), small_all, small_all, *operands)
    results = {nm: tuple(a.reshape(given[nm].shape) for a in res[4 * t:4 * t + 4]) for t, nm in enumerate(SMALL_NAMES)}
    return results, res[-1]


def kernel(x, ffn1_norm, ffn1_w_gate, ffn1_w_up, ffn1_w_down, mix_norm, w_in, conv_w, attn_sinks, w_out, ffn2_norm, ffn2_w_gate, ffn2_w_up, ffn2_w_down, final_norm, loss_target, m_ffn1_norm, m_ffn1_w_gate, m_ffn1_w_up, m_ffn1_w_down, m_mix_norm, m_w_in, m_conv_w, m_attn_sinks, m_w_out, m_ffn2_norm, m_ffn2_w_gate, m_ffn2_w_up, m_ffn2_w_down, m_final_norm, v_ffn1_norm, v_ffn1_w_gate, v_ffn1_w_up, v_ffn1_w_down, v_mix_norm, v_w_in, v_conv_w, v_attn_sinks, v_w_out, v_ffn2_norm, v_ffn2_w_gate, v_ffn2_w_up, v_ffn2_w_down, v_final_norm):
    ix, iy, ic = lax.axis_index("x"), lax.axis_index("y"), lax.axis_index("c")
    my_index = 4 * ix + 2 * iy + ic
    place = jnp.stack([ic, 2 * ix + iy]).astype(jnp.int32)

    given = dict(ffn1_norm=ffn1_norm, ffn1_w_gate=ffn1_w_gate, ffn1_w_up=ffn1_w_up, ffn1_w_down=ffn1_w_down,
                 mix_norm=mix_norm, w_in=w_in, conv_w=conv_w, attn_sinks=attn_sinks, w_out=w_out, ffn2_norm=ffn2_norm,
                 ffn2_w_gate=ffn2_w_gate, ffn2_w_up=ffn2_w_up, ffn2_w_down=ffn2_w_down, final_norm=final_norm)
    moments_m = dict(ffn1_norm=m_ffn1_norm, ffn1_w_gate=m_ffn1_w_gate, ffn1_w_up=m_ffn1_w_up, ffn1_w_down=m_ffn1_w_down,
                     mix_norm=m_mix_norm, w_in=m_w_in, conv_w=m_conv_w, attn_sinks=m_attn_sinks, w_out=m_w_out,
                     ffn2_norm=m_ffn2_norm, ffn2_w_gate=m_ffn2_w_gate, ffn2_w_up=m_ffn2_w_up, ffn2_w_down=m_ffn2_w_down,
                     final_norm=m_final_norm)
    moments_v = dict(ffn1_norm=v_ffn1_norm, ffn1_w_gate=v_ffn1_w_gate, ffn1_w_up=v_ffn1_w_up, ffn1_w_down=v_ffn1_w_down,
                     mix_norm=v_mix_norm, w_in=v_w_in, conv_w=v_conv_w, attn_sinks=v_attn_sinks, w_out=v_w_out,
                     ffn2_norm=v_ffn2_norm, ffn2_w_gate=v_ffn2_w_gate, ffn2_w_up=v_ffn2_w_up, ffn2_w_down=v_ffn2_w_down,
                     final_norm=v_final_norm)

    xs = x[0]
    target = loss_target[0]
    final_gain = final_norm.reshape(1, D_MODEL)

    def ffn_shards(wg, wu, wd):
        return jnp.stack([wg[0].T, wu[0].T]).astype(BF16), wd[0].astype(BF16)

    conv_cols = conv_w.shape[2]
    conv_shard = jnp.pad(conv_w[0], ((0, 5), (0, 128 - conv_cols)))
    gate_up1, down1 = ffn_shards(ffn1_w_gate, ffn1_w_up, ffn1_w_down)
    rest_shards = [down1, w_in[0].T.astype(BF16), w_out[0].astype(BF16), conv_shard,
                   *ffn_shards(ffn2_w_gate, ffn2_w_up, ffn2_w_down)]
    rest_rows = [s.shape[-2] for s in rest_shards]
    n_rest, n_early = len(rest_shards), 4
    (w1_gu,), _ = _all_gather_rows([gate_up1], name="gather_ffn1")

    fulls = [lax.empty(s.shape[:-2] + (N_DEV * s.shape[-2], s.shape[-1]), s.dtype) for s in rest_shards]
    fulls = _place_own(fulls, rest_shards, my_index.astype(jnp.int32).reshape(1), name="place_own_weights")
    ssem, rsem, bufs, token = _split_start(rest_shards + list(fulls), 4 * n_rest, _gather_send_plan(n_rest),
                                           name="gather_rest_start", after=[w1_gu])
    early = bufs[:n_early] + bufs[n_rest:n_rest + n_early]
    late = bufs[n_early:n_rest] + bufs[n_rest + n_early:]
    h1, s1, sa1, sb1 = _ffn_fwd(xs, ffn1_norm, w1_gu, name="ffn1_hidden", after=[token])
    early = _split_wait(ssem, rsem, early, h1, _gather_send_plan(n_early), name="gather_early_wait")
    fwd_early = _gather_forward_plan(rest_rows[:n_early])
    ssem_e, rsem_e, parts, token = _split_start(early[n_early:], 3 * n_early, fwd_early, name="forward_early_start")
    rope = _rope_tables(xs.shape[0], after=[token])
    w1_d, win_t, wout, conv_all = _split_wait(ssem_e, rsem_e, parts, rope, fwd_early, name="forward_early_wait")
    x1 = _ffn_down(xs, s1, w1_d, name="ffn1_down")
    conv_full = conv_all.reshape(N_DEV, 8, 128)[:, :3, :conv_cols].transpose(1, 0, 2).reshape(3, CONV_W)
    late = _split_wait(ssem, rsem, late, x1, _gather_send_plan(n_rest - n_early, first=n_early),
                       name="gather_late_wait")
    fwd_ffn2 = _gather_forward_plan(rest_rows[n_early:])
    ssem, rsem, parts, token = _split_start(late[n_rest - n_early:], 3 * (n_rest - n_early), fwd_ffn2,
                                            name="forward_ffn2_start")
    x2, hm, z, y = _mixer_fwd(x1, mix_norm, win_t, wout, conv_full, attn_sinks, rope, name="mixer_fwd", after=[token])
    w2_gu, w2_d = _split_wait(ssem, rsem, parts, x2, fwd_ffn2, name="forward_ffn2_wait")
    dx3, h2, s2, sa2, sb2, loss_local, d_final = _ffn_fwd(x2, ffn2_norm, w2_gu, w2_d, head=(final_gain, target),
                                                          name="ffn2_fwd")

    def to_sibling_start(grads, tag, after=()):
        views = [g.reshape(4, 2, g.shape[0] // N_DEV, D_MODEL) for g in grads]
        lands = [lax.empty((4,) + v.shape[2:], F32) for v in views]
        plan = _sibling_plan(len(views))
        ssem, rsem, bufs, token = _split_start(views + lands, len(views), plan, name=f"{tag}_sibling_start", after=after)
        return (ssem, rsem, bufs, plan, tag), token

    def to_sibling_finish(handle, after, names):
        ssem, rsem, bufs, plan, tag = handle
        bufs = _split_wait(ssem, rsem, bufs, after, plan, name=f"{tag}_sibling_wait")
        n = len(names)
        return _add_sibling(bufs[:n], bufs[n:], place, name=f"add_sibling_{tag}")

    def to_chips_start(partials, tag, small_all=None, after=()):
        p16 = [p for _, p in partials]
        lands = [lax.empty((3,) + p.shape[1:], BF16) for p in p16]
        extra = [] if small_all is None else [small_all]
        plan = _chips_plan(len(p16), small_all is not None)
        ssem, rsem, bufs, token = _split_start(p16 + lands + extra, 3 * len(p16) + 7 * len(extra), plan,
                                               name=f"{tag}_chips_start", after=after)
        return (ssem, rsem, bufs, plan, tag), token

    def to_chips_finish(handle, partials, after, names):
        ssem, rsem, bufs, plan, tag = handle
        bufs = _split_wait(ssem, rsem, bufs, after, plan, name=f"{tag}_chips_wait")
        n = len(names)
        return [(p32, r) for (p32, _), r in zip(partials, bufs[n:2 * n])], bufs[2 * n:]

    half_ff = D_FF // 2
    names2, namesm = ["ffn2_w_gate", "ffn2_w_up", "ffn2_w_down"], ["w_in", "w_out"]
    transposed = {"ffn1_w_gate", "ffn1_w_up", "w_in", "ffn2_w_gate", "ffn2_w_up"}
    grad, delta, new_m, new_v = {}, {}, {}, {}

    def adam_big(names, parts, tag, after=()):
        def to_rows(nm, a):
            return jnp.swapaxes(a, 1, 2)[0] if nm in transposed else a[0]

        def from_rows(nm, a):
            return jnp.swapaxes(a[None], 1, 2) if nm in transposed else a[None]

        operands = [(p32, recv, to_rows(nm, given[nm]), to_rows(nm, moments_m[nm]), to_rows(nm, moments_v[nm]))
                    for nm, (p32, recv) in zip(names, parts)]
        results, token = _reduce_adamw(operands, name=f"adamw_{tag}", after=after)
        for nm, outs in zip(names, results):
            grad[nm], delta[nm], new_m[nm], new_v[nm] = (from_rows(nm, a) for a in outs)
        return token

    dx2, da2, db2, g2b, d_norm2 = _ffn_dgrad(dx3, x2, ffn2_norm, sa2, sb2, w2_gu, w2_d, name="ffn2_dgrad")
    gw2 = [_tn_matmul(da2, h2, name="ffn2_wgrad_gate", bm=half_ff), _tn_matmul(db2, h2, name="ffn2_wgrad_up", bm=half_ff),
           _tn_matmul(s2, g2b, name="ffn2_wgrad_down", bm=half_ff)]
    sib2, tok = to_sibling_start(gw2, "ffn2")
    dx1, dz, gmb, d_conv, d_sink, d_normm = _mixer_bwd(dx2, x1, mix_norm, y, z, win_t, wout, conv_full, attn_sinks,
                                                       rope, name="mixer_bwd", after=[tok])
    p2 = to_sibling_finish(sib2, dx1, names2)
    chips2, tok = to_chips_start(p2, "ffn2")
    gwm = [_tn_matmul(dz, hm, name="mixer_wgrad_in", bm=Z_W // 2, after=[tok]),
           _tn_matmul(y, gmb, name="mixer_wgrad_out", bm=D_MODEL, after=[tok])]
    sibm, tok = to_sibling_start(gwm, "mixer")
    dx0, da1, db1, g1b, d_norm1 = _ffn_dgrad(dx1, xs, ffn1_norm, sa1, sb1, w1_gu, w1_d, name="ffn1_dgrad", after=[tok])
    r2, _ = to_chips_finish(chips2, p2, dx0, names2)
    pm = to_sibling_finish(sibm, dx0, namesm)
    chipsm, tok = to_chips_start(pm, "mixer")
    gw_gate = _tn_matmul(da1, h1, name="ffn1_wgrad_gate", bm=half_ff, after=[tok])
    sib_gate, tok = to_sibling_start([gw_gate], "ffn1_gate")
    gw_up = _tn_matmul(db1, h1, name="ffn1_wgrad_up", bm=half_ff, after=[tok])
    rm, _ = to_chips_finish(chipsm, pm, gw_up, namesm)
    p_gate = to_sibling_finish(sib_gate, gw_up, ["ffn1_w_gate"])
    chips_gate, tok_a = to_chips_start(p_gate, "ffn1_gate")
    sib_up, tok_b = to_sibling_start([gw_up], "ffn1_up", after=[tok_a])
    gw_down = _tn_matmul(s1, g1b, name="ffn1_wgrad_down_first", bm=half_ff, blocks=(0, 1), after=[tok_a, tok_b])
    p_up = to_sibling_finish(sib_up, gw_down, ["ffn1_w_up"])
    chips_up, tok_a = to_chips_start(p_up, "ffn1_up")
    gw_down = _tn_matmul(s1, g1b, name="ffn1_wgrad_down_second", bm=half_ff, blocks=(1, 1), into=gw_down, after=[tok_a])
    sib_down, tok_b = to_sibling_start([gw_down], "ffn1_down")
    p_down = to_sibling_finish(sib_down, tok_b, ["ffn1_w_down"])
    last_row = (jnp.pad(d_sink, ((0, 0), (0, D_MODEL - 128)))
                + jnp.pad(loss_local, ((0, 0), (LOSS_LANE, D_MODEL - LOSS_LANE - 1))))
    small = jnp.concatenate([
        d_norm1, d_normm, d_norm2, d_final, jnp.pad(d_conv[0:3], ((0, 0), (0, D_MODEL - CONV_W))), last_row], axis=0)
    (small_all,) = _place_own([lax.empty((N_DEV * 8, D_MODEL), F32)], [small], my_index.astype(jnp.int32).reshape(1),
                              name="place_own_small")
    chips_down, tok = to_chips_start(p_down, "ffn1_down", small_all)
    tok = adam_big(names2, r2, "ffn2", after=[tok])
    tok = adam_big(namesm, rm, "mixer", after=[tok])
    r_gate, _ = to_chips_finish(chips_gate, p_gate, tok, ["ffn1_w_gate"])
    tok = adam_big(["ffn1_w_gate"], r_gate, "ffn1_gate")
    r_up, _ = to_chips_finish(chips_up, p_up, tok, ["ffn1_w_up"])
    tok = adam_big(["ffn1_w_up"], r_up, "ffn1_up")
    r_down, (small_all,) = to_chips_finish(chips_down, p_down, tok, ["ffn1_w_down"])
    adam_big(["ffn1_w_down"], r_down, "ffn1_down")
    results, loss = _update_small(given, moments_m, moments_v, small_all.reshape(N_DEV, 8, D_MODEL), my_index,
                                  name="update_small")
    for nm, outs in results.items():
        grad[nm], delta[nm], new_m[nm], new_v[nm] = outs

    order = list(given)
    return (loss.reshape(()), dx0[None], *[grad[n] for n in order], *[delta[n] for n in order],
            *[new_m[n] for n in order], *[new_v[n] for n in order])
```

```python
import functools

import jax
import jax.numpy as jnp
from jax import lax
from jax.experimental import pallas as pl
from jax.experimental.pallas import tpu as pltpu

F32 = jnp.float32
BF16 = jnp.bfloat16
MESH = pl.DeviceIdType.MESH
ANY = pl.BlockSpec(memory_space=pl.ANY)
HBM_SPEC = pl.BlockSpec(memory_space=pltpu.HBM)
SEM_SPEC = pl.BlockSpec(memory_space=pltpu.SEMAPHORE)
DATAFLOW = pltpu.SideEffectType.DATAFLOW_SIDE_EFFECTING

N_DEV = 8
LOSS_LANE = 128
D_MODEL = 1024
D_FF = 2816
CONV_W = 512
ATTN_W = 512
KV_W = 128
HEAD_DIM = 64
N_Q_HEADS = 8
N_KV_HEADS = 2
Q_PER_KV = N_Q_HEADS // N_KV_HEADS
BLOCK = 128
ROT_DIM = 16
ROPE_THETA = 500000.0
Z_W = 3 * CONV_W + ATTN_W + 2 * KV_W
Q_OFF = 3 * CONV_W
K_OFF = Q_OFF + ATTN_W
V_OFF = K_OFF + KV_W
RMS_EPS = 1e-5
MASK_VALUE = -1e30
SM_SCALE = HEAD_DIM ** -0.5
FFN_RES_SCALE = 0.5

ADAM_LR = 0.001
ADAM_B1 = 0.9
ADAM_B2 = 0.999
ADAM_EPS = 1e-08
ADAM_WD = 0.01
ADAM_STEP = 10

NT_DIMS = (((1,), (1,)), ((), ()))
TN_DIMS = (((0,), (0,)), ((), ()))

VMEM_LIMIT = 62 * 1024 * 1024
FF_CHUNK = 256


def _params(sem, vmem=None):
    return pltpu.CompilerParams(dimension_semantics=sem, vmem_limit_bytes=vmem)


def _behind(body, n_in, after):
    k = len(after)
    if k == 0:
        return body
    return lambda *refs: body(*refs[:n_in], *refs[n_in + k:])


def _rms_stats(xf):
    inv = lax.rsqrt(jnp.mean(xf * xf, axis=-1, keepdims=True) + RMS_EPS)
    return xf * inv, inv


def _rms_bwd(dh, xhat, inv, gain):
    dxhat = dh * gain
    dx = inv * (dxhat - xhat * jnp.mean(dxhat * xhat, axis=-1, keepdims=True))
    dgain = jnp.sum(dh * xhat, axis=0, keepdims=True)
    return dx, dgain


LOAD_PIECES = 4


def _load_resident(pairs, sems):
    @pl.when(pl.program_id(0) == 0)
    def _():
        copies = []
        for k, (w_hbm, w_ref) in enumerate(pairs):
            rows = w_hbm.shape[-2] // LOAD_PIECES
            for p in range(LOAD_PIECES):
                piece = (slice(None),) * (len(w_hbm.shape) - 2) + (pl.ds(p * rows, rows), slice(None))
                copies.append(pltpu.make_async_copy(w_hbm.at[piece], w_ref.at[piece], sems.at[k, p]))
        for cp in copies:
            cp.start()
        for cp in copies:
            cp.wait()


def _ffn_fwd(x, gain, w_gu, w_d=None, *, name, head=None, after=(), tm=256, sub=256, tf=FF_CHUNK):
    t = x.shape[0]
    tm = min(tm, t)
    sub = min(sub, tm)
    n_down = 0 if w_d is None else 1
    n_head = 0 if head is None else 2
    assert n_down or not n_head
    n_in = 3 + n_down + n_head

    def body(*refs):
        x_ref, g_ref = refs[:2]
        w_hbms, head_refs = refs[2:3 + n_down], refs[3 + n_down:n_in]
        xo_refs = refs[n_in:n_in + n_down]
        h_ref, s_ref, sa_ref, sb_ref = refs[n_in + n_down:n_in + n_down + 4]
        head_outs = refs[n_in + n_down + 4:n_in + n_down + 4 + n_head]
        w_refs, sems = refs[n_in + n_down + 4 + n_head:-1], refs[-1]
        _load_resident(list(zip(w_hbms, w_refs)), sems)

        @pl.when(pl.program_id(0) == 0)
        def _():
            for ref in head_outs:
                ref[...] = jnp.zeros_like(ref)

        for r0 in range(0, tm, sub):
            rows = slice(r0, r0 + sub)
            xf = x_ref[rows, :]
            xhat, _ = _rms_stats(xf)
            h = (xhat * g_ref[...]).astype(BF16)
            h_ref[rows, :] = h
            for c in range(0, D_FF, tf):
                cols = slice(c, min(c + tf, D_FF))
                a = lax.dot_general(h, w_refs[0][0, cols, :], NT_DIMS, preferred_element_type=F32)
                b = lax.dot_general(h, w_refs[0][1, cols, :], NT_DIMS, preferred_element_type=F32)
                sig = jax.nn.sigmoid(a)
                silu = a * sig
                s_ref[rows, cols] = (silu * b).astype(BF16)
                sa_ref[rows, cols] = (b * (sig * (1.0 + a * (1.0 - sig)))).astype(BF16)
                sb_ref[rows, cols] = silu.astype(BF16)
            if not n_down:
                continue
            xo = xf + FFN_RES_SCALE * jnp.dot(s_ref[rows, :], w_refs[1][...], preferred_element_type=F32)
            if head is None:
                xo_refs[0][rows, :] = xo
            else:
                fg_ref, t_ref = head_refs
                loss_ref, dfg_ref = head_outs
                xhat_o, inv_o = _rms_stats(xo)
                err = xhat_o * fg_ref[...] - t_ref[rows, :]
                loss_ref[...] += 0.5 * jnp.sum(jnp.mean(err * err, axis=-1, keepdims=True), axis=0, keepdims=True)
                xo_refs[0][rows, :], dfg = _rms_bwd(err * (1.0 / D_MODEL), xhat_o, inv_o, fg_ref[...])
                dfg_ref[...] += dfg

    row = pl.BlockSpec((tm, D_MODEL), lambda i: (i, 0))
    hid = pl.BlockSpec((tm, D_FF), lambda i: (i, 0))
    vec = pl.BlockSpec((1, D_MODEL), lambda i: (0, 0))
    head_in = [] if head is None else [vec, row]
    head_out = [] if head is None else [pl.BlockSpec((1, 1), lambda i: (0, 0)), vec]
    head_shape = [] if head is None else [jax.ShapeDtypeStruct((1, 1), F32), jax.ShapeDtypeStruct((1, D_MODEL), F32)]
    return pl.pallas_call(
        _behind(body, n_in, after), name=name, grid=(t // tm,),
        in_specs=[row, vec] + [ANY] * (1 + n_down) + head_in + [ANY] * len(after),
        out_specs=[row] * (n_down + 1) + [hid, hid, hid] + head_out,
        out_shape=[jax.ShapeDtypeStruct((t, D_MODEL), F32)] * n_down + [jax.ShapeDtypeStruct((t, D_MODEL), BF16)]
        + [jax.ShapeDtypeStruct((t, D_FF), BF16)] * 3 + head_shape,
        scratch_shapes=[pltpu.VMEM((2, D_FF, D_MODEL), BF16)] + [pltpu.VMEM((D_FF, D_MODEL), BF16)] * n_down
        + [pltpu.SemaphoreType.DMA((2, LOAD_PIECES))],
        compiler_params=_params(("arbitrary",), VMEM_LIMIT),
    )(x, gain, w_gu, *([] if w_d is None else [w_d]), *(head or ()), *after)


def _ffn_down(x, s, w_d, *, name, after=(), tm=512):
    t = x.shape[0]
    tm = min(tm, t)

    def body(x_ref, s_ref, w_hbm, xo_ref, w_ref, sem):
        _load_resident([(w_hbm, w_ref)], sem)
        xo_ref[...] = x_ref[...] + FFN_RES_SCALE * jnp.dot(s_ref[...], w_ref[...], preferred_element_type=F32)

    row = pl.BlockSpec((tm, D_MODEL), lambda i: (i, 0))
    return pl.pallas_call(
        _behind(body, 3, after), name=name, grid=(t // tm,),
        in_specs=[row, pl.BlockSpec((tm, D_FF), lambda i: (i, 0)), ANY] + [ANY] * len(after), out_specs=row,
        out_shape=jax.ShapeDtypeStruct((t, D_MODEL), F32),
        scratch_shapes=[pltpu.VMEM((D_FF, D_MODEL), BF16), pltpu.SemaphoreType.DMA((1, LOAD_PIECES))],
        compiler_params=_params(("arbitrary",), VMEM_LIMIT),
    )(x, s, w_d, *after)


def _ffn_dgrad(dxo, x, gain, sa, sb, w_gu, w_d, *, name, after=(), tm=512, sub=512, tf=FF_CHUNK):
    t = x.shape[0]
    tm = min(tm, t)
    sub = min(sub, tm)

    def body(dxo_ref, x_ref, g_ref, sa_ref, sb_ref, wgu_hbm, wd_hbm, dxi_ref, da_ref, db_ref, gb_ref, dg_ref,
             wgu_ref, wd_ref, sems):
        _load_resident([(wd_hbm, wd_ref), (wgu_hbm, wgu_ref)], sems)

        @pl.when(pl.program_id(0) == 0)
        def _():
            dg_ref[...] = jnp.zeros_like(dg_ref)

        for r0 in range(0, tm, sub):
            rows = slice(r0, r0 + sub)
            go = dxo_ref[rows, :]
            gb = (FFN_RES_SCALE * go).astype(BF16)
            gb_ref[rows, :] = gb
            for c in range(0, D_FF, tf):
                cols = slice(c, min(c + tf, D_FF))
                ds = lax.dot_general(gb, wd_ref[cols, :], NT_DIMS, preferred_element_type=F32)
                da_ref[rows, cols] = (ds * sa_ref[rows, cols].astype(F32)).astype(BF16)
                db_ref[rows, cols] = (ds * sb_ref[rows, cols].astype(F32)).astype(BF16)
            dh = (jnp.dot(da_ref[rows, :], wgu_ref[0], preferred_element_type=F32)
                  + jnp.dot(db_ref[rows, :], wgu_ref[1], preferred_element_type=F32))
            xhat, inv = _rms_stats(x_ref[rows, :])
            dx, dgain = _rms_bwd(dh, xhat, inv, g_ref[...])
            dxi_ref[rows, :] = go + dx
            dg_ref[...] += dgain

    row = pl.BlockSpec((tm, D_MODEL), lambda i: (i, 0))
    hid = pl.BlockSpec((tm, D_FF), lambda i: (i, 0))
    vec = pl.BlockSpec((1, D_MODEL), lambda i: (0, 0))
    return pl.pallas_call(
        _behind(body, 7, after), name=name, grid=(t // tm,),
        in_specs=[row, row, vec, hid, hid, ANY, ANY] + [ANY] * len(after),
        out_specs=[row, hid, hid, row, vec],
        out_shape=[jax.ShapeDtypeStruct((t, D_MODEL), F32), jax.ShapeDtypeStruct((t, D_FF), BF16),
                   jax.ShapeDtypeStruct((t, D_FF), BF16),
                   jax.ShapeDtypeStruct((t, D_MODEL), BF16), jax.ShapeDtypeStruct((1, D_MODEL), F32)],
        scratch_shapes=[pltpu.VMEM((2, D_FF, D_MODEL), BF16), pltpu.VMEM((D_FF, D_MODEL), BF16),
                        pltpu.SemaphoreType.DMA((2, LOAD_PIECES))],
        compiler_params=_params(("arbitrary",), VMEM_LIMIT),
    )(dxo, x, gain, sa, sb, w_gu, w_d, *after)


def _tn_matmul(a, b, *, name, bm, after=(), tk=2048, blocks=None, into=None):
    t, m = a.shape
    n = b.shape[1]
    tk = min(tk, t)
    nk = t // tk
    first, count = blocks or (0, m // bm)
    behind = ([] if into is None else [into]) + list(after)

    def body(a_ref, b_ref, o_ref):
        @pl.when(pl.program_id(1) == 0)
        def _():
            o_ref[...] = jnp.zeros_like(o_ref)

        o_ref[...] += lax.dot_general(a_ref[...], b_ref[...], TN_DIMS, preferred_element_type=F32)

    return pl.pallas_call(
        _behind(body, 2, behind), name=name, grid=(count, nk),
        in_specs=[pl.BlockSpec((tk, bm), lambda i, k: (k, first + i)), pl.BlockSpec((tk, n), lambda i, k: (k, 0))]
        + [ANY] * len(behind),
        out_specs=pl.BlockSpec((bm, n), lambda i, k: (first + i, 0)),
        out_shape=jax.ShapeDtypeStruct((m, n), F32),
        input_output_aliases={} if into is None else {2: 0},
        compiler_params=_params(("parallel", "arbitrary"), VMEM_LIMIT),
    )(a, b, *behind)


def _rope_tables(t, after=()):
    inv_freq = ROPE_THETA ** (-jnp.arange(0, ROT_DIM, 2, dtype=F32) / ROT_DIM)
    ang = inv_freq[:, None] * jnp.arange(t, dtype=F32)[None, :]
    compact = jnp.stack([jnp.cos(ang), jnp.sin(ang)])
    tr = min(1024, t)

    def body(c_ref, o_ref):
        for k in range(2):
            o_ref[k] = jnp.tile(c_ref[k], (128 // inv_freq.shape[0], 1)).T

    return pl.pallas_call(
        _behind(body, 1, after), name="rope_tables", grid=(t // tr,),
        in_specs=[pl.BlockSpec((2, inv_freq.shape[0], tr), lambda i: (0, 0, i))] + [ANY] * len(after),
        out_specs=pl.BlockSpec((2, tr, 128), lambda i: (0, i, 0)),
        out_shape=jax.ShapeDtypeStruct((2, t, 128), F32), compiler_params=_params(("parallel",)),
    )(compact, *after)


def _rope_multipliers(cos_sin):
    half = ROT_DIM // 2
    cos, sin = cos_sin[0], cos_sin[1]
    d = lax.broadcasted_iota(jnp.int32, cos.shape, 1) & (HEAD_DIM - 1)
    mult = jnp.where(d < ROT_DIM, cos, 1.0)
    from_lo = jnp.where((d >= half) & (d < ROT_DIM), sin, 0.0)
    from_hi = jnp.where(d < half, -sin, 0.0)
    return mult, from_lo, from_hi


def _tile_lanes(tab, width):
    return jnp.tile(tab, (1, width // tab.shape[1]))


def _rope(v, tab):
    w = v.shape[1]
    half_rot = ROT_DIM // 2
    return (v * _tile_lanes(tab[0], w)
            + pltpu.roll(v, half_rot, axis=1) * _tile_lanes(tab[1], w)
            + pltpu.roll(v, w - half_rot, axis=1) * _tile_lanes(tab[2], w))


def _rope_bwd(dv, tab):
    w = dv.shape[1]
    half_rot = ROT_DIM // 2
    return (dv * _tile_lanes(tab[0], w)
            + pltpu.roll(dv * _tile_lanes(tab[1], w), w - half_rot, axis=1)
            + pltpu.roll(dv * _tile_lanes(tab[2], w), half_rot, axis=1))


def _shift_rows(v, prev8_ref, n):
    r = lax.broadcasted_iota(jnp.int32, v.shape, 0)
    rolled = pltpu.roll(v, n, axis=0)
    last = prev8_ref[7:8, :]
    if n == 1:
        return jnp.where(r >= 1, rolled, last)
    return jnp.where(r >= 2, rolled, jnp.where(r == 0, prev8_ref[6:7, :], last))


def _shift_rows_up(v, next8_ref, n):
    rows = v.shape[0]
    r = lax.broadcasted_iota(jnp.int32, v.shape, 0)
    rolled = pltpu.roll(v, rows - n, axis=0)
    first = next8_ref[0:1, :]
    if n == 1:
        return jnp.where(r <= rows - 2, rolled, first)
    return jnp.where(r <= rows - 3, rolled, jnp.where(r == rows - 2, first, next8_ref[1:2, :]))


def _lane_half_mask(shape, half):
    lane = lax.broadcasted_iota(jnp.int32, shape, 1)
    return (lane >= HEAD_DIM) if half else (lane < HEAD_DIM)


def _to_kv_lanes(chunk, head, kv):
    if head % 2 != kv:
        chunk = pltpu.roll(chunk, HEAD_DIM, axis=1)
    return jnp.where(_lane_half_mask(chunk.shape, kv), chunk, 0.0)


def _from_kv_lanes(chunk, head, kv):
    chunk = jnp.where(_lane_half_mask(chunk.shape, kv), chunk, 0.0)
    if head % 2 != kv:
        chunk = pltpu.roll(chunk, HEAD_DIM, axis=1)
    return chunk


def _stack_heads(wide):
    parts = []
    for head in range(N_Q_HEADS):
        chunk = wide[:, (head // 2) * 128:(head // 2 + 1) * 128]
        parts.append(_to_kv_lanes(chunk, head, head // Q_PER_KV))
    return jnp.concatenate(parts, axis=0)


def _window_mask(has_prev):
    shape = (N_Q_HEADS * BLOCK, 2 * BLOCK)
    qi = lax.broadcasted_iota(jnp.int32, shape, 0) & (BLOCK - 1)
    kj = lax.broadcasted_iota(jnp.int32, shape, 1)
    first_key = BLOCK - has_prev * BLOCK
    in_prev = (kj < BLOCK) & (kj > qi) & (kj >= first_key)
    in_own = (kj >= BLOCK) & ((kj - BLOCK) <= qi)
    return in_prev | in_own


def _sink_column(sink_ref):
    row = lax.broadcasted_iota(jnp.int32, (N_Q_HEADS * BLOCK, 1), 0)
    col = jnp.full((N_Q_HEADS * BLOCK, 1), sink_ref[0, 0], F32)
    for head in range(1, N_Q_HEADS):
        col = jnp.where(row >= head * BLOCK, sink_ref[0, head], col)
    return col


def _softmax_with_sink(q4, k2, mask, sink):
    s = lax.dot_general(q4, k2, NT_DIMS, preferred_element_type=F32) * SM_SCALE
    s = jnp.where(mask, s, MASK_VALUE)
    m = jnp.maximum(jnp.max(s, axis=-1, keepdims=True), sink)
    p = jnp.exp(s - m)
    e_sink = jnp.exp(sink - m)
    inv_den = 1.0 / (jnp.sum(p, axis=-1, keepdims=True) + e_sink)
    return p * inv_den, e_sink * inv_den


def _conv_terms(zf, prev8_ref, w_ref):
    b_gate, c_gate, u = zf[:, 0:CONV_W], zf[:, CONV_W:2 * CONV_W], zf[:, 2 * CONV_W:3 * CONV_W]
    vc = c_gate * u
    vm1 = _shift_rows(vc, prev8_ref, 1)
    vm2 = _shift_rows(vc, prev8_ref, 2)
    conv = w_ref[0:1, :] * vm2 + w_ref[1:2, :] * vm1 + w_ref[2:3, :] * vc
    return b_gate, c_gate, u, vc, vm1, vm2, conv


def _mixer_fwd(x, gain, win_t, wout, conv_w, sinks, rope, *, name, after=(), tq=512):
    t = x.shape[0]
    tq = min(tq, t)
    nblk = tq // BLOCK

    def body(x_ref, g_ref, win_hbm, wout_hbm, cw_ref, sink_ref, rope_ref,
             xo_ref, h_ref, z_ref, y_ref, kprev_ref, vprev_ref, cprev_ref, win_ref, wout_ref, sems):
        i = pl.program_id(0)
        _load_resident([(win_hbm, win_ref), (wout_hbm, wout_ref)], sems)

        @pl.when(i == 0)
        def _():
            kprev_ref[...] = jnp.zeros_like(kprev_ref)
            vprev_ref[...] = jnp.zeros_like(vprev_ref)
            cprev_ref[...] = jnp.zeros_like(cprev_ref)

        xf = x_ref[...]
        xhat, _ = _rms_stats(xf)
        h = (xhat * g_ref[...]).astype(BF16)
        h_ref[...] = h

        def project(c0, c1):
            zc = lax.dot_general(h, win_ref[c0:c1, :], NT_DIMS, preferred_element_type=F32).astype(BF16)
            z_ref[:, c0:c1] = zc
            return zc

        zb = project(Q_OFF, Z_W)
        zf = zb.astype(F32)
        tab = _rope_multipliers(rope_ref[...])
        qr = _rope(zf[:, 0:ATTN_W], tab)
        kr = _rope(zf[:, K_OFF - Q_OFF:V_OFF - Q_OFF], tab).astype(BF16)
        vb = zb[:, V_OFF - Q_OFF:Z_W - Q_OFF]
        conv_cols = [(c, c + CONV_W) for c in range(0, Q_OFF, CONV_W)]
        conv_z = []

        y_attn = []
        for j in range(nblk):
            if len(conv_z) < len(conv_cols):
                conv_z.append(project(*conv_cols[len(conv_z)]))
            rows = slice(j * BLOCK, (j + 1) * BLOCK)
            prev = slice((j - 1) * BLOCK, j * BLOCK)
            k2 = jnp.concatenate([kprev_ref[...] if j == 0 else kr[prev], kr[rows]], axis=0)
            v2 = jnp.concatenate([vprev_ref[...] if j == 0 else vb[prev], vb[rows]], axis=0)
            mask = _window_mask(jnp.minimum(i, 1) if j == 0 else 1)
            q8 = _stack_heads(qr[rows]).astype(BF16)
            probs, _ = _softmax_with_sink(q8, k2, mask, _sink_column(sink_ref))
            o8 = jnp.dot(probs.astype(BF16), v2, preferred_element_type=F32)
            chunks = [jnp.zeros((BLOCK, 128), F32) for _ in range(ATTN_W // 128)]
            for head in range(N_Q_HEADS):
                chunks[head // 2] += _from_kv_lanes(o8[head * BLOCK:(head + 1) * BLOCK], head, head // Q_PER_KV)
            y_attn.append(jnp.concatenate(chunks, axis=1))
        kprev_ref[...] = kr[tq - BLOCK:tq]
        vprev_ref[...] = vb[tq - BLOCK:tq]
        while len(conv_z) < len(conv_cols):
            conv_z.append(project(*conv_cols[len(conv_z)]))
        ya = jnp.concatenate(y_attn, axis=0).astype(BF16)
        y_ref[:, CONV_W:] = ya
        xo = xf + jnp.dot(ya, wout_ref[CONV_W:, :], preferred_element_type=F32)
        b_gate, _, _, vc, _, _, conv = _conv_terms(jnp.concatenate(conv_z, axis=1).astype(F32), cprev_ref, cw_ref)
        yc = (b_gate * conv).astype(BF16)
        cprev_ref[...] = vc[tq - 8:tq, :]
        y_ref[:, :CONV_W] = yc
        xo_ref[...] = xo + jnp.dot(yc, wout_ref[:CONV_W, :], preferred_element_type=F32)

    row = pl.BlockSpec((tq, D_MODEL), lambda i: (i, 0))
    full = lambda shape: pl.BlockSpec(shape, lambda i: (0,) * len(shape))
    return pl.pallas_call(
        _behind(body, 7, after), name=name, grid=(t // tq,),
        in_specs=[row, full((1, D_MODEL)), ANY, ANY, full((3, CONV_W)),
                  pl.BlockSpec(memory_space=pltpu.SMEM), pl.BlockSpec((2, tq, 128), lambda i: (0, i, 0))]
        + [ANY] * len(after),
        out_specs=[row, row, pl.BlockSpec((tq, Z_W), lambda i: (i, 0)), row],
        out_shape=[jax.ShapeDtypeStruct((t, D_MODEL), F32), jax.ShapeDtypeStruct((t, D_MODEL), BF16),
                   jax.ShapeDtypeStruct((t, Z_W), BF16), jax.ShapeDtypeStruct((t, D_MODEL), BF16)],
        scratch_shapes=[pltpu.VMEM((BLOCK, KV_W), BF16), pltpu.VMEM((BLOCK, KV_W), BF16),
                        pltpu.VMEM((8, CONV_W), F32), pltpu.VMEM((Z_W, D_MODEL), BF16),
                        pltpu.VMEM((D_MODEL, D_MODEL), BF16), pltpu.SemaphoreType.DMA((2, LOAD_PIECES))],
        compiler_params=_params(("arbitrary",), VMEM_LIMIT),
    )(x, gain, win_t, wout, conv_w, sinks, rope, *after)


def _mixer_bwd(dxo, x, gain, y, z, win_t, wout, conv_w, sinks, rope, *, name, after=(), tq=256):
    t = x.shape[0]
    tq = min(tq, t)
    nt, nblk = t // tq, tq // BLOCK

    def body(dxo_ref, x_ref, g_ref, y_ref, z_ref, zp_ref, win_hbm, wout_hbm, cw_ref, sink_ref, rope_ref, ropep_ref,
             dxi_ref, dz_ref, gb_ref, dcw_ref, dsink_ref, dg_ref, dk_ref, dv_ref, dcn_ref, pvc_ref,
             win_ref, wout_ref, sems):
        i = pl.program_id(0)
        tile = nt - 1 - i
        _load_resident([(win_hbm, win_ref), (wout_hbm, wout_ref)], sems)

        @pl.when(i == 0)
        def _():
            dk_ref[...] = jnp.zeros_like(dk_ref)
            dv_ref[...] = jnp.zeros_like(dv_ref)
            dcn_ref[...] = jnp.zeros_like(dcn_ref)
            dcw_ref[...] = jnp.zeros_like(dcw_ref)
            dsink_ref[...] = jnp.zeros_like(dsink_ref)
            dg_ref[...] = jnp.zeros_like(dg_ref)

        has_prev = jnp.minimum(tile, 1)
        go = dxo_ref[...]
        gb = go.astype(BF16)
        gb_ref[...] = gb
        dy = lax.dot_general(gb, wout_ref[...], NT_DIMS, preferred_element_type=F32)
        dy_conv, dy_attn = dy[:, 0:CONV_W], dy[:, CONV_W:D_MODEL]
        zb, zpb = z_ref[...], zp_ref[...]
        zf = zb.astype(F32)
        zpf = zpb.astype(F32) * has_prev.astype(F32)

        pvc_ref[...] = (zpf[:, CONV_W:2 * CONV_W] * zpf[:, 2 * CONV_W:3 * CONV_W])[BLOCK - 8:BLOCK, :]
        b_gate, c_gate, u, vc, vm1, vm2, conv = _conv_terms(zf, pvc_ref, cw_ref)
        d_bgate = dy_conv * conv
        dc = dy_conv * b_gate
        tap = lax.broadcasted_iota(jnp.int32, (8, CONV_W), 0)
        dcw_ref[...] += jnp.where(tap == 0, jnp.sum(dc * vm2, axis=0, keepdims=True),
                                  jnp.where(tap == 1, jnp.sum(dc * vm1, axis=0, keepdims=True),
                                            jnp.where(tap == 2, jnp.sum(dc * vc, axis=0, keepdims=True), 0.0)))
        dvc = (cw_ref[2:3, :] * dc + cw_ref[1:2, :] * _shift_rows_up(dc, dcn_ref, 1)
               + cw_ref[0:1, :] * _shift_rows_up(dc, dcn_ref, 2))
        dcn_ref[...] = dc[0:8, :]
        d_cgate = dvc * u
        d_u = dvc * c_gate

        tab, tabp = _rope_multipliers(rope_ref[...]), _rope_multipliers(ropep_ref[...])
        qr = _rope(zf[:, Q_OFF:K_OFF], tab)
        kr = _rope(zf[:, K_OFF:V_OFF], tab).astype(BF16)
        kpr = _rope(zpf[:, K_OFF:V_OFF], tabp).astype(BF16)
        vb, vpb = zb[:, V_OFF:Z_W], zpb[:, V_OFF:Z_W]
        out = y_ref[:, CONV_W:D_MODEL].astype(F32)
        do_out = dy_attn * out
        lane = lax.broadcasted_iota(jnp.int32, (1, 128), 1)
        dsink = jnp.zeros((1, 128), F32)
        dk_next, dv_next = dk_ref[...], dv_ref[...]
        dq_rows, dk_rows, dv_rows = [None] * nblk, [None] * nblk, [None] * nblk
        for j in reversed(range(nblk)):
            rows = slice(j * BLOCK, (j + 1) * BLOCK)
            prev = slice((j - 1) * BLOCK, j * BLOCK)
            k2 = jnp.concatenate([kpr if j == 0 else kr[prev], kr[rows]], axis=0)
            v2 = jnp.concatenate([vpb if j == 0 else vb[prev], vb[rows]], axis=0)
            mask = _window_mask(has_prev if j == 0 else 1)
            q8 = _stack_heads(qr[rows]).astype(BF16)
            do8 = _stack_heads(dy_attn[rows]).astype(BF16)
            delta = jnp.sum(_stack_heads(do_out[rows]), axis=-1, keepdims=True)
            probs, p_sink = _softmax_with_sink(q8, k2, mask, _sink_column(sink_ref))
            dp = lax.dot_general(do8, v2, NT_DIMS, preferred_element_type=F32)
            ds = (probs * (dp - delta) * SM_SCALE).astype(BF16)
            dq8 = jnp.dot(ds, k2, preferred_element_type=F32)
            dk2 = lax.dot_general(ds, q8, TN_DIMS, preferred_element_type=F32)
            dv2 = lax.dot_general(probs.astype(BF16), do8, TN_DIMS, preferred_element_type=F32)
            sink_terms = p_sink * delta
            dq_chunks = [jnp.zeros((BLOCK, 128), F32) for _ in range(ATTN_W // 128)]
            for head in range(N_Q_HEADS):
                grp = slice(head * BLOCK, (head + 1) * BLOCK)
                dq_chunks[head // 2] += _from_kv_lanes(dq8[grp], head, head // Q_PER_KV)
                dsink = dsink - jnp.where(lane == head, jnp.sum(sink_terms[grp], axis=0, keepdims=True), 0.0)
            dq_rows[j] = jnp.concatenate(dq_chunks, axis=1)
            dk_rows[j] = dk2[BLOCK:] + dk_next
            dv_rows[j] = dv2[BLOCK:] + dv_next
            dk_next, dv_next = dk2[:BLOCK], dv2[:BLOCK]
        dk_ref[...] = dk_next
        dv_ref[...] = dv_next
        dsink_ref[...] += dsink
        dq = _rope_bwd(jnp.concatenate(dq_rows, axis=0), tab)
        dk = _rope_bwd(jnp.concatenate(dk_rows, axis=0), tab)
        dv = jnp.concatenate(dv_rows, axis=0)

        dzb = jnp.concatenate([d_bgate, d_cgate, d_u, dq, dk, dv], axis=1).astype(BF16)
        dz_ref[...] = dzb
        dh = jnp.dot(dzb, win_ref[...], preferred_element_type=F32)
        xhat, inv = _rms_stats(x_ref[...])
        dx, dgain = _rms_bwd(dh, xhat, inv, g_ref[...])
        dxi_ref[...] = go + dx
        dg_ref[...] += dgain

    rev = lambda i: (nt - 1 - i, 0)
    block_before = lambda i: jnp.maximum((nt - 1 - i) * nblk - 1, 0)
    row = pl.BlockSpec((tq, D_MODEL), rev)
    full = lambda shape: pl.BlockSpec(shape, lambda i: (0,) * len(shape))
    return pl.pallas_call(
        _behind(body, 12, after), name=name, grid=(nt,),
        in_specs=[row, row, full((1, D_MODEL)), row,
                  pl.BlockSpec((tq, Z_W), rev), pl.BlockSpec((BLOCK, Z_W), lambda i: (block_before(i), 0)),
                  ANY, ANY, full((3, CONV_W)),
                  pl.BlockSpec(memory_space=pltpu.SMEM),
                  pl.BlockSpec((2, tq, 128), lambda i: (0, nt - 1 - i, 0)),
                  pl.BlockSpec((2, BLOCK, 128), lambda i: (0, block_before(i), 0))] + [ANY] * len(after),
        out_specs=[row, pl.BlockSpec((tq, Z_W), rev), row, full((8, CONV_W)), full((1, 128)), full((1, D_MODEL))],
        out_shape=[jax.ShapeDtypeStruct((t, D_MODEL), F32), jax.ShapeDtypeStruct((t, Z_W), BF16),
                   jax.ShapeDtypeStruct((t, D_MODEL), BF16), jax.ShapeDtypeStruct((8, CONV_W), F32),
                   jax.ShapeDtypeStruct((1, 128), F32), jax.ShapeDtypeStruct((1, D_MODEL), F32)],
        scratch_shapes=[pltpu.VMEM((BLOCK, KV_W), F32), pltpu.VMEM((BLOCK, KV_W), F32), pltpu.VMEM((8, CONV_W), F32),
                        pltpu.VMEM((8, CONV_W), F32), pltpu.VMEM((Z_W, D_MODEL), BF16),
                        pltpu.VMEM((D_MODEL, D_MODEL), BF16), pltpu.SemaphoreType.DMA((2, LOAD_PIECES))],
        compiler_params=_params(("arbitrary",), VMEM_LIMIT),
    )(dxo, x, gain, y, z, z, win_t, wout, conv_w, sinks, rope, rope, *after)


def _place():
    x, y, c = lax.axis_index("x"), lax.axis_index("y"), lax.axis_index("c")
    other_chips = [(1 - x, y), (x, 1 - y), (1 - x, 1 - y)]
    return x, y, c, other_chips


def _all_gather_rows(shards, place=(), *, name):
    n, p = len(shards), len(place)

    def body(*refs):
        srcs, place_srcs = refs[:n], refs[n:n + p]
        outs, place_outs = refs[n + p:2 * n + p], refs[2 * n + p:2 * (n + p)]
        send_sems, recv_sems, local_sems = refs[2 * (n + p):]
        x, y, c, _ = _place()
        me, sibling = (x, y, c), (x, y, 1 - c)
        relay_from = (x + (1 - c) - 2 * x * (1 - c), y + c - 2 * y * c)
        relay_to = (x + c - 2 * x * c, y + (1 - c) - 2 * y * (1 - c))
        chips = [relay_from, relay_to, (1 - x, 1 - y)]

        def rows(t, px, py, pc):
            r = srcs[t].shape[-2]
            start = pl.multiple_of((4 * px + 2 * py + pc) * r, 16 if r % 16 == 0 else 8)
            if len(srcs[t].shape) == 3:
                return outs[t].at[:, pl.ds(start, r), :]
            return outs[t].at[pl.ds(start, r), :]

        def copy(t, k, block, to, own=False):
            return pltpu.make_async_remote_copy(
                src_ref=srcs[t] if own else rows(t, *block), dst_ref=rows(t, *block),
                send_sem=send_sems.at[t, k], recv_sem=recv_sems.at[t, k], device_id=to, device_id_type=MESH)

        mine = [pltpu.make_async_copy(srcs[t], rows(t, *me), local_sems.at[t]) for t in range(n)]
        mine += [pltpu.make_async_copy(place_srcs[q],
                                       _block_rows(place_outs[q], place_srcs[q].shape[-2], 4 * x + 2 * y + c),
                                       local_sems.at[n + q]) for q in range(p)]
        for q in range(p):
            mine[n + q].start()
        first = []
        for t in range(n):
            mine[t].start()
            first.append(copy(t, 0, me, sibling, own=True))
            first += [copy(t, 1 + j, me, (*chip, c), own=True) for j, chip in enumerate(chips[:2])]
        for cp in first:
            cp.start()
        passed = []
        for j, chip in enumerate(chips):
            for t in range(n):
                copy(t, 1 + j, (*chip, c), me).wait_recv()
                if j == 0:
                    passed.append(copy(t, 3, (*chip, c), (*relay_to, c)))
                    passed[-1].start()
                passed.append(copy(t, 4 + j, (*chip, c), sibling))
                passed[-1].start()
        for t in range(n):
            copy(t, 0, sibling, me).wait_recv()
            for j, chip in enumerate([relay_to, relay_from, chips[2]]):
                copy(t, 4 + j, (*chip, 1 - c), me).wait_recv()
        for cp in first + passed:
            cp.wait_send()
        for cp in mine:
            cp.wait()

    out_shape = [jax.ShapeDtypeStruct(s.shape[:-2] + (N_DEV * s.shape[-2], s.shape[-1]), s.dtype)
                 for s in list(shards) + list(place)]
    res = pl.pallas_call(
        body, name=name, in_specs=[ANY] * (n + p), out_specs=[ANY] * (n + p), out_shape=out_shape,
        scratch_shapes=[pltpu.SemaphoreType.DMA((n, 7)), pltpu.SemaphoreType.DMA((n, 7)),
                        pltpu.SemaphoreType.DMA((n + p,))],
    )(*shards, *place)
    return res[:n], res[n:]


def _split_start(bufs, n_copies, plan, *, name, after=()):
    n = len(bufs)

    def body(*refs):
        token = refs[-1]
        for cp in plan(refs[:n], refs[n], refs[n + 1]):
            cp.start()
        token[...] = jnp.zeros_like(token)

    res = pl.pallas_call(
        _behind(body, n, after), name=name, in_specs=[HBM_SPEC] * n + [ANY] * len(after),
        out_specs=(SEM_SPEC, SEM_SPEC, *[HBM_SPEC] * n, pl.BlockSpec(memory_space=pltpu.VMEM)),
        out_shape=(pltpu.SemaphoreType.DMA((n_copies,)), pltpu.SemaphoreType.DMA((n_copies,)),
                   *[pltpu.HBM(b.shape, b.dtype) for b in bufs], jax.ShapeDtypeStruct((8, 128), F32)),
        input_output_aliases={i: 2 + i for i in range(n)},
        compiler_params=pltpu.CompilerParams(has_side_effects=DATAFLOW),
    )(*[pltpu.with_memory_space_constraint(b, pltpu.HBM) for b in bufs], *after)
    return res[0], res[1], list(res[2:2 + n]), res[-1]


def _split_wait(send_sems, recv_sems, bufs, after, plan, *, name):
    n = len(bufs)

    def body(*refs):
        for cp in plan(refs[:n], refs[n], refs[n + 1]):
            cp.wait_send()
            cp.wait_recv()

    return list(pl.pallas_call(
        body, name=name, in_specs=[HBM_SPEC] * n + [SEM_SPEC, SEM_SPEC, ANY], out_specs=[HBM_SPEC] * n,
        out_shape=tuple(pltpu.HBM(b.shape, b.dtype) for b in bufs),
        input_output_aliases={i: i for i in range(n)},
        compiler_params=pltpu.CompilerParams(has_side_effects=DATAFLOW),
    )(*bufs, send_sems, recv_sems, after))


def _sibling_plan(n):
    def plan(bufs, send_sems, recv_sems):
        x, y, c, _ = _place()
        return [pltpu.make_async_remote_copy(
            src_ref=bufs[t].at[:, 1 - c], dst_ref=bufs[n + t], send_sem=send_sems.at[t], recv_sem=recv_sems.at[t],
            device_id=(x, y, 1 - c), device_id_type=MESH) for t in range(n)]
    return plan


def _block_rows(ref, r, blk):
    start = pl.multiple_of(blk * r, 16 if r % 16 == 0 else 8)
    return ref.at[(slice(None),) * (len(ref.shape) - 2) + (pl.ds(start, r), slice(None))]


def _remote(src, dst, send_sems, recv_sems, k, peer):
    return pltpu.make_async_remote_copy(src_ref=src, dst_ref=dst, send_sem=send_sems.at[k], recv_sem=recv_sems.at[k],
                                        device_id=peer, device_id_type=MESH)


def _gather_send_plan(n, first=0):
    def plan(bufs, send_sems, recv_sems):
        x, y, c, chips = _place()
        peers = [(x, y, 1 - c)] + [(px, py, c) for px, py in chips]
        copies = []
        for t in range(n):
            dst = _block_rows(bufs[n + t], bufs[t].shape[-2], 4 * x + 2 * y + c)
            copies += [_remote(bufs[t], dst, send_sems, recv_sems, 4 * (first + t) + k, peer)
                       for k, peer in enumerate(peers)]
        return copies
    return plan


def _gather_forward_plan(rows):
    def plan(bufs, send_sems, recv_sems):
        x, y, c, chips = _place()
        copies = []
        for t, r in enumerate(rows):
            for j, (px, py) in enumerate(chips):
                blk = _block_rows(bufs[t], r, 4 * px + 2 * py + c)
                copies.append(_remote(blk, blk, send_sems, recv_sems, 3 * t + j, (x, y, 1 - c)))
        return copies
    return plan


def _chips_plan(n, with_small):
    def plan(bufs, send_sems, recv_sems):
        x, y, c, chips = _place()
        copies = []
        for t in range(n):
            for j, (px, py) in enumerate(chips):
                copies.append(_remote(bufs[t].at[2 * px + py], bufs[n + t].at[j], send_sems, recv_sems, 3 * t + j,
                                      (px, py, c)))
        if with_small:
            mine = _block_rows(bufs[2 * n], 8, 4 * x + 2 * y + c)
            flips = [(fx, fy, fc) for fx in range(2) for fy in range(2) for fc in range(2)][1:]
            for k, (fx, fy, fc) in enumerate(flips):
                peer = (x + fx - 2 * x * fx, y + fy - 2 * y * fy, c + fc - 2 * c * fc)
                copies.append(_remote(mine, mine, send_sems, recv_sems, 3 * n + k, peer))
        return copies
    return plan


def _place_own(fulls, shards, index, *, name):
    n = len(fulls)

    def body(index_ref, *refs):
        for t in range(n):
            refs[2 * n + t][...] = refs[n + t][...]

    def block_of(shard):
        lead = len(shard.shape) - 2
        return pl.BlockSpec(shard.shape, lambda i, index_ref: (0,) * lead + (index_ref[0], 0))

    def whole(shard):
        return pl.BlockSpec(shard.shape, lambda i, index_ref: (0,) * len(shard.shape))

    return list(pl.pallas_call(
        body, name=name,
        grid_spec=pltpu.PrefetchScalarGridSpec(
            num_scalar_prefetch=1, grid=(1,),
            in_specs=[ANY] * n + [whole(s) for s in shards], out_specs=[block_of(s) for s in shards]),
        out_shape=[jax.ShapeDtypeStruct(f.shape, f.dtype) for f in fulls],
        input_output_aliases={1 + t: t for t in range(n)},
        compiler_params=_params(("arbitrary",)),
    )(index, *fulls, *shards))


N_STEPS_SMALL = 2


def _add_sibling(grads, recvs, place, *, name):
    n = len(grads)

    def body(place_ref, *refs):
        chip = place_ref[1]
        for t in range(n):
            g_ref, r_ref, own_ref, ob_ref = refs[2 * t], refs[2 * t + 1], refs[2 * n + 2 * t], refs[2 * n + 2 * t + 1]
            own = jnp.zeros(own_ref.shape, F32)
            for m in range(4):
                p = g_ref[m, 0] + r_ref[m]
                ob_ref[m] = p.astype(BF16)
                own = jnp.where(chip == m, p, own)
            own_ref[...] = own

    in_specs, out_specs, out_shape = [], [], []
    for g, r in zip(grads, recvs):
        tr = g.shape[2] // N_STEPS_SMALL
        blocks = pl.BlockSpec((4, tr, D_MODEL), lambda i, place_ref: (0, i, 0))
        in_specs += [pl.BlockSpec((4, 1, tr, D_MODEL), lambda i, place_ref: (0, place_ref[0], i, 0)), blocks]
        out_specs += [pl.BlockSpec((tr, D_MODEL), lambda i, place_ref: (i, 0)), blocks]
        out_shape += [jax.ShapeDtypeStruct(r.shape[1:], F32), jax.ShapeDtypeStruct(r.shape, BF16)]
    res = pl.pallas_call(
        body, name=name,
        grid_spec=pltpu.PrefetchScalarGridSpec(num_scalar_prefetch=1, grid=(N_STEPS_SMALL,), in_specs=in_specs,
                                               out_specs=out_specs),
        out_shape=out_shape, compiler_params=_params(("arbitrary",), VMEM_LIMIT),
    )(place, *[a for pair in zip(grads, recvs) for a in pair])
    return [(res[2 * t], res[2 * t + 1]) for t in range(n)]


def _reduce_adamw(parts, *, name, after=()):
    n = len(parts)

    def body(*refs):
        for t in range(n):
            p_ref, r_ref, w_ref, m_ref, v_ref = refs[5 * t:5 * t + 5]
            g_ref, d_ref, mo_ref, vo_ref = refs[5 * n + 4 * t:5 * n + 4 * t + 4]
            g = p_ref[...] + r_ref[0].astype(F32) + r_ref[1].astype(F32) + r_ref[2].astype(F32)
            g_ref[...] = g
            d_ref[...], mo_ref[...], vo_ref[...] = _adamw_math(w_ref[...], g, m_ref[...], v_ref[...])
        refs[-1][...] = jnp.zeros_like(refs[-1])

    in_specs, out_specs, out_shape = [], [], []
    for own, _, _, _, _ in parts:
        rows = own.shape[0]
        tr = rows // N_STEPS_SMALL
        spec = pl.BlockSpec((tr, D_MODEL), lambda i: (i, 0))
        in_specs += [spec, pl.BlockSpec((3, tr, D_MODEL), lambda i: (0, i, 0)), spec, spec, spec]
        out_specs += [spec] * 4
        out_shape += [jax.ShapeDtypeStruct((rows, D_MODEL), F32)] * 4
    res = pl.pallas_call(
        _behind(body, 5 * n, after), name=name, grid=(N_STEPS_SMALL,),
        in_specs=in_specs + [ANY] * len(after),
        out_specs=out_specs + [pl.BlockSpec((8, 128), lambda i: (0, 0))],
        out_shape=out_shape + [jax.ShapeDtypeStruct((8, 128), F32)],
        compiler_params=_params(("arbitrary",), VMEM_LIMIT),
    )(*[a for part in parts for a in part], *after)
    return [tuple(res[4 * t:4 * t + 4]) for t in range(n)], res[-1]


def _adamw_math(w, g, m, v):
    m = ADAM_B1 * m + (1.0 - ADAM_B1) * g
    v = ADAM_B2 * v + (1.0 - ADAM_B2) * (g * g)
    m_hat = m / (1.0 - ADAM_B1 ** ADAM_STEP)
    v_hat = v / (1.0 - ADAM_B2 ** ADAM_STEP)
    delta = -ADAM_LR * (m_hat / (jnp.sqrt(v_hat) + ADAM_EPS) + ADAM_WD * w)
    return delta, m, v


SMALL_NAMES = ["ffn1_norm", "mix_norm", "ffn2_norm", "final_norm", "conv_w", "attn_sinks"]


def _update_small(given, moments_m, moments_v, small_all, my_index, *, name):
    conv_cols = given["conv_w"].shape[2]
    per_block = 128 // conv_cols

    def two_d(nm, a):
        return a.reshape(1, D_MODEL) if nm == "final_norm" else a

    operands = [two_d(nm, src[nm]) for nm in SMALL_NAMES for src in (given, moments_m, moments_v)]
    n = len(SMALL_NAMES)

    def body(index_ref, all_ref, conv_ref, *refs):
        ins, outs = refs[:3 * n], refs[3 * n:]
        total, conv_total = all_ref[0], conv_ref[0]
        for k in range(1, N_DEV):
            total, conv_total = total + all_ref[k], conv_total + conv_ref[k]
        which = index_ref[0] % per_block
        conv_g = conv_total[4:7, :conv_cols]
        for j in range(1, per_block):
            conv_g = jnp.where(which == j, conv_total[4:7, j * conv_cols:(j + 1) * conv_cols], conv_g)
        grads = [total[0:1], total[1:2], total[2:3], total[3:4], conv_g[None], total[7:8, :N_Q_HEADS]]
        for t, g in enumerate(grads):
            w_ref, m_ref, v_ref = ins[3 * t:3 * t + 3]
            g_ref, d_ref, mo_ref, vo_ref = outs[4 * t:4 * t + 4]
            g_ref[...] = g
            d_ref[...], mo_ref[...], vo_ref[...] = _adamw_math(w_ref[...], g, m_ref[...], v_ref[...])
        outs[-1][...] = total[7:8, LOSS_LANE:LOSS_LANE + 1]

    def whole(shape):
        return pl.BlockSpec(shape, lambda i, index_ref: (0,) * len(shape))

    shapes = [a.shape for a in operands[::3] for _ in range(4)] + [(1, 1)]
    res = pl.pallas_call(
        body, name=name,
        grid_spec=pltpu.PrefetchScalarGridSpec(
            num_scalar_prefetch=1, grid=(1,),
            in_specs=[whole(small_all.shape),
                      pl.BlockSpec((N_DEV, 8, 128), lambda i, index_ref: (0, 0, index_ref[0] // per_block))]
            + [whole(a.shape) for a in operands],
            out_specs=[whole(s) for s in shapes]),
        out_shape=[jax.ShapeDtypeStruct(s, F32) for s in shapes],
        compiler_params=_params(("arbitrary",)),
    )(my_index.astype(jnp.int32).reshape(1), small_all, small_all, *operands)
    results = {nm: tuple(a.reshape(given[nm].shape) for a in res[4 * t:4 * t + 4]) for t, nm in enumerate(SMALL_NAMES)}
    return results, res[-1]


def kernel(x, ffn1_norm, ffn1_w_gate, ffn1_w_up, ffn1_w_down, mix_norm, w_in, conv_w, attn_sinks, w_out, ffn2_norm, ffn2_w_gate, ffn2_w_up, ffn2_w_down, final_norm, loss_target, m_ffn1_norm, m_ffn1_w_gate, m_ffn1_w_up, m_ffn1_w_down, m_mix_norm, m_w_in, m_conv_w, m_attn_sinks, m_w_out, m_ffn2_norm, m_ffn2_w_gate, m_ffn2_w_up, m_ffn2_w_down, m_final_norm, v_ffn1_norm, v_ffn1_w_gate, v_ffn1_w_up, v_ffn1_w_down, v_mix_norm, v_w_in, v_conv_w, v_attn_sinks, v_w_out, v_ffn2_norm, v_ffn2_w_gate, v_ffn2_w_up, v_ffn2_w_down, v_final_norm):
    ix, iy, ic = lax.axis_index("x"), lax.axis_index("y"), lax.axis_index("c")
    my_index = 4 * ix + 2 * iy + ic
    place = jnp.stack([ic, 2 * ix + iy]).astype(jnp.int32)

    given = dict(ffn1_norm=ffn1_norm, ffn1_w_gate=ffn1_w_gate, ffn1_w_up=ffn1_w_up, ffn1_w_down=ffn1_w_down,
                 mix_norm=mix_norm, w_in=w_in, conv_w=conv_w, attn_sinks=attn_sinks, w_out=w_out, ffn2_norm=ffn2_norm,
                 ffn2_w_gate=ffn2_w_gate, ffn2_w_up=ffn2_w_up, ffn2_w_down=ffn2_w_down, final_norm=final_norm)
    moments_m = dict(ffn1_norm=m_ffn1_norm, ffn1_w_gate=m_ffn1_w_gate, ffn1_w_up=m_ffn1_w_up, ffn1_w_down=m_ffn1_w_down,
                     mix_norm=m_mix_norm, w_in=m_w_in, conv_w=m_conv_w, attn_sinks=m_attn_sinks, w_out=m_w_out,
                     ffn2_norm=m_ffn2_norm, ffn2_w_gate=m_ffn2_w_gate, ffn2_w_up=m_ffn2_w_up, ffn2_w_down=m_ffn2_w_down,
                     final_norm=m_final_norm)
    moments_v = dict(ffn1_norm=v_ffn1_norm, ffn1_w_gate=v_ffn1_w_gate, ffn1_w_up=v_ffn1_w_up, ffn1_w_down=v_ffn1_w_down,
                     mix_norm=v_mix_norm, w_in=v_w_in, conv_w=v_conv_w, attn_sinks=v_attn_sinks, w_out=v_w_out,
                     ffn2_norm=v_ffn2_norm, ffn2_w_gate=v_ffn2_w_gate, ffn2_w_up=v_ffn2_w_up, ffn2_w_down=v_ffn2_w_down,
                     final_norm=v_final_norm)

    xs = x[0]
    target = loss_target[0]
    final_gain = final_norm.reshape(1, D_MODEL)

    def ffn_shards(wg, wu, wd):
        return jnp.stack([wg[0].T, wu[0].T]).astype(BF16), wd[0].astype(BF16)

    conv_cols = conv_w.shape[2]
    conv_shard = jnp.pad(conv_w[0], ((0, 5), (0, 128 - conv_cols)))
    gate_up1, down1 = ffn_shards(ffn1_w_gate, ffn1_w_up, ffn1_w_down)
    rest_shards = [down1, w_in[0].T.astype(BF16), w_out[0].astype(BF16), conv_shard,
                   *ffn_shards(ffn2_w_gate, ffn2_w_up, ffn2_w_down)]
    rest_rows = [s.shape[-2] for s in rest_shards]
    n_rest, n_early = len(rest_shards), 4
    (w1_gu,), _ = _all_gather_rows([gate_up1], name="gather_ffn1")

    fulls = [lax.empty(s.shape[:-2] + (N_DEV * s.shape[-2], s.shape[-1]), s.dtype) for s in rest_shards]
    fulls = _place_own(fulls, rest_shards, my_index.astype(jnp.int32).reshape(1), name="place_own_weights")
    ssem, rsem, bufs, token = _split_start(rest_shards + list(fulls), 4 * n_rest, _gather_send_plan(n_rest),
                                           name="gather_rest_start", after=[w1_gu])
    early = bufs[:n_early] + bufs[n_rest:n_rest + n_early]
    late = bufs[n_early:n_rest] + bufs[n_rest + n_early:]
    h1, s1, sa1, sb1 = _ffn_fwd(xs, ffn1_norm, w1_gu, name="ffn1_hidden", after=[token])
    early = _split_wait(ssem, rsem, early, h1, _gather_send_plan(n_early), name="gather_early_wait")
    fwd_early = _gather_forward_plan(rest_rows[:n_early])
    ssem_e, rsem_e, parts, token = _split_start(early[n_early:], 3 * n_early, fwd_early, name="forward_early_start")
    rope = _rope_tables(xs.shape[0], after=[token])
    w1_d, win_t, wout, conv_all = _split_wait(ssem_e, rsem_e, parts, rope, fwd_early, name="forward_early_wait")
    x1 = _ffn_down(xs, s1, w1_d, name="ffn1_down")
    conv_full = conv_all.reshape(N_DEV, 8, 128)[:, :3, :conv_cols].transpose(1, 0, 2).reshape(3, CONV_W)
    late = _split_wait(ssem, rsem, late, x1, _gather_send_plan(n_rest - n_early, first=n_early),
                       name="gather_late_wait")
    fwd_ffn2 = _gather_forward_plan(rest_rows[n_early:])
    ssem, rsem, parts, token = _split_start(late[n_rest - n_early:], 3 * (n_rest - n_early), fwd_ffn2,
                                            name="forward_ffn2_start")
    x2, hm, z, y = _mixer_fwd(x1, mix_norm, win_t, wout, conv_full, attn_sinks, rope, name="mixer_fwd", after=[token])
    w2_gu, w2_d = _split_wait(ssem, rsem, parts, x2, fwd_ffn2, name="forward_ffn2_wait")
    dx3, h2, s2, sa2, sb2, loss_local, d_final = _ffn_fwd(x2, ffn2_norm, w2_gu, w2_d, head=(final_gain, target),
                                                          name="ffn2_fwd")

    def to_sibling_start(grads, tag, after=()):
        views = [g.reshape(4, 2, g.shape[0] // N_DEV, D_MODEL) for g in grads]
        lands = [lax.empty((4,) + v.shape[2:], F32) for v in views]
        plan = _sibling_plan(len(views))
        ssem, rsem, bufs, token = _split_start(views + lands, len(views), plan, name=f"{tag}_sibling_start", after=after)
        return (ssem, rsem, bufs, plan, tag), token

    def to_sibling_finish(handle, after, names):
        ssem, rsem, bufs, plan, tag = handle
        bufs = _split_wait(ssem, rsem, bufs, after, plan, name=f"{tag}_sibling_wait")
        n = len(names)
        return _add_sibling(bufs[:n], bufs[n:], place, name=f"add_sibling_{tag}")

    def to_chips_start(partials, tag, small_all=None, after=()):
        p16 = [p for _, p in partials]
        lands = [lax.empty((3,) + p.shape[1:], BF16) for p in p16]
        extra = [] if small_all is None else [small_all]
        plan = _chips_plan(len(p16), small_all is not None)
        ssem, rsem, bufs, token = _split_start(p16 + lands + extra, 3 * len(p16) + 7 * len(extra), plan,
                                               name=f"{tag}_chips_start", after=after)
        return (ssem, rsem, bufs, plan, tag), token

    def to_chips_finish(handle, partials, after, names):
        ssem, rsem, bufs, plan, tag = handle
        bufs = _split_wait(ssem, rsem, bufs, after, plan, name=f"{tag}_chips_wait")
        n = len(names)
        return [(p32, r) for (p32, _), r in zip(partials, bufs[n:2 * n])], bufs[2 * n:]

    half_ff = D_FF // 2
    names2, namesm = ["ffn2_w_gate", "ffn2_w_up", "ffn2_w_down"], ["w_in", "w_out"]
    transposed = {"ffn1_w_gate", "ffn1_w_up", "w_in", "ffn2_w_gate", "ffn2_w_up"}
    grad, delta, new_m, new_v = {}, {}, {}, {}

    def adam_big(names, parts, tag, after=()):
        def to_rows(nm, a):
            return jnp.swapaxes(a, 1, 2)[0] if nm in transposed else a[0]

        def from_rows(nm, a):
            return jnp.swapaxes(a[None], 1, 2) if nm in transposed else a[None]

        operands = [(p32, recv, to_rows(nm, given[nm]), to_rows(nm, moments_m[nm]), to_rows(nm, moments_v[nm]))
                    for nm, (p32, recv) in zip(names, parts)]
        results, token = _reduce_adamw(operands, name=f"adamw_{tag}", after=after)
        for nm, outs in zip(names, results):
            grad[nm], delta[nm], new_m[nm], new_v[nm] = (from_rows(nm, a) for a in outs)
        return token

    dx2, da2, db2, g2b, d_norm2 = _ffn_dgrad(dx3, x2, ffn2_norm, sa2, sb2, w2_gu, w2_d, name="ffn2_dgrad")
    gw2 = [_tn_matmul(da2, h2, name="ffn2_wgrad_gate", bm=half_ff), _tn_matmul(db2, h2, name="ffn2_wgrad_up", bm=half_ff),
           _tn_matmul(s2, g2b, name="ffn2_wgrad_down", bm=half_ff)]
    sib2, tok = to_sibling_start(gw2, "ffn2")
    dx1, dz, gmb, d_conv, d_sink, d_normm = _mixer_bwd(dx2, x1, mix_norm, y, z, win_t, wout, conv_full, attn_sinks,
                                                       rope, name="mixer_bwd", after=[tok])
    p2 = to_sibling_finish(sib2, dx1, names2)
    chips2, tok = to_chips_start(p2, "ffn2")
    gwm = [_tn_matmul(dz, hm, name="mixer_wgrad_in", bm=Z_W // 2, after=[tok]),
           _tn_matmul(y, gmb, name="mixer_wgrad_out", bm=D_MODEL, after=[tok])]
    sibm, tok = to_sibling_start(gwm, "mixer")
    dx0, da1, db1, g1b, d_norm1 = _ffn_dgrad(dx1, xs, ffn1_norm, sa1, sb1, w1_gu, w1_d, name="ffn1_dgrad", after=[tok])
    r2, _ = to_chips_finish(chips2, p2, dx0, names2)
    pm = to_sibling_finish(sibm, dx0, namesm)
    chipsm, tok = to_chips_start(pm, "mixer")
    gw_gate = _tn_matmul(da1, h1, name="ffn1_wgrad_gate", bm=half_ff, after=[tok])
    sib_gate, tok = to_sibling_start([gw_gate], "ffn1_gate")
    gw_up = _tn_matmul(db1, h1, name="ffn1_wgrad_up", bm=half_ff, after=[tok])
    rm, _ = to_chips_finish(chipsm, pm, gw_up, namesm)
    p_gate = to_sibling_finish(sib_gate, gw_up, ["ffn1_w_gate"])
    chips_gate, tok_a = to_chips_start(p_gate, "ffn1_gate")
    sib_up, tok_b = to_sibling_start([gw_up], "ffn1_up", after=[tok_a])
    gw_down = _tn_matmul(s1, g1b, name="ffn1_wgrad_down_first", bm=half_ff, blocks=(0, 1), after=[tok_a, tok_b])
    p_up = to_sibling_finish(sib_up, gw_down, ["ffn1_w_up"])
    chips_up, tok_a = to_chips_start(p_up, "ffn1_up")
    gw_down = _tn_matmul(s1, g1b, name="ffn1_wgrad_down_second", bm=half_ff, blocks=(1, 1), into=gw_down, after=[tok_a])
    sib_down, tok_b = to_sibling_start([gw_down], "ffn1_down")
    p_down = to_sibling_finish(sib_down, tok_b, ["ffn1_w_down"])
    last_row = (jnp.pad(d_sink, ((0, 0), (0, D_MODEL - 128)))
                + jnp.pad(loss_local, ((0, 0), (LOSS_LANE, D_MODEL - LOSS_LANE - 1))))
    small = jnp.concatenate([
        d_norm1, d_normm, d_norm2, d_final, jnp.pad(d_conv[0:3], ((0, 0), (0, D_MODEL - CONV_W))), last_row], axis=0)
    (small_all,) = _place_own([lax.empty((N_DEV * 8, D_MODEL), F32)], [small], my_index.astype(jnp.int32).reshape(1),
                              name="place_own_small")
    chips_down, tok = to_chips_start(p_down, "ffn1_down", small_all)
    tok = adam_big(names2, r2, "ffn2", after=[tok])
    tok = adam_big(namesm, rm, "mixer", after=[tok])
    r_gate, _ = to_chips_finish(chips_gate, p_gate, tok, ["ffn1_w_gate"])
    tok = adam_big(["ffn1_w_gate"], r_gate, "ffn1_gate")
    r_up, _ = to_chips_finish(chips_up, p_up, tok, ["ffn1_w_up"])
    tok = adam_big(["ffn1_w_up"], r_up, "ffn1_up")
    r_down, (small_all,) = to_chips_finish(chips_down, p_down, tok, ["ffn1_w_down"])
    adam_big(["ffn1_w_down"], r_down, "ffn1_down")
    results, loss = _update_small(given, moments_m, moments_v, small_all.reshape(N_DEV, 8, D_MODEL), my_index,
                                  name="update_small")
    for nm, outs in results.items():
        grad[nm], delta[nm], new_m[nm], new_v[nm] = outs

    order = list(given)
    return (loss.reshape(()), dx0[None], *[grad[n] for n in order], *[delta[n] for n in order],
            *[new_m[n] for n in order], *[new_v[n] for n in order])
```

```python
import functools

import jax
import jax.numpy as jnp
from jax import lax
from jax.experimental import pallas as pl
from jax.experimental.pallas import tpu as pltpu

F32 = jnp.float32
BF16 = jnp.bfloat16
MESH = pl.DeviceIdType.MESH
ANY = pl.BlockSpec(memory_space=pl.ANY)
HBM_SPEC = pl.BlockSpec(memory_space=pltpu.HBM)
SEM_SPEC = pl.BlockSpec(memory_space=pltpu.SEMAPHORE)
DATAFLOW = pltpu.SideEffectType.DATAFLOW_SIDE_EFFECTING

N_DEV = 8
LOSS_LANE = 128
D_MODEL = 1024
D_FF = 2816
CONV_W = 512
ATTN_W = 512
KV_W = 128
HEAD_DIM = 64
N_Q_HEADS = 8
N_KV_HEADS = 2
Q_PER_KV = N_Q_HEADS // N_KV_HEADS
BLOCK = 128
ROT_DIM = 16
ROPE_THETA = 500000.0
Z_W = 3 * CONV_W + ATTN_W + 2 * KV_W
Q_OFF = 3 * CONV_W
K_OFF = Q_OFF + ATTN_W
V_OFF = K_OFF + KV_W
RMS_EPS = 1e-5
MASK_VALUE = -1e30
SM_SCALE = HEAD_DIM ** -0.5
FFN_RES_SCALE = 0.5

ADAM_LR = 0.001
ADAM_B1 = 0.9
ADAM_B2 = 0.999
ADAM_EPS = 1e-08
ADAM_WD = 0.01
ADAM_STEP = 10

NT_DIMS = (((1,), (1,)), ((), ()))
TN_DIMS = (((0,), (0,)), ((), ()))

VMEM_LIMIT = 62 * 1024 * 1024
FF_CHUNK = 256


def _params(sem, vmem=None):
    return pltpu.CompilerParams(dimension_semantics=sem, vmem_limit_bytes=vmem)


def _behind(body, n_in, after):
    k = len(after)
    if k == 0:
        return body
    return lambda *refs: body(*refs[:n_in], *refs[n_in + k:])


def _rms_stats(xf):
    inv = lax.rsqrt(jnp.mean(xf * xf, axis=-1, keepdims=True) + RMS_EPS)
    return xf * inv, inv


def _rms_bwd(dh, xhat, inv, gain):
    dxhat = dh * gain
    dx = inv * (dxhat - xhat * jnp.mean(dxhat * xhat, axis=-1, keepdims=True))
    dgain = jnp.sum(dh * xhat, axis=0, keepdims=True)
    return dx, dgain


LOAD_PIECES = 4


def _load_resident(pairs, sems):
    @pl.when(pl.program_id(0) == 0)
    def _():
        copies = []
        for k, (w_hbm, w_ref) in enumerate(pairs):
            rows = w_hbm.shape[-2] // LOAD_PIECES
            for p in range(LOAD_PIECES):
                piece = (slice(None),) * (len(w_hbm.shape) - 2) + (pl.ds(p * rows, rows), slice(None))
                copies.append(pltpu.make_async_copy(w_hbm.at[piece], w_ref.at[piece], sems.at[k, p]))
        for cp in copies:
            cp.start()
        for cp in copies:
            cp.wait()


def _ffn_fwd(x, gain, w_gu, w_d=None, *, name, head=None, after=(), tm=256, sub=256, tf=FF_CHUNK):
    t = x.shape[0]
    tm = min(tm, t)
    sub = min(sub, tm)
    n_down = 0 if w_d is None else 1
    n_head = 0 if head is None else 2
    assert n_down or not n_head
    n_in = 3 + n_down + n_head

    def body(*refs):
        x_ref, g_ref = refs[:2]
        w_hbms, head_refs = refs[2:3 + n_down], refs[3 + n_down:n_in]
        xo_refs = refs[n_in:n_in + n_down]
        h_ref, s_ref, sa_ref, sb_ref = refs[n_in + n_down:n_in + n_down + 4]
        head_outs = refs[n_in + n_down + 4:n_in + n_down + 4 + n_head]
        w_refs, sems = refs[n_in + n_down + 4 + n_head:-1], refs[-1]
        _load_resident(list(zip(w_hbms, w_refs)), sems)

        @pl.when(pl.program_id(0) == 0)
        def _():
            for ref in head_outs:
                ref[...] = jnp.zeros_like(ref)

        for r0 in range(0, tm, sub):
            rows = slice(r0, r0 + sub)
            xf = x_ref[rows, :]
            xhat, _ = _rms_stats(xf)
            h = (xhat * g_ref[...]).astype(BF16)
            h_ref[rows, :] = h
            for c in range(0, D_FF, tf):
                cols = slice(c, min(c + tf, D_FF))
                a = lax.dot_general(h, w_refs[0][0, cols, :], NT_DIMS, preferred_element_type=F32)
                b = lax.dot_general(h, w_refs[0][1, cols, :], NT_DIMS, preferred_element_type=F32)
                sig = jax.nn.sigmoid(a)
                silu = a * sig
                s_ref[rows, cols] = (silu * b).astype(BF16)
                sa_ref[rows, cols] = (b * (sig * (1.0 + a * (1.0 - sig)))).astype(BF16)
                sb_ref[rows, cols] = silu.astype(BF16)
            if not n_down:
                continue
            xo = xf + FFN_RES_SCALE * jnp.dot(s_ref[rows, :], w_refs[1][...], preferred_element_type=F32)
            if head is None:
                xo_refs[0][rows, :] = xo
            else:
                fg_ref, t_ref = head_refs
                loss_ref, dfg_ref = head_outs
                xhat_o, inv_o = _rms_stats(xo)
                err = xhat_o * fg_ref[...] - t_ref[rows, :]
                loss_ref[...] += 0.5 * jnp.sum(jnp.mean(err * err, axis=-1, keepdims=True), axis=0, keepdims=True)
                xo_refs[0][rows, :], dfg = _rms_bwd(err * (1.0 / D_MODEL), xhat_o, inv_o, fg_ref[...])
                dfg_ref[...] += dfg

    row = pl.BlockSpec((tm, D_MODEL), lambda i: (i, 0))
    hid = pl.BlockSpec((tm, D_FF), lambda i: (i, 0))
    vec = pl.BlockSpec((1, D_MODEL), lambda i: (0, 0))
    head_in = [] if head is None else [vec, row]
    head_out = [] if head is None else [pl.BlockSpec((1, 1), lambda i: (0, 0)), vec]
    head_shape = [] if head is None else [jax.ShapeDtypeStruct((1, 1), F32), jax.ShapeDtypeStruct((1, D_MODEL), F32)]
    return pl.pallas_call(
        _behind(body, n_in, after), name=name, grid=(t // tm,),
        in_specs=[row, vec] + [ANY] * (1 + n_down) + head_in + [ANY] * len(after),
        out_specs=[row] * (n_down + 1) + [hid, hid, hid] + head_out,
        out_shape=[jax.ShapeDtypeStruct((t, D_MODEL), F32)] * n_down + [jax.ShapeDtypeStruct((t, D_MODEL), BF16)]
        + [jax.ShapeDtypeStruct((t, D_FF), BF16)] * 3 + head_shape,
        scratch_shapes=[pltpu.VMEM((2, D_FF, D_MODEL), BF16)] + [pltpu.VMEM((D_FF, D_MODEL), BF16)] * n_down
        + [pltpu.SemaphoreType.DMA((2, LOAD_PIECES))],
        compiler_params=_params(("arbitrary",), VMEM_LIMIT),
    )(x, gain, w_gu, *([] if w_d is None else [w_d]), *(head or ()), *after)


def _ffn_down(x, s, w_d, *, name, after=(), tm=512):
    t = x.shape[0]
    tm = min(tm, t)

    def body(x_ref, s_ref, w_hbm, xo_ref, w_ref, sem):
        _load_resident([(w_hbm, w_ref)], sem)
        xo_ref[...] = x_ref[...] + FFN_RES_SCALE * jnp.dot(s_ref[...], w_ref[...], preferred_element_type=F32)

    row = pl.BlockSpec((tm, D_MODEL), lambda i: (i, 0))
    return pl.pallas_call(
        _behind(body, 3, after), name=name, grid=(t // tm,),
        in_specs=[row, pl.BlockSpec((tm, D_FF), lambda i: (i, 0)), ANY] + [ANY] * len(after), out_specs=row,
        out_shape=jax.ShapeDtypeStruct((t, D_MODEL), F32),
        scratch_shapes=[pltpu.VMEM((D_FF, D_MODEL), BF16), pltpu.SemaphoreType.DMA((1, LOAD_PIECES))],
        compiler_params=_params(("arbitrary",), VMEM_LIMIT),
    )(x, s, w_d, *after)


def _ffn_dgrad(dxo, x, gain, sa, sb, w_gu, w_d, *, name, after=(), tm=512, sub=512, tf=FF_CHUNK):
    t = x.shape[0]
    tm = min(tm, t)
    sub = min(sub, tm)

    def body(dxo_ref, x_ref, g_ref, sa_ref, sb_ref, wgu_hbm, wd_hbm, dxi_ref, da_ref, db_ref, gb_ref, dg_ref,
             wgu_ref, wd_ref, sems):
        _load_resident([(wd_hbm, wd_ref), (wgu_hbm, wgu_ref)], sems)

        @pl.when(pl.program_id(0) == 0)
        def _():
            dg_ref[...] = jnp.zeros_like(dg_ref)

        for r0 in range(0, tm, sub):
            rows = slice(r0, r0 + sub)
            go = dxo_ref[rows, :]
            gb = (FFN_RES_SCALE * go).astype(BF16)
            gb_ref[rows, :] = gb
            for c in range(0, D_FF, tf):
                cols = slice(c, min(c + tf, D_FF))
                ds = lax.dot_general(gb, wd_ref[cols, :], NT_DIMS, preferred_element_type=F32)
                da_ref[rows, cols] = (ds * sa_ref[rows, cols].astype(F32)).astype(BF16)
                db_ref[rows, cols] = (ds * sb_ref[rows, cols].astype(F32)).astype(BF16)
            dh = (jnp.dot(da_ref[rows, :], wgu_ref[0], preferred_element_type=F32)
                  + jnp.dot(db_ref[rows, :], wgu_ref[1], preferred_element_type=F32))
            xhat, inv = _rms_stats(x_ref[rows, :])
            dx, dgain = _rms_bwd(dh, xhat, inv, g_ref[...])
            dxi_ref[rows, :] = go + dx
            dg_ref[...] += dgain

    row = pl.BlockSpec((tm, D_MODEL), lambda i: (i, 0))
    hid = pl.BlockSpec((tm, D_FF), lambda i: (i, 0))
    vec = pl.BlockSpec((1, D_MODEL), lambda i: (0, 0))
    return pl.pallas_call(
        _behind(body, 7, after), name=name, grid=(t // tm,),
        in_specs=[row, row, vec, hid, hid, ANY, ANY] + [ANY] * len(after),
        out_specs=[row, hid, hid, row, vec],
        out_shape=[jax.ShapeDtypeStruct((t, D_MODEL), F32), jax.ShapeDtypeStruct((t, D_FF), BF16),
                   jax.ShapeDtypeStruct((t, D_FF), BF16),
                   jax.ShapeDtypeStruct((t, D_MODEL), BF16), jax.ShapeDtypeStruct((1, D_MODEL), F32)],
        scratch_shapes=[pltpu.VMEM((2, D_FF, D_MODEL), BF16), pltpu.VMEM((D_FF, D_MODEL), BF16),
                        pltpu.SemaphoreType.DMA((2, LOAD_PIECES))],
        compiler_params=_params(("arbitrary",), VMEM_LIMIT),
    )(dxo, x, gain, sa, sb, w_gu, w_d, *after)


def _tn_matmul(a, b, *, name, bm, after=(), tk=2048, blocks=None, into=None):
    t, m = a.shape
    n = b.shape[1]
    tk = min(tk, t)
    nk = t // tk
    first, count = blocks or (0, m // bm)
    behind = ([] if into is None else [into]) + list(after)

    def body(a_ref, b_ref, o_ref):
        @pl.when(pl.program_id(1) == 0)
        def _():
            o_ref[...] = jnp.zeros_like(o_ref)

        o_ref[...] += lax.dot_general(a_ref[...], b_ref[...], TN_DIMS, preferred_element_type=F32)

    return pl.pallas_call(
        _behind(body, 2, behind), name=name, grid=(count, nk),
        in_specs=[pl.BlockSpec((tk, bm), lambda i, k: (k, first + i)), pl.BlockSpec((tk, n), lambda i, k: (k, 0))]
        + [ANY] * len(behind),
        out_specs=pl.BlockSpec((bm, n), lambda i, k: (first + i, 0)),
        out_shape=jax.ShapeDtypeStruct((m, n), F32),
        input_output_aliases={} if into is None else {2: 0},
        compiler_params=_params(("parallel", "arbitrary"), VMEM_LIMIT),
    )(a, b, *behind)


def _rope_tables(t, after=()):
    inv_freq = ROPE_THETA ** (-jnp.arange(0, ROT_DIM, 2, dtype=F32) / ROT_DIM)
    ang = inv_freq[:, None] * jnp.arange(t, dtype=F32)[None, :]
    compact = jnp.stack([jnp.cos(ang), jnp.sin(ang)])
    tr = min(1024, t)

    def body(c_ref, o_ref):
        for k in range(2):
            o_ref[k] = jnp.tile(c_ref[k], (128 // inv_freq.shape[0], 1)).T

    return pl.pallas_call(
        _behind(body, 1, after), name="rope_tables", grid=(t // tr,),
        in_specs=[pl.BlockSpec((2, inv_freq.shape[0], tr), lambda i: (0, 0, i))] + [ANY] * len(after),
        out_specs=pl.BlockSpec((2, tr, 128), lambda i: (0, i, 0)),
        out_shape=jax.ShapeDtypeStruct((2, t, 128), F32), compiler_params=_params(("parallel",)),
    )(compact, *after)


def _rope_multipliers(cos_sin):
    half = ROT_DIM // 2
    cos, sin = cos_sin[0], cos_sin[1]
    d = lax.broadcasted_iota(jnp.int32, cos.shape, 1) & (HEAD_DIM - 1)
    mult = jnp.where(d < ROT_DIM, cos, 1.0)
    from_lo = jnp.where((d >= half) & (d < ROT_DIM), sin, 0.0)
    from_hi = jnp.where(d < half, -sin, 0.0)
    return mult, from_lo, from_hi


def _tile_lanes(tab, width):
    return jnp.tile(tab, (1, width // tab.shape[1]))


def _rope(v, tab):
    w = v.shape[1]
    half_rot = ROT_DIM // 2
    return (v * _tile_lanes(tab[0], w)
            + pltpu.roll(v, half_rot, axis=1) * _tile_lanes(tab[1], w)
            + pltpu.roll(v, w - half_rot, axis=1) * _tile_lanes(tab[2], w))


def _rope_bwd(dv, tab):
    w = dv.shape[1]
    half_rot = ROT_DIM // 2
    return (dv * _tile_lanes(tab[0], w)
            + pltpu.roll(dv * _tile_lanes(tab[1], w), w - half_rot, axis=1)
            + pltpu.roll(dv * _tile_lanes(tab[2], w), half_rot, axis=1))


def _shift_rows(v, prev8_ref, n):
    r = lax.broadcasted_iota(jnp.int32, v.shape, 0)
    rolled = pltpu.roll(v, n, axis=0)
    last = prev8_ref[7:8, :]
    if n == 1:
        return jnp.where(r >= 1, rolled, last)
    return jnp.where(r >= 2, rolled, jnp.where(r == 0, prev8_ref[6:7, :], last))


def _shift_rows_up(v, next8_ref, n):
    rows = v.shape[0]
    r = lax.broadcasted_iota(jnp.int32, v.shape, 0)
    rolled = pltpu.roll(v, rows - n, axis=0)
    first = next8_ref[0:1, :]
    if n == 1:
        return jnp.where(r <= rows - 2, rolled, first)
    return jnp.where(r <= rows - 3, rolled, jnp.where(r == rows - 2, first, next8_ref[1:2, :]))


def _lane_half_mask(shape, half):
    lane = lax.broadcasted_iota(jnp.int32, shape, 1)
    return (lane >= HEAD_DIM) if half else (lane < HEAD_DIM)


def _to_kv_lanes(chunk, head, kv):
    if head % 2 != kv:
        chunk = pltpu.roll(chunk, HEAD_DIM, axis=1)
    return jnp.where(_lane_half_mask(chunk.shape, kv), chunk, 0.0)


def _from_kv_lanes(chunk, head, kv):
    chunk = jnp.where(_lane_half_mask(chunk.shape, kv), chunk, 0.0)
    if head % 2 != kv:
        chunk = pltpu.roll(chunk, HEAD_DIM, axis=1)
    return chunk


def _stack_heads(wide):
    parts = []
    for head in range(N_Q_HEADS):
        chunk = wide[:, (head // 2) * 128:(head // 2 + 1) * 128]
        parts.append(_to_kv_lanes(chunk, head, head // Q_PER_KV))
    return jnp.concatenate(parts, axis=0)


def _window_mask(has_prev):
    shape = (N_Q_HEADS * BLOCK, 2 * BLOCK)
    qi = lax.broadcasted_iota(jnp.int32, shape, 0) & (BLOCK - 1)
    kj = lax.broadcasted_iota(jnp.int32, shape, 1)
    first_key = BLOCK - has_prev * BLOCK
    in_prev = (kj < BLOCK) & (kj > qi) & (kj >= first_key)
    in_own = (kj >= BLOCK) & ((kj - BLOCK) <= qi)
    return in_prev | in_own


def _sink_column(sink_ref):
    row = lax.broadcasted_iota(jnp.int32, (N_Q_HEADS * BLOCK, 1), 0)
    col = jnp.full((N_Q_HEADS * BLOCK, 1), sink_ref[0, 0], F32)
    for head in range(1, N_Q_HEADS):
        col = jnp.where(row >= head * BLOCK, sink_ref[0, head], col)
    return col


def _softmax_with_sink(q4, k2, mask, sink):
    s = lax.dot_general(q4, k2, NT_DIMS, preferred_element_type=F32) * SM_SCALE
    s = jnp.where(mask, s, MASK_VALUE)
    m = jnp.maximum(jnp.max(s, axis=-1, keepdims=True), sink)
    p = jnp.exp(s - m)
    e_sink = jnp.exp(sink - m)
    inv_den = 1.0 / (jnp.sum(p, axis=-1, keepdims=True) + e_sink)
    return p * inv_den, e_sink * inv_den


def _conv_terms(zf, prev8_ref, w_ref):
    b_gate, c_gate, u = zf[:, 0:CONV_W], zf[:, CONV_W:2 * CONV_W], zf[:, 2 * CONV_W:3 * CONV_W]
    vc = c_gate * u
    vm1 = _shift_rows(vc, prev8_ref, 1)
    vm2 = _shift_rows(vc, prev8_ref, 2)
    conv = w_ref[0:1, :] * vm2 + w_ref[1:2, :] * vm1 + w_ref[2:3, :] * vc
    return b_gate, c_gate, u, vc, vm1, vm2, conv


def _mixer_fwd(x, gain, win_t, wout, conv_w, sinks, rope, *, name, after=(), tq=512):
    t = x.shape[0]
    tq = min(tq, t)
    nblk = tq // BLOCK

    def body(x_ref, g_ref, win_hbm, wout_hbm, cw_ref, sink_ref, rope_ref,
             xo_ref, h_ref, z_ref, y_ref, kprev_ref, vprev_ref, cprev_ref, win_ref, wout_ref, sems):
        i = pl.program_id(0)
        _load_resident([(win_hbm, win_ref), (wout_hbm, wout_ref)], sems)

        @pl.when(i == 0)
        def _():
            kprev_ref[...] = jnp.zeros_like(kprev_ref)
            vprev_ref[...] = jnp.zeros_like(vprev_ref)
            cprev_ref[...] = jnp.zeros_like(cprev_ref)

        xf = x_ref[...]
        xhat, _ = _rms_stats(xf)
        h = (xhat * g_ref[...]).astype(BF16)
        h_ref[...] = h

        def project(c0, c1):
            zc = lax.dot_general(h, win_ref[c0:c1, :], NT_DIMS, preferred_element_type=F32).astype(BF16)
            z_ref[:, c0:c1] = zc
            return zc

        zb = project(Q_OFF, Z_W)
        zf = zb.astype(F32)
        tab = _rope_multipliers(rope_ref[...])
        qr = _rope(zf[:, 0:ATTN_W], tab)
        kr = _rope(zf[:, K_OFF - Q_OFF:V_OFF - Q_OFF], tab).astype(BF16)
        vb = zb[:, V_OFF - Q_OFF:Z_W - Q_OFF]
        conv_cols = [(c, c + CONV_W) for c in range(0, Q_OFF, CONV_W)]
        conv_z = []

        y_attn = []
        for j in range(nblk):
            if len(conv_z) < len(conv_cols):
                conv_z.append(project(*conv_cols[len(conv_z)]))
            rows = slice(j * BLOCK, (j + 1) * BLOCK)
            prev = slice((j - 1) * BLOCK, j * BLOCK)
            k2 = jnp.concatenate([kprev_ref[...] if j == 0 else kr[prev], kr[rows]], axis=0)
            v2 = jnp.concatenate([vprev_ref[...] if j == 0 else vb[prev], vb[rows]], axis=0)
            mask = _window_mask(jnp.minimum(i, 1) if j == 0 else 1)
            q8 = _stack_heads(qr[rows]).astype(BF16)
            probs, _ = _softmax_with_sink(q8, k2, mask, _sink_column(sink_ref))
            o8 = jnp.dot(probs.astype(BF16), v2, preferred_element_type=F32)
            chunks = [jnp.zeros((BLOCK, 128), F32) for _ in range(ATTN_W // 128)]
            for head in range(N_Q_HEADS):
                chunks[head // 2] += _from_kv_lanes(o8[head * BLOCK:(head + 1) * BLOCK], head, head // Q_PER_KV)
            y_attn.append(jnp.concatenate(chunks, axis=1))
        kprev_ref[...] = kr[tq - BLOCK:tq]
        vprev_ref[...] = vb[tq - BLOCK:tq]
        while len(conv_z) < len(conv_cols):
            conv_z.append(project(*conv_cols[len(conv_z)]))
        ya = jnp.concatenate(y_attn, axis=0).astype(BF16)
        y_ref[:, CONV_W:] = ya
        xo = xf + jnp.dot(ya, wout_ref[CONV_W:, :], preferred_element_type=F32)
        b_gate, _, _, vc, _, _, conv = _conv_terms(jnp.concatenate(conv_z, axis=1).astype(F32), cprev_ref, cw_ref)
        yc = (b_gate * conv).astype(BF16)
        cprev_ref[...] = vc[tq - 8:tq, :]
        y_ref[:, :CONV_W] = yc
        xo_ref[...] = xo + jnp.dot(yc, wout_ref[:CONV_W, :], preferred_element_type=F32)

    row = pl.BlockSpec((tq, D_MODEL), lambda i: (i, 0))
    full = lambda shape: pl.BlockSpec(shape, lambda i: (0,) * len(shape))
    return pl.pallas_call(
        _behind(body, 7, after), name=name, grid=(t // tq,),
        in_specs=[row, full((1, D_MODEL)), ANY, ANY, full((3, CONV_W)),
                  pl.BlockSpec(memory_space=pltpu.SMEM), pl.BlockSpec((2, tq, 128), lambda i: (0, i, 0))]
        + [ANY] * len(after),
        out_specs=[row, row, pl.BlockSpec((tq, Z_W), lambda i: (i, 0)), row],
        out_shape=[jax.ShapeDtypeStruct((t, D_MODEL), F32), jax.ShapeDtypeStruct((t, D_MODEL), BF16),
                   jax.ShapeDtypeStruct((t, Z_W), BF16), jax.ShapeDtypeStruct((t, D_MODEL), BF16)],
        scratch_shapes=[pltpu.VMEM((BLOCK, KV_W), BF16), pltpu.VMEM((BLOCK, KV_W), BF16),
                        pltpu.VMEM((8, CONV_W), F32), pltpu.VMEM((Z_W, D_MODEL), BF16),
                        pltpu.VMEM((D_MODEL, D_MODEL), BF16), pltpu.SemaphoreType.DMA((2, LOAD_PIECES))],
        compiler_params=_params(("arbitrary",), VMEM_LIMIT),
    )(x, gain, win_t, wout, conv_w, sinks, rope, *after)


def _mixer_bwd(dxo, x, gain, y, z, win_t, wout, conv_w, sinks, rope, *, name, after=(), tq=256):
    t = x.shape[0]
    tq = min(tq, t)
    nt, nblk = t // tq, tq // BLOCK

    def body(dxo_ref, x_ref, g_ref, y_ref, z_ref, zp_ref, win_hbm, wout_hbm, cw_ref, sink_ref, rope_ref, ropep_ref,
             dxi_ref, dz_ref, gb_ref, dcw_ref, dsink_ref, dg_ref, dk_ref, dv_ref, dcn_ref, pvc_ref,
             win_ref, wout_ref, sems):
        i = pl.program_id(0)
        tile = nt - 1 - i
        _load_resident([(win_hbm, win_ref), (wout_hbm, wout_ref)], sems)

        @pl.when(i == 0)
        def _():
            dk_ref[...] = jnp.zeros_like(dk_ref)
            dv_ref[...] = jnp.zeros_like(dv_ref)
            dcn_ref[...] = jnp.zeros_like(dcn_ref)
            dcw_ref[...] = jnp.zeros_like(dcw_ref)
            dsink_ref[...] = jnp.zeros_like(dsink_ref)
            dg_ref[...] = jnp.zeros_like(dg_ref)

        has_prev = jnp.minimum(tile, 1)
        go = dxo_ref[...]
        gb = go.astype(BF16)
        gb_ref[...] = gb
        dy = lax.dot_general(gb, wout_ref[...], NT_DIMS, preferred_element_type=F32)
        dy_conv, dy_attn = dy[:, 0:CONV_W], dy[:, CONV_W:D_MODEL]
        zb, zpb = z_ref[...], zp_ref[...]
        zf = zb.astype(F32)
        zpf = zpb.astype(F32) * has_prev.astype(F32)

        pvc_ref[...] = (zpf[:, CONV_W:2 * CONV_W] * zpf[:, 2 * CONV_W:3 * CONV_W])[BLOCK - 8:BLOCK, :]
        b_gate, c_gate, u, vc, vm1, vm2, conv = _conv_terms(zf, pvc_ref, cw_ref)
        d_bgate = dy_conv * conv
        dc = dy_conv * b_gate
        tap = lax.broadcasted_iota(jnp.int32, (8, CONV_W), 0)
        dcw_ref[...] += jnp.where(tap == 0, jnp.sum(dc * vm2, axis=0, keepdims=True),
                                  jnp.where(tap == 1, jnp.sum(dc * vm1, axis=0, keepdims=True),
                                            jnp.where(tap == 2, jnp.sum(dc * vc, axis=0, keepdims=True), 0.0)))
        dvc = (cw_ref[2:3, :] * dc + cw_ref[1:2, :] * _shift_rows_up(dc, dcn_ref, 1)
               + cw_ref[0:1, :] * _shift_rows_up(dc, dcn_ref, 2))
        dcn_ref[...] = dc[0:8, :]
        d_cgate = dvc * u
        d_u = dvc * c_gate

        tab, tabp = _rope_multipliers(rope_ref[...]), _rope_multipliers(ropep_ref[...])
        qr = _rope(zf[:, Q_OFF:K_OFF], tab)
        kr = _rope(zf[:, K_OFF:V_OFF], tab).astype(BF16)
        kpr = _rope(zpf[:, K_OFF:V_OFF], tabp).astype(BF16)
        vb, vpb = zb[:, V_OFF:Z_W], zpb[:, V_OFF:Z_W]
        out = y_ref[:, CONV_W:D_MODEL].astype(F32)
        do_out = dy_attn * out
        lane = lax.broadcasted_iota(jnp.int32, (1, 128), 1)
        dsink = jnp.zeros((1, 128), F32)
        dk_next, dv_next = dk_ref[...], dv_ref[...]
        dq_rows, dk_rows, dv_rows = [None] * nblk, [None] * nblk, [None] * nblk
        for j in reversed(range(nblk)):
            rows = slice(j * BLOCK, (j + 1) * BLOCK)
            prev = slice((j - 1) * BLOCK, j * BLOCK)
            k2 = jnp.concatenate([kpr if j == 0 else kr[prev], kr[rows]], axis=0)
            v2 = jnp.concatenate([vpb if j == 0 else vb[prev], vb[rows]], axis=0)
            mask = _window_mask(has_prev if j == 0 else 1)
            q8 = _stack_heads(qr[rows]).astype(BF16)
            do8 = _stack_heads(dy_attn[rows]).astype(BF16)
            delta = jnp.sum(_stack_heads(do_out[rows]), axis=-1, keepdims=True)
            probs, p_sink = _softmax_with_sink(q8, k2, mask, _sink_column(sink_ref))
            dp = lax.dot_general(do8, v2, NT_DIMS, preferred_element_type=F32)
            ds = (probs * (dp - delta) * SM_SCALE).astype(BF16)
            dq8 = jnp.dot(ds, k2, preferred_element_type=F32)
            dk2 = lax.dot_general(ds, q8, TN_DIMS, preferred_element_type=F32)
            dv2 = lax.dot_general(probs.astype(BF16), do8, TN_DIMS, preferred_element_type=F32)
            sink_terms = p_sink * delta
            dq_chunks = [jnp.zeros((BLOCK, 128), F32) for _ in range(ATTN_W // 128)]
            for head in range(N_Q_HEADS):
                grp = slice(head * BLOCK, (head + 1) * BLOCK)
                dq_chunks[head // 2] += _from_kv_lanes(dq8[grp], head, head // Q_PER_KV)
                dsink = dsink - jnp.where(lane == head, jnp.sum(sink_terms[grp], axis=0, keepdims=True), 0.0)
            dq_rows[j] = jnp.concatenate(dq_chunks, axis=1)
            dk_rows[j] = dk2[BLOCK:] + dk_next
            dv_rows[j] = dv2[BLOCK:] + dv_next
            dk_next, dv_next = dk2[:BLOCK], dv2[:BLOCK]
        dk_ref[...] = dk_next
        dv_ref[...] = dv_next
        dsink_ref[...] += dsink
        dq = _rope_bwd(jnp.concatenate(dq_rows, axis=0), tab)
        dk = _rope_bwd(jnp.concatenate(dk_rows, axis=0), tab)
        dv = jnp.concatenate(dv_rows, axis=0)

        dzb = jnp.concatenate([d_bgate, d_cgate, d_u, dq, dk, dv], axis=1).astype(BF16)
        dz_ref[...] = dzb
        dh = jnp.dot(dzb, win_ref[...], preferred_element_type=F32)
        xhat, inv = _rms_stats(x_ref[...])
        dx, dgain = _rms_bwd(dh, xhat, inv, g_ref[...])
        dxi_ref[...] = go + dx
        dg_ref[...] += dgain

    rev = lambda i: (nt - 1 - i, 0)
    block_before = lambda i: jnp.maximum((nt - 1 - i) * nblk - 1, 0)
    row = pl.BlockSpec((tq, D_MODEL), rev)
    full = lambda shape: pl.BlockSpec(shape, lambda i: (0,) * len(shape))
    return pl.pallas_call(
        _behind(body, 12, after), name=name, grid=(nt,),
        in_specs=[row, row, full((1, D_MODEL)), row,
                  pl.BlockSpec((tq, Z_W), rev), pl.BlockSpec((BLOCK, Z_W), lambda i: (block_before(i), 0)),
                  ANY, ANY, full((3, CONV_W)),
                  pl.BlockSpec(memory_space=pltpu.SMEM),
                  pl.BlockSpec((2, tq, 128), lambda i: (0, nt - 1 - i, 0)),
                  pl.BlockSpec((2, BLOCK, 128), lambda i: (0, block_before(i), 0))] + [ANY] * len(after),
        out_specs=[row, pl.BlockSpec((tq, Z_W), rev), row, full((8, CONV_W)), full((1, 128)), full((1, D_MODEL))],
        out_shape=[jax.ShapeDtypeStruct((t, D_MODEL), F32), jax.ShapeDtypeStruct((t, Z_W), BF16),
                   jax.ShapeDtypeStruct((t, D_MODEL), BF16), jax.ShapeDtypeStruct((8, CONV_W), F32),
                   jax.ShapeDtypeStruct((1, 128), F32), jax.ShapeDtypeStruct((1, D_MODEL), F32)],
        scratch_shapes=[pltpu.VMEM((BLOCK, KV_W), F32), pltpu.VMEM((BLOCK, KV_W), F32), pltpu.VMEM((8, CONV_W), F32),
                        pltpu.VMEM((8, CONV_W), F32), pltpu.VMEM((Z_W, D_MODEL), BF16),
                        pltpu.VMEM((D_MODEL, D_MODEL), BF16), pltpu.SemaphoreType.DMA((2, LOAD_PIECES))],
        compiler_params=_params(("arbitrary",), VMEM_LIMIT),
    )(dxo, x, gain, y, z, z, win_t, wout, conv_w, sinks, rope, rope, *after)


def _place():
    x, y, c = lax.axis_index("x"), lax.axis_index("y"), lax.axis_index("c")
    other_chips = [(1 - x, y), (x, 1 - y), (1 - x, 1 - y)]
    return x, y, c, other_chips


def _all_gather_rows(shards, place=(), *, name):
    n, p = len(shards), len(place)

    def body(*refs):
        srcs, place_srcs = refs[:n], refs[n:n + p]
        outs, place_outs = refs[n + p:2 * n + p], refs[2 * n + p:2 * (n + p)]
        send_sems, recv_sems, local_sems = refs[2 * (n + p):]
        x, y, c, _ = _place()
        me, sibling = (x, y, c), (x, y, 1 - c)
        relay_from = (x + (1 - c) - 2 * x * (1 - c), y + c - 2 * y * c)
        relay_to = (x + c - 2 * x * c, y + (1 - c) - 2 * y * (1 - c))
        chips = [relay_from, relay_to, (1 - x, 1 - y)]

        def rows(t, px, py, pc):
            r = srcs[t].shape[-2]
            start = pl.multiple_of((4 * px + 2 * py + pc) * r, 16 if r % 16 == 0 else 8)
            if len(srcs[t].shape) == 3:
                return outs[t].at[:, pl.ds(start, r), :]
            return outs[t].at[pl.ds(start, r), :]

        def copy(t, k, block, to, own=False):
            return pltpu.make_async_remote_copy(
                src_ref=srcs[t] if own else rows(t, *block), dst_ref=rows(t, *block),
                send_sem=send_sems.at[t, k], recv_sem=recv_sems.at[t, k], device_id=to, device_id_type=MESH)

        mine = [pltpu.make_async_copy(srcs[t], rows(t, *me), local_sems.at[t]) for t in range(n)]
        mine += [pltpu.make_async_copy(place_srcs[q],
                                       _block_rows(place_outs[q], place_srcs[q].shape[-2], 4 * x + 2 * y + c),
                                       local_sems.at[n + q]) for q in range(p)]
        for q in range(p):
            mine[n + q].start()
        first = []
        for t in range(n):
            mine[t].start()
            first.append(copy(t, 0, me, sibling, own=True))
            first += [copy(t, 1 + j, me, (*chip, c), own=True) for j, chip in enumerate(chips[:2])]
        for cp in first:
            cp.start()
        passed = []
        for j, chip in enumerate(chips):
            for t in range(n):
                copy(t, 1 + j, (*chip, c), me).wait_recv()
                if j == 0:
                    passed.append(copy(t, 3, (*chip, c), (*relay_to, c)))
                    passed[-1].start()
                passed.append(copy(t, 4 + j, (*chip, c), sibling))
                passed[-1].start()
        for t in range(n):
            copy(t, 0, sibling, me).wait_recv()
            for j, chip in enumerate([relay_to, relay_from, chips[2]]):
                copy(t, 4 + j, (*chip, 1 - c), me).wait_recv()
        for cp in first + passed:
            cp.wait_send()
        for cp in mine:
            cp.wait()

    out_shape = [jax.ShapeDtypeStruct(s.shape[:-2] + (N_DEV * s.shape[-2], s.shape[-1]), s.dtype)
                 for s in list(shards) + list(place)]
    res = pl.pallas_call(
        body, name=name, in_specs=[ANY] * (n + p), out_specs=[ANY] * (n + p), out_shape=out_shape,
        scratch_shapes=[pltpu.SemaphoreType.DMA((n, 7)), pltpu.SemaphoreType.DMA((n, 7)),
                        pltpu.SemaphoreType.DMA((n + p,))],
    )(*shards, *place)
    return res[:n], res[n:]


def _split_start(bufs, n_copies, plan, *, name, after=()):
    n = len(bufs)

    def body(*refs):
        token = refs[-1]
        for cp in plan(refs[:n], refs[n], refs[n + 1]):
            cp.start()
        token[...] = jnp.zeros_like(token)

    res = pl.pallas_call(
        _behind(body, n, after), name=name, in_specs=[HBM_SPEC] * n + [ANY] * len(after),
        out_specs=(SEM_SPEC, SEM_SPEC, *[HBM_SPEC] * n, pl.BlockSpec(memory_space=pltpu.VMEM)),
        out_shape=(pltpu.SemaphoreType.DMA((n_copies,)), pltpu.SemaphoreType.DMA((n_copies,)),
                   *[pltpu.HBM(b.shape, b.dtype) for b in bufs], jax.ShapeDtypeStruct((8, 128), F32)),
        input_output_aliases={i: 2 + i for i in range(n)},
        compiler_params=pltpu.CompilerParams(has_side_effects=DATAFLOW),
    )(*[pltpu.with_memory_space_constraint(b, pltpu.HBM) for b in bufs], *after)
    return res[0], res[1], list(res[2:2 + n]), res[-1]


def _split_wait(send_sems, recv_sems, bufs, after, plan, *, name):
    n = len(bufs)

    def body(*refs):
        for cp in plan(refs[:n], refs[n], refs[n + 1]):
            cp.wait_send()
            cp.wait_recv()

    return list(pl.pallas_call(
        body, name=name, in_specs=[HBM_SPEC] * n + [SEM_SPEC, SEM_SPEC, ANY], out_specs=[HBM_SPEC] * n,
        out_shape=tuple(pltpu.HBM(b.shape, b.dtype) for b in bufs),
        input_output_aliases={i: i for i in range(n)},
        compiler_params=pltpu.CompilerParams(has_side_effects=DATAFLOW),
    )(*bufs, send_sems, recv_sems, after))


def _sibling_plan(n, first=0):
    def plan(bufs, send_sems, recv_sems):
        x, y, c, _ = _place()
        return [_remote(bufs[t].at[:, 1 - c], bufs[n + t], send_sems, recv_sems, first + t, (x, y, 1 - c))
                for t in range(n)]
    return plan


def _block_rows(ref, r, blk):
    start = pl.multiple_of(blk * r, 16 if r % 16 == 0 else 8)
    return ref.at[(slice(None),) * (len(ref.shape) - 2) + (pl.ds(start, r), slice(None))]


def _remote(src, dst, send_sems, recv_sems, k, peer):
    return pltpu.make_async_remote_copy(src_ref=src, dst_ref=dst, send_sem=send_sems.at[k], recv_sem=recv_sems.at[k],
                                        device_id=peer, device_id_type=MESH)


def _gather_send_plan(n, first=0):
    def plan(bufs, send_sems, recv_sems):
        x, y, c, chips = _place()
        peers = [(x, y, 1 - c)] + [(px, py, c) for px, py in chips]
        copies = []
        for t in range(n):
            dst = _block_rows(bufs[n + t], bufs[t].shape[-2], 4 * x + 2 * y + c)
            copies += [_remote(bufs[t], dst, send_sems, recv_sems, 4 * (first + t) + k, peer)
                       for k, peer in enumerate(peers)]
        return copies
    return plan


def _gather_forward_plan(rows):
    def plan(bufs, send_sems, recv_sems):
        x, y, c, chips = _place()
        copies = []
        for t, r in enumerate(rows):
            for j, (px, py) in enumerate(chips):
                blk = _block_rows(bufs[t], r, 4 * px + 2 * py + c)
                copies.append(_remote(blk, blk, send_sems, recv_sems, 3 * t + j, (x, y, 1 - c)))
        return copies
    return plan


def _chips_plan(n, with_small):
    def plan(bufs, send_sems, recv_sems):
        x, y, c, chips = _place()
        copies = []
        for t in range(n):
            for j, (px, py) in enumerate(chips):
                copies.append(_remote(bufs[t].at[2 * px + py], bufs[n + t].at[j], send_sems, recv_sems, 3 * t + j,
                                      (px, py, c)))
        if with_small:
            mine = _block_rows(bufs[2 * n], 8, 4 * x + 2 * y + c)
            flips = [(fx, fy, fc) for fx in range(2) for fy in range(2) for fc in range(2)][1:]
            for k, (fx, fy, fc) in enumerate(flips):
                peer = (x + fx - 2 * x * fx, y + fy - 2 * y * fy, c + fc - 2 * c * fc)
                copies.append(_remote(mine, mine, send_sems, recv_sems, 3 * n + k, peer))
        return copies
    return plan


def _place_own(fulls, shards, index, *, name):
    n = len(fulls)

    def body(index_ref, *refs):
        for t in range(n):
            refs[2 * n + t][...] = refs[n + t][...]

    def block_of(shard):
        lead = len(shard.shape) - 2
        return pl.BlockSpec(shard.shape, lambda i, index_ref: (0,) * lead + (index_ref[0], 0))

    def whole(shard):
        return pl.BlockSpec(shard.shape, lambda i, index_ref: (0,) * len(shard.shape))

    return list(pl.pallas_call(
        body, name=name,
        grid_spec=pltpu.PrefetchScalarGridSpec(
            num_scalar_prefetch=1, grid=(1,),
            in_specs=[ANY] * n + [whole(s) for s in shards], out_specs=[block_of(s) for s in shards]),
        out_shape=[jax.ShapeDtypeStruct(f.shape, f.dtype) for f in fulls],
        input_output_aliases={1 + t: t for t in range(n)},
        compiler_params=_params(("arbitrary",)),
    )(index, *fulls, *shards))


N_STEPS_SMALL = 2


def _add_sibling(grads, recvs, place, *, name):
    n = len(grads)

    def body(place_ref, *refs):
        chip = place_ref[1]
        for t in range(n):
            g_ref, r_ref, own_ref, ob_ref = refs[2 * t], refs[2 * t + 1], refs[2 * n + 2 * t], refs[2 * n + 2 * t + 1]
            own = jnp.zeros(own_ref.shape, F32)
            for m in range(4):
                p = g_ref[m, 0] + r_ref[m]
                ob_ref[m] = p.astype(BF16)
                own = jnp.where(chip == m, p, own)
            own_ref[...] = own

    in_specs, out_specs, out_shape = [], [], []
    for g, r in zip(grads, recvs):
        tr = g.shape[2] // N_STEPS_SMALL
        blocks = pl.BlockSpec((4, tr, D_MODEL), lambda i, place_ref: (0, i, 0))
        in_specs += [pl.BlockSpec((4, 1, tr, D_MODEL), lambda i, place_ref: (0, place_ref[0], i, 0)), blocks]
        out_specs += [pl.BlockSpec((tr, D_MODEL), lambda i, place_ref: (i, 0)), blocks]
        out_shape += [jax.ShapeDtypeStruct(r.shape[1:], F32), jax.ShapeDtypeStruct(r.shape, BF16)]
    res = pl.pallas_call(
        body, name=name,
        grid_spec=pltpu.PrefetchScalarGridSpec(num_scalar_prefetch=1, grid=(N_STEPS_SMALL,), in_specs=in_specs,
                                               out_specs=out_specs),
        out_shape=out_shape, compiler_params=_params(("arbitrary",), VMEM_LIMIT),
    )(place, *[a for pair in zip(grads, recvs) for a in pair])
    return [(res[2 * t], res[2 * t + 1]) for t in range(n)]


def _reduce_adamw(parts, *, name, after=()):
    n = len(parts)

    def body(*refs):
        for t in range(n):
            p_ref, r_ref, w_ref, m_ref, v_ref = refs[5 * t:5 * t + 5]
            g_ref, d_ref, mo_ref, vo_ref = refs[5 * n + 4 * t:5 * n + 4 * t + 4]
            g = p_ref[...] + r_ref[0].astype(F32) + r_ref[1].astype(F32) + r_ref[2].astype(F32)
            g_ref[...] = g
            d_ref[...], mo_ref[...], vo_ref[...] = _adamw_math(w_ref[...], g, m_ref[...], v_ref[...])
        refs[-1][...] = jnp.zeros_like(refs[-1])

    in_specs, out_specs, out_shape = [], [], []
    for own, _, _, _, _ in parts:
        rows = own.shape[0]
        tr = rows // N_STEPS_SMALL
        spec = pl.BlockSpec((tr, D_MODEL), lambda i: (i, 0))
        in_specs += [spec, pl.BlockSpec((3, tr, D_MODEL), lambda i: (0, i, 0)), spec, spec, spec]
        out_specs += [spec] * 4
        out_shape += [jax.ShapeDtypeStruct((rows, D_MODEL), F32)] * 4
    res = pl.pallas_call(
        _behind(body, 5 * n, after), name=name, grid=(N_STEPS_SMALL,),
        in_specs=in_specs + [ANY] * len(after),
        out_specs=out_specs + [pl.BlockSpec((8, 128), lambda i: (0, 0))],
        out_shape=out_shape + [jax.ShapeDtypeStruct((8, 128), F32)],
        compiler_params=_params(("arbitrary",), VMEM_LIMIT),
    )(*[a for part in parts for a in part], *after)
    return [tuple(res[4 * t:4 * t + 4]) for t in range(n)], res[-1]


def _adamw_math(w, g, m, v):
    m = ADAM_B1 * m + (1.0 - ADAM_B1) * g
    v = ADAM_B2 * v + (1.0 - ADAM_B2) * (g * g)
    m_hat = m / (1.0 - ADAM_B1 ** ADAM_STEP)
    v_hat = v / (1.0 - ADAM_B2 ** ADAM_STEP)
    delta = -ADAM_LR * (m_hat / (jnp.sqrt(v_hat) + ADAM_EPS) + ADAM_WD * w)
    return delta, m, v


SMALL_NAMES = ["ffn1_norm", "mix_norm", "ffn2_norm", "final_norm", "conv_w", "attn_sinks"]


def _update_small(given, moments_m, moments_v, small_all, my_index, *, name):
    conv_cols = given["conv_w"].shape[2]
    per_block = 128 // conv_cols

    def two_d(nm, a):
        return a.reshape(1, D_MODEL) if nm == "final_norm" else a

    operands = [two_d(nm, src[nm]) for nm in SMALL_NAMES for src in (given, moments_m, moments_v)]
    n = len(SMALL_NAMES)

    def body(index_ref, all_ref, conv_ref, *refs):
        ins, outs = refs[:3 * n], refs[3 * n:]
        total, conv_total = all_ref[0], conv_ref[0]
        for k in range(1, N_DEV):
            total, conv_total = total + all_ref[k], conv_total + conv_ref[k]
        which = index_ref[0] % per_block
        conv_g = conv_total[4:7, :conv_cols]
        for j in range(1, per_block):
            conv_g = jnp.where(which == j, conv_total[4:7, j * conv_cols:(j + 1) * conv_cols], conv_g)
        grads = [total[0:1], total[1:2], total[2:3], total[3:4], conv_g[None], total[7:8, :N_Q_HEADS]]
        for t, g in enumerate(grads):
            w_ref, m_ref, v_ref = ins[3 * t:3 * t + 3]
            g_ref, d_ref, mo_ref, vo_ref = outs[4 * t:4 * t + 4]
            g_ref[...] = g
            d_ref[...], mo_ref[...], vo_ref[...] = _adamw_math(w_ref[...], g, m_ref[...], v_ref[...])
        outs[-1][...] = total[7:8, LOSS_LANE:LOSS_LANE + 1]

    def whole(shape):
        return pl.BlockSpec(shape, lambda i, index_ref: (0,) * len(shape))

    shapes = [a.shape for a in operands[::3] for _ in range(4)] + [(1, 1)]
    res = pl.pallas_call(
        body, name=name,
        grid_spec=pltpu.PrefetchScalarGridSpec(
            num_scalar_prefetch=1, grid=(1,),
            in_specs=[whole(small_all.shape),
                      pl.BlockSpec((N_DEV, 8, 128), lambda i, index_ref: (0, 0, index_ref[0] // per_block))]
            + [whole(a.shape) for a in operands],
            out_specs=[whole(s) for s in shapes]),
        out_shape=[jax.ShapeDtypeStruct(s, F32) for s in shapes],
        compiler_params=_params(("arbitrary",)),
    )(my_index.astype(jnp.int32).reshape(1), small_all, small_all, *operands)
    results = {nm: tuple(a.reshape(given[nm].shape) for a in res[4 * t:4 * t + 4]) for t, nm in enumerate(SMALL_NAMES)}
    return results, res[-1]


def kernel(x, ffn1_norm, ffn1_w_gate, ffn1_w_up, ffn1_w_down, mix_norm, w_in, conv_w, attn_sinks, w_out, ffn2_norm, ffn2_w_gate, ffn2_w_up, ffn2_w_down, final_norm, loss_target, m_ffn1_norm, m_ffn1_w_gate, m_ffn1_w_up, m_ffn1_w_down, m_mix_norm, m_w_in, m_conv_w, m_attn_sinks, m_w_out, m_ffn2_norm, m_ffn2_w_gate, m_ffn2_w_up, m_ffn2_w_down, m_final_norm, v_ffn1_norm, v_ffn1_w_gate, v_ffn1_w_up, v_ffn1_w_down, v_mix_norm, v_w_in, v_conv_w, v_attn_sinks, v_w_out, v_ffn2_norm, v_ffn2_w_gate, v_ffn2_w_up, v_ffn2_w_down, v_final_norm):
    ix, iy, ic = lax.axis_index("x"), lax.axis_index("y"), lax.axis_index("c")
    my_index = 4 * ix + 2 * iy + ic
    place = jnp.stack([ic, 2 * ix + iy]).astype(jnp.int32)

    given = dict(ffn1_norm=ffn1_norm, ffn1_w_gate=ffn1_w_gate, ffn1_w_up=ffn1_w_up, ffn1_w_down=ffn1_w_down,
                 mix_norm=mix_norm, w_in=w_in, conv_w=conv_w, attn_sinks=attn_sinks, w_out=w_out, ffn2_norm=ffn2_norm,
                 ffn2_w_gate=ffn2_w_gate, ffn2_w_up=ffn2_w_up, ffn2_w_down=ffn2_w_down, final_norm=final_norm)
    moments_m = dict(ffn1_norm=m_ffn1_norm, ffn1_w_gate=m_ffn1_w_gate, ffn1_w_up=m_ffn1_w_up, ffn1_w_down=m_ffn1_w_down,
                     mix_norm=m_mix_norm, w_in=m_w_in, conv_w=m_conv_w, attn_sinks=m_attn_sinks, w_out=m_w_out,
                     ffn2_norm=m_ffn2_norm, ffn2_w_gate=m_ffn2_w_gate, ffn2_w_up=m_ffn2_w_up, ffn2_w_down=m_ffn2_w_down,
                     final_norm=m_final_norm)
    moments_v = dict(ffn1_norm=v_ffn1_norm, ffn1_w_gate=v_ffn1_w_gate, ffn1_w_up=v_ffn1_w_up, ffn1_w_down=v_ffn1_w_down,
                     mix_norm=v_mix_norm, w_in=v_w_in, conv_w=v_conv_w, attn_sinks=v_attn_sinks, w_out=v_w_out,
                     ffn2_norm=v_ffn2_norm, ffn2_w_gate=v_ffn2_w_gate, ffn2_w_up=v_ffn2_w_up, ffn2_w_down=v_ffn2_w_down,
                     final_norm=v_final_norm)

    xs = x[0]
    target = loss_target[0]
    final_gain = final_norm.reshape(1, D_MODEL)

    def ffn_shards(wg, wu, wd):
        return jnp.stack([wg[0].T, wu[0].T]).astype(BF16), wd[0].astype(BF16)

    conv_cols = conv_w.shape[2]
    conv_shard = jnp.pad(conv_w[0], ((0, 5), (0, 128 - conv_cols)))
    gate_up1, down1 = ffn_shards(ffn1_w_gate, ffn1_w_up, ffn1_w_down)
    rest_shards = [down1, w_in[0].T.astype(BF16), w_out[0].astype(BF16), conv_shard,
                   *ffn_shards(ffn2_w_gate, ffn2_w_up, ffn2_w_down)]
    rest_rows = [s.shape[-2] for s in rest_shards]
    n_rest, n_early = len(rest_shards), 4
    (w1_gu,), _ = _all_gather_rows([gate_up1], name="gather_ffn1")

    fulls = [lax.empty(s.shape[:-2] + (N_DEV * s.shape[-2], s.shape[-1]), s.dtype) for s in rest_shards]
    fulls = _place_own(fulls, rest_shards, my_index.astype(jnp.int32).reshape(1), name="place_own_weights")
    ssem, rsem, bufs, token = _split_start(rest_shards + list(fulls), 4 * n_rest, _gather_send_plan(n_rest),
                                           name="gather_rest_start", after=[w1_gu])
    early = bufs[:n_early] + bufs[n_rest:n_rest + n_early]
    late = bufs[n_early:n_rest] + bufs[n_rest + n_early:]
    h1, s1, sa1, sb1 = _ffn_fwd(xs, ffn1_norm, w1_gu, name="ffn1_hidden", after=[token])
    early = _split_wait(ssem, rsem, early, h1, _gather_send_plan(n_early), name="gather_early_wait")
    fwd_early = _gather_forward_plan(rest_rows[:n_early])
    ssem_e, rsem_e, parts, token = _split_start(early[n_early:], 3 * n_early, fwd_early, name="forward_early_start")
    rope = _rope_tables(xs.shape[0], after=[token])
    w1_d, win_t, wout, conv_all = _split_wait(ssem_e, rsem_e, parts, rope, fwd_early, name="forward_early_wait")
    x1 = _ffn_down(xs, s1, w1_d, name="ffn1_down")
    conv_full = conv_all.reshape(N_DEV, 8, 128)[:, :3, :conv_cols].transpose(1, 0, 2).reshape(3, CONV_W)
    late = _split_wait(ssem, rsem, late, x1, _gather_send_plan(n_rest - n_early, first=n_early),
                       name="gather_late_wait")
    fwd_ffn2 = _gather_forward_plan(rest_rows[n_early:])
    ssem, rsem, parts, token = _split_start(late[n_rest - n_early:], 3 * (n_rest - n_early), fwd_ffn2,
                                            name="forward_ffn2_start")
    x2, hm, z, y = _mixer_fwd(x1, mix_norm, win_t, wout, conv_full, attn_sinks, rope, name="mixer_fwd", after=[token])
    w2_gu, w2_d = _split_wait(ssem, rsem, parts, x2, fwd_ffn2, name="forward_ffn2_wait")
    dx3, h2, s2, sa2, sb2, loss_local, d_final = _ffn_fwd(x2, ffn2_norm, w2_gu, w2_d, head=(final_gain, target),
                                                          name="ffn2_fwd")

    def sibling_bufs(grads):
        views = [g.reshape(4, 2, g.shape[0] // N_DEV, D_MODEL) for g in grads]
        return views + [lax.empty((4,) + v.shape[2:], F32) for v in views]

    def chips_bufs(partials, small_all):
        p16 = [p for _, p in partials]
        return p16 + [lax.empty((3,) + p.shape[1:], BF16) for p in p16] + ([] if small_all is None else [small_all])

    def to_sibling_start(grads, tag, after=()):
        plan = _sibling_plan(len(grads))
        ssem, rsem, bufs, token = _split_start(sibling_bufs(grads), len(grads), plan, name=f"{tag}_sibling_start",
                                               after=after)
        return (ssem, rsem, bufs, plan, tag), token

    def to_chips_and_sibling_start(partials, tag, grads, sibling_tag, after=()):
        for_chips, n_ici = chips_bufs(partials, None), 3 * len(partials)
        chips_plan, sibling_plan = _chips_plan(len(partials), False), _sibling_plan(len(grads), first=n_ici)

        def both(bufs, send_sems, recv_sems):
            return (chips_plan(bufs[:len(for_chips)], send_sems, recv_sems)
                    + sibling_plan(bufs[len(for_chips):], send_sems, recv_sems))

        ssem, rsem, bufs, token = _split_start(for_chips + sibling_bufs(grads), n_ici + len(grads), both,
                                               name=f"{tag}_chips_{sibling_tag}_sibling_start", after=after)
        return ((ssem, rsem, bufs[:len(for_chips)], chips_plan, tag),
                (ssem, rsem, bufs[len(for_chips):], sibling_plan, sibling_tag), token)

    def to_sibling_finish(handle, after, names):
        ssem, rsem, bufs, plan, tag = handle
        bufs = _split_wait(ssem, rsem, bufs, after, plan, name=f"{tag}_sibling_wait")
        n = len(names)
        return _add_sibling(bufs[:n], bufs[n:], place, name=f"add_sibling_{tag}")

    def to_chips_start(partials, tag, small_all=None, after=()):
        plan = _chips_plan(len(partials), small_all is not None)
        n_copies = 3 * len(partials) + (0 if small_all is None else N_DEV - 1)
        ssem, rsem, bufs, token = _split_start(chips_bufs(partials, small_all), n_copies, plan,
                                               name=f"{tag}_chips_start", after=after)
        return (ssem, rsem, bufs, plan, tag), token

    def to_chips_finish(handle, partials, after, names):
        ssem, rsem, bufs, plan, tag = handle
        bufs = _split_wait(ssem, rsem, bufs, after, plan, name=f"{tag}_chips_wait")
        n = len(names)
        return [(p32, r) for (p32, _), r in zip(partials, bufs[n:2 * n])], bufs[2 * n:]

    half_ff = D_FF // 2
    names2, namesm = ["ffn2_w_gate", "ffn2_w_up", "ffn2_w_down"], ["w_in", "w_out"]
    transposed = {"ffn1_w_gate", "ffn1_w_up", "w_in", "ffn2_w_gate", "ffn2_w_up"}
    grad, delta, new_m, new_v = {}, {}, {}, {}

    def adam_big(names, parts, tag, after=()):
        def to_rows(nm, a):
            return jnp.swapaxes(a, 1, 2)[0] if nm in transposed else a[0]

        def from_rows(nm, a):
            return jnp.swapaxes(a[None], 1, 2) if nm in transposed else a[None]

        operands = [(p32, recv, to_rows(nm, given[nm]), to_rows(nm, moments_m[nm]), to_rows(nm, moments_v[nm]))
                    for nm, (p32, recv) in zip(names, parts)]
        results, token = _reduce_adamw(operands, name=f"adamw_{tag}", after=after)
        for nm, outs in zip(names, results):
            grad[nm], delta[nm], new_m[nm], new_v[nm] = (from_rows(nm, a) for a in outs)
        return token

    dx2, da2, db2, g2b, d_norm2 = _ffn_dgrad(dx3, x2, ffn2_norm, sa2, sb2, w2_gu, w2_d, name="ffn2_dgrad")
    gw2 = [_tn_matmul(da2, h2, name="ffn2_wgrad_gate", bm=half_ff), _tn_matmul(db2, h2, name="ffn2_wgrad_up", bm=half_ff),
           _tn_matmul(s2, g2b, name="ffn2_wgrad_down", bm=half_ff)]
    sib2, tok = to_sibling_start(gw2, "ffn2")
    dx1, dz, gmb, d_conv, d_sink, d_normm = _mixer_bwd(dx2, x1, mix_norm, y, z, win_t, wout, conv_full, attn_sinks,
                                                       rope, name="mixer_bwd", after=[tok])
    gwm = [_tn_matmul(dz, hm, name="mixer_wgrad_in", bm=Z_W // 2), _tn_matmul(y, gmb, name="mixer_wgrad_out", bm=D_MODEL)]
    p2 = to_sibling_finish(sib2, gwm[1], names2)
    chips2, sibm, tok = to_chips_and_sibling_start(p2, "ffn2", gwm, "mixer")
    dx0, da1, db1, g1b, d_norm1 = _ffn_dgrad(dx1, xs, ffn1_norm, sa1, sb1, w1_gu, w1_d, name="ffn1_dgrad", after=[tok])
    r2, _ = to_chips_finish(chips2, p2, dx0, names2)
    pm = to_sibling_finish(sibm, dx0, namesm)
    gw_gate = _tn_matmul(da1, h1, name="ffn1_wgrad_gate", bm=half_ff)
    chipsm, sib_gate, tok = to_chips_and_sibling_start(pm, "mixer", [gw_gate], "ffn1_gate")
    gw_up = _tn_matmul(db1, h1, name="ffn1_wgrad_up", bm=half_ff, after=[tok])
    rm, _ = to_chips_finish(chipsm, pm, gw_up, namesm)
    p_gate = to_sibling_finish(sib_gate, gw_up, ["ffn1_w_gate"])
    chips_gate, sib_up, tok = to_chips_and_sibling_start(p_gate, "ffn1_gate", [gw_up], "ffn1_up")
    gw_down = _tn_matmul(s1, g1b, name="ffn1_wgrad_down_first", bm=half_ff, blocks=(0, 1), after=[tok])
    p_up = to_sibling_finish(sib_up, gw_down, ["ffn1_w_up"])
    chips_up, tok_a = to_chips_start(p_up, "ffn1_up")
    gw_down = _tn_matmul(s1, g1b, name="ffn1_wgrad_down_second", bm=half_ff, blocks=(1, 1), into=gw_down, after=[tok_a])
    sib_down, tok_b = to_sibling_start([gw_down], "ffn1_down")
    p_down = to_sibling_finish(sib_down, tok_b, ["ffn1_w_down"])
    last_row = (jnp.pad(d_sink, ((0, 0), (0, D_MODEL - 128)))
                + jnp.pad(loss_local, ((0, 0), (LOSS_LANE, D_MODEL - LOSS_LANE - 1))))
    small = jnp.concatenate([
        d_norm1, d_normm, d_norm2, d_final, jnp.pad(d_conv[0:3], ((0, 0), (0, D_MODEL - CONV_W))), last_row], axis=0)
    (small_all,) = _place_own([lax.empty((N_DEV * 8, D_MODEL), F32)], [small], my_index.astype(jnp.int32).reshape(1),
                              name="place_own_small")
    chips_down, tok = to_chips_start(p_down, "ffn1_down", small_all)
    tok = adam_big(names2, r2, "ffn2", after=[tok])
    tok = adam_big(namesm, rm, "mixer", after=[tok])
    r_gate, _ = to_chips_finish(chips_gate, p_gate, tok, ["ffn1_w_gate"])
    tok = adam_big(["ffn1_w_gate"], r_gate, "ffn1_gate")
    r_up, _ = to_chips_finish(chips_up, p_up, tok, ["ffn1_w_up"])
    tok = adam_big(["ffn1_w_up"], r_up, "ffn1_up")
    r_down, (small_all,) = to_chips_finish(chips_down, p_down, tok, ["ffn1_w_down"])
    adam_big(["ffn1_w_down"], r_down, "ffn1_down")
    results, loss = _update_small(given, moments_m, moments_v, small_all.reshape(N_DEV, 8, D_MODEL), my_index,
                                  name="update_small")
    for nm, outs in results.items():
        grad[nm], delta[nm], new_m[nm], new_v[nm] = outs

    order = list(given)
    return (loss.reshape(()), dx0[None], *[grad[n] for n in order], *[delta[n] for n in order],
            *[new_m[n] for n in order], *[new_v[n] for n in order])
```

```python
import functools

import jax
import jax.numpy as jnp
from jax import lax
from jax.experimental import pallas as pl
from jax.experimental.pallas import tpu as pltpu

F32 = jnp.float32
BF16 = jnp.bfloat16
MESH = pl.DeviceIdType.MESH
ANY = pl.BlockSpec(memory_space=pl.ANY)
HBM_SPEC = pl.BlockSpec(memory_space=pltpu.HBM)
SEM_SPEC = pl.BlockSpec(memory_space=pltpu.SEMAPHORE)
DATAFLOW = pltpu.SideEffectType.DATAFLOW_SIDE_EFFECTING

N_DEV = 8
LOSS_LANE = 128
D_MODEL = 1024
D_FF = 2816
CONV_W = 512
ATTN_W = 512
KV_W = 128
HEAD_DIM = 64
N_Q_HEADS = 8
N_KV_HEADS = 2
Q_PER_KV = N_Q_HEADS // N_KV_HEADS
BLOCK = 128
ROT_DIM = 16
ROPE_THETA = 500000.0
Z_W = 3 * CONV_W + ATTN_W + 2 * KV_W
Q_OFF = 3 * CONV_W
K_OFF = Q_OFF + ATTN_W
V_OFF = K_OFF + KV_W
RMS_EPS = 1e-5
MASK_VALUE = -1e30
SM_SCALE = HEAD_DIM ** -0.5
FFN_RES_SCALE = 0.5

ADAM_LR = 0.001
ADAM_B1 = 0.9
ADAM_B2 = 0.999
ADAM_EPS = 1e-08
ADAM_WD = 0.01
ADAM_STEP = 10

NT_DIMS = (((1,), (1,)), ((), ()))
TN_DIMS = (((0,), (0,)), ((), ()))

VMEM_LIMIT = 62 * 1024 * 1024
FF_CHUNK = 256


def _params(sem, vmem=None):
    return pltpu.CompilerParams(dimension_semantics=sem, vmem_limit_bytes=vmem)


def _behind(body, n_in, after):
    k = len(after)
    if k == 0:
        return body
    return lambda *refs: body(*refs[:n_in], *refs[n_in + k:])


def _rms_stats(xf):
    inv = lax.rsqrt(jnp.mean(xf * xf, axis=-1, keepdims=True) + RMS_EPS)
    return xf * inv, inv


def _rms_bwd(dh, xhat, inv, gain):
    dxhat = dh * gain
    dx = inv * (dxhat - xhat * jnp.mean(dxhat * xhat, axis=-1, keepdims=True))
    dgain = jnp.sum(dh * xhat, axis=0, keepdims=True)
    return dx, dgain


LOAD_PIECES = 4


def _load_resident(pairs, sems):
    @pl.when(pl.program_id(0) == 0)
    def _():
        copies = []
        for k, (w_hbm, w_ref) in enumerate(pairs):
            rows = w_hbm.shape[-2] // LOAD_PIECES
            for p in range(LOAD_PIECES):
                piece = (slice(None),) * (len(w_hbm.shape) - 2) + (pl.ds(p * rows, rows), slice(None))
                copies.append(pltpu.make_async_copy(w_hbm.at[piece], w_ref.at[piece], sems.at[k, p]))
        for cp in copies:
            cp.start()
        for cp in copies:
            cp.wait()


def _ffn_fwd(x, gain, w_gu, w_d=None, *, name, head=None, after=(), tm=256, tf=FF_CHUNK):
    t = x.shape[0]
    tm = min(tm, t)
    n_tiles = t // tm
    assert (w_d is None) == (head is None)
    n_in = 3 if head is None else 6
    first = 0 if head is None else 1
    half = D_FF // tf // 2 * tf

    def body(*refs):
        x_ref, g_ref, wgu_hbm = refs[:3]
        h_ref, s_ref, sa_ref, sb_ref = refs[n_in + first:n_in + first + 4]
        scratch = refs[n_in + 3 * first + 4:]
        wgu_ref, sems = scratch[0], scratch[-1]
        i = pl.program_id(0)

        def norm():
            h = (_rms_stats(x_ref[...])[0] * g_ref[...]).astype(BF16)
            h_ref[...] = h
            return h

        def hidden(h, lo, hi):
            for c in range(lo, hi, tf):
                cols = slice(c, min(c + tf, D_FF))
                a = lax.dot_general(h, wgu_ref[0, cols, :], NT_DIMS, preferred_element_type=F32)
                b = lax.dot_general(h, wgu_ref[1, cols, :], NT_DIMS, preferred_element_type=F32)
                sig = jax.nn.sigmoid(a)
                silu = a * sig
                s_ref[:, cols] = (silu * b).astype(BF16)
                sa_ref[:, cols] = (b * (sig * (1.0 + a * (1.0 - sig)))).astype(BF16)
                sb_ref[:, cols] = silu.astype(BF16)

        if head is None:
            _load_resident([(wgu_hbm, wgu_ref)], sems)
            hidden(norm(), 0, D_FF)
            return

        wd_hbm, fg_ref, t_ref = refs[3:n_in]
        dxo_ref, loss_ref, dfg_ref = refs[n_in], refs[n_in + 5], refs[n_in + 6]
        wd_ref, xo_ref = scratch[1], scratch[2]
        _load_resident([(wgu_hbm, wgu_ref), (wd_hbm, wd_ref)], sems)

        @pl.when(i == 0)
        def _():
            loss_ref[...] = jnp.zeros_like(loss_ref)
            dfg_ref[...] = jnp.zeros_like(dfg_ref)
            xo_ref[...] = jnp.zeros_like(xo_ref)

        def head_of_previous_tile():
            xhat_o, inv_o = _rms_stats(xo_ref[...])
            err = xhat_o * fg_ref[...] - t_ref[...]
            loss = 0.5 * jnp.sum(jnp.mean(err * err, axis=-1, keepdims=True), axis=0, keepdims=True)
            dxo, dfg = _rms_bwd(err * (1.0 / D_MODEL), xhat_o, inv_o, fg_ref[...])
            dxo_ref[...] = dxo
            loss_ref[...] += jnp.where(i > 0, loss, 0.0)
            dfg_ref[...] += jnp.where(i > 0, dfg, 0.0)
            return dxo

        @pl.when(i < n_tiles)
        def _():
            h = norm()
            hidden(h, 0, half)
            dxo = head_of_previous_tile()
            zero = lax.shift_right_logical(lax.shift_right_logical(
                lax.bitcast_convert_type(dxo[0:tm // 2, :], jnp.uint32), jnp.uint32(16)), jnp.uint32(16))
            h = pltpu.bitcast(pltpu.bitcast(h, jnp.uint32) + zero, BF16)
            hidden(h, half, D_FF)
            xo_ref[...] = x_ref[...] + FFN_RES_SCALE * jnp.dot(s_ref[...], wd_ref[...], preferred_element_type=F32)

        @pl.when(i == n_tiles)
        def _():
            head_of_previous_tile()

    this = lambda i: (jnp.minimum(i, n_tiles - 1), 0)
    previous = lambda i: (jnp.maximum(i - 1, 0), 0)
    row = pl.BlockSpec((tm, D_MODEL), this)
    hid = pl.BlockSpec((tm, D_FF), this)
    vec = pl.BlockSpec((1, D_MODEL), lambda i: (0, 0))
    late_row = pl.BlockSpec((tm, D_MODEL), previous)
    hidden_shapes = [jax.ShapeDtypeStruct((t, D_MODEL), BF16)] + [jax.ShapeDtypeStruct((t, D_FF), BF16)] * 3
    if head is None:
        extra_in, extra_out, extra_shapes, extra_scratch, steps = [], [], [], [], n_tiles
    else:
        extra_in = [ANY, vec, late_row]
        extra_out, extra_shapes = [late_row], [jax.ShapeDtypeStruct((t, D_MODEL), F32)]
        extra_scratch = [pltpu.VMEM((D_FF, D_MODEL), BF16), pltpu.VMEM((tm, D_MODEL), F32)]
        steps = n_tiles + 1
    tail_out = [] if head is None else [pl.BlockSpec((1, 1), lambda i: (0, 0)), vec]
    tail_shapes = [] if head is None else [jax.ShapeDtypeStruct((1, 1), F32), jax.ShapeDtypeStruct((1, D_MODEL), F32)]
    return pl.pallas_call(
        _behind(body, n_in, after), name=name, grid=(steps,),
        in_specs=[row, vec, ANY] + extra_in + [ANY] * len(after),
        out_specs=extra_out + [row, hid, hid, hid] + tail_out,
        out_shape=extra_shapes + hidden_shapes + tail_shapes,
        scratch_shapes=[pltpu.VMEM((2, D_FF, D_MODEL), BF16)] + extra_scratch
        + [pltpu.SemaphoreType.DMA((2, LOAD_PIECES))],
        compiler_params=_params(("arbitrary",), VMEM_LIMIT),
    )(x, gain, w_gu, *([] if head is None else [w_d, *head]), *after)


def _ffn_down(x, s, w_d, *, name, after=(), tm=512):
    t = x.shape[0]
    tm = min(tm, t)

    def body(x_ref, s_ref, w_hbm, xo_ref, w_ref, sem):
        _load_resident([(w_hbm, w_ref)], sem)
        xo_ref[...] = x_ref[...] + FFN_RES_SCALE * jnp.dot(s_ref[...], w_ref[...], preferred_element_type=F32)

    row = pl.BlockSpec((tm, D_MODEL), lambda i: (i, 0))
    return pl.pallas_call(
        _behind(body, 3, after), name=name, grid=(t // tm,),
        in_specs=[row, pl.BlockSpec((tm, D_FF), lambda i: (i, 0)), ANY] + [ANY] * len(after), out_specs=row,
        out_shape=jax.ShapeDtypeStruct((t, D_MODEL), F32),
        scratch_shapes=[pltpu.VMEM((D_FF, D_MODEL), BF16), pltpu.SemaphoreType.DMA((1, LOAD_PIECES))],
        compiler_params=_params(("arbitrary",), VMEM_LIMIT),
    )(x, s, w_d, *after)


def _ffn_dgrad(dxo, x, gain, sa, sb, w_gu, w_d, *, name, after=(), tm=512, sub=512, tf=FF_CHUNK):
    t = x.shape[0]
    tm = min(tm, t)
    sub = min(sub, tm)

    def body(dxo_ref, x_ref, g_ref, sa_ref, sb_ref, wgu_hbm, wd_hbm, dxi_ref, da_ref, db_ref, gb_ref, dg_ref,
             wgu_ref, wd_ref, sems):
        _load_resident([(wd_hbm, wd_ref), (wgu_hbm, wgu_ref)], sems)

        @pl.when(pl.program_id(0) == 0)
        def _():
            dg_ref[...] = jnp.zeros_like(dg_ref)

        for r0 in range(0, tm, sub):
            rows = slice(r0, r0 + sub)
            go = dxo_ref[rows, :]
            gb = (FFN_RES_SCALE * go).astype(BF16)
            gb_ref[rows, :] = gb
            for c in range(0, D_FF, tf):
                cols = slice(c, min(c + tf, D_FF))
                ds = lax.dot_general(gb, wd_ref[cols, :], NT_DIMS, preferred_element_type=F32)
                da_ref[rows, cols] = (ds * sa_ref[rows, cols].astype(F32)).astype(BF16)
                db_ref[rows, cols] = (ds * sb_ref[rows, cols].astype(F32)).astype(BF16)
            dh = (jnp.dot(da_ref[rows, :], wgu_ref[0], preferred_element_type=F32)
                  + jnp.dot(db_ref[rows, :], wgu_ref[1], preferred_element_type=F32))
            xhat, inv = _rms_stats(x_ref[rows, :])
            dx, dgain = _rms_bwd(dh, xhat, inv, g_ref[...])
            dxi_ref[rows, :] = go + dx
            dg_ref[...] += dgain

    row = pl.BlockSpec((tm, D_MODEL), lambda i: (i, 0))
    hid = pl.BlockSpec((tm, D_FF), lambda i: (i, 0))
    vec = pl.BlockSpec((1, D_MODEL), lambda i: (0, 0))
    return pl.pallas_call(
        _behind(body, 7, after), name=name, grid=(t // tm,),
        in_specs=[row, row, vec, hid, hid, ANY, ANY] + [ANY] * len(after),
        out_specs=[row, hid, hid, row, vec],
        out_shape=[jax.ShapeDtypeStruct((t, D_MODEL), F32), jax.ShapeDtypeStruct((t, D_FF), BF16),
                   jax.ShapeDtypeStruct((t, D_FF), BF16),
                   jax.ShapeDtypeStruct((t, D_MODEL), BF16), jax.ShapeDtypeStruct((1, D_MODEL), F32)],
        scratch_shapes=[pltpu.VMEM((2, D_FF, D_MODEL), BF16), pltpu.VMEM((D_FF, D_MODEL), BF16),
                        pltpu.SemaphoreType.DMA((2, LOAD_PIECES))],
        compiler_params=_params(("arbitrary",), VMEM_LIMIT),
    )(dxo, x, gain, sa, sb, w_gu, w_d, *after)


def _tn_matmul(a, b, *, name, bm, after=(), tk=2048, blocks=None, into=None):
    t, m = a.shape
    n = b.shape[1]
    tk = min(tk, t)
    nk = t // tk
    first, count = blocks or (0, m // bm)
    behind = ([] if into is None else [into]) + list(after)

    def body(a_ref, b_ref, o_ref):
        @pl.when(pl.program_id(1) == 0)
        def _():
            o_ref[...] = jnp.zeros_like(o_ref)

        o_ref[...] += lax.dot_general(a_ref[...], b_ref[...], TN_DIMS, preferred_element_type=F32)

    return pl.pallas_call(
        _behind(body, 2, behind), name=name, grid=(count, nk),
        in_specs=[pl.BlockSpec((tk, bm), lambda i, k: (k, first + i)), pl.BlockSpec((tk, n), lambda i, k: (k, 0))]
        + [ANY] * len(behind),
        out_specs=pl.BlockSpec((bm, n), lambda i, k: (first + i, 0)),
        out_shape=jax.ShapeDtypeStruct((m, n), F32),
        input_output_aliases={} if into is None else {2: 0},
        compiler_params=_params(("parallel", "arbitrary"), VMEM_LIMIT),
    )(a, b, *behind)


def _rope_tables(t, after=()):
    inv_freq = ROPE_THETA ** (-jnp.arange(0, ROT_DIM, 2, dtype=F32) / ROT_DIM)
    ang = inv_freq[:, None] * jnp.arange(t, dtype=F32)[None, :]
    compact = jnp.stack([jnp.cos(ang), jnp.sin(ang)])
    tr = min(1024, t)

    def body(c_ref, o_ref):
        for k in range(2):
            o_ref[k] = jnp.tile(c_ref[k], (128 // inv_freq.shape[0], 1)).T

    return pl.pallas_call(
        _behind(body, 1, after), name="rope_tables", grid=(t // tr,),
        in_specs=[pl.BlockSpec((2, inv_freq.shape[0], tr), lambda i: (0, 0, i))] + [ANY] * len(after),
        out_specs=pl.BlockSpec((2, tr, 128), lambda i: (0, i, 0)),
        out_shape=jax.ShapeDtypeStruct((2, t, 128), F32), compiler_params=_params(("parallel",)),
    )(compact, *after)


def _rope_multipliers(cos_sin):
    half = ROT_DIM // 2
    cos, sin = cos_sin[0], cos_sin[1]
    d = lax.broadcasted_iota(jnp.int32, cos.shape, 1) & (HEAD_DIM - 1)
    mult = jnp.where(d < ROT_DIM, cos, 1.0)
    from_lo = jnp.where((d >= half) & (d < ROT_DIM), sin, 0.0)
    from_hi = jnp.where(d < half, -sin, 0.0)
    return mult, from_lo, from_hi


def _tile_lanes(tab, width):
    return jnp.tile(tab, (1, width // tab.shape[1]))


def _rope(v, tab):
    w = v.shape[1]
    half_rot = ROT_DIM // 2
    return (v * _tile_lanes(tab[0], w)
            + pltpu.roll(v, half_rot, axis=1) * _tile_lanes(tab[1], w)
            + pltpu.roll(v, w - half_rot, axis=1) * _tile_lanes(tab[2], w))


def _rope_bwd(dv, tab):
    w = dv.shape[1]
    half_rot = ROT_DIM // 2
    return (dv * _tile_lanes(tab[0], w)
            + pltpu.roll(dv * _tile_lanes(tab[1], w), w - half_rot, axis=1)
            + pltpu.roll(dv * _tile_lanes(tab[2], w), half_rot, axis=1))


def _shift_rows(v, prev8_ref, n):
    r = lax.broadcasted_iota(jnp.int32, v.shape, 0)
    rolled = pltpu.roll(v, n, axis=0)
    last = prev8_ref[7:8, :]
    if n == 1:
        return jnp.where(r >= 1, rolled, last)
    return jnp.where(r >= 2, rolled, jnp.where(r == 0, prev8_ref[6:7, :], last))


def _shift_rows_up(v, next8_ref, n):
    rows = v.shape[0]
    r = lax.broadcasted_iota(jnp.int32, v.shape, 0)
    rolled = pltpu.roll(v, rows - n, axis=0)
    first = next8_ref[0:1, :]
    if n == 1:
        return jnp.where(r <= rows - 2, rolled, first)
    return jnp.where(r <= rows - 3, rolled, jnp.where(r == rows - 2, first, next8_ref[1:2, :]))


def _lane_half_mask(shape, half):
    lane = lax.broadcasted_iota(jnp.int32, shape, 1)
    return (lane >= HEAD_DIM) if half else (lane < HEAD_DIM)


def _to_kv_lanes(chunk, head, kv):
    if head % 2 != kv:
        chunk = pltpu.roll(chunk, HEAD_DIM, axis=1)
    return jnp.where(_lane_half_mask(chunk.shape, kv), chunk, 0.0)


def _from_kv_lanes(chunk, head, kv):
    chunk = jnp.where(_lane_half_mask(chunk.shape, kv), chunk, 0.0)
    if head % 2 != kv:
        chunk = pltpu.roll(chunk, HEAD_DIM, axis=1)
    return chunk


def _stack_heads(wide):
    parts = []
    for head in range(N_Q_HEADS):
        chunk = wide[:, (head // 2) * 128:(head // 2 + 1) * 128]
        parts.append(_to_kv_lanes(chunk, head, head // Q_PER_KV))
    return jnp.concatenate(parts, axis=0)


def _window_mask(has_prev):
    shape = (N_Q_HEADS * BLOCK, 2 * BLOCK)
    qi = lax.broadcasted_iota(jnp.int32, shape, 0) & (BLOCK - 1)
    kj = lax.broadcasted_iota(jnp.int32, shape, 1)
    first_key = BLOCK - has_prev * BLOCK
    in_prev = (kj < BLOCK) & (kj > qi) & (kj >= first_key)
    in_own = (kj >= BLOCK) & ((kj - BLOCK) <= qi)
    return in_prev | in_own


def _sink_column(sink_ref):
    row = lax.broadcasted_iota(jnp.int32, (N_Q_HEADS * BLOCK, 1), 0)
    col = jnp.full((N_Q_HEADS * BLOCK, 1), sink_ref[0, 0], F32)
    for head in range(1, N_Q_HEADS):
        col = jnp.where(row >= head * BLOCK, sink_ref[0, head], col)
    return col


def _softmax_with_sink(q4, k2, mask, sink):
    s = lax.dot_general(q4, k2, NT_DIMS, preferred_element_type=F32) * SM_SCALE
    s = jnp.where(mask, s, MASK_VALUE)
    m = jnp.maximum(jnp.max(s, axis=-1, keepdims=True), sink)
    p = jnp.exp(s - m)
    e_sink = jnp.exp(sink - m)
    inv_den = 1.0 / (jnp.sum(p, axis=-1, keepdims=True) + e_sink)
    return p * inv_den, e_sink * inv_den


def _conv_terms(zf, prev8_ref, w_ref):
    b_gate, c_gate, u = zf[:, 0:CONV_W], zf[:, CONV_W:2 * CONV_W], zf[:, 2 * CONV_W:3 * CONV_W]
    vc = c_gate * u
    vm1 = _shift_rows(vc, prev8_ref, 1)
    vm2 = _shift_rows(vc, prev8_ref, 2)
    conv = w_ref[0:1, :] * vm2 + w_ref[1:2, :] * vm1 + w_ref[2:3, :] * vc
    return b_gate, c_gate, u, vc, vm1, vm2, conv


def _mixer_fwd(x, gain, win_t, wout, conv_w, sinks, rope, *, name, after=(), tq=512):
    t = x.shape[0]
    tq = min(tq, t)
    nblk = tq // BLOCK

    def body(x_ref, g_ref, win_hbm, wout_hbm, cw_ref, sink_ref, rope_ref,
             xo_ref, h_ref, z_ref, y_ref, kprev_ref, vprev_ref, cprev_ref, win_ref, wout_ref, sems):
        i = pl.program_id(0)
        _load_resident([(win_hbm, win_ref), (wout_hbm, wout_ref)], sems)

        @pl.when(i == 0)
        def _():
            kprev_ref[...] = jnp.zeros_like(kprev_ref)
            vprev_ref[...] = jnp.zeros_like(vprev_ref)
            cprev_ref[...] = jnp.zeros_like(cprev_ref)

        xf = x_ref[...]
        xhat, _ = _rms_stats(xf)
        h = (xhat * g_ref[...]).astype(BF16)
        h_ref[...] = h

        def project(c0, c1):
            zc = lax.dot_general(h, win_ref[c0:c1, :], NT_DIMS, preferred_element_type=F32).astype(BF16)
            z_ref[:, c0:c1] = zc
            return zc

        zb = project(Q_OFF, Z_W)
        zf = zb.astype(F32)
        tab = _rope_multipliers(rope_ref[...])
        qr = _rope(zf[:, 0:ATTN_W], tab)
        kr = _rope(zf[:, K_OFF - Q_OFF:V_OFF - Q_OFF], tab).astype(BF16)
        vb = zb[:, V_OFF - Q_OFF:Z_W - Q_OFF]
        conv_cols = [(c, c + CONV_W) for c in range(0, Q_OFF, CONV_W)]
        conv_z = []

        y_attn = []
        for j in range(nblk):
            if len(conv_z) < len(conv_cols):
                conv_z.append(project(*conv_cols[len(conv_z)]))
            rows = slice(j * BLOCK, (j + 1) * BLOCK)
            prev = slice((j - 1) * BLOCK, j * BLOCK)
            k2 = jnp.concatenate([kprev_ref[...] if j == 0 else kr[prev], kr[rows]], axis=0)
            v2 = jnp.concatenate([vprev_ref[...] if j == 0 else vb[prev], vb[rows]], axis=0)
            mask = _window_mask(jnp.minimum(i, 1) if j == 0 else 1)
            q8 = _stack_heads(qr[rows]).astype(BF16)
            probs, _ = _softmax_with_sink(q8, k2, mask, _sink_column(sink_ref))
            o8 = jnp.dot(probs.astype(BF16), v2, preferred_element_type=F32)
            chunks = [jnp.zeros((BLOCK, 128), F32) for _ in range(ATTN_W // 128)]
            for head in range(N_Q_HEADS):
                chunks[head // 2] += _from_kv_lanes(o8[head * BLOCK:(head + 1) * BLOCK], head, head // Q_PER_KV)
            y_attn.append(jnp.concatenate(chunks, axis=1))
        kprev_ref[...] = kr[tq - BLOCK:tq]
        vprev_ref[...] = vb[tq - BLOCK:tq]
        while len(conv_z) < len(conv_cols):
            conv_z.append(project(*conv_cols[len(conv_z)]))
        ya = jnp.concatenate(y_attn, axis=0).astype(BF16)
        y_ref[:, CONV_W:] = ya
        xo = xf + jnp.dot(ya, wout_ref[CONV_W:, :], preferred_element_type=F32)
        b_gate, _, _, vc, _, _, conv = _conv_terms(jnp.concatenate(conv_z, axis=1).astype(F32), cprev_ref, cw_ref)
        yc = (b_gate * conv).astype(BF16)
        cprev_ref[...] = vc[tq - 8:tq, :]
        y_ref[:, :CONV_W] = yc
        xo_ref[...] = xo + jnp.dot(yc, wout_ref[:CONV_W, :], preferred_element_type=F32)

    row = pl.BlockSpec((tq, D_MODEL), lambda i: (i, 0))
    full = lambda shape: pl.BlockSpec(shape, lambda i: (0,) * len(shape))
    return pl.pallas_call(
        _behind(body, 7, after), name=name, grid=(t // tq,),
        in_specs=[row, full((1, D_MODEL)), ANY, ANY, full((3, CONV_W)),
                  pl.BlockSpec(memory_space=pltpu.SMEM), pl.BlockSpec((2, tq, 128), lambda i: (0, i, 0))]
        + [ANY] * len(after),
        out_specs=[row, row, pl.BlockSpec((tq, Z_W), lambda i: (i, 0)), row],
        out_shape=[jax.ShapeDtypeStruct((t, D_MODEL), F32), jax.ShapeDtypeStruct((t, D_MODEL), BF16),
                   jax.ShapeDtypeStruct((t, Z_W), BF16), jax.ShapeDtypeStruct((t, D_MODEL), BF16)],
        scratch_shapes=[pltpu.VMEM((BLOCK, KV_W), BF16), pltpu.VMEM((BLOCK, KV_W), BF16),
                        pltpu.VMEM((8, CONV_W), F32), pltpu.VMEM((Z_W, D_MODEL), BF16),
                        pltpu.VMEM((D_MODEL, D_MODEL), BF16), pltpu.SemaphoreType.DMA((2, LOAD_PIECES))],
        compiler_params=_params(("arbitrary",), VMEM_LIMIT),
    )(x, gain, win_t, wout, conv_w, sinks, rope, *after)


def _mixer_bwd(dxo, x, gain, y, z, win_t, wout, conv_w, sinks, rope, *, name, after=(), tq=256):
    t = x.shape[0]
    tq = min(tq, t)
    nt, nblk = t // tq, tq // BLOCK

    def body(dxo_ref, x_ref, g_ref, y_ref, z_ref, zp_ref, win_hbm, wout_hbm, cw_ref, sink_ref, rope_ref, ropep_ref,
             dxi_ref, dz_ref, gb_ref, dcw_ref, dsink_ref, dg_ref, dk_ref, dv_ref, dcn_ref, pvc_ref,
             win_ref, wout_ref, sems):
        i = pl.program_id(0)
        tile = nt - 1 - i
        _load_resident([(win_hbm, win_ref), (wout_hbm, wout_ref)], sems)

        @pl.when(i == 0)
        def _():
            dk_ref[...] = jnp.zeros_like(dk_ref)
            dv_ref[...] = jnp.zeros_like(dv_ref)
            dcn_ref[...] = jnp.zeros_like(dcn_ref)
            dcw_ref[...] = jnp.zeros_like(dcw_ref)
            dsink_ref[...] = jnp.zeros_like(dsink_ref)
            dg_ref[...] = jnp.zeros_like(dg_ref)

        has_prev = jnp.minimum(tile, 1)
        go = dxo_ref[...]
        gb = go.astype(BF16)
        gb_ref[...] = gb
        dy = lax.dot_general(gb, wout_ref[...], NT_DIMS, preferred_element_type=F32)
        dy_conv, dy_attn = dy[:, 0:CONV_W], dy[:, CONV_W:D_MODEL]
        zb, zpb = z_ref[...], zp_ref[...]
        zf = zb.astype(F32)
        zpf = zpb.astype(F32) * has_prev.astype(F32)

        pvc_ref[...] = (zpf[:, CONV_W:2 * CONV_W] * zpf[:, 2 * CONV_W:3 * CONV_W])[BLOCK - 8:BLOCK, :]
        b_gate, c_gate, u, vc, vm1, vm2, conv = _conv_terms(zf, pvc_ref, cw_ref)
        d_bgate = dy_conv * conv
        dc = dy_conv * b_gate
        tap = lax.broadcasted_iota(jnp.int32, (8, CONV_W), 0)
        dcw_ref[...] += jnp.where(tap == 0, jnp.sum(dc * vm2, axis=0, keepdims=True),
                                  jnp.where(tap == 1, jnp.sum(dc * vm1, axis=0, keepdims=True),
                                            jnp.where(tap == 2, jnp.sum(dc * vc, axis=0, keepdims=True), 0.0)))
        dvc = (cw_ref[2:3, :] * dc + cw_ref[1:2, :] * _shift_rows_up(dc, dcn_ref, 1)
               + cw_ref[0:1, :] * _shift_rows_up(dc, dcn_ref, 2))
        dcn_ref[...] = dc[0:8, :]
        d_cgate = dvc * u
        d_u = dvc * c_gate

        tab, tabp = _rope_multipliers(rope_ref[...]), _rope_multipliers(ropep_ref[...])
        qr = _rope(zf[:, Q_OFF:K_OFF], tab)
        kr = _rope(zf[:, K_OFF:V_OFF], tab).astype(BF16)
        kpr = _rope(zpf[:, K_OFF:V_OFF], tabp).astype(BF16)
        vb, vpb = zb[:, V_OFF:Z_W], zpb[:, V_OFF:Z_W]
        out = y_ref[:, CONV_W:D_MODEL].astype(F32)
        do_out = dy_attn * out
        lane = lax.broadcasted_iota(jnp.int32, (1, 128), 1)
        dsink = jnp.zeros((1, 128), F32)
        dk_next, dv_next = dk_ref[...], dv_ref[...]
        dq_rows, dk_rows, dv_rows = [None] * nblk, [None] * nblk, [None] * nblk
        for j in reversed(range(nblk)):
            rows = slice(j * BLOCK, (j + 1) * BLOCK)
            prev = slice((j - 1) * BLOCK, j * BLOCK)
            k2 = jnp.concatenate([kpr if j == 0 else kr[prev], kr[rows]], axis=0)
            v2 = jnp.concatenate([vpb if j == 0 else vb[prev], vb[rows]], axis=0)
            mask = _window_mask(has_prev if j == 0 else 1)
            q8 = _stack_heads(qr[rows]).astype(BF16)
            do8 = _stack_heads(dy_attn[rows]).astype(BF16)
            delta = jnp.sum(_stack_heads(do_out[rows]), axis=-1, keepdims=True)
            probs, p_sink = _softmax_with_sink(q8, k2, mask, _sink_column(sink_ref))
            dp = lax.dot_general(do8, v2, NT_DIMS, preferred_element_type=F32)
            ds = (probs * (dp - delta) * SM_SCALE).astype(BF16)
            dq8 = jnp.dot(ds, k2, preferred_element_type=F32)
            dk2 = lax.dot_general(ds, q8, TN_DIMS, preferred_element_type=F32)
            dv2 = lax.dot_general(probs.astype(BF16), do8, TN_DIMS, preferred_element_type=F32)
            sink_terms = p_sink * delta
            dq_chunks = [jnp.zeros((BLOCK, 128), F32) for _ in range(ATTN_W // 128)]
            for head in range(N_Q_HEADS):
                grp = slice(head * BLOCK, (head + 1) * BLOCK)
                dq_chunks[head // 2] += _from_kv_lanes(dq8[grp], head, head // Q_PER_KV)
                dsink = dsink - jnp.where(lane == head, jnp.sum(sink_terms[grp], axis=0, keepdims=True), 0.0)
            dq_rows[j] = jnp.concatenate(dq_chunks, axis=1)
            dk_rows[j] = dk2[BLOCK:] + dk_next
            dv_rows[j] = dv2[BLOCK:] + dv_next
            dk_next, dv_next = dk2[:BLOCK], dv2[:BLOCK]
        dk_ref[...] = dk_next
        dv_ref[...] = dv_next
        dsink_ref[...] += dsink
        dq = _rope_bwd(jnp.concatenate(dq_rows, axis=0), tab)
        dk = _rope_bwd(jnp.concatenate(dk_rows, axis=0), tab)
        dv = jnp.concatenate(dv_rows, axis=0)

        dzb = jnp.concatenate([d_bgate, d_cgate, d_u, dq, dk, dv], axis=1).astype(BF16)
        dz_ref[...] = dzb
        dh = jnp.dot(dzb, win_ref[...], preferred_element_type=F32)
        xhat, inv = _rms_stats(x_ref[...])
        dx, dgain = _rms_bwd(dh, xhat, inv, g_ref[...])
        dxi_ref[...] = go + dx
        dg_ref[...] += dgain

    rev = lambda i: (nt - 1 - i, 0)
    block_before = lambda i: jnp.maximum((nt - 1 - i) * nblk - 1, 0)
    row = pl.BlockSpec((tq, D_MODEL), rev)
    full = lambda shape: pl.BlockSpec(shape, lambda i: (0,) * len(shape))
    return pl.pallas_call(
        _behind(body, 12, after), name=name, grid=(nt,),
        in_specs=[row, row, full((1, D_MODEL)), row,
                  pl.BlockSpec((tq, Z_W), rev), pl.BlockSpec((BLOCK, Z_W), lambda i: (block_before(i), 0)),
                  ANY, ANY, full((3, CONV_W)),
                  pl.BlockSpec(memory_space=pltpu.SMEM),
                  pl.BlockSpec((2, tq, 128), lambda i: (0, nt - 1 - i, 0)),
                  pl.BlockSpec((2, BLOCK, 128), lambda i: (0, block_before(i), 0))] + [ANY] * len(after),
        out_specs=[row, pl.BlockSpec((tq, Z_W), rev), row, full((8, CONV_W)), full((1, 128)), full((1, D_MODEL))],
        out_shape=[jax.ShapeDtypeStruct((t, D_MODEL), F32), jax.ShapeDtypeStruct((t, Z_W), BF16),
                   jax.ShapeDtypeStruct((t, D_MODEL), BF16), jax.ShapeDtypeStruct((8, CONV_W), F32),
                   jax.ShapeDtypeStruct((1, 128), F32), jax.ShapeDtypeStruct((1, D_MODEL), F32)],
        scratch_shapes=[pltpu.VMEM((BLOCK, KV_W), F32), pltpu.VMEM((BLOCK, KV_W), F32), pltpu.VMEM((8, CONV_W), F32),
                        pltpu.VMEM((8, CONV_W), F32), pltpu.VMEM((Z_W, D_MODEL), BF16),
                        pltpu.VMEM((D_MODEL, D_MODEL), BF16), pltpu.SemaphoreType.DMA((2, LOAD_PIECES))],
        compiler_params=_params(("arbitrary",), VMEM_LIMIT),
    )(dxo, x, gain, y, z, z, win_t, wout, conv_w, sinks, rope, rope, *after)


def _place():
    x, y, c = lax.axis_index("x"), lax.axis_index("y"), lax.axis_index("c")
    other_chips = [(1 - x, y), (x, 1 - y), (1 - x, 1 - y)]
    return x, y, c, other_chips


def _all_gather_rows(shards, place=(), *, name):
    n, p = len(shards), len(place)

    def body(*refs):
        srcs, place_srcs = refs[:n], refs[n:n + p]
        outs, place_outs = refs[n + p:2 * n + p], refs[2 * n + p:2 * (n + p)]
        send_sems, recv_sems, local_sems = refs[2 * (n + p):]
        x, y, c, _ = _place()
        me, sibling = (x, y, c), (x, y, 1 - c)
        relay_from = (x + (1 - c) - 2 * x * (1 - c), y + c - 2 * y * c)
        relay_to = (x + c - 2 * x * c, y + (1 - c) - 2 * y * (1 - c))
        chips = [relay_from, relay_to, (1 - x, 1 - y)]

        def rows(t, px, py, pc):
            r = srcs[t].shape[-2]
            start = pl.multiple_of((4 * px + 2 * py + pc) * r, 16 if r % 16 == 0 else 8)
            if len(srcs[t].shape) == 3:
                return outs[t].at[:, pl.ds(start, r), :]
            return outs[t].at[pl.ds(start, r), :]

        def copy(t, k, block, to, own=False):
            return pltpu.make_async_remote_copy(
                src_ref=srcs[t] if own else rows(t, *block), dst_ref=rows(t, *block),
                send_sem=send_sems.at[t, k], recv_sem=recv_sems.at[t, k], device_id=to, device_id_type=MESH)

        mine = [pltpu.make_async_copy(srcs[t], rows(t, *me), local_sems.at[t]) for t in range(n)]
        mine += [pltpu.make_async_copy(place_srcs[q],
                                       _block_rows(place_outs[q], place_srcs[q].shape[-2], 4 * x + 2 * y + c),
                                       local_sems.at[n + q]) for q in range(p)]
        for q in range(p):
            mine[n + q].start()
        first = []
        for t in range(n):
            mine[t].start()
            first.append(copy(t, 0, me, sibling, own=True))
            first += [copy(t, 1 + j, me, (*chip, c), own=True) for j, chip in enumerate(chips[:2])]
        for cp in first:
            cp.start()
        passed = []
        for j, chip in enumerate(chips):
            for t in range(n):
                copy(t, 1 + j, (*chip, c), me).wait_recv()
                if j == 0:
                    passed.append(copy(t, 3, (*chip, c), (*relay_to, c)))
                    passed[-1].start()
                passed.append(copy(t, 4 + j, (*chip, c), sibling))
                passed[-1].start()
        for t in range(n):
            copy(t, 0, sibling, me).wait_recv()
            for j, chip in enumerate([relay_to, relay_from, chips[2]]):
                copy(t, 4 + j, (*chip, 1 - c), me).wait_recv()
        for cp in first + passed:
            cp.wait_send()
        for cp in mine:
            cp.wait()

    out_shape = [jax.ShapeDtypeStruct(s.shape[:-2] + (N_DEV * s.shape[-2], s.shape[-1]), s.dtype)
                 for s in list(shards) + list(place)]
    res = pl.pallas_call(
        body, name=name, in_specs=[ANY] * (n + p), out_specs=[ANY] * (n + p), out_shape=out_shape,
        scratch_shapes=[pltpu.SemaphoreType.DMA((n, 7)), pltpu.SemaphoreType.DMA((n, 7)),
                        pltpu.SemaphoreType.DMA((n + p,))],
    )(*shards, *place)
    return res[:n], res[n:]


def _split_start(bufs, n_copies, plan, *, name, after=()):
    n = len(bufs)

    def body(*refs):
        token = refs[-1]
        for cp in plan(refs[:n], refs[n], refs[n + 1]):
            cp.start()
        token[...] = jnp.zeros_like(token)

    res = pl.pallas_call(
        _behind(body, n, after), name=name, in_specs=[HBM_SPEC] * n + [ANY] * len(after),
        out_specs=(SEM_SPEC, SEM_SPEC, *[HBM_SPEC] * n, pl.BlockSpec(memory_space=pltpu.VMEM)),
        out_shape=(pltpu.SemaphoreType.DMA((n_copies,)), pltpu.SemaphoreType.DMA((n_copies,)),
                   *[pltpu.HBM(b.shape, b.dtype) for b in bufs], jax.ShapeDtypeStruct((8, 128), F32)),
        input_output_aliases={i: 2 + i for i in range(n)},
        compiler_params=pltpu.CompilerParams(has_side_effects=DATAFLOW),
    )(*[pltpu.with_memory_space_constraint(b, pltpu.HBM) for b in bufs], *after)
    return res[0], res[1], list(res[2:2 + n]), res[-1]


def _split_wait(send_sems, recv_sems, bufs, after, plan, *, name):
    n = len(bufs)

    def body(*refs):
        for cp in plan(refs[:n], refs[n], refs[n + 1]):
            cp.wait_send()
            cp.wait_recv()

    return list(pl.pallas_call(
        body, name=name, in_specs=[HBM_SPEC] * n + [SEM_SPEC, SEM_SPEC, ANY], out_specs=[HBM_SPEC] * n,
        out_shape=tuple(pltpu.HBM(b.shape, b.dtype) for b in bufs),
        input_output_aliases={i: i for i in range(n)},
        compiler_params=pltpu.CompilerParams(has_side_effects=DATAFLOW),
    )(*bufs, send_sems, recv_sems, after))


def _sibling_plan(n, first=0):
    def plan(bufs, send_sems, recv_sems):
        x, y, c, _ = _place()
        return [_remote(bufs[t].at[:, 1 - c], bufs[n + t], send_sems, recv_sems, first + t, (x, y, 1 - c))
                for t in range(n)]
    return plan


def _block_rows(ref, r, blk):
    start = pl.multiple_of(blk * r, 16 if r % 16 == 0 else 8)
    return ref.at[(slice(None),) * (len(ref.shape) - 2) + (pl.ds(start, r), slice(None))]


def _remote(src, dst, send_sems, recv_sems, k, peer):
    return pltpu.make_async_remote_copy(src_ref=src, dst_ref=dst, send_sem=send_sems.at[k], recv_sem=recv_sems.at[k],
                                        device_id=peer, device_id_type=MESH)


def _gather_send_plan(n, first=0):
    def plan(bufs, send_sems, recv_sems):
        x, y, c, chips = _place()
        peers = [(x, y, 1 - c)] + [(px, py, c) for px, py in chips]
        copies = []
        for t in range(n):
            dst = _block_rows(bufs[n + t], bufs[t].shape[-2], 4 * x + 2 * y + c)
            copies += [_remote(bufs[t], dst, send_sems, recv_sems, 4 * (first + t) + k, peer)
                       for k, peer in enumerate(peers)]
        return copies
    return plan


def _gather_forward_plan(rows):
    def plan(bufs, send_sems, recv_sems):
        x, y, c, chips = _place()
        copies = []
        for t, r in enumerate(rows):
            for j, (px, py) in enumerate(chips):
                blk = _block_rows(bufs[t], r, 4 * px + 2 * py + c)
                copies.append(_remote(blk, blk, send_sems, recv_sems, 3 * t + j, (x, y, 1 - c)))
        return copies
    return plan


def _chips_plan(n, with_small):
    def plan(bufs, send_sems, recv_sems):
        x, y, c, chips = _place()
        copies = []
        for t in range(n):
            for j, (px, py) in enumerate(chips):
                copies.append(_remote(bufs[t].at[2 * px + py], bufs[n + t].at[j], send_sems, recv_sems, 3 * t + j,
                                      (px, py, c)))
        if with_small:
            mine = _block_rows(bufs[2 * n], 8, 4 * x + 2 * y + c)
            flips = [(fx, fy, fc) for fx in range(2) for fy in range(2) for fc in range(2)][1:]
            for k, (fx, fy, fc) in enumerate(flips):
                peer = (x + fx - 2 * x * fx, y + fy - 2 * y * fy, c + fc - 2 * c * fc)
                copies.append(_remote(mine, mine, send_sems, recv_sems, 3 * n + k, peer))
        return copies
    return plan


def _place_own(fulls, shards, index, *, name):
    n = len(fulls)

    def body(index_ref, *refs):
        for t in range(n):
            refs[2 * n + t][...] = refs[n + t][...]

    def block_of(shard):
        lead = len(shard.shape) - 2
        return pl.BlockSpec(shard.shape, lambda i, index_ref: (0,) * lead + (index_ref[0], 0))

    def whole(shard):
        return pl.BlockSpec(shard.shape, lambda i, index_ref: (0,) * len(shard.shape))

    return list(pl.pallas_call(
        body, name=name,
        grid_spec=pltpu.PrefetchScalarGridSpec(
            num_scalar_prefetch=1, grid=(1,),
            in_specs=[ANY] * n + [whole(s) for s in shards], out_specs=[block_of(s) for s in shards]),
        out_shape=[jax.ShapeDtypeStruct(f.shape, f.dtype) for f in fulls],
        input_output_aliases={1 + t: t for t in range(n)},
        compiler_params=_params(("arbitrary",)),
    )(index, *fulls, *shards))


N_STEPS_SMALL = 2


def _add_sibling(grads, recvs, place, *, name):
    n = len(grads)

    def body(place_ref, *refs):
        chip = place_ref[1]
        for t in range(n):
            g_ref, r_ref, own_ref, ob_ref = refs[2 * t], refs[2 * t + 1], refs[2 * n + 2 * t], refs[2 * n + 2 * t + 1]
            own = jnp.zeros(own_ref.shape, F32)
            for m in range(4):
                p = g_ref[m, 0] + r_ref[m]
                ob_ref[m] = p.astype(BF16)
                own = jnp.where(chip == m, p, own)
            own_ref[...] = own

    in_specs, out_specs, out_shape = [], [], []
    for g, r in zip(grads, recvs):
        tr = g.shape[2] // N_STEPS_SMALL
        blocks = pl.BlockSpec((4, tr, D_MODEL), lambda i, place_ref: (0, i, 0))
        in_specs += [pl.BlockSpec((4, 1, tr, D_MODEL), lambda i, place_ref: (0, place_ref[0], i, 0)), blocks]
        out_specs += [pl.BlockSpec((tr, D_MODEL), lambda i, place_ref: (i, 0)), blocks]
        out_shape += [jax.ShapeDtypeStruct(r.shape[1:], F32), jax.ShapeDtypeStruct(r.shape, BF16)]
    res = pl.pallas_call(
        body, name=name,
        grid_spec=pltpu.PrefetchScalarGridSpec(num_scalar_prefetch=1, grid=(N_STEPS_SMALL,), in_specs=in_specs,
                                               out_specs=out_specs),
        out_shape=out_shape, compiler_params=_params(("arbitrary",), VMEM_LIMIT),
    )(place, *[a for pair in zip(grads, recvs) for a in pair])
    return [(res[2 * t], res[2 * t + 1]) for t in range(n)]


def _reduce_adamw(parts, *, name, after=()):
    n = len(parts)

    def body(*refs):
        for t in range(n):
            p_ref, r_ref, w_ref, m_ref, v_ref = refs[5 * t:5 * t + 5]
            g_ref, d_ref, mo_ref, vo_ref = refs[5 * n + 4 * t:5 * n + 4 * t + 4]
            g = p_ref[...] + r_ref[0].astype(F32) + r_ref[1].astype(F32) + r_ref[2].astype(F32)
            g_ref[...] = g
            d_ref[...], mo_ref[...], vo_ref[...] = _adamw_math(w_ref[...], g, m_ref[...], v_ref[...])
        refs[-1][...] = jnp.zeros_like(refs[-1])

    in_specs, out_specs, out_shape = [], [], []
    for own, _, _, _, _ in parts:
        rows = own.shape[0]
        tr = rows // N_STEPS_SMALL
        spec = pl.BlockSpec((tr, D_MODEL), lambda i: (i, 0))
        in_specs += [spec, pl.BlockSpec((3, tr, D_MODEL), lambda i: (0, i, 0)), spec, spec, spec]
        out_specs += [spec] * 4
        out_shape += [jax.ShapeDtypeStruct((rows, D_MODEL), F32)] * 4
    res = pl.pallas_call(
        _behind(body, 5 * n, after), name=name, grid=(N_STEPS_SMALL,),
        in_specs=in_specs + [ANY] * len(after),
        out_specs=out_specs + [pl.BlockSpec((8, 128), lambda i: (0, 0))],
        out_shape=out_shape + [jax.ShapeDtypeStruct((8, 128), F32)],
        compiler_params=_params(("arbitrary",), VMEM_LIMIT),
    )(*[a for part in parts for a in part], *after)
    return [tuple(res[4 * t:4 * t + 4]) for t in range(n)], res[-1]


def _adamw_math(w, g, m, v):
    m = ADAM_B1 * m + (1.0 - ADAM_B1) * g
    v = ADAM_B2 * v + (1.0 - ADAM_B2) * (g * g)
    m_hat = m / (1.0 - ADAM_B1 ** ADAM_STEP)
    v_hat = v / (1.0 - ADAM_B2 ** ADAM_STEP)
    delta = -ADAM_LR * (m_hat / (jnp.sqrt(v_hat) + ADAM_EPS) + ADAM_WD * w)
    return delta, m, v


SMALL_NAMES = ["ffn1_norm", "mix_norm", "ffn2_norm", "final_norm", "conv_w", "attn_sinks"]


def _update_small(given, moments_m, moments_v, small_all, my_index, *, name):
    conv_cols = given["conv_w"].shape[2]
    per_block = 128 // conv_cols

    def two_d(nm, a):
        return a.reshape(1, D_MODEL) if nm == "final_norm" else a

    operands = [two_d(nm, src[nm]) for nm in SMALL_NAMES for src in (given, moments_m, moments_v)]
    n = len(SMALL_NAMES)

    def body(index_ref, all_ref, conv_ref, *refs):
        ins, outs = refs[:3 * n], refs[3 * n:]
        total, conv_total = all_ref[0], conv_ref[0]
        for k in range(1, N_DEV):
            total, conv_total = total + all_ref[k], conv_total + conv_ref[k]
        which = index_ref[0] % per_block
        conv_g = conv_total[4:7, :conv_cols]
        for j in range(1, per_block):
            conv_g = jnp.where(which == j, conv_total[4:7, j * conv_cols:(j + 1) * conv_cols], conv_g)
        grads = [total[0:1], total[1:2], total[2:3], total[3:4], conv_g[None], total[7:8, :N_Q_HEADS]]
        for t, g in enumerate(grads):
            w_ref, m_ref, v_ref = ins[3 * t:3 * t + 3]
            g_ref, d_ref, mo_ref, vo_ref = outs[4 * t:4 * t + 4]
            g_ref[...] = g
            d_ref[...], mo_ref[...], vo_ref[...] = _adamw_math(w_ref[...], g, m_ref[...], v_ref[...])
        outs[-1][...] = total[7:8, LOSS_LANE:LOSS_LANE + 1]

    def whole(shape):
        return pl.BlockSpec(shape, lambda i, index_ref: (0,) * len(shape))

    shapes = [a.shape for a in operands[::3] for _ in range(4)] + [(1, 1)]
    res = pl.pallas_call(
        body, name=name,
        grid_spec=pltpu.PrefetchScalarGridSpec(
            num_scalar_prefetch=1, grid=(1,),
            in_specs=[whole(small_all.shape),
                      pl.BlockSpec((N_DEV, 8, 128), lambda i, index_ref: (0, 0, index_ref[0] // per_block))]
            + [whole(a.shape) for a in operands],
            out_specs=[whole(s) for s in shapes]),
        out_shape=[jax.ShapeDtypeStruct(s, F32) for s in shapes],
        compiler_params=_params(("arbitrary",)),
    )(my_index.astype(jnp.int32).reshape(1), small_all, small_all, *operands)
    results = {nm: tuple(a.reshape(given[nm].shape) for a in res[4 * t:4 * t + 4]) for t, nm in enumerate(SMALL_NAMES)}
    return results, res[-1]


def kernel(x, ffn1_norm, ffn1_w_gate, ffn1_w_up, ffn1_w_down, mix_norm, w_in, conv_w, attn_sinks, w_out, ffn2_norm, ffn2_w_gate, ffn2_w_up, ffn2_w_down, final_norm, loss_target, m_ffn1_norm, m_ffn1_w_gate, m_ffn1_w_up, m_ffn1_w_down, m_mix_norm, m_w_in, m_conv_w, m_attn_sinks, m_w_out, m_ffn2_norm, m_ffn2_w_gate, m_ffn2_w_up, m_ffn2_w_down, m_final_norm, v_ffn1_norm, v_ffn1_w_gate, v_ffn1_w_up, v_ffn1_w_down, v_mix_norm, v_w_in, v_conv_w, v_attn_sinks, v_w_out, v_ffn2_norm, v_ffn2_w_gate, v_ffn2_w_up, v_ffn2_w_down, v_final_norm):
    ix, iy, ic = lax.axis_index("x"), lax.axis_index("y"), lax.axis_index("c")
    my_index = 4 * ix + 2 * iy + ic
    place = jnp.stack([ic, 2 * ix + iy]).astype(jnp.int32)

    given = dict(ffn1_norm=ffn1_norm, ffn1_w_gate=ffn1_w_gate, ffn1_w_up=ffn1_w_up, ffn1_w_down=ffn1_w_down,
                 mix_norm=mix_norm, w_in=w_in, conv_w=conv_w, attn_sinks=attn_sinks, w_out=w_out, ffn2_norm=ffn2_norm,
                 ffn2_w_gate=ffn2_w_gate, ffn2_w_up=ffn2_w_up, ffn2_w_down=ffn2_w_down, final_norm=final_norm)
    moments_m = dict(ffn1_norm=m_ffn1_norm, ffn1_w_gate=m_ffn1_w_gate, ffn1_w_up=m_ffn1_w_up, ffn1_w_down=m_ffn1_w_down,
                     mix_norm=m_mix_norm, w_in=m_w_in, conv_w=m_conv_w, attn_sinks=m_attn_sinks, w_out=m_w_out,
                     ffn2_norm=m_ffn2_norm, ffn2_w_gate=m_ffn2_w_gate, ffn2_w_up=m_ffn2_w_up, ffn2_w_down=m_ffn2_w_down,
                     final_norm=m_final_norm)
    moments_v = dict(ffn1_norm=v_ffn1_norm, ffn1_w_gate=v_ffn1_w_gate, ffn1_w_up=v_ffn1_w_up, ffn1_w_down=v_ffn1_w_down,
                     mix_norm=v_mix_norm, w_in=v_w_in, conv_w=v_conv_w, attn_sinks=v_attn_sinks, w_out=v_w_out,
                     ffn2_norm=v_ffn2_norm, ffn2_w_gate=v_ffn2_w_gate, ffn2_w_up=v_ffn2_w_up, ffn2_w_down=v_ffn2_w_down,
                     final_norm=v_final_norm)

    xs = x[0]
    target = loss_target[0]
    final_gain = final_norm.reshape(1, D_MODEL)

    def ffn_shards(wg, wu, wd):
        return jnp.stack([wg[0].T, wu[0].T]).astype(BF16), wd[0].astype(BF16)

    conv_cols = conv_w.shape[2]
    conv_shard = jnp.pad(conv_w[0], ((0, 5), (0, 128 - conv_cols)))
    gate_up1, down1 = ffn_shards(ffn1_w_gate, ffn1_w_up, ffn1_w_down)
    rest_shards = [down1, w_in[0].T.astype(BF16), w_out[0].astype(BF16), conv_shard,
                   *ffn_shards(ffn2_w_gate, ffn2_w_up, ffn2_w_down)]
    rest_rows = [s.shape[-2] for s in rest_shards]
    n_rest, n_early = len(rest_shards), 4
    (w1_gu,), _ = _all_gather_rows([gate_up1], name="gather_ffn1")

    fulls = [lax.empty(s.shape[:-2] + (N_DEV * s.shape[-2], s.shape[-1]), s.dtype) for s in rest_shards]
    fulls = _place_own(fulls, rest_shards, my_index.astype(jnp.int32).reshape(1), name="place_own_weights")
    ssem, rsem, bufs, token = _split_start(rest_shards + list(fulls), 4 * n_rest, _gather_send_plan(n_rest),
                                           name="gather_rest_start", after=[w1_gu])
    early = bufs[:n_early] + bufs[n_rest:n_rest + n_early]
    late = bufs[n_early:n_rest] + bufs[n_rest + n_early:]
    h1, s1, sa1, sb1 = _ffn_fwd(xs, ffn1_norm, w1_gu, name="ffn1_hidden", after=[token])
    early = _split_wait(ssem, rsem, early, h1, _gather_send_plan(n_early), name="gather_early_wait")
    fwd_early = _gather_forward_plan(rest_rows[:n_early])
    ssem_e, rsem_e, parts, token = _split_start(early[n_early:], 3 * n_early, fwd_early, name="forward_early_start")
    rope = _rope_tables(xs.shape[0], after=[token])
    w1_d, win_t, wout, conv_all = _split_wait(ssem_e, rsem_e, parts, rope, fwd_early, name="forward_early_wait")
    x1 = _ffn_down(xs, s1, w1_d, name="ffn1_down")
    conv_full = conv_all.reshape(N_DEV, 8, 128)[:, :3, :conv_cols].transpose(1, 0, 2).reshape(3, CONV_W)
    late = _split_wait(ssem, rsem, late, x1, _gather_send_plan(n_rest - n_early, first=n_early),
                       name="gather_late_wait")
    fwd_ffn2 = _gather_forward_plan(rest_rows[n_early:])
    ssem, rsem, parts, token = _split_start(late[n_rest - n_early:], 3 * (n_rest - n_early), fwd_ffn2,
                                            name="forward_ffn2_start")
    x2, hm, z, y = _mixer_fwd(x1, mix_norm, win_t, wout, conv_full, attn_sinks, rope, name="mixer_fwd", after=[token])
    w2_gu, w2_d = _split_wait(ssem, rsem, parts, x2, fwd_ffn2, name="forward_ffn2_wait")
    dx3, h2, s2, sa2, sb2, loss_local, d_final = _ffn_fwd(x2, ffn2_norm, w2_gu, w2_d, head=(final_gain, target),
                                                          name="ffn2_fwd")

    def sibling_bufs(grads):
        views = [g.reshape(4, 2, g.shape[0] // N_DEV, D_MODEL) for g in grads]
        return views + [lax.empty((4,) + v.shape[2:], F32) for v in views]

    def chips_bufs(partials, small_all):
        p16 = [p for _, p in partials]
        return p16 + [lax.empty((3,) + p.shape[1:], BF16) for p in p16] + ([] if small_all is None else [small_all])

    def to_sibling_start(grads, tag, after=()):
        plan = _sibling_plan(len(grads))
        ssem, rsem, bufs, token = _split_start(sibling_bufs(grads), len(grads), plan, name=f"{tag}_sibling_start",
                                               after=after)
        return (ssem, rsem, bufs, plan, tag), token

    def to_chips_and_sibling_start(partials, tag, grads, sibling_tag, after=()):
        for_chips, n_ici = chips_bufs(partials, None), 3 * len(partials)
        chips_plan, sibling_plan = _chips_plan(len(partials), False), _sibling_plan(len(grads), first=n_ici)

        def both(bufs, send_sems, recv_sems):
            return (chips_plan(bufs[:len(for_chips)], send_sems, recv_sems)
                    + sibling_plan(bufs[len(for_chips):], send_sems, recv_sems))

        ssem, rsem, bufs, token = _split_start(for_chips + sibling_bufs(grads), n_ici + len(grads), both,
                                               name=f"{tag}_chips_{sibling_tag}_sibling_start", after=after)
        return ((ssem, rsem, bufs[:len(for_chips)], chips_plan, tag),
                (ssem, rsem, bufs[len(for_chips):], sibling_plan, sibling_tag), token)

    def to_sibling_finish(handle, after, names):
        ssem, rsem, bufs, plan, tag = handle
        bufs = _split_wait(ssem, rsem, bufs, after, plan, name=f"{tag}_sibling_wait")
        n = len(names)
        return _add_sibling(bufs[:n], bufs[n:], place, name=f"add_sibling_{tag}")

    def to_chips_start(partials, tag, small_all=None, after=()):
        plan = _chips_plan(len(partials), small_all is not None)
        n_copies = 3 * len(partials) + (0 if small_all is None else N_DEV - 1)
        ssem, rsem, bufs, token = _split_start(chips_bufs(partials, small_all), n_copies, plan,
                                               name=f"{tag}_chips_start", after=after)
        return (ssem, rsem, bufs, plan, tag), token

    def to_chips_finish(handle, partials, after, names):
        ssem, rsem, bufs, plan, tag = handle
        bufs = _split_wait(ssem, rsem, bufs, after, plan, name=f"{tag}_chips_wait")
        n = len(names)
        return [(p32, r) for (p32, _), r in zip(partials, bufs[n:2 * n])], bufs[2 * n:]

    half_ff = D_FF // 2
    names2, namesm = ["ffn2_w_gate", "ffn2_w_up", "ffn2_w_down"], ["w_in", "w_out"]
    transposed = {"ffn1_w_gate", "ffn1_w_up", "w_in", "ffn2_w_gate", "ffn2_w_up"}
    grad, delta, new_m, new_v = {}, {}, {}, {}

    def adam_big(names, parts, tag, after=()):
        def to_rows(nm, a):
            return jnp.swapaxes(a, 1, 2)[0] if nm in transposed else a[0]

        def from_rows(nm, a):
            return jnp.swapaxes(a[None], 1, 2) if nm in transposed else a[None]

        operands = [(p32, recv, to_rows(nm, given[nm]), to_rows(nm, moments_m[nm]), to_rows(nm, moments_v[nm]))
                    for nm, (p32, recv) in zip(names, parts)]
        results, token = _reduce_adamw(operands, name=f"adamw_{tag}", after=after)
        for nm, outs in zip(names, results):
            grad[nm], delta[nm], new_m[nm], new_v[nm] = (from_rows(nm, a) for a in outs)
        return token

    dx2, da2, db2, g2b, d_norm2 = _ffn_dgrad(dx3, x2, ffn2_norm, sa2, sb2, w2_gu, w2_d, name="ffn2_dgrad")
    gw2 = [_tn_matmul(da2, h2, name="ffn2_wgrad_gate", bm=half_ff), _tn_matmul(db2, h2, name="ffn2_wgrad_up", bm=half_ff),
           _tn_matmul(s2, g2b, name="ffn2_wgrad_down", bm=half_ff)]
    sib2, tok = to_sibling_start(gw2, "ffn2")
    dx1, dz, gmb, d_conv, d_sink, d_normm = _mixer_bwd(dx2, x1, mix_norm, y, z, win_t, wout, conv_full, attn_sinks,
                                                       rope, name="mixer_bwd", after=[tok])
    gwm = [_tn_matmul(dz, hm, name="mixer_wgrad_in", bm=Z_W // 2), _tn_matmul(y, gmb, name="mixer_wgrad_out", bm=D_MODEL)]
    p2 = to_sibling_finish(sib2, gwm[1], names2)
    chips2, sibm, tok = to_chips_and_sibling_start(p2, "ffn2", gwm, "mixer")
    dx0, da1, db1, g1b, d_norm1 = _ffn_dgrad(dx1, xs, ffn1_norm, sa1, sb1, w1_gu, w1_d, name="ffn1_dgrad", after=[tok])
    r2, _ = to_chips_finish(chips2, p2, dx0, names2)
    pm = to_sibling_finish(sibm, dx0, namesm)
    gw_gate = _tn_matmul(da1, h1, name="ffn1_wgrad_gate", bm=half_ff)
    chipsm, sib_gate, tok = to_chips_and_sibling_start(pm, "mixer", [gw_gate], "ffn1_gate")
    gw_up = _tn_matmul(db1, h1, name="ffn1_wgrad_up", bm=half_ff, after=[tok])
    rm, _ = to_chips_finish(chipsm, pm, gw_up, namesm)
    p_gate = to_sibling_finish(sib_gate, gw_up, ["ffn1_w_gate"])
    chips_gate, sib_up, tok = to_chips_and_sibling_start(p_gate, "ffn1_gate", [gw_up], "ffn1_up")
    gw_down = _tn_matmul(s1, g1b, name="ffn1_wgrad_down_first", bm=half_ff, blocks=(0, 1), after=[tok])
    p_up = to_sibling_finish(sib_up, gw_down, ["ffn1_w_up"])
    chips_up, tok_a = to_chips_start(p_up, "ffn1_up")
    gw_down = _tn_matmul(s1, g1b, name="ffn1_wgrad_down_second", bm=half_ff, blocks=(1, 1), into=gw_down, after=[tok_a])
    sib_down, tok_b = to_sibling_start([gw_down], "ffn1_down")
    p_down = to_sibling_finish(sib_down, tok_b, ["ffn1_w_down"])
    last_row = (jnp.pad(d_sink, ((0, 0), (0, D_MODEL - 128)))
                + jnp.pad(loss_local, ((0, 0), (LOSS_LANE, D_MODEL - LOSS_LANE - 1))))
    small = jnp.concatenate([
        d_norm1, d_normm, d_norm2, d_final, jnp.pad(d_conv[0:3], ((0, 0), (0, D_MODEL - CONV_W))), last_row], axis=0)
    (small_all,) = _place_own([lax.empty((N_DEV * 8, D_MODEL), F32)], [small], my_index.astype(jnp.int32).reshape(1),
                              name="place_own_small")
    chips_down, tok = to_chips_start(p_down, "ffn1_down", small_all)
    tok = adam_big(names2, r2, "ffn2", after=[tok])
    tok = adam_big(namesm, rm, "mixer", after=[tok])
    r_gate, _ = to_chips_finish(chips_gate, p_gate, tok, ["ffn1_w_gate"])
    tok = adam_big(["ffn1_w_gate"], r_gate, "ffn1_gate")
    r_up, _ = to_chips_finish(chips_up, p_up, tok, ["ffn1_w_up"])
    tok = adam_big(["ffn1_w_up"], r_up, "ffn1_up")
    r_down, (small_all,) = to_chips_finish(chips_down, p_down, tok, ["ffn1_w_down"])
    adam_big(["ffn1_w_down"], r_down, "ffn1_down")
    results, loss = _update_small(given, moments_m, moments_v, small_all.reshape(N_DEV, 8, D_MODEL), my_index,
                                  name="update_small")
    for nm, outs in results.items():
        grad[nm], delta[nm], new_m[nm], new_v[nm] = outs

    order = list(given)
    return (loss.reshape(()), dx0[None], *[grad[n] for n in order], *[delta[n] for n in order],
            *[new_m[n] for n in order], *[new_v[n] for n in order])
```

```python
import functools

import jax
import jax.numpy as jnp
from jax import lax
from jax.experimental import pallas as pl
from jax.experimental.pallas import tpu as pltpu

F32 = jnp.float32
BF16 = jnp.bfloat16
MESH = pl.DeviceIdType.MESH
ANY = pl.BlockSpec(memory_space=pl.ANY)
HBM_SPEC = pl.BlockSpec(memory_space=pltpu.HBM)
SEM_SPEC = pl.BlockSpec(memory_space=pltpu.SEMAPHORE)
DATAFLOW = pltpu.SideEffectType.DATAFLOW_SIDE_EFFECTING

N_DEV = 8
LOSS_LANE = 128
D_MODEL = 1024
D_FF = 2816
CONV_W = 512
ATTN_W = 512
KV_W = 128
HEAD_DIM = 64
N_Q_HEADS = 8
N_KV_HEADS = 2
Q_PER_KV = N_Q_HEADS // N_KV_HEADS
BLOCK = 128
ROT_DIM = 16
ROPE_THETA = 500000.0
Z_W = 3 * CONV_W + ATTN_W + 2 * KV_W
Q_OFF = 3 * CONV_W
K_OFF = Q_OFF + ATTN_W
V_OFF = K_OFF + KV_W
RMS_EPS = 1e-5
MASK_VALUE = -1e30
SM_SCALE = HEAD_DIM ** -0.5
FFN_RES_SCALE = 0.5

ADAM_LR = 0.001
ADAM_B1 = 0.9
ADAM_B2 = 0.999
ADAM_EPS = 1e-08
ADAM_WD = 0.01
ADAM_STEP = 10

NT_DIMS = (((1,), (1,)), ((), ()))
TN_DIMS = (((0,), (0,)), ((), ()))

VMEM_LIMIT = 62 * 1024 * 1024
FF_CHUNK = 256


def _params(sem, vmem=None):
    return pltpu.CompilerParams(dimension_semantics=sem, vmem_limit_bytes=vmem)


def _behind(body, n_in, after):
    k = len(after)
    if k == 0:
        return body
    return lambda *refs: body(*refs[:n_in], *refs[n_in + k:])


def _rms_stats(xf):
    inv = lax.rsqrt(jnp.mean(xf * xf, axis=-1, keepdims=True) + RMS_EPS)
    return xf * inv, inv


def _rms_bwd(dh, xhat, inv, gain):
    dxhat = dh * gain
    dx = inv * (dxhat - xhat * jnp.mean(dxhat * xhat, axis=-1, keepdims=True))
    dgain = jnp.sum(dh * xhat, axis=0, keepdims=True)
    return dx, dgain


LOAD_PIECES = 4


def _load_resident(pairs, sems):
    @pl.when(pl.program_id(0) == 0)
    def _():
        copies = []
        for k, (w_hbm, w_ref) in enumerate(pairs):
            rows = w_hbm.shape[-2] // LOAD_PIECES
            for p in range(LOAD_PIECES):
                piece = (slice(None),) * (len(w_hbm.shape) - 2) + (pl.ds(p * rows, rows), slice(None))
                copies.append(pltpu.make_async_copy(w_hbm.at[piece], w_ref.at[piece], sems.at[k, p]))
        for cp in copies:
            cp.start()
        for cp in copies:
            cp.wait()


def _ffn_fwd(x, gain, w_gu, w_d=None, *, name, head=None, after=(), tm=256, sub=256, tf=FF_CHUNK):
    t = x.shape[0]
    tm = min(tm, t)
    sub = min(sub, tm)
    n_down = 0 if w_d is None else 1
    n_head = 0 if head is None else 2
    assert n_down or not n_head
    n_in = 3 + n_down + n_head

    def body(*refs):
        x_ref, g_ref = refs[:2]
        w_hbms, head_refs = refs[2:3 + n_down], refs[3 + n_down:n_in]
        xo_refs = refs[n_in:n_in + n_down]
        h_ref, s_ref, sa_ref, sb_ref = refs[n_in + n_down:n_in + n_down + 4]
        head_outs = refs[n_in + n_down + 4:n_in + n_down + 4 + n_head]
        w_refs, sems = refs[n_in + n_down + 4 + n_head:-1], refs[-1]
        _load_resident(list(zip(w_hbms, w_refs)), sems)

        @pl.when(pl.program_id(0) == 0)
        def _():
            for ref in head_outs:
                ref[...] = jnp.zeros_like(ref)

        for r0 in range(0, tm, sub):
            rows = slice(r0, r0 + sub)
            xf = x_ref[rows, :]
            xhat, _ = _rms_stats(xf)
            h = (xhat * g_ref[...]).astype(BF16)
            h_ref[rows, :] = h
            for c in range(0, D_FF, tf):
                cols = slice(c, min(c + tf, D_FF))
                a = lax.dot_general(h, w_refs[0][0, cols, :], NT_DIMS, preferred_element_type=F32)
                b = lax.dot_general(h, w_refs[0][1, cols, :], NT_DIMS, preferred_element_type=F32)
                sig = jax.nn.sigmoid(a)
                silu = a * sig
                s_ref[rows, cols] = (silu * b).astype(BF16)
                sa_ref[rows, cols] = (b * (sig * (1.0 + a * (1.0 - sig)))).astype(BF16)
                sb_ref[rows, cols] = silu.astype(BF16)
            if not n_down:
                continue
            xo = xf + FFN_RES_SCALE * jnp.dot(s_ref[rows, :], w_refs[1][...], preferred_element_type=F32)
            if head is None:
                xo_refs[0][rows, :] = xo
            else:
                fg_ref, t_ref = head_refs
                loss_ref, dfg_ref = head_outs
                xhat_o, inv_o = _rms_stats(xo)
                err = xhat_o * fg_ref[...] - t_ref[rows, :]
                loss_ref[...] += 0.5 * jnp.sum(jnp.mean(err * err, axis=-1, keepdims=True), axis=0, keepdims=True)
                xo_refs[0][rows, :], dfg = _rms_bwd(err * (1.0 / D_MODEL), xhat_o, inv_o, fg_ref[...])
                dfg_ref[...] += dfg

    row = pl.BlockSpec((tm, D_MODEL), lambda i: (i, 0))
    hid = pl.BlockSpec((tm, D_FF), lambda i: (i, 0))
    vec = pl.BlockSpec((1, D_MODEL), lambda i: (0, 0))
    head_in = [] if head is None else [vec, row]
    head_out = [] if head is None else [pl.BlockSpec((1, 1), lambda i: (0, 0)), vec]
    head_shape = [] if head is None else [jax.ShapeDtypeStruct((1, 1), F32), jax.ShapeDtypeStruct((1, D_MODEL), F32)]
    return pl.pallas_call(
        _behind(body, n_in, after), name=name, grid=(t // tm,),
        in_specs=[row, vec] + [ANY] * (1 + n_down) + head_in + [ANY] * len(after),
        out_specs=[row] * (n_down + 1) + [hid, hid, hid] + head_out,
        out_shape=[jax.ShapeDtypeStruct((t, D_MODEL), F32)] * n_down + [jax.ShapeDtypeStruct((t, D_MODEL), BF16)]
        + [jax.ShapeDtypeStruct((t, D_FF), BF16)] * 3 + head_shape,
        scratch_shapes=[pltpu.VMEM((2, D_FF, D_MODEL), BF16)] + [pltpu.VMEM((D_FF, D_MODEL), BF16)] * n_down
        + [pltpu.SemaphoreType.DMA((2, LOAD_PIECES))],
        compiler_params=_params(("arbitrary",), VMEM_LIMIT),
    )(x, gain, w_gu, *([] if w_d is None else [w_d]), *(head or ()), *after)


def _ffn_down(x, s, w_d, *, name, after=(), tm=512):
    t = x.shape[0]
    tm = min(tm, t)

    def body(x_ref, s_ref, w_hbm, xo_ref, w_ref, sem):
        _load_resident([(w_hbm, w_ref)], sem)
        xo_ref[...] = x_ref[...] + FFN_RES_SCALE * jnp.dot(s_ref[...], w_ref[...], preferred_element_type=F32)

    row = pl.BlockSpec((tm, D_MODEL), lambda i: (i, 0))
    return pl.pallas_call(
        _behind(body, 3, after), name=name, grid=(t // tm,),
        in_specs=[row, pl.BlockSpec((tm, D_FF), lambda i: (i, 0)), ANY] + [ANY] * len(after), out_specs=row,
        out_shape=jax.ShapeDtypeStruct((t, D_MODEL), F32),
        scratch_shapes=[pltpu.VMEM((D_FF, D_MODEL), BF16), pltpu.SemaphoreType.DMA((1, LOAD_PIECES))],
        compiler_params=_params(("arbitrary",), VMEM_LIMIT),
    )(x, s, w_d, *after)


def _ffn_dgrad(dxo, x, gain, sa, sb, w_gu, w_d, *, name, after=(), tm=512, sub=512, tf=FF_CHUNK):
    t = x.shape[0]
    tm = min(tm, t)
    sub = min(sub, tm)

    def body(dxo_ref, x_ref, g_ref, sa_ref, sb_ref, wgu_hbm, wd_hbm, dxi_ref, da_ref, db_ref, gb_ref, dg_ref,
             wgu_ref, wd_ref, sems):
        _load_resident([(wd_hbm, wd_ref), (wgu_hbm, wgu_ref)], sems)

        @pl.when(pl.program_id(0) == 0)
        def _():
            dg_ref[...] = jnp.zeros_like(dg_ref)

        for r0 in range(0, tm, sub):
            rows = slice(r0, r0 + sub)
            go = dxo_ref[rows, :]
            gb = (FFN_RES_SCALE * go).astype(BF16)
            gb_ref[rows, :] = gb
            for c in range(0, D_FF, tf):
                cols = slice(c, min(c + tf, D_FF))
                ds = lax.dot_general(gb, wd_ref[cols, :], NT_DIMS, preferred_element_type=F32)
                da_ref[rows, cols] = (ds * sa_ref[rows, cols].astype(F32)).astype(BF16)
                db_ref[rows, cols] = (ds * sb_ref[rows, cols].astype(F32)).astype(BF16)
            dh = (jnp.dot(da_ref[rows, :], wgu_ref[0], preferred_element_type=F32)
                  + jnp.dot(db_ref[rows, :], wgu_ref[1], preferred_element_type=F32))
            xhat, inv = _rms_stats(x_ref[rows, :])
            dx, dgain = _rms_bwd(dh, xhat, inv, g_ref[...])
            dxi_ref[rows, :] = go + dx
            dg_ref[...] += dgain

    row = pl.BlockSpec((tm, D_MODEL), lambda i: (i, 0))
    hid = pl.BlockSpec((tm, D_FF), lambda i: (i, 0))
    vec = pl.BlockSpec((1, D_MODEL), lambda i: (0, 0))
    return pl.pallas_call(
        _behind(body, 7, after), name=name, grid=(t // tm,),
        in_specs=[row, row, vec, hid, hid, ANY, ANY] + [ANY] * len(after),
        out_specs=[row, hid, hid, row, vec],
        out_shape=[jax.ShapeDtypeStruct((t, D_MODEL), F32), jax.ShapeDtypeStruct((t, D_FF), BF16),
                   jax.ShapeDtypeStruct((t, D_FF), BF16),
                   jax.ShapeDtypeStruct((t, D_MODEL), BF16), jax.ShapeDtypeStruct((1, D_MODEL), F32)],
        scratch_shapes=[pltpu.VMEM((2, D_FF, D_MODEL), BF16), pltpu.VMEM((D_FF, D_MODEL), BF16),
                        pltpu.SemaphoreType.DMA((2, LOAD_PIECES))],
        compiler_params=_params(("arbitrary",), VMEM_LIMIT),
    )(dxo, x, gain, sa, sb, w_gu, w_d, *after)


def _tn_matmul(a, b, *, name, bm, after=(), tk=2048, blocks=None, into=None):
    t, m = a.shape
    n = b.shape[1]
    tk = min(tk, t)
    nk = t // tk
    first, count = blocks or (0, m // bm)
    behind = ([] if into is None else [into]) + list(after)

    def body(a_ref, b_ref, o_ref):
        @pl.when(pl.program_id(1) == 0)
        def _():
            o_ref[...] = jnp.zeros_like(o_ref)

        o_ref[...] += lax.dot_general(a_ref[...], b_ref[...], TN_DIMS, preferred_element_type=F32)

    return pl.pallas_call(
        _behind(body, 2, behind), name=name, grid=(count, nk),
        in_specs=[pl.BlockSpec((tk, bm), lambda i, k: (k, first + i)), pl.BlockSpec((tk, n), lambda i, k: (k, 0))]
        + [ANY] * len(behind),
        out_specs=pl.BlockSpec((bm, n), lambda i, k: (first + i, 0)),
        out_shape=jax.ShapeDtypeStruct((m, n), F32),
        input_output_aliases={} if into is None else {2: 0},
        compiler_params=_params(("parallel", "arbitrary"), VMEM_LIMIT),
    )(a, b, *behind)


def _rope_tables(t, after=()):
    inv_freq = ROPE_THETA ** (-jnp.arange(0, ROT_DIM, 2, dtype=F32) / ROT_DIM)
    ang = inv_freq[:, None] * jnp.arange(t, dtype=F32)[None, :]
    compact = jnp.stack([jnp.cos(ang), jnp.sin(ang)])
    tr = min(1024, t)

    def body(c_ref, o_ref):
        for k in range(2):
            o_ref[k] = jnp.tile(c_ref[k], (128 // inv_freq.shape[0], 1)).T

    return pl.pallas_call(
        _behind(body, 1, after), name="rope_tables", grid=(t // tr,),
        in_specs=[pl.BlockSpec((2, inv_freq.shape[0], tr), lambda i: (0, 0, i))] + [ANY] * len(after),
        out_specs=pl.BlockSpec((2, tr, 128), lambda i: (0, i, 0)),
        out_shape=jax.ShapeDtypeStruct((2, t, 128), F32), compiler_params=_params(("parallel",)),
    )(compact, *after)


def _rope_multipliers(cos_sin):
    half = ROT_DIM // 2
    cos, sin = cos_sin[0], cos_sin[1]
    d = lax.broadcasted_iota(jnp.int32, cos.shape, 1) & (HEAD_DIM - 1)
    mult = jnp.where(d < ROT_DIM, cos, 1.0)
    from_lo = jnp.where((d >= half) & (d < ROT_DIM), sin, 0.0)
    from_hi = jnp.where(d < half, -sin, 0.0)
    return mult, from_lo, from_hi


def _tile_lanes(tab, width):
    return jnp.tile(tab, (1, width // tab.shape[1]))


def _rope(v, tab):
    w = v.shape[1]
    half_rot = ROT_DIM // 2
    return (v * _tile_lanes(tab[0], w)
            + pltpu.roll(v, half_rot, axis=1) * _tile_lanes(tab[1], w)
            + pltpu.roll(v, w - half_rot, axis=1) * _tile_lanes(tab[2], w))


def _rope_bwd(dv, tab):
    w = dv.shape[1]
    half_rot = ROT_DIM // 2
    return (dv * _tile_lanes(tab[0], w)
            + pltpu.roll(dv * _tile_lanes(tab[1], w), w - half_rot, axis=1)
            + pltpu.roll(dv * _tile_lanes(tab[2], w), half_rot, axis=1))


def _shift_rows(v, prev8_ref, n):
    r = lax.broadcasted_iota(jnp.int32, v.shape, 0)
    rolled = pltpu.roll(v, n, axis=0)
    last = prev8_ref[7:8, :]
    if n == 1:
        return jnp.where(r >= 1, rolled, last)
    return jnp.where(r >= 2, rolled, jnp.where(r == 0, prev8_ref[6:7, :], last))


def _shift_rows_up(v, next8_ref, n):
    rows = v.shape[0]
    r = lax.broadcasted_iota(jnp.int32, v.shape, 0)
    rolled = pltpu.roll(v, rows - n, axis=0)
    first = next8_ref[0:1, :]
    if n == 1:
        return jnp.where(r <= rows - 2, rolled, first)
    return jnp.where(r <= rows - 3, rolled, jnp.where(r == rows - 2, first, next8_ref[1:2, :]))


def _lane_half_mask(shape, half):
    lane = lax.broadcasted_iota(jnp.int32, shape, 1)
    return (lane >= HEAD_DIM) if half else (lane < HEAD_DIM)


def _to_kv_lanes(chunk, head, kv):
    if head % 2 != kv:
        chunk = pltpu.roll(chunk, HEAD_DIM, axis=1)
    return jnp.where(_lane_half_mask(chunk.shape, kv), chunk, 0.0)


def _from_kv_lanes(chunk, head, kv):
    chunk = jnp.where(_lane_half_mask(chunk.shape, kv), chunk, 0.0)
    if head % 2 != kv:
        chunk = pltpu.roll(chunk, HEAD_DIM, axis=1)
    return chunk


def _stack_heads(wide):
    parts = []
    for head in range(N_Q_HEADS):
        chunk = wide[:, (head // 2) * 128:(head // 2 + 1) * 128]
        parts.append(_to_kv_lanes(chunk, head, head // Q_PER_KV))
    return jnp.concatenate(parts, axis=0)


def _window_mask(has_prev):
    shape = (N_Q_HEADS * BLOCK, 2 * BLOCK)
    qi = lax.broadcasted_iota(jnp.int32, shape, 0) & (BLOCK - 1)
    kj = lax.broadcasted_iota(jnp.int32, shape, 1)
    first_key = BLOCK - has_prev * BLOCK
    in_prev = (kj < BLOCK) & (kj > qi) & (kj >= first_key)
    in_own = (kj >= BLOCK) & ((kj - BLOCK) <= qi)
    return in_prev | in_own


def _sink_column(sink_ref):
    row = lax.broadcasted_iota(jnp.int32, (N_Q_HEADS * BLOCK, 1), 0)
    col = jnp.full((N_Q_HEADS * BLOCK, 1), sink_ref[0, 0], F32)
    for head in range(1, N_Q_HEADS):
        col = jnp.where(row >= head * BLOCK, sink_ref[0, head], col)
    return col


def _softmax_with_sink(q4, k2, mask, sink):
    s = lax.dot_general(q4, k2, NT_DIMS, preferred_element_type=F32) * SM_SCALE
    s = jnp.where(mask, s, MASK_VALUE)
    m = jnp.maximum(jnp.max(s, axis=-1, keepdims=True), sink)
    p = jnp.exp(s - m)
    e_sink = jnp.exp(sink - m)
    inv_den = 1.0 / (jnp.sum(p, axis=-1, keepdims=True) + e_sink)
    return p * inv_den, e_sink * inv_den


def _conv_terms(zf, prev8_ref, w_ref):
    b_gate, c_gate, u = zf[:, 0:CONV_W], zf[:, CONV_W:2 * CONV_W], zf[:, 2 * CONV_W:3 * CONV_W]
    vc = c_gate * u
    vm1 = _shift_rows(vc, prev8_ref, 1)
    vm2 = _shift_rows(vc, prev8_ref, 2)
    conv = w_ref[0:1, :] * vm2 + w_ref[1:2, :] * vm1 + w_ref[2:3, :] * vc
    return b_gate, c_gate, u, vc, vm1, vm2, conv


def _mixer_fwd(x, gain, win_t, wout, conv_w, sinks, rope, *, name, after=(), tq=512):
    t = x.shape[0]
    tq = min(tq, t)
    nblk = tq // BLOCK

    def body(x_ref, g_ref, win_hbm, wout_hbm, cw_ref, sink_ref, rope_ref,
             xo_ref, h_ref, z_ref, y_ref, kprev_ref, vprev_ref, cprev_ref, win_ref, wout_ref, sems):
        i = pl.program_id(0)
        _load_resident([(win_hbm, win_ref), (wout_hbm, wout_ref)], sems)

        @pl.when(i == 0)
        def _():
            kprev_ref[...] = jnp.zeros_like(kprev_ref)
            vprev_ref[...] = jnp.zeros_like(vprev_ref)
            cprev_ref[...] = jnp.zeros_like(cprev_ref)

        xf = x_ref[...]
        xhat, _ = _rms_stats(xf)
        h = (xhat * g_ref[...]).astype(BF16)
        h_ref[...] = h

        def project(c0, c1):
            zc = lax.dot_general(h, win_ref[c0:c1, :], NT_DIMS, preferred_element_type=F32).astype(BF16)
            z_ref[:, c0:c1] = zc
            return zc

        zb = project(Q_OFF, Z_W)
        zf = zb.astype(F32)
        tab = _rope_multipliers(rope_ref[...])
        qr = _rope(zf[:, 0:ATTN_W], tab)
        kr = _rope(zf[:, K_OFF - Q_OFF:V_OFF - Q_OFF], tab).astype(BF16)
        vb = zb[:, V_OFF - Q_OFF:Z_W - Q_OFF]
        conv_cols = [(c, c + CONV_W) for c in range(0, Q_OFF, CONV_W)]
        conv_z = []

        y_attn = []
        for j in range(nblk):
            if len(conv_z) < len(conv_cols):
                conv_z.append(project(*conv_cols[len(conv_z)]))
            rows = slice(j * BLOCK, (j + 1) * BLOCK)
            prev = slice((j - 1) * BLOCK, j * BLOCK)
            k2 = jnp.concatenate([kprev_ref[...] if j == 0 else kr[prev], kr[rows]], axis=0)
            v2 = jnp.concatenate([vprev_ref[...] if j == 0 else vb[prev], vb[rows]], axis=0)
            mask = _window_mask(jnp.minimum(i, 1) if j == 0 else 1)
            q8 = _stack_heads(qr[rows]).astype(BF16)
            probs, _ = _softmax_with_sink(q8, k2, mask, _sink_column(sink_ref))
            o8 = jnp.dot(probs.astype(BF16), v2, preferred_element_type=F32)
            chunks = [jnp.zeros((BLOCK, 128), F32) for _ in range(ATTN_W // 128)]
            for head in range(N_Q_HEADS):
                chunks[head // 2] += _from_kv_lanes(o8[head * BLOCK:(head + 1) * BLOCK], head, head // Q_PER_KV)
            y_attn.append(jnp.concatenate(chunks, axis=1))
        kprev_ref[...] = kr[tq - BLOCK:tq]
        vprev_ref[...] = vb[tq - BLOCK:tq]
        while len(conv_z) < len(conv_cols):
            conv_z.append(project(*conv_cols[len(conv_z)]))
        ya = jnp.concatenate(y_attn, axis=0).astype(BF16)
        y_ref[:, CONV_W:] = ya
        xo = xf + jnp.dot(ya, wout_ref[CONV_W:, :], preferred_element_type=F32)
        b_gate, _, _, vc, _, _, conv = _conv_terms(jnp.concatenate(conv_z, axis=1).astype(F32), cprev_ref, cw_ref)
        yc = (b_gate * conv).astype(BF16)
        cprev_ref[...] = vc[tq - 8:tq, :]
        y_ref[:, :CONV_W] = yc
        xo_ref[...] = xo + jnp.dot(yc, wout_ref[:CONV_W, :], preferred_element_type=F32)

    row = pl.BlockSpec((tq, D_MODEL), lambda i: (i, 0))
    full = lambda shape: pl.BlockSpec(shape, lambda i: (0,) * len(shape))
    return pl.pallas_call(
        _behind(body, 7, after), name=name, grid=(t // tq,),
        in_specs=[row, full((1, D_MODEL)), ANY, ANY, full((3, CONV_W)),
                  pl.BlockSpec(memory_space=pltpu.SMEM), pl.BlockSpec((2, tq, 128), lambda i: (0, i, 0))]
        + [ANY] * len(after),
        out_specs=[row, row, pl.BlockSpec((tq, Z_W), lambda i: (i, 0)), row],
        out_shape=[jax.ShapeDtypeStruct((t, D_MODEL), F32), jax.ShapeDtypeStruct((t, D_MODEL), BF16),
                   jax.ShapeDtypeStruct((t, Z_W), BF16), jax.ShapeDtypeStruct((t, D_MODEL), BF16)],
        scratch_shapes=[pltpu.VMEM((BLOCK, KV_W), BF16), pltpu.VMEM((BLOCK, KV_W), BF16),
                        pltpu.VMEM((8, CONV_W), F32), pltpu.VMEM((Z_W, D_MODEL), BF16),
                        pltpu.VMEM((D_MODEL, D_MODEL), BF16), pltpu.SemaphoreType.DMA((2, LOAD_PIECES))],
        compiler_params=_params(("arbitrary",), VMEM_LIMIT),
    )(x, gain, win_t, wout, conv_w, sinks, rope, *after)


def _mixer_bwd(dxo, x, gain, y, z, win_t, wout, conv_w, sinks, rope, *, name, after=(), tq=256):
    t = x.shape[0]
    tq = min(tq, t)
    nt, nblk = t // tq, tq // BLOCK

    def body(dxo_ref, x_ref, g_ref, y_ref, z_ref, zp_ref, win_hbm, wout_hbm, cw_ref, sink_ref, rope_ref, ropep_ref,
             dxi_ref, dz_ref, gb_ref, dcw_ref, dsink_ref, dg_ref, dk_ref, dv_ref, dcn_ref, pvc_ref,
             win_ref, wout_ref, sems):
        i = pl.program_id(0)
        tile = nt - 1 - i
        _load_resident([(win_hbm, win_ref), (wout_hbm, wout_ref)], sems)

        @pl.when(i == 0)
        def _():
            dk_ref[...] = jnp.zeros_like(dk_ref)
            dv_ref[...] = jnp.zeros_like(dv_ref)
            dcn_ref[...] = jnp.zeros_like(dcn_ref)
            dcw_ref[...] = jnp.zeros_like(dcw_ref)
            dsink_ref[...] = jnp.zeros_like(dsink_ref)
            dg_ref[...] = jnp.zeros_like(dg_ref)

        has_prev = jnp.minimum(tile, 1)
        go = dxo_ref[...]
        gb = go.astype(BF16)
        gb_ref[...] = gb
        dy = lax.dot_general(gb, wout_ref[...], NT_DIMS, preferred_element_type=F32)
        dy_conv, dy_attn = dy[:, 0:CONV_W], dy[:, CONV_W:D_MODEL]
        zb, zpb = z_ref[...], zp_ref[...]
        zf = zb.astype(F32)
        zpf = zpb.astype(F32) * has_prev.astype(F32)

        pvc_ref[...] = (zpf[:, CONV_W:2 * CONV_W] * zpf[:, 2 * CONV_W:3 * CONV_W])[BLOCK - 8:BLOCK, :]
        b_gate, c_gate, u, vc, vm1, vm2, conv = _conv_terms(zf, pvc_ref, cw_ref)
        d_bgate = dy_conv * conv
        dc = dy_conv * b_gate
        tap = lax.broadcasted_iota(jnp.int32, (8, CONV_W), 0)
        dcw_ref[...] += jnp.where(tap == 0, jnp.sum(dc * vm2, axis=0, keepdims=True),
                                  jnp.where(tap == 1, jnp.sum(dc * vm1, axis=0, keepdims=True),
                                            jnp.where(tap == 2, jnp.sum(dc * vc, axis=0, keepdims=True), 0.0)))
        dvc = (cw_ref[2:3, :] * dc + cw_ref[1:2, :] * _shift_rows_up(dc, dcn_ref, 1)
               + cw_ref[0:1, :] * _shift_rows_up(dc, dcn_ref, 2))
        dcn_ref[...] = dc[0:8, :]
        d_cgate = dvc * u
        d_u = dvc * c_gate

        tab, tabp = _rope_multipliers(rope_ref[...]), _rope_multipliers(ropep_ref[...])
        qr = _rope(zf[:, Q_OFF:K_OFF], tab)
        kr = _rope(zf[:, K_OFF:V_OFF], tab).astype(BF16)
        kpr = _rope(zpf[:, K_OFF:V_OFF], tabp).astype(BF16)
        vb, vpb = zb[:, V_OFF:Z_W], zpb[:, V_OFF:Z_W]
        out = y_ref[:, CONV_W:D_MODEL].astype(F32)
        do_out = dy_attn * out
        lane = lax.broadcasted_iota(jnp.int32, (1, 128), 1)
        dsink = jnp.zeros((1, 128), F32)
        dk_next, dv_next = dk_ref[...], dv_ref[...]
        dq_rows, dk_rows, dv_rows = [None] * nblk, [None] * nblk, [None] * nblk
        for j in reversed(range(nblk)):
            rows = slice(j * BLOCK, (j + 1) * BLOCK)
            prev = slice((j - 1) * BLOCK, j * BLOCK)
            k2 = jnp.concatenate([kpr if j == 0 else kr[prev], kr[rows]], axis=0)
            v2 = jnp.concatenate([vpb if j == 0 else vb[prev], vb[rows]], axis=0)
            mask = _window_mask(has_prev if j == 0 else 1)
            q8 = _stack_heads(qr[rows]).astype(BF16)
            do8 = _stack_heads(dy_attn[rows]).astype(BF16)
            delta = jnp.sum(_stack_heads(do_out[rows]), axis=-1, keepdims=True)
            probs, p_sink = _softmax_with_sink(q8, k2, mask, _sink_column(sink_ref))
            dp = lax.dot_general(do8, v2, NT_DIMS, preferred_element_type=F32)
            ds = (probs * (dp - delta) * SM_SCALE).astype(BF16)
            dq8 = jnp.dot(ds, k2, preferred_element_type=F32)
            dk2 = lax.dot_general(ds, q8, TN_DIMS, preferred_element_type=F32)
            dv2 = lax.dot_general(probs.astype(BF16), do8, TN_DIMS, preferred_element_type=F32)
            sink_terms = p_sink * delta
            dq_chunks = [jnp.zeros((BLOCK, 128), F32) for _ in range(ATTN_W // 128)]
            for head in range(N_Q_HEADS):
                grp = slice(head * BLOCK, (head + 1) * BLOCK)
                dq_chunks[head // 2] += _from_kv_lanes(dq8[grp], head, head // Q_PER_KV)
                dsink = dsink - jnp.where(lane == head, jnp.sum(sink_terms[grp], axis=0, keepdims=True), 0.0)
            dq_rows[j] = jnp.concatenate(dq_chunks, axis=1)
            dk_rows[j] = dk2[BLOCK:] + dk_next
            dv_rows[j] = dv2[BLOCK:] + dv_next
            dk_next, dv_next = dk2[:BLOCK], dv2[:BLOCK]
        dk_ref[...] = dk_next
        dv_ref[...] = dv_next
        dsink_ref[...] += dsink
        dq = _rope_bwd(jnp.concatenate(dq_rows, axis=0), tab)
        dk = _rope_bwd(jnp.concatenate(dk_rows, axis=0), tab)
        dv = jnp.concatenate(dv_rows, axis=0)

        dzb = jnp.concatenate([d_bgate, d_cgate, d_u, dq, dk, dv], axis=1).astype(BF16)
        dz_ref[...] = dzb
        dh = jnp.dot(dzb, win_ref[...], preferred_element_type=F32)
        xhat, inv = _rms_stats(x_ref[...])
        dx, dgain = _rms_bwd(dh, xhat, inv, g_ref[...])
        dxi_ref[...] = go + dx
        dg_ref[...] += dgain

    rev = lambda i: (nt - 1 - i, 0)
    block_before = lambda i: jnp.maximum((nt - 1 - i) * nblk - 1, 0)
    row = pl.BlockSpec((tq, D_MODEL), rev)
    full = lambda shape: pl.BlockSpec(shape, lambda i: (0,) * len(shape))
    return pl.pallas_call(
        _behind(body, 12, after), name=name, grid=(nt,),
        in_specs=[row, row, full((1, D_MODEL)), row,
                  pl.BlockSpec((tq, Z_W), rev), pl.BlockSpec((BLOCK, Z_W), lambda i: (block_before(i), 0)),
                  ANY, ANY, full((3, CONV_W)),
                  pl.BlockSpec(memory_space=pltpu.SMEM),
                  pl.BlockSpec((2, tq, 128), lambda i: (0, nt - 1 - i, 0)),
                  pl.BlockSpec((2, BLOCK, 128), lambda i: (0, block_before(i), 0))] + [ANY] * len(after),
        out_specs=[row, pl.BlockSpec((tq, Z_W), rev), row, full((8, CONV_W)), full((1, 128)), full((1, D_MODEL))],
        out_shape=[jax.ShapeDtypeStruct((t, D_MODEL), F32), jax.ShapeDtypeStruct((t, Z_W), BF16),
                   jax.ShapeDtypeStruct((t, D_MODEL), BF16), jax.ShapeDtypeStruct((8, CONV_W), F32),
                   jax.ShapeDtypeStruct((1, 128), F32), jax.ShapeDtypeStruct((1, D_MODEL), F32)],
        scratch_shapes=[pltpu.VMEM((BLOCK, KV_W), F32), pltpu.VMEM((BLOCK, KV_W), F32), pltpu.VMEM((8, CONV_W), F32),
                        pltpu.VMEM((8, CONV_W), F32), pltpu.VMEM((Z_W, D_MODEL), BF16),
                        pltpu.VMEM((D_MODEL, D_MODEL), BF16), pltpu.SemaphoreType.DMA((2, LOAD_PIECES))],
        compiler_params=_params(("arbitrary",), VMEM_LIMIT),
    )(dxo, x, gain, y, z, z, win_t, wout, conv_w, sinks, rope, rope, *after)


def _place():
    x, y, c = lax.axis_index("x"), lax.axis_index("y"), lax.axis_index("c")
    other_chips = [(1 - x, y), (x, 1 - y), (1 - x, 1 - y)]
    return x, y, c, other_chips


def _all_gather_rows(shards, place=(), *, name):
    n, p = len(shards), len(place)

    def body(*refs):
        srcs, place_srcs = refs[:n], refs[n:n + p]
        outs, place_outs = refs[n + p:2 * n + p], refs[2 * n + p:2 * (n + p)]
        send_sems, recv_sems, local_sems = refs[2 * (n + p):]
        x, y, c, _ = _place()
        me, sibling = (x, y, c), (x, y, 1 - c)
        relay_from = (x + (1 - c) - 2 * x * (1 - c), y + c - 2 * y * c)
        relay_to = (x + c - 2 * x * c, y + (1 - c) - 2 * y * (1 - c))
        chips = [relay_from, relay_to, (1 - x, 1 - y)]

        def rows(t, px, py, pc):
            r = srcs[t].shape[-2]
            start = pl.multiple_of((4 * px + 2 * py + pc) * r, 16 if r % 16 == 0 else 8)
            if len(srcs[t].shape) == 3:
                return outs[t].at[:, pl.ds(start, r), :]
            return outs[t].at[pl.ds(start, r), :]

        def copy(t, k, block, to, own=False):
            return pltpu.make_async_remote_copy(
                src_ref=srcs[t] if own else rows(t, *block), dst_ref=rows(t, *block),
                send_sem=send_sems.at[t, k], recv_sem=recv_sems.at[t, k], device_id=to, device_id_type=MESH)

        mine = [pltpu.make_async_copy(srcs[t], rows(t, *me), local_sems.at[t]) for t in range(n)]
        mine += [pltpu.make_async_copy(place_srcs[q],
                                       _block_rows(place_outs[q], place_srcs[q].shape[-2], 4 * x + 2 * y + c),
                                       local_sems.at[n + q]) for q in range(p)]
        for q in range(p):
            mine[n + q].start()
        first = []
        for t in range(n):
            mine[t].start()
            first.append(copy(t, 0, me, sibling, own=True))
            first += [copy(t, 1 + j, me, (*chip, c), own=True) for j, chip in enumerate(chips[:2])]
        for cp in first:
            cp.start()
        passed = []
        for j, chip in enumerate(chips):
            for t in range(n):
                copy(t, 1 + j, (*chip, c), me).wait_recv()
                if j == 0:
                    passed.append(copy(t, 3, (*chip, c), (*relay_to, c)))
                    passed[-1].start()
                passed.append(copy(t, 4 + j, (*chip, c), sibling))
                passed[-1].start()
        for t in range(n):
            copy(t, 0, sibling, me).wait_recv()
            for j, chip in enumerate([relay_to, relay_from, chips[2]]):
                copy(t, 4 + j, (*chip, 1 - c), me).wait_recv()
        for cp in first + passed:
            cp.wait_send()
        for cp in mine:
            cp.wait()

    out_shape = [jax.ShapeDtypeStruct(s.shape[:-2] + (N_DEV * s.shape[-2], s.shape[-1]), s.dtype)
                 for s in list(shards) + list(place)]
    res = pl.pallas_call(
        body, name=name, in_specs=[ANY] * (n + p), out_specs=[ANY] * (n + p), out_shape=out_shape,
        scratch_shapes=[pltpu.SemaphoreType.DMA((n, 7)), pltpu.SemaphoreType.DMA((n, 7)),
                        pltpu.SemaphoreType.DMA((n + p,))],
    )(*shards, *place)
    return res[:n], res[n:]


def _split_start(bufs, n_copies, plan, *, name, after=()):
    n = len(bufs)

    def body(*refs):
        token = refs[-1]
        for cp in plan(refs[:n], refs[n], refs[n + 1]):
            cp.start()
        token[...] = jnp.zeros_like(token)

    res = pl.pallas_call(
        _behind(body, n, after), name=name, in_specs=[HBM_SPEC] * n + [ANY] * len(after),
        out_specs=(SEM_SPEC, SEM_SPEC, *[HBM_SPEC] * n, pl.BlockSpec(memory_space=pltpu.VMEM)),
        out_shape=(pltpu.SemaphoreType.DMA((n_copies,)), pltpu.SemaphoreType.DMA((n_copies,)),
                   *[pltpu.HBM(b.shape, b.dtype) for b in bufs], jax.ShapeDtypeStruct((8, 128), F32)),
        input_output_aliases={i: 2 + i for i in range(n)},
        compiler_params=pltpu.CompilerParams(has_side_effects=DATAFLOW),
    )(*[pltpu.with_memory_space_constraint(b, pltpu.HBM) for b in bufs], *after)
    return res[0], res[1], list(res[2:2 + n]), res[-1]


def _split_wait(send_sems, recv_sems, bufs, after, plan, *, name):
    n = len(bufs)

    def body(*refs):
        for cp in plan(refs[:n], refs[n], refs[n + 1]):
            cp.wait_send()
            cp.wait_recv()

    return list(pl.pallas_call(
        body, name=name, in_specs=[HBM_SPEC] * n + [SEM_SPEC, SEM_SPEC, ANY], out_specs=[HBM_SPEC] * n,
        out_shape=tuple(pltpu.HBM(b.shape, b.dtype) for b in bufs),
        input_output_aliases={i: i for i in range(n)},
        compiler_params=pltpu.CompilerParams(has_side_effects=DATAFLOW),
    )(*bufs, send_sems, recv_sems, after))


def _sibling_plan(n, first=0, chips=slice(0, 4)):
    def plan(bufs, send_sems, recv_sems):
        x, y, c, _ = _place()
        return [_remote(bufs[t].at[chips, 1 - c], bufs[n + t].at[chips], send_sems, recv_sems, first + t, (x, y, 1 - c))
                for t in range(n)]
    return plan


def _block_rows(ref, r, blk):
    start = pl.multiple_of(blk * r, 16 if r % 16 == 0 else 8)
    return ref.at[(slice(None),) * (len(ref.shape) - 2) + (pl.ds(start, r), slice(None))]


def _remote(src, dst, send_sems, recv_sems, k, peer):
    return pltpu.make_async_remote_copy(src_ref=src, dst_ref=dst, send_sem=send_sems.at[k], recv_sem=recv_sems.at[k],
                                        device_id=peer, device_id_type=MESH)


def _gather_send_plan(n, first=0):
    def plan(bufs, send_sems, recv_sems):
        x, y, c, chips = _place()
        peers = [(x, y, 1 - c)] + [(px, py, c) for px, py in chips]
        copies = []
        for t in range(n):
            dst = _block_rows(bufs[n + t], bufs[t].shape[-2], 4 * x + 2 * y + c)
            copies += [_remote(bufs[t], dst, send_sems, recv_sems, 4 * (first + t) + k, peer)
                       for k, peer in enumerate(peers)]
        return copies
    return plan


def _gather_forward_plan(rows):
    def plan(bufs, send_sems, recv_sems):
        x, y, c, chips = _place()
        copies = []
        for t, r in enumerate(rows):
            for j, (px, py) in enumerate(chips):
                blk = _block_rows(bufs[t], r, 4 * px + 2 * py + c)
                copies.append(_remote(blk, blk, send_sems, recv_sems, 3 * t + j, (x, y, 1 - c)))
        return copies
    return plan


def _chips_plan(n, with_small):
    def plan(bufs, send_sems, recv_sems):
        x, y, c, chips = _place()
        copies = []
        for t in range(n):
            for j, (px, py) in enumerate(chips):
                copies.append(_remote(bufs[t].at[2 * px + py], bufs[n + t].at[j], send_sems, recv_sems, 3 * t + j,
                                      (px, py, c)))
        if with_small:
            mine = _block_rows(bufs[2 * n], 8, 4 * x + 2 * y + c)
            flips = [(fx, fy, fc) for fx in range(2) for fy in range(2) for fc in range(2)][1:]
            for k, (fx, fy, fc) in enumerate(flips):
                peer = (x + fx - 2 * x * fx, y + fy - 2 * y * fy, c + fc - 2 * c * fc)
                copies.append(_remote(mine, mine, send_sems, recv_sems, 3 * n + k, peer))
        return copies
    return plan


def _place_own(fulls, shards, index, *, name):
    n = len(fulls)

    def body(index_ref, *refs):
        for t in range(n):
            refs[2 * n + t][...] = refs[n + t][...]

    def block_of(shard):
        lead = len(shard.shape) - 2
        return pl.BlockSpec(shard.shape, lambda i, index_ref: (0,) * lead + (index_ref[0], 0))

    def whole(shard):
        return pl.BlockSpec(shard.shape, lambda i, index_ref: (0,) * len(shard.shape))

    return list(pl.pallas_call(
        body, name=name,
        grid_spec=pltpu.PrefetchScalarGridSpec(
            num_scalar_prefetch=1, grid=(1,),
            in_specs=[ANY] * n + [whole(s) for s in shards], out_specs=[block_of(s) for s in shards]),
        out_shape=[jax.ShapeDtypeStruct(f.shape, f.dtype) for f in fulls],
        input_output_aliases={1 + t: t for t in range(n)},
        compiler_params=_params(("arbitrary",)),
    )(index, *fulls, *shards))


N_STEPS_SMALL = 2


def _add_sibling(grads, recvs, place, *, name):
    n = len(grads)

    def body(place_ref, *refs):
        chip = place_ref[1]
        for t in range(n):
            g_ref, r_ref, own_ref, ob_ref = refs[2 * t], refs[2 * t + 1], refs[2 * n + 2 * t], refs[2 * n + 2 * t + 1]
            own = jnp.zeros(own_ref.shape, F32)
            for m in range(4):
                p = g_ref[m, 0] + r_ref[m]
                ob_ref[m] = p.astype(BF16)
                own = jnp.where(chip == m, p, own)
            own_ref[...] = own

    in_specs, out_specs, out_shape = [], [], []
    for g, r in zip(grads, recvs):
        tr = g.shape[2] // N_STEPS_SMALL
        blocks = pl.BlockSpec((4, tr, D_MODEL), lambda i, place_ref: (0, i, 0))
        in_specs += [pl.BlockSpec((4, 1, tr, D_MODEL), lambda i, place_ref: (0, place_ref[0], i, 0)), blocks]
        out_specs += [pl.BlockSpec((tr, D_MODEL), lambda i, place_ref: (i, 0)), blocks]
        out_shape += [jax.ShapeDtypeStruct(r.shape[1:], F32), jax.ShapeDtypeStruct(r.shape, BF16)]
    res = pl.pallas_call(
        body, name=name,
        grid_spec=pltpu.PrefetchScalarGridSpec(num_scalar_prefetch=1, grid=(N_STEPS_SMALL,), in_specs=in_specs,
                                               out_specs=out_specs),
        out_shape=out_shape, compiler_params=_params(("arbitrary",), VMEM_LIMIT),
    )(place, *[a for pair in zip(grads, recvs) for a in pair])
    return [(res[2 * t], res[2 * t + 1]) for t in range(n)]


def _reduce_adamw(parts, *, name, after=()):
    n = len(parts)

    def body(*refs):
        for t in range(n):
            p_ref, r_ref, w_ref, m_ref, v_ref = refs[5 * t:5 * t + 5]
            g_ref, d_ref, mo_ref, vo_ref = refs[5 * n + 4 * t:5 * n + 4 * t + 4]
            g = p_ref[...] + r_ref[0].astype(F32) + r_ref[1].astype(F32) + r_ref[2].astype(F32)
            g_ref[...] = g
            d_ref[...], mo_ref[...], vo_ref[...] = _adamw_math(w_ref[...], g, m_ref[...], v_ref[...])
        refs[-1][...] = jnp.zeros_like(refs[-1])

    in_specs, out_specs, out_shape = [], [], []
    for own, _, _, _, _ in parts:
        rows = own.shape[0]
        tr = rows // N_STEPS_SMALL
        spec = pl.BlockSpec((tr, D_MODEL), lambda i: (i, 0))
        in_specs += [spec, pl.BlockSpec((3, tr, D_MODEL), lambda i: (0, i, 0)), spec, spec, spec]
        out_specs += [spec] * 4
        out_shape += [jax.ShapeDtypeStruct((rows, D_MODEL), F32)] * 4
    res = pl.pallas_call(
        _behind(body, 5 * n, after), name=name, grid=(N_STEPS_SMALL,),
        in_specs=in_specs + [ANY] * len(after),
        out_specs=out_specs + [pl.BlockSpec((8, 128), lambda i: (0, 0))],
        out_shape=out_shape + [jax.ShapeDtypeStruct((8, 128), F32)],
        compiler_params=_params(("arbitrary",), VMEM_LIMIT),
    )(*[a for part in parts for a in part], *after)
    return [tuple(res[4 * t:4 * t + 4]) for t in range(n)], res[-1]


def _adamw_math(w, g, m, v):
    m = ADAM_B1 * m + (1.0 - ADAM_B1) * g
    v = ADAM_B2 * v + (1.0 - ADAM_B2) * (g * g)
    m_hat = m / (1.0 - ADAM_B1 ** ADAM_STEP)
    v_hat = v / (1.0 - ADAM_B2 ** ADAM_STEP)
    delta = -ADAM_LR * (m_hat / (jnp.sqrt(v_hat) + ADAM_EPS) + ADAM_WD * w)
    return delta, m, v


SMALL_NAMES = ["ffn1_norm", "mix_norm", "ffn2_norm", "final_norm", "conv_w", "attn_sinks"]


def _update_small(given, moments_m, moments_v, small_all, my_index, *, name):
    conv_cols = given["conv_w"].shape[2]
    per_block = 128 // conv_cols

    def two_d(nm, a):
        return a.reshape(1, D_MODEL) if nm == "final_norm" else a

    operands = [two_d(nm, src[nm]) for nm in SMALL_NAMES for src in (given, moments_m, moments_v)]
    n = len(SMALL_NAMES)

    def body(index_ref, all_ref, conv_ref, *refs):
        ins, outs = refs[:3 * n], refs[3 * n:]
        total, conv_total = all_ref[0], conv_ref[0]
        for k in range(1, N_DEV):
            total, conv_total = total + all_ref[k], conv_total + conv_ref[k]
        which = index_ref[0] % per_block
        conv_g = conv_total[4:7, :conv_cols]
        for j in range(1, per_block):
            conv_g = jnp.where(which == j, conv_total[4:7, j * conv_cols:(j + 1) * conv_cols], conv_g)
        grads = [total[0:1], total[1:2], total[2:3], total[3:4], conv_g[None], total[7:8, :N_Q_HEADS]]
        for t, g in enumerate(grads):
            w_ref, m_ref, v_ref = ins[3 * t:3 * t + 3]
            g_ref, d_ref, mo_ref, vo_ref = outs[4 * t:4 * t + 4]
            g_ref[...] = g
            d_ref[...], mo_ref[...], vo_ref[...] = _adamw_math(w_ref[...], g, m_ref[...], v_ref[...])
        outs[-1][...] = total[7:8, LOSS_LANE:LOSS_LANE + 1]

    def whole(shape):
        return pl.BlockSpec(shape, lambda i, index_ref: (0,) * len(shape))

    shapes = [a.shape for a in operands[::3] for _ in range(4)] + [(1, 1)]
    res = pl.pallas_call(
        body, name=name,
        grid_spec=pltpu.PrefetchScalarGridSpec(
            num_scalar_prefetch=1, grid=(1,),
            in_specs=[whole(small_all.shape),
                      pl.BlockSpec((N_DEV, 8, 128), lambda i, index_ref: (0, 0, index_ref[0] // per_block))]
            + [whole(a.shape) for a in operands],
            out_specs=[whole(s) for s in shapes]),
        out_shape=[jax.ShapeDtypeStruct(s, F32) for s in shapes],
        compiler_params=_params(("arbitrary",)),
    )(my_index.astype(jnp.int32).reshape(1), small_all, small_all, *operands)
    results = {nm: tuple(a.reshape(given[nm].shape) for a in res[4 * t:4 * t + 4]) for t, nm in enumerate(SMALL_NAMES)}
    return results, res[-1]


def kernel(x, ffn1_norm, ffn1_w_gate, ffn1_w_up, ffn1_w_down, mix_norm, w_in, conv_w, attn_sinks, w_out, ffn2_norm, ffn2_w_gate, ffn2_w_up, ffn2_w_down, final_norm, loss_target, m_ffn1_norm, m_ffn1_w_gate, m_ffn1_w_up, m_ffn1_w_down, m_mix_norm, m_w_in, m_conv_w, m_attn_sinks, m_w_out, m_ffn2_norm, m_ffn2_w_gate, m_ffn2_w_up, m_ffn2_w_down, m_final_norm, v_ffn1_norm, v_ffn1_w_gate, v_ffn1_w_up, v_ffn1_w_down, v_mix_norm, v_w_in, v_conv_w, v_attn_sinks, v_w_out, v_ffn2_norm, v_ffn2_w_gate, v_ffn2_w_up, v_ffn2_w_down, v_final_norm):
    ix, iy, ic = lax.axis_index("x"), lax.axis_index("y"), lax.axis_index("c")
    my_index = 4 * ix + 2 * iy + ic
    place = jnp.stack([ic, 2 * ix + iy]).astype(jnp.int32)

    given = dict(ffn1_norm=ffn1_norm, ffn1_w_gate=ffn1_w_gate, ffn1_w_up=ffn1_w_up, ffn1_w_down=ffn1_w_down,
                 mix_norm=mix_norm, w_in=w_in, conv_w=conv_w, attn_sinks=attn_sinks, w_out=w_out, ffn2_norm=ffn2_norm,
                 ffn2_w_gate=ffn2_w_gate, ffn2_w_up=ffn2_w_up, ffn2_w_down=ffn2_w_down, final_norm=final_norm)
    moments_m = dict(ffn1_norm=m_ffn1_norm, ffn1_w_gate=m_ffn1_w_gate, ffn1_w_up=m_ffn1_w_up, ffn1_w_down=m_ffn1_w_down,
                     mix_norm=m_mix_norm, w_in=m_w_in, conv_w=m_conv_w, attn_sinks=m_attn_sinks, w_out=m_w_out,
                     ffn2_norm=m_ffn2_norm, ffn2_w_gate=m_ffn2_w_gate, ffn2_w_up=m_ffn2_w_up, ffn2_w_down=m_ffn2_w_down,
                     final_norm=m_final_norm)
    moments_v = dict(ffn1_norm=v_ffn1_norm, ffn1_w_gate=v_ffn1_w_gate, ffn1_w_up=v_ffn1_w_up, ffn1_w_down=v_ffn1_w_down,
                     mix_norm=v_mix_norm, w_in=v_w_in, conv_w=v_conv_w, attn_sinks=v_attn_sinks, w_out=v_w_out,
                     ffn2_norm=v_ffn2_norm, ffn2_w_gate=v_ffn2_w_gate, ffn2_w_up=v_ffn2_w_up, ffn2_w_down=v_ffn2_w_down,
                     final_norm=v_final_norm)

    xs = x[0]
    target = loss_target[0]
    final_gain = final_norm.reshape(1, D_MODEL)

    def ffn_shards(wg, wu, wd):
        return jnp.stack([wg[0].T, wu[0].T]).astype(BF16), wd[0].astype(BF16)

    conv_cols = conv_w.shape[2]
    conv_shard = jnp.pad(conv_w[0], ((0, 5), (0, 128 - conv_cols)))
    gate_up1, down1 = ffn_shards(ffn1_w_gate, ffn1_w_up, ffn1_w_down)
    rest_shards = [down1, w_in[0].T.astype(BF16), w_out[0].astype(BF16), conv_shard,
                   *ffn_shards(ffn2_w_gate, ffn2_w_up, ffn2_w_down)]
    rest_rows = [s.shape[-2] for s in rest_shards]
    n_rest, n_early = len(rest_shards), 4
    (w1_gu,), _ = _all_gather_rows([gate_up1], name="gather_ffn1")

    fulls = [lax.empty(s.shape[:-2] + (N_DEV * s.shape[-2], s.shape[-1]), s.dtype) for s in rest_shards]
    fulls = _place_own(fulls, rest_shards, my_index.astype(jnp.int32).reshape(1), name="place_own_weights")
    ssem, rsem, bufs, token = _split_start(rest_shards + list(fulls), 4 * n_rest, _gather_send_plan(n_rest),
                                           name="gather_rest_start", after=[w1_gu])
    early = bufs[:n_early] + bufs[n_rest:n_rest + n_early]
    late = bufs[n_early:n_rest] + bufs[n_rest + n_early:]
    h1, s1, sa1, sb1 = _ffn_fwd(xs, ffn1_norm, w1_gu, name="ffn1_hidden", after=[token])
    early = _split_wait(ssem, rsem, early, h1, _gather_send_plan(n_early), name="gather_early_wait")
    fwd_early = _gather_forward_plan(rest_rows[:n_early])
    ssem_e, rsem_e, parts, token = _split_start(early[n_early:], 3 * n_early, fwd_early, name="forward_early_start")
    rope = _rope_tables(xs.shape[0], after=[token])
    w1_d, win_t, wout, conv_all = _split_wait(ssem_e, rsem_e, parts, rope, fwd_early, name="forward_early_wait")
    x1 = _ffn_down(xs, s1, w1_d, name="ffn1_down")
    conv_full = conv_all.reshape(N_DEV, 8, 128)[:, :3, :conv_cols].transpose(1, 0, 2).reshape(3, CONV_W)
    late = _split_wait(ssem, rsem, late, x1, _gather_send_plan(n_rest - n_early, first=n_early),
                       name="gather_late_wait")
    fwd_ffn2 = _gather_forward_plan(rest_rows[n_early:])
    ssem, rsem, parts, token = _split_start(late[n_rest - n_early:], 3 * (n_rest - n_early), fwd_ffn2,
                                            name="forward_ffn2_start")
    x2, hm, z, y = _mixer_fwd(x1, mix_norm, win_t, wout, conv_full, attn_sinks, rope, name="mixer_fwd", after=[token])
    w2_gu, w2_d = _split_wait(ssem, rsem, parts, x2, fwd_ffn2, name="forward_ffn2_wait")
    dx3, h2, s2, sa2, sb2, loss_local, d_final = _ffn_fwd(x2, ffn2_norm, w2_gu, w2_d, head=(final_gain, target),
                                                          name="ffn2_fwd")

    def sibling_bufs(grads):
        views = [g.reshape(4, 2, g.shape[0] // N_DEV, D_MODEL) for g in grads]
        return views + [lax.empty((4,) + v.shape[2:], F32) for v in views]

    def chips_bufs(partials, small_all):
        p16 = [p for _, p in partials]
        return p16 + [lax.empty((3,) + p.shape[1:], BF16) for p in p16] + ([] if small_all is None else [small_all])

    def to_sibling_start(grads, tag, after=()):
        plan = _sibling_plan(len(grads))
        ssem, rsem, bufs, token = _split_start(sibling_bufs(grads), len(grads), plan, name=f"{tag}_sibling_start",
                                               after=after)
        return (ssem, rsem, bufs, plan, tag), token

    def to_chips_and_sibling_start(partials, tag, grads, sibling_tag, after=(), chips=slice(0, 4)):
        for_chips, n_ici = chips_bufs(partials, None), 3 * len(partials)
        chips_plan = _chips_plan(len(partials), False)
        sibling_plan = _sibling_plan(len(grads), first=n_ici, chips=chips)

        def both(bufs, send_sems, recv_sems):
            return (chips_plan(bufs[:len(for_chips)], send_sems, recv_sems)
                    + sibling_plan(bufs[len(for_chips):], send_sems, recv_sems))

        ssem, rsem, bufs, token = _split_start(for_chips + sibling_bufs(grads), n_ici + len(grads), both,
                                               name=f"{tag}_chips_{sibling_tag}_sibling_start", after=after)
        return ((ssem, rsem, bufs[:len(for_chips)], chips_plan, tag),
                (ssem, rsem, bufs[len(for_chips):], sibling_plan, sibling_tag), token)

    def to_sibling_finish(handle, after, names):
        ssem, rsem, bufs, plan, tag = handle
        bufs = _split_wait(ssem, rsem, bufs, after, plan, name=f"{tag}_sibling_wait")
        n = len(names)
        return _add_sibling(bufs[:n], bufs[n:], place, name=f"add_sibling_{tag}")

    def to_chips_start(partials, tag, small_all=None, after=()):
        plan = _chips_plan(len(partials), small_all is not None)
        n_copies = 3 * len(partials) + (0 if small_all is None else N_DEV - 1)
        ssem, rsem, bufs, token = _split_start(chips_bufs(partials, small_all), n_copies, plan,
                                               name=f"{tag}_chips_start", after=after)
        return (ssem, rsem, bufs, plan, tag), token

    def to_chips_finish(handle, partials, after, names):
        ssem, rsem, bufs, plan, tag = handle
        bufs = _split_wait(ssem, rsem, bufs, after, plan, name=f"{tag}_chips_wait")
        n = len(names)
        return [(p32, r) for (p32, _), r in zip(partials, bufs[n:2 * n])], bufs[2 * n:]

    half_ff = D_FF // 2
    names2, namesm = ["ffn2_w_gate", "ffn2_w_up", "ffn2_w_down"], ["w_in", "w_out"]
    transposed = {"ffn1_w_gate", "ffn1_w_up", "w_in", "ffn2_w_gate", "ffn2_w_up"}
    grad, delta, new_m, new_v = {}, {}, {}, {}

    def adam_big(names, parts, tag, after=()):
        def to_rows(nm, a):
            return jnp.swapaxes(a, 1, 2)[0] if nm in transposed else a[0]

        def from_rows(nm, a):
            return jnp.swapaxes(a[None], 1, 2) if nm in transposed else a[None]

        operands = [(p32, recv, to_rows(nm, given[nm]), to_rows(nm, moments_m[nm]), to_rows(nm, moments_v[nm]))
                    for nm, (p32, recv) in zip(names, parts)]
        results, token = _reduce_adamw(operands, name=f"adamw_{tag}", after=after)
        for nm, outs in zip(names, results):
            grad[nm], delta[nm], new_m[nm], new_v[nm] = (from_rows(nm, a) for a in outs)
        return token

    dx2, da2, db2, g2b, d_norm2 = _ffn_dgrad(dx3, x2, ffn2_norm, sa2, sb2, w2_gu, w2_d, name="ffn2_dgrad")
    gw2 = [_tn_matmul(da2, h2, name="ffn2_wgrad_gate", bm=half_ff), _tn_matmul(db2, h2, name="ffn2_wgrad_up", bm=half_ff),
           _tn_matmul(s2, g2b, name="ffn2_wgrad_down", bm=half_ff)]
    sib2, tok = to_sibling_start(gw2, "ffn2")
    dx1, dz, gmb, d_conv, d_sink, d_normm = _mixer_bwd(dx2, x1, mix_norm, y, z, win_t, wout, conv_full, attn_sinks,
                                                       rope, name="mixer_bwd", after=[tok])
    gwm = [_tn_matmul(dz, hm, name="mixer_wgrad_in", bm=Z_W // 2), _tn_matmul(y, gmb, name="mixer_wgrad_out", bm=D_MODEL)]
    p2 = to_sibling_finish(sib2, gwm[1], names2)
    chips2, sibm, tok = to_chips_and_sibling_start(p2, "ffn2", gwm, "mixer")
    dx0, da1, db1, g1b, d_norm1 = _ffn_dgrad(dx1, xs, ffn1_norm, sa1, sb1, w1_gu, w1_d, name="ffn1_dgrad", after=[tok])
    r2, _ = to_chips_finish(chips2, p2, dx0, names2)
    pm = to_sibling_finish(sibm, dx0, namesm)
    gw_gate = _tn_matmul(da1, h1, name="ffn1_wgrad_gate", bm=half_ff)
    chipsm, sib_gate, tok = to_chips_and_sibling_start(pm, "mixer", [gw_gate], "ffn1_gate")
    gw_up = _tn_matmul(db1, h1, name="ffn1_wgrad_up", bm=half_ff, after=[tok])
    rm, _ = to_chips_finish(chipsm, pm, gw_up, namesm)
    p_gate = to_sibling_finish(sib_gate, gw_up, ["ffn1_w_gate"])
    chips_gate, sib_up, tok = to_chips_and_sibling_start(p_gate, "ffn1_gate", [gw_up], "ffn1_up")
    gw_down = _tn_matmul(s1, g1b, name="ffn1_wgrad_down_first", bm=half_ff, blocks=(0, 1), after=[tok])
    p_up = to_sibling_finish(sib_up, gw_down, ["ffn1_w_up"])
    chips_up, sib_first, tok = to_chips_and_sibling_start(p_up, "ffn1_up", [gw_down], "ffn1_down_first",
                                                          chips=slice(0, 2))
    view, land = sib_first[2]
    gw_down = _tn_matmul(s1, g1b, name="ffn1_wgrad_down_second", bm=half_ff, blocks=(1, 1),
                         into=view.reshape(D_FF, D_MODEL), after=[tok])
    rest = _sibling_plan(1, chips=slice(2, 4))
    ssem, rsem, bufs, tok = _split_start([gw_down.reshape(view.shape), land], 1, rest,
                                         name="ffn1_down_second_sibling_start")
    bufs = _split_wait(sib_first[0], sib_first[1], bufs, tok, sib_first[3], name="ffn1_down_first_sibling_wait")
    bufs = _split_wait(ssem, rsem, bufs, tok, rest, name="ffn1_down_second_sibling_wait")
    p_down = _add_sibling(bufs[:1], bufs[1:], place, name="add_sibling_ffn1_down")
    last_row = (jnp.pad(d_sink, ((0, 0), (0, D_MODEL - 128)))
                + jnp.pad(loss_local, ((0, 0), (LOSS_LANE, D_MODEL - LOSS_LANE - 1))))
    small = jnp.concatenate([
        d_norm1, d_normm, d_norm2, d_final, jnp.pad(d_conv[0:3], ((0, 0), (0, D_MODEL - CONV_W))), last_row], axis=0)
    (small_all,) = _place_own([lax.empty((N_DEV * 8, D_MODEL), F32)], [small], my_index.astype(jnp.int32).reshape(1),
                              name="place_own_small")
    chips_down, tok = to_chips_start(p_down, "ffn1_down", small_all)
    tok = adam_big(names2, r2, "ffn2", after=[tok])
    tok = adam_big(namesm, rm, "mixer", after=[tok])
    r_gate, _ = to_chips_finish(chips_gate, p_gate, tok, ["ffn1_w_gate"])
    tok = adam_big(["ffn1_w_gate"], r_gate, "ffn1_gate")
    r_up, _ = to_chips_finish(chips_up, p_up, tok, ["ffn1_w_up"])
    tok = adam_big(["ffn1_w_up"], r_up, "ffn1_up")
    r_down, (small_all,) = to_chips_finish(chips_down, p_down, tok, ["ffn1_w_down"])
    adam_big(["ffn1_w_down"], r_down, "ffn1_down")
    results, loss = _update_small(given, moments_m, moments_v, small_all.reshape(N_DEV, 8, D_MODEL), my_index,
                                  name="update_small")
    for nm, outs in results.items():
        grad[nm], delta[nm], new_m[nm], new_v[nm] = outs

    order = list(given)
    return (loss.reshape(()), dx0[None], *[grad[n] for n in order], *[delta[n] for n in order],
            *[new_m[n] for n in order], *[new_v[n] for n in order])
```

```python
import functools

import jax
import jax.numpy as jnp
from jax import lax
from jax.experimental import pallas as pl
from jax.experimental.pallas import tpu as pltpu

F32 = jnp.float32
BF16 = jnp.bfloat16
MESH = pl.DeviceIdType.MESH
ANY = pl.BlockSpec(memory_space=pl.ANY)
HBM_SPEC = pl.BlockSpec(memory_space=pltpu.HBM)
SEM_SPEC = pl.BlockSpec(memory_space=pltpu.SEMAPHORE)
DATAFLOW = pltpu.SideEffectType.DATAFLOW_SIDE_EFFECTING

N_DEV = 8
LOSS_LANE = 128
D_MODEL = 1024
D_FF = 2816
CONV_W = 512
ATTN_W = 512
KV_W = 128
HEAD_DIM = 64
N_Q_HEADS = 8
N_KV_HEADS = 2
Q_PER_KV = N_Q_HEADS // N_KV_HEADS
BLOCK = 128
ROT_DIM = 16
ROPE_THETA = 500000.0
Z_W = 3 * CONV_W + ATTN_W + 2 * KV_W
Q_OFF = 3 * CONV_W
K_OFF = Q_OFF + ATTN_W
V_OFF = K_OFF + KV_W
RMS_EPS = 1e-5
MASK_VALUE = -1e30
SM_SCALE = HEAD_DIM ** -0.5
FFN_RES_SCALE = 0.5

ADAM_LR = 0.001
ADAM_B1 = 0.9
ADAM_B2 = 0.999
ADAM_EPS = 1e-08
ADAM_WD = 0.01
ADAM_STEP = 10

NT_DIMS = (((1,), (1,)), ((), ()))
TN_DIMS = (((0,), (0,)), ((), ()))

VMEM_LIMIT = 62 * 1024 * 1024
FF_CHUNK = 256


def _params(sem, vmem=None):
    return pltpu.CompilerParams(dimension_semantics=sem, vmem_limit_bytes=vmem)


def _behind(body, n_in, after):
    k = len(after)
    if k == 0:
        return body
    return lambda *refs: body(*refs[:n_in], *refs[n_in + k:])


def _rms_stats(xf):
    inv = lax.rsqrt(jnp.mean(xf * xf, axis=-1, keepdims=True) + RMS_EPS)
    return xf * inv, inv


def _rms_bwd(dh, xhat, inv, gain):
    dxhat = dh * gain
    dx = inv * (dxhat - xhat * jnp.mean(dxhat * xhat, axis=-1, keepdims=True))
    dgain = jnp.sum(dh * xhat, axis=0, keepdims=True)
    return dx, dgain


LOAD_PIECES = 4


def _load_resident(pairs, sems):
    @pl.when(pl.program_id(0) == 0)
    def _():
        copies = []
        for k, (w_hbm, w_ref) in enumerate(pairs):
            rows = w_hbm.shape[-2] // LOAD_PIECES
            for p in range(LOAD_PIECES):
                piece = (slice(None),) * (len(w_hbm.shape) - 2) + (pl.ds(p * rows, rows), slice(None))
                copies.append(pltpu.make_async_copy(w_hbm.at[piece], w_ref.at[piece], sems.at[k, p]))
        for cp in copies:
            cp.start()
        for cp in copies:
            cp.wait()


def _ffn_fwd(x, gain, w_gu, w_d=None, *, name, head=None, after=(), tm=256, sub=256, tf=FF_CHUNK):
    t = x.shape[0]
    tm = min(tm, t)
    sub = min(sub, tm)
    n_down = 0 if w_d is None else 1
    n_head = 0 if head is None else 2
    assert n_down or not n_head
    n_in = 3 + n_down + n_head

    def body(*refs):
        x_ref, g_ref = refs[:2]
        w_hbms, head_refs = refs[2:3 + n_down], refs[3 + n_down:n_in]
        xo_refs = refs[n_in:n_in + n_down]
        h_ref, s_ref, sa_ref, sb_ref = refs[n_in + n_down:n_in + n_down + 4]
        head_outs = refs[n_in + n_down + 4:n_in + n_down + 4 + n_head]
        w_refs, sems = refs[n_in + n_down + 4 + n_head:-1], refs[-1]
        _load_resident(list(zip(w_hbms, w_refs)), sems)

        @pl.when(pl.program_id(0) == 0)
        def _():
            for ref in head_outs:
                ref[...] = jnp.zeros_like(ref)

        for r0 in range(0, tm, sub):
            rows = slice(r0, r0 + sub)
            xf = x_ref[rows, :]
            xhat, _ = _rms_stats(xf)
            h = (xhat * g_ref[...]).astype(BF16)
            h_ref[rows, :] = h
            for c in range(0, D_FF, tf):
                cols = slice(c, min(c + tf, D_FF))
                a = lax.dot_general(h, w_refs[0][0, cols, :], NT_DIMS, preferred_element_type=F32)
                b = lax.dot_general(h, w_refs[0][1, cols, :], NT_DIMS, preferred_element_type=F32)
                sig = jax.nn.sigmoid(a)
                silu = a * sig
                s_ref[rows, cols] = (silu * b).astype(BF16)
                sa_ref[rows, cols] = (b * (sig * (1.0 + a * (1.0 - sig)))).astype(BF16)
                sb_ref[rows, cols] = silu.astype(BF16)
            if not n_down:
                continue
            xo = xf + FFN_RES_SCALE * jnp.dot(s_ref[rows, :], w_refs[1][...], preferred_element_type=F32)
            if head is None:
                xo_refs[0][rows, :] = xo
            else:
                fg_ref, t_ref = head_refs
                loss_ref, dfg_ref = head_outs
                xhat_o, inv_o = _rms_stats(xo)
                err = xhat_o * fg_ref[...] - t_ref[rows, :]
                loss_ref[...] += 0.5 * jnp.sum(jnp.mean(err * err, axis=-1, keepdims=True), axis=0, keepdims=True)
                xo_refs[0][rows, :], dfg = _rms_bwd(err * (1.0 / D_MODEL), xhat_o, inv_o, fg_ref[...])
                dfg_ref[...] += dfg

    row = pl.BlockSpec((tm, D_MODEL), lambda i: (i, 0))
    hid = pl.BlockSpec((tm, D_FF), lambda i: (i, 0))
    vec = pl.BlockSpec((1, D_MODEL), lambda i: (0, 0))
    head_in = [] if head is None else [vec, row]
    head_out = [] if head is None else [pl.BlockSpec((1, 1), lambda i: (0, 0)), vec]
    head_shape = [] if head is None else [jax.ShapeDtypeStruct((1, 1), F32), jax.ShapeDtypeStruct((1, D_MODEL), F32)]
    return pl.pallas_call(
        _behind(body, n_in, after), name=name, grid=(t // tm,),
        in_specs=[row, vec] + [ANY] * (1 + n_down) + head_in + [ANY] * len(after),
        out_specs=[row] * (n_down + 1) + [hid, hid, hid] + head_out,
        out_shape=[jax.ShapeDtypeStruct((t, D_MODEL), F32)] * n_down + [jax.ShapeDtypeStruct((t, D_MODEL), BF16)]
        + [jax.ShapeDtypeStruct((t, D_FF), BF16)] * 3 + head_shape,
        scratch_shapes=[pltpu.VMEM((2, D_FF, D_MODEL), BF16)] + [pltpu.VMEM((D_FF, D_MODEL), BF16)] * n_down
        + [pltpu.SemaphoreType.DMA((2, LOAD_PIECES))],
        compiler_params=_params(("arbitrary",), VMEM_LIMIT),
    )(x, gain, w_gu, *([] if w_d is None else [w_d]), *(head or ()), *after)


def _ffn_down(x, s, w_d, *, name, after=(), tm=512):
    t = x.shape[0]
    tm = min(tm, t)

    def body(x_ref, s_ref, w_hbm, xo_ref, w_ref, sem):
        _load_resident([(w_hbm, w_ref)], sem)
        xo_ref[...] = x_ref[...] + FFN_RES_SCALE * jnp.dot(s_ref[...], w_ref[...], preferred_element_type=F32)

    row = pl.BlockSpec((tm, D_MODEL), lambda i: (i, 0))
    return pl.pallas_call(
        _behind(body, 3, after), name=name, grid=(t // tm,),
        in_specs=[row, pl.BlockSpec((tm, D_FF), lambda i: (i, 0)), ANY] + [ANY] * len(after), out_specs=row,
        out_shape=jax.ShapeDtypeStruct((t, D_MODEL), F32),
        scratch_shapes=[pltpu.VMEM((D_FF, D_MODEL), BF16), pltpu.SemaphoreType.DMA((1, LOAD_PIECES))],
        compiler_params=_params(("arbitrary",), VMEM_LIMIT),
    )(x, s, w_d, *after)


def _ffn_dgrad(dxo, x, gain, sa, sb, w_gu, w_d, *, name, after=(), tm=512, sub=512, tf=FF_CHUNK):
    t = x.shape[0]
    tm = min(tm, t)
    sub = min(sub, tm)

    def body(dxo_ref, x_ref, g_ref, sa_ref, sb_ref, wgu_hbm, wd_hbm, dxi_ref, da_ref, db_ref, gb_ref, dg_ref,
             wgu_ref, wd_ref, sems):
        _load_resident([(wd_hbm, wd_ref), (wgu_hbm, wgu_ref)], sems)

        @pl.when(pl.program_id(0) == 0)
        def _():
            dg_ref[...] = jnp.zeros_like(dg_ref)

        for r0 in range(0, tm, sub):
            rows = slice(r0, r0 + sub)
            go = dxo_ref[rows, :]
            gb = (FFN_RES_SCALE * go).astype(BF16)
            gb_ref[rows, :] = gb
            for c in range(0, D_FF, tf):
                cols = slice(c, min(c + tf, D_FF))
                ds = lax.dot_general(gb, wd_ref[cols, :], NT_DIMS, preferred_element_type=F32)
                da_ref[rows, cols] = (ds * sa_ref[rows, cols].astype(F32)).astype(BF16)
                db_ref[rows, cols] = (ds * sb_ref[rows, cols].astype(F32)).astype(BF16)
            dh = (jnp.dot(da_ref[rows, :], wgu_ref[0], preferred_element_type=F32)
                  + jnp.dot(db_ref[rows, :], wgu_ref[1], preferred_element_type=F32))
            xhat, inv = _rms_stats(x_ref[rows, :])
            dx, dgain = _rms_bwd(dh, xhat, inv, g_ref[...])
            dxi_ref[rows, :] = go + dx
            dg_ref[...] += dgain

    row = pl.BlockSpec((tm, D_MODEL), lambda i: (i, 0))
    hid = pl.BlockSpec((tm, D_FF), lambda i: (i, 0))
    vec = pl.BlockSpec((1, D_MODEL), lambda i: (0, 0))
    return pl.pallas_call(
        _behind(body, 7, after), name=name, grid=(t // tm,),
        in_specs=[row, row, vec, hid, hid, ANY, ANY] + [ANY] * len(after),
        out_specs=[row, hid, hid, row, vec],
        out_shape=[jax.ShapeDtypeStruct((t, D_MODEL), F32), jax.ShapeDtypeStruct((t, D_FF), BF16),
                   jax.ShapeDtypeStruct((t, D_FF), BF16),
                   jax.ShapeDtypeStruct((t, D_MODEL), BF16), jax.ShapeDtypeStruct((1, D_MODEL), F32)],
        scratch_shapes=[pltpu.VMEM((2, D_FF, D_MODEL), BF16), pltpu.VMEM((D_FF, D_MODEL), BF16),
                        pltpu.SemaphoreType.DMA((2, LOAD_PIECES))],
        compiler_params=_params(("arbitrary",), VMEM_LIMIT),
    )(dxo, x, gain, sa, sb, w_gu, w_d, *after)


def _tn_matmul(a, b, *, name, bm, after=(), tk=2048, blocks=None, into=None):
    t, m = a.shape
    n = b.shape[1]
    tk = min(tk, t)
    nk = t // tk
    first, count = blocks or (0, m // bm)
    behind = ([] if into is None else [into]) + list(after)

    def body(a_ref, b_ref, o_ref):
        @pl.when(pl.program_id(1) == 0)
        def _():
            o_ref[...] = jnp.zeros_like(o_ref)

        o_ref[...] += lax.dot_general(a_ref[...], b_ref[...], TN_DIMS, preferred_element_type=F32)

    return pl.pallas_call(
        _behind(body, 2, behind), name=name, grid=(count, nk),
        in_specs=[pl.BlockSpec((tk, bm), lambda i, k: (k, first + i)), pl.BlockSpec((tk, n), lambda i, k: (k, 0))]
        + [ANY] * len(behind),
        out_specs=pl.BlockSpec((bm, n), lambda i, k: (first + i, 0)),
        out_shape=jax.ShapeDtypeStruct((m, n), F32),
        input_output_aliases={} if into is None else {2: 0},
        compiler_params=_params(("parallel", "arbitrary"), VMEM_LIMIT),
    )(a, b, *behind)


def _rope_tables(t, after=()):
    inv_freq = ROPE_THETA ** (-jnp.arange(0, ROT_DIM, 2, dtype=F32) / ROT_DIM)
    ang = inv_freq[:, None] * jnp.arange(t, dtype=F32)[None, :]
    compact = jnp.stack([jnp.cos(ang), jnp.sin(ang)])
    tr = min(1024, t)

    def body(c_ref, o_ref):
        for k in range(2):
            o_ref[k] = jnp.tile(c_ref[k], (128 // inv_freq.shape[0], 1)).T

    return pl.pallas_call(
        _behind(body, 1, after), name="rope_tables", grid=(t // tr,),
        in_specs=[pl.BlockSpec((2, inv_freq.shape[0], tr), lambda i: (0, 0, i))] + [ANY] * len(after),
        out_specs=pl.BlockSpec((2, tr, 128), lambda i: (0, i, 0)),
        out_shape=jax.ShapeDtypeStruct((2, t, 128), F32), compiler_params=_params(("parallel",)),
    )(compact, *after)


def _rope_multipliers(cos_sin):
    half = ROT_DIM // 2
    cos, sin = cos_sin[0], cos_sin[1]
    d = lax.broadcasted_iota(jnp.int32, cos.shape, 1) & (HEAD_DIM - 1)
    mult = jnp.where(d < ROT_DIM, cos, 1.0)
    from_lo = jnp.where((d >= half) & (d < ROT_DIM), sin, 0.0)
    from_hi = jnp.where(d < half, -sin, 0.0)
    return mult, from_lo, from_hi


def _tile_lanes(tab, width):
    return jnp.tile(tab, (1, width // tab.shape[1]))


def _rope(v, tab):
    w = v.shape[1]
    half_rot = ROT_DIM // 2
    return (v * _tile_lanes(tab[0], w)
            + pltpu.roll(v, half_rot, axis=1) * _tile_lanes(tab[1], w)
            + pltpu.roll(v, w - half_rot, axis=1) * _tile_lanes(tab[2], w))


def _rope_bwd(dv, tab):
    w = dv.shape[1]
    half_rot = ROT_DIM // 2
    return (dv * _tile_lanes(tab[0], w)
            + pltpu.roll(dv * _tile_lanes(tab[1], w), w - half_rot, axis=1)
            + pltpu.roll(dv * _tile_lanes(tab[2], w), half_rot, axis=1))


def _shift_rows(v, prev8_ref, n):
    r = lax.broadcasted_iota(jnp.int32, v.shape, 0)
    rolled = pltpu.roll(v, n, axis=0)
    last = prev8_ref[7:8, :]
    if n == 1:
        return jnp.where(r >= 1, rolled, last)
    return jnp.where(r >= 2, rolled, jnp.where(r == 0, prev8_ref[6:7, :], last))


def _shift_rows_up(v, next8_ref, n):
    rows = v.shape[0]
    r = lax.broadcasted_iota(jnp.int32, v.shape, 0)
    rolled = pltpu.roll(v, rows - n, axis=0)
    first = next8_ref[0:1, :]
    if n == 1:
        return jnp.where(r <= rows - 2, rolled, first)
    return jnp.where(r <= rows - 3, rolled, jnp.where(r == rows - 2, first, next8_ref[1:2, :]))


def _lane_half_mask(shape, half):
    lane = lax.broadcasted_iota(jnp.int32, shape, 1)
    return (lane >= HEAD_DIM) if half else (lane < HEAD_DIM)


def _to_kv_lanes(chunk, head, kv):
    if head % 2 != kv:
        chunk = pltpu.roll(chunk, HEAD_DIM, axis=1)
    return jnp.where(_lane_half_mask(chunk.shape, kv), chunk, 0.0)


def _from_kv_lanes(chunk, head, kv):
    chunk = jnp.where(_lane_half_mask(chunk.shape, kv), chunk, 0.0)
    if head % 2 != kv:
        chunk = pltpu.roll(chunk, HEAD_DIM, axis=1)
    return chunk


def _stack_heads(wide):
    parts = []
    for head in range(N_Q_HEADS):
        chunk = wide[:, (head // 2) * 128:(head // 2 + 1) * 128]
        parts.append(_to_kv_lanes(chunk, head, head // Q_PER_KV))
    return jnp.concatenate(parts, axis=0)


def _window_mask(has_prev):
    shape = (N_Q_HEADS * BLOCK, 2 * BLOCK)
    qi = lax.broadcasted_iota(jnp.int32, shape, 0) & (BLOCK - 1)
    kj = lax.broadcasted_iota(jnp.int32, shape, 1)
    first_key = BLOCK - has_prev * BLOCK
    in_prev = (kj < BLOCK) & (kj > qi) & (kj >= first_key)
    in_own = (kj >= BLOCK) & ((kj - BLOCK) <= qi)
    return in_prev | in_own


def _sink_column(sink_ref):
    row = lax.broadcasted_iota(jnp.int32, (N_Q_HEADS * BLOCK, 1), 0)
    col = jnp.full((N_Q_HEADS * BLOCK, 1), sink_ref[0, 0], F32)
    for head in range(1, N_Q_HEADS):
        col = jnp.where(row >= head * BLOCK, sink_ref[0, head], col)
    return col


def _softmax_with_sink(q4, k2, mask, sink):
    s = lax.dot_general(q4, k2, NT_DIMS, preferred_element_type=F32) * SM_SCALE
    s = jnp.where(mask, s, MASK_VALUE)
    m = jnp.maximum(jnp.max(s, axis=-1, keepdims=True), sink)
    p = jnp.exp(s - m)
    e_sink = jnp.exp(sink - m)
    inv_den = 1.0 / (jnp.sum(p, axis=-1, keepdims=True) + e_sink)
    return p * inv_den, e_sink * inv_den


def _conv_terms(zf, prev8_ref, w_ref):
    b_gate, c_gate, u = zf[:, 0:CONV_W], zf[:, CONV_W:2 * CONV_W], zf[:, 2 * CONV_W:3 * CONV_W]
    vc = c_gate * u
    vm1 = _shift_rows(vc, prev8_ref, 1)
    vm2 = _shift_rows(vc, prev8_ref, 2)
    conv = w_ref[0:1, :] * vm2 + w_ref[1:2, :] * vm1 + w_ref[2:3, :] * vc
    return b_gate, c_gate, u, vc, vm1, vm2, conv


def _mixer_fwd(x, gain, win_t, wout, conv_w, sinks, rope, *, name, after=(), tq=512):
    t = x.shape[0]
    tq = min(tq, t)
    nblk = tq // BLOCK

    def body(x_ref, g_ref, win_hbm, wout_hbm, cw_ref, sink_ref, rope_ref,
             xo_ref, h_ref, z_ref, y_ref, kprev_ref, vprev_ref, cprev_ref, win_ref, wout_ref, sems):
        i = pl.program_id(0)
        _load_resident([(win_hbm, win_ref), (wout_hbm, wout_ref)], sems)

        @pl.when(i == 0)
        def _():
            kprev_ref[...] = jnp.zeros_like(kprev_ref)
            vprev_ref[...] = jnp.zeros_like(vprev_ref)
            cprev_ref[...] = jnp.zeros_like(cprev_ref)

        xf = x_ref[...]
        xhat, _ = _rms_stats(xf)
        h = (xhat * g_ref[...]).astype(BF16)
        h_ref[...] = h

        def project(c0, c1):
            zc = lax.dot_general(h, win_ref[c0:c1, :], NT_DIMS, preferred_element_type=F32).astype(BF16)
            z_ref[:, c0:c1] = zc
            return zc

        zb = project(Q_OFF, Z_W)
        zf = zb.astype(F32)
        tab = _rope_multipliers(rope_ref[...])
        qr = _rope(zf[:, 0:ATTN_W], tab)
        kr = _rope(zf[:, K_OFF - Q_OFF:V_OFF - Q_OFF], tab).astype(BF16)
        vb = zb[:, V_OFF - Q_OFF:Z_W - Q_OFF]
        conv_cols = [(c, c + CONV_W) for c in range(0, Q_OFF, CONV_W)]
        conv_z = []

        y_attn = []
        for j in range(nblk):
            if len(conv_z) < len(conv_cols):
                conv_z.append(project(*conv_cols[len(conv_z)]))
            rows = slice(j * BLOCK, (j + 1) * BLOCK)
            prev = slice((j - 1) * BLOCK, j * BLOCK)
            k2 = jnp.concatenate([kprev_ref[...] if j == 0 else kr[prev], kr[rows]], axis=0)
            v2 = jnp.concatenate([vprev_ref[...] if j == 0 else vb[prev], vb[rows]], axis=0)
            mask = _window_mask(jnp.minimum(i, 1) if j == 0 else 1)
            q8 = _stack_heads(qr[rows]).astype(BF16)
            probs, _ = _softmax_with_sink(q8, k2, mask, _sink_column(sink_ref))
            o8 = jnp.dot(probs.astype(BF16), v2, preferred_element_type=F32)
            chunks = [jnp.zeros((BLOCK, 128), F32) for _ in range(ATTN_W // 128)]
            for head in range(N_Q_HEADS):
                chunks[head // 2] += _from_kv_lanes(o8[head * BLOCK:(head + 1) * BLOCK], head, head // Q_PER_KV)
            y_attn.append(jnp.concatenate(chunks, axis=1))
        kprev_ref[...] = kr[tq - BLOCK:tq]
        vprev_ref[...] = vb[tq - BLOCK:tq]
        while len(conv_z) < len(conv_cols):
            conv_z.append(project(*conv_cols[len(conv_z)]))
        ya = jnp.concatenate(y_attn, axis=0).astype(BF16)
        y_ref[:, CONV_W:] = ya
        xo = xf + jnp.dot(ya, wout_ref[CONV_W:, :], preferred_element_type=F32)
        b_gate, _, _, vc, _, _, conv = _conv_terms(jnp.concatenate(conv_z, axis=1).astype(F32), cprev_ref, cw_ref)
        yc = (b_gate * conv).astype(BF16)
        cprev_ref[...] = vc[tq - 8:tq, :]
        y_ref[:, :CONV_W] = yc
        xo_ref[...] = xo + jnp.dot(yc, wout_ref[:CONV_W, :], preferred_element_type=F32)

    row = pl.BlockSpec((tq, D_MODEL), lambda i: (i, 0))
    full = lambda shape: pl.BlockSpec(shape, lambda i: (0,) * len(shape))
    return pl.pallas_call(
        _behind(body, 7, after), name=name, grid=(t // tq,),
        in_specs=[row, full((1, D_MODEL)), ANY, ANY, full((3, CONV_W)),
                  pl.BlockSpec(memory_space=pltpu.SMEM), pl.BlockSpec((2, tq, 128), lambda i: (0, i, 0))]
        + [ANY] * len(after),
        out_specs=[row, row, pl.BlockSpec((tq, Z_W), lambda i: (i, 0)), row],
        out_shape=[jax.ShapeDtypeStruct((t, D_MODEL), F32), jax.ShapeDtypeStruct((t, D_MODEL), BF16),
                   jax.ShapeDtypeStruct((t, Z_W), BF16), jax.ShapeDtypeStruct((t, D_MODEL), BF16)],
        scratch_shapes=[pltpu.VMEM((BLOCK, KV_W), BF16), pltpu.VMEM((BLOCK, KV_W), BF16),
                        pltpu.VMEM((8, CONV_W), F32), pltpu.VMEM((Z_W, D_MODEL), BF16),
                        pltpu.VMEM((D_MODEL, D_MODEL), BF16), pltpu.SemaphoreType.DMA((2, LOAD_PIECES))],
        compiler_params=_params(("arbitrary",), VMEM_LIMIT),
    )(x, gain, win_t, wout, conv_w, sinks, rope, *after)


def _mixer_bwd(dxo, x, gain, y, z, win_t, wout, conv_w, sinks, rope, *, name, after=(), tq=256):
    t = x.shape[0]
    tq = min(tq, t)
    nt, nblk = t // tq, tq // BLOCK

    def body(dxo_ref, x_ref, g_ref, y_ref, z_ref, zp_ref, win_hbm, wout_hbm, cw_ref, sink_ref, rope_ref, ropep_ref,
             dxi_ref, dz_ref, gb_ref, dcw_ref, dsink_ref, dg_ref, dk_ref, dv_ref, dcn_ref, pvc_ref,
             win_ref, wout_ref, sems):
        i = pl.program_id(0)
        tile = nt - 1 - i
        _load_resident([(win_hbm, win_ref), (wout_hbm, wout_ref)], sems)

        @pl.when(i == 0)
        def _():
            dk_ref[...] = jnp.zeros_like(dk_ref)
            dv_ref[...] = jnp.zeros_like(dv_ref)
            dcn_ref[...] = jnp.zeros_like(dcn_ref)
            dcw_ref[...] = jnp.zeros_like(dcw_ref)
            dsink_ref[...] = jnp.zeros_like(dsink_ref)
            dg_ref[...] = jnp.zeros_like(dg_ref)

        has_prev = jnp.minimum(tile, 1)
        go = dxo_ref[...]
        gb = go.astype(BF16)
        gb_ref[...] = gb
        dy_attn = lax.dot_general(gb, wout_ref[CONV_W:D_MODEL, :], NT_DIMS, preferred_element_type=F32)
        zb, zpb = z_ref[...], zp_ref[...]
        zf = zb.astype(F32)
        zpf = zpb.astype(F32) * has_prev.astype(F32)

        tab, tabp = _rope_multipliers(rope_ref[...]), _rope_multipliers(ropep_ref[...])
        qr = _rope(zf[:, Q_OFF:K_OFF], tab)
        kr = _rope(zf[:, K_OFF:V_OFF], tab).astype(BF16)
        kpr = _rope(zpf[:, K_OFF:V_OFF], tabp).astype(BF16)
        vb, vpb = zb[:, V_OFF:Z_W], zpb[:, V_OFF:Z_W]
        out = y_ref[:, CONV_W:D_MODEL].astype(F32)
        do_out = dy_attn * out
        lane = lax.broadcasted_iota(jnp.int32, (1, 128), 1)
        dsink = jnp.zeros((1, 128), F32)
        dk_next, dv_next = dk_ref[...], dv_ref[...]
        dq_rows, dk_rows, dv_rows = [None] * nblk, [None] * nblk, [None] * nblk
        for j in reversed(range(nblk)):
            rows = slice(j * BLOCK, (j + 1) * BLOCK)
            prev = slice((j - 1) * BLOCK, j * BLOCK)
            k2 = jnp.concatenate([kpr if j == 0 else kr[prev], kr[rows]], axis=0)
            v2 = jnp.concatenate([vpb if j == 0 else vb[prev], vb[rows]], axis=0)
            mask = _window_mask(has_prev if j == 0 else 1)
            q8 = _stack_heads(qr[rows]).astype(BF16)
            do8 = _stack_heads(dy_attn[rows]).astype(BF16)
            delta = jnp.sum(_stack_heads(do_out[rows]), axis=-1, keepdims=True)
            probs, p_sink = _softmax_with_sink(q8, k2, mask, _sink_column(sink_ref))
            dp = lax.dot_general(do8, v2, NT_DIMS, preferred_element_type=F32)
            ds = (probs * (dp - delta) * SM_SCALE).astype(BF16)
            dq8 = jnp.dot(ds, k2, preferred_element_type=F32)
            dk2 = lax.dot_general(ds, q8, TN_DIMS, preferred_element_type=F32)
            dv2 = lax.dot_general(probs.astype(BF16), do8, TN_DIMS, preferred_element_type=F32)
            sink_terms = p_sink * delta
            dq_chunks = [jnp.zeros((BLOCK, 128), F32) for _ in range(ATTN_W // 128)]
            for head in range(N_Q_HEADS):
                grp = slice(head * BLOCK, (head + 1) * BLOCK)
                dq_chunks[head // 2] += _from_kv_lanes(dq8[grp], head, head // Q_PER_KV)
                dsink = dsink - jnp.where(lane == head, jnp.sum(sink_terms[grp], axis=0, keepdims=True), 0.0)
            dq_rows[j] = jnp.concatenate(dq_chunks, axis=1)
            dk_rows[j] = dk2[BLOCK:] + dk_next
            dv_rows[j] = dv2[BLOCK:] + dv_next
            dk_next, dv_next = dk2[:BLOCK], dv2[:BLOCK]
            if j == nblk - 1:
                dy_conv = lax.dot_general(gb, wout_ref[0:CONV_W, :], NT_DIMS, preferred_element_type=F32)
        dk_ref[...] = dk_next
        dv_ref[...] = dv_next
        dsink_ref[...] += dsink
        dq = _rope_bwd(jnp.concatenate(dq_rows, axis=0), tab)
        dk = _rope_bwd(jnp.concatenate(dk_rows, axis=0), tab)
        dv = jnp.concatenate(dv_rows, axis=0)

        pvc_ref[...] = (zpf[:, CONV_W:2 * CONV_W] * zpf[:, 2 * CONV_W:3 * CONV_W])[BLOCK - 8:BLOCK, :]
        b_gate, c_gate, u, vc, vm1, vm2, conv = _conv_terms(zf, pvc_ref, cw_ref)
        d_bgate = dy_conv * conv
        dc = dy_conv * b_gate
        tap = lax.broadcasted_iota(jnp.int32, (8, CONV_W), 0)
        dcw_ref[...] += jnp.where(tap == 0, jnp.sum(dc * vm2, axis=0, keepdims=True),
                                  jnp.where(tap == 1, jnp.sum(dc * vm1, axis=0, keepdims=True),
                                            jnp.where(tap == 2, jnp.sum(dc * vc, axis=0, keepdims=True), 0.0)))
        dvc = (cw_ref[2:3, :] * dc + cw_ref[1:2, :] * _shift_rows_up(dc, dcn_ref, 1)
               + cw_ref[0:1, :] * _shift_rows_up(dc, dcn_ref, 2))
        dcn_ref[...] = dc[0:8, :]
        d_cgate = dvc * u
        d_u = dvc * c_gate

        dzb =jnp.concatenate([d_bgate, d_cgate, d_u, dq, dk, dv], axis=1).astype(BF16)
        dz_ref[...] = dzb
        dh = jnp.dot(dzb, win_ref[...], preferred_element_type=F32)
        xhat, inv = _rms_stats(x_ref[...])
        dx, dgain = _rms_bwd(dh, xhat, inv, g_ref[...])
        dxi_ref[...] = go + dx
        dg_ref[...] += dgain

    rev = lambda i: (nt - 1 - i, 0)
    block_before = lambda i: jnp.maximum((nt - 1 - i) * nblk - 1, 0)
    row = pl.BlockSpec((tq, D_MODEL), rev)
    full = lambda shape: pl.BlockSpec(shape, lambda i: (0,) * len(shape))
    return pl.pallas_call(
        _behind(body, 12, after), name=name, grid=(nt,),
        in_specs=[row, row, full((1, D_MODEL)), row,
                  pl.BlockSpec((tq, Z_W), rev), pl.BlockSpec((BLOCK, Z_W), lambda i: (block_before(i), 0)),
                  ANY, ANY, full((3, CONV_W)),
                  pl.BlockSpec(memory_space=pltpu.SMEM),
                  pl.BlockSpec((2, tq, 128), lambda i: (0, nt - 1 - i, 0)),
                  pl.BlockSpec((2, BLOCK, 128), lambda i: (0, block_before(i), 0))] + [ANY] * len(after),
        out_specs=[row, pl.BlockSpec((tq, Z_W), rev), row, full((8, CONV_W)), full((1, 128)), full((1, D_MODEL))],
        out_shape=[jax.ShapeDtypeStruct((t, D_MODEL), F32), jax.ShapeDtypeStruct((t, Z_W), BF16),
                   jax.ShapeDtypeStruct((t, D_MODEL), BF16), jax.ShapeDtypeStruct((8, CONV_W), F32),
                   jax.ShapeDtypeStruct((1, 128), F32), jax.ShapeDtypeStruct((1, D_MODEL), F32)],
        scratch_shapes=[pltpu.VMEM((BLOCK, KV_W), F32), pltpu.VMEM((BLOCK, KV_W), F32), pltpu.VMEM((8, CONV_W), F32),
                        pltpu.VMEM((8, CONV_W), F32), pltpu.VMEM((Z_W, D_MODEL), BF16),
                        pltpu.VMEM((D_MODEL, D_MODEL), BF16), pltpu.SemaphoreType.DMA((2, LOAD_PIECES))],
        compiler_params=_params(("arbitrary",), VMEM_LIMIT),
    )(dxo, x, gain, y, z, z, win_t, wout, conv_w, sinks, rope, rope, *after)


def _place():
    x, y, c = lax.axis_index("x"), lax.axis_index("y"), lax.axis_index("c")
    other_chips = [(1 - x, y), (x, 1 - y), (1 - x, 1 - y)]
    return x, y, c, other_chips


def _all_gather_rows(shards, place=(), *, name):
    n, p = len(shards), len(place)

    def body(*refs):
        srcs, place_srcs = refs[:n], refs[n:n + p]
        outs, place_outs = refs[n + p:2 * n + p], refs[2 * n + p:2 * (n + p)]
        send_sems, recv_sems, local_sems = refs[2 * (n + p):]
        x, y, c, _ = _place()
        me, sibling = (x, y, c), (x, y, 1 - c)
        relay_from = (x + (1 - c) - 2 * x * (1 - c), y + c - 2 * y * c)
        relay_to = (x + c - 2 * x * c, y + (1 - c) - 2 * y * (1 - c))
        chips = [relay_from, relay_to, (1 - x, 1 - y)]

        def rows(t, px, py, pc):
            r = srcs[t].shape[-2]
            start = pl.multiple_of((4 * px + 2 * py + pc) * r, 16 if r % 16 == 0 else 8)
            if len(srcs[t].shape) == 3:
                return outs[t].at[:, pl.ds(start, r), :]
            return outs[t].at[pl.ds(start, r), :]

        def copy(t, k, block, to, own=False):
            return pltpu.make_async_remote_copy(
                src_ref=srcs[t] if own else rows(t, *block), dst_ref=rows(t, *block),
                send_sem=send_sems.at[t, k], recv_sem=recv_sems.at[t, k], device_id=to, device_id_type=MESH)

        mine = [pltpu.make_async_copy(srcs[t], rows(t, *me), local_sems.at[t]) for t in range(n)]
        mine += [pltpu.make_async_copy(place_srcs[q],
                                       _block_rows(place_outs[q], place_srcs[q].shape[-2], 4 * x + 2 * y + c),
                                       local_sems.at[n + q]) for q in range(p)]
        for q in range(p):
            mine[n + q].start()
        first = []
        for t in range(n):
            mine[t].start()
            first.append(copy(t, 0, me, sibling, own=True))
            first += [copy(t, 1 + j, me, (*chip, c), own=True) for j, chip in enumerate(chips[:2])]
        for cp in first:
            cp.start()
        passed = []
        for j, chip in enumerate(chips):
            for t in range(n):
                copy(t, 1 + j, (*chip, c), me).wait_recv()
                if j == 0:
                    passed.append(copy(t, 3, (*chip, c), (*relay_to, c)))
                    passed[-1].start()
                passed.append(copy(t, 4 + j, (*chip, c), sibling))
                passed[-1].start()
        for t in range(n):
            copy(t, 0, sibling, me).wait_recv()
            for j, chip in enumerate([relay_to, relay_from, chips[2]]):
                copy(t, 4 + j, (*chip, 1 - c), me).wait_recv()
        for cp in first + passed:
            cp.wait_send()
        for cp in mine:
            cp.wait()

    out_shape = [jax.ShapeDtypeStruct(s.shape[:-2] + (N_DEV * s.shape[-2], s.shape[-1]), s.dtype)
                 for s in list(shards) + list(place)]
    res = pl.pallas_call(
        body, name=name, in_specs=[ANY] * (n + p), out_specs=[ANY] * (n + p), out_shape=out_shape,
        scratch_shapes=[pltpu.SemaphoreType.DMA((n, 7)), pltpu.SemaphoreType.DMA((n, 7)),
                        pltpu.SemaphoreType.DMA((n + p,))],
    )(*shards, *place)
    return res[:n], res[n:]


def _split_start(bufs, n_copies, plan, *, name, after=()):
    n = len(bufs)

    def body(*refs):
        token = refs[-1]
        for cp in plan(refs[:n], refs[n], refs[n + 1]):
            cp.start()
        token[...] = jnp.zeros_like(token)

    res = pl.pallas_call(
        _behind(body, n, after), name=name, in_specs=[HBM_SPEC] * n + [ANY] * len(after),
        out_specs=(SEM_SPEC, SEM_SPEC, *[HBM_SPEC] * n, pl.BlockSpec(memory_space=pltpu.VMEM)),
        out_shape=(pltpu.SemaphoreType.DMA((n_copies,)), pltpu.SemaphoreType.DMA((n_copies,)),
                   *[pltpu.HBM(b.shape, b.dtype) for b in bufs], jax.ShapeDtypeStruct((8, 128), F32)),
        input_output_aliases={i: 2 + i for i in range(n)},
        compiler_params=pltpu.CompilerParams(has_side_effects=DATAFLOW),
    )(*[pltpu.with_memory_space_constraint(b, pltpu.HBM) for b in bufs], *after)
    return res[0], res[1], list(res[2:2 + n]), res[-1]


def _split_wait(send_sems, recv_sems, bufs, after, plan, *, name):
    n = len(bufs)

    def body(*refs):
        for cp in plan(refs[:n], refs[n], refs[n + 1]):
            cp.wait_send()
            cp.wait_recv()

    return list(pl.pallas_call(
        body, name=name, in_specs=[HBM_SPEC] * n + [SEM_SPEC, SEM_SPEC, ANY], out_specs=[HBM_SPEC] * n,
        out_shape=tuple(pltpu.HBM(b.shape, b.dtype) for b in bufs),
        input_output_aliases={i: i for i in range(n)},
        compiler_params=pltpu.CompilerParams(has_side_effects=DATAFLOW),
    )(*bufs, send_sems, recv_sems, after))


def _sibling_plan(n, first=0, chips=slice(0, 4)):
    def plan(bufs, send_sems, recv_sems):
        x, y, c, _ = _place()
        return [_remote(bufs[t].at[chips, 1 - c], bufs[n + t].at[chips], send_sems, recv_sems, first + t, (x, y, 1 - c))
                for t in range(n)]
    return plan


def _block_rows(ref, r, blk):
    start = pl.multiple_of(blk * r, 16 if r % 16 == 0 else 8)
    return ref.at[(slice(None),) * (len(ref.shape) - 2) + (pl.ds(start, r), slice(None))]


def _remote(src, dst, send_sems, recv_sems, k, peer):
    return pltpu.make_async_remote_copy(src_ref=src, dst_ref=dst, send_sem=send_sems.at[k], recv_sem=recv_sems.at[k],
                                        device_id=peer, device_id_type=MESH)


def _gather_send_plan(n, first=0):
    def plan(bufs, send_sems, recv_sems):
        x, y, c, chips = _place()
        peers = [(x, y, 1 - c)] + [(px, py, c) for px, py in chips]
        copies = []
        for t in range(n):
            dst = _block_rows(bufs[n + t], bufs[t].shape[-2], 4 * x + 2 * y + c)
            copies += [_remote(bufs[t], dst, send_sems, recv_sems, 4 * (first + t) + k, peer)
                       for k, peer in enumerate(peers)]
        return copies
    return plan


def _gather_forward_plan(rows):
    def plan(bufs, send_sems, recv_sems):
        x, y, c, chips = _place()
        copies = []
        for t, r in enumerate(rows):
            for j, (px, py) in enumerate(chips):
                blk = _block_rows(bufs[t], r, 4 * px + 2 * py + c)
                copies.append(_remote(blk, blk, send_sems, recv_sems, 3 * t + j, (x, y, 1 - c)))
        return copies
    return plan


def _chips_plan(n, with_small):
    def plan(bufs, send_sems, recv_sems):
        x, y, c, chips = _place()
        copies = []
        for t in range(n):
            for j, (px, py) in enumerate(chips):
                copies.append(_remote(bufs[t].at[2 * px + py], bufs[n + t].at[j], send_sems, recv_sems, 3 * t + j,
                                      (px, py, c)))
        if with_small:
            mine = _block_rows(bufs[2 * n], 8, 4 * x + 2 * y + c)
            flips = [(fx, fy, fc) for fx in range(2) for fy in range(2) for fc in range(2)][1:]
            for k, (fx, fy, fc) in enumerate(flips):
                peer = (x + fx - 2 * x * fx, y + fy - 2 * y * fy, c + fc - 2 * c * fc)
                copies.append(_remote(mine, mine, send_sems, recv_sems, 3 * n + k, peer))
        return copies
    return plan


def _place_own(fulls, shards, index, *, name):
    n = len(fulls)

    def body(index_ref, *refs):
        for t in range(n):
            refs[2 * n + t][...] = refs[n + t][...]

    def block_of(shard):
        lead = len(shard.shape) - 2
        return pl.BlockSpec(shard.shape, lambda i, index_ref: (0,) * lead + (index_ref[0], 0))

    def whole(shard):
        return pl.BlockSpec(shard.shape, lambda i, index_ref: (0,) * len(shard.shape))

    return list(pl.pallas_call(
        body, name=name,
        grid_spec=pltpu.PrefetchScalarGridSpec(
            num_scalar_prefetch=1, grid=(1,),
            in_specs=[ANY] * n + [whole(s) for s in shards], out_specs=[block_of(s) for s in shards]),
        out_shape=[jax.ShapeDtypeStruct(f.shape, f.dtype) for f in fulls],
        input_output_aliases={1 + t: t for t in range(n)},
        compiler_params=_params(("arbitrary",)),
    )(index, *fulls, *shards))


N_STEPS_SMALL = 2


def _add_sibling(grads, recvs, place, *, name):
    n = len(grads)

    def body(place_ref, *refs):
        chip = place_ref[1]
        for t in range(n):
            g_ref, r_ref, own_ref, ob_ref = refs[2 * t], refs[2 * t + 1], refs[2 * n + 2 * t], refs[2 * n + 2 * t + 1]
            own = jnp.zeros(own_ref.shape, F32)
            for m in range(4):
                p = g_ref[m, 0] + r_ref[m]
                ob_ref[m] = p.astype(BF16)
                own = jnp.where(chip == m, p, own)
            own_ref[...] = own

    in_specs, out_specs, out_shape = [], [], []
    for g, r in zip(grads, recvs):
        tr = g.shape[2] // N_STEPS_SMALL
        blocks = pl.BlockSpec((4, tr, D_MODEL), lambda i, place_ref: (0, i, 0))
        in_specs += [pl.BlockSpec((4, 1, tr, D_MODEL), lambda i, place_ref: (0, place_ref[0], i, 0)), blocks]
        out_specs += [pl.BlockSpec((tr, D_MODEL), lambda i, place_ref: (i, 0)), blocks]
        out_shape += [jax.ShapeDtypeStruct(r.shape[1:], F32), jax.ShapeDtypeStruct(r.shape, BF16)]
    res = pl.pallas_call(
        body, name=name,
        grid_spec=pltpu.PrefetchScalarGridSpec(num_scalar_prefetch=1, grid=(N_STEPS_SMALL,), in_specs=in_specs,
                                               out_specs=out_specs),
        out_shape=out_shape, compiler_params=_params(("arbitrary",), VMEM_LIMIT),
    )(place, *[a for pair in zip(grads, recvs) for a in pair])
    return [(res[2 * t], res[2 * t + 1]) for t in range(n)]


def _reduce_adamw(parts, *, name, after=()):
    n = len(parts)

    def body(*refs):
        for t in range(n):
            p_ref, r_ref, w_ref, m_ref, v_ref = refs[5 * t:5 * t + 5]
            g_ref, d_ref, mo_ref, vo_ref = refs[5 * n + 4 * t:5 * n + 4 * t + 4]
            g = p_ref[...] + r_ref[0].astype(F32) + r_ref[1].astype(F32) + r_ref[2].astype(F32)
            g_ref[...] = g
            d_ref[...], mo_ref[...], vo_ref[...] = _adamw_math(w_ref[...], g, m_ref[...], v_ref[...])
        refs[-1][...] = jnp.zeros_like(refs[-1])

    in_specs, out_specs, out_shape = [], [], []
    for own, _, _, _, _ in parts:
        rows = own.shape[0]
        tr = rows // N_STEPS_SMALL
        spec = pl.BlockSpec((tr, D_MODEL), lambda i: (i, 0))
        in_specs += [spec, pl.BlockSpec((3, tr, D_MODEL), lambda i: (0, i, 0)), spec, spec, spec]
        out_specs += [spec] * 4
        out_shape += [jax.ShapeDtypeStruct((rows, D_MODEL), F32)] * 4
    res = pl.pallas_call(
        _behind(body, 5 * n, after), name=name, grid=(N_STEPS_SMALL,),
        in_specs=in_specs + [ANY] * len(after),
        out_specs=out_specs + [pl.BlockSpec((8, 128), lambda i: (0, 0))],
        out_shape=out_shape + [jax.ShapeDtypeStruct((8, 128), F32)],
        compiler_params=_params(("arbitrary",), VMEM_LIMIT),
    )(*[a for part in parts for a in part], *after)
    return [tuple(res[4 * t:4 * t + 4]) for t in range(n)], res[-1]


def _adamw_math(w, g, m, v):
    m = ADAM_B1 * m + (1.0 - ADAM_B1) * g
    v = ADAM_B2 * v + (1.0 - ADAM_B2) * (g * g)
    m_hat = m / (1.0 - ADAM_B1 ** ADAM_STEP)
    v_hat = v / (1.0 - ADAM_B2 ** ADAM_STEP)
    delta = -ADAM_LR * (m_hat / (jnp.sqrt(v_hat) + ADAM_EPS) + ADAM_WD * w)
    return delta, m, v


SMALL_NAMES = ["ffn1_norm", "mix_norm", "ffn2_norm", "final_norm", "conv_w", "attn_sinks"]


def _update_small(given, moments_m, moments_v, small_all, my_index, *, name):
    conv_cols = given["conv_w"].shape[2]
    per_block = 128 // conv_cols

    def two_d(nm, a):
        return a.reshape(1, D_MODEL) if nm == "final_norm" else a

    operands = [two_d(nm, src[nm]) for nm in SMALL_NAMES for src in (given, moments_m, moments_v)]
    n = len(SMALL_NAMES)

    def body(index_ref, all_ref, conv_ref, *refs):
        ins, outs = refs[:3 * n], refs[3 * n:]
        total, conv_total = all_ref[0], conv_ref[0]
        for k in range(1, N_DEV):
            total, conv_total = total + all_ref[k], conv_total + conv_ref[k]
        which = index_ref[0] % per_block
        conv_g = conv_total[4:7, :conv_cols]
        for j in range(1, per_block):
            conv_g = jnp.where(which == j, conv_total[4:7, j * conv_cols:(j + 1) * conv_cols], conv_g)
        grads = [total[0:1], total[1:2], total[2:3], total[3:4], conv_g[None], total[7:8, :N_Q_HEADS]]
        for t, g in enumerate(grads):
            w_ref, m_ref, v_ref = ins[3 * t:3 * t + 3]
            g_ref, d_ref, mo_ref, vo_ref = outs[4 * t:4 * t + 4]
            g_ref[...] = g
            d_ref[...], mo_ref[...], vo_ref[...] = _adamw_math(w_ref[...], g, m_ref[...], v_ref[...])
        outs[-1][...] = total[7:8, LOSS_LANE:LOSS_LANE + 1]

    def whole(shape):
        return pl.BlockSpec(shape, lambda i, index_ref: (0,) * len(shape))

    shapes = [a.shape for a in operands[::3] for _ in range(4)] + [(1, 1)]
    res = pl.pallas_call(
        body, name=name,
        grid_spec=pltpu.PrefetchScalarGridSpec(
            num_scalar_prefetch=1, grid=(1,),
            in_specs=[whole(small_all.shape),
                      pl.BlockSpec((N_DEV, 8, 128), lambda i, index_ref: (0, 0, index_ref[0] // per_block))]
            + [whole(a.shape) for a in operands],
            out_specs=[whole(s) for s in shapes]),
        out_shape=[jax.ShapeDtypeStruct(s, F32) for s in shapes],
        compiler_params=_params(("arbitrary",)),
    )(my_index.astype(jnp.int32).reshape(1), small_all, small_all, *operands)
    results = {nm: tuple(a.reshape(given[nm].shape) for a in res[4 * t:4 * t + 4]) for t, nm in enumerate(SMALL_NAMES)}
    return results, res[-1]


def kernel(x, ffn1_norm, ffn1_w_gate, ffn1_w_up, ffn1_w_down, mix_norm, w_in, conv_w, attn_sinks, w_out, ffn2_norm, ffn2_w_gate, ffn2_w_up, ffn2_w_down, final_norm, loss_target, m_ffn1_norm, m_ffn1_w_gate, m_ffn1_w_up, m_ffn1_w_down, m_mix_norm, m_w_in, m_conv_w, m_attn_sinks, m_w_out, m_ffn2_norm, m_ffn2_w_gate, m_ffn2_w_up, m_ffn2_w_down, m_final_norm, v_ffn1_norm, v_ffn1_w_gate, v_ffn1_w_up, v_ffn1_w_down, v_mix_norm, v_w_in, v_conv_w, v_attn_sinks, v_w_out, v_ffn2_norm, v_ffn2_w_gate, v_ffn2_w_up, v_ffn2_w_down, v_final_norm):
    ix, iy, ic = lax.axis_index("x"), lax.axis_index("y"), lax.axis_index("c")
    my_index = 4 * ix + 2 * iy + ic
    place = jnp.stack([ic, 2 * ix + iy]).astype(jnp.int32)

    given = dict(ffn1_norm=ffn1_norm, ffn1_w_gate=ffn1_w_gate, ffn1_w_up=ffn1_w_up, ffn1_w_down=ffn1_w_down,
                 mix_norm=mix_norm, w_in=w_in, conv_w=conv_w, attn_sinks=attn_sinks, w_out=w_out, ffn2_norm=ffn2_norm,
                 ffn2_w_gate=ffn2_w_gate, ffn2_w_up=ffn2_w_up, ffn2_w_down=ffn2_w_down, final_norm=final_norm)
    moments_m = dict(ffn1_norm=m_ffn1_norm, ffn1_w_gate=m_ffn1_w_gate, ffn1_w_up=m_ffn1_w_up, ffn1_w_down=m_ffn1_w_down,
                     mix_norm=m_mix_norm, w_in=m_w_in, conv_w=m_conv_w, attn_sinks=m_attn_sinks, w_out=m_w_out,
                     ffn2_norm=m_ffn2_norm, ffn2_w_gate=m_ffn2_w_gate, ffn2_w_up=m_ffn2_w_up, ffn2_w_down=m_ffn2_w_down,
                     final_norm=m_final_norm)
    moments_v = dict(ffn1_norm=v_ffn1_norm, ffn1_w_gate=v_ffn1_w_gate, ffn1_w_up=v_ffn1_w_up, ffn1_w_down=v_ffn1_w_down,
                     mix_norm=v_mix_norm, w_in=v_w_in, conv_w=v_conv_w, attn_sinks=v_attn_sinks, w_out=v_w_out,
                     ffn2_norm=v_ffn2_norm, ffn2_w_gate=v_ffn2_w_gate, ffn2_w_up=v_ffn2_w_up, ffn2_w_down=v_ffn2_w_down,
                     final_norm=v_final_norm)

    xs = x[0]
    target = loss_target[0]
    final_gain = final_norm.reshape(1, D_MODEL)

    def ffn_shards(wg, wu, wd):
        return jnp.stack([wg[0].T, wu[0].T]).astype(BF16), wd[0].astype(BF16)

    conv_cols = conv_w.shape[2]
    conv_shard = jnp.pad(conv_w[0], ((0, 5), (0, 128 - conv_cols)))
    gate_up1, down1 = ffn_shards(ffn1_w_gate, ffn1_w_up, ffn1_w_down)
    rest_shards = [down1, w_in[0].T.astype(BF16), w_out[0].astype(BF16), conv_shard,
                   *ffn_shards(ffn2_w_gate, ffn2_w_up, ffn2_w_down)]
    rest_rows = [s.shape[-2] for s in rest_shards]
    n_rest, n_early = len(rest_shards), 4
    (w1_gu,), _ = _all_gather_rows([gate_up1], name="gather_ffn1")

    fulls = [lax.empty(s.shape[:-2] + (N_DEV * s.shape[-2], s.shape[-1]), s.dtype) for s in rest_shards]
    fulls = _place_own(fulls, rest_shards, my_index.astype(jnp.int32).reshape(1), name="place_own_weights")
    ssem, rsem, bufs, token = _split_start(rest_shards + list(fulls), 4 * n_rest, _gather_send_plan(n_rest),
                                           name="gather_rest_start", after=[w1_gu])
    early = bufs[:n_early] + bufs[n_rest:n_rest + n_early]
    late = bufs[n_early:n_rest] + bufs[n_rest + n_early:]
    h1, s1, sa1, sb1 = _ffn_fwd(xs, ffn1_norm, w1_gu, name="ffn1_hidden", after=[token])
    early = _split_wait(ssem, rsem, early, h1, _gather_send_plan(n_early), name="gather_early_wait")
    fwd_early = _gather_forward_plan(rest_rows[:n_early])
    ssem_e, rsem_e, parts, token = _split_start(early[n_early:], 3 * n_early, fwd_early, name="forward_early_start")
    rope = _rope_tables(xs.shape[0], after=[token])
    w1_d, win_t, wout, conv_all = _split_wait(ssem_e, rsem_e, parts, rope, fwd_early, name="forward_early_wait")
    x1 = _ffn_down(xs, s1, w1_d, name="ffn1_down")
    conv_full = conv_all.reshape(N_DEV, 8, 128)[:, :3, :conv_cols].transpose(1, 0, 2).reshape(3, CONV_W)
    late = _split_wait(ssem, rsem, late, x1, _gather_send_plan(n_rest - n_early, first=n_early),
                       name="gather_late_wait")
    fwd_ffn2 = _gather_forward_plan(rest_rows[n_early:])
    ssem, rsem, parts, token = _split_start(late[n_rest - n_early:], 3 * (n_rest - n_early), fwd_ffn2,
                                            name="forward_ffn2_start")
    x2, hm, z, y = _mixer_fwd(x1, mix_norm, win_t, wout, conv_full, attn_sinks, rope, name="mixer_fwd", after=[token])
    w2_gu, w2_d = _split_wait(ssem, rsem, parts, x2, fwd_ffn2, name="forward_ffn2_wait")
    dx3, h2, s2, sa2, sb2, loss_local, d_final = _ffn_fwd(x2, ffn2_norm, w2_gu, w2_d, head=(final_gain, target),
                                                          name="ffn2_fwd")

    def sibling_bufs(grads):
        views = [g.reshape(4, 2, g.shape[0] // N_DEV, D_MODEL) for g in grads]
        return views + [lax.empty((4,) + v.shape[2:], F32) for v in views]

    def chips_bufs(partials, small_all):
        p16 = [p for _, p in partials]
        return p16 + [lax.empty((3,) + p.shape[1:], BF16) for p in p16] + ([] if small_all is None else [small_all])

    def to_sibling_start(grads, tag, after=()):
        plan = _sibling_plan(len(grads))
        ssem, rsem, bufs, token = _split_start(sibling_bufs(grads), len(grads), plan, name=f"{tag}_sibling_start",
                                               after=after)
        return (ssem, rsem, bufs, plan, tag), token

    def to_chips_and_sibling_start(partials, tag, grads, sibling_tag, after=(), chips=slice(0, 4)):
        for_chips, n_ici = chips_bufs(partials, None), 3 * len(partials)
        chips_plan = _chips_plan(len(partials), False)
        sibling_plan = _sibling_plan(len(grads), first=n_ici, chips=chips)

        def both(bufs, send_sems, recv_sems):
            return (chips_plan(bufs[:len(for_chips)], send_sems, recv_sems)
                    + sibling_plan(bufs[len(for_chips):], send_sems, recv_sems))

        ssem, rsem, bufs, token = _split_start(for_chips + sibling_bufs(grads), n_ici + len(grads), both,
                                               name=f"{tag}_chips_{sibling_tag}_sibling_start", after=after)
        return ((ssem, rsem, bufs[:len(for_chips)], chips_plan, tag),
                (ssem, rsem, bufs[len(for_chips):], sibling_plan, sibling_tag), token)

    def to_sibling_finish(handle, after, names):
        ssem, rsem, bufs, plan, tag = handle
        bufs = _split_wait(ssem, rsem, bufs, after, plan, name=f"{tag}_sibling_wait")
        n = len(names)
        return _add_sibling(bufs[:n], bufs[n:], place, name=f"add_sibling_{tag}")

    def to_chips_start(partials, tag, small_all=None, after=()):
        plan = _chips_plan(len(partials), small_all is not None)
        n_copies = 3 * len(partials) + (0 if small_all is None else N_DEV - 1)
        ssem, rsem, bufs, token = _split_start(chips_bufs(partials, small_all), n_copies, plan,
                                               name=f"{tag}_chips_start", after=after)
        return (ssem, rsem, bufs, plan, tag), token

    def to_chips_finish(handle, partials, after, names):
        ssem, rsem, bufs, plan, tag = handle
        bufs = _split_wait(ssem, rsem, bufs, after, plan, name=f"{tag}_chips_wait")
        n = len(names)
        return [(p32, r) for (p32, _), r in zip(partials, bufs[n:2 * n])], bufs[2 * n:]

    half_ff = D_FF // 2
    names2, namesm = ["ffn2_w_gate", "ffn2_w_up", "ffn2_w_down"], ["w_in", "w_out"]
    transposed = {"ffn1_w_gate", "ffn1_w_up", "w_in", "ffn2_w_gate", "ffn2_w_up"}
    grad, delta, new_m, new_v = {}, {}, {}, {}

    def adam_big(names, parts, tag, after=()):
        def to_rows(nm, a):
            return jnp.swapaxes(a, 1, 2)[0] if nm in transposed else a[0]

        def from_rows(nm, a):
            return jnp.swapaxes(a[None], 1, 2) if nm in transposed else a[None]

        operands = [(p32, recv, to_rows(nm, given[nm]), to_rows(nm, moments_m[nm]), to_rows(nm, moments_v[nm]))
                    for nm, (p32, recv) in zip(names, parts)]
        results, token = _reduce_adamw(operands, name=f"adamw_{tag}", after=after)
        for nm, outs in zip(names, results):
            grad[nm], delta[nm], new_m[nm], new_v[nm] = (from_rows(nm, a) for a in outs)
        return token

    dx2, da2, db2, g2b, d_norm2 = _ffn_dgrad(dx3, x2, ffn2_norm, sa2, sb2, w2_gu, w2_d, name="ffn2_dgrad")
    gw2 = [_tn_matmul(da2, h2, name="ffn2_wgrad_gate", bm=half_ff), _tn_matmul(db2, h2, name="ffn2_wgrad_up", bm=half_ff),
           _tn_matmul(s2, g2b, name="ffn2_wgrad_down", bm=half_ff)]
    sib2, tok = to_sibling_start(gw2, "ffn2")
    dx1, dz, gmb, d_conv, d_sink, d_normm = _mixer_bwd(dx2, x1, mix_norm, y, z, win_t, wout, conv_full, attn_sinks,
                                                       rope, name="mixer_bwd", after=[tok])
    gwm = [_tn_matmul(dz, hm, name="mixer_wgrad_in", bm=Z_W // 2), _tn_matmul(y, gmb, name="mixer_wgrad_out", bm=D_MODEL)]
    p2 = to_sibling_finish(sib2, gwm[1], names2)
    chips2, sibm, tok = to_chips_and_sibling_start(p2, "ffn2", gwm, "mixer")
    dx0, da1, db1, g1b, d_norm1 = _ffn_dgrad(dx1, xs, ffn1_norm, sa1, sb1, w1_gu, w1_d, name="ffn1_dgrad", after=[tok])
    r2, _ = to_chips_finish(chips2, p2, dx0, names2)
    pm = to_sibling_finish(sibm, dx0, namesm)
    gw_gate = _tn_matmul(da1, h1, name="ffn1_wgrad_gate", bm=half_ff)
    chipsm, sib_gate, tok = to_chips_and_sibling_start(pm, "mixer", [gw_gate], "ffn1_gate")
    gw_up = _tn_matmul(db1, h1, name="ffn1_wgrad_up", bm=half_ff, after=[tok])
    rm, _ = to_chips_finish(chipsm, pm, gw_up, namesm)
    p_gate = to_sibling_finish(sib_gate, gw_up, ["ffn1_w_gate"])
    chips_gate, sib_up, tok = to_chips_and_sibling_start(p_gate, "ffn1_gate", [gw_up], "ffn1_up")
    gw_down = _tn_matmul(s1, g1b, name="ffn1_wgrad_down_first", bm=half_ff, blocks=(0, 1), after=[tok])
    p_up = to_sibling_finish(sib_up, gw_down, ["ffn1_w_up"])
    chips_up, sib_first, tok = to_chips_and_sibling_start(p_up, "ffn1_up", [gw_down], "ffn1_down_first",
                                                          chips=slice(0, 2))
    view, land = sib_first[2]
    gw_down = _tn_matmul(s1, g1b, name="ffn1_wgrad_down_second", bm=half_ff, blocks=(1, 1),
                         into=view.reshape(D_FF, D_MODEL), after=[tok])
    rest = _sibling_plan(1, chips=slice(2, 4))
    ssem, rsem, bufs, tok = _split_start([gw_down.reshape(view.shape), land], 1, rest,
                                         name="ffn1_down_second_sibling_start")
    bufs = _split_wait(sib_first[0], sib_first[1], bufs, tok, sib_first[3], name="ffn1_down_first_sibling_wait")
    bufs = _split_wait(ssem, rsem, bufs, tok, rest, name="ffn1_down_second_sibling_wait")
    p_down = _add_sibling(bufs[:1], bufs[1:], place, name="add_sibling_ffn1_down")
    last_row = (jnp.pad(d_sink, ((0, 0), (0, D_MODEL - 128)))
                + jnp.pad(loss_local, ((0, 0), (LOSS_LANE, D_MODEL - LOSS_LANE - 1))))
    small = jnp.concatenate([
        d_norm1, d_normm, d_norm2, d_final, jnp.pad(d_conv[0:3], ((0, 0), (0, D_MODEL - CONV_W))), last_row], axis=0)
    (small_all,) = _place_own([lax.empty((N_DEV * 8, D_MODEL), F32)], [small], my_index.astype(jnp.int32).reshape(1),
                              name="place_own_small")
    chips_down, tok = to_chips_start(p_down, "ffn1_down", small_all)
    tok = adam_big(names2, r2, "ffn2", after=[tok])
    tok = adam_big(namesm, rm, "mixer", after=[tok])
    r_gate, _ = to_chips_finish(chips_gate, p_gate, tok, ["ffn1_w_gate"])
    tok = adam_big(["ffn1_w_gate"], r_gate, "ffn1_gate")
    r_up, _ = to_chips_finish(chips_up, p_up, tok, ["ffn1_w_up"])
    tok = adam_big(["ffn1_w_up"], r_up, "ffn1_up")
    r_down, (small_all,) = to_chips_finish(chips_down, p_down, tok, ["ffn1_w_down"])
    adam_big(["ffn1_w_down"], r_down, "ffn1_down")
    results, loss = _update_small(given, moments_m, moments_v, small_all.reshape(N_DEV, 8, D_MODEL), my_index,
                                  name="update_small")
    for nm, outs in results.items():
        grad[nm], delta[nm], new_m[nm], new_v[nm] = outs

    order = list(given)
    return (loss.reshape(()), dx0[None], *[grad[n] for n in order], *[delta[n] for n in order],
            *[new_m[n] for n in order], *[new_v[n] for n in order])
```

```python
import functools

import jax
import jax.numpy as jnp
from jax import lax
from jax.experimental import pallas as pl
from jax.experimental.pallas import tpu as pltpu

F32 = jnp.float32
BF16 = jnp.bfloat16
MESH = pl.DeviceIdType.MESH
ANY = pl.BlockSpec(memory_space=pl.ANY)
HBM_SPEC = pl.BlockSpec(memory_space=pltpu.HBM)
SEM_SPEC = pl.BlockSpec(memory_space=pltpu.SEMAPHORE)
DATAFLOW = pltpu.SideEffectType.DATAFLOW_SIDE_EFFECTING

N_DEV = 8
LOSS_LANE = 128
D_MODEL = 1024
D_FF = 2816
CONV_W = 512
ATTN_W = 512
KV_W = 128
HEAD_DIM = 64
N_Q_HEADS = 8
N_KV_HEADS = 2
Q_PER_KV = N_Q_HEADS // N_KV_HEADS
BLOCK = 128
ROT_DIM = 16
ROPE_THETA = 500000.0
Z_W = 3 * CONV_W + ATTN_W + 2 * KV_W
Q_OFF = 3 * CONV_W
K_OFF = Q_OFF + ATTN_W
V_OFF = K_OFF + KV_W
RMS_EPS = 1e-5
MASK_VALUE = -1e30
SM_SCALE = HEAD_DIM ** -0.5
FFN_RES_SCALE = 0.5

ADAM_LR = 0.001
ADAM_B1 = 0.9
ADAM_B2 = 0.999
ADAM_EPS = 1e-08
ADAM_WD = 0.01
ADAM_STEP = 10

NT_DIMS = (((1,), (1,)), ((), ()))
TN_DIMS = (((0,), (0,)), ((), ()))

VMEM_LIMIT = 62 * 1024 * 1024
FF_CHUNK = 256


def _params(sem, vmem=None):
    return pltpu.CompilerParams(dimension_semantics=sem, vmem_limit_bytes=vmem)


def _behind(body, n_in, after):
    k = len(after)
    if k == 0:
        return body
    return lambda *refs: body(*refs[:n_in], *refs[n_in + k:])


def _rms_stats(xf):
    inv = lax.rsqrt(jnp.mean(xf * xf, axis=-1, keepdims=True) + RMS_EPS)
    return xf * inv, inv


def _rms_bwd(dh, xhat, inv, gain):
    dxhat = dh * gain
    dx = inv * (dxhat - xhat * jnp.mean(dxhat * xhat, axis=-1, keepdims=True))
    dgain = jnp.sum(dh * xhat, axis=0, keepdims=True)
    return dx, dgain


LOAD_PIECES = 4


def _load_resident(pairs, sems):
    @pl.when(pl.program_id(0) == 0)
    def _():
        copies = []
        for k, (w_hbm, w_ref) in enumerate(pairs):
            rows = w_hbm.shape[-2] // LOAD_PIECES
            for p in range(LOAD_PIECES):
                piece = (slice(None),) * (len(w_hbm.shape) - 2) + (pl.ds(p * rows, rows), slice(None))
                copies.append(pltpu.make_async_copy(w_hbm.at[piece], w_ref.at[piece], sems.at[k, p]))
        for cp in copies:
            cp.start()
        for cp in copies:
            cp.wait()


def _ffn_fwd(x, gain, w_gu, w_d=None, *, name, head=None, after=(), tm=256, sub=256, tf=FF_CHUNK):
    t = x.shape[0]
    tm = min(tm, t)
    sub = min(sub, tm)
    n_down = 0 if w_d is None else 1
    n_head = 0 if head is None else 2
    assert n_down or not n_head
    n_in = 3 + n_down + n_head

    def body(*refs):
        x_ref, g_ref = refs[:2]
        w_hbms, head_refs = refs[2:3 + n_down], refs[3 + n_down:n_in]
        xo_refs = refs[n_in:n_in + n_down]
        h_ref, s_ref, sa_ref, sb_ref = refs[n_in + n_down:n_in + n_down + 4]
        head_outs = refs[n_in + n_down + 4:n_in + n_down + 4 + n_head]
        w_refs, sems = refs[n_in + n_down + 4 + n_head:-1], refs[-1]
        _load_resident(list(zip(w_hbms, w_refs)), sems)

        @pl.when(pl.program_id(0) == 0)
        def _():
            for ref in head_outs:
                ref[...] = jnp.zeros_like(ref)

        for r0 in range(0, tm, sub):
            rows = slice(r0, r0 + sub)
            xf = x_ref[rows, :]
            xhat, _ = _rms_stats(xf)
            h = (xhat * g_ref[...]).astype(BF16)
            h_ref[rows, :] = h
            for c in range(0, D_FF, tf):
                cols = slice(c, min(c + tf, D_FF))
                a = lax.dot_general(h, w_refs[0][0, cols, :], NT_DIMS, preferred_element_type=F32)
                b = lax.dot_general(h, w_refs[0][1, cols, :], NT_DIMS, preferred_element_type=F32)
                sig = jax.nn.sigmoid(a)
                silu = a * sig
                s_ref[rows, cols] = (silu * b).astype(BF16)
                sa_ref[rows, cols] = (b * (sig * (1.0 + a * (1.0 - sig)))).astype(BF16)
                sb_ref[rows, cols] = silu.astype(BF16)
            if not n_down:
                continue
            xo = xf + FFN_RES_SCALE * jnp.dot(s_ref[rows, :], w_refs[1][...], preferred_element_type=F32)
            if head is None:
                xo_refs[0][rows, :] = xo
            else:
                fg_ref, t_ref = head_refs
                loss_ref, dfg_ref = head_outs
                xhat_o, inv_o = _rms_stats(xo)
                err = xhat_o * fg_ref[...] - t_ref[rows, :]
                loss_ref[...] += 0.5 * jnp.sum(jnp.mean(err * err, axis=-1, keepdims=True), axis=0, keepdims=True)
                xo_refs[0][rows, :], dfg = _rms_bwd(err * (1.0 / D_MODEL), xhat_o, inv_o, fg_ref[...])
                dfg_ref[...] += dfg

    row = pl.BlockSpec((tm, D_MODEL), lambda i: (i, 0))
    hid = pl.BlockSpec((tm, D_FF), lambda i: (i, 0))
    vec = pl.BlockSpec((1, D_MODEL), lambda i: (0, 0))
    head_in = [] if head is None else [vec, row]
    head_out = [] if head is None else [pl.BlockSpec((1, 1), lambda i: (0, 0)), vec]
    head_shape = [] if head is None else [jax.ShapeDtypeStruct((1, 1), F32), jax.ShapeDtypeStruct((1, D_MODEL), F32)]
    return pl.pallas_call(
        _behind(body, n_in, after), name=name, grid=(t // tm,),
        in_specs=[row, vec] + [ANY] * (1 + n_down) + head_in + [ANY] * len(after),
        out_specs=[row] * (n_down + 1) + [hid, hid, hid] + head_out,
        out_shape=[jax.ShapeDtypeStruct((t, D_MODEL), F32)] * n_down + [jax.ShapeDtypeStruct((t, D_MODEL), BF16)]
        + [jax.ShapeDtypeStruct((t, D_FF), BF16)] * 3 + head_shape,
        scratch_shapes=[pltpu.VMEM((2, D_FF, D_MODEL), BF16)] + [pltpu.VMEM((D_FF, D_MODEL), BF16)] * n_down
        + [pltpu.SemaphoreType.DMA((2, LOAD_PIECES))],
        compiler_params=_params(("arbitrary",), VMEM_LIMIT),
    )(x, gain, w_gu, *([] if w_d is None else [w_d]), *(head or ()), *after)


def _ffn_down(x, s, w_d, *, name, after=(), tm=512):
    t = x.shape[0]
    tm = min(tm, t)

    def body(x_ref, s_ref, w_hbm, xo_ref, w_ref, sem):
        _load_resident([(w_hbm, w_ref)], sem)
        xo_ref[...] = x_ref[...] + FFN_RES_SCALE * jnp.dot(s_ref[...], w_ref[...], preferred_element_type=F32)

    row = pl.BlockSpec((tm, D_MODEL), lambda i: (i, 0))
    return pl.pallas_call(
        _behind(body, 3, after), name=name, grid=(t // tm,),
        in_specs=[row, pl.BlockSpec((tm, D_FF), lambda i: (i, 0)), ANY] + [ANY] * len(after), out_specs=row,
        out_shape=jax.ShapeDtypeStruct((t, D_MODEL), F32),
        scratch_shapes=[pltpu.VMEM((D_FF, D_MODEL), BF16), pltpu.SemaphoreType.DMA((1, LOAD_PIECES))],
        compiler_params=_params(("arbitrary",), VMEM_LIMIT),
    )(x, s, w_d, *after)


def _ffn_dgrad(dxo, x, gain, sa, sb, w_gu, w_d, *, name, after=(), tm=512, sub=512, tf=FF_CHUNK):
    t = x.shape[0]
    tm = min(tm, t)
    sub = min(sub, tm)

    def body(dxo_ref, x_ref, g_ref, sa_ref, sb_ref, wgu_hbm, wd_hbm, dxi_ref, da_ref, db_ref, gb_ref, dg_ref,
             wgu_ref, wd_ref, sems):
        _load_resident([(wd_hbm, wd_ref), (wgu_hbm, wgu_ref)], sems)

        @pl.when(pl.program_id(0) == 0)
        def _():
            dg_ref[...] = jnp.zeros_like(dg_ref)

        for r0 in range(0, tm, sub):
            rows = slice(r0, r0 + sub)
            go = dxo_ref[rows, :]
            gb = (FFN_RES_SCALE * go).astype(BF16)
            gb_ref[rows, :] = gb
            for c in range(0, D_FF, tf):
                cols = slice(c, min(c + tf, D_FF))
                ds = lax.dot_general(gb, wd_ref[cols, :], NT_DIMS, preferred_element_type=F32)
                da_ref[rows, cols] = (ds * sa_ref[rows, cols].astype(F32)).astype(BF16)
                db_ref[rows, cols] = (ds * sb_ref[rows, cols].astype(F32)).astype(BF16)
            dh = (jnp.dot(da_ref[rows, :], wgu_ref[0], preferred_element_type=F32)
                  + jnp.dot(db_ref[rows, :], wgu_ref[1], preferred_element_type=F32))
            xhat, inv = _rms_stats(x_ref[rows, :])
            dx, dgain = _rms_bwd(dh, xhat, inv, g_ref[...])
            dxi_ref[rows, :] = go + dx
            dg_ref[...] += dgain

    row = pl.BlockSpec((tm, D_MODEL), lambda i: (i, 0))
    hid = pl.BlockSpec((tm, D_FF), lambda i: (i, 0))
    vec = pl.BlockSpec((1, D_MODEL), lambda i: (0, 0))
    return pl.pallas_call(
        _behind(body, 7, after), name=name, grid=(t // tm,),
        in_specs=[row, row, vec, hid, hid, ANY, ANY] + [ANY] * len(after),
        out_specs=[row, hid, hid, row, vec],
        out_shape=[jax.ShapeDtypeStruct((t, D_MODEL), F32), jax.ShapeDtypeStruct((t, D_FF), BF16),
                   jax.ShapeDtypeStruct((t, D_FF), BF16),
                   jax.ShapeDtypeStruct((t, D_MODEL), BF16), jax.ShapeDtypeStruct((1, D_MODEL), F32)],
        scratch_shapes=[pltpu.VMEM((2, D_FF, D_MODEL), BF16), pltpu.VMEM((D_FF, D_MODEL), BF16),
                        pltpu.SemaphoreType.DMA((2, LOAD_PIECES))],
        compiler_params=_params(("arbitrary",), VMEM_LIMIT),
    )(dxo, x, gain, sa, sb, w_gu, w_d, *after)


def _tn_matmul(a, b, *, name, bm, after=(), tk=2048, blocks=None, into=None):
    t, m = a.shape
    n = b.shape[1]
    tk = min(tk, t)
    nk = t // tk
    first, count = blocks or (0, m // bm)
    behind = ([] if into is None else [into]) + list(after)

    def body(a_ref, b_ref, o_ref):
        @pl.when(pl.program_id(1) == 0)
        def _():
            o_ref[...] = jnp.zeros_like(o_ref)

        o_ref[...] += lax.dot_general(a_ref[...], b_ref[...], TN_DIMS, preferred_element_type=F32)

    return pl.pallas_call(
        _behind(body, 2, behind), name=name, grid=(count, nk),
        in_specs=[pl.BlockSpec((tk, bm), lambda i, k: (k, first + i)), pl.BlockSpec((tk, n), lambda i, k: (k, 0))]
        + [ANY] * len(behind),
        out_specs=pl.BlockSpec((bm, n), lambda i, k: (first + i, 0)),
        out_shape=jax.ShapeDtypeStruct((m, n), F32),
        input_output_aliases={} if into is None else {2: 0},
        compiler_params=_params(("parallel", "arbitrary"), VMEM_LIMIT),
    )(a, b, *behind)


def _rope_tables(t, after=()):
    inv_freq = ROPE_THETA ** (-jnp.arange(0, ROT_DIM, 2, dtype=F32) / ROT_DIM)
    ang = inv_freq[:, None] * jnp.arange(t, dtype=F32)[None, :]
    compact = jnp.stack([jnp.cos(ang), jnp.sin(ang)])
    tr = min(1024, t)

    def body(c_ref, o_ref):
        for k in range(2):
            o_ref[k] = jnp.tile(c_ref[k], (128 // inv_freq.shape[0], 1)).T

    return pl.pallas_call(
        _behind(body, 1, after), name="rope_tables", grid=(t // tr,),
        in_specs=[pl.BlockSpec((2, inv_freq.shape[0], tr), lambda i: (0, 0, i))] + [ANY] * len(after),
        out_specs=pl.BlockSpec((2, tr, 128), lambda i: (0, i, 0)),
        out_shape=jax.ShapeDtypeStruct((2, t, 128), F32), compiler_params=_params(("parallel",)),
    )(compact, *after)


def _rope_multipliers(cos_sin):
    half = ROT_DIM // 2
    cos, sin = cos_sin[0], cos_sin[1]
    d = lax.broadcasted_iota(jnp.int32, cos.shape, 1) & (HEAD_DIM - 1)
    mult = jnp.where(d < ROT_DIM, cos, 1.0)
    from_lo = jnp.where((d >= half) & (d < ROT_DIM), sin, 0.0)
    from_hi = jnp.where(d < half, -sin, 0.0)
    return mult, from_lo, from_hi


def _tile_lanes(tab, width):
    return jnp.tile(tab, (1, width // tab.shape[1]))


def _rope(v, tab):
    w = v.shape[1]
    half_rot = ROT_DIM // 2
    return (v * _tile_lanes(tab[0], w)
            + pltpu.roll(v, half_rot, axis=1) * _tile_lanes(tab[1], w)
            + pltpu.roll(v, w - half_rot, axis=1) * _tile_lanes(tab[2], w))


def _rope_bwd(dv, tab):
    w = dv.shape[1]
    half_rot = ROT_DIM // 2
    return (dv * _tile_lanes(tab[0], w)
            + pltpu.roll(dv * _tile_lanes(tab[1], w), w - half_rot, axis=1)
            + pltpu.roll(dv * _tile_lanes(tab[2], w), half_rot, axis=1))


def _shift_rows(v, prev8_ref, n):
    r = lax.broadcasted_iota(jnp.int32, v.shape, 0)
    rolled = pltpu.roll(v, n, axis=0)
    last = prev8_ref[7:8, :]
    if n == 1:
        return jnp.where(r >= 1, rolled, last)
    return jnp.where(r >= 2, rolled, jnp.where(r == 0, prev8_ref[6:7, :], last))


def _shift_rows_up(v, next8_ref, n):
    rows = v.shape[0]
    r = lax.broadcasted_iota(jnp.int32, v.shape, 0)
    rolled = pltpu.roll(v, rows - n, axis=0)
    first = next8_ref[0:1, :]
    if n == 1:
        return jnp.where(r <= rows - 2, rolled, first)
    return jnp.where(r <= rows - 3, rolled, jnp.where(r == rows - 2, first, next8_ref[1:2, :]))


def _lane_half_mask(shape, half):
    lane = lax.broadcasted_iota(jnp.int32, shape, 1)
    return (lane >= HEAD_DIM) if half else (lane < HEAD_DIM)


def _to_kv_lanes(chunk, head, kv):
    if head % 2 != kv:
        chunk = pltpu.roll(chunk, HEAD_DIM, axis=1)
    return jnp.where(_lane_half_mask(chunk.shape, kv), chunk, 0.0)


def _from_kv_lanes(chunk, head, kv):
    chunk = jnp.where(_lane_half_mask(chunk.shape, kv), chunk, 0.0)
    if head % 2 != kv:
        chunk = pltpu.roll(chunk, HEAD_DIM, axis=1)
    return chunk


def _stack_heads(wide):
    parts = []
    for head in range(N_Q_HEADS):
        chunk = wide[:, (head // 2) * 128:(head // 2 + 1) * 128]
        parts.append(_to_kv_lanes(chunk, head, head // Q_PER_KV))
    return jnp.concatenate(parts, axis=0)


def _window_mask(has_prev):
    shape = (N_Q_HEADS * BLOCK, 2 * BLOCK)
    qi = lax.broadcasted_iota(jnp.int32, shape, 0) & (BLOCK - 1)
    kj = lax.broadcasted_iota(jnp.int32, shape, 1)
    first_key = BLOCK - has_prev * BLOCK
    in_prev = (kj < BLOCK) & (kj > qi) & (kj >= first_key)
    in_own = (kj >= BLOCK) & ((kj - BLOCK) <= qi)
    return in_prev | in_own


def _sink_column(sink_ref):
    row = lax.broadcasted_iota(jnp.int32, (N_Q_HEADS * BLOCK, 1), 0)
    col = jnp.full((N_Q_HEADS * BLOCK, 1), sink_ref[0, 0], F32)
    for head in range(1, N_Q_HEADS):
        col = jnp.where(row >= head * BLOCK, sink_ref[0, head], col)
    return col


def _softmax_with_sink(q4, k2, mask, sink):
    s = lax.dot_general(q4, k2, NT_DIMS, preferred_element_type=F32) * SM_SCALE
    s = jnp.where(mask, s, MASK_VALUE)
    m = jnp.maximum(jnp.max(s, axis=-1, keepdims=True), sink)
    p = jnp.exp(s - m)
    e_sink = jnp.exp(sink - m)
    inv_den = 1.0 / (jnp.sum(p, axis=-1, keepdims=True) + e_sink)
    return p * inv_den, e_sink * inv_den


def _conv_terms(zf, prev8_ref, w_ref):
    b_gate, c_gate, u = zf[:, 0:CONV_W], zf[:, CONV_W:2 * CONV_W], zf[:, 2 * CONV_W:3 * CONV_W]
    vc = c_gate * u
    vm1 = _shift_rows(vc, prev8_ref, 1)
    vm2 = _shift_rows(vc, prev8_ref, 2)
    conv = w_ref[0:1, :] * vm2 + w_ref[1:2, :] * vm1 + w_ref[2:3, :] * vc
    return b_gate, c_gate, u, vc, vm1, vm2, conv


def _mixer_fwd(x, gain, win_t, wout, conv_w, sinks, rope, *, name, after=(), tq=512):
    t = x.shape[0]
    tq = min(tq, t)
    nblk = tq // BLOCK

    def body(x_ref, g_ref, win_hbm, wout_hbm, cw_ref, sink_ref, rope_ref,
             xo_ref, h_ref, z_ref, y_ref, kprev_ref, vprev_ref, cprev_ref, win_ref, wout_ref, sems):
        i = pl.program_id(0)
        _load_resident([(win_hbm, win_ref), (wout_hbm, wout_ref)], sems)

        @pl.when(i == 0)
        def _():
            kprev_ref[...] = jnp.zeros_like(kprev_ref)
            vprev_ref[...] = jnp.zeros_like(vprev_ref)
            cprev_ref[...] = jnp.zeros_like(cprev_ref)

        xf = x_ref[...]
        xhat, _ = _rms_stats(xf)
        h = (xhat * g_ref[...]).astype(BF16)
        h_ref[...] = h

        def project(c0, c1):
            zc = lax.dot_general(h, win_ref[c0:c1, :], NT_DIMS, preferred_element_type=F32).astype(BF16)
            z_ref[:, c0:c1] = zc
            return zc

        zb = project(Q_OFF, Z_W)
        zf = zb.astype(F32)
        tab = _rope_multipliers(rope_ref[...])
        qr = _rope(zf[:, 0:ATTN_W], tab)
        kr = _rope(zf[:, K_OFF - Q_OFF:V_OFF - Q_OFF], tab).astype(BF16)
        vb = zb[:, V_OFF - Q_OFF:Z_W - Q_OFF]
        conv_cols = [(c, c + CONV_W) for c in range(0, Q_OFF, CONV_W)]
        conv_z = []

        y_attn = []
        for j in range(nblk):
            if len(conv_z) < len(conv_cols):
                conv_z.append(project(*conv_cols[len(conv_z)]))
            rows = slice(j * BLOCK, (j + 1) * BLOCK)
            prev = slice((j - 1) * BLOCK, j * BLOCK)
            k2 = jnp.concatenate([kprev_ref[...] if j == 0 else kr[prev], kr[rows]], axis=0)
            v2 = jnp.concatenate([vprev_ref[...] if j == 0 else vb[prev], vb[rows]], axis=0)
            mask = _window_mask(jnp.minimum(i, 1) if j == 0 else 1)
            q8 = _stack_heads(qr[rows]).astype(BF16)
            probs, _ = _softmax_with_sink(q8, k2, mask, _sink_column(sink_ref))
            o8 = jnp.dot(probs.astype(BF16), v2, preferred_element_type=F32)
            chunks = [jnp.zeros((BLOCK, 128), F32) for _ in range(ATTN_W // 128)]
            for head in range(N_Q_HEADS):
                chunks[head // 2] += _from_kv_lanes(o8[head * BLOCK:(head + 1) * BLOCK], head, head // Q_PER_KV)
            y_attn.append(jnp.concatenate(chunks, axis=1))
        kprev_ref[...] = kr[tq - BLOCK:tq]
        vprev_ref[...] = vb[tq - BLOCK:tq]
        while len(conv_z) < len(conv_cols):
            conv_z.append(project(*conv_cols[len(conv_z)]))
        ya = jnp.concatenate(y_attn, axis=0).astype(BF16)
        y_ref[:, CONV_W:] = ya
        xo = xf + jnp.dot(ya, wout_ref[CONV_W:, :], preferred_element_type=F32)
        b_gate, _, _, vc, _, _, conv = _conv_terms(jnp.concatenate(conv_z, axis=1).astype(F32), cprev_ref, cw_ref)
        yc = (b_gate * conv).astype(BF16)
        cprev_ref[...] = vc[tq - 8:tq, :]
        y_ref[:, :CONV_W] = yc
        xo_ref[...] = xo + jnp.dot(yc, wout_ref[:CONV_W, :], preferred_element_type=F32)

    row = pl.BlockSpec((tq, D_MODEL), lambda i: (i, 0))
    full = lambda shape: pl.BlockSpec(shape, lambda i: (0,) * len(shape))
    return pl.pallas_call(
        _behind(body, 7, after), name=name, grid=(t // tq,),
        in_specs=[row, full((1, D_MODEL)), ANY, ANY, full((3, CONV_W)),
                  pl.BlockSpec(memory_space=pltpu.SMEM), pl.BlockSpec((2, tq, 128), lambda i: (0, i, 0))]
        + [ANY] * len(after),
        out_specs=[row, row, pl.BlockSpec((tq, Z_W), lambda i: (i, 0)), row],
        out_shape=[jax.ShapeDtypeStruct((t, D_MODEL), F32), jax.ShapeDtypeStruct((t, D_MODEL), BF16),
                   jax.ShapeDtypeStruct((t, Z_W), BF16), jax.ShapeDtypeStruct((t, D_MODEL), BF16)],
        scratch_shapes=[pltpu.VMEM((BLOCK, KV_W), BF16), pltpu.VMEM((BLOCK, KV_W), BF16),
                        pltpu.VMEM((8, CONV_W), F32), pltpu.VMEM((Z_W, D_MODEL), BF16),
                        pltpu.VMEM((D_MODEL, D_MODEL), BF16), pltpu.SemaphoreType.DMA((2, LOAD_PIECES))],
        compiler_params=_params(("arbitrary",), VMEM_LIMIT),
    )(x, gain, win_t, wout, conv_w, sinks, rope, *after)


def _mixer_bwd(dxo, x, gain, y, z, win_t, wout, conv_w, sinks, rope, *, name, after=(), tq=256):
    t = x.shape[0]
    tq = min(tq, t)
    nt, nblk = t // tq, tq // BLOCK

    def body(dxo_ref, x_ref, g_ref, y_ref, z_ref, zp_ref, win_hbm, wout_hbm, cw_ref, sink_ref, rope_ref, ropep_ref,
             dxi_ref, dz_ref, gb_ref, dcw_ref, dsink_ref, dg_ref, dk_ref, dv_ref, dcn_ref, pvc_ref,
             win_ref, wout_ref, sems):
        i = pl.program_id(0)
        tile = nt - 1 - i
        _load_resident([(win_hbm, win_ref), (wout_hbm, wout_ref)], sems)

        @pl.when(i == 0)
        def _():
            dk_ref[...] = jnp.zeros_like(dk_ref)
            dv_ref[...] = jnp.zeros_like(dv_ref)
            dcn_ref[...] = jnp.zeros_like(dcn_ref)
            dcw_ref[...] = jnp.zeros_like(dcw_ref)
            dsink_ref[...] = jnp.zeros_like(dsink_ref)
            dg_ref[...] = jnp.zeros_like(dg_ref)

        has_prev = jnp.minimum(tile, 1)
        go = dxo_ref[...]
        gb = go.astype(BF16)
        gb_ref[...] = gb
        dy_attn = lax.dot_general(gb, wout_ref[CONV_W:D_MODEL, :], NT_DIMS, preferred_element_type=F32)
        zb, zpb = z_ref[...], zp_ref[...]
        zf = zb.astype(F32)
        zpf = zpb.astype(F32) * has_prev.astype(F32)

        tab, tabp = _rope_multipliers(rope_ref[...]), _rope_multipliers(ropep_ref[...])
        qr = _rope(zf[:, Q_OFF:K_OFF], tab)
        kr = _rope(zf[:, K_OFF:V_OFF], tab).astype(BF16)
        kpr = _rope(zpf[:, K_OFF:V_OFF], tabp).astype(BF16)
        vb, vpb = zb[:, V_OFF:Z_W], zpb[:, V_OFF:Z_W]
        out = y_ref[:, CONV_W:D_MODEL].astype(F32)
        do_out = dy_attn * out
        lane = lax.broadcasted_iota(jnp.int32, (1, 128), 1)
        dsink = jnp.zeros((1, 128), F32)
        dk_next, dv_next = dk_ref[...], dv_ref[...]
        dq_rows, dk_rows, dv_rows = [None] * nblk, [None] * nblk, [None] * nblk
        for j in reversed(range(nblk)):
            rows = slice(j * BLOCK, (j + 1) * BLOCK)
            prev = slice((j - 1) * BLOCK, j * BLOCK)
            k2 = jnp.concatenate([kpr if j == 0 else kr[prev], kr[rows]], axis=0)
            v2 = jnp.concatenate([vpb if j == 0 else vb[prev], vb[rows]], axis=0)
            mask = _window_mask(has_prev if j == 0 else 1)
            q8 = _stack_heads(qr[rows]).astype(BF16)
            do8 = _stack_heads(dy_attn[rows]).astype(BF16)
            delta = jnp.sum(_stack_heads(do_out[rows]), axis=-1, keepdims=True)
            probs, p_sink = _softmax_with_sink(q8, k2, mask, _sink_column(sink_ref))
            dp = lax.dot_general(do8, v2, NT_DIMS, preferred_element_type=F32)
            ds = (probs * (dp - delta) * SM_SCALE).astype(BF16)
            dq8 = jnp.dot(ds, k2, preferred_element_type=F32)
            dk2 = lax.dot_general(ds, q8, TN_DIMS, preferred_element_type=F32)
            dv2 = lax.dot_general(probs.astype(BF16), do8, TN_DIMS, preferred_element_type=F32)
            sink_terms = p_sink * delta
            dq_chunks = [jnp.zeros((BLOCK, 128), F32) for _ in range(ATTN_W // 128)]
            for head in range(N_Q_HEADS):
                grp = slice(head * BLOCK, (head + 1) * BLOCK)
                dq_chunks[head // 2] += _from_kv_lanes(dq8[grp], head, head // Q_PER_KV)
                dsink = dsink - jnp.where(lane == head, jnp.sum(sink_terms[grp], axis=0, keepdims=True), 0.0)
            dq_rows[j] = jnp.concatenate(dq_chunks, axis=1)
            dk_rows[j] = dk2[BLOCK:] + dk_next
            dv_rows[j] = dv2[BLOCK:] + dv_next
            dk_next, dv_next = dk2[:BLOCK], dv2[:BLOCK]
            if j == nblk - 1:
                dy_conv = lax.dot_general(gb, wout_ref[0:CONV_W, :], NT_DIMS, preferred_element_type=F32)
        dk_ref[...] = dk_next
        dv_ref[...] = dv_next
        dsink_ref[...] += dsink
        dq = _rope_bwd(jnp.concatenate(dq_rows, axis=0), tab)
        dk = _rope_bwd(jnp.concatenate(dk_rows, axis=0), tab)
        dv = jnp.concatenate(dv_rows, axis=0)
        dz_attn = jnp.concatenate([dq, dk, dv], axis=1).astype(BF16)
        dz_ref[:, Q_OFF:Z_W] = dz_attn
        dh_attn = jnp.dot(dz_attn, win_ref[Q_OFF:Z_W, :], preferred_element_type=F32)

        pvc_ref[...] = (zpf[:, CONV_W:2 * CONV_W] * zpf[:, 2 * CONV_W:3 * CONV_W])[BLOCK - 8:BLOCK, :]
        b_gate, c_gate, u, vc, vm1, vm2, conv = _conv_terms(zf, pvc_ref, cw_ref)
        d_bgate = dy_conv * conv
        dc = dy_conv * b_gate
        tap = lax.broadcasted_iota(jnp.int32, (8, CONV_W), 0)
        dcw_ref[...] += jnp.where(tap == 0, jnp.sum(dc * vm2, axis=0, keepdims=True),
                                  jnp.where(tap == 1, jnp.sum(dc * vm1, axis=0, keepdims=True),
                                            jnp.where(tap == 2, jnp.sum(dc * vc, axis=0, keepdims=True), 0.0)))
        dvc = (cw_ref[2:3, :] * dc + cw_ref[1:2, :] * _shift_rows_up(dc, dcn_ref, 1)
               + cw_ref[0:1, :] * _shift_rows_up(dc, dcn_ref, 2))
        dcn_ref[...] = dc[0:8, :]
        d_cgate = dvc * u
        d_u = dvc * c_gate

        dz_conv = jnp.concatenate([d_bgate, d_cgate, d_u], axis=1).astype(BF16)
        dz_ref[:, 0:Q_OFF] = dz_conv
        dh = dh_attn + jnp.dot(dz_conv, win_ref[0:Q_OFF, :], preferred_element_type=F32)
        xhat, inv = _rms_stats(x_ref[...])
        dx, dgain = _rms_bwd(dh, xhat, inv, g_ref[...])
        dxi_ref[...] = go + dx
        dg_ref[...] += dgain

    rev = lambda i: (nt - 1 - i, 0)
    block_before = lambda i: jnp.maximum((nt - 1 - i) * nblk - 1, 0)
    row = pl.BlockSpec((tq, D_MODEL), rev)
    full = lambda shape: pl.BlockSpec(shape, lambda i: (0,) * len(shape))
    return pl.pallas_call(
        _behind(body, 12, after), name=name, grid=(nt,),
        in_specs=[row, row, full((1, D_MODEL)), row,
                  pl.BlockSpec((tq, Z_W), rev), pl.BlockSpec((BLOCK, Z_W), lambda i: (block_before(i), 0)),
                  ANY, ANY, full((3, CONV_W)),
                  pl.BlockSpec(memory_space=pltpu.SMEM),
                  pl.BlockSpec((2, tq, 128), lambda i: (0, nt - 1 - i, 0)),
                  pl.BlockSpec((2, BLOCK, 128), lambda i: (0, block_before(i), 0))] + [ANY] * len(after),
        out_specs=[row, pl.BlockSpec((tq, Z_W), rev), row, full((8, CONV_W)), full((1, 128)), full((1, D_MODEL))],
        out_shape=[jax.ShapeDtypeStruct((t, D_MODEL), F32), jax.ShapeDtypeStruct((t, Z_W), BF16),
                   jax.ShapeDtypeStruct((t, D_MODEL), BF16), jax.ShapeDtypeStruct((8, CONV_W), F32),
                   jax.ShapeDtypeStruct((1, 128), F32), jax.ShapeDtypeStruct((1, D_MODEL), F32)],
        scratch_shapes=[pltpu.VMEM((BLOCK, KV_W), F32), pltpu.VMEM((BLOCK, KV_W), F32), pltpu.VMEM((8, CONV_W), F32),
                        pltpu.VMEM((8, CONV_W), F32), pltpu.VMEM((Z_W, D_MODEL), BF16),
                        pltpu.VMEM((D_MODEL, D_MODEL), BF16), pltpu.SemaphoreType.DMA((2, LOAD_PIECES))],
        compiler_params=_params(("arbitrary",), VMEM_LIMIT),
    )(dxo, x, gain, y, z, z, win_t, wout, conv_w, sinks, rope, rope, *after)


def _place():
    x, y, c = lax.axis_index("x"), lax.axis_index("y"), lax.axis_index("c")
    other_chips = [(1 - x, y), (x, 1 - y), (1 - x, 1 - y)]
    return x, y, c, other_chips


def _all_gather_rows(shards, place=(), *, name):
    n, p = len(shards), len(place)

    def body(*refs):
        srcs, place_srcs = refs[:n], refs[n:n + p]
        outs, place_outs = refs[n + p:2 * n + p], refs[2 * n + p:2 * (n + p)]
        send_sems, recv_sems, local_sems = refs[2 * (n + p):]
        x, y, c, _ = _place()
        me, sibling = (x, y, c), (x, y, 1 - c)
        relay_from = (x + (1 - c) - 2 * x * (1 - c), y + c - 2 * y * c)
        relay_to = (x + c - 2 * x * c, y + (1 - c) - 2 * y * (1 - c))
        chips = [relay_from, relay_to, (1 - x, 1 - y)]

        def rows(t, px, py, pc):
            r = srcs[t].shape[-2]
            start = pl.multiple_of((4 * px + 2 * py + pc) * r, 16 if r % 16 == 0 else 8)
            if len(srcs[t].shape) == 3:
                return outs[t].at[:, pl.ds(start, r), :]
            return outs[t].at[pl.ds(start, r), :]

        def copy(t, k, block, to, own=False):
            return pltpu.make_async_remote_copy(
                src_ref=srcs[t] if own else rows(t, *block), dst_ref=rows(t, *block),
                send_sem=send_sems.at[t, k], recv_sem=recv_sems.at[t, k], device_id=to, device_id_type=MESH)

        mine = [pltpu.make_async_copy(srcs[t], rows(t, *me), local_sems.at[t]) for t in range(n)]
        mine += [pltpu.make_async_copy(place_srcs[q],
                                       _block_rows(place_outs[q], place_srcs[q].shape[-2], 4 * x + 2 * y + c),
                                       local_sems.at[n + q]) for q in range(p)]
        for q in range(p):
            mine[n + q].start()
        first = []
        for t in range(n):
            mine[t].start()
            first.append(copy(t, 0, me, sibling, own=True))
            first += [copy(t, 1 + j, me, (*chip, c), own=True) for j, chip in enumerate(chips[:2])]
        for cp in first:
            cp.start()
        passed = []
        for j, chip in enumerate(chips):
            for t in range(n):
                copy(t, 1 + j, (*chip, c), me).wait_recv()
                if j == 0:
                    passed.append(copy(t, 3, (*chip, c), (*relay_to, c)))
                    passed[-1].start()
                passed.append(copy(t, 4 + j, (*chip, c), sibling))
                passed[-1].start()
        for t in range(n):
            copy(t, 0, sibling, me).wait_recv()
            for j, chip in enumerate([relay_to, relay_from, chips[2]]):
                copy(t, 4 + j, (*chip, 1 - c), me).wait_recv()
        for cp in first + passed:
            cp.wait_send()
        for cp in mine:
            cp.wait()

    out_shape = [jax.ShapeDtypeStruct(s.shape[:-2] + (N_DEV * s.shape[-2], s.shape[-1]), s.dtype)
                 for s in list(shards) + list(place)]
    res = pl.pallas_call(
        body, name=name, in_specs=[ANY] * (n + p), out_specs=[ANY] * (n + p), out_shape=out_shape,
        scratch_shapes=[pltpu.SemaphoreType.DMA((n, 7)), pltpu.SemaphoreType.DMA((n, 7)),
                        pltpu.SemaphoreType.DMA((n + p,))],
    )(*shards, *place)
    return res[:n], res[n:]


def _split_start(bufs, n_copies, plan, *, name, after=()):
    n = len(bufs)

    def body(*refs):
        token = refs[-1]
        for cp in plan(refs[:n], refs[n], refs[n + 1]):
            cp.start()
        token[...] = jnp.zeros_like(token)

    res = pl.pallas_call(
        _behind(body, n, after), name=name, in_specs=[HBM_SPEC] * n + [ANY] * len(after),
        out_specs=(SEM_SPEC, SEM_SPEC, *[HBM_SPEC] * n, pl.BlockSpec(memory_space=pltpu.VMEM)),
        out_shape=(pltpu.SemaphoreType.DMA((n_copies,)), pltpu.SemaphoreType.DMA((n_copies,)),
                   *[pltpu.HBM(b.shape, b.dtype) for b in bufs], jax.ShapeDtypeStruct((8, 128), F32)),
        input_output_aliases={i: 2 + i for i in range(n)},
        compiler_params=pltpu.CompilerParams(has_side_effects=DATAFLOW),
    )(*[pltpu.with_memory_space_constraint(b, pltpu.HBM) for b in bufs], *after)
    return res[0], res[1], list(res[2:2 + n]), res[-1]


def _split_wait(send_sems, recv_sems, bufs, after, plan, *, name):
    n = len(bufs)

    def body(*refs):
        for cp in plan(refs[:n], refs[n], refs[n + 1]):
            cp.wait_send()
            cp.wait_recv()

    return list(pl.pallas_call(
        body, name=name, in_specs=[HBM_SPEC] * n + [SEM_SPEC, SEM_SPEC, ANY], out_specs=[HBM_SPEC] * n,
        out_shape=tuple(pltpu.HBM(b.shape, b.dtype) for b in bufs),
        input_output_aliases={i: i for i in range(n)},
        compiler_params=pltpu.CompilerParams(has_side_effects=DATAFLOW),
    )(*bufs, send_sems, recv_sems, after))


def _sibling_plan(n, first=0, chips=slice(0, 4)):
    def plan(bufs, send_sems, recv_sems):
        x, y, c, _ = _place()
        return [_remote(bufs[t].at[chips, 1 - c], bufs[n + t].at[chips], send_sems, recv_sems, first + t, (x, y, 1 - c))
                for t in range(n)]
    return plan


def _block_rows(ref, r, blk):
    start = pl.multiple_of(blk * r, 16 if r % 16 == 0 else 8)
    return ref.at[(slice(None),) * (len(ref.shape) - 2) + (pl.ds(start, r), slice(None))]


def _remote(src, dst, send_sems, recv_sems, k, peer):
    return pltpu.make_async_remote_copy(src_ref=src, dst_ref=dst, send_sem=send_sems.at[k], recv_sem=recv_sems.at[k],
                                        device_id=peer, device_id_type=MESH)


def _gather_send_plan(n, first=0):
    def plan(bufs, send_sems, recv_sems):
        x, y, c, chips = _place()
        peers = [(x, y, 1 - c)] + [(px, py, c) for px, py in chips]
        copies = []
        for t in range(n):
            dst = _block_rows(bufs[n + t], bufs[t].shape[-2], 4 * x + 2 * y + c)
            copies += [_remote(bufs[t], dst, send_sems, recv_sems, 4 * (first + t) + k, peer)
                       for k, peer in enumerate(peers)]
        return copies
    return plan


def _gather_forward_plan(rows):
    def plan(bufs, send_sems, recv_sems):
        x, y, c, chips = _place()
        copies = []
        for t, r in enumerate(rows):
            for j, (px, py) in enumerate(chips):
                blk = _block_rows(bufs[t], r, 4 * px + 2 * py + c)
                copies.append(_remote(blk, blk, send_sems, recv_sems, 3 * t + j, (x, y, 1 - c)))
        return copies
    return plan


def _chips_plan(n, with_small):
    def plan(bufs, send_sems, recv_sems):
        x, y, c, chips = _place()
        copies = []
        for t in range(n):
            for j, (px, py) in enumerate(chips):
                copies.append(_remote(bufs[t].at[2 * px + py], bufs[n + t].at[j], send_sems, recv_sems, 3 * t + j,
                                      (px, py, c)))
        if with_small:
            mine = _block_rows(bufs[2 * n], 8, 4 * x + 2 * y + c)
            flips = [(fx, fy, fc) for fx in range(2) for fy in range(2) for fc in range(2)][1:]
            for k, (fx, fy, fc) in enumerate(flips):
                peer = (x + fx - 2 * x * fx, y + fy - 2 * y * fy, c + fc - 2 * c * fc)
                copies.append(_remote(mine, mine, send_sems, recv_sems, 3 * n + k, peer))
        return copies
    return plan


def _place_own(fulls, shards, index, *, name):
    n = len(fulls)

    def body(index_ref, *refs):
        for t in range(n):
            refs[2 * n + t][...] = refs[n + t][...]

    def block_of(shard):
        lead = len(shard.shape) - 2
        return pl.BlockSpec(shard.shape, lambda i, index_ref: (0,) * lead + (index_ref[0], 0))

    def whole(shard):
        return pl.BlockSpec(shard.shape, lambda i, index_ref: (0,) * len(shard.shape))

    return list(pl.pallas_call(
        body, name=name,
        grid_spec=pltpu.PrefetchScalarGridSpec(
            num_scalar_prefetch=1, grid=(1,),
            in_specs=[ANY] * n + [whole(s) for s in shards], out_specs=[block_of(s) for s in shards]),
        out_shape=[jax.ShapeDtypeStruct(f.shape, f.dtype) for f in fulls],
        input_output_aliases={1 + t: t for t in range(n)},
        compiler_params=_params(("arbitrary",)),
    )(index, *fulls, *shards))


N_STEPS_SMALL = 2


def _add_sibling(grads, recvs, place, *, name):
    n = len(grads)

    def body(place_ref, *refs):
        chip = place_ref[1]
        for t in range(n):
            g_ref, r_ref, own_ref, ob_ref = refs[2 * t], refs[2 * t + 1], refs[2 * n + 2 * t], refs[2 * n + 2 * t + 1]
            own = jnp.zeros(own_ref.shape, F32)
            for m in range(4):
                p = g_ref[m, 0] + r_ref[m]
                ob_ref[m] = p.astype(BF16)
                own = jnp.where(chip == m, p, own)
            own_ref[...] = own

    in_specs, out_specs, out_shape = [], [], []
    for g, r in zip(grads, recvs):
        tr = g.shape[2] // N_STEPS_SMALL
        blocks = pl.BlockSpec((4, tr, D_MODEL), lambda i, place_ref: (0, i, 0))
        in_specs += [pl.BlockSpec((4, 1, tr, D_MODEL), lambda i, place_ref: (0, place_ref[0], i, 0)), blocks]
        out_specs += [pl.BlockSpec((tr, D_MODEL), lambda i, place_ref: (i, 0)), blocks]
        out_shape += [jax.ShapeDtypeStruct(r.shape[1:], F32), jax.ShapeDtypeStruct(r.shape, BF16)]
    res = pl.pallas_call(
        body, name=name,
        grid_spec=pltpu.PrefetchScalarGridSpec(num_scalar_prefetch=1, grid=(N_STEPS_SMALL,), in_specs=in_specs,
                                               out_specs=out_specs),
        out_shape=out_shape, compiler_params=_params(("arbitrary",), VMEM_LIMIT),
    )(place, *[a for pair in zip(grads, recvs) for a in pair])
    return [(res[2 * t], res[2 * t + 1]) for t in range(n)]


def _reduce_adamw(parts, *, name, after=()):
    n = len(parts)

    def body(*refs):
        for t in range(n):
            p_ref, r_ref, w_ref, m_ref, v_ref = refs[5 * t:5 * t + 5]
            g_ref, d_ref, mo_ref, vo_ref = refs[5 * n + 4 * t:5 * n + 4 * t + 4]
            g = p_ref[...] + r_ref[0].astype(F32) + r_ref[1].astype(F32) + r_ref[2].astype(F32)
            g_ref[...] = g
            d_ref[...], mo_ref[...], vo_ref[...] = _adamw_math(w_ref[...], g, m_ref[...], v_ref[...])
        refs[-1][...] = jnp.zeros_like(refs[-1])

    in_specs, out_specs, out_shape = [], [], []
    for own, _, _, _, _ in parts:
        rows = own.shape[0]
        tr = rows // N_STEPS_SMALL
        spec = pl.BlockSpec((tr, D_MODEL), lambda i: (i, 0))
        in_specs += [spec, pl.BlockSpec((3, tr, D_MODEL), lambda i: (0, i, 0)), spec, spec, spec]
        out_specs += [spec] * 4
        out_shape += [jax.ShapeDtypeStruct((rows, D_MODEL), F32)] * 4
    res = pl.pallas_call(
        _behind(body, 5 * n, after), name=name, grid=(N_STEPS_SMALL,),
        in_specs=in_specs + [ANY] * len(after),
        out_specs=out_specs + [pl.BlockSpec((8, 128), lambda i: (0, 0))],
        out_shape=out_shape + [jax.ShapeDtypeStruct((8, 128), F32)],
        compiler_params=_params(("arbitrary",), VMEM_LIMIT),
    )(*[a for part in parts for a in part], *after)
    return [tuple(res[4 * t:4 * t + 4]) for t in range(n)], res[-1]


def _adamw_math(w, g, m, v):
    m = ADAM_B1 * m + (1.0 - ADAM_B1) * g
    v = ADAM_B2 * v + (1.0 - ADAM_B2) * (g * g)
    m_hat = m / (1.0 - ADAM_B1 ** ADAM_STEP)
    v_hat = v / (1.0 - ADAM_B2 ** ADAM_STEP)
    delta = -ADAM_LR * (m_hat / (jnp.sqrt(v_hat) + ADAM_EPS) + ADAM_WD * w)
    return delta, m, v


SMALL_NAMES = ["ffn1_norm", "mix_norm", "ffn2_norm", "final_norm", "conv_w", "attn_sinks"]


def _update_small(given, moments_m, moments_v, small_all, my_index, *, name):
    conv_cols = given["conv_w"].shape[2]
    per_block = 128 // conv_cols

    def two_d(nm, a):
        return a.reshape(1, D_MODEL) if nm == "final_norm" else a

    operands = [two_d(nm, src[nm]) for nm in SMALL_NAMES for src in (given, moments_m, moments_v)]
    n = len(SMALL_NAMES)

    def body(index_ref, all_ref, conv_ref, *refs):
        ins, outs = refs[:3 * n], refs[3 * n:]
        total, conv_total = all_ref[0], conv_ref[0]
        for k in range(1, N_DEV):
            total, conv_total = total + all_ref[k], conv_total + conv_ref[k]
        which = index_ref[0] % per_block
        conv_g = conv_total[4:7, :conv_cols]
        for j in range(1, per_block):
            conv_g = jnp.where(which == j, conv_total[4:7, j * conv_cols:(j + 1) * conv_cols], conv_g)
        grads = [total[0:1], total[1:2], total[2:3], total[3:4], conv_g[None], total[7:8, :N_Q_HEADS]]
        for t, g in enumerate(grads):
            w_ref, m_ref, v_ref = ins[3 * t:3 * t + 3]
            g_ref, d_ref, mo_ref, vo_ref = outs[4 * t:4 * t + 4]
            g_ref[...] = g
            d_ref[...], mo_ref[...], vo_ref[...] = _adamw_math(w_ref[...], g, m_ref[...], v_ref[...])
        outs[-1][...] = total[7:8, LOSS_LANE:LOSS_LANE + 1]

    def whole(shape):
        return pl.BlockSpec(shape, lambda i, index_ref: (0,) * len(shape))

    shapes = [a.shape for a in operands[::3] for _ in range(4)] + [(1, 1)]
    res = pl.pallas_call(
        body, name=name,
        grid_spec=pltpu.PrefetchScalarGridSpec(
            num_scalar_prefetch=1, grid=(1,),
            in_specs=[whole(small_all.shape),
                      pl.BlockSpec((N_DEV, 8, 128), lambda i, index_ref: (0, 0, index_ref[0] // per_block))]
            + [whole(a.shape) for a in operands],
            out_specs=[whole(s) for s in shapes]),
        out_shape=[jax.ShapeDtypeStruct(s, F32) for s in shapes],
        compiler_params=_params(("arbitrary",)),
    )(my_index.astype(jnp.int32).reshape(1), small_all, small_all, *operands)
    results = {nm: tuple(a.reshape(given[nm].shape) for a in res[4 * t:4 * t + 4]) for t, nm in enumerate(SMALL_NAMES)}
    return results, res[-1]


def kernel(x, ffn1_norm, ffn1_w_gate, ffn1_w_up, ffn1_w_down, mix_norm, w_in, conv_w, attn_sinks, w_out, ffn2_norm, ffn2_w_gate, ffn2_w_up, ffn2_w_down, final_norm, loss_target, m_ffn1_norm, m_ffn1_w_gate, m_ffn1_w_up, m_ffn1_w_down, m_mix_norm, m_w_in, m_conv_w, m_attn_sinks, m_w_out, m_ffn2_norm, m_ffn2_w_gate, m_ffn2_w_up, m_ffn2_w_down, m_final_norm, v_ffn1_norm, v_ffn1_w_gate, v_ffn1_w_up, v_ffn1_w_down, v_mix_norm, v_w_in, v_conv_w, v_attn_sinks, v_w_out, v_ffn2_norm, v_ffn2_w_gate, v_ffn2_w_up, v_ffn2_w_down, v_final_norm):
    ix, iy, ic = lax.axis_index("x"), lax.axis_index("y"), lax.axis_index("c")
    my_index = 4 * ix + 2 * iy + ic
    place = jnp.stack([ic, 2 * ix + iy]).astype(jnp.int32)

    given = dict(ffn1_norm=ffn1_norm, ffn1_w_gate=ffn1_w_gate, ffn1_w_up=ffn1_w_up, ffn1_w_down=ffn1_w_down,
                 mix_norm=mix_norm, w_in=w_in, conv_w=conv_w, attn_sinks=attn_sinks, w_out=w_out, ffn2_norm=ffn2_norm,
                 ffn2_w_gate=ffn2_w_gate, ffn2_w_up=ffn2_w_up, ffn2_w_down=ffn2_w_down, final_norm=final_norm)
    moments_m = dict(ffn1_norm=m_ffn1_norm, ffn1_w_gate=m_ffn1_w_gate, ffn1_w_up=m_ffn1_w_up, ffn1_w_down=m_ffn1_w_down,
                     mix_norm=m_mix_norm, w_in=m_w_in, conv_w=m_conv_w, attn_sinks=m_attn_sinks, w_out=m_w_out,
                     ffn2_norm=m_ffn2_norm, ffn2_w_gate=m_ffn2_w_gate, ffn2_w_up=m_ffn2_w_up, ffn2_w_down=m_ffn2_w_down,
                     final_norm=m_final_norm)
    moments_v = dict(ffn1_norm=v_ffn1_norm, ffn1_w_gate=v_ffn1_w_gate, ffn1_w_up=v_ffn1_w_up, ffn1_w_down=v_ffn1_w_down,
                     mix_norm=v_mix_norm, w_in=v_w_in, conv_w=v_conv_w, attn_sinks=v_attn_sinks, w_out=v_w_out,
                     ffn2_norm=v_ffn2_norm, ffn2_w_gate=v_ffn2_w_gate, ffn2_w_up=v_ffn2_w_up, ffn2_w_down=v_ffn2_w_down,
                     final_norm=v_final_norm)

    xs = x[0]
    target = loss_target[0]
    final_gain = final_norm.reshape(1, D_MODEL)

    def ffn_shards(wg, wu, wd):
        return jnp.stack([wg[0].T, wu[0].T]).astype(BF16), wd[0].astype(BF16)

    conv_cols = conv_w.shape[2]
    conv_shard = jnp.pad(conv_w[0], ((0, 5), (0, 128 - conv_cols)))
    gate_up1, down1 = ffn_shards(ffn1_w_gate, ffn1_w_up, ffn1_w_down)
    rest_shards = [down1, w_in[0].T.astype(BF16), w_out[0].astype(BF16), conv_shard,
                   *ffn_shards(ffn2_w_gate, ffn2_w_up, ffn2_w_down)]
    rest_rows = [s.shape[-2] for s in rest_shards]
    n_rest, n_early = len(rest_shards), 4
    (w1_gu,), _ = _all_gather_rows([gate_up1], name="gather_ffn1")

    fulls = [lax.empty(s.shape[:-2] + (N_DEV * s.shape[-2], s.shape[-1]), s.dtype) for s in rest_shards]
    fulls = _place_own(fulls, rest_shards, my_index.astype(jnp.int32).reshape(1), name="place_own_weights")
    ssem, rsem, bufs, token = _split_start(rest_shards + list(fulls), 4 * n_rest, _gather_send_plan(n_rest),
                                           name="gather_rest_start", after=[w1_gu])
    early = bufs[:n_early] + bufs[n_rest:n_rest + n_early]
    late = bufs[n_early:n_rest] + bufs[n_rest + n_early:]
    h1, s1, sa1, sb1 = _ffn_fwd(xs, ffn1_norm, w1_gu, name="ffn1_hidden", after=[token])
    early = _split_wait(ssem, rsem, early, h1, _gather_send_plan(n_early), name="gather_early_wait")
    fwd_early = _gather_forward_plan(rest_rows[:n_early])
    ssem_e, rsem_e, parts, token = _split_start(early[n_early:], 3 * n_early, fwd_early, name="forward_early_start")
    rope = _rope_tables(xs.shape[0], after=[token])
    w1_d, win_t, wout, conv_all = _split_wait(ssem_e, rsem_e, parts, rope, fwd_early, name="forward_early_wait")
    x1 = _ffn_down(xs, s1, w1_d, name="ffn1_down")
    conv_full = conv_all.reshape(N_DEV, 8, 128)[:, :3, :conv_cols].transpose(1, 0, 2).reshape(3, CONV_W)
    late = _split_wait(ssem, rsem, late, x1, _gather_send_plan(n_rest - n_early, first=n_early),
                       name="gather_late_wait")
    fwd_ffn2 = _gather_forward_plan(rest_rows[n_early:])
    ssem, rsem, parts, token = _split_start(late[n_rest - n_early:], 3 * (n_rest - n_early), fwd_ffn2,
                                            name="forward_ffn2_start")
    x2, hm, z, y = _mixer_fwd(x1, mix_norm, win_t, wout, conv_full, attn_sinks, rope, name="mixer_fwd", after=[token])
    w2_gu, w2_d = _split_wait(ssem, rsem, parts, x2, fwd_ffn2, name="forward_ffn2_wait")
    dx3, h2, s2, sa2, sb2, loss_local, d_final = _ffn_fwd(x2, ffn2_norm, w2_gu, w2_d, head=(final_gain, target),
                                                          name="ffn2_fwd")

    def sibling_bufs(grads):
        views = [g.reshape(4, 2, g.shape[0] // N_DEV, D_MODEL) for g in grads]
        return views + [lax.empty((4,) + v.shape[2:], F32) for v in views]

    def chips_bufs(partials, small_all):
        p16 = [p for _, p in partials]
        return p16 + [lax.empty((3,) + p.shape[1:], BF16) for p in p16] + ([] if small_all is None else [small_all])

    def to_sibling_start(grads, tag, after=()):
        plan = _sibling_plan(len(grads))
        ssem, rsem, bufs, token = _split_start(sibling_bufs(grads), len(grads), plan, name=f"{tag}_sibling_start",
                                               after=after)
        return (ssem, rsem, bufs, plan, tag), token

    def to_chips_and_sibling_start(partials, tag, grads, sibling_tag, after=(), chips=slice(0, 4)):
        for_chips, n_ici = chips_bufs(partials, None), 3 * len(partials)
        chips_plan = _chips_plan(len(partials), False)
        sibling_plan = _sibling_plan(len(grads), first=n_ici, chips=chips)

        def both(bufs, send_sems, recv_sems):
            return (chips_plan(bufs[:len(for_chips)], send_sems, recv_sems)
                    + sibling_plan(bufs[len(for_chips):], send_sems, recv_sems))

        ssem, rsem, bufs, token = _split_start(for_chips + sibling_bufs(grads), n_ici + len(grads), both,
                                               name=f"{tag}_chips_{sibling_tag}_sibling_start", after=after)
        return ((ssem, rsem, bufs[:len(for_chips)], chips_plan, tag),
                (ssem, rsem, bufs[len(for_chips):], sibling_plan, sibling_tag), token)

    def to_sibling_finish(handle, after, names):
        ssem, rsem, bufs, plan, tag = handle
        bufs = _split_wait(ssem, rsem, bufs, after, plan, name=f"{tag}_sibling_wait")
        n = len(names)
        return _add_sibling(bufs[:n], bufs[n:], place, name=f"add_sibling_{tag}")

    def to_chips_start(partials, tag, small_all=None, after=()):
        plan = _chips_plan(len(partials), small_all is not None)
        n_copies = 3 * len(partials) + (0 if small_all is None else N_DEV - 1)
        ssem, rsem, bufs, token = _split_start(chips_bufs(partials, small_all), n_copies, plan,
                                               name=f"{tag}_chips_start", after=after)
        return (ssem, rsem, bufs, plan, tag), token

    def to_chips_finish(handle, partials, after, names):
        ssem, rsem, bufs, plan, tag = handle
        bufs = _split_wait(ssem, rsem, bufs, after, plan, name=f"{tag}_chips_wait")
        n = len(names)
        return [(p32, r) for (p32, _), r in zip(partials, bufs[n:2 * n])], bufs[2 * n:]

    half_ff = D_FF // 2
    names2, namesm = ["ffn2_w_gate", "ffn2_w_up", "ffn2_w_down"], ["w_in", "w_out"]
    transposed = {"ffn1_w_gate", "ffn1_w_up", "w_in", "ffn2_w_gate", "ffn2_w_up"}
    grad, delta, new_m, new_v = {}, {}, {}, {}

    def adam_big(names, parts, tag, after=()):
        def to_rows(nm, a):
            return jnp.swapaxes(a, 1, 2)[0] if nm in transposed else a[0]

        def from_rows(nm, a):
            return jnp.swapaxes(a[None], 1, 2) if nm in transposed else a[None]

        operands = [(p32, recv, to_rows(nm, given[nm]), to_rows(nm, moments_m[nm]), to_rows(nm, moments_v[nm]))
                    for nm, (p32, recv) in zip(names, parts)]
        results, token = _reduce_adamw(operands, name=f"adamw_{tag}", after=after)
        for nm, outs in zip(names, results):
            grad[nm], delta[nm], new_m[nm], new_v[nm] = (from_rows(nm, a) for a in outs)
        return token

    dx2, da2, db2, g2b, d_norm2 = _ffn_dgrad(dx3, x2, ffn2_norm, sa2, sb2, w2_gu, w2_d, name="ffn2_dgrad")
    gw2 = [_tn_matmul(da2, h2, name="ffn2_wgrad_gate", bm=half_ff), _tn_matmul(db2, h2, name="ffn2_wgrad_up", bm=half_ff),
           _tn_matmul(s2, g2b, name="ffn2_wgrad_down", bm=half_ff)]
    sib2, tok = to_sibling_start(gw2, "ffn2")
    dx1, dz, gmb, d_conv, d_sink, d_normm = _mixer_bwd(dx2, x1, mix_norm, y, z, win_t, wout, conv_full, attn_sinks,
                                                       rope, name="mixer_bwd", after=[tok])
    gwm = [_tn_matmul(dz, hm, name="mixer_wgrad_in", bm=Z_W // 2), _tn_matmul(y, gmb, name="mixer_wgrad_out", bm=D_MODEL)]
    p2 = to_sibling_finish(sib2, gwm[1], names2)
    chips2, sibm, tok = to_chips_and_sibling_start(p2, "ffn2", gwm, "mixer")
    dx0, da1, db1, g1b, d_norm1 = _ffn_dgrad(dx1, xs, ffn1_norm, sa1, sb1, w1_gu, w1_d, name="ffn1_dgrad", after=[tok])
    r2, _ = to_chips_finish(chips2, p2, dx0, names2)
    pm = to_sibling_finish(sibm, dx0, namesm)
    gw_gate = _tn_matmul(da1, h1, name="ffn1_wgrad_gate", bm=half_ff)
    chipsm, sib_gate, tok = to_chips_and_sibling_start(pm, "mixer", [gw_gate], "ffn1_gate")
    gw_up = _tn_matmul(db1, h1, name="ffn1_wgrad_up", bm=half_ff, after=[tok])
    rm, _ = to_chips_finish(chipsm, pm, gw_up, namesm)
    p_gate = to_sibling_finish(sib_gate, gw_up, ["ffn1_w_gate"])
    chips_gate, sib_up, tok = to_chips_and_sibling_start(p_gate, "ffn1_gate", [gw_up], "ffn1_up")
    gw_down = _tn_matmul(s1, g1b, name="ffn1_wgrad_down_first", bm=half_ff, blocks=(0, 1), after=[tok])
    p_up = to_sibling_finish(sib_up, gw_down, ["ffn1_w_up"])
    chips_up, sib_first, tok = to_chips_and_sibling_start(p_up, "ffn1_up", [gw_down], "ffn1_down_first",
                                                          chips=slice(0, 2))
    view, land = sib_first[2]
    gw_down = _tn_matmul(s1, g1b, name="ffn1_wgrad_down_second", bm=half_ff, blocks=(1, 1),
                         into=view.reshape(D_FF, D_MODEL), after=[tok])
    rest = _sibling_plan(1, chips=slice(2, 4))
    ssem, rsem, bufs, tok = _split_start([gw_down.reshape(view.shape), land], 1, rest,
                                         name="ffn1_down_second_sibling_start")
    bufs = _split_wait(sib_first[0], sib_first[1], bufs, tok, sib_first[3], name="ffn1_down_first_sibling_wait")
    bufs = _split_wait(ssem, rsem, bufs, tok, rest, name="ffn1_down_second_sibling_wait")
    p_down = _add_sibling(bufs[:1], bufs[1:], place, name="add_sibling_ffn1_down")
    last_row = (jnp.pad(d_sink, ((0, 0), (0, D_MODEL - 128)))
                + jnp.pad(loss_local, ((0, 0), (LOSS_LANE, D_MODEL - LOSS_LANE - 1))))
    small = jnp.concatenate([
        d_norm1, d_normm, d_norm2, d_final, jnp.pad(d_conv[0:3], ((0, 0), (0, D_MODEL - CONV_W))), last_row], axis=0)
    (small_all,) = _place_own([lax.empty((N_DEV * 8, D_MODEL), F32)], [small], my_index.astype(jnp.int32).reshape(1),
                              name="place_own_small")
    chips_down, tok = to_chips_start(p_down, "ffn1_down", small_all)
    tok = adam_big(names2, r2, "ffn2", after=[tok])
    tok = adam_big(namesm, rm, "mixer", after=[tok])
    r_gate, _ = to_chips_finish(chips_gate, p_gate, tok, ["ffn1_w_gate"])
    tok = adam_big(["ffn1_w_gate"], r_gate, "ffn1_gate")
    r_up, _ = to_chips_finish(chips_up, p_up, tok, ["ffn1_w_up"])
    tok = adam_big(["ffn1_w_up"], r_up, "ffn1_up")
    r_down, (small_all,) = to_chips_finish(chips_down, p_down, tok, ["ffn1_w_down"])
    adam_big(["ffn1_w_down"], r_down, "ffn1_down")
    results, loss = _update_small(given, moments_m, moments_v, small_all.reshape(N_DEV, 8, D_MODEL), my_index,
                                  name="update_small")
    for nm, outs in results.items():
        grad[nm], delta[nm], new_m[nm], new_v[nm] = outs

    order = list(given)
    return (loss.reshape(()), dx0[None], *[grad[n] for n in order], *[delta[n] for n in order],
            *[new_m[n] for n in order], *[new_v[n] for n in order])
```
